```python
import jax, jax.numpy as jnp
from jax import lax
import numpy as np

D_MODEL = 1024
BATCH = 8
SEQ = 2048
DEPTH = 2
DEC_BATCH = 128
DEC_SEQ = 4
PAST_LEN = 16384
PAGE_SIZE = 128

D_MIX = D_MODEL
C_CONV = D_MIX // 2
CONV_GROUPS = 8
CONV_WIDTH = 31
CONV_STATE = CONV_WIDTH - 1
C_SGU = D_MIX - C_CONV
SGU_HEADS = 4
SGU_HEAD_DIM = C_SGU // SGU_HEADS
CHUNK = 128
N_EXPERTS = 32
TOP_K = 4
D_FF = D_MODEL
SWIGLU_LIMIT = 7.0
SWIGLU_ALPHA = 1.702
N_MOD = 6
EPS = 1e-5

kernel_name = "hymba_conformer_gmlp_moe_adaln_step"


def rms_norm(x):
    xf = x.astype(jnp.float32)
    return (xf * lax.rsqrt(jnp.mean(xf * xf, axis=-1, keepdims=True) + EPS)).astype(x.dtype)


def layer_norm(x, g, b):
    xf = x.astype(jnp.float32)
    mu = jnp.mean(xf, axis=-1, keepdims=True)
    var = jnp.mean(jnp.square(xf - mu), axis=-1, keepdims=True)
    return ((xf - mu) * lax.rsqrt(var + EPS)).astype(x.dtype) * g + b


def group_norm_tokenwise(x, g, b):
    xf = x.astype(jnp.float32).reshape(x.shape[:-1] + (CONV_GROUPS, C_CONV // CONV_GROUPS))
    mu = jnp.mean(xf, axis=-1, keepdims=True)
    var = jnp.mean(jnp.square(xf - mu), axis=-1, keepdims=True)
    y = ((xf - mu) * lax.rsqrt(var + EPS)).reshape(x.shape).astype(x.dtype)
    return y * g + b


def adaln_params(c, w_ada, b_ada):
    mod = jax.nn.silu(c) @ w_ada + b_ada
    return jnp.split(mod[:, None, :], N_MOD, axis=-1)


def depthwise_causal_conv(a_ext, w_conv, b_conv):
    out = lax.conv_general_dilated(
        a_ext, w_conv[:, None, :], window_strides=(1,), padding="VALID",
        dimension_numbers=("NWC", "WIO", "NWC"), feature_group_count=C_CONV)
    return out + b_conv


def spatial_gate(v_chunks, w_s, b_s):
    L = v_chunks.shape[2]
    mask = jnp.tril(jnp.ones((L, L), dtype=w_s.dtype))
    ws = w_s[:, :L, :L] * mask
    s = jnp.einsum("hij,bnjhd->bnihd", ws, v_chunks)
    return s + b_s[:, :L].T[None, None, :, :, None]


def mixer(h, conv_past, w_in, w_conv, b_conv, gn_g, gn_b, sgu_ln_g, sgu_ln_b, w_s, b_s,
          beta_a, beta_b, w_out):
    n, L, _ = h.shape
    z = h @ w_in
    a_val, a_gate, u, v = jnp.split(z, [C_CONV, 2 * C_CONV, 2 * C_CONV + C_SGU], axis=-1)
    a = a_val * jax.nn.sigmoid(a_gate)
    a_ext = jnp.concatenate([conv_past, a], axis=1)
    ya = jax.nn.silu(group_norm_tokenwise(depthwise_causal_conv(a_ext, w_conv, b_conv), gn_g, gn_b))
    u = jax.nn.gelu(u, approximate=False)
    v = layer_norm(jax.nn.gelu(v, approximate=False), sgu_ln_g, sgu_ln_b)
    chunk = min(L, CHUNK)
    vh = v.reshape(n, L // chunk, chunk, SGU_HEADS, SGU_HEAD_DIM)
    yb = u * spatial_gate(vh, w_s, b_s).reshape(n, L, C_SGU)
    y = jnp.concatenate([rms_norm(ya) * beta_a, rms_norm(yb) * beta_b], axis=-1) @ w_out
    return y, a_ext[:, -CONV_STATE:], v


def moe(h, w_router, b_router, w1, b1, w2, b2):
    logits = (h @ w_router + b_router).astype(jnp.float32)
    top_vals, top_idx = lax.top_k(logits, TOP_K)
    top_w = jax.nn.softmax(top_vals, axis=-1)
    comb = jnp.sum(jax.nn.one_hot(top_idx, N_EXPERTS, dtype=jnp.float32) * top_w[..., None],
                   axis=-2).astype(h.dtype)

    def expert(acc, xs):
        w1e, b1e, w2e, b2e, ce = xs
        gu = h @ w1e + b1e
        g, up = jnp.split(gu, 2, axis=-1)
        g = jnp.minimum(g, SWIGLU_LIMIT)
        up = jnp.clip(up, -SWIGLU_LIMIT, SWIGLU_LIMIT)
        act = g * jax.nn.sigmoid(SWIGLU_ALPHA * g) * (up + 1)
        return acc + ce[:, None] * (act @ w2e + b2e), None

    acc, _ = lax.scan(expert, jnp.zeros_like(h), (w1, b1, w2, b2, comb.T))
    return acc


def setup_inputs(seed: int = 0) -> dict:
    key = jax.random.key(seed)
    ks = jax.random.split(key, 32)
    f = jnp.float32
    nrm = lambda k, s, sc: jax.random.normal(k, s, f) * sc
    return {
        "x_prompt": nrm(ks[0], (BATCH, SEQ, D_MODEL), 1.0),
        "x_sample": nrm(ks[1], (DEC_BATCH, DEC_SEQ, D_MODEL), 1.0),
        "state_conv": nrm(ks[2], (DEPTH, DEC_BATCH, CONV_STATE, C_CONV), 0.5),
        "c_prompt": nrm(ks[3], (BATCH, D_MODEL), 1.0),
        "c_sample": nrm(ks[4], (DEC_BATCH, D_MODEL), 1.0),
        "w_ada": nrm(ks[5], (DEPTH, D_MODEL, N_MOD * D_MODEL), 0.5 * D_MODEL ** -0.5),
        "b_ada": nrm(ks[6], (DEPTH, N_MOD * D_MODEL), 0.02),
        "w_in": nrm(ks[7], (DEPTH, D_MODEL, 2 * C_CONV + 2 * C_SGU), D_MODEL ** -0.5),
        "w_conv": nrm(ks[8], (DEPTH, CONV_WIDTH, C_CONV), CONV_WIDTH ** -0.5),
        "b_conv": nrm(ks[9], (DEPTH, C_CONV), 0.02),
        "gn_g": 1.0 + nrm(ks[10], (DEPTH, C_CONV), 0.02),
        "gn_b": nrm(ks[11], (DEPTH, C_CONV), 0.02),
        "sgu_ln_g": 1.0 + nrm(ks[12], (DEPTH, C_SGU), 0.02),
        "sgu_ln_b": nrm(ks[13], (DEPTH, C_SGU), 0.02),
        "w_s": nrm(ks[14], (DEPTH, SGU_HEADS, CHUNK, CHUNK), CHUNK ** -0.5),
        "b_s": 1.0 + nrm(ks[15], (DEPTH, SGU_HEADS, CHUNK), 0.01),
        "beta_a": 1.0 + nrm(ks[16], (DEPTH, C_CONV), 0.02),
        "beta_b": 1.0 + nrm(ks[17], (DEPTH, C_SGU), 0.02),
        "w_out": nrm(ks[18], (DEPTH, D_MIX, D_MODEL), D_MIX ** -0.5),
        "w_router": nrm(ks[19], (DEPTH, D_MODEL, N_EXPERTS), D_MODEL ** -0.5),
        "b_router": nrm(ks[20], (DEPTH, N_EXPERTS), 0.01),
        "w1": nrm(ks[21], (DEPTH, N_EXPERTS, D_MODEL, 2 * D_FF), D_MODEL ** -0.5),
        "b1": nrm(ks[22], (DEPTH, N_EXPERTS, 2 * D_FF), 0.02),
        "w2": nrm(ks[23], (DEPTH, N_EXPERTS, D_FF, D_MODEL), D_FF ** -0.5),
        "b2": nrm(ks[24], (DEPTH, N_EXPERTS, D_MODEL), 0.02),
        "final_g": 1.0 + nrm(ks[25], (D_MODEL,), 0.02),
    }


def reference(x_prompt, x_sample, state_conv, c_prompt, c_sample, w_ada, b_ada, w_in, w_conv,
              b_conv, gn_g, gn_b, sgu_ln_g, sgu_ln_b, w_s, b_s, beta_a, beta_b, w_out,
              w_router, b_router, w1, b1, w2, b2, final_g):
    xp, xs = x_prompt, x_sample
    n_tok_p = xp.shape[0] * xp.shape[1]
    conv_p_list, conv_s_list, v_s_list = [], [], []
    for l in range(DEPTH):
        mp = adaln_params(c_prompt, w_ada[l], b_ada[l])
        ms = adaln_params(c_sample, w_ada[l], b_ada[l])
        mix_args = (w_in[l], w_conv[l], b_conv[l], gn_g[l], gn_b[l], sgu_ln_g[l], sgu_ln_b[l],
                    w_s[l], b_s[l], beta_a[l], beta_b[l], w_out[l])
        hp = rms_norm(xp) * (1 + mp[1]) + mp[0]
        hs = rms_norm(xs) * (1 + ms[1]) + ms[0]
        zero_past = jnp.zeros((xp.shape[0], CONV_STATE, C_CONV), dtype=xp.dtype)
        yp, conv_p, _ = mixer(hp, zero_past, *mix_args)
        ys, conv_s, v_s = mixer(hs, state_conv[l], *mix_args)
        xp = xp + mp[2] * yp
        xs = xs + ms[2] * ys
        conv_p_list.append(conv_p)
        conv_s_list.append(conv_s)
        v_s_list.append(v_s)
        hp2 = rms_norm(xp) * (1 + mp[4]) + mp[3]
        hs2 = rms_norm(xs) * (1 + ms[4]) + ms[3]
        h_all = jnp.concatenate([hp2.reshape(-1, D_MODEL), hs2.reshape(-1, D_MODEL)], axis=0)
        f_all = moe(h_all, w_router[l], b_router[l], w1[l], b1[l], w2[l], b2[l])
        xp = xp + mp[5] * f_all[:n_tok_p].reshape(xp.shape)
        xs = xs + ms[5] * f_all[n_tok_p:].reshape(xs.shape)
    y_prompt = rms_norm(xp) * final_g
    y_sample = rms_norm(xs) * final_g
    new_conv_prompt = jnp.stack(conv_p_list, axis=0)
    new_conv_sample = jnp.stack(conv_s_list, axis=0)
    new_sgu_v_sample = jnp.stack(v_s_list, axis=0)
    return (y_prompt, y_sample, new_conv_prompt, new_conv_sample, new_sgu_v_sample)
```

```python
import functools
import math

import jax
import jax.numpy as jnp
from jax import lax
from jax.experimental import pallas as pl
from jax.experimental.pallas import tpu as pltpu

F32 = jnp.float32
BF16 = jnp.bfloat16

D_MODEL = 1024
C_CONV = 512
C_SGU = 512
CONV_GROUPS = 8
CONV_WIDTH = 31
CONV_STATE = CONV_WIDTH - 1
SGU_HEADS = 4
SGU_HEAD_DIM = C_SGU // SGU_HEADS
CHUNK = 128
N_EXPERTS = 32
TOP_K = 4
D_FF = 1024
SWIGLU_LIMIT = 7.0
SWIGLU_ALPHA = 1.702
N_MOD = 6
EPS = 1e-5

ROW_TILE = 512
ROW_CHUNK = 32
CARRY_ROWS = 32
SUBLANES = 8
GEMM_TILE = 512
VMEM_LIMIT = 56 * 1024 * 1024


def _rms(x):
    return x * lax.rsqrt(jnp.mean(x * x, axis=-1, keepdims=True) + EPS)


def _gelu(x):
    return 0.5 * x * (1.0 + lax.erf(x * (1.0 / math.sqrt(2.0))))


def _sigmoid(x):
    return 1.0 / (1.0 + jnp.exp(-x))


def _split_bf16(x):
    hi = x.astype(BF16)
    lo = (x - hi.astype(F32)).astype(BF16)
    return hi, lo


def _dot(a, b):
    return jnp.dot(a, b, preferred_element_type=F32)


def _row_loop(n_rows, body):
    def step(i, carry):
        body(pl.multiple_of(i * ROW_CHUNK, ROW_CHUNK))
        return carry
    lax.fori_loop(0, n_rows // ROW_CHUNK, step, 0)


def _adaln_kernel(c_ref, w_ref, b_ref, o_ref):
    c = c_ref[...]
    s_hi, s_lo = _split_bf16(c * _sigmoid(c))
    w_hi, w_lo = _split_bf16(w_ref[0])
    acc = _dot(s_hi, w_hi) + _dot(s_hi, w_lo) + _dot(s_lo, w_hi)
    o_ref[0] = acc + b_ref[0]


def _adaln(c_all, w_ada, b_ada):
    depth, _, n_out = w_ada.shape
    n_rows = c_all.shape[0]
    tn = 1024
    return pl.pallas_call(
        _adaln_kernel,
        grid=(depth, n_out // tn),
        in_specs=[
            pl.BlockSpec((n_rows, D_MODEL), lambda l, j: (0, 0)),
            pl.BlockSpec((1, D_MODEL, tn), lambda l, j: (l, 0, j)),
            pl.BlockSpec((1, 1, tn), lambda l, j: (l, 0, j)),
        ],
        out_specs=pl.BlockSpec((1, n_rows, tn), lambda l, j: (l, 0, j)),
        out_shape=jax.ShapeDtypeStruct((depth, n_rows, n_out), F32),
        compiler_params=pltpu.CompilerParams(
            dimension_semantics=("arbitrary", "arbitrary"), vmem_limit_bytes=VMEM_LIMIT),
        name="adaln",
    )(c_all, w_ada, b_ada.reshape(depth, 1, n_out))


def _phase_in_norm(x_ref, h_scr, mod, n_rows):
    def body(r0):
        r = pl.ds(r0, ROW_CHUNK)
        h = _rms(x_ref[r, :]) * (1.0 + mod(1, r0)) + mod(0, r0)
        h_scr[r, :] = h.astype(BF16)
    _row_loop(n_rows, body)


def _phase_gates(z_scr, vec_ref, a_dst, a_off, u_scr, v_scr, vout_ref, n_rows):
    ln_g = vec_ref[3:4, :]
    ln_b = vec_ref[4:5, :]

    def body(r0):
        r = pl.ds(r0, ROW_CHUNK)
        a = z_scr[r, 0:C_CONV] * _sigmoid(z_scr[r, C_CONV:2 * C_CONV])
        a_dst[pl.ds(r0 + a_off, ROW_CHUNK), :] = a
        u_scr[r, :] = _gelu(z_scr[r, 2 * C_CONV:2 * C_CONV + C_SGU])
        gv = _gelu(z_scr[r, 2 * C_CONV + C_SGU:2 * C_CONV + 2 * C_SGU])
        mu = jnp.mean(gv, axis=-1, keepdims=True)
        dv = gv - mu
        var = jnp.mean(dv * dv, axis=-1, keepdims=True)
        v = dv * lax.rsqrt(var + EPS) * ln_g + ln_b
        if vout_ref is not None:
            vout_ref[r, :] = v
        v_scr[r, :] = v.astype(v_scr.dtype)
    _row_loop(n_rows, body)


def _phase_group_norm(conv_scr, convb_scr, stat_scr, g_ref, vec_ref, y_scr, n_rows):
    stat_scr[...] = _dot(convb_scr[...], g_ref[...])

    def center(r0):
        r = pl.ds(r0, ROW_CHUNK)
        d = conv_scr[r, :] - stat_scr[r, :]
        conv_scr[r, :] = d
        convb_scr[r, :] = (d * d).astype(BF16)
    _row_loop(n_rows, center)
    stat_scr[...] = _dot(convb_scr[...], g_ref[...])
    gn_g = vec_ref[1:2, :]
    gn_b = vec_ref[2:3, :]
    beta_a = vec_ref[5:6, :]

    def finish(r0):
        r = pl.ds(r0, ROW_CHUNK)
        gn = conv_scr[r, :] * lax.rsqrt(stat_scr[r, :] + EPS) * gn_g + gn_b
        ya = gn * _sigmoid(gn)
        y_scr[r, 0:C_CONV] = (_rms(ya) * beta_a).astype(BF16)
    _row_loop(n_rows, finish)


def _phase_out(x_ref, y_scr, wout_ref, z_scr, wr_ref, br_ref, mod, h_scr,
               xmid_ref, h2_ref, logit_ref, n_rows):
    z_scr[:, 0:D_MODEL] = _dot(y_scr[...], wout_ref[...])

    def body(r0):
        r = pl.ds(r0, ROW_CHUNK)
        xm = x_ref[r, :] + mod(2, r0) * z_scr[r, 0:D_MODEL]
        xmid_ref[r, :] = xm
        h2 = _rms(xm) * (1.0 + mod(4, r0)) + mod(3, r0)
        hi, lo = _split_bf16(h2)
        h2_ref[r, :] = hi
        h_scr[r, :] = lo
    _row_loop(n_rows, body)
    w_hi, w_lo = _split_bf16(wr_ref[...])
    hi = h2_ref[...]
    logit_ref[...] = (_dot(hi, w_hi) + _dot(hi, w_lo) + _dot(h_scr[...], w_hi)) + br_ref[...]


def _mixer_prompt_kernel(x_ref, mod_ref, win_ref, wout_ref, wconv_ref, vec_ref, g_ref,
                         ws_ref, bs_ref, wr_ref, br_ref,
                         xmid_ref, h2_ref, logit_ref, cst_ref,
                         h_scr, z_scr, aext_scr, conv_scr, convb_scr, stat_scr,
                         u_scr, v_scr, y_scr):
    j = pl.program_id(1)
    n_rows = ROW_TILE

    def mod(idx, r0):
        del r0
        return mod_ref[0, :, idx * D_MODEL:(idx + 1) * D_MODEL]

    a_buf = aext_scr.at[0]

    @pl.when(j == 0)
    def _():
        a_buf[0:CARRY_ROWS, :] = jnp.zeros((CARRY_ROWS, C_CONV), F32)

    _phase_in_norm(x_ref, h_scr, mod, n_rows)
    z_scr[...] = _dot(h_scr[...], win_ref[...])
    _phase_gates(z_scr, vec_ref, a_buf, CARRY_ROWS, u_scr, v_scr, None, n_rows)
    n_shift = n_rows + CARRY_ROWS - SUBLANES
    for s in range(1, SUBLANES):
        aext_scr[s, 0:n_shift, :] = a_buf[s:s + n_shift, :]

    b_conv = vec_ref[0:1, :]
    lead = CARRY_ROWS - CONV_STATE

    def conv_body(r0):
        acc = jnp.zeros((ROW_CHUNK, C_CONV), F32)
        for k in range(CONV_WIDTH):
            s = (lead + k) % SUBLANES
            acc = acc + wconv_ref[k:k + 1, :] * aext_scr[s, pl.ds(r0 + (lead + k - s), ROW_CHUNK), :]
        conv = acc + b_conv
        r = pl.ds(r0, ROW_CHUNK)
        conv_scr[r, :] = conv
        convb_scr[r, :] = conv.astype(BF16)
    _row_loop(n_rows, conv_body)

    @pl.when(j == pl.num_programs(1) - 1)
    def _():
        cst_ref[0] = a_buf[n_rows + lead:n_rows + CARRY_ROWS, :]

    a_buf[0:CARRY_ROWS, :] = a_buf[n_rows:n_rows + CARRY_ROWS, :]

    _phase_group_norm(conv_scr, convb_scr, stat_scr, g_ref, vec_ref, y_scr, n_rows)

    row_i = lax.broadcasted_iota(jnp.int32, (CHUNK, CHUNK), 0)
    col_i = lax.broadcasted_iota(jnp.int32, (CHUNK, CHUNK), 1)
    tril = (col_i <= row_i).astype(F32)
    for hd in range(SGU_HEADS):
        ws_h = (ws_ref[hd] * tril).astype(BF16)
        cols = slice(hd * SGU_HEAD_DIM, (hd + 1) * SGU_HEAD_DIM)
        for c in range(n_rows // CHUNK):
            rows = slice(c * CHUNK, (c + 1) * CHUNK)
            stat_scr[rows, cols] = _dot(ws_h, v_scr[rows, cols])
    beta_b = vec_ref[6:7, :]

    def gate_body(r0):
        r = pl.ds(r0, ROW_CHUNK)
        rb = pl.ds(pl.multiple_of(lax.rem(r0, CHUNK), ROW_CHUNK), ROW_CHUNK)
        yb = u_scr[r, :] * (stat_scr[r, :] + bs_ref[rb, :])
        y_scr[r, C_CONV:C_CONV + C_SGU] = (_rms(yb) * beta_b).astype(BF16)
    _row_loop(n_rows, gate_body)

    _phase_out(x_ref, y_scr, wout_ref, z_scr, wr_ref, br_ref, mod, h_scr,
               xmid_ref, h2_ref, logit_ref, n_rows)


def _mixer_sample_kernel(x_ref, mod_ref, win_ref, wout_ref, wconv_ref, vec_ref, g_ref,
                         wsv_ref, bsv_ref, wr_ref, br_ref, state_ref,
                         xmid_in, h2_in, logit_in,
                         xmid_ref, h2_ref, logit_ref, anew_ref, vout_ref,
                         h_scr, z_scr, conv_scr, convb_scr, stat_scr, u_scr, v_scr, y_scr):
    del xmid_in, h2_in, logit_in
    n_rows = x_ref.shape[0]
    n_seq = state_ref.shape[1]
    n_pos = n_rows // n_seq

    def mod(idx, r0):
        rs = pl.ds(pl.multiple_of(lax.rem(r0, n_seq), ROW_CHUNK), ROW_CHUNK)
        return mod_ref[rs, idx * D_MODEL:(idx + 1) * D_MODEL]

    _phase_in_norm(x_ref, h_scr, mod, n_rows)
    z_scr[...] = _dot(h_scr[...], win_ref[...])
    _phase_gates(z_scr, vec_ref, anew_ref, 0, u_scr, v_scr, vout_ref, n_rows)

    b_conv = vec_ref[0:1, :]

    def conv_body(s0):
        rs = pl.ds(s0, ROW_CHUNK)
        for t in range(n_pos):
            acc = jnp.zeros((ROW_CHUNK, C_CONV), F32)
            for k in range(CONV_WIDTH):
                p = t + k
                if p < CONV_STATE:
                    src = state_ref[p, rs, :]
                else:
                    src = anew_ref[pl.ds(s0 + (p - CONV_STATE) * n_seq, ROW_CHUNK), :]
                acc = acc + wconv_ref[k:k + 1, :] * src
            conv = acc + b_conv
            r = pl.ds(s0 + t * n_seq, ROW_CHUNK)
            conv_scr[r, :] = conv
            convb_scr[r, :] = conv.astype(BF16)
    _row_loop(n_seq, conv_body)

    _phase_group_norm(conv_scr, convb_scr, stat_scr, g_ref, vec_ref, y_scr, n_rows)

    beta_b = vec_ref[6:7, :]

    def gate_body(s0):
        for t in range(n_pos):
            s = jnp.zeros((ROW_CHUNK, C_SGU), F32)
            for jj in range(t + 1):
                s = s + wsv_ref[t * n_pos + jj:t * n_pos + jj + 1, :] * v_scr[pl.ds(s0 + jj * n_seq, ROW_CHUNK), :]
            r = pl.ds(s0 + t * n_seq, ROW_CHUNK)
            yb = u_scr[r, :] * (s + bsv_ref[t:t + 1, :])
            y_scr[r, C_CONV:C_CONV + C_SGU] = (_rms(yb) * beta_b).astype(BF16)
    _row_loop(n_seq, gate_body)

    _phase_out(x_ref, y_scr, wout_ref, z_scr, wr_ref, br_ref, mod, h_scr,
               xmid_ref, h2_ref, logit_ref, n_rows)


def _const_spec(shape):
    nd = len(shape)
    return pl.BlockSpec(shape, lambda *_: (0,) * nd)


def _mixer(x_all, mod_p, mod_s, w_in_b, w_out_b, wconv_p, vec, g_mat, w_s, bs_full, wsv, bsv,
           w_router, b_router, state_t, n_prompt_seq, seq_len):
    t_all = x_all.shape[0]
    n_prompt = n_prompt_seq * seq_len
    n_sample = t_all - n_prompt
    nj = seq_len // ROW_TILE
    weight_specs = [
        _const_spec((D_MODEL, 2 * C_CONV + 2 * C_SGU)),
        _const_spec((D_MODEL, D_MODEL)),
        _const_spec((CARRY_ROWS, C_CONV)),
        _const_spec((8, C_CONV)),
        _const_spec((C_CONV, C_CONV)),
    ]
    router_specs = [_const_spec((D_MODEL, N_EXPERTS)), _const_spec((1, N_EXPERTS))]
    common_scratch = lambda n: [
        pltpu.VMEM((n, D_MODEL), BF16),
        pltpu.VMEM((n, 2 * C_CONV + 2 * C_SGU), F32),
    ]
    tail_scratch = lambda n, vdt: [
        pltpu.VMEM((n, C_CONV), F32),
        pltpu.VMEM((n, C_CONV), BF16),
        pltpu.VMEM((n, C_CONV), F32),
        pltpu.VMEM((n, C_SGU), F32),
        pltpu.VMEM((n, C_SGU), vdt),
        pltpu.VMEM((n, D_MODEL), BF16),
    ]
    out_shapes = (
        jax.ShapeDtypeStruct((t_all, D_MODEL), F32),
        jax.ShapeDtypeStruct((t_all, D_MODEL), BF16),
        jax.ShapeDtypeStruct((t_all, N_EXPERTS), F32),
    )
    row_map = lambda b, j: (b * nj + j, 0)
    xmid, h2, logits, cst = pl.pallas_call(
        _mixer_prompt_kernel,
        grid=(n_prompt_seq, nj),
        in_specs=[
            pl.BlockSpec((ROW_TILE, D_MODEL), row_map),
            pl.BlockSpec((1, 1, N_MOD * D_MODEL), lambda b, j: (b, 0, 0)),
            *weight_specs,
            _const_spec((SGU_HEADS, CHUNK, CHUNK)),
            _const_spec((CHUNK, C_SGU)),
            *router_specs,
        ],
        out_specs=(
            pl.BlockSpec((ROW_TILE, D_MODEL), row_map),
            pl.BlockSpec((ROW_TILE, D_MODEL), row_map),
            pl.BlockSpec((ROW_TILE, N_EXPERTS), row_map),
            pl.BlockSpec((1, CONV_STATE, C_CONV), lambda b, j: (b, 0, 0)),
        ),
        out_shape=out_shapes + (jax.ShapeDtypeStruct((n_prompt_seq, CONV_STATE, C_CONV), F32),),
        scratch_shapes=common_scratch(ROW_TILE)
        + [pltpu.VMEM((SUBLANES, CARRY_ROWS + ROW_TILE, C_CONV), F32)]
        + tail_scratch(ROW_TILE, BF16),
        compiler_params=pltpu.CompilerParams(
            dimension_semantics=("arbitrary", "arbitrary"), vmem_limit_bytes=VMEM_LIMIT),
        name="mixer_prompt",
    )(x_all, mod_p, w_in_b, w_out_b, wconv_p, vec, g_mat, w_s, bs_full, w_router, b_router)

    blk = n_prompt // n_sample
    srow_map = lambda i: (blk, 0)
    any_spec = pl.BlockSpec(memory_space=pl.ANY)
    n_seq = state_t.shape[1]
    n_pos = n_sample // n_seq
    xmid, h2, logits, a_new, v_new = pl.pallas_call(
        _mixer_sample_kernel,
        grid=(1,),
        in_specs=[
            pl.BlockSpec((n_sample, D_MODEL), srow_map),
            _const_spec((n_seq, N_MOD * D_MODEL)),
            *weight_specs,
            _const_spec((n_pos * n_pos, C_SGU)),
            _const_spec((n_pos, C_SGU)),
            *router_specs,
            _const_spec((CONV_STATE, n_seq, C_CONV)),
            any_spec, any_spec, any_spec,
        ],
        out_specs=(
            pl.BlockSpec((n_sample, D_MODEL), srow_map),
            pl.BlockSpec((n_sample, D_MODEL), srow_map),
            pl.BlockSpec((n_sample, N_EXPERTS), srow_map),
            _const_spec((n_sample, C_CONV)),
            _const_spec((n_sample, C_SGU)),
        ),
        out_shape=out_shapes + (
            jax.ShapeDtypeStruct((n_sample, C_CONV), F32),
            jax.ShapeDtypeStruct((n_sample, C_SGU), F32),
        ),
        input_output_aliases={12: 0, 13: 1, 14: 2},
        scratch_shapes=common_scratch(n_sample) + tail_scratch(n_sample, F32),
        compiler_params=pltpu.CompilerParams(
            dimension_semantics=("arbitrary",), vmem_limit_bytes=VMEM_LIMIT),
        name="mixer_sample",
    )(x_all, mod_s, w_in_b, w_out_b, wconv_p, vec, g_mat, wsv, bsv, w_router, b_router,
      state_t, xmid, h2, logits)
    return xmid, h2, logits, cst, a_new, v_new


def _router_kernel(logit_ref, eid_ref, wgt_ref, rank_ref, cnt_ref, cnt_scr):
    i = pl.program_id(0)
    n = logit_ref.shape[0]

    @pl.when(i == 0)
    def _():
        cnt_scr[...] = jnp.zeros_like(cnt_scr)

    lg = logit_ref[...]
    lane = lax.broadcasted_iota(jnp.int32, lg.shape, 1).astype(F32)
    vals, sels = [], []
    for k in range(TOP_K):
        m = jnp.max(lg, axis=-1, keepdims=True)
        idx = jnp.min(jnp.where(lg == m, lane, float(N_EXPERTS)), axis=-1, keepdims=True)
        sel = lane == idx
        eid_ref[:, k:k + 1] = idx.astype(jnp.int32)
        vals.append(m)
        sels.append(sel)
        lg = jnp.where(sel, -jnp.inf, lg)
    exps = [jnp.exp(v - vals[0]) for v in vals]
    inv = 1.0 / (exps[0] + exps[1] + exps[2] + exps[3])
    for k in range(TOP_K):
        wgt_ref[:, k:k + 1] = exps[k] * inv

    onehot = jnp.zeros(lg.shape, F32)
    for sel in sels:
        onehot = onehot + sel.astype(F32)
    row_i = lax.broadcasted_iota(jnp.int32, (n, n), 0)
    col_i = lax.broadcasted_iota(jnp.int32, (n, n), 1)
    before = jnp.where(col_i < row_i, 1.0, 0.0).astype(BF16)
    prefix = _dot(before, onehot.astype(BF16)) + cnt_scr[...]
    for k in range(TOP_K):
        rk = jnp.sum(jnp.where(sels[k], prefix, 0.0), axis=-1, keepdims=True)
        rank_ref[:, k:k + 1] = rk.astype(jnp.int32)
    total = cnt_scr[...] + jnp.sum(onehot, axis=0, keepdims=True)
    cnt_scr[...] = total
    cnt_ref[...] = total.astype(jnp.int32)


def _router(logits):
    t_all = logits.shape[0]
    row_spec = lambda w: pl.BlockSpec((ROW_TILE, w), lambda i: (i, 0))
    return pl.pallas_call(
        _router_kernel,
        grid=(t_all // ROW_TILE,),
        in_specs=[row_spec(N_EXPERTS)],
        out_specs=(row_spec(TOP_K), row_spec(TOP_K), row_spec(TOP_K), _const_spec((1, N_EXPERTS))),
        out_shape=(
            jax.ShapeDtypeStruct((t_all, TOP_K), jnp.int32),
            jax.ShapeDtypeStruct((t_all, TOP_K), F32),
            jax.ShapeDtypeStruct((t_all, TOP_K), jnp.int32),
            jax.ShapeDtypeStruct((1, N_EXPERTS), jnp.int32),
        ),
        scratch_shapes=[pltpu.VMEM((1, N_EXPERTS), F32)],
        compiler_params=pltpu.CompilerParams(
            dimension_semantics=("arbitrary",), vmem_limit_bytes=VMEM_LIMIT),
        name="router",
    )(logits)


def _expert_kernel(te_ref, nv_ref, x_ref, w1_ref, b1_ref, w2_ref, b2_ref, o_ref,
                   w1b_scr, w2b_scr, gu_scr, act_scr):
    g = pl.program_id(0)
    nv = nv_ref[g]
    prev = te_ref[jnp.maximum(g - 1, 0)]
    fresh = jnp.logical_or(g == 0, te_ref[g] != prev)

    @pl.when(jnp.logical_and(fresh, nv > 0))
    def _():
        w1b_scr[...] = w1_ref[0].astype(BF16)
        w2b_scr[...] = w2_ref[0].astype(BF16)

    @pl.when(nv > 0)
    def _():
        gu_scr[...] = _dot(x_ref[...], w1b_scr[...])

        def body(r0):
            r = pl.ds(r0, ROW_CHUNK)
            gg = jnp.minimum(gu_scr[r, 0:D_FF] + b1_ref[0, :, 0:D_FF], SWIGLU_LIMIT)
            up = jnp.clip(gu_scr[r, D_FF:2 * D_FF] + b1_ref[0, :, D_FF:2 * D_FF],
                          -SWIGLU_LIMIT, SWIGLU_LIMIT)
            act = gg * _sigmoid(SWIGLU_ALPHA * gg) * (up + 1.0)
            act_scr[r, :] = act.astype(BF16)
        _row_loop(GEMM_TILE, body)
        y = _dot(act_scr[...], w2b_scr[...]) + b2_ref[0]
        row = lax.broadcasted_iota(jnp.int32, y.shape, 0)
        o_ref[...] = jnp.where(row < nv, y, 0.0).astype(o_ref.dtype)

    @pl.when(nv == 0)
    def _():
        o_ref[...] = jnp.zeros_like(o_ref)


def _experts(xs, tile_e, tile_nv, w1, b1, w2, b2):
    n_tiles = tile_e.shape[0]
    grid_spec = pltpu.PrefetchScalarGridSpec(
        num_scalar_prefetch=2,
        grid=(n_tiles,),
        in_specs=[
            pl.BlockSpec((GEMM_TILE, D_MODEL), lambda g, te, nv: (g, 0)),
            pl.BlockSpec((1, D_MODEL, 2 * D_FF), lambda g, te, nv: (te[g], 0, 0)),
            pl.BlockSpec((1, 1, 2 * D_FF), lambda g, te, nv: (te[g], 0, 0)),
            pl.BlockSpec((1, D_FF, D_MODEL), lambda g, te, nv: (te[g], 0, 0)),
            pl.BlockSpec((1, 1, D_MODEL), lambda g, te, nv: (te[g], 0, 0)),
        ],
        out_specs=pl.BlockSpec((GEMM_TILE, D_MODEL), lambda g, te, nv: (g, 0)),
        scratch_shapes=[
            pltpu.VMEM((D_MODEL, 2 * D_FF), BF16),
            pltpu.VMEM((D_FF, D_MODEL), BF16),
            pltpu.VMEM((GEMM_TILE, 2 * D_FF), F32),
            pltpu.VMEM((GEMM_TILE, D_FF), BF16),
        ],
    )
    return pl.pallas_call(
        _expert_kernel,
        grid_spec=grid_spec,
        out_shape=jax.ShapeDtypeStruct((n_tiles * GEMM_TILE, D_MODEL), BF16),
        compiler_params=pltpu.CompilerParams(
            dimension_semantics=("arbitrary",), vmem_limit_bytes=VMEM_LIMIT),
        name="experts",
    )(tile_e, tile_nv, xs, w1, b1.reshape(N_EXPERTS, 1, 2 * D_FF), w2,
      b2.reshape(N_EXPERTS, 1, D_MODEL))


def _combine_kernel(x_ref, yg_ref, wgt_ref, gp_ref, gs_ref, fg_ref, o_ref, *, n_prompt_blocks, final):
    i = pl.program_id(0)
    n_seq = gs_ref.shape[0]

    def run(gate):
        def body(r0):
            r = pl.ds(r0, ROW_CHUNK)
            w = wgt_ref[r, :]
            moe = jnp.zeros((ROW_CHUNK, D_MODEL), F32)
            for k in range(TOP_K):
                moe = moe + w[:, k:k + 1] * yg_ref[k, r, :].astype(F32)
            xn = x_ref[r, :] + gate(r0) * moe
            if final:
                xn = _rms(xn) * fg_ref[...]
            o_ref[r, :] = xn
        _row_loop(x_ref.shape[0], body)

    @pl.when(i < n_prompt_blocks)
    def _():
        run(lambda r0: gp_ref[0])

    @pl.when(i >= n_prompt_blocks)
    def _():
        run(lambda r0: gs_ref[pl.ds(pl.multiple_of(lax.rem(r0, n_seq), ROW_CHUNK), ROW_CHUNK), :])


def _combine(xmid, yg, wgt, gate_p, gate_s, final_g, n_prompt, seq_len, final):
    t_all = xmid.shape[0]
    n_prompt_blocks = n_prompt // ROW_TILE
    per_seq = seq_len // ROW_TILE
    n_seq_p = gate_p.shape[0]
    kern = functools.partial(_combine_kernel, n_prompt_blocks=n_prompt_blocks, final=final)
    return pl.pallas_call(
        kern,
        grid=(t_all // ROW_TILE,),
        in_specs=[
            pl.BlockSpec((ROW_TILE, D_MODEL), lambda i: (i, 0)),
            pl.BlockSpec((TOP_K, ROW_TILE, D_MODEL), lambda i: (0, i, 0)),
            pl.BlockSpec((ROW_TILE, TOP_K), lambda i: (i, 0)),
            pl.BlockSpec((1, 1, D_MODEL), lambda i: (jnp.minimum(i // per_seq, n_seq_p - 1), 0, 0)),
            _const_spec(gate_s.shape),
            _const_spec((1, D_MODEL)),
        ],
        out_specs=pl.BlockSpec((ROW_TILE, D_MODEL), lambda i: (i, 0)),
        out_shape=jax.ShapeDtypeStruct((t_all, D_MODEL), F32),
        compiler_params=pltpu.CompilerParams(
            dimension_semantics=("arbitrary",), vmem_limit_bytes=VMEM_LIMIT),
        name="combine",
    )(xmid, yg, wgt, gate_p.reshape(n_seq_p, 1, D_MODEL), gate_s, final_g.reshape(1, D_MODEL))


def _dispatch_plan(eid, rank, counts, n_tiles):
    counts = counts.reshape(N_EXPERTS)
    tiles_per = (counts + GEMM_TILE - 1) // GEMM_TILE
    tile_end = jnp.cumsum(tiles_per)
    tile_start = tile_end - tiles_per
    pos = (tile_start * GEMM_TILE)[eid] + rank
    g = jnp.arange(n_tiles, dtype=jnp.int32)
    te = jnp.searchsorted(tile_end, g, side="right").astype(jnp.int32)
    active = te < N_EXPERTS
    te_c = jnp.minimum(te, N_EXPERTS - 1)
    nv = jnp.clip(counts[te_c] - (g - tile_start[te_c]) * GEMM_TILE, 0, GEMM_TILE)
    nv = jnp.where(active, nv, 0).astype(jnp.int32)
    last_e = te_c[jnp.maximum(jnp.sum(active.astype(jnp.int32)) - 1, 0)]
    te_c = jnp.where(active, te_c, last_e).astype(jnp.int32)
    return pos.astype(jnp.int32), te_c, nv


def kernel(x_prompt, x_sample, state_conv, c_prompt, c_sample, w_ada, b_ada, w_in, w_conv, b_conv,
           gn_g, gn_b, sgu_ln_g, sgu_ln_b, w_s, b_s, beta_a, beta_b, w_out, w_router, b_router,
           w1, b1, w2, b2, final_g):
    n_bp, seq_len, _ = x_prompt.shape
    n_bs, n_pos, _ = x_sample.shape
    depth = w_ada.shape[0]
    n_prompt = n_bp * seq_len
    n_sample = n_bs * n_pos
    t_all = n_prompt + n_sample
    n_tiles = (t_all * TOP_K) // GEMM_TILE + N_EXPERTS

    x_all = jnp.concatenate(
        [x_prompt.reshape(n_prompt, D_MODEL),
         x_sample.transpose(1, 0, 2).reshape(n_sample, D_MODEL)], axis=0)
    mods = _adaln(jnp.concatenate([c_prompt, c_sample], axis=0), w_ada, b_ada)
    state_t = state_conv.transpose(0, 2, 1, 3)
    grp = jnp.arange(C_CONV) // (C_CONV // CONV_GROUPS)
    g_mat = ((grp[:, None] == grp[None, :]).astype(F32) / (C_CONV // CONV_GROUPS)).astype(BF16)

    conv_p, conv_s, v_s = [], [], []
    for l in range(depth):
        mod_p = mods[l, :n_bp].reshape(n_bp, 1, N_MOD * D_MODEL)
        mod_s = mods[l, n_bp:]
        wconv_p = jnp.pad(w_conv[l], ((0, CARRY_ROWS - CONV_WIDTH), (0, 0)))
        zero = jnp.zeros_like(b_conv[l])
        vec = jnp.stack([b_conv[l], gn_g[l], gn_b[l], sgu_ln_g[l], sgu_ln_b[l],
                         beta_a[l], beta_b[l], zero], axis=0)
        bs_full = jnp.repeat(b_s[l].T, SGU_HEAD_DIM, axis=1)
        wsv = jnp.repeat(w_s[l][:, :n_pos, :n_pos].transpose(1, 2, 0).reshape(n_pos * n_pos, SGU_HEADS),
                         SGU_HEAD_DIM, axis=1)
        bsv = jnp.repeat(b_s[l][:, :n_pos].T, SGU_HEAD_DIM, axis=1)
        xmid, h2, logits, cst, a_new, v_new = _mixer(
            x_all, mod_p, mod_s, w_in[l].astype(BF16), w_out[l].astype(BF16), wconv_p, vec, g_mat,
            w_s[l], bs_full, wsv, bsv, w_router[l], b_router[l].reshape(1, N_EXPERTS),
            state_t[l], n_bp, seq_len)
        conv_p.append(cst)
        a_new = a_new.reshape(n_pos, n_bs, C_CONV).transpose(1, 0, 2)
        conv_s.append(jnp.concatenate([state_conv[l], a_new], axis=1)[:, -CONV_STATE:])
        v_s.append(v_new.reshape(n_pos, n_bs, C_SGU).transpose(1, 0, 2))

        eid, wgt, rank, counts = _router(logits)
        pos, tile_e, tile_nv = _dispatch_plan(eid, rank, counts, n_tiles)
        tok = jnp.zeros((n_tiles * GEMM_TILE,), jnp.int32).at[pos.reshape(-1)].set(
            jnp.arange(t_all * TOP_K, dtype=jnp.int32) // TOP_K)
        xs = jnp.take(h2, tok, axis=0)
        ys = _experts(xs, tile_e, tile_nv, w1[l], b1[l], w2[l], b2[l])
        yg = jnp.take(ys, pos.T, axis=0)
        x_all = _combine(xmid, yg, wgt, mod_p[:, 0, 5 * D_MODEL:], mod_s[:, 5 * D_MODEL:],
                         final_g, n_prompt, seq_len, final=(l == depth - 1))

    y_prompt = x_all[:n_prompt].reshape(n_bp, seq_len, D_MODEL)
    y_sample = x_all[n_prompt:].reshape(n_pos, n_bs, D_MODEL).transpose(1, 0, 2)
    return (y_prompt, y_sample, jnp.stack(conv_p, axis=0), jnp.stack(conv_s, axis=0),
            jnp.stack(v_s, axis=0))
```

```python
import functools
import math

import jax
import jax.numpy as jnp
from jax import lax
from jax.experimental import pallas as pl
from jax.experimental.pallas import tpu as pltpu
from jax.experimental.pallas import tpu_sc as plsc

F32 = jnp.float32
BF16 = jnp.bfloat16

D_MODEL = 1024
C_CONV = 512
C_SGU = 512
CONV_GROUPS = 8
CONV_WIDTH = 31
CONV_STATE = CONV_WIDTH - 1
SGU_HEADS = 4
SGU_HEAD_DIM = C_SGU // SGU_HEADS
CHUNK = 128
N_EXPERTS = 32
TOP_K = 4
D_FF = 1024
SWIGLU_LIMIT = 7.0
SWIGLU_ALPHA = 1.702
N_MOD = 6
EPS = 1e-5

ROW_TILE = 512
ROW_CHUNK = 32
CARRY_ROWS = 32
SUBLANES = 8
GEMM_TILE = 512
VMEM_LIMIT = 56 * 1024 * 1024
D_WORDS = D_MODEL // 2
SC_CHUNK = 48
HI_MASK = -65536
SC_CORES = 2
SC_SUBCORES = 16


def _rms(x):
    return x * lax.rsqrt(jnp.mean(x * x, axis=-1, keepdims=True) + EPS)


def _gelu(x):
    return 0.5 * x * (1.0 + lax.erf(x * (1.0 / math.sqrt(2.0))))


def _sigmoid(x):
    return 1.0 / (1.0 + jnp.exp(-x))


def _split_bf16(x):
    hi = x.astype(BF16)
    lo = (x - hi.astype(F32)).astype(BF16)
    return hi, lo


def _pack_bf16_pair(x):
    bits = lax.bitcast_convert_type(x.astype(BF16).astype(F32), jnp.int32)
    return lax.shift_right_logical(bits[:, :D_WORDS], 16) | (bits[:, D_WORDS:] & HI_MASK)


def _unpack_bf16_pair(w):
    left = lax.bitcast_convert_type(lax.shift_left(w, 16), F32)
    right = lax.bitcast_convert_type(w & HI_MASK, F32)
    return left, right


def _dot(a, b):
    return jnp.dot(a, b, preferred_element_type=F32)


def _row_loop(n_rows, body):
    def step(i, carry):
        body(pl.multiple_of(i * ROW_CHUNK, ROW_CHUNK))
        return carry
    lax.fori_loop(0, n_rows // ROW_CHUNK, step, 0)


def _adaln_kernel(c_ref, w_ref, b_ref, o_ref):
    c = c_ref[...]
    s_hi, s_lo = _split_bf16(c * _sigmoid(c))
    w_hi, w_lo = _split_bf16(w_ref[0])
    acc = _dot(s_hi, w_hi) + _dot(s_hi, w_lo) + _dot(s_lo, w_hi)
    o_ref[0] = acc + b_ref[0]


def _adaln(c_all, w_ada, b_ada):
    depth, _, n_out = w_ada.shape
    n_rows = c_all.shape[0]
    tn = 1024
    return pl.pallas_call(
        _adaln_kernel,
        grid=(depth, n_out // tn),
        in_specs=[
            pl.BlockSpec((n_rows, D_MODEL), lambda l, j: (0, 0)),
            pl.BlockSpec((1, D_MODEL, tn), lambda l, j: (l, 0, j)),
            pl.BlockSpec((1, 1, tn), lambda l, j: (l, 0, j)),
        ],
        out_specs=pl.BlockSpec((1, n_rows, tn), lambda l, j: (l, 0, j)),
        out_shape=jax.ShapeDtypeStruct((depth, n_rows, n_out), F32),
        compiler_params=pltpu.CompilerParams(
            dimension_semantics=("arbitrary", "arbitrary"), vmem_limit_bytes=VMEM_LIMIT),
        name="adaln",
    )(c_all, w_ada, b_ada.reshape(depth, 1, n_out))


def _phase_in_norm(x_ref, h_scr, mod, n_rows):
    def body(r0):
        r = pl.ds(r0, ROW_CHUNK)
        h = _rms(x_ref[r, :]) * (1.0 + mod(1, r0)) + mod(0, r0)
        h_scr[r, :] = h.astype(BF16)
    _row_loop(n_rows, body)


def _phase_gates(z_scr, vec_ref, a_dst, a_off, u_scr, v_scr, vout_ref, n_rows):
    ln_g = vec_ref[3:4, :]
    ln_b = vec_ref[4:5, :]

    def body(r0):
        r = pl.ds(r0, ROW_CHUNK)
        a = z_scr[r, 0:C_CONV] * _sigmoid(z_scr[r, C_CONV:2 * C_CONV])
        a_dst[pl.ds(r0 + a_off, ROW_CHUNK), :] = a
        u_scr[r, :] = _gelu(z_scr[r, 2 * C_CONV:2 * C_CONV + C_SGU])
        gv = _gelu(z_scr[r, 2 * C_CONV + C_SGU:2 * C_CONV + 2 * C_SGU])
        mu = jnp.mean(gv, axis=-1, keepdims=True)
        dv = gv - mu
        var = jnp.mean(dv * dv, axis=-1, keepdims=True)
        v = dv * lax.rsqrt(var + EPS) * ln_g + ln_b
        if vout_ref is not None:
            vout_ref[r, :] = v
        v_scr[r, :] = v.astype(v_scr.dtype)
    _row_loop(n_rows, body)


def _phase_group_norm(conv_scr, convb_scr, stat_scr, g_ref, vec_ref, y_scr, n_rows):
    stat_scr[...] = _dot(convb_scr[...], g_ref[...])

    def center(r0):
        r = pl.ds(r0, ROW_CHUNK)
        d = conv_scr[r, :] - stat_scr[r, :]
        conv_scr[r, :] = d
        convb_scr[r, :] = (d * d).astype(BF16)
    _row_loop(n_rows, center)
    stat_scr[...] = _dot(convb_scr[...], g_ref[...])
    gn_g = vec_ref[1:2, :]
    gn_b = vec_ref[2:3, :]
    beta_a = vec_ref[5:6, :]

    def finish(r0):
        r = pl.ds(r0, ROW_CHUNK)
        gn = conv_scr[r, :] * lax.rsqrt(stat_scr[r, :] + EPS) * gn_g + gn_b
        ya = gn * _sigmoid(gn)
        y_scr[r, 0:C_CONV] = (_rms(ya) * beta_a).astype(BF16)
    _row_loop(n_rows, finish)


def _phase_out(x_ref, y_scr, wout_ref, z_scr, wr_ref, br_ref, mod, h_scr,
               xmid_ref, h2_ref, logit_ref, n_rows):
    z_scr[:, 0:D_MODEL] = _dot(y_scr[...], wout_ref[...])

    def body(r0):
        r = pl.ds(r0, ROW_CHUNK)
        xm = x_ref[r, :] + mod(2, r0) * z_scr[r, 0:D_MODEL]
        xmid_ref[r, :] = xm
        h2 = _rms(xm) * (1.0 + mod(4, r0)) + mod(3, r0)
        hi, lo = _split_bf16(h2)
        h2_ref[r, :] = _pack_bf16_pair(h2)
        y_scr[r, :] = hi
        h_scr[r, :] = lo
    _row_loop(n_rows, body)
    w_hi, w_lo = _split_bf16(wr_ref[...])
    hi = y_scr[...]
    logit_ref[...] = (_dot(hi, w_hi) + _dot(hi, w_lo) + _dot(h_scr[...], w_hi)) + br_ref[...]


def _mixer_prompt_kernel(x_ref, mod_ref, win_ref, wout_ref, wconv_ref, vec_ref, g_ref,
                         ws_ref, bs_ref, wr_ref, br_ref,
                         xmid_ref, h2_ref, logit_ref, cst_ref,
                         h_scr, z_scr, aext_scr, conv_scr, convb_scr, stat_scr,
                         u_scr, v_scr, y_scr):
    j = pl.program_id(1)
    n_rows = ROW_TILE

    def mod(idx, r0):
        del r0
        return mod_ref[0, :, idx * D_MODEL:(idx + 1) * D_MODEL]

    a_buf = aext_scr.at[0]

    @pl.when(j == 0)
    def _():
        a_buf[0:CARRY_ROWS, :] = jnp.zeros((CARRY_ROWS, C_CONV), F32)

    _phase_in_norm(x_ref, h_scr, mod, n_rows)
    z_scr[...] = _dot(h_scr[...], win_ref[...])
    _phase_gates(z_scr, vec_ref, a_buf, CARRY_ROWS, u_scr, v_scr, None, n_rows)
    n_shift = n_rows + CARRY_ROWS - SUBLANES
    for s in range(1, SUBLANES):
        aext_scr[s, 0:n_shift, :] = a_buf[s:s + n_shift, :]

    b_conv = vec_ref[0:1, :]
    lead = CARRY_ROWS - CONV_STATE

    def conv_body(r0):
        acc = jnp.zeros((ROW_CHUNK, C_CONV), F32)
        for k in range(CONV_WIDTH):
            s = (lead + k) % SUBLANES
            acc = acc + wconv_ref[k:k + 1, :] * aext_scr[s, pl.ds(r0 + (lead + k - s), ROW_CHUNK), :]
        conv = acc + b_conv
        r = pl.ds(r0, ROW_CHUNK)
        conv_scr[r, :] = conv
        convb_scr[r, :] = conv.astype(BF16)
    _row_loop(n_rows, conv_body)

    @pl.when(j == pl.num_programs(1) - 1)
    def _():
        cst_ref[0] = a_buf[n_rows + lead:n_rows + CARRY_ROWS, :]

    a_buf[0:CARRY_ROWS, :] = a_buf[n_rows:n_rows + CARRY_ROWS, :]

    _phase_group_norm(conv_scr, convb_scr, stat_scr, g_ref, vec_ref, y_scr, n_rows)

    row_i = lax.broadcasted_iota(jnp.int32, (CHUNK, CHUNK), 0)
    col_i = lax.broadcasted_iota(jnp.int32, (CHUNK, CHUNK), 1)
    tril = (col_i <= row_i).astype(F32)
    for hd in range(SGU_HEADS):
        ws_h = (ws_ref[hd] * tril).astype(BF16)
        cols = slice(hd * SGU_HEAD_DIM, (hd + 1) * SGU_HEAD_DIM)
        for c in range(n_rows // CHUNK):
            rows = slice(c * CHUNK, (c + 1) * CHUNK)
            stat_scr[rows, cols] = _dot(ws_h, v_scr[rows, cols])
    beta_b = vec_ref[6:7, :]

    def gate_body(r0):
        r = pl.ds(r0, ROW_CHUNK)
        rb = pl.ds(pl.multiple_of(lax.rem(r0, CHUNK), ROW_CHUNK), ROW_CHUNK)
        yb = u_scr[r, :] * (stat_scr[r, :] + bs_ref[rb, :])
        y_scr[r, C_CONV:C_CONV + C_SGU] = (_rms(yb) * beta_b).astype(BF16)
    _row_loop(n_rows, gate_body)

    _phase_out(x_ref, y_scr, wout_ref, z_scr, wr_ref, br_ref, mod, h_scr,
               xmid_ref, h2_ref, logit_ref, n_rows)


def _mixer_sample_kernel(x_ref, mod_ref, win_ref, wout_ref, wconv_ref, vec_ref, g_ref,
                         wsv_ref, bsv_ref, wr_ref, br_ref, state_ref,
                         xmid_in, h2_in, logit_in,
                         xmid_ref, h2_ref, logit_ref, anew_ref, vout_ref,
                         h_scr, z_scr, conv_scr, convb_scr, stat_scr, u_scr, v_scr, y_scr):
    del xmid_in, h2_in, logit_in
    n_rows = x_ref.shape[0]
    n_seq = state_ref.shape[1]
    n_pos = n_rows // n_seq

    def mod(idx, r0):
        rs = pl.ds(pl.multiple_of(lax.rem(r0, n_seq), ROW_CHUNK), ROW_CHUNK)
        return mod_ref[rs, idx * D_MODEL:(idx + 1) * D_MODEL]

    _phase_in_norm(x_ref, h_scr, mod, n_rows)
    z_scr[...] = _dot(h_scr[...], win_ref[...])
    _phase_gates(z_scr, vec_ref, anew_ref, 0, u_scr, v_scr, vout_ref, n_rows)

    b_conv = vec_ref[0:1, :]

    def conv_body(s0):
        rs = pl.ds(s0, ROW_CHUNK)
        for t in range(n_pos):
            acc = jnp.zeros((ROW_CHUNK, C_CONV), F32)
            for k in range(CONV_WIDTH):
                p = t + k
                if p < CONV_STATE:
                    src = state_ref[p, rs, :]
                else:
                    src = anew_ref[pl.ds(s0 + (p - CONV_STATE) * n_seq, ROW_CHUNK), :]
                acc = acc + wconv_ref[k:k + 1, :] * src
            conv = acc + b_conv
            r = pl.ds(s0 + t * n_seq, ROW_CHUNK)
            conv_scr[r, :] = conv
            convb_scr[r, :] = conv.astype(BF16)
    _row_loop(n_seq, conv_body)

    _phase_group_norm(conv_scr, convb_scr, stat_scr, g_ref, vec_ref, y_scr, n_rows)

    beta_b = vec_ref[6:7, :]

    def gate_body(s0):
        for t in range(n_pos):
            s = jnp.zeros((ROW_CHUNK, C_SGU), F32)
            for jj in range(t + 1):
                s = s + wsv_ref[t * n_pos + jj:t * n_pos + jj + 1, :] * v_scr[pl.ds(s0 + jj * n_seq, ROW_CHUNK), :]
            r = pl.ds(s0 + t * n_seq, ROW_CHUNK)
            yb = u_scr[r, :] * (s + bsv_ref[t:t + 1, :])
            y_scr[r, C_CONV:C_CONV + C_SGU] = (_rms(yb) * beta_b).astype(BF16)
    _row_loop(n_seq, gate_body)

    _phase_out(x_ref, y_scr, wout_ref, z_scr, wr_ref, br_ref, mod, h_scr,
               xmid_ref, h2_ref, logit_ref, n_rows)


def _const_spec(shape):
    nd = len(shape)
    return pl.BlockSpec(shape, lambda *_: (0,) * nd)


def _mixer(x_all, mod_p, mod_s, w_in_b, w_out_b, wconv_p, vec, g_mat, w_s, bs_full, wsv, bsv,
           w_router, b_router, state_t, n_prompt_seq, seq_len):
    t_all = x_all.shape[0]
    n_prompt = n_prompt_seq * seq_len
    n_sample = t_all - n_prompt
    nj = seq_len // ROW_TILE
    weight_specs = [
        _const_spec((D_MODEL, 2 * C_CONV + 2 * C_SGU)),
        _const_spec((D_MODEL, D_MODEL)),
        _const_spec((CARRY_ROWS, C_CONV)),
        _const_spec((8, C_CONV)),
        _const_spec((C_CONV, C_CONV)),
    ]
    router_specs = [_const_spec((D_MODEL, N_EXPERTS)), _const_spec((1, N_EXPERTS))]
    common_scratch = lambda n: [
        pltpu.VMEM((n, D_MODEL), BF16),
        pltpu.VMEM((n, 2 * C_CONV + 2 * C_SGU), F32),
    ]
    tail_scratch = lambda n, vdt: [
        pltpu.VMEM((n, C_CONV), F32),
        pltpu.VMEM((n, C_CONV), BF16),
        pltpu.VMEM((n, C_CONV), F32),
        pltpu.VMEM((n, C_SGU), F32),
        pltpu.VMEM((n, C_SGU), vdt),
        pltpu.VMEM((n, D_MODEL), BF16),
    ]
    out_shapes = (
        jax.ShapeDtypeStruct((t_all, D_MODEL), F32),
        jax.ShapeDtypeStruct((t_all, D_WORDS), jnp.int32),
        jax.ShapeDtypeStruct((t_all, N_EXPERTS), F32),
    )
    row_map = lambda b, j: (b * nj + j, 0)
    xmid, h2, logits, cst = pl.pallas_call(
        _mixer_prompt_kernel,
        grid=(n_prompt_seq, nj),
        in_specs=[
            pl.BlockSpec((ROW_TILE, D_MODEL), row_map),
            pl.BlockSpec((1, 1, N_MOD * D_MODEL), lambda b, j: (b, 0, 0)),
            *weight_specs,
            _const_spec((SGU_HEADS, CHUNK, CHUNK)),
            _const_spec((CHUNK, C_SGU)),
            *router_specs,
        ],
        out_specs=(
            pl.BlockSpec((ROW_TILE, D_MODEL), row_map),
            pl.BlockSpec((ROW_TILE, D_WORDS), row_map),
            pl.BlockSpec((ROW_TILE, N_EXPERTS), row_map),
            pl.BlockSpec((1, CONV_STATE, C_CONV), lambda b, j: (b, 0, 0)),
        ),
        out_shape=out_shapes + (jax.ShapeDtypeStruct((n_prompt_seq, CONV_STATE, C_CONV), F32),),
        scratch_shapes=common_scratch(ROW_TILE)
        + [pltpu.VMEM((SUBLANES, CARRY_ROWS + ROW_TILE, C_CONV), F32)]
        + tail_scratch(ROW_TILE, BF16),
        compiler_params=pltpu.CompilerParams(
            dimension_semantics=("arbitrary", "arbitrary"), vmem_limit_bytes=VMEM_LIMIT),
        name="mixer_prompt",
    )(x_all, mod_p, w_in_b, w_out_b, wconv_p, vec, g_mat, w_s, bs_full, w_router, b_router)

    blk = n_prompt // n_sample
    srow_map = lambda i: (blk, 0)
    any_spec = pl.BlockSpec(memory_space=pl.ANY)
    n_seq = state_t.shape[1]
    n_pos = n_sample // n_seq
    xmid, h2, logits, a_new, v_new = pl.pallas_call(
        _mixer_sample_kernel,
        grid=(1,),
        in_specs=[
            pl.BlockSpec((n_sample, D_MODEL), srow_map),
            _const_spec((n_seq, N_MOD * D_MODEL)),
            *weight_specs,
            _const_spec((n_pos * n_pos, C_SGU)),
            _const_spec((n_pos, C_SGU)),
            *router_specs,
            _const_spec((CONV_STATE, n_seq, C_CONV)),
            any_spec, any_spec, any_spec,
        ],
        out_specs=(
            pl.BlockSpec((n_sample, D_MODEL), srow_map),
            pl.BlockSpec((n_sample, D_WORDS), srow_map),
            pl.BlockSpec((n_sample, N_EXPERTS), srow_map),
            _const_spec((n_sample, C_CONV)),
            _const_spec((n_sample, C_SGU)),
        ),
        out_shape=out_shapes + (
            jax.ShapeDtypeStruct((n_sample, C_CONV), F32),
            jax.ShapeDtypeStruct((n_sample, C_SGU), F32),
        ),
        input_output_aliases={12: 0, 13: 1, 14: 2},
        scratch_shapes=common_scratch(n_sample) + tail_scratch(n_sample, F32),
        compiler_params=pltpu.CompilerParams(
            dimension_semantics=("arbitrary",), vmem_limit_bytes=VMEM_LIMIT),
        name="mixer_sample",
    )(x_all, mod_s, w_in_b, w_out_b, wconv_p, vec, g_mat, wsv, bsv, w_router, b_router,
      state_t, xmid, h2, logits)
    return xmid, h2, logits, cst, a_new, v_new


def _router_kernel(logit_ref, eid_ref, wgt_ref, rank_ref, cnt_ref, cnt_scr):
    i = pl.program_id(0)
    n = logit_ref.shape[0]

    @pl.when(i == 0)
    def _():
        cnt_scr[...] = jnp.zeros_like(cnt_scr)

    lg = logit_ref[...]
    lane = lax.broadcasted_iota(jnp.int32, lg.shape, 1).astype(F32)
    vals, sels = [], []
    for k in range(TOP_K):
        m = jnp.max(lg, axis=-1, keepdims=True)
        idx = jnp.min(jnp.where(lg == m, lane, float(N_EXPERTS)), axis=-1, keepdims=True)
        sel = lane == idx
        eid_ref[:, k:k + 1] = idx.astype(jnp.int32)
        vals.append(m)
        sels.append(sel)
        lg = jnp.where(sel, -jnp.inf, lg)
    exps = [jnp.exp(v - vals[0]) for v in vals]
    inv = 1.0 / (exps[0] + exps[1] + exps[2] + exps[3])
    for k in range(TOP_K):
        wgt_ref[:, k:k + 1] = exps[k] * inv

    onehot = jnp.zeros(lg.shape, F32)
    for sel in sels:
        onehot = onehot + sel.astype(F32)
    row_i = lax.broadcasted_iota(jnp.int32, (n, n), 0)
    col_i = lax.broadcasted_iota(jnp.int32, (n, n), 1)
    before = jnp.where(col_i < row_i, 1.0, 0.0).astype(BF16)
    prefix = _dot(before, onehot.astype(BF16)) + cnt_scr[...]
    for k in range(TOP_K):
        rk = jnp.sum(jnp.where(sels[k], prefix, 0.0), axis=-1, keepdims=True)
        rank_ref[:, k:k + 1] = rk.astype(jnp.int32)
    total = cnt_scr[...] + jnp.sum(onehot, axis=0, keepdims=True)
    cnt_scr[...] = total
    cnt_ref[...] = total.astype(jnp.int32)


def _router(logits):
    t_all = logits.shape[0]
    row_spec = lambda w: pl.BlockSpec((ROW_TILE, w), lambda i: (i, 0))
    return pl.pallas_call(
        _router_kernel,
        grid=(t_all // ROW_TILE,),
        in_specs=[row_spec(N_EXPERTS)],
        out_specs=(row_spec(TOP_K), row_spec(TOP_K), row_spec(TOP_K), _const_spec((1, N_EXPERTS))),
        out_shape=(
            jax.ShapeDtypeStruct((t_all, TOP_K), jnp.int32),
            jax.ShapeDtypeStruct((t_all, TOP_K), F32),
            jax.ShapeDtypeStruct((t_all, TOP_K), jnp.int32),
            jax.ShapeDtypeStruct((1, N_EXPERTS), jnp.int32),
        ),
        scratch_shapes=[pltpu.VMEM((1, N_EXPERTS), F32)],
        compiler_params=pltpu.CompilerParams(
            dimension_semantics=("arbitrary",), vmem_limit_bytes=VMEM_LIMIT),
        name="router",
    )(logits)


def _expert_kernel(te_ref, nv_ref, x_ref, w1_ref, b1_ref, w2_ref, b2_ref, o_ref,
                   w1b_scr, w2b_scr, gu_scr, act_scr):
    g = pl.program_id(0)
    nv = nv_ref[g]
    prev = te_ref[jnp.maximum(g - 1, 0)]
    fresh = jnp.logical_or(g == 0, te_ref[g] != prev)

    @pl.when(jnp.logical_and(fresh, nv > 0))
    def _():
        w1b_scr[...] = w1_ref[0].astype(BF16)
        w2b_scr[...] = w2_ref[0].astype(BF16)

    @pl.when(nv > 0)
    def _():
        x_left, x_right = _unpack_bf16_pair(x_ref[...])
        gu_scr[...] = (_dot(x_left.astype(BF16), w1b_scr[0:D_WORDS, :])
                       + _dot(x_right.astype(BF16), w1b_scr[D_WORDS:D_MODEL, :]))

        def body(r0):
            r = pl.ds(r0, ROW_CHUNK)
            gg = jnp.minimum(gu_scr[r, 0:D_FF] + b1_ref[0, :, 0:D_FF], SWIGLU_LIMIT)
            up = jnp.clip(gu_scr[r, D_FF:2 * D_FF] + b1_ref[0, :, D_FF:2 * D_FF],
                          -SWIGLU_LIMIT, SWIGLU_LIMIT)
            act = gg * _sigmoid(SWIGLU_ALPHA * gg) * (up + 1.0)
            act_scr[r, :] = act.astype(BF16)
        _row_loop(GEMM_TILE, body)
        y = _dot(act_scr[...], w2b_scr[...]) + b2_ref[0]
        row = lax.broadcasted_iota(jnp.int32, y.shape, 0)
        o_ref[...] = _pack_bf16_pair(jnp.where(row < nv, y, 0.0))

    @pl.when(nv == 0)
    def _():
        o_ref[...] = jnp.zeros_like(o_ref)


def _experts(xs, tile_e, tile_nv, w1, b1, w2, b2):
    n_tiles = tile_e.shape[0]
    grid_spec = pltpu.PrefetchScalarGridSpec(
        num_scalar_prefetch=2,
        grid=(n_tiles,),
        in_specs=[
            pl.BlockSpec((GEMM_TILE, D_WORDS), lambda g, te, nv: (g, 0)),
            pl.BlockSpec((1, D_MODEL, 2 * D_FF), lambda g, te, nv: (te[g], 0, 0)),
            pl.BlockSpec((1, 1, 2 * D_FF), lambda g, te, nv: (te[g], 0, 0)),
            pl.BlockSpec((1, D_FF, D_MODEL), lambda g, te, nv: (te[g], 0, 0)),
            pl.BlockSpec((1, 1, D_MODEL), lambda g, te, nv: (te[g], 0, 0)),
        ],
        out_specs=pl.BlockSpec((GEMM_TILE, D_WORDS), lambda g, te, nv: (g, 0)),
        scratch_shapes=[
            pltpu.VMEM((D_MODEL, 2 * D_FF), BF16),
            pltpu.VMEM((D_FF, D_MODEL), BF16),
            pltpu.VMEM((GEMM_TILE, 2 * D_FF), F32),
            pltpu.VMEM((GEMM_TILE, D_FF), BF16),
        ],
    )
    return pl.pallas_call(
        _expert_kernel,
        grid_spec=grid_spec,
        out_shape=jax.ShapeDtypeStruct((n_tiles * GEMM_TILE, D_WORDS), jnp.int32),
        compiler_params=pltpu.CompilerParams(
            dimension_semantics=("arbitrary",), vmem_limit_bytes=VMEM_LIMIT),
        name="experts",
    )(tile_e, tile_nv, xs, w1, b1.reshape(N_EXPERTS, 1, 2 * D_FF), w2,
      b2.reshape(N_EXPERTS, 1, D_MODEL))


def _combine_kernel(x_ref, yg_ref, wgt_ref, gp_ref, gs_ref, fg_ref, o_ref, *, n_prompt_blocks, final):
    i = pl.program_id(0)
    n_seq = gs_ref.shape[0]

    def run(gate):
        def body(r0):
            r = pl.ds(r0, ROW_CHUNK)
            w = wgt_ref[r, :]
            moe_l = jnp.zeros((ROW_CHUNK, D_WORDS), F32)
            moe_r = jnp.zeros((ROW_CHUNK, D_WORDS), F32)
            for k in range(TOP_K):
                left, right = _unpack_bf16_pair(yg_ref[k, r, :])
                moe_l = moe_l + w[:, k:k + 1] * left
                moe_r = moe_r + w[:, k:k + 1] * right
            g = gate(r0)
            xl = x_ref[r, 0:D_WORDS] + g[:, 0:D_WORDS] * moe_l
            xr = x_ref[r, D_WORDS:D_MODEL] + g[:, D_WORDS:D_MODEL] * moe_r
            if final:
                ms = (jnp.sum(xl * xl, axis=-1, keepdims=True)
                      + jnp.sum(xr * xr, axis=-1, keepdims=True)) * (1.0 / D_MODEL)
                scale = lax.rsqrt(ms + EPS)
                xl = xl * scale * fg_ref[:, 0:D_WORDS]
                xr = xr * scale * fg_ref[:, D_WORDS:D_MODEL]
            o_ref[r, 0:D_WORDS] = xl
            o_ref[r, D_WORDS:D_MODEL] = xr
        _row_loop(x_ref.shape[0], body)

    @pl.when(i < n_prompt_blocks)
    def _():
        run(lambda r0: gp_ref[0])

    @pl.when(i >= n_prompt_blocks)
    def _():
        run(lambda r0: gs_ref[pl.ds(pl.multiple_of(lax.rem(r0, n_seq), ROW_CHUNK), ROW_CHUNK), :])


def _combine(xmid, yg, wgt, gate_p, gate_s, final_g, n_prompt, seq_len, final):
    t_all = xmid.shape[0]
    n_prompt_blocks = n_prompt // ROW_TILE
    per_seq = seq_len // ROW_TILE
    n_seq_p = gate_p.shape[0]
    kern = functools.partial(_combine_kernel, n_prompt_blocks=n_prompt_blocks, final=final)
    return pl.pallas_call(
        kern,
        grid=(t_all // ROW_TILE,),
        in_specs=[
            pl.BlockSpec((ROW_TILE, D_MODEL), lambda i: (i, 0)),
            pl.BlockSpec((TOP_K, ROW_TILE, D_WORDS), lambda i: (0, i, 0)),
            pl.BlockSpec((ROW_TILE, TOP_K), lambda i: (i, 0)),
            pl.BlockSpec((1, 1, D_MODEL), lambda i: (jnp.minimum(i // per_seq, n_seq_p - 1), 0, 0)),
            _const_spec(gate_s.shape),
            _const_spec((1, D_MODEL)),
        ],
        out_specs=pl.BlockSpec((ROW_TILE, D_MODEL), lambda i: (i, 0)),
        out_shape=jax.ShapeDtypeStruct((t_all, D_MODEL), F32),
        compiler_params=pltpu.CompilerParams(
            dimension_semantics=("arbitrary",), vmem_limit_bytes=VMEM_LIMIT),
        name="combine",
    )(xmid, yg, wgt, gate_p.reshape(n_seq_p, 1, D_MODEL), gate_s, final_g.reshape(1, D_MODEL))


def _sc_mesh():
    return plsc.VectorSubcoreMesh(core_axis_name="core", subcore_axis_name="subcore",
                                  num_cores=SC_CORES, num_subcores=SC_SUBCORES)


def _sc_worker_id():
    return lax.axis_index("subcore") * SC_CORES + lax.axis_index("core")


def _sc_dispatch(h2w, pos_c, n_slots):
    n_chunks = pos_c.shape[1]

    @functools.partial(
        pl.kernel, mesh=_sc_mesh(),
        out_type=jax.ShapeDtypeStruct((n_slots, D_WORDS), jnp.int32),
        scratch_types=[pltpu.VMEM((TOP_K, SC_CHUNK), jnp.int32),
                       pltpu.VMEM((SC_CHUNK, D_WORDS), jnp.int32),
                       pltpu.SemaphoreType.DMA],
        name="moe_dispatch")
    def run(h_hbm, pos_hbm, out_hbm, idx_v, rows_v, sem):
        wid = _sc_worker_id()

        @pl.loop(0, n_chunks)
        def _(c):
            base = pl.multiple_of((wid * n_chunks + c) * SC_CHUNK, SUBLANES)
            pltpu.sync_copy(pos_hbm.at[wid, c], idx_v)
            pltpu.sync_copy(h_hbm.at[pl.ds(base, SC_CHUNK)], rows_v)
            copies = [pltpu.async_copy(rows_v, out_hbm.at[idx_v.at[k]], sem) for k in range(TOP_K)]
            for cp in copies:
                cp.wait()

    return run(h2w, pos_c)


def _sc_gather_back(ys, pos_c, t_all):
    n_chunks = pos_c.shape[1]

    @functools.partial(
        pl.kernel, mesh=_sc_mesh(),
        out_type=jax.ShapeDtypeStruct((TOP_K, t_all, D_WORDS), jnp.int32),
        scratch_types=[pltpu.VMEM((TOP_K, SC_CHUNK), jnp.int32),
                       pltpu.VMEM((TOP_K, SC_CHUNK, D_WORDS), jnp.int32),
                       pltpu.SemaphoreType.DMA],
        name="moe_gather_back")
    def run(ys_hbm, pos_hbm, out_hbm, idx_v, rows_v, sem):
        wid = _sc_worker_id()

        @pl.loop(0, n_chunks)
        def _(c):
            base = pl.multiple_of((wid * n_chunks + c) * SC_CHUNK, SUBLANES)
            pltpu.sync_copy(pos_hbm.at[wid, c], idx_v)
            copies = [pltpu.async_copy(ys_hbm.at[idx_v.at[k]], rows_v.at[k], sem) for k in range(TOP_K)]
            for cp in copies:
                cp.wait()
            for k in range(TOP_K):
                pltpu.sync_copy(rows_v.at[k], out_hbm.at[k, pl.ds(base, SC_CHUNK)])

    return run(ys, pos_c)


def _dispatch_plan(eid, rank, counts, n_tiles):
    counts = counts.reshape(N_EXPERTS)
    tiles_per = (counts + GEMM_TILE - 1) // GEMM_TILE
    tile_end = jnp.cumsum(tiles_per)
    tile_start = tile_end - tiles_per
    pos = (tile_start * GEMM_TILE)[eid] + rank
    g = jnp.arange(n_tiles, dtype=jnp.int32)
    te = jnp.sum((tile_end[None, :] <= g[:, None]).astype(jnp.int32), axis=1)
    active = te < N_EXPERTS
    te_c = jnp.minimum(te, N_EXPERTS - 1)
    nv = jnp.clip(counts[te_c] - (g - tile_start[te_c]) * GEMM_TILE, 0, GEMM_TILE)
    nv = jnp.where(active, nv, 0).astype(jnp.int32)
    last_e = te_c[jnp.maximum(jnp.sum(active.astype(jnp.int32)) - 1, 0)]
    te_c = jnp.where(active, te_c, last_e).astype(jnp.int32)
    return pos.astype(jnp.int32), te_c, nv


def kernel(x_prompt, x_sample, state_conv, c_prompt, c_sample, w_ada, b_ada, w_in, w_conv, b_conv,
           gn_g, gn_b, sgu_ln_g, sgu_ln_b, w_s, b_s, beta_a, beta_b, w_out, w_router, b_router,
           w1, b1, w2, b2, final_g):
    n_bp, seq_len, _ = x_prompt.shape
    n_bs, n_pos, _ = x_sample.shape
    depth = w_ada.shape[0]
    n_prompt = n_bp * seq_len
    n_sample = n_bs * n_pos
    t_all = n_prompt + n_sample
    n_tiles = (t_all * TOP_K) // GEMM_TILE + N_EXPERTS
    n_workers = SC_CORES * SC_SUBCORES
    assert t_all % (n_workers * SC_CHUNK) == 0

    x_all = jnp.concatenate(
        [x_prompt.reshape(n_prompt, D_MODEL),
         x_sample.transpose(1, 0, 2).reshape(n_sample, D_MODEL)], axis=0)
    mods = _adaln(jnp.concatenate([c_prompt, c_sample], axis=0), w_ada, b_ada)
    state_t = state_conv.transpose(0, 2, 1, 3)
    grp = jnp.arange(C_CONV) // (C_CONV // CONV_GROUPS)
    g_mat = ((grp[:, None] == grp[None, :]).astype(F32) / (C_CONV // CONV_GROUPS)).astype(BF16)

    conv_p, conv_s, v_s = [], [], []
    for l in range(depth):
        mod_p = mods[l, :n_bp].reshape(n_bp, 1, N_MOD * D_MODEL)
        mod_s = mods[l, n_bp:]
        wconv_p = jnp.pad(w_conv[l], ((0, CARRY_ROWS - CONV_WIDTH), (0, 0)))
        zero = jnp.zeros_like(b_conv[l])
        vec = jnp.stack([b_conv[l], gn_g[l], gn_b[l], sgu_ln_g[l], sgu_ln_b[l],
                         beta_a[l], beta_b[l], zero], axis=0)
        bs_full = jnp.repeat(b_s[l].T, SGU_HEAD_DIM, axis=1)
        wsv = jnp.repeat(w_s[l][:, :n_pos, :n_pos].transpose(1, 2, 0).reshape(n_pos * n_pos, SGU_HEADS),
                         SGU_HEAD_DIM, axis=1)
        bsv = jnp.repeat(b_s[l][:, :n_pos].T, SGU_HEAD_DIM, axis=1)
        xmid, h2, logits, cst, a_new, v_new = _mixer(
            x_all, mod_p, mod_s, w_in[l].astype(BF16), w_out[l].astype(BF16), wconv_p, vec, g_mat,
            w_s[l], bs_full, wsv, bsv, w_router[l], b_router[l].reshape(1, N_EXPERTS),
            state_t[l], n_bp, seq_len)
        conv_p.append(cst)
        a_new = a_new.reshape(n_pos, n_bs, C_CONV).transpose(1, 0, 2)
        conv_s.append(jnp.concatenate([state_conv[l], a_new], axis=1)[:, -CONV_STATE:])
        v_s.append(v_new.reshape(n_pos, n_bs, C_SGU).transpose(1, 0, 2))

        eid, wgt, rank, counts = _router(logits)
        pos, tile_e, tile_nv = _dispatch_plan(eid, rank, counts, n_tiles)
        pos_c = pos.reshape(n_workers, -1, SC_CHUNK, TOP_K).transpose(0, 1, 3, 2)
        xs = _sc_dispatch(h2, pos_c, n_tiles * GEMM_TILE)
        ys = _experts(xs, tile_e, tile_nv, w1[l], b1[l], w2[l], b2[l])
        yg = _sc_gather_back(ys, pos_c, t_all)
        x_all = _combine(xmid, yg, wgt, mod_p[:, 0, 5 * D_MODEL:], mod_s[:, 5 * D_MODEL:],
                         final_g, n_prompt, seq_len, final=(l == depth - 1))

    y_prompt = x_all[:n_prompt].reshape(n_bp, seq_len, D_MODEL)
    y_sample = x_all[n_prompt:].reshape(n_pos, n_bs, D_MODEL).transpose(1, 0, 2)
    return (y_prompt, y_sample, jnp.stack(conv_p, axis=0), jnp.stack(conv_s, axis=0),
            jnp.stack(v_s, axis=0))
```

```python
import functools
import math

import jax
import jax.numpy as jnp
from jax import lax
from jax.experimental import pallas as pl
from jax.experimental.pallas import tpu as pltpu
from jax.experimental.pallas import tpu_sc as plsc

F32 = jnp.float32
BF16 = jnp.bfloat16

D_MODEL = 1024
C_CONV = 512
C_SGU = 512
CONV_GROUPS = 8
CONV_WIDTH = 31
CONV_STATE = CONV_WIDTH - 1
SGU_HEADS = 4
SGU_HEAD_DIM = C_SGU // SGU_HEADS
CHUNK = 128
N_EXPERTS = 32
TOP_K = 4
D_FF = 1024
SWIGLU_LIMIT = 7.0
SWIGLU_ALPHA = 1.702
N_MOD = 6
EPS = 1e-5

ROW_TILE = 512
ROW_CHUNK = 128
CONV_CHUNK = 32
COMBINE_CHUNK = 64
CARRY_ROWS = 32
SUBLANES = 8
GEMM_TILE = 512
FF_BLOCK = 256
VMEM_LIMIT = 56 * 1024 * 1024
D_WORDS = D_MODEL // 2
SC_CHUNK = 48
HI_MASK = -65536
SC_CORES = 2
SC_SUBCORES = 16


def _rms(x):
    return x * lax.rsqrt(jnp.mean(x * x, axis=-1, keepdims=True) + EPS)


def _gelu(x):
    return 0.5 * x * (1.0 + lax.erf(x * (1.0 / math.sqrt(2.0))))


def _sigmoid(x):
    return 1.0 / (1.0 + jnp.exp(-x))


def _split_bf16(x):
    hi = x.astype(BF16)
    lo = (x - hi.astype(F32)).astype(BF16)
    return hi, lo


def _pack_bf16_pair(x):
    bits = lax.bitcast_convert_type(x.astype(BF16).astype(F32), jnp.int32)
    return lax.shift_right_logical(bits[:, :D_WORDS], 16) | (bits[:, D_WORDS:] & HI_MASK)


def _unpack_bf16_pair(w):
    left = lax.bitcast_convert_type(lax.shift_left(w, 16), F32)
    right = lax.bitcast_convert_type(w & HI_MASK, F32)
    return left, right


def _dot(a, b):
    return jnp.dot(a, b, preferred_element_type=F32)


def _row_loop(n_rows, body, chunk=None):
    chunk = ROW_CHUNK if chunk is None else chunk

    def step(i, carry):
        body(pl.multiple_of(i * chunk, chunk))
        return carry
    lax.fori_loop(0, n_rows // chunk, step, 0)


def _adaln_kernel(c_ref, w_ref, b_ref, o_ref):
    c = c_ref[...]
    s_hi, s_lo = _split_bf16(c * _sigmoid(c))
    w_hi, w_lo = _split_bf16(w_ref[0])
    acc = _dot(s_hi, w_hi) + _dot(s_hi, w_lo) + _dot(s_lo, w_hi)
    o_ref[0] = acc + b_ref[0]


def _adaln(c_all, w_ada, b_ada):
    depth, _, n_out = w_ada.shape
    n_rows = c_all.shape[0]
    tn = 1024
    return pl.pallas_call(
        _adaln_kernel,
        grid=(depth, n_out // tn),
        in_specs=[
            pl.BlockSpec((n_rows, D_MODEL), lambda l, j: (0, 0)),
            pl.BlockSpec((1, D_MODEL, tn), lambda l, j: (l, 0, j)),
            pl.BlockSpec((1, 1, tn), lambda l, j: (l, 0, j)),
        ],
        out_specs=pl.BlockSpec((1, n_rows, tn), lambda l, j: (l, 0, j)),
        out_shape=jax.ShapeDtypeStruct((depth, n_rows, n_out), F32),
        compiler_params=pltpu.CompilerParams(
            dimension_semantics=("arbitrary", "arbitrary"), vmem_limit_bytes=VMEM_LIMIT),
        name="adaln",
    )(c_all, w_ada, b_ada.reshape(depth, 1, n_out))


def _phase_in_norm(x_ref, h_scr, mod, n_rows):
    def body(r0):
        r = pl.ds(r0, ROW_CHUNK)
        h = _rms(x_ref[r, :]) * (1.0 + mod(1, r0)) + mod(0, r0)
        h_scr[r, :] = h.astype(BF16)
    _row_loop(n_rows, body)


def _phase_gates(z_scr, vec_ref, a_dst, a_off, u_scr, v_scr, vout_ref, n_rows):
    ln_g = vec_ref[3:4, :]
    ln_b = vec_ref[4:5, :]

    def body(r0):
        r = pl.ds(r0, ROW_CHUNK)
        a = z_scr[r, 0:C_CONV] * _sigmoid(z_scr[r, C_CONV:2 * C_CONV])
        a_dst[pl.ds(r0 + a_off, ROW_CHUNK), :] = a
        u_scr[r, :] = _gelu(z_scr[r, 2 * C_CONV:2 * C_CONV + C_SGU])
        gv = _gelu(z_scr[r, 2 * C_CONV + C_SGU:2 * C_CONV + 2 * C_SGU])
        mu = jnp.mean(gv, axis=-1, keepdims=True)
        dv = gv - mu
        var = jnp.mean(dv * dv, axis=-1, keepdims=True)
        v = dv * lax.rsqrt(var + EPS) * ln_g + ln_b
        if vout_ref is not None:
            vout_ref[r, :] = v
        v_scr[r, :] = v.astype(v_scr.dtype)
    _row_loop(n_rows, body)


def _phase_group_norm(conv_scr, convb_scr, stat_scr, g_ref, vec_ref, y_scr, n_rows):
    stat_scr[...] = _dot(convb_scr[...], g_ref[...])

    def center(r0):
        r = pl.ds(r0, ROW_CHUNK)
        d = conv_scr[r, :] - stat_scr[r, :]
        conv_scr[r, :] = d
        convb_scr[r, :] = (d * d).astype(BF16)
    _row_loop(n_rows, center)
    stat_scr[...] = _dot(convb_scr[...], g_ref[...])
    gn_g = vec_ref[1:2, :]
    gn_b = vec_ref[2:3, :]
    beta_a = vec_ref[5:6, :]

    def finish(r0):
        r = pl.ds(r0, ROW_CHUNK)
        gn = conv_scr[r, :] * lax.rsqrt(stat_scr[r, :] + EPS) * gn_g + gn_b
        ya = gn * _sigmoid(gn)
        y_scr[r, 0:C_CONV] = (_rms(ya) * beta_a).astype(BF16)
    _row_loop(n_rows, finish)


def _phase_out(x_ref, y_scr, wout_ref, z_scr, wr_ref, br_ref, mod, h_scr,
               xmid_ref, h2_ref, logit_ref, n_rows):
    z_scr[:, 0:D_MODEL] = _dot(y_scr[...], wout_ref[...])

    def body(r0):
        r = pl.ds(r0, ROW_CHUNK)
        xm = x_ref[r, :] + mod(2, r0) * z_scr[r, 0:D_MODEL]
        xmid_ref[r, :] = xm
        h2 = _rms(xm) * (1.0 + mod(4, r0)) + mod(3, r0)
        hi, lo = _split_bf16(h2)
        h2_ref[r, :] = _pack_bf16_pair(h2)
        y_scr[r, :] = hi
        h_scr[r, :] = lo
    _row_loop(n_rows, body)
    w_hi, w_lo = _split_bf16(wr_ref[...])
    hi = y_scr[...]
    logit_ref[...] = (_dot(hi, w_hi) + _dot(hi, w_lo) + _dot(h_scr[...], w_hi)) + br_ref[...]


def _mixer_prompt_kernel(x_ref, mod_ref, win_ref, wout_ref, wconv_ref, vec_ref, g_ref,
                         ws_ref, bs_ref, wr_ref, br_ref,
                         xmid_ref, h2_ref, logit_ref, cst_ref,
                         h_scr, z_scr, aext_scr, conv_scr, convb_scr, stat_scr,
                         u_scr, v_scr, y_scr):
    j = pl.program_id(1)
    n_rows = ROW_TILE

    def mod(idx, r0):
        del r0
        return mod_ref[0, :, idx * D_MODEL:(idx + 1) * D_MODEL]

    a_buf = aext_scr.at[0]

    @pl.when(j == 0)
    def _():
        a_buf[0:CARRY_ROWS, :] = jnp.zeros((CARRY_ROWS, C_CONV), F32)

    _phase_in_norm(x_ref, h_scr, mod, n_rows)
    z_scr[...] = _dot(h_scr[...], win_ref[...])
    _phase_gates(z_scr, vec_ref, a_buf, CARRY_ROWS, u_scr, v_scr, None, n_rows)
    n_shift = n_rows + CARRY_ROWS - SUBLANES
    for s in range(1, SUBLANES):
        aext_scr[s, 0:n_shift, :] = a_buf[s:s + n_shift, :]

    b_conv = vec_ref[0:1, :]
    lead = CARRY_ROWS - CONV_STATE

    def conv_body(r0):
        acc = jnp.zeros((CONV_CHUNK, C_CONV), F32)
        for k in range(CONV_WIDTH):
            s = (lead + k) % SUBLANES
            acc = acc + wconv_ref[k:k + 1, :] * aext_scr[s, pl.ds(r0 + (lead + k - s), CONV_CHUNK), :]
        conv = acc + b_conv
        r = pl.ds(r0, CONV_CHUNK)
        conv_scr[r, :] = conv
        convb_scr[r, :] = conv.astype(BF16)
    _row_loop(n_rows, conv_body, CONV_CHUNK)

    @pl.when(j == pl.num_programs(1) - 1)
    def _():
        cst_ref[0] = a_buf[n_rows + lead:n_rows + CARRY_ROWS, :]

    a_buf[0:CARRY_ROWS, :] = a_buf[n_rows:n_rows + CARRY_ROWS, :]

    _phase_group_norm(conv_scr, convb_scr, stat_scr, g_ref, vec_ref, y_scr, n_rows)

    row_i = lax.broadcasted_iota(jnp.int32, (CHUNK, CHUNK), 0)
    col_i = lax.broadcasted_iota(jnp.int32, (CHUNK, CHUNK), 1)
    tril = (col_i <= row_i).astype(F32)
    for hd in range(SGU_HEADS):
        ws_h = (ws_ref[hd] * tril).astype(BF16)
        cols = slice(hd * SGU_HEAD_DIM, (hd + 1) * SGU_HEAD_DIM)
        for c in range(n_rows // CHUNK):
            rows = slice(c * CHUNK, (c + 1) * CHUNK)
            stat_scr[rows, cols] = _dot(ws_h, v_scr[rows, cols])
    beta_b = vec_ref[6:7, :]

    def gate_body(r0):
        r = pl.ds(r0, ROW_CHUNK)
        rb = pl.ds(pl.multiple_of(lax.rem(r0, CHUNK), ROW_CHUNK), ROW_CHUNK)
        yb = u_scr[r, :] * (stat_scr[r, :] + bs_ref[rb, :])
        y_scr[r, C_CONV:C_CONV + C_SGU] = (_rms(yb) * beta_b).astype(BF16)
    _row_loop(n_rows, gate_body)

    _phase_out(x_ref, y_scr, wout_ref, z_scr, wr_ref, br_ref, mod, h_scr,
               xmid_ref, h2_ref, logit_ref, n_rows)


def _mixer_sample_kernel(x_ref, mod_ref, win_ref, wout_ref, wconv_ref, vec_ref, g_ref,
                         wsv_ref, bsv_ref, wr_ref, br_ref, state_ref,
                         xmid_in, h2_in, logit_in,
                         xmid_ref, h2_ref, logit_ref, anew_ref, vout_ref,
                         h_scr, z_scr, conv_scr, convb_scr, stat_scr, u_scr, v_scr, y_scr):
    del xmid_in, h2_in, logit_in
    n_rows = x_ref.shape[0]
    n_seq = state_ref.shape[1]
    n_pos = n_rows // n_seq

    def mod(idx, r0):
        rs = pl.ds(pl.multiple_of(lax.rem(r0, n_seq), ROW_CHUNK), ROW_CHUNK)
        return mod_ref[rs, idx * D_MODEL:(idx + 1) * D_MODEL]

    _phase_in_norm(x_ref, h_scr, mod, n_rows)
    z_scr[...] = _dot(h_scr[...], win_ref[...])
    _phase_gates(z_scr, vec_ref, anew_ref, 0, u_scr, v_scr, vout_ref, n_rows)

    b_conv = vec_ref[0:1, :]

    def conv_body(s0):
        rs = pl.ds(s0, CONV_CHUNK)
        for t in range(n_pos):
            acc = jnp.zeros((CONV_CHUNK, C_CONV), F32)
            for k in range(CONV_WIDTH):
                p = t + k
                if p < CONV_STATE:
                    src = state_ref[p, rs, :]
                else:
                    src = anew_ref[pl.ds(s0 + (p - CONV_STATE) * n_seq, CONV_CHUNK), :]
                acc = acc + wconv_ref[k:k + 1, :] * src
            conv = acc + b_conv
            r = pl.ds(s0 + t * n_seq, CONV_CHUNK)
            conv_scr[r, :] = conv
            convb_scr[r, :] = conv.astype(BF16)
    _row_loop(n_seq, conv_body, CONV_CHUNK)

    _phase_group_norm(conv_scr, convb_scr, stat_scr, g_ref, vec_ref, y_scr, n_rows)

    beta_b = vec_ref[6:7, :]

    def gate_body(s0):
        for t in range(n_pos):
            s = jnp.zeros((CONV_CHUNK, C_SGU), F32)
            for jj in range(t + 1):
                s = s + wsv_ref[t * n_pos + jj:t * n_pos + jj + 1, :] * v_scr[pl.ds(s0 + jj * n_seq, CONV_CHUNK), :]
            r = pl.ds(s0 + t * n_seq, CONV_CHUNK)
            yb = u_scr[r, :] * (s + bsv_ref[t:t + 1, :])
            y_scr[r, C_CONV:C_CONV + C_SGU] = (_rms(yb) * beta_b).astype(BF16)
    _row_loop(n_seq, gate_body, CONV_CHUNK)

    _phase_out(x_ref, y_scr, wout_ref, z_scr, wr_ref, br_ref, mod, h_scr,
               xmid_ref, h2_ref, logit_ref, n_rows)


def _const_spec(shape):
    nd = len(shape)
    return pl.BlockSpec(shape, lambda *_: (0,) * nd)


def _mixer(x_p, x_s, x_s_block, mod_p, mod_s, w_in_b, w_out_b, wconv_p, vec, g_mat, w_s, bs_full,
           wsv, bsv, w_router, b_router, state_t, n_prompt_seq, seq_len):
    n_prompt = n_prompt_seq * seq_len
    n_seq, n_pos = state_t.shape[1], bsv.shape[0]
    n_sample = n_seq * n_pos
    t_all = n_prompt + n_sample
    nj = seq_len // ROW_TILE
    weight_specs = [
        _const_spec((D_MODEL, 2 * C_CONV + 2 * C_SGU)),
        _const_spec((D_MODEL, D_MODEL)),
        _const_spec((CARRY_ROWS, C_CONV)),
        _const_spec((8, C_CONV)),
        _const_spec((C_CONV, C_CONV)),
    ]
    router_specs = [_const_spec((D_MODEL, N_EXPERTS)), _const_spec((1, N_EXPERTS))]
    common_scratch = lambda n: [
        pltpu.VMEM((n, D_MODEL), BF16),
        pltpu.VMEM((n, 2 * C_CONV + 2 * C_SGU), F32),
    ]
    tail_scratch = lambda n, vdt: [
        pltpu.VMEM((n, C_CONV), F32),
        pltpu.VMEM((n, C_CONV), BF16),
        pltpu.VMEM((n, C_CONV), F32),
        pltpu.VMEM((n, C_SGU), F32),
        pltpu.VMEM((n, C_SGU), vdt),
        pltpu.VMEM((n, D_MODEL), BF16),
    ]
    out_shapes = (
        jax.ShapeDtypeStruct((t_all, D_MODEL), F32),
        jax.ShapeDtypeStruct((t_all, D_WORDS), jnp.int32),
        jax.ShapeDtypeStruct((t_all, N_EXPERTS), F32),
    )
    row_map = lambda b, j: (b * nj + j, 0)
    xmid, h2, logits, cst = pl.pallas_call(
        _mixer_prompt_kernel,
        grid=(n_prompt_seq, nj),
        in_specs=[
            pl.BlockSpec((ROW_TILE, D_MODEL), row_map),
            pl.BlockSpec((1, 1, N_MOD * D_MODEL), lambda b, j: (b, 0, 0)),
            *weight_specs,
            _const_spec((SGU_HEADS, CHUNK, CHUNK)),
            _const_spec((CHUNK, C_SGU)),
            *router_specs,
        ],
        out_specs=(
            pl.BlockSpec((ROW_TILE, D_MODEL), row_map),
            pl.BlockSpec((ROW_TILE, D_WORDS), row_map),
            pl.BlockSpec((ROW_TILE, N_EXPERTS), row_map),
            pl.BlockSpec((1, CONV_STATE, C_CONV), lambda b, j: (b, 0, 0)),
        ),
        out_shape=out_shapes + (jax.ShapeDtypeStruct((n_prompt_seq, CONV_STATE, C_CONV), F32),),
        scratch_shapes=common_scratch(ROW_TILE)
        + [pltpu.VMEM((SUBLANES, CARRY_ROWS + ROW_TILE, C_CONV), F32)]
        + tail_scratch(ROW_TILE, BF16),
        compiler_params=pltpu.CompilerParams(
            dimension_semantics=("arbitrary", "arbitrary"), vmem_limit_bytes=VMEM_LIMIT),
        name="mixer_prompt",
    )(x_p, mod_p, w_in_b, w_out_b, wconv_p, vec, g_mat, w_s, bs_full, w_router, b_router)

    blk = n_prompt // n_sample
    srow_map = lambda i: (blk, 0)
    any_spec = pl.BlockSpec(memory_space=pl.ANY)
    xmid, h2, logits, a_new, v_new = pl.pallas_call(
        _mixer_sample_kernel,
        grid=(1,),
        in_specs=[
            pl.BlockSpec((n_sample, D_MODEL), lambda i: (x_s_block, 0)),
            _const_spec((n_seq, N_MOD * D_MODEL)),
            *weight_specs,
            _const_spec((n_pos * n_pos, C_SGU)),
            _const_spec((n_pos, C_SGU)),
            *router_specs,
            _const_spec((CONV_STATE, n_seq, C_CONV)),
            any_spec, any_spec, any_spec,
        ],
        out_specs=(
            pl.BlockSpec((n_sample, D_MODEL), srow_map),
            pl.BlockSpec((n_sample, D_WORDS), srow_map),
            pl.BlockSpec((n_sample, N_EXPERTS), srow_map),
            _const_spec((n_sample, C_CONV)),
            _const_spec((n_sample, C_SGU)),
        ),
        out_shape=out_shapes + (
            jax.ShapeDtypeStruct((n_sample, C_CONV), F32),
            jax.ShapeDtypeStruct((n_sample, C_SGU), F32),
        ),
        input_output_aliases={12: 0, 13: 1, 14: 2},
        scratch_shapes=common_scratch(n_sample) + tail_scratch(n_sample, F32),
        compiler_params=pltpu.CompilerParams(
            dimension_semantics=("arbitrary",), vmem_limit_bytes=VMEM_LIMIT),
        name="mixer_sample",
    )(x_s, mod_s, w_in_b, w_out_b, wconv_p, vec, g_mat, wsv, bsv, w_router, b_router,
      state_t, xmid, h2, logits)
    return xmid, h2, logits, cst, a_new, v_new


def _router_kernel(logit_ref, eid_ref, wgt_ref, rank_ref, cnt_ref, cnt_scr):
    i = pl.program_id(0)
    n = logit_ref.shape[0]

    @pl.when(i == 0)
    def _():
        cnt_scr[...] = jnp.zeros_like(cnt_scr)

    lg = logit_ref[...]
    lane = lax.broadcasted_iota(jnp.int32, lg.shape, 1).astype(F32)
    vals, sels = [], []
    for k in range(TOP_K):
        m = jnp.max(lg, axis=-1, keepdims=True)
        idx = jnp.min(jnp.where(lg == m, lane, float(N_EXPERTS)), axis=-1, keepdims=True)
        sel = lane == idx
        eid_ref[:, k:k + 1] = idx.astype(jnp.int32)
        vals.append(m)
        sels.append(sel)
        lg = jnp.where(sel, -jnp.inf, lg)
    exps = [jnp.exp(v - vals[0]) for v in vals]
    inv = 1.0 / (exps[0] + exps[1] + exps[2] + exps[3])
    for k in range(TOP_K):
        wgt_ref[:, k:k + 1] = exps[k] * inv

    onehot = jnp.zeros(lg.shape, F32)
    for sel in sels:
        onehot = onehot + sel.astype(F32)
    row_i = lax.broadcasted_iota(jnp.int32, (n, n), 0)
    col_i = lax.broadcasted_iota(jnp.int32, (n, n), 1)
    before = jnp.where(col_i < row_i, 1.0, 0.0).astype(BF16)
    prefix = _dot(before, onehot.astype(BF16)) + cnt_scr[...]
    for k in range(TOP_K):
        rk = jnp.sum(jnp.where(sels[k], prefix, 0.0), axis=-1, keepdims=True)
        rank_ref[:, k:k + 1] = rk.astype(jnp.int32)
    total = cnt_scr[...] + jnp.sum(onehot, axis=0, keepdims=True)
    cnt_scr[...] = total
    cnt_ref[...] = total.astype(jnp.int32)


def _router(logits):
    t_all = logits.shape[0]
    row_spec = lambda w: pl.BlockSpec((ROW_TILE, w), lambda i: (i, 0))
    return pl.pallas_call(
        _router_kernel,
        grid=(t_all // ROW_TILE,),
        in_specs=[row_spec(N_EXPERTS)],
        out_specs=(row_spec(TOP_K), row_spec(TOP_K), row_spec(TOP_K), _const_spec((1, N_EXPERTS))),
        out_shape=(
            jax.ShapeDtypeStruct((t_all, TOP_K), jnp.int32),
            jax.ShapeDtypeStruct((t_all, TOP_K), F32),
            jax.ShapeDtypeStruct((t_all, TOP_K), jnp.int32),
            jax.ShapeDtypeStruct((1, N_EXPERTS), jnp.int32),
        ),
        scratch_shapes=[pltpu.VMEM((1, N_EXPERTS), F32)],
        compiler_params=pltpu.CompilerParams(
            dimension_semantics=("arbitrary",), vmem_limit_bytes=VMEM_LIMIT),
        name="router",
    )(logits)


def _expert_kernel(te_ref, nv_ref, x_ref, w1_ref, b1_ref, w2_ref, b2_ref, o_ref,
                   w1b_scr, w2b_scr):
    g = pl.program_id(0)
    nv = nv_ref[g]
    prev = te_ref[jnp.maximum(g - 1, 0)]
    fresh = jnp.logical_or(g == 0, te_ref[g] != prev)

    @pl.when(jnp.logical_and(fresh, nv > 0))
    def _():
        w1b_scr[...] = w1_ref[0].astype(BF16)
        w2b_scr[...] = w2_ref[0].astype(BF16)

    @pl.when(nv > 0)
    def _():
        x_left, x_right = _unpack_bf16_pair(x_ref[...])
        xl = x_left.astype(BF16)
        xr = x_right.astype(BF16)

        def hidden(col0):
            cols = slice(col0, col0 + FF_BLOCK)
            return (_dot(xl, w1b_scr[0:D_WORDS, cols]) + _dot(xr, w1b_scr[D_WORDS:D_MODEL, cols])
                    + b1_ref[0, :, cols])

        y = jnp.zeros((GEMM_TILE, D_MODEL), F32) + b2_ref[0]
        for jb in range(D_FF // FF_BLOCK):
            gg = jnp.minimum(hidden(jb * FF_BLOCK), SWIGLU_LIMIT)
            up = jnp.clip(hidden(D_FF + jb * FF_BLOCK), -SWIGLU_LIMIT, SWIGLU_LIMIT)
            act = gg * _sigmoid(SWIGLU_ALPHA * gg) * (up + 1.0)
            y = y + _dot(act.astype(BF16), w2b_scr[jb * FF_BLOCK:(jb + 1) * FF_BLOCK, :])
        row = lax.broadcasted_iota(jnp.int32, y.shape, 0)
        o_ref[...] = _pack_bf16_pair(jnp.where(row < nv, y, 0.0))

    @pl.when(nv == 0)
    def _():
        o_ref[...] = jnp.zeros_like(o_ref)


def _experts(xs, tile_e, tile_nv, w1, b1, w2, b2, layer):
    n_tiles = tile_e.shape[0]
    depth = w1.shape[0]
    grid_spec = pltpu.PrefetchScalarGridSpec(
        num_scalar_prefetch=2,
        grid=(n_tiles,),
        in_specs=[
            pl.BlockSpec((GEMM_TILE, D_WORDS), lambda g, te, nv: (g, 0)),
            pl.BlockSpec((None, 1, D_MODEL, 2 * D_FF), lambda g, te, nv: (layer, te[g], 0, 0)),
            pl.BlockSpec((None, 1, 1, 2 * D_FF), lambda g, te, nv: (layer, te[g], 0, 0)),
            pl.BlockSpec((None, 1, D_FF, D_MODEL), lambda g, te, nv: (layer, te[g], 0, 0)),
            pl.BlockSpec((None, 1, 1, D_MODEL), lambda g, te, nv: (layer, te[g], 0, 0)),
        ],
        out_specs=pl.BlockSpec((GEMM_TILE, D_WORDS), lambda g, te, nv: (g, 0)),
        scratch_shapes=[
            pltpu.VMEM((D_MODEL, 2 * D_FF), BF16),
            pltpu.VMEM((D_FF, D_MODEL), BF16),
        ],
    )
    return pl.pallas_call(
        _expert_kernel,
        grid_spec=grid_spec,
        out_shape=jax.ShapeDtypeStruct((n_tiles * GEMM_TILE, D_WORDS), jnp.int32),
        compiler_params=pltpu.CompilerParams(
            dimension_semantics=("arbitrary",), vmem_limit_bytes=VMEM_LIMIT),
        name="experts",
    )(tile_e, tile_nv, xs, w1, b1.reshape(depth, N_EXPERTS, 1, 2 * D_FF), w2,
      b2.reshape(depth, N_EXPERTS, 1, D_MODEL))


def _combine_kernel(x_ref, yg_ref, wgt_ref, gp_ref, gs_ref, fg_ref, *out_refs, n_prompt_blocks, final):
    i = pl.program_id(0)
    n_seq = gs_ref.shape[0]
    op_ref, os_ref = (out_refs[0], out_refs[-1])

    def run(gate, o_ref):
        def body(r0):
            r = pl.ds(r0, COMBINE_CHUNK)
            w = wgt_ref[r, :]
            g = gate(r0)

            def half(hi_half):
                cols = slice(D_WORDS, D_MODEL) if hi_half else slice(0, D_WORDS)
                moe = jnp.zeros((COMBINE_CHUNK, D_WORDS), F32)
                for k in range(TOP_K):
                    word = yg_ref[k, r, :]
                    bits = (word & HI_MASK) if hi_half else lax.shift_left(word, 16)
                    moe = moe + w[:, k:k + 1] * lax.bitcast_convert_type(bits, F32)
                return x_ref[r, cols] + g[:, cols] * moe
            xl = half(False)
            xr = half(True)
            if final:
                ms = (jnp.sum(xl * xl, axis=-1, keepdims=True)
                      + jnp.sum(xr * xr, axis=-1, keepdims=True)) * (1.0 / D_MODEL)
                scale = lax.rsqrt(ms + EPS)
                xl = xl * scale * fg_ref[:, 0:D_WORDS]
                xr = xr * scale * fg_ref[:, D_WORDS:D_MODEL]
            o_ref[r, 0:D_WORDS] = xl
            o_ref[r, D_WORDS:D_MODEL] = xr
        _row_loop(x_ref.shape[0], body, COMBINE_CHUNK)

    @pl.when(i < n_prompt_blocks)
    def _():
        run(lambda r0: gp_ref[0], op_ref)

    @pl.when(i >= n_prompt_blocks)
    def _():
        run(lambda r0: gs_ref[pl.ds(pl.multiple_of(lax.rem(r0, n_seq), COMBINE_CHUNK), COMBINE_CHUNK), :],
            os_ref)


def _combine(xmid, yg, wgt, gate_p, gate_s, final_g, n_prompt, seq_len, final):
    t_all = xmid.shape[0]
    n_prompt_blocks = n_prompt // ROW_TILE
    per_seq = seq_len // ROW_TILE
    n_seq_p = gate_p.shape[0]
    kern = functools.partial(_combine_kernel, n_prompt_blocks=n_prompt_blocks, final=final)
    if final:
        assert t_all - n_prompt == ROW_TILE
        out_specs = (
            pl.BlockSpec((ROW_TILE, D_MODEL), lambda i: (jnp.minimum(i, n_prompt_blocks - 1), 0)),
            pl.BlockSpec((ROW_TILE, D_MODEL), lambda i: (0, 0)),
        )
        out_shape = (jax.ShapeDtypeStruct((n_prompt, D_MODEL), F32),
                     jax.ShapeDtypeStruct((ROW_TILE, D_MODEL), F32))
    else:
        out_specs = pl.BlockSpec((ROW_TILE, D_MODEL), lambda i: (i, 0))
        out_shape = jax.ShapeDtypeStruct((t_all, D_MODEL), F32)
    return pl.pallas_call(
        kern,
        grid=(t_all // ROW_TILE,),
        in_specs=[
            pl.BlockSpec((ROW_TILE, D_MODEL), lambda i: (i, 0)),
            pl.BlockSpec((TOP_K, ROW_TILE, D_WORDS), lambda i: (0, i, 0)),
            pl.BlockSpec((ROW_TILE, TOP_K), lambda i: (i, 0)),
            pl.BlockSpec((1, 1, D_MODEL), lambda i: (jnp.minimum(i // per_seq, n_seq_p - 1), 0, 0)),
            _const_spec(gate_s.shape),
            _const_spec((1, D_MODEL)),
        ],
        out_specs=out_specs,
        out_shape=out_shape,
        compiler_params=pltpu.CompilerParams(
            dimension_semantics=("arbitrary",), vmem_limit_bytes=VMEM_LIMIT),
        name="combine",
    )(xmid, yg, wgt, gate_p.reshape(n_seq_p, 1, D_MODEL), gate_s, final_g.reshape(1, D_MODEL))


def _sc_mesh():
    return plsc.VectorSubcoreMesh(core_axis_name="core", subcore_axis_name="subcore",
                                  num_cores=SC_CORES, num_subcores=SC_SUBCORES)


def _sc_worker_id():
    return lax.axis_index("subcore") * SC_CORES + lax.axis_index("core")


def _sc_dispatch(h2w, pos_c, n_slots):
    n_chunks = pos_c.shape[1]

    @functools.partial(
        pl.kernel, mesh=_sc_mesh(),
        out_type=jax.ShapeDtypeStruct((n_slots, D_WORDS), jnp.int32),
        scratch_types=[pltpu.VMEM((TOP_K, SC_CHUNK), jnp.int32),
                       pltpu.VMEM((SC_CHUNK, D_WORDS), jnp.int32),
                       pltpu.SemaphoreType.DMA],
        name="moe_dispatch")
    def run(h_hbm, pos_hbm, out_hbm, idx_v, rows_v, sem):
        wid = _sc_worker_id()

        @pl.loop(0, n_chunks)
        def _(c):
            base = pl.multiple_of((wid * n_chunks + c) * SC_CHUNK, SUBLANES)
            pltpu.sync_copy(pos_hbm.at[wid, c], idx_v)
            pltpu.sync_copy(h_hbm.at[pl.ds(base, SC_CHUNK)], rows_v)
            copies = [pltpu.async_copy(rows_v, out_hbm.at[idx_v.at[k]], sem) for k in range(TOP_K)]
            for cp in copies:
                cp.wait()

    return run(h2w, pos_c)


def _sc_gather_back(ys, pos_c, t_all):
    n_chunks = pos_c.shape[1]

    @functools.partial(
        pl.kernel, mesh=_sc_mesh(),
        out_type=jax.ShapeDtypeStruct((TOP_K, t_all, D_WORDS), jnp.int32),
        scratch_types=[pltpu.VMEM((TOP_K, SC_CHUNK), jnp.int32),
                       pltpu.VMEM((TOP_K, SC_CHUNK, D_WORDS), jnp.int32),
                       pltpu.SemaphoreType.DMA],
        name="moe_gather_back")
    def run(ys_hbm, pos_hbm, out_hbm, idx_v, rows_v, sem):
        wid = _sc_worker_id()

        @pl.loop(0, n_chunks)
        def _(c):
            base = pl.multiple_of((wid * n_chunks + c) * SC_CHUNK, SUBLANES)
            pltpu.sync_copy(pos_hbm.at[wid, c], idx_v)
            copies = [pltpu.async_copy(ys_hbm.at[idx_v.at[k]], rows_v.at[k], sem) for k in range(TOP_K)]
            for cp in copies:
                cp.wait()
            for k in range(TOP_K):
                pltpu.sync_copy(rows_v.at[k], out_hbm.at[k, pl.ds(base, SC_CHUNK)])

    return run(ys, pos_c)


def _dispatch_plan(eid, rank, counts, n_tiles):
    counts = counts.reshape(N_EXPERTS)
    tiles_per = (counts + GEMM_TILE - 1) // GEMM_TILE
    tile_end = jnp.cumsum(tiles_per)
    tile_start = tile_end - tiles_per
    pos = (tile_start * GEMM_TILE)[eid] + rank
    g = jnp.arange(n_tiles, dtype=jnp.int32)
    te = jnp.sum((tile_end[None, :] <= g[:, None]).astype(jnp.int32), axis=1)
    active = te < N_EXPERTS
    te_c = jnp.minimum(te, N_EXPERTS - 1)
    nv = jnp.clip(counts[te_c] - (g - tile_start[te_c]) * GEMM_TILE, 0, GEMM_TILE)
    nv = jnp.where(active, nv, 0).astype(jnp.int32)
    last_e = te_c[jnp.maximum(jnp.sum(active.astype(jnp.int32)) - 1, 0)]
    te_c = jnp.where(active, te_c, last_e).astype(jnp.int32)
    return pos.astype(jnp.int32), te_c, nv


def kernel(x_prompt, x_sample, state_conv, c_prompt, c_sample, w_ada, b_ada, w_in, w_conv, b_conv,
           gn_g, gn_b, sgu_ln_g, sgu_ln_b, w_s, b_s, beta_a, beta_b, w_out, w_router, b_router,
           w1, b1, w2, b2, final_g):
    n_bp, seq_len, _ = x_prompt.shape
    n_bs, n_pos, _ = x_sample.shape
    depth = w_ada.shape[0]
    n_prompt = n_bp * seq_len
    n_sample = n_bs * n_pos
    t_all = n_prompt + n_sample
    n_tiles = (t_all * TOP_K) // GEMM_TILE + N_EXPERTS
    n_workers = SC_CORES * SC_SUBCORES
    assert t_all % (n_workers * SC_CHUNK) == 0

    x_p = x_prompt.reshape(n_prompt, D_MODEL)
    x_s = x_sample.transpose(1, 0, 2).reshape(n_sample, D_MODEL)
    x_s_block = 0
    mods = _adaln(jnp.concatenate([c_prompt, c_sample], axis=0), w_ada, b_ada)
    state_t = state_conv.transpose(0, 2, 1, 3)
    grp = jnp.arange(C_CONV) // (C_CONV // CONV_GROUPS)
    g_mat = ((grp[:, None] == grp[None, :]).astype(F32) / (C_CONV // CONV_GROUPS)).astype(BF16)

    conv_p, conv_s, v_s = [], [], []
    for l in range(depth):
        mod_p = mods[l, :n_bp].reshape(n_bp, 1, N_MOD * D_MODEL)
        mod_s = mods[l, n_bp:]
        wconv_p = jnp.pad(w_conv[l], ((0, CARRY_ROWS - CONV_WIDTH), (0, 0)))
        zero = jnp.zeros_like(b_conv[l])
        vec = jnp.stack([b_conv[l], gn_g[l], gn_b[l], sgu_ln_g[l], sgu_ln_b[l],
                         beta_a[l], beta_b[l], zero], axis=0)
        bs_full = jnp.repeat(b_s[l].T, SGU_HEAD_DIM, axis=1)
        wsv = jnp.repeat(w_s[l][:, :n_pos, :n_pos].transpose(1, 2, 0).reshape(n_pos * n_pos, SGU_HEADS),
                         SGU_HEAD_DIM, axis=1)
        bsv = jnp.repeat(b_s[l][:, :n_pos].T, SGU_HEAD_DIM, axis=1)
        xmid, h2, logits, cst, a_new, v_new = _mixer(
            x_p, x_s, x_s_block, mod_p, mod_s, w_in[l].astype(BF16), w_out[l].astype(BF16), wconv_p, vec, g_mat,
            w_s[l], bs_full, wsv, bsv, w_router[l], b_router[l].reshape(1, N_EXPERTS),
            state_t[l], n_bp, seq_len)
        conv_p.append(cst)
        a_new = a_new.reshape(n_pos, n_bs, C_CONV).transpose(1, 0, 2)
        conv_s.append(jnp.concatenate([state_conv[l], a_new], axis=1)[:, -CONV_STATE:])
        v_s.append(v_new.reshape(n_pos, n_bs, C_SGU).transpose(1, 0, 2))

        eid, wgt, rank, counts = _router(logits)
        pos, tile_e, tile_nv = _dispatch_plan(eid, rank, counts, n_tiles)
        pos_c = pos.reshape(n_workers, -1, SC_CHUNK, TOP_K).transpose(0, 1, 3, 2)
        xs = _sc_dispatch(h2, pos_c, n_tiles * GEMM_TILE)
        ys = _experts(xs, tile_e, tile_nv, w1, b1, w2, b2, l)
        yg = _sc_gather_back(ys, pos_c, t_all)
        out = _combine(xmid, yg, wgt, mod_p[:, 0, 5 * D_MODEL:], mod_s[:, 5 * D_MODEL:],
                       final_g, n_prompt, seq_len, final=(l == depth - 1))
        if l < depth - 1:
            x_p = x_s = out
            x_s_block = n_prompt // n_sample

    y_prompt = out[0].reshape(n_bp, seq_len, D_MODEL)
    y_sample = out[1].reshape(n_pos, n_bs, D_MODEL).transpose(1, 0, 2)
    return (y_prompt, y_sample, jnp.stack(conv_p, axis=0), jnp.stack(conv_s, axis=0),
            jnp.stack(v_s, axis=0))
```

```python
import functools
import math

import jax
import jax.numpy as jnp
from jax import lax
from jax.experimental import pallas as pl
from jax.experimental.pallas import tpu as pltpu
from jax.experimental.pallas import tpu_sc as plsc

F32 = jnp.float32
BF16 = jnp.bfloat16

D_MODEL = 1024
C_CONV = 512
C_SGU = 512
CONV_GROUPS = 8
CONV_WIDTH = 31
CONV_STATE = CONV_WIDTH - 1
SGU_HEADS = 4
SGU_HEAD_DIM = C_SGU // SGU_HEADS
CHUNK = 128
N_EXPERTS = 32
TOP_K = 4
D_FF = 1024
SWIGLU_LIMIT = 7.0
SWIGLU_ALPHA = 1.702
N_MOD = 6
EPS = 1e-5

ROW_TILE = 512
ROW_CHUNK = 128
CONV_CHUNK = 32
COMBINE_CHUNK = 64
CARRY_ROWS = 32
SUBLANES = 8
LANES = 128
PLAN_ROWS = 8
PLAN_LANES = 256
PLAN_EXPERT, PLAN_VALID, PLAN_FIRST, PLAN_SLOT, PLAN_NEXT = range(5)
GEMM_TILE = 512
FF_BLOCK = 256
VMEM_LIMIT = 56 * 1024 * 1024
D_WORDS = D_MODEL // 2
SC_CHUNK = 48
HI_MASK = -65536
SC_CORES = 2
SC_SUBCORES = 16


def _rms(x):
    return x * lax.rsqrt(jnp.mean(x * x, axis=-1, keepdims=True) + EPS)


def _gelu(x):
    return 0.5 * x * (1.0 + lax.erf(x * (1.0 / math.sqrt(2.0))))


def _sigmoid(x):
    return 1.0 / (1.0 + jnp.exp(-x))


def _split_bf16(x):
    hi = x.astype(BF16)
    lo = (x - hi.astype(F32)).astype(BF16)
    return hi, lo


def _pack_bf16_pair(x):
    bits = lax.bitcast_convert_type(x.astype(BF16).astype(F32), jnp.int32)
    return lax.shift_right_logical(bits[:, :D_WORDS], 16) | (bits[:, D_WORDS:] & HI_MASK)


def _unpack_bf16_pair(w):
    left = lax.bitcast_convert_type(lax.shift_left(w, 16), F32)
    right = lax.bitcast_convert_type(w & HI_MASK, F32)
    return left, right


def _dot(a, b):
    return jnp.dot(a, b, preferred_element_type=F32)


def _dot_nt(a, b):
    return lax.dot_general(a, b, (((1,), (1,)), ((), ())), preferred_element_type=F32)


def _row_loop(n_rows, body, chunk=None):
    chunk = ROW_CHUNK if chunk is None else chunk

    def step(i, carry):
        body(pl.multiple_of(i * chunk, chunk))
        return carry
    lax.fori_loop(0, n_rows // chunk, step, 0)


def _adaln_kernel(c_ref, w_ref, b_ref, o_ref):
    c = c_ref[...]
    s_hi, s_lo = _split_bf16(c * _sigmoid(c))
    w_hi, w_lo = _split_bf16(w_ref[0])
    acc = _dot(s_hi, w_hi) + _dot(s_hi, w_lo) + _dot(s_lo, w_hi)
    o_ref[0] = acc + b_ref[0]


def _adaln(c_all, w_ada, b_ada):
    depth, _, n_out = w_ada.shape
    n_rows = c_all.shape[0]
    tn = 1024
    return pl.pallas_call(
        _adaln_kernel,
        grid=(depth, n_out // tn),
        in_specs=[
            pl.BlockSpec((n_rows, D_MODEL), lambda l, j: (0, 0)),
            pl.BlockSpec((1, D_MODEL, tn), lambda l, j: (l, 0, j)),
            pl.BlockSpec((1, 1, tn), lambda l, j: (l, 0, j)),
        ],
        out_specs=pl.BlockSpec((1, n_rows, tn), lambda l, j: (l, 0, j)),
        out_shape=jax.ShapeDtypeStruct((depth, n_rows, n_out), F32),
        compiler_params=pltpu.CompilerParams(
            dimension_semantics=("arbitrary", "arbitrary"), vmem_limit_bytes=VMEM_LIMIT),
        name="adaln",
    )(c_all, w_ada, b_ada.reshape(depth, 1, n_out))


def _phase_in_norm(x_ref, h_scr, mod, n_rows):
    def body(r0):
        r = pl.ds(r0, ROW_CHUNK)
        h = _rms(x_ref[r, :]) * (1.0 + mod(1, r0)) + mod(0, r0)
        h_scr[r, :] = h.astype(BF16)
    _row_loop(n_rows, body)


def _phase_gates(z_scr, vec_ref, a_dst, a_off, u_scr, v_scr, vout_ref, n_rows):
    ln_g = vec_ref[3:4, :]
    ln_b = vec_ref[4:5, :]

    def body(r0):
        r = pl.ds(r0, ROW_CHUNK)
        a = z_scr[r, 0:C_CONV] * _sigmoid(z_scr[r, C_CONV:2 * C_CONV])
        a_dst[pl.ds(r0 + a_off, ROW_CHUNK), :] = a
        u_scr[r, :] = _gelu(z_scr[r, 2 * C_CONV:2 * C_CONV + C_SGU])
        gv = _gelu(z_scr[r, 2 * C_CONV + C_SGU:2 * C_CONV + 2 * C_SGU])
        mu = jnp.mean(gv, axis=-1, keepdims=True)
        dv = gv - mu
        var = jnp.mean(dv * dv, axis=-1, keepdims=True)
        v = dv * lax.rsqrt(var + EPS) * ln_g + ln_b
        if vout_ref is not None:
            vout_ref[r, :] = v
        v_scr[r, :] = v.astype(v_scr.dtype)
    _row_loop(n_rows, body)


def _phase_group_norm(conv_scr, convb_scr, stat_scr, g_ref, vec_ref, y_scr, n_rows):
    stat_scr[...] = _dot(convb_scr[...], g_ref[...])

    def center(r0):
        r = pl.ds(r0, ROW_CHUNK)
        d = conv_scr[r, :] - stat_scr[r, :]
        conv_scr[r, :] = d
        convb_scr[r, :] = (d * d).astype(BF16)
    _row_loop(n_rows, center)
    stat_scr[...] = _dot(convb_scr[...], g_ref[...])
    gn_g = vec_ref[1:2, :]
    gn_b = vec_ref[2:3, :]
    beta_a = vec_ref[5:6, :]

    def finish(r0):
        r = pl.ds(r0, ROW_CHUNK)
        gn = conv_scr[r, :] * lax.rsqrt(stat_scr[r, :] + EPS) * gn_g + gn_b
        ya = gn * _sigmoid(gn)
        y_scr[r, 0:C_CONV] = (_rms(ya) * beta_a).astype(BF16)
    _row_loop(n_rows, finish)


def _phase_out(x_ref, y_scr, wout_ref, z_scr, wr_ref, br_ref, mod, h_scr,
               xmid_ref, h2_ref, logit_ref, n_rows):
    z_scr[:, 0:D_MODEL] = _dot(y_scr[...], wout_ref[...])

    def body(r0):
        r = pl.ds(r0, ROW_CHUNK)
        xm = x_ref[r, :] + mod(2, r0) * z_scr[r, 0:D_MODEL]
        xmid_ref[r, :] = xm
        h2 = _rms(xm) * (1.0 + mod(4, r0)) + mod(3, r0)
        hi, lo = _split_bf16(h2)
        h2_ref[r, :] = _pack_bf16_pair(h2)
        y_scr[r, :] = hi
        h_scr[r, :] = lo
    _row_loop(n_rows, body)
    w_hi, w_lo = _split_bf16(wr_ref[...])
    hi = y_scr[...]
    logit_ref[...] = (_dot_nt(w_hi, hi) + _dot_nt(w_lo, hi) + _dot_nt(w_hi, h_scr[...])) + br_ref[...]


def _mixer_prompt_kernel(x_ref, mod_ref, win_ref, wout_ref, wconv_ref, vec_ref, g_ref,
                         ws_ref, bs_ref, wr_ref, br_ref,
                         xmid_ref, h2_ref, logit_ref, cst_ref,
                         h_scr, z_scr, aext_scr, conv_scr, convb_scr, stat_scr,
                         u_scr, v_scr, y_scr):
    j = pl.program_id(1)
    n_rows = ROW_TILE

    def mod(idx, r0):
        del r0
        return mod_ref[0, :, idx * D_MODEL:(idx + 1) * D_MODEL]

    a_buf = aext_scr.at[0]

    @pl.when(j == 0)
    def _():
        a_buf[0:CARRY_ROWS, :] = jnp.zeros((CARRY_ROWS, C_CONV), F32)

    _phase_in_norm(x_ref, h_scr, mod, n_rows)
    z_scr[...] = _dot(h_scr[...], win_ref[...])
    _phase_gates(z_scr, vec_ref, a_buf, CARRY_ROWS, u_scr, v_scr, None, n_rows)
    n_shift = n_rows + CARRY_ROWS - SUBLANES
    for s in range(1, SUBLANES):
        aext_scr[s, 0:n_shift, :] = a_buf[s:s + n_shift, :]

    b_conv = vec_ref[0:1, :]
    lead = CARRY_ROWS - CONV_STATE

    def conv_body(r0):
        acc = jnp.zeros((CONV_CHUNK, C_CONV), F32)
        for k in range(CONV_WIDTH):
            s = (lead + k) % SUBLANES
            acc = acc + wconv_ref[k:k + 1, :] * aext_scr[s, pl.ds(r0 + (lead + k - s), CONV_CHUNK), :]
        conv = acc + b_conv
        r = pl.ds(r0, CONV_CHUNK)
        conv_scr[r, :] = conv
        convb_scr[r, :] = conv.astype(BF16)
    _row_loop(n_rows, conv_body, CONV_CHUNK)

    @pl.when(j == pl.num_programs(1) - 1)
    def _():
        cst_ref[0] = a_buf[n_rows + lead:n_rows + CARRY_ROWS, :]

    a_buf[0:CARRY_ROWS, :] = a_buf[n_rows:n_rows + CARRY_ROWS, :]

    _phase_group_norm(conv_scr, convb_scr, stat_scr, g_ref, vec_ref, y_scr, n_rows)

    row_i = lax.broadcasted_iota(jnp.int32, (CHUNK, CHUNK), 0)
    col_i = lax.broadcasted_iota(jnp.int32, (CHUNK, CHUNK), 1)
    tril = (col_i <= row_i).astype(F32)
    for hd in range(SGU_HEADS):
        ws_h = (ws_ref[hd] * tril).astype(BF16)
        cols = slice(hd * SGU_HEAD_DIM, (hd + 1) * SGU_HEAD_DIM)
        for c in range(n_rows // CHUNK):
            rows = slice(c * CHUNK, (c + 1) * CHUNK)
            stat_scr[rows, cols] = _dot(ws_h, v_scr[rows, cols])
    beta_b = vec_ref[6:7, :]

    def gate_body(r0):
        r = pl.ds(r0, ROW_CHUNK)
        rb = pl.ds(pl.multiple_of(lax.rem(r0, CHUNK), ROW_CHUNK), ROW_CHUNK)
        yb = u_scr[r, :] * (stat_scr[r, :] + bs_ref[rb, :])
        y_scr[r, C_CONV:C_CONV + C_SGU] = (_rms(yb) * beta_b).astype(BF16)
    _row_loop(n_rows, gate_body)

    _phase_out(x_ref, y_scr, wout_ref, z_scr, wr_ref, br_ref, mod, h_scr,
               xmid_ref, h2_ref, logit_ref, n_rows)


def _mixer_sample_kernel(x_ref, mod_ref, win_ref, wout_ref, wconv_ref, vec_ref, g_ref,
                         wsv_ref, bsv_ref, wr_ref, br_ref, state_ref,
                         xmid_in, h2_in, logit_in,
                         xmid_ref, h2_ref, logit_ref, anew_ref, vout_ref,
                         h_scr, z_scr, conv_scr, convb_scr, stat_scr, u_scr, v_scr, y_scr):
    del xmid_in, h2_in, logit_in
    n_rows = x_ref.shape[0]
    n_seq = state_ref.shape[1]
    n_pos = n_rows // n_seq

    def mod(idx, r0):
        rs = pl.ds(pl.multiple_of(lax.rem(r0, n_seq), ROW_CHUNK), ROW_CHUNK)
        return mod_ref[rs, idx * D_MODEL:(idx + 1) * D_MODEL]

    _phase_in_norm(x_ref, h_scr, mod, n_rows)
    z_scr[...] = _dot(h_scr[...], win_ref[...])
    _phase_gates(z_scr, vec_ref, anew_ref, 0, u_scr, v_scr, vout_ref, n_rows)

    b_conv = vec_ref[0:1, :]

    def conv_body(s0):
        rs = pl.ds(s0, CONV_CHUNK)
        for t in range(n_pos):
            acc = jnp.zeros((CONV_CHUNK, C_CONV), F32)
            for k in range(CONV_WIDTH):
                p = t + k
                if p < CONV_STATE:
                    src = state_ref[p, rs, :]
                else:
                    src = anew_ref[pl.ds(s0 + (p - CONV_STATE) * n_seq, CONV_CHUNK), :]
                acc = acc + wconv_ref[k:k + 1, :] * src
            conv = acc + b_conv
            r = pl.ds(s0 + t * n_seq, CONV_CHUNK)
            conv_scr[r, :] = conv
            convb_scr[r, :] = conv.astype(BF16)
    _row_loop(n_seq, conv_body, CONV_CHUNK)

    _phase_group_norm(conv_scr, convb_scr, stat_scr, g_ref, vec_ref, y_scr, n_rows)

    beta_b = vec_ref[6:7, :]

    def gate_body(s0):
        for t in range(n_pos):
            s = jnp.zeros((CONV_CHUNK, C_SGU), F32)
            for jj in range(t + 1):
                s = s + wsv_ref[t * n_pos + jj:t * n_pos + jj + 1, :] * v_scr[pl.ds(s0 + jj * n_seq, CONV_CHUNK), :]
            r = pl.ds(s0 + t * n_seq, CONV_CHUNK)
            yb = u_scr[r, :] * (s + bsv_ref[t:t + 1, :])
            y_scr[r, C_CONV:C_CONV + C_SGU] = (_rms(yb) * beta_b).astype(BF16)
    _row_loop(n_seq, gate_body, CONV_CHUNK)

    _phase_out(x_ref, y_scr, wout_ref, z_scr, wr_ref, br_ref, mod, h_scr,
               xmid_ref, h2_ref, logit_ref, n_rows)


def _const_spec(shape):
    nd = len(shape)
    return pl.BlockSpec(shape, lambda *_: (0,) * nd)


def _mixer(x_p, x_s, x_s_block, mod_p, mod_s, w_in_b, w_out_b, wconv_p, vec, g_mat, w_s, bs_full,
           wsv, bsv, w_router, b_router, state_t, n_prompt_seq, seq_len):
    n_prompt = n_prompt_seq * seq_len
    n_seq, n_pos = state_t.shape[1], bsv.shape[0]
    n_sample = n_seq * n_pos
    t_all = n_prompt + n_sample
    nj = seq_len // ROW_TILE
    weight_specs = [
        _const_spec((D_MODEL, 2 * C_CONV + 2 * C_SGU)),
        _const_spec((D_MODEL, D_MODEL)),
        _const_spec((CARRY_ROWS, C_CONV)),
        _const_spec((8, C_CONV)),
        _const_spec((C_CONV, C_CONV)),
    ]
    router_specs = [_const_spec((N_EXPERTS, D_MODEL)), _const_spec((N_EXPERTS, 1))]
    common_scratch = lambda n: [
        pltpu.VMEM((n, D_MODEL), BF16),
        pltpu.VMEM((n, 2 * C_CONV + 2 * C_SGU), F32),
    ]
    tail_scratch = lambda n, vdt: [
        pltpu.VMEM((n, C_CONV), F32),
        pltpu.VMEM((n, C_CONV), BF16),
        pltpu.VMEM((n, C_CONV), F32),
        pltpu.VMEM((n, C_SGU), F32),
        pltpu.VMEM((n, C_SGU), vdt),
        pltpu.VMEM((n, D_MODEL), BF16),
    ]
    out_shapes = (
        jax.ShapeDtypeStruct((t_all, D_MODEL), F32),
        jax.ShapeDtypeStruct((t_all, D_WORDS), jnp.int32),
        jax.ShapeDtypeStruct((N_EXPERTS, t_all), F32),
    )
    row_map = lambda b, j: (b * nj + j, 0)
    xmid, h2, logits, cst = pl.pallas_call(
        _mixer_prompt_kernel,
        grid=(n_prompt_seq, nj),
        in_specs=[
            pl.BlockSpec((ROW_TILE, D_MODEL), row_map),
            pl.BlockSpec((1, 1, N_MOD * D_MODEL), lambda b, j: (b, 0, 0)),
            *weight_specs,
            _const_spec((SGU_HEADS, CHUNK, CHUNK)),
            _const_spec((CHUNK, C_SGU)),
            *router_specs,
        ],
        out_specs=(
            pl.BlockSpec((ROW_TILE, D_MODEL), row_map),
            pl.BlockSpec((ROW_TILE, D_WORDS), row_map),
            pl.BlockSpec((N_EXPERTS, ROW_TILE), lambda b, j: (0, b * nj + j)),
            pl.BlockSpec((1, CONV_STATE, C_CONV), lambda b, j: (b, 0, 0)),
        ),
        out_shape=out_shapes + (jax.ShapeDtypeStruct((n_prompt_seq, CONV_STATE, C_CONV), F32),),
        scratch_shapes=common_scratch(ROW_TILE)
        + [pltpu.VMEM((SUBLANES, CARRY_ROWS + ROW_TILE, C_CONV), F32)]
        + tail_scratch(ROW_TILE, BF16),
        compiler_params=pltpu.CompilerParams(
            dimension_semantics=("arbitrary", "arbitrary"), vmem_limit_bytes=VMEM_LIMIT),
        name="mixer_prompt",
    )(x_p, mod_p, w_in_b, w_out_b, wconv_p, vec, g_mat, w_s, bs_full, w_router, b_router)

    blk = n_prompt // n_sample
    srow_map = lambda i: (blk, 0)
    any_spec = pl.BlockSpec(memory_space=pl.ANY)
    xmid, h2, logits, a_new, v_new = pl.pallas_call(
        _mixer_sample_kernel,
        grid=(1,),
        in_specs=[
            pl.BlockSpec((n_sample, D_MODEL), lambda i: (x_s_block, 0)),
            _const_spec((n_seq, N_MOD * D_MODEL)),
            *weight_specs,
            _const_spec((n_pos * n_pos, C_SGU)),
            _const_spec((n_pos, C_SGU)),
            *router_specs,
            _const_spec((CONV_STATE, n_seq, C_CONV)),
            any_spec, any_spec, any_spec,
        ],
        out_specs=(
            pl.BlockSpec((n_sample, D_MODEL), srow_map),
            pl.BlockSpec((n_sample, D_WORDS), srow_map),
            pl.BlockSpec((N_EXPERTS, n_sample), lambda i: (0, blk)),
            _const_spec((n_sample, C_CONV)),
            _const_spec((n_sample, C_SGU)),
        ),
        out_shape=out_shapes + (
            jax.ShapeDtypeStruct((n_sample, C_CONV), F32),
            jax.ShapeDtypeStruct((n_sample, C_SGU), F32),
        ),
        input_output_aliases={12: 0, 13: 1, 14: 2},
        scratch_shapes=common_scratch(n_sample) + tail_scratch(n_sample, F32),
        compiler_params=pltpu.CompilerParams(
            dimension_semantics=("arbitrary",), vmem_limit_bytes=VMEM_LIMIT),
        name="mixer_sample",
    )(x_s, mod_s, w_in_b, w_out_b, wconv_p, vec, g_mat, wsv, bsv, w_router, b_router,
      state_t, xmid, h2, logits)
    return xmid, h2, logits, cst, a_new, v_new


def _tile_plan(cnt):
    e_sub = lax.broadcasted_iota(jnp.int32, (N_EXPERTS, N_EXPERTS), 0)
    e_lane = lax.broadcasted_iota(jnp.int32, (N_EXPERTS, N_EXPERTS), 1)
    tiles = jnp.floor((cnt + (GEMM_TILE - 1.0)) * (1.0 / GEMM_TILE))
    active = jnp.where(cnt > 0.0, 1.0, 0.0)

    def over_experts(tri, col):
        return _dot(tri.astype(BF16), jnp.broadcast_to(col, (N_EXPERTS, LANES)).astype(BF16))[:, 0:1]
    tile_end = over_experts(jnp.where(e_lane <= e_sub, 1.0, 0.0), tiles)
    tile_start = tile_end - tiles
    ordinal = over_experts(jnp.where(e_lane < e_sub, 1.0, 0.0), active)

    g = lax.broadcasted_iota(jnp.int32, (1, PLAN_LANES), 1).astype(F32)
    e_id = lax.broadcasted_iota(jnp.int32, (N_EXPERTS, PLAN_LANES), 0).astype(F32)
    done = jnp.sum(jnp.where(tile_end <= g, 1.0, 0.0), axis=0, keepdims=True)
    live = done < float(N_EXPERTS)
    last_e = jnp.max(jnp.where(cnt > 0.0, e_id[:, 0:1], 0.0), axis=0, keepdims=True)
    te = jnp.where(live, done, last_e)
    pick = e_id == te

    def at_tile(col):
        return jnp.sum(jnp.where(pick, col, 0.0), axis=0, keepdims=True)
    start_g = at_tile(tile_start)
    ord_g = at_tile(ordinal)
    valid = jnp.clip(at_tile(cnt) - (g - start_g) * GEMM_TILE, 0.0, float(GEMM_TILE))
    valid = jnp.where(live, valid, 0.0)
    first = jnp.where(live, jnp.where(g == start_g, 1.0, 0.0), 0.0)
    slot = ord_g - 2.0 * jnp.floor(ord_g * 0.5)
    later = jnp.where(e_id > te, jnp.where(cnt > 0.0, e_id, float(N_EXPERTS)), float(N_EXPERTS))
    nxt = jnp.min(later, axis=0, keepdims=True)
    nxt = jnp.where(nxt >= float(N_EXPERTS), -1.0, nxt)

    row = lax.broadcasted_iota(jnp.int32, (PLAN_ROWS, PLAN_LANES), 0)
    plan = jnp.zeros((PLAN_ROWS, PLAN_LANES), F32)
    for r, v in ((PLAN_EXPERT, te), (PLAN_VALID, valid), (PLAN_FIRST, first), (PLAN_SLOT, slot),
                 (PLAN_NEXT, nxt)):
        plan = jnp.where(row == r, v, plan)
    return plan.astype(jnp.int32), tile_start * GEMM_TILE


def _router_kernel(lg_ref, wgt_ref, pos_ref, plan_ref, cnt_scr, run_scr, start_scr):
    ph = pl.program_id(0)
    i = pl.program_id(1)
    n = lg_ref.shape[1]

    @pl.when(jnp.logical_and(ph == 0, i == 0))
    def _():
        cnt_scr[...] = jnp.zeros_like(cnt_scr)

    lg = lg_ref[...]
    sub = lax.broadcasted_iota(jnp.int32, lg.shape, 0).astype(F32)
    vals, sels = [], []
    for _ in range(TOP_K):
        m = jnp.max(lg, axis=0, keepdims=True)
        idx = jnp.min(jnp.where(lg == m, sub, float(N_EXPERTS)), axis=0, keepdims=True)
        sel = sub == idx
        vals.append(m)
        sels.append(sel)
        lg = jnp.where(sel, -jnp.inf, lg)
    onehot = jnp.zeros(lg.shape, F32)
    for sel in sels:
        onehot = onehot + jnp.where(sel, 1.0, 0.0)
    tile_cnt = jnp.sum(onehot, axis=1, keepdims=True)

    @pl.when(ph == 0)
    def _():
        cnt_scr[...] = cnt_scr[...] + tile_cnt

    @pl.when(jnp.logical_and(ph == 1, i == 0))
    def _():
        plan, row_start = _tile_plan(cnt_scr[...])
        plan_ref[...] = plan
        start_scr[...] = row_start
        run_scr[...] = jnp.zeros_like(run_scr)

    @pl.when(ph == 1)
    def _():
        exps = [jnp.exp(v - vals[0]) for v in vals]
        inv = 1.0 / (exps[0] + exps[1] + exps[2] + exps[3])
        w_row = lax.broadcasted_iota(jnp.int32, (LANES, n), 0)
        w_all = jnp.zeros((LANES, n), F32)
        for k in range(TOP_K):
            w_all = jnp.where(w_row == k, exps[k] * inv, w_all)
        wgt_ref[...] = w_all.T
        t_row = lax.broadcasted_iota(jnp.int32, (n, n), 0)
        t_col = lax.broadcasted_iota(jnp.int32, (n, n), 1)
        earlier = jnp.where(t_row < t_col, 1.0, 0.0).astype(BF16)
        slot0 = _dot(onehot.astype(BF16), earlier) + (run_scr[...] + start_scr[...])
        for k in range(TOP_K):
            pos_k = jnp.sum(jnp.where(sels[k], slot0, 0.0), axis=0, keepdims=True)
            pos_ref[k:k + 1, :] = pos_k.astype(jnp.int32)
        run_scr[...] = run_scr[...] + tile_cnt


def _router(logits):
    t_all = logits.shape[1]
    col = pltpu.VMEM((N_EXPERTS, 1), F32)
    return pl.pallas_call(
        _router_kernel,
        grid=(2, t_all // ROW_TILE),
        in_specs=[pl.BlockSpec((N_EXPERTS, ROW_TILE), lambda ph, i: (0, i))],
        out_specs=(
            pl.BlockSpec((ROW_TILE, LANES), lambda ph, i: (i * ph, 0)),
            pl.BlockSpec((TOP_K, ROW_TILE), lambda ph, i: (0, i * ph)),
            _const_spec((PLAN_ROWS, PLAN_LANES)),
        ),
        out_shape=(
            jax.ShapeDtypeStruct((t_all, LANES), F32),
            jax.ShapeDtypeStruct((TOP_K, t_all), jnp.int32),
            jax.ShapeDtypeStruct((PLAN_ROWS, PLAN_LANES), jnp.int32),
        ),
        scratch_shapes=[col, col, col],
        compiler_params=pltpu.CompilerParams(
            dimension_semantics=("arbitrary", "arbitrary"), vmem_limit_bytes=VMEM_LIMIT),
        name="router",
    )(logits)


def _expert_kernel(plan_ref, x_ref, w1_hbm, b1_ref, w2_hbm, b2_ref, o_ref,
                   w1f_scr, w2f_scr, w1b_scr, w2b_scr, sem, *, layer):
    g = pl.program_id(0)
    nv = plan_ref[PLAN_VALID, g]

    def weight_copies(expert, slot):
        return (pltpu.make_async_copy(w1_hbm.at[layer, expert], w1f_scr.at[slot], sem.at[0, slot]),
                pltpu.make_async_copy(w2_hbm.at[layer, expert], w2f_scr.at[slot], sem.at[1, slot]))

    @pl.when(plan_ref[PLAN_FIRST, g] == 1)
    def _():
        expert = plan_ref[PLAN_EXPERT, g]
        slot = plan_ref[PLAN_SLOT, g]
        nxt = plan_ref[PLAN_NEXT, g]

        @pl.when(g == 0)
        def _():
            for cp in weight_copies(expert, slot):
                cp.start()
        for cp in weight_copies(expert, slot):
            cp.wait()

        @pl.when(nxt >= 0)
        def _():
            for cp in weight_copies(nxt, 1 - slot):
                cp.start()
        w1b_scr[...] = w1f_scr[slot].astype(BF16)
        w2b_scr[...] = w2f_scr[slot].astype(BF16)

    @pl.when(nv > 0)
    def _():
        x_left, x_right = _unpack_bf16_pair(x_ref[...])
        xl = x_left.astype(BF16)
        xr = x_right.astype(BF16)

        def hidden(col0):
            cols = slice(col0, col0 + FF_BLOCK)
            return (_dot(xl, w1b_scr[0:D_WORDS, cols]) + _dot(xr, w1b_scr[D_WORDS:D_MODEL, cols])
                    + b1_ref[0, :, cols])

        y = jnp.zeros((GEMM_TILE, D_MODEL), F32) + b2_ref[0]
        for jb in range(D_FF // FF_BLOCK):
            gg = jnp.minimum(hidden(jb * FF_BLOCK), SWIGLU_LIMIT)
            up = jnp.clip(hidden(D_FF + jb * FF_BLOCK), -SWIGLU_LIMIT, SWIGLU_LIMIT)
            act = gg * _sigmoid(SWIGLU_ALPHA * gg) * (up + 1.0)
            y = y + _dot(act.astype(BF16), w2b_scr[jb * FF_BLOCK:(jb + 1) * FF_BLOCK, :])
        row = lax.broadcasted_iota(jnp.int32, y.shape, 0)
        o_ref[...] = _pack_bf16_pair(jnp.where(row < nv, y, 0.0))

    @pl.when(nv == 0)
    def _():
        o_ref[...] = jnp.zeros_like(o_ref)


def _experts(xs, plan, w1, b1, w2, b2, layer):
    n_tiles = xs.shape[0] // GEMM_TILE
    depth = w1.shape[0]
    bias_map = lambda g, plan: (layer, plan[PLAN_EXPERT, g], 0, 0)
    grid_spec = pltpu.PrefetchScalarGridSpec(
        num_scalar_prefetch=1,
        grid=(n_tiles,),
        in_specs=[
            pl.BlockSpec((GEMM_TILE, D_WORDS), lambda g, plan: (g, 0)),
            pl.BlockSpec(memory_space=pl.ANY),
            pl.BlockSpec((None, 1, 1, 2 * D_FF), bias_map),
            pl.BlockSpec(memory_space=pl.ANY),
            pl.BlockSpec((None, 1, 1, D_MODEL), bias_map),
        ],
        out_specs=pl.BlockSpec((GEMM_TILE, D_WORDS), lambda g, plan: (g, 0)),
        scratch_shapes=[
            pltpu.VMEM((2, D_MODEL, 2 * D_FF), F32),
            pltpu.VMEM((2, D_FF, D_MODEL), F32),
            pltpu.VMEM((D_MODEL, 2 * D_FF), BF16),
            pltpu.VMEM((D_FF, D_MODEL), BF16),
            pltpu.SemaphoreType.DMA((2, 2)),
        ],
    )
    return pl.pallas_call(
        functools.partial(_expert_kernel, layer=layer),
        grid_spec=grid_spec,
        out_shape=jax.ShapeDtypeStruct((n_tiles * GEMM_TILE, D_WORDS), jnp.int32),
        compiler_params=pltpu.CompilerParams(
            dimension_semantics=("arbitrary",), vmem_limit_bytes=VMEM_LIMIT),
        name="experts",
    )(plan, xs, w1, b1.reshape(depth, N_EXPERTS, 1, 2 * D_FF), w2,
      b2.reshape(depth, N_EXPERTS, 1, D_MODEL))


def _combine_kernel(x_ref, yg_ref, wgt_ref, gp_ref, gs_ref, fg_ref, *out_refs, n_prompt_blocks, final):
    i = pl.program_id(0)
    n_seq = gs_ref.shape[0]
    op_ref, os_ref = (out_refs[0], out_refs[-1])

    def run(gate, o_ref):
        def body(r0):
            r = pl.ds(r0, COMBINE_CHUNK)
            w = wgt_ref[r, :]
            g = gate(r0)

            def half(hi_half):
                cols = slice(D_WORDS, D_MODEL) if hi_half else slice(0, D_WORDS)
                moe = jnp.zeros((COMBINE_CHUNK, D_WORDS), F32)
                for k in range(TOP_K):
                    word = yg_ref[k, r, :]
                    bits = (word & HI_MASK) if hi_half else lax.shift_left(word, 16)
                    moe = moe + w[:, k:k + 1] * lax.bitcast_convert_type(bits, F32)
                return x_ref[r, cols] + g[:, cols] * moe
            xl = half(False)
            xr = half(True)
            if final:
                ms = (jnp.sum(xl * xl, axis=-1, keepdims=True)
                      + jnp.sum(xr * xr, axis=-1, keepdims=True)) * (1.0 / D_MODEL)
                scale = lax.rsqrt(ms + EPS)
                xl = xl * scale * fg_ref[:, 0:D_WORDS]
                xr = xr * scale * fg_ref[:, D_WORDS:D_MODEL]
            o_ref[r, 0:D_WORDS] = xl
            o_ref[r, D_WORDS:D_MODEL] = xr
        _row_loop(x_ref.shape[0], body, COMBINE_CHUNK)

    @pl.when(i < n_prompt_blocks)
    def _():
        run(lambda r0: gp_ref[0], op_ref)

    @pl.when(i >= n_prompt_blocks)
    def _():
        run(lambda r0: gs_ref[pl.ds(pl.multiple_of(lax.rem(r0, n_seq), COMBINE_CHUNK), COMBINE_CHUNK), :],
            os_ref)


def _combine(xmid, yg, wgt, gate_p, gate_s, final_g, n_prompt, seq_len, final):
    t_all = xmid.shape[0]
    n_prompt_blocks = n_prompt // ROW_TILE
    per_seq = seq_len // ROW_TILE
    n_seq_p = gate_p.shape[0]
    kern = functools.partial(_combine_kernel, n_prompt_blocks=n_prompt_blocks, final=final)
    if final:
        assert t_all - n_prompt == ROW_TILE
        out_specs = (
            pl.BlockSpec((ROW_TILE, D_MODEL), lambda i: (jnp.minimum(i, n_prompt_blocks - 1), 0)),
            pl.BlockSpec((ROW_TILE, D_MODEL), lambda i: (0, 0)),
        )
        out_shape = (jax.ShapeDtypeStruct((n_prompt, D_MODEL), F32),
                     jax.ShapeDtypeStruct((ROW_TILE, D_MODEL), F32))
    else:
        out_specs = pl.BlockSpec((ROW_TILE, D_MODEL), lambda i: (i, 0))
        out_shape = jax.ShapeDtypeStruct((t_all, D_MODEL), F32)
    return pl.pallas_call(
        kern,
        grid=(t_all // ROW_TILE,),
        in_specs=[
            pl.BlockSpec((ROW_TILE, D_MODEL), lambda i: (i, 0)),
            pl.BlockSpec((TOP_K, ROW_TILE, D_WORDS), lambda i: (0, i, 0)),
            pl.BlockSpec((ROW_TILE, LANES), lambda i: (i, 0)),
            pl.BlockSpec((1, 1, D_MODEL), lambda i: (jnp.minimum(i // per_seq, n_seq_p - 1), 0, 0)),
            _const_spec(gate_s.shape),
            _const_spec((1, D_MODEL)),
        ],
        out_specs=out_specs,
        out_shape=out_shape,
        compiler_params=pltpu.CompilerParams(
            dimension_semantics=("arbitrary",), vmem_limit_bytes=VMEM_LIMIT),
        name="combine",
    )(xmid, yg, wgt, gate_p.reshape(n_seq_p, 1, D_MODEL), gate_s, final_g.reshape(1, D_MODEL))


def _sc_mesh():
    return plsc.VectorSubcoreMesh(core_axis_name="core", subcore_axis_name="subcore",
                                  num_cores=SC_CORES, num_subcores=SC_SUBCORES)


def _sc_worker_id():
    return lax.axis_index("subcore") * SC_CORES + lax.axis_index("core")


def _sc_dispatch(h2w, pos_c, n_slots):
    n_chunks = pos_c.shape[1]

    @functools.partial(
        pl.kernel, mesh=_sc_mesh(),
        out_type=jax.ShapeDtypeStruct((n_slots, D_WORDS), jnp.int32),
        scratch_types=[pltpu.VMEM((TOP_K, SC_CHUNK), jnp.int32),
                       pltpu.VMEM((SC_CHUNK, D_WORDS), jnp.int32),
                       pltpu.SemaphoreType.DMA],
        name="moe_dispatch")
    def run(h_hbm, pos_hbm, out_hbm, idx_v, rows_v, sem):
        wid = _sc_worker_id()

        @pl.loop(0, n_chunks)
        def _(c):
            base = pl.multiple_of((wid * n_chunks + c) * SC_CHUNK, SUBLANES)
            pltpu.sync_copy(pos_hbm.at[wid, c], idx_v)
            pltpu.sync_copy(h_hbm.at[pl.ds(base, SC_CHUNK)], rows_v)
            copies = [pltpu.async_copy(rows_v, out_hbm.at[idx_v.at[k]], sem) for k in range(TOP_K)]
            for cp in copies:
                cp.wait()

    return run(h2w, pos_c)


def _sc_gather_back(ys, pos_c, t_all):
    n_chunks = pos_c.shape[1]

    @functools.partial(
        pl.kernel, mesh=_sc_mesh(),
        out_type=jax.ShapeDtypeStruct((TOP_K, t_all, D_WORDS), jnp.int32),
        scratch_types=[pltpu.VMEM((TOP_K, SC_CHUNK), jnp.int32),
                       pltpu.VMEM((TOP_K, SC_CHUNK, D_WORDS), jnp.int32),
                       pltpu.SemaphoreType.DMA],
        name="moe_gather_back")
    def run(ys_hbm, pos_hbm, out_hbm, idx_v, rows_v, sem):
        wid = _sc_worker_id()

        @pl.loop(0, n_chunks)
        def _(c):
            base = pl.multiple_of((wid * n_chunks + c) * SC_CHUNK, SUBLANES)
            pltpu.sync_copy(pos_hbm.at[wid, c], idx_v)
            copies = [pltpu.async_copy(ys_hbm.at[idx_v.at[k]], rows_v.at[k], sem) for k in range(TOP_K)]
            for cp in copies:
                cp.wait()
            for k in range(TOP_K):
                pltpu.sync_copy(rows_v.at[k], out_hbm.at[k, pl.ds(base, SC_CHUNK)])

    return run(ys, pos_c)


def kernel(x_prompt, x_sample, state_conv, c_prompt, c_sample, w_ada, b_ada, w_in, w_conv, b_conv,
           gn_g, gn_b, sgu_ln_g, sgu_ln_b, w_s, b_s, beta_a, beta_b, w_out, w_router, b_router,
           w1, b1, w2, b2, final_g):
    n_bp, seq_len, _ = x_prompt.shape
    n_bs, n_pos, _ = x_sample.shape
    depth = w_ada.shape[0]
    n_prompt = n_bp * seq_len
    n_sample = n_bs * n_pos
    t_all = n_prompt + n_sample
    n_tiles = (t_all * TOP_K) // GEMM_TILE + N_EXPERTS
    n_workers = SC_CORES * SC_SUBCORES
    assert t_all % (n_workers * SC_CHUNK) == 0 and n_tiles <= PLAN_LANES

    x_p = x_prompt.reshape(n_prompt, D_MODEL)
    x_s = x_sample.transpose(1, 0, 2).reshape(n_sample, D_MODEL)
    x_s_block = 0
    mods = _adaln(jnp.concatenate([c_prompt, c_sample], axis=0), w_ada, b_ada)
    state_t = state_conv.transpose(0, 2, 1, 3)
    grp = jnp.arange(C_CONV) // (C_CONV // CONV_GROUPS)
    g_mat = ((grp[:, None] == grp[None, :]).astype(F32) / (C_CONV // CONV_GROUPS)).astype(BF16)

    conv_p, conv_s, v_s = [], [], []
    for l in range(depth):
        mod_p = mods[l, :n_bp].reshape(n_bp, 1, N_MOD * D_MODEL)
        mod_s = mods[l, n_bp:]
        wconv_p = jnp.pad(w_conv[l], ((0, CARRY_ROWS - CONV_WIDTH), (0, 0)))
        zero = jnp.zeros_like(b_conv[l])
        vec = jnp.stack([b_conv[l], gn_g[l], gn_b[l], sgu_ln_g[l], sgu_ln_b[l],
                         beta_a[l], beta_b[l], zero], axis=0)
        bs_full = jnp.repeat(b_s[l].T, SGU_HEAD_DIM, axis=1)
        wsv = jnp.repeat(w_s[l][:, :n_pos, :n_pos].transpose(1, 2, 0).reshape(n_pos * n_pos, SGU_HEADS),
                         SGU_HEAD_DIM, axis=1)
        bsv = jnp.repeat(b_s[l][:, :n_pos].T, SGU_HEAD_DIM, axis=1)
        xmid, h2, logits, cst, a_new, v_new = _mixer(
            x_p, x_s, x_s_block, mod_p, mod_s, w_in[l].astype(BF16), w_out[l].astype(BF16), wconv_p, vec, g_mat,
            w_s[l], bs_full, wsv, bsv, w_router[l].T, b_router[l].reshape(N_EXPERTS, 1),
            state_t[l], n_bp, seq_len)
        conv_p.append(cst)
        a_new = a_new.reshape(n_pos, n_bs, C_CONV).transpose(1, 0, 2)
        conv_s.append(jnp.concatenate([state_conv[l], a_new], axis=1)[:, -CONV_STATE:])
        v_s.append(v_new.reshape(n_pos, n_bs, C_SGU).transpose(1, 0, 2))

        wgt, pos, plan = _router(logits)
        pos_c = pos.reshape(TOP_K, n_workers, -1, SC_CHUNK).transpose(1, 2, 0, 3)
        xs = _sc_dispatch(h2, pos_c, n_tiles * GEMM_TILE)
        ys = _experts(xs, plan, w1, b1, w2, b2, l)
        yg = _sc_gather_back(ys, pos_c, t_all)
        out = _combine(xmid, yg, wgt, mod_p[:, 0, 5 * D_MODEL:], mod_s[:, 5 * D_MODEL:],
                       final_g, n_prompt, seq_len, final=(l == depth - 1))
        if l < depth - 1:
            x_p = x_s = out
            x_s_block = n_prompt // n_sample

    y_prompt = out[0].reshape(n_bp, seq_len, D_MODEL)
    y_sample = out[1].reshape(n_pos, n_bs, D_MODEL).transpose(1, 0, 2)
    return (y_prompt, y_sample, jnp.stack(conv_p, axis=0), jnp.stack(conv_s, axis=0),
            jnp.stack(v_s, axis=0))
```

```python
import functools
import math

import jax
import jax.numpy as jnp
from jax import lax
from jax.experimental import pallas as pl
from jax.experimental.pallas import tpu as pltpu
from jax.experimental.pallas import tpu_sc as plsc

F32 = jnp.float32
BF16 = jnp.bfloat16

D_MODEL = 1024
C_CONV = 512
C_SGU = 512
CONV_GROUPS = 8
CONV_WIDTH = 31
CONV_STATE = CONV_WIDTH - 1
SGU_HEADS = 4
SGU_HEAD_DIM = C_SGU // SGU_HEADS
CHUNK = 128
N_EXPERTS = 32
TOP_K = 4
D_FF = 1024
SWIGLU_LIMIT = 7.0
SWIGLU_ALPHA = 1.702
N_MOD = 6
EPS = 1e-5

ROW_TILE = 512
ROUTER_TILE = 1536
ROW_CHUNK = 128
CONV_CHUNK = 32
COMBINE_CHUNK = 64
CARRY_ROWS = 32
SUBLANES = 8
LANES = 128
PLAN_ROWS = 8
PLAN_LANES = 256
PLAN_EXPERT, PLAN_VALID, PLAN_FIRST, PLAN_SLOT, PLAN_NEXT = range(5)
GEMM_TILE = 512
FF_BLOCK = 256
VMEM_LIMIT = 56 * 1024 * 1024
D_WORDS = D_MODEL // 2
SC_CHUNK = 48
HI_MASK = -65536
SC_CORES = 2
SC_SUBCORES = 16


def _rms(x):
    return x * lax.rsqrt(jnp.mean(x * x, axis=-1, keepdims=True) + EPS)


def _gelu(x):
    return 0.5 * x * (1.0 + lax.erf(x * (1.0 / math.sqrt(2.0))))


def _sigmoid(x):
    return 1.0 / (1.0 + jnp.exp(-x))


def _split_bf16(x):
    hi = x.astype(BF16)
    lo = (x - hi.astype(F32)).astype(BF16)
    return hi, lo


def _pack_bf16_pair(x):
    bits = lax.bitcast_convert_type(x.astype(BF16).astype(F32), jnp.int32)
    return lax.shift_right_logical(bits[:, :D_WORDS], 16) | (bits[:, D_WORDS:] & HI_MASK)


def _unpack_bf16_pair(w):
    left = lax.bitcast_convert_type(lax.shift_left(w, 16), F32)
    right = lax.bitcast_convert_type(w & HI_MASK, F32)
    return left, right


def _dot(a, b):
    return jnp.dot(a, b, preferred_element_type=F32)


def _dot_nt(a, b):
    return lax.dot_general(a, b, (((1,), (1,)), ((), ())), preferred_element_type=F32)


def _row_loop(n_rows, body, chunk=None):
    chunk = ROW_CHUNK if chunk is None else chunk

    def step(i, carry):
        body(pl.multiple_of(i * chunk, chunk))
        return carry
    lax.fori_loop(0, n_rows // chunk, step, 0)


def _adaln_kernel(c_ref, w_ref, b_ref, o_ref):
    c = c_ref[...]
    s_hi, s_lo = _split_bf16(c * _sigmoid(c))
    w_hi, w_lo = _split_bf16(w_ref[0])
    acc = _dot(s_hi, w_hi) + _dot(s_hi, w_lo) + _dot(s_lo, w_hi)
    o_ref[0] = acc + b_ref[0]


def _adaln(c_all, w_ada, b_ada):
    depth, _, n_out = w_ada.shape
    n_rows = c_all.shape[0]
    tn = 1024
    return pl.pallas_call(
        _adaln_kernel,
        grid=(depth, n_out // tn),
        in_specs=[
            pl.BlockSpec((n_rows, D_MODEL), lambda l, j: (0, 0)),
            pl.BlockSpec((1, D_MODEL, tn), lambda l, j: (l, 0, j)),
            pl.BlockSpec((1, 1, tn), lambda l, j: (l, 0, j)),
        ],
        out_specs=pl.BlockSpec((1, n_rows, tn), lambda l, j: (l, 0, j)),
        out_shape=jax.ShapeDtypeStruct((depth, n_rows, n_out), F32),
        compiler_params=pltpu.CompilerParams(
            dimension_semantics=("arbitrary", "arbitrary"), vmem_limit_bytes=VMEM_LIMIT),
        name="adaln",
    )(c_all, w_ada, b_ada.reshape(depth, 1, n_out))


def _phase_in_norm(x_ref, h_scr, mod, n_rows):
    def body(r0):
        r = pl.ds(r0, ROW_CHUNK)
        h = _rms(x_ref[r, :]) * (1.0 + mod(1, r0)) + mod(0, r0)
        h_scr[r, :] = h.astype(BF16)
    _row_loop(n_rows, body)


def _phase_gates(z_scr, vec_ref, a_dst, a_off, u_scr, v_scr, vout_ref, n_rows):
    ln_g = vec_ref[3:4, :]
    ln_b = vec_ref[4:5, :]

    def body(r0):
        r = pl.ds(r0, ROW_CHUNK)
        a = z_scr[r, 0:C_CONV] * _sigmoid(z_scr[r, C_CONV:2 * C_CONV])
        a_dst[pl.ds(r0 + a_off, ROW_CHUNK), :] = a
        u_scr[r, :] = _gelu(z_scr[r, 2 * C_CONV:2 * C_CONV + C_SGU])
        gv = _gelu(z_scr[r, 2 * C_CONV + C_SGU:2 * C_CONV + 2 * C_SGU])
        mu = jnp.mean(gv, axis=-1, keepdims=True)
        dv = gv - mu
        var = jnp.mean(dv * dv, axis=-1, keepdims=True)
        v = dv * lax.rsqrt(var + EPS) * ln_g + ln_b
        if vout_ref is not None:
            vout_ref[r, :] = v
        v_scr[r, :] = v.astype(v_scr.dtype)
    _row_loop(n_rows, body)


def _phase_group_norm(conv_scr, convb_scr, stat_scr, g_ref, vec_ref, y_scr, n_rows):
    stat_scr[...] = _dot(convb_scr[...], g_ref[...])

    def center(r0):
        r = pl.ds(r0, ROW_CHUNK)
        d = conv_scr[r, :] - stat_scr[r, :]
        conv_scr[r, :] = d
        convb_scr[r, :] = (d * d).astype(BF16)
    _row_loop(n_rows, center)
    stat_scr[...] = _dot(convb_scr[...], g_ref[...])
    gn_g = vec_ref[1:2, :]
    gn_b = vec_ref[2:3, :]
    beta_a = vec_ref[5:6, :]

    def finish(r0):
        r = pl.ds(r0, ROW_CHUNK)
        gn = conv_scr[r, :] * lax.rsqrt(stat_scr[r, :] + EPS) * gn_g + gn_b
        ya = gn * _sigmoid(gn)
        y_scr[r, 0:C_CONV] = (_rms(ya) * beta_a).astype(BF16)
    _row_loop(n_rows, finish)


def _phase_out(x_ref, y_scr, wout_ref, z_scr, wr_ref, br_ref, mod, h_scr,
               xmid_ref, h2_ref, logit_ref, n_rows):
    z_scr[:, 0:D_MODEL] = _dot(y_scr[...], wout_ref[...])

    def body(r0):
        r = pl.ds(r0, ROW_CHUNK)
        xm = x_ref[r, :] + mod(2, r0) * z_scr[r, 0:D_MODEL]
        xmid_ref[r, :] = xm
        h2 = _rms(xm) * (1.0 + mod(4, r0)) + mod(3, r0)
        hi, lo = _split_bf16(h2)
        h2_ref[r, :] = _pack_bf16_pair(h2)
        y_scr[r, :] = hi
        h_scr[r, :] = lo
    _row_loop(n_rows, body)
    w_hi, w_lo = _split_bf16(wr_ref[...])
    hi = y_scr[...]
    logit_ref[...] = (_dot_nt(w_hi, hi) + _dot_nt(w_lo, hi) + _dot_nt(w_hi, h_scr[...])) + br_ref[...]


def _moe_half(yg_ref, w, r, hi_half):
    moe = jnp.zeros((COMBINE_CHUNK, D_WORDS), F32)
    for k in range(TOP_K):
        word = yg_ref[k, r, :]
        bits = (word & HI_MASK) if hi_half else lax.shift_left(word, 16)
        moe = moe + w[:, k:k + 1] * lax.bitcast_convert_type(bits, F32)
    return moe


def _phase_moe_residual(xmid_ref, yg_ref, wgt_ref, gate, x_dst, n_rows):
    def body(r0):
        r = pl.ds(r0, COMBINE_CHUNK)
        w = wgt_ref[r, :]
        g = gate(r0)
        for hi_half in (False, True):
            cols = slice(D_WORDS, D_MODEL) if hi_half else slice(0, D_WORDS)
            x_dst[r, cols] = xmid_ref[r, cols] + g[:, cols] * _moe_half(yg_ref, w, r, hi_half)
    _row_loop(n_rows, body, COMBINE_CHUNK)


def _mixer_prompt_kernel(*refs, tiles_per_seq, n_prompt_tiles, fused):
    x_ref, mod_ref = refs[:2]
    refs = refs[2:]
    if fused:
        yg_ref, wgt_ref, gate_ref = refs[:3]
        refs = refs[3:]
        x_scr = refs[-1]
        refs = refs[:-1]
    (win_ref, wout_ref, wconv_ref, vec_ref, g_ref, ws_ref, bs_ref, wr_ref, br_ref,
     smid_ref, sh2_ref, slogit_ref, xmid_ref, h2_ref, logit_ref, cst_ref,
     h_scr, z_scr, aext_scr, conv_scr, convb_scr, stat_scr, u_scr, v_scr, y_scr) = refs
    i = pl.program_id(0)

    @pl.when(i == n_prompt_tiles)
    def _():
        xmid_ref[...] = smid_ref[...]
        h2_ref[...] = sh2_ref[...]
        logit_ref[...] = slogit_ref[...]

    @pl.when(i < n_prompt_tiles)
    def _():
        x_src = x_ref
        if fused:
            _phase_moe_residual(x_ref, yg_ref, wgt_ref, lambda r0: gate_ref[0], x_scr, ROW_TILE)
            x_src = x_scr
        _mixer_prompt_tile(lax.rem(i, tiles_per_seq), tiles_per_seq,
                           x_src, mod_ref, win_ref, wout_ref, wconv_ref, vec_ref, g_ref,
                           ws_ref, bs_ref, wr_ref, br_ref, xmid_ref, h2_ref, logit_ref, cst_ref,
                           h_scr, z_scr, aext_scr, conv_scr, convb_scr, stat_scr, u_scr, v_scr, y_scr)


def _mixer_prompt_tile(j, tiles_per_seq, x_ref, mod_ref, win_ref, wout_ref, wconv_ref, vec_ref, g_ref,
                       ws_ref, bs_ref, wr_ref, br_ref, xmid_ref, h2_ref, logit_ref, cst_ref,
                       h_scr, z_scr, aext_scr, conv_scr, convb_scr, stat_scr, u_scr, v_scr, y_scr):
    n_rows = ROW_TILE

    def mod(idx, r0):
        del r0
        return mod_ref[0, :, idx * D_MODEL:(idx + 1) * D_MODEL]

    a_buf = aext_scr.at[0]

    @pl.when(j == 0)
    def _():
        a_buf[0:CARRY_ROWS, :] = jnp.zeros((CARRY_ROWS, C_CONV), F32)

    _phase_in_norm(x_ref, h_scr, mod, n_rows)
    z_scr[...] = _dot(h_scr[...], win_ref[...])
    _phase_gates(z_scr, vec_ref, a_buf, CARRY_ROWS, u_scr, v_scr, None, n_rows)
    n_shift = n_rows + CARRY_ROWS - SUBLANES
    for s in range(1, SUBLANES):
        aext_scr[s, 0:n_shift, :] = a_buf[s:s + n_shift, :]

    b_conv = vec_ref[0:1, :]
    lead = CARRY_ROWS - CONV_STATE

    def conv_body(r0):
        acc = jnp.zeros((CONV_CHUNK, C_CONV), F32)
        for k in range(CONV_WIDTH):
            s = (lead + k) % SUBLANES
            acc = acc + wconv_ref[k:k + 1, :] * aext_scr[s, pl.ds(r0 + (lead + k - s), CONV_CHUNK), :]
        conv = acc + b_conv
        r = pl.ds(r0, CONV_CHUNK)
        conv_scr[r, :] = conv
        convb_scr[r, :] = conv.astype(BF16)
    _row_loop(n_rows, conv_body, CONV_CHUNK)

    @pl.when(j == tiles_per_seq - 1)
    def _():
        cst_ref[0] = a_buf[n_rows + lead:n_rows + CARRY_ROWS, :]

    a_buf[0:CARRY_ROWS, :] = a_buf[n_rows:n_rows + CARRY_ROWS, :]

    _phase_group_norm(conv_scr, convb_scr, stat_scr, g_ref, vec_ref, y_scr, n_rows)

    row_i = lax.broadcasted_iota(jnp.int32, (CHUNK, CHUNK), 0)
    col_i = lax.broadcasted_iota(jnp.int32, (CHUNK, CHUNK), 1)
    tril = (col_i <= row_i).astype(F32)
    for hd in range(SGU_HEADS):
        ws_h = (ws_ref[hd] * tril).astype(BF16)
        cols = slice(hd * SGU_HEAD_DIM, (hd + 1) * SGU_HEAD_DIM)
        for c in range(n_rows // CHUNK):
            rows = slice(c * CHUNK, (c + 1) * CHUNK)
            stat_scr[rows, cols] = _dot(ws_h, v_scr[rows, cols])
    beta_b = vec_ref[6:7, :]

    def gate_body(r0):
        r = pl.ds(r0, ROW_CHUNK)
        rb = pl.ds(pl.multiple_of(lax.rem(r0, CHUNK), ROW_CHUNK), ROW_CHUNK)
        yb = u_scr[r, :] * (stat_scr[r, :] + bs_ref[rb, :])
        y_scr[r, C_CONV:C_CONV + C_SGU] = (_rms(yb) * beta_b).astype(BF16)
    _row_loop(n_rows, gate_body)

    _phase_out(x_ref, y_scr, wout_ref, z_scr, wr_ref, br_ref, mod, h_scr,
               xmid_ref, h2_ref, logit_ref, n_rows)


def _mixer_sample_kernel(*refs, fused):
    x_ref, mod_ref = refs[:2]
    refs = refs[2:]
    if fused:
        yg_ref, wgt_ref, gate_ref = refs[:3]
        refs = refs[3:]
        x_scr = refs[-1]
        refs = refs[:-1]
    (win_ref, wout_ref, wconv_ref, vec_ref, g_ref, wsv_ref, bsv_ref, wr_ref, br_ref, state_ref,
     xmid_ref, h2_ref, logit_ref, anew_ref, vout_ref,
     h_scr, z_scr, conv_scr, convb_scr, stat_scr, u_scr, v_scr, y_scr) = refs
    n_rows = x_ref.shape[0]
    n_seq = state_ref.shape[1]
    n_pos = n_rows // n_seq

    def mod(idx, r0):
        rs = pl.ds(pl.multiple_of(lax.rem(r0, n_seq), ROW_CHUNK), ROW_CHUNK)
        return mod_ref[rs, idx * D_MODEL:(idx + 1) * D_MODEL]

    if fused:
        def gate(r0):
            return gate_ref[pl.ds(pl.multiple_of(lax.rem(r0, n_seq), COMBINE_CHUNK), COMBINE_CHUNK), :]
        _phase_moe_residual(x_ref, yg_ref, wgt_ref, gate, x_scr, n_rows)
        x_ref = x_scr

    _phase_in_norm(x_ref, h_scr, mod, n_rows)
    z_scr[...] = _dot(h_scr[...], win_ref[...])
    _phase_gates(z_scr, vec_ref, anew_ref, 0, u_scr, v_scr, vout_ref, n_rows)

    b_conv = vec_ref[0:1, :]

    def conv_body(s0):
        rs = pl.ds(s0, CONV_CHUNK)
        for t in range(n_pos):
            acc = jnp.zeros((CONV_CHUNK, C_CONV), F32)
            for k in range(CONV_WIDTH):
                p = t + k
                if p < CONV_STATE:
                    src = state_ref[p, rs, :]
                else:
                    src = anew_ref[pl.ds(s0 + (p - CONV_STATE) * n_seq, CONV_CHUNK), :]
                acc = acc + wconv_ref[k:k + 1, :] * src
            conv = acc + b_conv
            r = pl.ds(s0 + t * n_seq, CONV_CHUNK)
            conv_scr[r, :] = conv
            convb_scr[r, :] = conv.astype(BF16)
    _row_loop(n_seq, conv_body, CONV_CHUNK)

    _phase_group_norm(conv_scr, convb_scr, stat_scr, g_ref, vec_ref, y_scr, n_rows)

    beta_b = vec_ref[6:7, :]

    def gate_body(s0):
        for t in range(n_pos):
            s = jnp.zeros((CONV_CHUNK, C_SGU), F32)
            for jj in range(t + 1):
                s = s + wsv_ref[t * n_pos + jj:t * n_pos + jj + 1, :] * v_scr[pl.ds(s0 + jj * n_seq, CONV_CHUNK), :]
            r = pl.ds(s0 + t * n_seq, CONV_CHUNK)
            yb = u_scr[r, :] * (s + bsv_ref[t:t + 1, :])
            y_scr[r, C_CONV:C_CONV + C_SGU] = (_rms(yb) * beta_b).astype(BF16)
    _row_loop(n_seq, gate_body, CONV_CHUNK)

    _phase_out(x_ref, y_scr, wout_ref, z_scr, wr_ref, br_ref, mod, h_scr,
               xmid_ref, h2_ref, logit_ref, n_rows)


def _const_spec(shape):
    nd = len(shape)
    return pl.BlockSpec(shape, lambda *_: (0,) * nd, pipeline_mode=pl.Buffered(1))


def _const_out_spec(shape):
    nd = len(shape)
    return pl.BlockSpec(shape, lambda *_: (0,) * nd)


def _mixer(x_p, x_s, x_s_block, moe, mod_p, mod_s, w_in_b, w_out_b, wconv_p, vec, g_mat, w_s, bs_full,
           wsv, bsv, w_router, b_router, state_t, n_prompt_seq, seq_len):
    fused = moe is not None
    n_prompt = n_prompt_seq * seq_len
    n_seq, n_pos = state_t.shape[1], bsv.shape[0]
    n_sample = n_seq * n_pos
    t_all = n_prompt + n_sample
    nj = seq_len // ROW_TILE
    weight_specs = [
        _const_spec((D_MODEL, 2 * C_CONV + 2 * C_SGU)),
        _const_spec((D_MODEL, D_MODEL)),
        _const_spec((CARRY_ROWS, C_CONV)),
        _const_spec((8, C_CONV)),
        _const_spec((C_CONV, C_CONV)),
    ]
    router_specs = [_const_spec((N_EXPERTS, D_MODEL)), _const_spec((N_EXPERTS, 1))]
    common_scratch = lambda n: [
        pltpu.VMEM((n, D_MODEL), BF16),
        pltpu.VMEM((n, 2 * C_CONV + 2 * C_SGU), F32),
    ]
    tail_scratch = lambda n, vdt: [
        pltpu.VMEM((n, C_CONV), F32),
        pltpu.VMEM((n, C_CONV), BF16),
        pltpu.VMEM((n, C_CONV), F32),
        pltpu.VMEM((n, C_SGU), F32),
        pltpu.VMEM((n, C_SGU), vdt),
        pltpu.VMEM((n, D_MODEL), BF16),
    ]
    assert n_sample == ROW_TILE
    n_tiles_p = n_prompt // ROW_TILE
    tile_p = lambda i: jnp.minimum(i, n_tiles_p - 1)
    seq_of = lambda i: jnp.minimum(i // nj, n_prompt_seq - 1)
    moe_s_specs, moe_p_specs, moe_s_args, moe_p_args, x_scratch = [], [], [], [], []
    if fused:
        yg, wgt, gate_p, gate_s = moe
        moe_s_specs = [pl.BlockSpec((TOP_K, n_sample, D_WORDS), lambda i: (0, x_s_block, 0)),
                       pl.BlockSpec((n_sample, LANES), lambda i: (x_s_block, 0)),
                       _const_spec((n_seq, D_MODEL))]
        moe_p_specs = [pl.BlockSpec((TOP_K, ROW_TILE, D_WORDS), lambda i: (0, tile_p(i), 0)),
                       pl.BlockSpec((ROW_TILE, LANES), lambda i: (tile_p(i), 0)),
                       pl.BlockSpec((1, 1, D_MODEL), lambda i: (seq_of(i), 0, 0))]
        moe_s_args = [yg, wgt, gate_s]
        moe_p_args = [yg, wgt, gate_p.reshape(n_prompt_seq, 1, D_MODEL)]
        x_scratch = [pltpu.VMEM((ROW_TILE, D_MODEL), F32)]
    smid, sh2, slogits, a_new, v_new = pl.pallas_call(
        functools.partial(_mixer_sample_kernel, fused=fused),
        grid=(1,),
        in_specs=[
            pl.BlockSpec((n_sample, D_MODEL), lambda i: (x_s_block, 0)),
            _const_spec((n_seq, N_MOD * D_MODEL)),
            *moe_s_specs,
            *weight_specs,
            _const_spec((n_pos * n_pos, C_SGU)),
            _const_spec((n_pos, C_SGU)),
            *router_specs,
            _const_spec((CONV_STATE, n_seq, C_CONV)),
        ],
        out_specs=(
            _const_out_spec((n_sample, D_MODEL)),
            _const_out_spec((n_sample, D_WORDS)),
            _const_out_spec((N_EXPERTS, n_sample)),
            _const_out_spec((n_sample, C_CONV)),
            _const_out_spec((n_sample, C_SGU)),
        ),
        out_shape=(
            jax.ShapeDtypeStruct((n_sample, D_MODEL), F32),
            jax.ShapeDtypeStruct((n_sample, D_WORDS), jnp.int32),
            jax.ShapeDtypeStruct((N_EXPERTS, n_sample), F32),
            jax.ShapeDtypeStruct((n_sample, C_CONV), F32),
            jax.ShapeDtypeStruct((n_sample, C_SGU), F32),
        ),
        scratch_shapes=common_scratch(n_sample) + tail_scratch(n_sample, F32) + x_scratch,
        compiler_params=pltpu.CompilerParams(
            dimension_semantics=("arbitrary",), vmem_limit_bytes=VMEM_LIMIT),
        name="mixer_sample",
    )(x_s, mod_s, *moe_s_args, w_in_b, w_out_b, wconv_p, vec, g_mat, wsv, bsv, w_router, b_router,
      state_t)

    xmid, h2, logits, cst = pl.pallas_call(
        functools.partial(_mixer_prompt_kernel, tiles_per_seq=nj, n_prompt_tiles=n_tiles_p,
                          fused=fused),
        grid=(n_tiles_p + 1,),
        in_specs=[
            pl.BlockSpec((ROW_TILE, D_MODEL), lambda i: (tile_p(i), 0)),
            pl.BlockSpec((1, 1, N_MOD * D_MODEL), lambda i: (seq_of(i), 0, 0)),
            *moe_p_specs,
            *weight_specs,
            _const_spec((SGU_HEADS, CHUNK, CHUNK)),
            _const_spec((CHUNK, C_SGU)),
            *router_specs,
            _const_spec((n_sample, D_MODEL)),
            _const_spec((n_sample, D_WORDS)),
            _const_spec((N_EXPERTS, n_sample)),
        ],
        out_specs=(
            pl.BlockSpec((ROW_TILE, D_MODEL), lambda i: (i, 0)),
            pl.BlockSpec((ROW_TILE, D_WORDS), lambda i: (i, 0)),
            pl.BlockSpec((N_EXPERTS, ROW_TILE), lambda i: (0, i)),
            pl.BlockSpec((1, CONV_STATE, C_CONV), lambda i: (seq_of(i), 0, 0)),
        ),
        out_shape=(
            jax.ShapeDtypeStruct((t_all, D_MODEL), F32),
            jax.ShapeDtypeStruct((t_all, D_WORDS), jnp.int32),
            jax.ShapeDtypeStruct((N_EXPERTS, t_all), F32),
            jax.ShapeDtypeStruct((n_prompt_seq, CONV_STATE, C_CONV), F32),
        ),
        scratch_shapes=common_scratch(ROW_TILE)
        + [pltpu.VMEM((SUBLANES, CARRY_ROWS + ROW_TILE, C_CONV), F32)]
        + tail_scratch(ROW_TILE, BF16) + x_scratch,
        compiler_params=pltpu.CompilerParams(
            dimension_semantics=("arbitrary",), vmem_limit_bytes=VMEM_LIMIT),
        name="mixer_prompt",
    )(x_p, mod_p, *moe_p_args, w_in_b, w_out_b, wconv_p, vec, g_mat, w_s, bs_full, w_router, b_router,
      smid, sh2, slogits)
    return xmid, h2, logits, cst, a_new, v_new


def _tile_plan(cnt):
    e_sub = lax.broadcasted_iota(jnp.int32, (N_EXPERTS, N_EXPERTS), 0)
    e_lane = lax.broadcasted_iota(jnp.int32, (N_EXPERTS, N_EXPERTS), 1)
    tiles = jnp.floor((cnt + (GEMM_TILE - 1.0)) * (1.0 / GEMM_TILE))
    active = jnp.where(cnt > 0.0, 1.0, 0.0)

    def over_experts(tri, col):
        return _dot(tri.astype(BF16), jnp.broadcast_to(col, (N_EXPERTS, LANES)).astype(BF16))[:, 0:1]
    tile_end = over_experts(jnp.where(e_lane <= e_sub, 1.0, 0.0), tiles)
    tile_start = tile_end - tiles
    ordinal = over_experts(jnp.where(e_lane < e_sub, 1.0, 0.0), active)

    g = lax.broadcasted_iota(jnp.int32, (1, PLAN_LANES), 1).astype(F32)
    e_id = lax.broadcasted_iota(jnp.int32, (N_EXPERTS, PLAN_LANES), 0).astype(F32)
    done = jnp.sum(jnp.where(tile_end <= g, 1.0, 0.0), axis=0, keepdims=True)
    live = done < float(N_EXPERTS)
    last_e = jnp.max(jnp.where(cnt > 0.0, e_id[:, 0:1], 0.0), axis=0, keepdims=True)
    te = jnp.where(live, done, last_e)
    pick = e_id == te

    def at_tile(col):
        return jnp.sum(jnp.where(pick, col, 0.0), axis=0, keepdims=True)
    start_g = at_tile(tile_start)
    ord_g = at_tile(ordinal)
    valid = jnp.clip(at_tile(cnt) - (g - start_g) * GEMM_TILE, 0.0, float(GEMM_TILE))
    valid = jnp.where(live, valid, 0.0)
    first = jnp.where(live, jnp.where(g == start_g, 1.0, 0.0), 0.0)
    slot = ord_g - 2.0 * jnp.floor(ord_g * 0.5)
    later = jnp.where(e_id > te, jnp.where(cnt > 0.0, e_id, float(N_EXPERTS)), float(N_EXPERTS))
    nxt = jnp.min(later, axis=0, keepdims=True)
    nxt = jnp.where(nxt >= float(N_EXPERTS), -1.0, nxt)

    row = lax.broadcasted_iota(jnp.int32, (PLAN_ROWS, PLAN_LANES), 0)
    plan = jnp.zeros((PLAN_ROWS, PLAN_LANES), F32)
    for r, v in ((PLAN_EXPERT, te), (PLAN_VALID, valid), (PLAN_FIRST, first), (PLAN_SLOT, slot),
                 (PLAN_NEXT, nxt)):
        plan = jnp.where(row == r, v, plan)
    return plan.astype(jnp.int32), tile_start * GEMM_TILE


def _router_kernel(lg_ref, wgt_ref, pos_ref, plan_ref, cnt_scr, run_scr, start_scr, earlier_scr):
    ph = pl.program_id(0)
    i = pl.program_id(1)
    n = lg_ref.shape[1]

    @pl.when(jnp.logical_and(ph == 0, i == 0))
    def _():
        cnt_scr[...] = jnp.zeros_like(cnt_scr)

    lg = lg_ref[...]
    sub = lax.broadcasted_iota(jnp.int32, lg.shape, 0).astype(F32)
    vals, sels = [], []
    for _ in range(TOP_K):
        m = jnp.max(lg, axis=0, keepdims=True)
        idx = jnp.min(jnp.where(lg == m, sub, float(N_EXPERTS)), axis=0, keepdims=True)
        sel = sub == idx
        vals.append(m)
        sels.append(sel)
        lg = jnp.where(sel, -jnp.inf, lg)
    onehot = jnp.zeros(lg.shape, F32)
    for sel in sels:
        onehot = onehot + jnp.where(sel, 1.0, 0.0)
    tile_cnt = jnp.sum(onehot, axis=1, keepdims=True)

    @pl.when(ph == 0)
    def _():
        cnt_scr[...] = cnt_scr[...] + tile_cnt

    @pl.when(jnp.logical_and(ph == 1, i == 0))
    def _():
        plan, row_start = _tile_plan(cnt_scr[...])
        plan_ref[...] = plan
        start_scr[...] = row_start
        run_scr[...] = jnp.zeros_like(run_scr)
        t_row = lax.broadcasted_iota(jnp.int32, (n, n), 0)
        t_col = lax.broadcasted_iota(jnp.int32, (n, n), 1)
        earlier_scr[...] = jnp.where(t_row < t_col, 1.0, 0.0).astype(BF16)

    @pl.when(ph == 1)
    def _():
        exps = [jnp.exp(v - vals[0]) for v in vals]
        inv = 1.0 / (exps[0] + exps[1] + exps[2] + exps[3])
        w_row = lax.broadcasted_iota(jnp.int32, (LANES, n), 0)
        w_all = jnp.zeros((LANES, n), F32)
        for k in range(TOP_K):
            w_all = jnp.where(w_row == k, exps[k] * inv, w_all)
        wgt_ref[...] = w_all.T
        slot0 = _dot(onehot.astype(BF16), earlier_scr[...]) + (run_scr[...] + start_scr[...])
        for k in range(TOP_K):
            pos_k = jnp.sum(jnp.where(sels[k], slot0, 0.0), axis=0, keepdims=True)
            pos_ref[k:k + 1, :] = pos_k.astype(jnp.int32)
        run_scr[...] = run_scr[...] + tile_cnt


def _router(logits):
    t_all = logits.shape[1]
    col = pltpu.VMEM((N_EXPERTS, 1), F32)
    return pl.pallas_call(
        _router_kernel,
        grid=(2, t_all // ROUTER_TILE),
        in_specs=[pl.BlockSpec((N_EXPERTS, ROUTER_TILE), lambda ph, i: (0, i))],
        out_specs=(
            pl.BlockSpec((ROUTER_TILE, LANES), lambda ph, i: (i * ph, 0)),
            pl.BlockSpec((TOP_K, ROUTER_TILE), lambda ph, i: (0, i * ph)),
            _const_out_spec((PLAN_ROWS, PLAN_LANES)),
        ),
        out_shape=(
            jax.ShapeDtypeStruct((t_all, LANES), F32),
            jax.ShapeDtypeStruct((TOP_K, t_all), jnp.int32),
            jax.ShapeDtypeStruct((PLAN_ROWS, PLAN_LANES), jnp.int32),
        ),
        scratch_shapes=[col, col, col, pltpu.VMEM((ROUTER_TILE, ROUTER_TILE), BF16)],
        compiler_params=pltpu.CompilerParams(
            dimension_semantics=("arbitrary", "arbitrary"), vmem_limit_bytes=VMEM_LIMIT),
        name="router",
    )(logits)


def _expert_kernel(plan_ref, x_ref, w1_hbm, b1_ref, w2_hbm, b2_ref, o_ref,
                   w1f_scr, w2f_scr, w1b_scr, w2b_scr, sem, *, layer):
    g = pl.program_id(0)
    nv = plan_ref[PLAN_VALID, g]

    def weight_copies(expert, slot):
        return (pltpu.make_async_copy(w1_hbm.at[layer, expert], w1f_scr.at[slot], sem.at[0, slot]),
                pltpu.make_async_copy(w2_hbm.at[layer, expert], w2f_scr.at[slot], sem.at[1, slot]))

    @pl.when(plan_ref[PLAN_FIRST, g] == 1)
    def _():
        expert = plan_ref[PLAN_EXPERT, g]
        slot = plan_ref[PLAN_SLOT, g]
        nxt = plan_ref[PLAN_NEXT, g]

        @pl.when(g == 0)
        def _():
            for cp in weight_copies(expert, slot):
                cp.start()
        for cp in weight_copies(expert, slot):
            cp.wait()

        @pl.when(nxt >= 0)
        def _():
            for cp in weight_copies(nxt, 1 - slot):
                cp.start()
        w1b_scr[...] = w1f_scr[slot].astype(BF16)
        w2b_scr[...] = w2f_scr[slot].astype(BF16)

    @pl.when(nv > 0)
    def _():
        x_left, x_right = _unpack_bf16_pair(x_ref[...])
        xl = x_left.astype(BF16)
        xr = x_right.astype(BF16)

        def hidden(col0):
            cols = slice(col0, col0 + FF_BLOCK)
            return (_dot(xl, w1b_scr[0:D_WORDS, cols]) + _dot(xr, w1b_scr[D_WORDS:D_MODEL, cols])
                    + b1_ref[0, :, cols])

        y = jnp.zeros((GEMM_TILE, D_MODEL), F32) + b2_ref[0]
        for jb in range(D_FF // FF_BLOCK):
            gg = jnp.minimum(hidden(jb * FF_BLOCK), SWIGLU_LIMIT)
            up = jnp.clip(hidden(D_FF + jb * FF_BLOCK), -SWIGLU_LIMIT, SWIGLU_LIMIT)
            act = gg * _sigmoid(SWIGLU_ALPHA * gg) * (up + 1.0)
            y = y + _dot(act.astype(BF16), w2b_scr[jb * FF_BLOCK:(jb + 1) * FF_BLOCK, :])
        row = lax.broadcasted_iota(jnp.int32, y.shape, 0)
        o_ref[...] = _pack_bf16_pair(jnp.where(row < nv, y, 0.0))

    @pl.when(nv == 0)
    def _():
        o_ref[...] = jnp.zeros_like(o_ref)


def _experts(xs, plan, w1, b1, w2, b2, layer):
    n_tiles = xs.shape[0] // GEMM_TILE
    depth = w1.shape[0]
    bias_map = lambda g, plan: (layer, plan[PLAN_EXPERT, g], 0, 0)
    grid_spec = pltpu.PrefetchScalarGridSpec(
        num_scalar_prefetch=1,
        grid=(n_tiles,),
        in_specs=[
            pl.BlockSpec((GEMM_TILE, D_WORDS), lambda g, plan: (g, 0)),
            pl.BlockSpec(memory_space=pl.ANY),
            pl.BlockSpec((None, 1, 1, 2 * D_FF), bias_map),
            pl.BlockSpec(memory_space=pl.ANY),
            pl.BlockSpec((None, 1, 1, D_MODEL), bias_map),
        ],
        out_specs=pl.BlockSpec((GEMM_TILE, D_WORDS), lambda g, plan: (g, 0)),
        scratch_shapes=[
            pltpu.VMEM((2, D_MODEL, 2 * D_FF), F32),
            pltpu.VMEM((2, D_FF, D_MODEL), F32),
            pltpu.VMEM((D_MODEL, 2 * D_FF), BF16),
            pltpu.VMEM((D_FF, D_MODEL), BF16),
            pltpu.SemaphoreType.DMA((2, 2)),
        ],
    )
    return pl.pallas_call(
        functools.partial(_expert_kernel, layer=layer),
        grid_spec=grid_spec,
        out_shape=jax.ShapeDtypeStruct((n_tiles * GEMM_TILE, D_WORDS), jnp.int32),
        compiler_params=pltpu.CompilerParams(
            dimension_semantics=("arbitrary",), vmem_limit_bytes=VMEM_LIMIT),
        name="experts",
    )(plan, xs, w1, b1.reshape(depth, N_EXPERTS, 1, 2 * D_FF), w2,
      b2.reshape(depth, N_EXPERTS, 1, D_MODEL))


def _final_kernel(x_ref, yg_ref, wgt_ref, gp_ref, gs_ref, fg_ref, op_ref, os_ref, *, n_prompt_blocks):
    i = pl.program_id(0)
    n_seq = gs_ref.shape[0]

    def run(gate, o_ref):
        def body(r0):
            r = pl.ds(r0, COMBINE_CHUNK)
            w = wgt_ref[r, :]
            g = gate(r0)
            xl = x_ref[r, 0:D_WORDS] + g[:, 0:D_WORDS] * _moe_half(yg_ref, w, r, False)
            xr = x_ref[r, D_WORDS:D_MODEL] + g[:, D_WORDS:D_MODEL] * _moe_half(yg_ref, w, r, True)
            ms = (jnp.sum(xl * xl, axis=-1, keepdims=True)
                  + jnp.sum(xr * xr, axis=-1, keepdims=True)) * (1.0 / D_MODEL)
            scale = lax.rsqrt(ms + EPS)
            o_ref[r, 0:D_WORDS] = xl * scale * fg_ref[:, 0:D_WORDS]
            o_ref[r, D_WORDS:D_MODEL] = xr * scale * fg_ref[:, D_WORDS:D_MODEL]
        _row_loop(x_ref.shape[0], body, COMBINE_CHUNK)

    @pl.when(i < n_prompt_blocks)
    def _():
        run(lambda r0: gp_ref[0], op_ref)

    @pl.when(i >= n_prompt_blocks)
    def _():
        run(lambda r0: gs_ref[pl.ds(pl.multiple_of(lax.rem(r0, n_seq), COMBINE_CHUNK), COMBINE_CHUNK), :],
            os_ref)


def _final(xmid, yg, wgt, gate_p, gate_s, final_g, n_prompt, seq_len):
    t_all = xmid.shape[0]
    n_prompt_blocks = n_prompt // ROW_TILE
    per_seq = seq_len // ROW_TILE
    n_seq_p = gate_p.shape[0]
    assert t_all - n_prompt == ROW_TILE
    out_specs = (
        pl.BlockSpec((ROW_TILE, D_MODEL), lambda i: (jnp.minimum(i, n_prompt_blocks - 1), 0)),
        pl.BlockSpec((ROW_TILE, D_MODEL), lambda i: (0, 0)),
    )
    out_shape = (jax.ShapeDtypeStruct((n_prompt, D_MODEL), F32),
                 jax.ShapeDtypeStruct((ROW_TILE, D_MODEL), F32))
    return pl.pallas_call(
        functools.partial(_final_kernel, n_prompt_blocks=n_prompt_blocks),
        grid=(t_all // ROW_TILE,),
        in_specs=[
            pl.BlockSpec((ROW_TILE, D_MODEL), lambda i: (i, 0)),
            pl.BlockSpec((TOP_K, ROW_TILE, D_WORDS), lambda i: (0, i, 0)),
            pl.BlockSpec((ROW_TILE, LANES), lambda i: (i, 0)),
            pl.BlockSpec((1, 1, D_MODEL), lambda i: (jnp.minimum(i // per_seq, n_seq_p - 1), 0, 0)),
            _const_spec(gate_s.shape),
            _const_spec((1, D_MODEL)),
        ],
        out_specs=out_specs,
        out_shape=out_shape,
        compiler_params=pltpu.CompilerParams(
            dimension_semantics=("arbitrary",), vmem_limit_bytes=VMEM_LIMIT),
        name="final_combine",
    )(xmid, yg, wgt, gate_p.reshape(n_seq_p, 1, D_MODEL), gate_s, final_g.reshape(1, D_MODEL))


def _sc_mesh():
    return plsc.VectorSubcoreMesh(core_axis_name="core", subcore_axis_name="subcore",
                                  num_cores=SC_CORES, num_subcores=SC_SUBCORES)


def _sc_worker_id():
    return lax.axis_index("subcore") * SC_CORES + lax.axis_index("core")


def _sc_dispatch(h2w, pos_c, n_slots):
    n_chunks = pos_c.shape[1]

    @functools.partial(
        pl.kernel, mesh=_sc_mesh(),
        out_type=jax.ShapeDtypeStruct((n_slots, D_WORDS), jnp.int32),
        scratch_types=[pltpu.VMEM((TOP_K, SC_CHUNK), jnp.int32),
                       pltpu.VMEM((SC_CHUNK, D_WORDS), jnp.int32),
                       pltpu.SemaphoreType.DMA],
        name="moe_dispatch")
    def run(h_hbm, pos_hbm, out_hbm, idx_v, rows_v, sem):
        wid = _sc_worker_id()

        @pl.loop(0, n_chunks)
        def _(c):
            base = pl.multiple_of((wid * n_chunks + c) * SC_CHUNK, SUBLANES)
            pltpu.sync_copy(pos_hbm.at[wid, c], idx_v)
            pltpu.sync_copy(h_hbm.at[pl.ds(base, SC_CHUNK)], rows_v)
            copies = [pltpu.async_copy(rows_v, out_hbm.at[idx_v.at[k]], sem) for k in range(TOP_K)]
            for cp in copies:
                cp.wait()

    return run(h2w, pos_c)


def _sc_gather_back(ys, pos_c, t_all):
    n_chunks = pos_c.shape[1]

    @functools.partial(
        pl.kernel, mesh=_sc_mesh(),
        out_type=jax.ShapeDtypeStruct((TOP_K, t_all, D_WORDS), jnp.int32),
        scratch_types=[pltpu.VMEM((TOP_K, SC_CHUNK), jnp.int32),
                       pltpu.VMEM((TOP_K, SC_CHUNK, D_WORDS), jnp.int32),
                       pltpu.SemaphoreType.DMA],
        name="moe_gather_back")
    def run(ys_hbm, pos_hbm, out_hbm, idx_v, rows_v, sem):
        wid = _sc_worker_id()

        @pl.loop(0, n_chunks)
        def _(c):
            base = pl.multiple_of((wid * n_chunks + c) * SC_CHUNK, SUBLANES)
            pltpu.sync_copy(pos_hbm.at[wid, c], idx_v)
            copies = [pltpu.async_copy(ys_hbm.at[idx_v.at[k]], rows_v.at[k], sem) for k in range(TOP_K)]
            for cp in copies:
                cp.wait()
            for k in range(TOP_K):
                pltpu.sync_copy(rows_v.at[k], out_hbm.at[k, pl.ds(base, SC_CHUNK)])

    return run(ys, pos_c)


def kernel(x_prompt, x_sample, state_conv, c_prompt, c_sample, w_ada, b_ada, w_in, w_conv, b_conv,
           gn_g, gn_b, sgu_ln_g, sgu_ln_b, w_s, b_s, beta_a, beta_b, w_out, w_router, b_router,
           w1, b1, w2, b2, final_g):
    n_bp, seq_len, _ = x_prompt.shape
    n_bs, n_pos, _ = x_sample.shape
    depth = w_ada.shape[0]
    n_prompt = n_bp * seq_len
    n_sample = n_bs * n_pos
    t_all = n_prompt + n_sample
    n_tiles = (t_all * TOP_K) // GEMM_TILE + N_EXPERTS
    n_workers = SC_CORES * SC_SUBCORES
    assert t_all % (n_workers * SC_CHUNK) == 0 and n_tiles <= PLAN_LANES and t_all % ROUTER_TILE == 0

    x_p = x_prompt.reshape(n_prompt, D_MODEL)
    x_s = x_sample.transpose(1, 0, 2).reshape(n_sample, D_MODEL)
    x_s_block = 0
    moe = None
    mods =_adaln(jnp.concatenate([c_prompt, c_sample], axis=0), w_ada, b_ada)
    state_t = state_conv.transpose(0, 2, 1, 3)
    grp = jnp.arange(C_CONV) // (C_CONV // CONV_GROUPS)
    g_mat = ((grp[:, None] == grp[None, :]).astype(F32) / (C_CONV // CONV_GROUPS)).astype(BF16)

    conv_p, conv_s, v_s = [], [], []
    for l in range(depth):
        mod_p = mods[l, :n_bp].reshape(n_bp, 1, N_MOD * D_MODEL)
        mod_s = mods[l, n_bp:]
        wconv_p = jnp.pad(w_conv[l], ((0, CARRY_ROWS - CONV_WIDTH), (0, 0)))
        zero = jnp.zeros_like(b_conv[l])
        vec = jnp.stack([b_conv[l], gn_g[l], gn_b[l], sgu_ln_g[l], sgu_ln_b[l],
                         beta_a[l], beta_b[l], zero], axis=0)
        bs_full = jnp.repeat(b_s[l].T, SGU_HEAD_DIM, axis=1)
        wsv = jnp.repeat(w_s[l][:, :n_pos, :n_pos].transpose(1, 2, 0).reshape(n_pos * n_pos, SGU_HEADS),
                         SGU_HEAD_DIM, axis=1)
        bsv = jnp.repeat(b_s[l][:, :n_pos].T, SGU_HEAD_DIM, axis=1)
        xmid, h2, logits, cst, a_new, v_new = _mixer(
            x_p, x_s, x_s_block, moe, mod_p, mod_s, w_in[l].astype(BF16), w_out[l].astype(BF16), wconv_p,
            vec, g_mat,
            w_s[l], bs_full, wsv, bsv, w_router[l].T, b_router[l].reshape(N_EXPERTS, 1),
            state_t[l], n_bp, seq_len)
        conv_p.append(cst)
        a_new = a_new.reshape(n_pos, n_bs, C_CONV).transpose(1, 0, 2)
        conv_s.append(jnp.concatenate([state_conv[l], a_new], axis=1)[:, -CONV_STATE:])
        v_s.append(v_new.reshape(n_pos, n_bs, C_SGU).transpose(1, 0, 2))

        wgt, pos, plan = _router(logits)
        pos_c = pos.reshape(TOP_K, n_workers, -1, SC_CHUNK).transpose(1, 2, 0, 3)
        xs = _sc_dispatch(h2, pos_c, n_tiles * GEMM_TILE)
        ys = _experts(xs, plan, w1, b1, w2, b2, l)
        yg = _sc_gather_back(ys, pos_c, t_all)
        moe = (yg, wgt, mod_p[:, 0, 5 * D_MODEL:], mod_s[:, 5 * D_MODEL:])
        x_p = x_s = xmid
        x_s_block = n_prompt // n_sample

    y_p, y_s = _final(xmid, *moe, final_g, n_prompt, seq_len)
    y_prompt = y_p.reshape(n_bp, seq_len, D_MODEL)
    y_sample = y_s.reshape(n_pos, n_bs, D_MODEL).transpose(1, 0, 2)
    return (y_prompt, y_sample, jnp.stack(conv_p, axis=0), jnp.stack(conv_s, axis=0),
            jnp.stack(v_s, axis=0))
```

```python
import functools
import math

import jax
import jax.numpy as jnp
from jax import lax
from jax.experimental import pallas as pl
from jax.experimental.pallas import tpu as pltpu
from jax.experimental.pallas import tpu_sc as plsc

F32 = jnp.float32
BF16 = jnp.bfloat16

D_MODEL = 1024
C_CONV = 512
C_SGU = 512
CONV_GROUPS = 8
CONV_WIDTH = 31
CONV_STATE = CONV_WIDTH - 1
SGU_HEADS = 4
SGU_HEAD_DIM = C_SGU // SGU_HEADS
CHUNK = 128
N_EXPERTS = 32
TOP_K = 4
D_FF = 1024
SWIGLU_LIMIT = 7.0
SWIGLU_ALPHA = 1.702
N_MOD = 6
EPS = 1e-5

ROW_TILE = 512
ROUTER_TILE = 1536
ROW_CHUNK = 128
CONV_CHUNK = 32
COMBINE_CHUNK = 64
CARRY_ROWS = 32
SUBLANES = 8
LANES = 128
PLAN_ROWS = 8
PLAN_LANES = 256
PLAN_EXPERT, PLAN_VALID, PLAN_FIRST, PLAN_SLOT, PLAN_NEXT = range(5)
GEMM_TILE = 512
FF_BLOCK = 256
TILES_PER_STEP = 2
VMEM_LIMIT = 56 * 1024 * 1024
D_WORDS = D_MODEL // 2
SC_CHUNK = 48
HI_MASK = -65536
SC_CORES = 2
SC_SUBCORES = 16


def _rms(x):
    return x * lax.rsqrt(jnp.mean(x * x, axis=-1, keepdims=True) + EPS)


def _gelu(x):
    return 0.5 * x * (1.0 + lax.erf(x * (1.0 / math.sqrt(2.0))))


def _sigmoid(x):
    return 1.0 / (1.0 + jnp.exp(-x))


def _split_bf16(x):
    hi = x.astype(BF16)
    lo = (x - hi.astype(F32)).astype(BF16)
    return hi, lo


def _pack_bf16_pair(x):
    bits = lax.bitcast_convert_type(x.astype(BF16).astype(F32), jnp.int32)
    return lax.shift_right_logical(bits[:, :D_WORDS], 16) | (bits[:, D_WORDS:] & HI_MASK)


def _unpack_bf16_pair(w):
    left = lax.bitcast_convert_type(lax.shift_left(w, 16), F32)
    right = lax.bitcast_convert_type(w & HI_MASK, F32)
    return left, right


def _dot(a, b):
    return jnp.dot(a, b, preferred_element_type=F32)


def _dot_nt(a, b):
    return lax.dot_general(a, b, (((1,), (1,)), ((), ())), preferred_element_type=F32)


def _row_loop(n_rows, body, chunk=None):
    chunk = ROW_CHUNK if chunk is None else chunk

    def step(i, carry):
        body(pl.multiple_of(i * chunk, chunk))
        return carry
    lax.fori_loop(0, n_rows // chunk, step, 0)


def _adaln_kernel(c_ref, w_ref, b_ref, o_ref):
    c = c_ref[...]
    s_hi, s_lo = _split_bf16(c * _sigmoid(c))
    w_hi, w_lo = _split_bf16(w_ref[0])
    acc = _dot(s_hi, w_hi) + _dot(s_hi, w_lo) + _dot(s_lo, w_hi)
    o_ref[0] = acc + b_ref[0]


def _adaln(c_all, w_ada, b_ada):
    depth, _, n_out = w_ada.shape
    n_rows = c_all.shape[0]
    tn = 1024
    return pl.pallas_call(
        _adaln_kernel,
        grid=(depth, n_out // tn),
        in_specs=[
            pl.BlockSpec((n_rows, D_MODEL), lambda l, j: (0, 0)),
            pl.BlockSpec((1, D_MODEL, tn), lambda l, j: (l, 0, j)),
            pl.BlockSpec((1, 1, tn), lambda l, j: (l, 0, j)),
        ],
        out_specs=pl.BlockSpec((1, n_rows, tn), lambda l, j: (l, 0, j)),
        out_shape=jax.ShapeDtypeStruct((depth, n_rows, n_out), F32),
        compiler_params=pltpu.CompilerParams(
            dimension_semantics=("arbitrary", "arbitrary"), vmem_limit_bytes=VMEM_LIMIT),
        name="adaln",
    )(c_all, w_ada, b_ada.reshape(depth, 1, n_out))


def _phase_in_norm(x_ref, h_scr, mod, n_rows):
    def body(r0):
        r = pl.ds(r0, ROW_CHUNK)
        h = _rms(x_ref[r, :]) * (1.0 + mod(1, r0)) + mod(0, r0)
        h_scr[r, :] = h.astype(BF16)
    _row_loop(n_rows, body)


def _phase_gates(z_scr, vec_ref, a_dst, a_off, u_scr, v_scr, vout_ref, n_rows):
    ln_g = vec_ref[3:4, :]
    ln_b = vec_ref[4:5, :]

    def body(r0):
        r = pl.ds(r0, ROW_CHUNK)
        a = z_scr[r, 0:C_CONV] * _sigmoid(z_scr[r, C_CONV:2 * C_CONV])
        a_dst[pl.ds(r0 + a_off, ROW_CHUNK), :] = a
        u_scr[r, :] = _gelu(z_scr[r, 2 * C_CONV:2 * C_CONV + C_SGU])
        gv = _gelu(z_scr[r, 2 * C_CONV + C_SGU:2 * C_CONV + 2 * C_SGU])
        mu = jnp.mean(gv, axis=-1, keepdims=True)
        dv = gv - mu
        var = jnp.mean(dv * dv, axis=-1, keepdims=True)
        v = dv * lax.rsqrt(var + EPS) * ln_g + ln_b
        if vout_ref is not None:
            vout_ref[r, :] = v
        v_scr[r, :] = v.astype(v_scr.dtype)
    _row_loop(n_rows, body)


def _phase_group_norm(conv_scr, convb_scr, stat_scr, g_ref, vec_ref, y_scr, n_rows):
    stat_scr[...] = _dot(convb_scr[...], g_ref[...])

    def center(r0):
        r = pl.ds(r0, ROW_CHUNK)
        d = conv_scr[r, :] - stat_scr[r, :]
        conv_scr[r, :] = d
        convb_scr[r, :] = (d * d).astype(BF16)
    _row_loop(n_rows, center)
    stat_scr[...] = _dot(convb_scr[...], g_ref[...])
    gn_g = vec_ref[1:2, :]
    gn_b = vec_ref[2:3, :]
    beta_a = vec_ref[5:6, :]

    def finish(r0):
        r = pl.ds(r0, ROW_CHUNK)
        gn = conv_scr[r, :] * lax.rsqrt(stat_scr[r, :] + EPS) * gn_g + gn_b
        ya = gn * _sigmoid(gn)
        y_scr[r, 0:C_CONV] = (_rms(ya) * beta_a).astype(BF16)
    _row_loop(n_rows, finish)


def _phase_out(x_ref, y_scr, wout_ref, z_scr, wr_ref, br_ref, mod, h_scr,
               xmid_ref, h2_ref, logit_ref, n_rows):
    z_scr[:, 0:D_MODEL] = _dot(y_scr[...], wout_ref[...])

    def body(r0):
        r = pl.ds(r0, ROW_CHUNK)
        xm = x_ref[r, :] + mod(2, r0) * z_scr[r, 0:D_MODEL]
        xmid_ref[r, :] = xm
        h2 = _rms(xm) * (1.0 + mod(4, r0)) + mod(3, r0)
        hi, lo = _split_bf16(h2)
        h2_ref[r, :] = _pack_bf16_pair(h2)
        y_scr[r, :] = hi
        h_scr[r, :] = lo
    _row_loop(n_rows, body)
    w_hi, w_lo = _split_bf16(wr_ref[...])
    both = _dot_nt(jnp.concatenate([w_hi, w_lo], axis=0), y_scr[...])
    logit_ref[...] = (both[0:N_EXPERTS] + both[N_EXPERTS:2 * N_EXPERTS]
                      + _dot_nt(w_hi, h_scr[...])) + br_ref[...]


def _moe_half(yg_ref, w, r, hi_half):
    moe = jnp.zeros((COMBINE_CHUNK, D_WORDS), F32)
    for k in range(TOP_K):
        word = yg_ref[k, r, :]
        bits = (word & HI_MASK) if hi_half else lax.shift_left(word, 16)
        moe = moe + w[:, k:k + 1] * lax.bitcast_convert_type(bits, F32)
    return moe


def _phase_moe_residual(xmid_ref, yg_ref, wgt_ref, gate, x_dst, n_rows):
    def body(r0):
        r = pl.ds(r0, COMBINE_CHUNK)
        w = wgt_ref[r, :]
        g = gate(r0)
        for hi_half in (False, True):
            cols = slice(D_WORDS, D_MODEL) if hi_half else slice(0, D_WORDS)
            x_dst[r, cols] = xmid_ref[r, cols] + g[:, cols] * _moe_half(yg_ref, w, r, hi_half)
    _row_loop(n_rows, body, COMBINE_CHUNK)


def _mixer_prompt_kernel(*refs, tiles_per_seq, n_prompt_tiles, fused):
    x_ref, mod_ref = refs[:2]
    refs = refs[2:]
    if fused:
        yg_ref, wgt_ref, gate_ref = refs[:3]
        refs = refs[3:]
        x_scr = refs[-1]
        refs = refs[:-1]
    (win_ref, wout_ref, wconv_ref, vec_ref, g_ref, ws_ref, bs_ref, wr_ref, br_ref,
     smid_ref, sh2_ref, slogit_ref, xmid_ref, h2_ref, logit_ref, cst_ref,
     h_scr, z_scr, aext_scr, conv_scr, convb_scr, stat_scr, u_scr, v_scr, y_scr) = refs
    i = pl.program_id(0)

    @pl.when(i == n_prompt_tiles)
    def _():
        xmid_ref[...] = smid_ref[...]
        h2_ref[...] = sh2_ref[...]
        logit_ref[...] = slogit_ref[...]

    @pl.when(i < n_prompt_tiles)
    def _():
        x_src = x_ref
        if fused:
            _phase_moe_residual(x_ref, yg_ref, wgt_ref, lambda r0: gate_ref[0], x_scr, ROW_TILE)
            x_src = x_scr
        _mixer_prompt_tile(lax.rem(i, tiles_per_seq), tiles_per_seq,
                           x_src, mod_ref, win_ref, wout_ref, wconv_ref, vec_ref, g_ref,
                           ws_ref, bs_ref, wr_ref, br_ref, xmid_ref, h2_ref, logit_ref, cst_ref,
                           h_scr, z_scr, aext_scr, conv_scr, convb_scr, stat_scr, u_scr, v_scr, y_scr)


def _mixer_prompt_tile(j, tiles_per_seq, x_ref, mod_ref, win_ref, wout_ref, wconv_ref, vec_ref, g_ref,
                       ws_ref, bs_ref, wr_ref, br_ref, xmid_ref, h2_ref, logit_ref, cst_ref,
                       h_scr, z_scr, aext_scr, conv_scr, convb_scr, stat_scr, u_scr, v_scr, y_scr):
    n_rows = ROW_TILE

    def mod(idx, r0):
        del r0
        return mod_ref[0, :, idx * D_MODEL:(idx + 1) * D_MODEL]

    a_buf = aext_scr.at[0]

    @pl.when(j == 0)
    def _():
        a_buf[0:CARRY_ROWS, :] = jnp.zeros((CARRY_ROWS, C_CONV), F32)

    _phase_in_norm(x_ref, h_scr, mod, n_rows)
    z_scr[...] = _dot(h_scr[...], win_ref[...])
    _phase_gates(z_scr, vec_ref, a_buf, CARRY_ROWS, u_scr, v_scr, None, n_rows)
    n_shift = n_rows + CARRY_ROWS - SUBLANES
    for s in range(1, SUBLANES):
        aext_scr[s, 0:n_shift, :] = a_buf[s:s + n_shift, :]

    b_conv = vec_ref[0:1, :]
    lead = CARRY_ROWS - CONV_STATE

    def conv_body(r0):
        acc = jnp.zeros((CONV_CHUNK, C_CONV), F32)
        for k in range(CONV_WIDTH):
            s = (lead + k) % SUBLANES
            acc = acc + wconv_ref[k:k + 1, :] * aext_scr[s, pl.ds(r0 + (lead + k - s), CONV_CHUNK), :]
        conv = acc + b_conv
        r = pl.ds(r0, CONV_CHUNK)
        conv_scr[r, :] = conv
        convb_scr[r, :] = conv.astype(BF16)
    _row_loop(n_rows, conv_body, CONV_CHUNK)

    @pl.when(j == tiles_per_seq - 1)
    def _():
        cst_ref[0] = a_buf[n_rows + lead:n_rows + CARRY_ROWS, :]

    a_buf[0:CARRY_ROWS, :] = a_buf[n_rows:n_rows + CARRY_ROWS, :]

    _phase_group_norm(conv_scr, convb_scr, stat_scr, g_ref, vec_ref, y_scr, n_rows)

    row_i = lax.broadcasted_iota(jnp.int32, (CHUNK, CHUNK), 0)
    col_i = lax.broadcasted_iota(jnp.int32, (CHUNK, CHUNK), 1)
    tril = (col_i <= row_i).astype(F32)
    for hd in range(SGU_HEADS):
        ws_h = (ws_ref[hd] * tril).astype(BF16)
        cols = slice(hd * SGU_HEAD_DIM, (hd + 1) * SGU_HEAD_DIM)
        for c in range(n_rows // CHUNK):
            rows = slice(c * CHUNK, (c + 1) * CHUNK)
            stat_scr[rows, cols] = _dot(ws_h, v_scr[rows, cols])
    beta_b = vec_ref[6:7, :]

    def gate_body(r0):
        r = pl.ds(r0, ROW_CHUNK)
        rb = pl.ds(pl.multiple_of(lax.rem(r0, CHUNK), ROW_CHUNK), ROW_CHUNK)
        yb = u_scr[r, :] * (stat_scr[r, :] + bs_ref[rb, :])
        y_scr[r, C_CONV:C_CONV + C_SGU] = (_rms(yb) * beta_b).astype(BF16)
    _row_loop(n_rows, gate_body)

    _phase_out(x_ref, y_scr, wout_ref, z_scr, wr_ref, br_ref, mod, h_scr,
               xmid_ref, h2_ref, logit_ref, n_rows)


def _mixer_sample_kernel(*refs, fused):
    x_ref, mod_ref = refs[:2]
    refs = refs[2:]
    if fused:
        yg_ref, wgt_ref, gate_ref = refs[:3]
        refs = refs[3:]
        x_scr = refs[-1]
        refs = refs[:-1]
    (win_ref, wout_ref, wconv_ref, vec_ref, g_ref, wsv_ref, bsv_ref, wr_ref, br_ref, state_ref,
     xmid_ref, h2_ref, logit_ref, anew_ref, vout_ref,
     h_scr, z_scr, conv_scr, convb_scr, stat_scr, u_scr, v_scr, y_scr) = refs
    n_rows = x_ref.shape[0]
    n_seq = state_ref.shape[1]
    n_pos = n_rows // n_seq

    def mod(idx, r0):
        rs = pl.ds(pl.multiple_of(lax.rem(r0, n_seq), ROW_CHUNK), ROW_CHUNK)
        return mod_ref[rs, idx * D_MODEL:(idx + 1) * D_MODEL]

    if fused:
        def gate(r0):
            return gate_ref[pl.ds(pl.multiple_of(lax.rem(r0, n_seq), COMBINE_CHUNK), COMBINE_CHUNK), :]
        _phase_moe_residual(x_ref, yg_ref, wgt_ref, gate, x_scr, n_rows)
        x_ref = x_scr

    _phase_in_norm(x_ref, h_scr, mod, n_rows)
    z_scr[...] = _dot(h_scr[...], win_ref[...])
    _phase_gates(z_scr, vec_ref, anew_ref, 0, u_scr, v_scr, vout_ref, n_rows)

    b_conv = vec_ref[0:1, :]

    def conv_body(s0):
        rs = pl.ds(s0, CONV_CHUNK)
        for t in range(n_pos):
            acc = jnp.zeros((CONV_CHUNK, C_CONV), F32)
            for k in range(CONV_WIDTH):
                p = t + k
                if p < CONV_STATE:
                    src = state_ref[p, rs, :]
                else:
                    src = anew_ref[pl.ds(s0 + (p - CONV_STATE) * n_seq, CONV_CHUNK), :]
                acc = acc + wconv_ref[k:k + 1, :] * src
            conv = acc + b_conv
            r = pl.ds(s0 + t * n_seq, CONV_CHUNK)
            conv_scr[r, :] = conv
            convb_scr[r, :] = conv.astype(BF16)
    _row_loop(n_seq, conv_body, CONV_CHUNK)

    _phase_group_norm(conv_scr, convb_scr, stat_scr, g_ref, vec_ref, y_scr, n_rows)

    beta_b = vec_ref[6:7, :]

    def gate_body(s0):
        for t in range(n_pos):
            s = jnp.zeros((CONV_CHUNK, C_SGU), F32)
            for jj in range(t + 1):
                s = s + wsv_ref[t * n_pos + jj:t * n_pos + jj + 1, :] * v_scr[pl.ds(s0 + jj * n_seq, CONV_CHUNK), :]
            r = pl.ds(s0 + t * n_seq, CONV_CHUNK)
            yb = u_scr[r, :] * (s + bsv_ref[t:t + 1, :])
            y_scr[r, C_CONV:C_CONV + C_SGU] = (_rms(yb) * beta_b).astype(BF16)
    _row_loop(n_seq, gate_body, CONV_CHUNK)

    _phase_out(x_ref, y_scr, wout_ref, z_scr, wr_ref, br_ref, mod, h_scr,
               xmid_ref, h2_ref, logit_ref, n_rows)


def _const_spec(shape):
    nd = len(shape)
    return pl.BlockSpec(shape, lambda *_: (0,) * nd, pipeline_mode=pl.Buffered(1))


def _const_out_spec(shape):
    nd = len(shape)
    return pl.BlockSpec(shape, lambda *_: (0,) * nd)


def _mixer(x_p, x_s, x_s_block, moe, mod_p, mod_s, w_in_b, w_out_b, wconv_p, vec, g_mat, w_s, bs_full,
           wsv, bsv, w_router, b_router, state_t, n_prompt_seq, seq_len):
    fused = moe is not None
    n_prompt = n_prompt_seq * seq_len
    n_seq, n_pos = state_t.shape[1], bsv.shape[0]
    n_sample = n_seq * n_pos
    t_all = n_prompt + n_sample
    nj = seq_len // ROW_TILE
    weight_specs = [
        _const_spec((D_MODEL, 2 * C_CONV + 2 * C_SGU)),
        _const_spec((D_MODEL, D_MODEL)),
        _const_spec((CARRY_ROWS, C_CONV)),
        _const_spec((8, C_CONV)),
        _const_spec((C_CONV, C_CONV)),
    ]
    router_specs = [_const_spec((N_EXPERTS, D_MODEL)), _const_spec((N_EXPERTS, 1))]
    common_scratch = lambda n: [
        pltpu.VMEM((n, D_MODEL), BF16),
        pltpu.VMEM((n, 2 * C_CONV + 2 * C_SGU), F32),
    ]
    tail_scratch = lambda n, vdt: [
        pltpu.VMEM((n, C_CONV), F32),
        pltpu.VMEM((n, C_CONV), BF16),
        pltpu.VMEM((n, C_CONV), F32),
        pltpu.VMEM((n, C_SGU), F32),
        pltpu.VMEM((n, C_SGU), vdt),
        pltpu.VMEM((n, D_MODEL), BF16),
    ]
    assert n_sample == ROW_TILE
    n_tiles_p = n_prompt // ROW_TILE
    tile_p = lambda i: jnp.minimum(i, n_tiles_p - 1)
    seq_of = lambda i: jnp.minimum(i // nj, n_prompt_seq - 1)
    moe_s_specs, moe_p_specs, moe_s_args, moe_p_args, x_scratch = [], [], [], [], []
    if fused:
        yg, wgt, gate_p, gate_s = moe
        moe_s_specs = [pl.BlockSpec((TOP_K, n_sample, D_WORDS), lambda i: (0, x_s_block, 0)),
                       pl.BlockSpec((n_sample, LANES), lambda i: (x_s_block, 0)),
                       _const_spec((n_seq, D_MODEL))]
        moe_p_specs = [pl.BlockSpec((TOP_K, ROW_TILE, D_WORDS), lambda i: (0, tile_p(i), 0)),
                       pl.BlockSpec((ROW_TILE, LANES), lambda i: (tile_p(i), 0)),
                       pl.BlockSpec((1, 1, D_MODEL), lambda i: (seq_of(tile_p(i)), 0, 0))]
        moe_s_args = [yg, wgt, gate_s]
        moe_p_args = [yg, wgt, gate_p.reshape(n_prompt_seq, 1, D_MODEL)]
        x_scratch = [pltpu.VMEM((ROW_TILE, D_MODEL), F32)]
    smid, sh2, slogits, a_new, v_new = pl.pallas_call(
        functools.partial(_mixer_sample_kernel, fused=fused),
        grid=(1,),
        in_specs=[
            pl.BlockSpec((n_sample, D_MODEL), lambda i: (x_s_block, 0)),
            _const_spec((n_seq, N_MOD * D_MODEL)),
            *moe_s_specs,
            *weight_specs,
            _const_spec((n_pos * n_pos, C_SGU)),
            _const_spec((n_pos, C_SGU)),
            *router_specs,
            _const_spec((CONV_STATE, n_seq, C_CONV)),
        ],
        out_specs=(
            _const_out_spec((n_sample, D_MODEL)),
            _const_out_spec((n_sample, D_WORDS)),
            _const_out_spec((N_EXPERTS, n_sample)),
            _const_out_spec((n_sample, C_CONV)),
            _const_out_spec((n_sample, C_SGU)),
        ),
        out_shape=(
            jax.ShapeDtypeStruct((n_sample, D_MODEL), F32),
            jax.ShapeDtypeStruct((n_sample, D_WORDS), jnp.int32),
            jax.ShapeDtypeStruct((N_EXPERTS, n_sample), F32),
            jax.ShapeDtypeStruct((n_sample, C_CONV), F32),
            jax.ShapeDtypeStruct((n_sample, C_SGU), F32),
        ),
        scratch_shapes=common_scratch(n_sample) + tail_scratch(n_sample, F32) + x_scratch,
        compiler_params=pltpu.CompilerParams(
            dimension_semantics=("arbitrary",), vmem_limit_bytes=VMEM_LIMIT),
        name="mixer_sample",
    )(x_s, mod_s, *moe_s_args, w_in_b, w_out_b, wconv_p, vec, g_mat, wsv, bsv, w_router, b_router,
      state_t)

    xmid, h2, logits, cst = pl.pallas_call(
        functools.partial(_mixer_prompt_kernel, tiles_per_seq=nj, n_prompt_tiles=n_tiles_p,
                          fused=fused),
        grid=(n_tiles_p + 1,),
        in_specs=[
            pl.BlockSpec((ROW_TILE, D_MODEL), lambda i: (tile_p(i), 0)),
            pl.BlockSpec((1, 1, N_MOD * D_MODEL), lambda i: (seq_of(i), 0, 0)),
            *moe_p_specs,
            *weight_specs,
            _const_spec((SGU_HEADS, CHUNK, CHUNK)),
            _const_spec((CHUNK, C_SGU)),
            *router_specs,
            _const_spec((n_sample, D_MODEL)),
            _const_spec((n_sample, D_WORDS)),
            _const_spec((N_EXPERTS, n_sample)),
        ],
        out_specs=(
            pl.BlockSpec((ROW_TILE, D_MODEL), lambda i: (i, 0)),
            pl.BlockSpec((ROW_TILE, D_WORDS), lambda i: (i, 0)),
            pl.BlockSpec((N_EXPERTS, ROW_TILE), lambda i: (0, i)),
            pl.BlockSpec((1, CONV_STATE, C_CONV), lambda i: (seq_of(i), 0, 0)),
        ),
        out_shape=(
            jax.ShapeDtypeStruct((t_all, D_MODEL), F32),
            jax.ShapeDtypeStruct((t_all, D_WORDS), jnp.int32),
            jax.ShapeDtypeStruct((N_EXPERTS, t_all), F32),
            jax.ShapeDtypeStruct((n_prompt_seq, CONV_STATE, C_CONV), F32),
        ),
        scratch_shapes=common_scratch(ROW_TILE)
        + [pltpu.VMEM((SUBLANES, CARRY_ROWS + ROW_TILE, C_CONV), F32)]
        + tail_scratch(ROW_TILE, BF16) + x_scratch,
        compiler_params=pltpu.CompilerParams(
            dimension_semantics=("arbitrary",), vmem_limit_bytes=VMEM_LIMIT),
        name="mixer_prompt",
    )(x_p, mod_p, *moe_p_args, w_in_b, w_out_b, wconv_p, vec, g_mat, w_s, bs_full, w_router, b_router,
      smid, sh2, slogits)
    return xmid, h2, logits, cst, a_new, v_new


def _tile_plan(cnt):
    e_sub = lax.broadcasted_iota(jnp.int32, (N_EXPERTS, N_EXPERTS), 0)
    e_lane = lax.broadcasted_iota(jnp.int32, (N_EXPERTS, N_EXPERTS), 1)
    tiles = jnp.floor((cnt + (GEMM_TILE - 1.0)) * (1.0 / GEMM_TILE))
    active = jnp.where(cnt > 0.0, 1.0, 0.0)

    def over_experts(tri, col):
        return _dot(tri.astype(BF16), jnp.broadcast_to(col, (N_EXPERTS, LANES)).astype(BF16))[:, 0:1]
    tile_end = over_experts(jnp.where(e_lane <= e_sub, 1.0, 0.0), tiles)
    tile_start = tile_end - tiles
    ordinal = over_experts(jnp.where(e_lane < e_sub, 1.0, 0.0), active)

    g = lax.broadcasted_iota(jnp.int32, (1, PLAN_LANES), 1).astype(F32)
    e_id = lax.broadcasted_iota(jnp.int32, (N_EXPERTS, PLAN_LANES), 0).astype(F32)
    done = jnp.sum(jnp.where(tile_end <= g, 1.0, 0.0), axis=0, keepdims=True)
    live = done < float(N_EXPERTS)
    last_e = jnp.max(jnp.where(cnt > 0.0, e_id[:, 0:1], 0.0), axis=0, keepdims=True)
    te = jnp.where(live, done, last_e)
    pick = e_id == te

    def at_tile(col):
        return jnp.sum(jnp.where(pick, col, 0.0), axis=0, keepdims=True)
    start_g = at_tile(tile_start)
    ord_g = at_tile(ordinal)
    valid = jnp.clip(at_tile(cnt) - (g - start_g) * GEMM_TILE, 0.0, float(GEMM_TILE))
    valid = jnp.where(live, valid, 0.0)
    first = jnp.where(live, jnp.where(g == start_g, 1.0, 0.0), 0.0)
    slot = ord_g - 2.0 * jnp.floor(ord_g * 0.5)
    later = jnp.where(e_id > te, jnp.where(cnt > 0.0, e_id, float(N_EXPERTS)), float(N_EXPERTS))
    nxt = jnp.min(later, axis=0, keepdims=True)
    nxt = jnp.where(nxt >= float(N_EXPERTS), -1.0, nxt)

    row = lax.broadcasted_iota(jnp.int32, (PLAN_ROWS, PLAN_LANES), 0)
    plan = jnp.zeros((PLAN_ROWS, PLAN_LANES), F32)
    for r, v in ((PLAN_EXPERT, te), (PLAN_VALID, valid), (PLAN_FIRST, first), (PLAN_SLOT, slot),
                 (PLAN_NEXT, nxt)):
        plan = jnp.where(row == r, v, plan)
    return plan.astype(jnp.int32), tile_start * GEMM_TILE


def _router_kernel(lg_ref, wgt_ref, pos_ref, plan_ref, cnt_scr, run_scr, start_scr, earlier_scr):
    ph = pl.program_id(0)
    i = pl.program_id(1)
    n = lg_ref.shape[1]

    @pl.when(jnp.logical_and(ph == 0, i == 0))
    def _():
        cnt_scr[...] = jnp.zeros_like(cnt_scr)

    lg = lg_ref[...]
    sub = lax.broadcasted_iota(jnp.int32, lg.shape, 0).astype(F32)
    vals, sels = [], []
    for _ in range(TOP_K):
        m = jnp.max(lg, axis=0, keepdims=True)
        idx = jnp.min(jnp.where(lg == m, sub, float(N_EXPERTS)), axis=0, keepdims=True)
        sel = sub == idx
        vals.append(m)
        sels.append(sel)
        lg = jnp.where(sel, -jnp.inf, lg)
    onehot = jnp.zeros(lg.shape, F32)
    for sel in sels:
        onehot = onehot + jnp.where(sel, 1.0, 0.0)
    tile_cnt = jnp.sum(onehot, axis=1, keepdims=True)

    @pl.when(ph == 0)
    def _():
        cnt_scr[...] = cnt_scr[...] + tile_cnt

    @pl.when(jnp.logical_and(ph == 1, i == 0))
    def _():
        plan, row_start = _tile_plan(cnt_scr[...])
        plan_ref[...] = plan
        start_scr[...] = row_start
        run_scr[...] = jnp.zeros_like(run_scr)
        t_row = lax.broadcasted_iota(jnp.int32, (n, n), 0)
        t_col = lax.broadcasted_iota(jnp.int32, (n, n), 1)
        earlier_scr[...] = jnp.where(t_row < t_col, 1.0, 0.0).astype(BF16)

    @pl.when(ph == 1)
    def _():
        exps = [jnp.exp(v - vals[0]) for v in vals]
        inv = 1.0 / (exps[0] + exps[1] + exps[2] + exps[3])
        w_row = lax.broadcasted_iota(jnp.int32, (LANES, n), 0)
        w_all = jnp.zeros((LANES, n), F32)
        for k in range(TOP_K):
            w_all = jnp.where(w_row == k, exps[k] * inv, w_all)
        wgt_ref[...] = w_all.T
        slot0 = _dot(onehot.astype(BF16), earlier_scr[...]) + (run_scr[...] + start_scr[...])
        for k in range(TOP_K):
            pos_k = jnp.sum(jnp.where(sels[k], slot0, 0.0), axis=0, keepdims=True)
            pos_ref[k:k + 1, :] = pos_k.astype(jnp.int32)
        run_scr[...] = run_scr[...] + tile_cnt


def _router(logits):
    t_all = logits.shape[1]
    col = pltpu.VMEM((N_EXPERTS, 1), F32)
    return pl.pallas_call(
        _router_kernel,
        grid=(2, t_all // ROUTER_TILE),
        in_specs=[pl.BlockSpec((N_EXPERTS, ROUTER_TILE), lambda ph, i: (0, i))],
        out_specs=(
            pl.BlockSpec((ROUTER_TILE, LANES), lambda ph, i: (i * ph, 0)),
            pl.BlockSpec((TOP_K, ROUTER_TILE), lambda ph, i: (0, i * ph)),
            _const_out_spec((PLAN_ROWS, PLAN_LANES)),
        ),
        out_shape=(
            jax.ShapeDtypeStruct((t_all, LANES), F32),
            jax.ShapeDtypeStruct((TOP_K, t_all), jnp.int32),
            jax.ShapeDtypeStruct((PLAN_ROWS, PLAN_LANES), jnp.int32),
        ),
        scratch_shapes=[col, col, col, pltpu.VMEM((ROUTER_TILE, ROUTER_TILE), BF16)],
        compiler_params=pltpu.CompilerParams(
            dimension_semantics=("arbitrary", "arbitrary"), vmem_limit_bytes=VMEM_LIMIT),
        name="router",
    )(logits)


def _expert_kernel(plan_ref, x_ref, w1_hbm, b1_ref, w2_hbm, b2_ref, o_ref,
                   w1f_scr, w2f_scr, w1b_scr, w2b_scr, sem, *, layer):
    def weight_copies(expert, slot):
        return (pltpu.make_async_copy(w1_hbm.at[layer, expert], w1f_scr.at[slot], sem.at[0, slot]),
                pltpu.make_async_copy(w2_hbm.at[layer, expert], w2f_scr.at[slot], sem.at[1, slot]))

    for sub in range(TILES_PER_STEP):
        _expert_tile(pl.program_id(0) * TILES_PER_STEP + sub, sub * GEMM_TILE, plan_ref, x_ref,
                     b1_ref, b2_ref, o_ref, w1f_scr, w2f_scr, w1b_scr, w2b_scr, weight_copies)


def _expert_tile(g, row0, plan_ref, x_ref, b1_ref, b2_ref, o_ref, w1f_scr, w2f_scr, w1b_scr, w2b_scr,
                 weight_copies):
    nv = plan_ref[PLAN_VALID, g]
    expert = plan_ref[PLAN_EXPERT, g]

    @pl.when(plan_ref[PLAN_FIRST, g] == 1)
    def _():
        slot = plan_ref[PLAN_SLOT, g]
        nxt = plan_ref[PLAN_NEXT, g]

        @pl.when(g == 0)
        def _():
            for cp in weight_copies(expert, slot):
                cp.start()
        for cp in weight_copies(expert, slot):
            cp.wait()

        @pl.when(nxt >= 0)
        def _():
            for cp in weight_copies(nxt, 1 - slot):
                cp.start()
        w1b_scr[...] = w1f_scr[slot].astype(BF16)
        w2b_scr[...] = w2f_scr[slot].astype(BF16)

    def ffn(m):
        x_left, x_right = _unpack_bf16_pair(x_ref[row0:row0 + m, :])
        x = jnp.concatenate([x_left.astype(BF16), x_right.astype(BF16)], axis=1)

        def hidden(col0):
            cols = slice(col0, col0 + FF_BLOCK)
            return _dot(x, w1b_scr[:, cols]) + b1_ref[expert, :, cols]

        y = jnp.zeros((m, D_MODEL), F32) + b2_ref[expert]
        for jb in range(D_FF // FF_BLOCK):
            gg = jnp.minimum(hidden(jb * FF_BLOCK), SWIGLU_LIMIT)
            up = jnp.clip(hidden(D_FF + jb * FF_BLOCK), -SWIGLU_LIMIT, SWIGLU_LIMIT)
            act = gg * _sigmoid(SWIGLU_ALPHA * gg) * (up + 1.0)
            y = y + _dot(act.astype(BF16), w2b_scr[jb * FF_BLOCK:(jb + 1) * FF_BLOCK, :])
        row = lax.broadcasted_iota(jnp.int32, y.shape, 0)
        o_ref[row0:row0 + m, :] = _pack_bf16_pair(jnp.where(row < nv, y, 0.0))
        if m < GEMM_TILE:
            o_ref[row0 + m:row0 + GEMM_TILE, :] = jnp.zeros((GEMM_TILE - m, D_WORDS), jnp.int32)

    @pl.when(nv > GEMM_TILE // 2)
    def _():
        ffn(GEMM_TILE)

    @pl.when(jnp.logical_and(nv > 0, nv <= GEMM_TILE // 2))
    def _():
        ffn(GEMM_TILE // 2)

    @pl.when(nv == 0)
    def _():
        o_ref[row0:row0 + GEMM_TILE, :] = jnp.zeros((GEMM_TILE, D_WORDS), jnp.int32)


def _experts(xs, plan, w1, b1, w2, b2, layer):
    n_tiles = xs.shape[0] // GEMM_TILE
    depth = w1.shape[0]
    assert n_tiles % TILES_PER_STEP == 0
    step_rows = TILES_PER_STEP * GEMM_TILE
    bias_map = lambda s, plan: (layer, 0, 0, 0)
    grid_spec = pltpu.PrefetchScalarGridSpec(
        num_scalar_prefetch=1,
        grid=(n_tiles // TILES_PER_STEP,),
        in_specs=[
            pl.BlockSpec((step_rows, D_WORDS), lambda s, plan: (s, 0)),
            pl.BlockSpec(memory_space=pl.ANY),
            pl.BlockSpec((None, N_EXPERTS, 1, 2 * D_FF), bias_map, pipeline_mode=pl.Buffered(1)),
            pl.BlockSpec(memory_space=pl.ANY),
            pl.BlockSpec((None, N_EXPERTS, 1, D_MODEL), bias_map, pipeline_mode=pl.Buffered(1)),
        ],
        out_specs=pl.BlockSpec((step_rows, D_WORDS), lambda s, plan: (s, 0)),
        scratch_shapes=[
            pltpu.VMEM((2, D_MODEL, 2 * D_FF), F32),
            pltpu.VMEM((2, D_FF, D_MODEL), F32),
            pltpu.VMEM((D_MODEL, 2 * D_FF), BF16),
            pltpu.VMEM((D_FF, D_MODEL), BF16),
            pltpu.SemaphoreType.DMA((2, 2)),
        ],
    )
    return pl.pallas_call(
        functools.partial(_expert_kernel, layer=layer),
        grid_spec=grid_spec,
        out_shape=jax.ShapeDtypeStruct((n_tiles * GEMM_TILE, D_WORDS), jnp.int32),
        compiler_params=pltpu.CompilerParams(
            dimension_semantics=("arbitrary",), vmem_limit_bytes=VMEM_LIMIT),
        name="experts",
    )(plan, xs, w1, b1.reshape(depth, N_EXPERTS, 1, 2 * D_FF), w2,
      b2.reshape(depth, N_EXPERTS, 1, D_MODEL))


def _final_kernel(x_ref, yg_ref, wgt_ref, gp_ref, gs_ref, fg_ref, op_ref, os_ref, *, n_prompt_blocks):
    i = pl.program_id(0)
    n_seq = gs_ref.shape[0]

    def run(gate, o_ref):
        def body(r0):
            r = pl.ds(r0, COMBINE_CHUNK)
            w = wgt_ref[r, :]
            g = gate(r0)
            xl = x_ref[r, 0:D_WORDS] + g[:, 0:D_WORDS] * _moe_half(yg_ref, w, r, False)
            xr = x_ref[r, D_WORDS:D_MODEL] + g[:, D_WORDS:D_MODEL] * _moe_half(yg_ref, w, r, True)
            ms = (jnp.sum(xl * xl, axis=-1, keepdims=True)
                  + jnp.sum(xr * xr, axis=-1, keepdims=True)) * (1.0 / D_MODEL)
            scale = lax.rsqrt(ms + EPS)
            o_ref[r, 0:D_WORDS] = xl * scale * fg_ref[:, 0:D_WORDS]
            o_ref[r, D_WORDS:D_MODEL] = xr * scale * fg_ref[:, D_WORDS:D_MODEL]
        _row_loop(x_ref.shape[0], body, COMBINE_CHUNK)

    @pl.when(i < n_prompt_blocks)
    def _():
        run(lambda r0: gp_ref[0], op_ref)

    @pl.when(i >= n_prompt_blocks)
    def _():
        run(lambda r0: gs_ref[pl.ds(pl.multiple_of(lax.rem(r0, n_seq), COMBINE_CHUNK), COMBINE_CHUNK), :],
            os_ref)


def _final(xmid, yg, wgt, gate_p, gate_s, final_g, n_prompt, seq_len):
    t_all = xmid.shape[0]
    n_prompt_blocks = n_prompt // ROW_TILE
    per_seq = seq_len // ROW_TILE
    n_seq_p = gate_p.shape[0]
    assert t_all - n_prompt == ROW_TILE
    out_specs = (
        pl.BlockSpec((ROW_TILE, D_MODEL), lambda i: (jnp.minimum(i, n_prompt_blocks - 1), 0)),
        pl.BlockSpec((ROW_TILE, D_MODEL), lambda i: (0, 0)),
    )
    out_shape = (jax.ShapeDtypeStruct((n_prompt, D_MODEL), F32),
                 jax.ShapeDtypeStruct((ROW_TILE, D_MODEL), F32))
    return pl.pallas_call(
        functools.partial(_final_kernel, n_prompt_blocks=n_prompt_blocks),
        grid=(t_all // ROW_TILE,),
        in_specs=[
            pl.BlockSpec((ROW_TILE, D_MODEL), lambda i: (i, 0)),
            pl.BlockSpec((TOP_K, ROW_TILE, D_WORDS), lambda i: (0, i, 0)),
            pl.BlockSpec((ROW_TILE, LANES), lambda i: (i, 0)),
            pl.BlockSpec((1, 1, D_MODEL), lambda i: (jnp.minimum(i // per_seq, n_seq_p - 1), 0, 0)),
            _const_spec(gate_s.shape),
            _const_spec((1, D_MODEL)),
        ],
        out_specs=out_specs,
        out_shape=out_shape,
        compiler_params=pltpu.CompilerParams(
            dimension_semantics=("arbitrary",), vmem_limit_bytes=VMEM_LIMIT),
        name="final_combine",
    )(xmid, yg, wgt, gate_p.reshape(n_seq_p, 1, D_MODEL), gate_s, final_g.reshape(1, D_MODEL))


def _sc_mesh():
    return plsc.VectorSubcoreMesh(core_axis_name="core", subcore_axis_name="subcore",
                                  num_cores=SC_CORES, num_subcores=SC_SUBCORES)


def _sc_worker_id():
    return lax.axis_index("subcore") * SC_CORES + lax.axis_index("core")


def _sc_dispatch(h2w, pos_c, n_slots):
    n_chunks = pos_c.shape[1]

    @functools.partial(
        pl.kernel, mesh=_sc_mesh(),
        out_type=jax.ShapeDtypeStruct((n_slots, D_WORDS), jnp.int32),
        scratch_types=[pltpu.VMEM((TOP_K, SC_CHUNK), jnp.int32),
                       pltpu.VMEM((SC_CHUNK, D_WORDS), jnp.int32),
                       pltpu.SemaphoreType.DMA],
        name="moe_dispatch")
    def run(h_hbm, pos_hbm, out_hbm, idx_v, rows_v, sem):
        wid = _sc_worker_id()

        @pl.loop(0, n_chunks)
        def _(c):
            base = pl.multiple_of((wid * n_chunks + c) * SC_CHUNK, SUBLANES)
            pltpu.sync_copy(pos_hbm.at[wid, c], idx_v)
            pltpu.sync_copy(h_hbm.at[pl.ds(base, SC_CHUNK)], rows_v)
            copies = [pltpu.async_copy(rows_v, out_hbm.at[idx_v.at[k]], sem) for k in range(TOP_K)]
            for cp in copies:
                cp.wait()

    return run(h2w, pos_c)


def _sc_gather_back(ys, pos_c, t_all):
    n_chunks = pos_c.shape[1]

    @functools.partial(
        pl.kernel, mesh=_sc_mesh(),
        out_type=jax.ShapeDtypeStruct((TOP_K, t_all, D_WORDS), jnp.int32),
        scratch_types=[pltpu.VMEM((TOP_K, SC_CHUNK), jnp.int32),
                       pltpu.VMEM((TOP_K, SC_CHUNK, D_WORDS), jnp.int32),
                       pltpu.SemaphoreType.DMA],
        name="moe_gather_back")
    def run(ys_hbm, pos_hbm, out_hbm, idx_v, rows_v, sem):
        wid = _sc_worker_id()

        @pl.loop(0, n_chunks)
        def _(c):
            base = pl.multiple_of((wid * n_chunks + c) * SC_CHUNK, SUBLANES)
            pltpu.sync_copy(pos_hbm.at[wid, c], idx_v)
            copies = [pltpu.async_copy(ys_hbm.at[idx_v.at[k]], rows_v.at[k], sem) for k in range(TOP_K)]
            for cp in copies:
                cp.wait()
            for k in range(TOP_K):
                pltpu.sync_copy(rows_v.at[k], out_hbm.at[k, pl.ds(base, SC_CHUNK)])

    return run(ys, pos_c)


def kernel(x_prompt, x_sample, state_conv, c_prompt, c_sample, w_ada, b_ada, w_in, w_conv, b_conv,
           gn_g, gn_b, sgu_ln_g, sgu_ln_b, w_s, b_s, beta_a, beta_b, w_out, w_router, b_router,
           w1, b1, w2, b2, final_g):
    n_bp, seq_len, _ = x_prompt.shape
    n_bs, n_pos, _ = x_sample.shape
    depth = w_ada.shape[0]
    n_prompt = n_bp * seq_len
    n_sample = n_bs * n_pos
    t_all = n_prompt + n_sample
    n_tiles = (t_all * TOP_K) // GEMM_TILE + N_EXPERTS
    n_workers = SC_CORES * SC_SUBCORES
    assert t_all % (n_workers * SC_CHUNK) == 0 and n_tiles <= PLAN_LANES and t_all % ROUTER_TILE == 0

    x_p = x_prompt.reshape(n_prompt, D_MODEL)
    x_s = x_sample.transpose(1, 0, 2).reshape(n_sample, D_MODEL)
    x_s_block = 0
    moe = None
    mods =_adaln(jnp.concatenate([c_prompt, c_sample], axis=0), w_ada, b_ada)
    state_t = state_conv.transpose(0, 2, 1, 3)
    grp = jnp.arange(C_CONV) // (C_CONV // CONV_GROUPS)
    g_mat = ((grp[:, None] == grp[None, :]).astype(F32) / (C_CONV // CONV_GROUPS)).astype(BF16)

    conv_p, conv_s, v_s = [], [], []
    for l in range(depth):
        mod_p = mods[l, :n_bp].reshape(n_bp, 1, N_MOD * D_MODEL)
        mod_s = mods[l, n_bp:]
        wconv_p = jnp.pad(w_conv[l], ((0, CARRY_ROWS - CONV_WIDTH), (0, 0)))
        zero = jnp.zeros_like(b_conv[l])
        vec = jnp.stack([b_conv[l], gn_g[l], gn_b[l], sgu_ln_g[l], sgu_ln_b[l],
                         beta_a[l], beta_b[l], zero], axis=0)
        bs_full = jnp.repeat(b_s[l].T, SGU_HEAD_DIM, axis=1)
        wsv = jnp.repeat(w_s[l][:, :n_pos, :n_pos].transpose(1, 2, 0).reshape(n_pos * n_pos, SGU_HEADS),
                         SGU_HEAD_DIM, axis=1)
        bsv = jnp.repeat(b_s[l][:, :n_pos].T, SGU_HEAD_DIM, axis=1)
        xmid, h2, logits, cst, a_new, v_new = _mixer(
            x_p, x_s, x_s_block, moe, mod_p, mod_s, w_in[l].astype(BF16), w_out[l].astype(BF16), wconv_p,
            vec, g_mat,
            w_s[l], bs_full, wsv, bsv, w_router[l].T, b_router[l].reshape(N_EXPERTS, 1),
            state_t[l], n_bp, seq_len)
        conv_p.append(cst)
        a_new = a_new.reshape(n_pos, n_bs, C_CONV).transpose(1, 0, 2)
        conv_s.append(jnp.concatenate([state_conv[l], a_new], axis=1)[:, -CONV_STATE:])
        v_s.append(v_new.reshape(n_pos, n_bs, C_SGU).transpose(1, 0, 2))

        wgt, pos, plan = _router(logits)
        pos_c = pos.reshape(TOP_K, n_workers, -1, SC_CHUNK).transpose(1, 2, 0, 3)
        xs = _sc_dispatch(h2, pos_c, n_tiles * GEMM_TILE)
        ys = _experts(xs, plan, w1, b1, w2, b2, l)
        yg = _sc_gather_back(ys, pos_c, t_all)
        moe = (yg, wgt, mod_p[:, 0, 5 * D_MODEL:], mod_s[:, 5 * D_MODEL:])
        x_p = x_s = xmid
        x_s_block = n_prompt // n_sample

    y_p, y_s = _final(xmid, *moe, final_g, n_prompt, seq_len)
    y_prompt = y_p.reshape(n_bp, seq_len, D_MODEL)
    y_sample = y_s.reshape(n_pos, n_bs, D_MODEL).transpose(1, 0, 2)
    return (y_prompt, y_sample, jnp.stack(conv_p, axis=0), jnp.stack(conv_s, axis=0),
            jnp.stack(v_s, axis=0))
```

```python
import functools
import math

import jax
import jax.numpy as jnp
from jax import lax
from jax.experimental import pallas as pl
from jax.experimental.pallas import tpu as pltpu
from jax.experimental.pallas import tpu_sc as plsc

F32 = jnp.float32
BF16 = jnp.bfloat16

D_MODEL = 1024
C_CONV = 512
C_SGU = 512
CONV_GROUPS = 8
CONV_WIDTH = 31
CONV_STATE = CONV_WIDTH - 1
SGU_HEADS = 4
SGU_HEAD_DIM = C_SGU // SGU_HEADS
CHUNK = 128
N_EXPERTS = 32
TOP_K = 4
D_FF = 1024
SWIGLU_LIMIT = 7.0
SWIGLU_ALPHA = 1.702
N_MOD = 6
EPS = 1e-5

ROW_TILE = 512
ROUTER_TILE = 1536
ROW_CHUNK = 128
CONV_CHUNK = 32
COMBINE_CHUNK = 64
CARRY_ROWS = 32
SUBLANES = 8
LANES = 128
PLAN_ROWS = 8
PLAN_LANES = 256
PLAN_EXPERT, PLAN_VALID, PLAN_FIRST, PLAN_SLOT, PLAN_NEXT = range(5)
GEMM_TILE = 512
FF_BLOCK = 256
TILES_PER_STEP = 2
VMEM_LIMIT = 56 * 1024 * 1024
D_WORDS = D_MODEL // 2
SC_CHUNK = 48
HI_MASK = -65536
SC_CORES = 2
SC_SUBCORES = 16


def _rms(x):
    return x * lax.rsqrt(jnp.mean(x * x, axis=-1, keepdims=True) + EPS)


def _gelu(x):
    return 0.5 * x * (1.0 + lax.erf(x * (1.0 / math.sqrt(2.0))))


def _sigmoid(x):
    return 1.0 / (1.0 + jnp.exp(-x))


def _split_bf16(x):
    hi = x.astype(BF16)
    lo = (x - hi.astype(F32)).astype(BF16)
    return hi, lo


def _pack_bf16_pair(x):
    bits = lax.bitcast_convert_type(x.astype(BF16).astype(F32), jnp.int32)
    return lax.shift_right_logical(bits[:, :D_WORDS], 16) | (bits[:, D_WORDS:] & HI_MASK)


def _unpack_bf16_pair(w):
    left = lax.bitcast_convert_type(lax.shift_left(w, 16), F32)
    right = lax.bitcast_convert_type(w & HI_MASK, F32)
    return left, right


def _dot(a, b):
    return jnp.dot(a, b, preferred_element_type=F32)


def _dot_nt(a, b):
    return lax.dot_general(a, b, (((1,), (1,)), ((), ())), preferred_element_type=F32)


def _row_loop(n_rows, body, chunk=None):
    chunk = ROW_CHUNK if chunk is None else chunk

    def step(i, carry):
        body(pl.multiple_of(i * chunk, chunk))
        return carry
    lax.fori_loop(0, n_rows // chunk, step, 0)


def _static_loop(n_rows, body, chunk=None):
    chunk = ROW_CHUNK if chunk is None else chunk
    for r0 in range(0, n_rows, chunk):
        body(r0)


def _adaln_kernel(c_ref, w_ref, b_ref, o_ref):
    c = c_ref[...]
    s_hi, s_lo = _split_bf16(c * _sigmoid(c))
    w_hi, w_lo = _split_bf16(w_ref[0])
    acc = _dot(s_hi, w_hi) + _dot(s_hi, w_lo) + _dot(s_lo, w_hi)
    o_ref[0] = acc + b_ref[0]


def _adaln(c_all, w_ada, b_ada):
    depth, _, n_out = w_ada.shape
    n_rows = c_all.shape[0]
    tn = 1024
    return pl.pallas_call(
        _adaln_kernel,
        grid=(depth, n_out // tn),
        in_specs=[
            pl.BlockSpec((n_rows, D_MODEL), lambda l, j: (0, 0)),
            pl.BlockSpec((1, D_MODEL, tn), lambda l, j: (l, 0, j)),
            pl.BlockSpec((1, 1, tn), lambda l, j: (l, 0, j)),
        ],
        out_specs=pl.BlockSpec((1, n_rows, tn), lambda l, j: (l, 0, j)),
        out_shape=jax.ShapeDtypeStruct((depth, n_rows, n_out), F32),
        compiler_params=pltpu.CompilerParams(
            dimension_semantics=("arbitrary", "arbitrary"), vmem_limit_bytes=VMEM_LIMIT),
        name="adaln",
    )(c_all, w_ada, b_ada.reshape(depth, 1, n_out))


def _phase_in_norm(x_ref, h_scr, mod, n_rows, loop=_row_loop):
    def body(r0):
        r = pl.ds(r0, ROW_CHUNK)
        h = _rms(x_ref[r, :]) * (1.0 + mod(1, r0)) + mod(0, r0)
        h_scr[r, :] = h.astype(BF16)
    loop(n_rows, body)


def _phase_glu(z_scr, a_dst, a_off, n_rows, loop=_row_loop):
    def body(r0):
        r = pl.ds(r0, ROW_CHUNK)
        a_dst[pl.ds(r0 + a_off, ROW_CHUNK), :] = z_scr[r, 0:C_CONV] * _sigmoid(z_scr[r, C_CONV:2 * C_CONV])
    loop(n_rows, body)


def _phase_uv(z_scr, vec_ref, u_scr, v_scr, vout_ref, n_rows, loop=_row_loop):
    ln_g = vec_ref[3:4, :]
    ln_b = vec_ref[4:5, :]

    def body(r0):
        r = pl.ds(r0, ROW_CHUNK)
        u_scr[r, :] = _gelu(z_scr[r, 2 * C_CONV:2 * C_CONV + C_SGU])
        gv = _gelu(z_scr[r, 2 * C_CONV + C_SGU:2 * C_CONV + 2 * C_SGU])
        mu = jnp.mean(gv, axis=-1, keepdims=True)
        dv = gv - mu
        var = jnp.mean(dv * dv, axis=-1, keepdims=True)
        v = dv * lax.rsqrt(var + EPS) * ln_g + ln_b
        if vout_ref is not None:
            vout_ref[r, :] = v
        v_scr[r, :] = v.astype(v_scr.dtype)
    loop(n_rows, body)


def _phase_group_norm(conv_scr, convb_scr, stat_scr, g_ref, vec_ref, y_scr, n_rows, loop=_row_loop):
    stat_scr[...] = _dot(convb_scr[...], g_ref[...])

    def center(r0):
        r = pl.ds(r0, ROW_CHUNK)
        d = conv_scr[r, :] - stat_scr[r, :]
        conv_scr[r, :] = d
        convb_scr[r, :] = (d * d).astype(BF16)
    loop(n_rows, center)
    stat_scr[...] = _dot(convb_scr[...], g_ref[...])
    gn_g = vec_ref[1:2, :]
    gn_b = vec_ref[2:3, :]
    beta_a = vec_ref[5:6, :]

    def finish(r0):
        r = pl.ds(r0, ROW_CHUNK)
        gn = conv_scr[r, :] * lax.rsqrt(stat_scr[r, :] + EPS) * gn_g + gn_b
        ya = gn * _sigmoid(gn)
        y_scr[r, 0:C_CONV] = (_rms(ya) * beta_a).astype(BF16)
    loop(n_rows, finish)


def _phase_out(x_ref, y_scr, wout_ref, z_scr, wr_ref, br_ref, mod, h_scr,
               xmid_ref, h2_ref, logit_ref, n_rows, loop=_row_loop):
    z_scr[:, 0:D_MODEL] = (_dot(y_scr[:, 0:C_CONV], wout_ref[0:C_CONV, :])
                           + _dot(y_scr[:, C_CONV:C_CONV + C_SGU], wout_ref[C_CONV:C_CONV + C_SGU, :]))

    def body(r0):
        r = pl.ds(r0, ROW_CHUNK)
        xm = x_ref[r, :] + mod(2, r0) * z_scr[r, 0:D_MODEL]
        xmid_ref[r, :] = xm
        h2 = _rms(xm) * (1.0 + mod(4, r0)) + mod(3, r0)
        hi, lo = _split_bf16(h2)
        h2_ref[r, :] = _pack_bf16_pair(h2)
        y_scr[r, :] = hi
        h_scr[r, :] = lo
    loop(n_rows, body)
    w_hi, w_lo = _split_bf16(wr_ref[...])
    both = _dot_nt(jnp.concatenate([w_hi, w_lo], axis=0), y_scr[...])
    logit_ref[...] = (both[0:N_EXPERTS] + both[N_EXPERTS:2 * N_EXPERTS]
                      + _dot_nt(w_hi, h_scr[...])) + br_ref[...]


def _moe_half(yg_ref, w, r, hi_half):
    moe = jnp.zeros((COMBINE_CHUNK, D_WORDS), F32)
    for k in range(TOP_K):
        word = yg_ref[k, r, :]
        bits = (word & HI_MASK) if hi_half else lax.shift_left(word, 16)
        moe = moe + w[:, k:k + 1] * lax.bitcast_convert_type(bits, F32)
    return moe


def _phase_moe_residual(xmid_ref, yg_ref, wgt_ref, gate, x_dst, n_rows, loop=_row_loop):
    def body(r0):
        r = pl.ds(r0, COMBINE_CHUNK)
        w = wgt_ref[r, :]
        g = gate(r0)
        for hi_half in (False, True):
            cols = slice(D_WORDS, D_MODEL) if hi_half else slice(0, D_WORDS)
            x_dst[r, cols] = xmid_ref[r, cols] + g[:, cols] * _moe_half(yg_ref, w, r, hi_half)
    loop(n_rows, body, COMBINE_CHUNK)


def _mixer_prompt_kernel(*refs, tiles_per_seq, n_prompt_tiles, fused):
    x_ref, mod_ref = refs[:2]
    refs = refs[2:]
    if fused:
        yg_ref, wgt_ref, gate_ref = refs[:3]
        refs = refs[3:]
        x_scr = refs[-1]
        refs = refs[:-1]
    (win_ref, wout_ref, wconv_ref, vec_ref, g_ref, ws_ref, bs_ref, wr_ref, br_ref,
     smid_ref, sh2_ref, slogit_ref, xmid_ref, h2_ref, logit_ref, cst_ref,
     h_scr, z_scr, aext_scr, conv_scr, convb_scr, stat_scr, u_scr, v_scr, y_scr) = refs
    i = pl.program_id(0)

    @pl.when(i == n_prompt_tiles)
    def _():
        xmid_ref[...] = smid_ref[...]
        h2_ref[...] = sh2_ref[...]
        logit_ref[...] = slogit_ref[...]

    @pl.when(i < n_prompt_tiles)
    def _():
        x_src = x_ref
        if fused:
            _phase_moe_residual(x_ref, yg_ref, wgt_ref, lambda r0: gate_ref[0], x_scr, ROW_TILE,
                                _static_loop)
            x_src = x_scr
        _mixer_prompt_tile(lax.rem(i, tiles_per_seq), tiles_per_seq,
                           x_src, mod_ref, win_ref, wout_ref, wconv_ref, vec_ref, g_ref,
                           ws_ref, bs_ref, wr_ref, br_ref, xmid_ref, h2_ref, logit_ref, cst_ref,
                           h_scr, z_scr, aext_scr, conv_scr, convb_scr, stat_scr, u_scr, v_scr, y_scr)


def _mixer_prompt_tile(j, tiles_per_seq, x_ref, mod_ref, win_ref, wout_ref, wconv_ref, vec_ref, g_ref,
                       ws_ref, bs_ref, wr_ref, br_ref, xmid_ref, h2_ref, logit_ref, cst_ref,
                       h_scr, z_scr, aext_scr, conv_scr, convb_scr, stat_scr, u_scr, v_scr, y_scr):
    n_rows = ROW_TILE

    def mod(idx, r0):
        del r0
        return mod_ref[0, :, idx * D_MODEL:(idx + 1) * D_MODEL]

    a_buf = aext_scr.at[0]

    @pl.when(j == 0)
    def _():
        a_buf[0:CARRY_ROWS, :] = jnp.zeros((CARRY_ROWS, C_CONV), F32)

    loop = _static_loop
    _phase_in_norm(x_ref, h_scr, mod, n_rows, loop)
    n_glu = 2 * C_CONV
    z_scr[:, 0:n_glu] = _dot(h_scr[...], win_ref[:, 0:n_glu])
    _phase_glu(z_scr, a_buf, CARRY_ROWS, n_rows, loop)
    z_scr[:, n_glu:] = _dot(h_scr[...], win_ref[:, n_glu:])
    n_shift = n_rows + CARRY_ROWS - SUBLANES
    for s in range(1, SUBLANES):
        aext_scr[s, 0:n_shift, :] = a_buf[s:s + n_shift, :]

    b_conv = vec_ref[0:1, :]
    lead = CARRY_ROWS - CONV_STATE
    for r0 in range(0, n_rows, CONV_CHUNK):
        acc = jnp.zeros((CONV_CHUNK, C_CONV), F32)
        for k in range(CONV_WIDTH):
            s = (lead + k) % SUBLANES
            q0 = r0 + (lead + k - s)
            acc = acc + wconv_ref[k:k + 1, :] * aext_scr[s, q0:q0 + CONV_CHUNK, :]
        conv = acc + b_conv
        conv_scr[r0:r0 + CONV_CHUNK, :] = conv
        convb_scr[r0:r0 + CONV_CHUNK, :] = conv.astype(BF16)
    _phase_uv(z_scr, vec_ref, u_scr, v_scr, None, n_rows, loop)
    _phase_group_norm(conv_scr, convb_scr, stat_scr, g_ref, vec_ref, y_scr, n_rows, loop)

    row_i = lax.broadcasted_iota(jnp.int32, (CHUNK, CHUNK), 0)
    col_i = lax.broadcasted_iota(jnp.int32, (CHUNK, CHUNK), 1)
    tril = (col_i <= row_i).astype(F32)
    for hd in range(SGU_HEADS):
        ws_h = (ws_ref[hd] * tril).astype(BF16)
        cols = slice(hd * SGU_HEAD_DIM, (hd + 1) * SGU_HEAD_DIM)
        s_cols = slice(n_glu + hd * SGU_HEAD_DIM, n_glu + (hd + 1) * SGU_HEAD_DIM)
        for c in range(n_rows // CHUNK):
            rows = slice(c * CHUNK, (c + 1) * CHUNK)
            z_scr[rows, s_cols] = _dot(ws_h, v_scr[rows, cols])
    beta_b = vec_ref[6:7, :]

    def gate_body(r0):
        r = pl.ds(r0, ROW_CHUNK)
        rb = pl.ds(r0 % CHUNK, ROW_CHUNK)
        yb = u_scr[r, :] * (z_scr[r, n_glu:n_glu + C_SGU] + bs_ref[rb, :])
        y_scr[r, C_CONV:C_CONV + C_SGU] = (_rms(yb) * beta_b).astype(BF16)
    loop(n_rows, gate_body)

    _phase_out(x_ref, y_scr, wout_ref, z_scr, wr_ref, br_ref, mod, h_scr,
               xmid_ref, h2_ref, logit_ref, n_rows, loop)

    @pl.when(j == tiles_per_seq - 1)
    def _():
        cst_ref[0] = a_buf[n_rows + lead:n_rows + CARRY_ROWS, :]

    a_buf[0:CARRY_ROWS, :] = a_buf[n_rows:n_rows + CARRY_ROWS, :]


def _mixer_sample_kernel(*refs, fused):
    x_ref, mod_ref = refs[:2]
    refs = refs[2:]
    if fused:
        yg_ref, wgt_ref, gate_ref = refs[:3]
        refs = refs[3:]
        x_scr = refs[-1]
        refs = refs[:-1]
    (win_ref, wout_ref, wconv_ref, vec_ref, g_ref, wsv_ref, bsv_ref, wr_ref, br_ref, state_ref,
     xmid_ref, h2_ref, logit_ref, anew_ref, vout_ref,
     h_scr, z_scr, conv_scr, convb_scr, stat_scr, u_scr, v_scr, y_scr) = refs
    n_rows = x_ref.shape[0]
    n_seq = state_ref.shape[1]
    n_pos = n_rows // n_seq

    def mod(idx, r0):
        rs = pl.ds(pl.multiple_of(lax.rem(r0, n_seq), ROW_CHUNK), ROW_CHUNK)
        return mod_ref[rs, idx * D_MODEL:(idx + 1) * D_MODEL]

    if fused:
        def gate(r0):
            return gate_ref[pl.ds(pl.multiple_of(lax.rem(r0, n_seq), COMBINE_CHUNK), COMBINE_CHUNK), :]
        _phase_moe_residual(x_ref, yg_ref, wgt_ref, gate, x_scr, n_rows)
        x_ref = x_scr

    _phase_in_norm(x_ref, h_scr, mod, n_rows)
    z_scr[...] = _dot(h_scr[...], win_ref[...])
    _phase_glu(z_scr, anew_ref, 0, n_rows)
    _phase_uv(z_scr, vec_ref, u_scr, v_scr, vout_ref, n_rows)

    b_conv = vec_ref[0:1, :]

    def conv_body(s0):
        rs = pl.ds(s0, CONV_CHUNK)
        for t in range(n_pos):
            acc = jnp.zeros((CONV_CHUNK, C_CONV), F32)
            for k in range(CONV_WIDTH):
                p = t + k
                if p < CONV_STATE:
                    src = state_ref[p, rs, :]
                else:
                    src = anew_ref[pl.ds(s0 + (p - CONV_STATE) * n_seq, CONV_CHUNK), :]
                acc = acc + wconv_ref[k:k + 1, :] * src
            conv = acc + b_conv
            r = pl.ds(s0 + t * n_seq, CONV_CHUNK)
            conv_scr[r, :] = conv
            convb_scr[r, :] = conv.astype(BF16)
    _row_loop(n_seq, conv_body, CONV_CHUNK)

    _phase_group_norm(conv_scr, convb_scr, stat_scr, g_ref, vec_ref, y_scr, n_rows)

    beta_b = vec_ref[6:7, :]

    def gate_body(s0):
        for t in range(n_pos):
            s = jnp.zeros((CONV_CHUNK, C_SGU), F32)
            for jj in range(t + 1):
                s = s + wsv_ref[t * n_pos + jj:t * n_pos + jj + 1, :] * v_scr[pl.ds(s0 + jj * n_seq, CONV_CHUNK), :]
            r = pl.ds(s0 + t * n_seq, CONV_CHUNK)
            yb = u_scr[r, :] * (s + bsv_ref[t:t + 1, :])
            y_scr[r, C_CONV:C_CONV + C_SGU] = (_rms(yb) * beta_b).astype(BF16)
    _row_loop(n_seq, gate_body, CONV_CHUNK)

    _phase_out(x_ref, y_scr, wout_ref, z_scr, wr_ref, br_ref, mod, h_scr,
               xmid_ref, h2_ref, logit_ref, n_rows)


def _const_spec(shape):
    nd = len(shape)
    return pl.BlockSpec(shape, lambda *_: (0,) * nd, pipeline_mode=pl.Buffered(1))


def _const_out_spec(shape):
    nd = len(shape)
    return pl.BlockSpec(shape, lambda *_: (0,) * nd)


def _mixer(x_p, x_s, x_s_block, moe, mod_p, mod_s, w_in_b, w_out_b, wconv_p, vec, g_mat, w_s, bs_full,
           wsv, bsv, w_router, b_router, state_t, n_prompt_seq, seq_len):
    fused = moe is not None
    n_prompt = n_prompt_seq * seq_len
    n_seq, n_pos = state_t.shape[1], bsv.shape[0]
    n_sample = n_seq * n_pos
    t_all = n_prompt + n_sample
    nj = seq_len // ROW_TILE
    weight_specs = [
        _const_spec((D_MODEL, 2 * C_CONV + 2 * C_SGU)),
        _const_spec((D_MODEL, D_MODEL)),
        _const_spec((CARRY_ROWS, C_CONV)),
        _const_spec((8, C_CONV)),
        _const_spec((C_CONV, C_CONV)),
    ]
    router_specs = [_const_spec((N_EXPERTS, D_MODEL)), _const_spec((N_EXPERTS, 1))]
    common_scratch = lambda n: [
        pltpu.VMEM((n, D_MODEL), BF16),
        pltpu.VMEM((n, 2 * C_CONV + 2 * C_SGU), F32),
    ]
    tail_scratch = lambda n, vdt: [
        pltpu.VMEM((n, C_CONV), F32),
        pltpu.VMEM((n, C_CONV), BF16),
        pltpu.VMEM((n, C_CONV), F32),
        pltpu.VMEM((n, C_SGU), F32),
        pltpu.VMEM((n, C_SGU), vdt),
        pltpu.VMEM((n, D_MODEL), BF16),
    ]
    assert n_sample == ROW_TILE
    n_tiles_p = n_prompt // ROW_TILE
    tile_p = lambda i: jnp.minimum(i, n_tiles_p - 1)
    seq_of = lambda i: jnp.minimum(i // nj, n_prompt_seq - 1)
    moe_s_specs, moe_p_specs, moe_s_args, moe_p_args, x_scratch = [], [], [], [], []
    if fused:
        yg, wgt, gate_p, gate_s = moe
        moe_s_specs = [pl.BlockSpec((TOP_K, n_sample, D_WORDS), lambda i: (0, x_s_block, 0)),
                       pl.BlockSpec((n_sample, LANES), lambda i: (x_s_block, 0)),
                       _const_spec((n_seq, D_MODEL))]
        moe_p_specs = [pl.BlockSpec((TOP_K, ROW_TILE, D_WORDS), lambda i: (0, tile_p(i), 0)),
                       pl.BlockSpec((ROW_TILE, LANES), lambda i: (tile_p(i), 0)),
                       pl.BlockSpec((1, 1, D_MODEL), lambda i: (seq_of(tile_p(i)), 0, 0))]
        moe_s_args = [yg, wgt, gate_s]
        moe_p_args = [yg, wgt, gate_p.reshape(n_prompt_seq, 1, D_MODEL)]
        x_scratch = [pltpu.VMEM((ROW_TILE, D_MODEL), F32)]
    smid, sh2, slogits, a_new, v_new = pl.pallas_call(
        functools.partial(_mixer_sample_kernel, fused=fused),
        grid=(1,),
        in_specs=[
            pl.BlockSpec((n_sample, D_MODEL), lambda i: (x_s_block, 0)),
            _const_spec((n_seq, N_MOD * D_MODEL)),
            *moe_s_specs,
            *weight_specs,
            _const_spec((n_pos * n_pos, C_SGU)),
            _const_spec((n_pos, C_SGU)),
            *router_specs,
            _const_spec((CONV_STATE, n_seq, C_CONV)),
        ],
        out_specs=(
            _const_out_spec((n_sample, D_MODEL)),
            _const_out_spec((n_sample, D_WORDS)),
            _const_out_spec((N_EXPERTS, n_sample)),
            _const_out_spec((n_sample, C_CONV)),
            _const_out_spec((n_sample, C_SGU)),
        ),
        out_shape=(
            jax.ShapeDtypeStruct((n_sample, D_MODEL), F32),
            jax.ShapeDtypeStruct((n_sample, D_WORDS), jnp.int32),
            jax.ShapeDtypeStruct((N_EXPERTS, n_sample), F32),
            jax.ShapeDtypeStruct((n_sample, C_CONV), F32),
            jax.ShapeDtypeStruct((n_sample, C_SGU), F32),
        ),
        scratch_shapes=common_scratch(n_sample) + tail_scratch(n_sample, F32) + x_scratch,
        compiler_params=pltpu.CompilerParams(
            dimension_semantics=("arbitrary",), vmem_limit_bytes=VMEM_LIMIT),
        name="mixer_sample",
    )(x_s, mod_s, *moe_s_args, w_in_b, w_out_b, wconv_p, vec, g_mat, wsv, bsv, w_router, b_router,
      state_t)

    xmid, h2, logits, cst = pl.pallas_call(
        functools.partial(_mixer_prompt_kernel, tiles_per_seq=nj, n_prompt_tiles=n_tiles_p,
                          fused=fused),
        grid=(n_tiles_p + 1,),
        in_specs=[
            pl.BlockSpec((ROW_TILE, D_MODEL), lambda i: (tile_p(i), 0)),
            pl.BlockSpec((1, 1, N_MOD * D_MODEL), lambda i: (seq_of(i), 0, 0)),
            *moe_p_specs,
            *weight_specs,
            _const_spec((SGU_HEADS, CHUNK, CHUNK)),
            _const_spec((CHUNK, C_SGU)),
            *router_specs,
            _const_spec((n_sample, D_MODEL)),
            _const_spec((n_sample, D_WORDS)),
            _const_spec((N_EXPERTS, n_sample)),
        ],
        out_specs=(
            pl.BlockSpec((ROW_TILE, D_MODEL), lambda i: (i, 0)),
            pl.BlockSpec((ROW_TILE, D_WORDS), lambda i: (i, 0)),
            pl.BlockSpec((N_EXPERTS, ROW_TILE), lambda i: (0, i)),
            pl.BlockSpec((1, CONV_STATE, C_CONV), lambda i: (seq_of(i), 0, 0)),
        ),
        out_shape=(
            jax.ShapeDtypeStruct((t_all, D_MODEL), F32),
            jax.ShapeDtypeStruct((t_all, D_WORDS), jnp.int32),
            jax.ShapeDtypeStruct((N_EXPERTS, t_all), F32),
            jax.ShapeDtypeStruct((n_prompt_seq, CONV_STATE, C_CONV), F32),
        ),
        scratch_shapes=common_scratch(ROW_TILE)
        + [pltpu.VMEM((SUBLANES, CARRY_ROWS + ROW_TILE, C_CONV), F32)]
        + tail_scratch(ROW_TILE, BF16) + x_scratch,
        compiler_params=pltpu.CompilerParams(
            dimension_semantics=("arbitrary",), vmem_limit_bytes=VMEM_LIMIT),
        name="mixer_prompt",
    )(x_p, mod_p, *moe_p_args, w_in_b, w_out_b, wconv_p, vec, g_mat, w_s, bs_full, w_router, b_router,
      smid, sh2, slogits)
    return xmid, h2, logits, cst, a_new, v_new


def _tile_plan(cnt):
    e_sub = lax.broadcasted_iota(jnp.int32, (N_EXPERTS, N_EXPERTS), 0)
    e_lane = lax.broadcasted_iota(jnp.int32, (N_EXPERTS, N_EXPERTS), 1)
    tiles = jnp.floor((cnt + (GEMM_TILE - 1.0)) * (1.0 / GEMM_TILE))
    active = jnp.where(cnt > 0.0, 1.0, 0.0)

    def over_experts(tri, col):
        return _dot(tri.astype(BF16), jnp.broadcast_to(col, (N_EXPERTS, LANES)).astype(BF16))[:, 0:1]
    tile_end = over_experts(jnp.where(e_lane <= e_sub, 1.0, 0.0), tiles)
    tile_start = tile_end - tiles
    ordinal = over_experts(jnp.where(e_lane < e_sub, 1.0, 0.0), active)

    g = lax.broadcasted_iota(jnp.int32, (1, PLAN_LANES), 1).astype(F32)
    e_id = lax.broadcasted_iota(jnp.int32, (N_EXPERTS, PLAN_LANES), 0).astype(F32)
    done = jnp.sum(jnp.where(tile_end <= g, 1.0, 0.0), axis=0, keepdims=True)
    live = done < float(N_EXPERTS)
    last_e = jnp.max(jnp.where(cnt > 0.0, e_id[:, 0:1], 0.0), axis=0, keepdims=True)
    te = jnp.where(live, done, last_e)
    pick = e_id == te

    def at_tile(col):
        return jnp.sum(jnp.where(pick, col, 0.0), axis=0, keepdims=True)
    start_g = at_tile(tile_start)
    ord_g = at_tile(ordinal)
    valid = jnp.clip(at_tile(cnt) - (g - start_g) * GEMM_TILE, 0.0, float(GEMM_TILE))
    valid = jnp.where(live, valid, 0.0)
    first = jnp.where(live, jnp.where(g == start_g, 1.0, 0.0), 0.0)
    slot = ord_g - 2.0 * jnp.floor(ord_g * 0.5)
    later = jnp.where(e_id > te, jnp.where(cnt > 0.0, e_id, float(N_EXPERTS)), float(N_EXPERTS))
    nxt = jnp.min(later, axis=0, keepdims=True)
    nxt = jnp.where(nxt >= float(N_EXPERTS), -1.0, nxt)

    row = lax.broadcasted_iota(jnp.int32, (PLAN_ROWS, PLAN_LANES), 0)
    plan = jnp.zeros((PLAN_ROWS, PLAN_LANES), F32)
    for r, v in ((PLAN_EXPERT, te), (PLAN_VALID, valid), (PLAN_FIRST, first), (PLAN_SLOT, slot),
                 (PLAN_NEXT, nxt)):
        plan = jnp.where(row == r, v, plan)
    return plan.astype(jnp.int32), tile_start * GEMM_TILE


def _router_kernel(lg_ref, wgt_ref, pos_ref, plan_ref, cnt_scr, run_scr, start_scr, earlier_scr):
    ph = pl.program_id(0)
    i = pl.program_id(1)
    n = lg_ref.shape[1]

    @pl.when(jnp.logical_and(ph == 0, i == 0))
    def _():
        cnt_scr[...] = jnp.zeros_like(cnt_scr)

    lg = lg_ref[...]
    sub = lax.broadcasted_iota(jnp.int32, lg.shape, 0).astype(F32)
    vals, sels = [], []
    for _ in range(TOP_K):
        m = jnp.max(lg, axis=0, keepdims=True)
        idx = jnp.min(jnp.where(lg == m, sub, float(N_EXPERTS)), axis=0, keepdims=True)
        sel = sub == idx
        vals.append(m)
        sels.append(sel)
        lg = jnp.where(sel, -jnp.inf, lg)
    onehot = jnp.zeros(lg.shape, F32)
    for sel in sels:
        onehot = onehot + jnp.where(sel, 1.0, 0.0)
    tile_cnt = jnp.sum(onehot, axis=1, keepdims=True)

    @pl.when(ph == 0)
    def _():
        cnt_scr[...] = cnt_scr[...] + tile_cnt

    @pl.when(jnp.logical_and(ph == 1, i == 0))
    def _():
        plan, row_start = _tile_plan(cnt_scr[...])
        plan_ref[...] = plan
        start_scr[...] = row_start
        run_scr[...] = jnp.zeros_like(run_scr)
        t_row = lax.broadcasted_iota(jnp.int32, (n, n), 0)
        t_col = lax.broadcasted_iota(jnp.int32, (n, n), 1)
        earlier_scr[...] = jnp.where(t_row < t_col, 1.0, 0.0).astype(BF16)

    @pl.when(ph == 1)
    def _():
        exps = [jnp.exp(v - vals[0]) for v in vals]
        inv = 1.0 / (exps[0] + exps[1] + exps[2] + exps[3])
        w_row = lax.broadcasted_iota(jnp.int32, (LANES, n), 0)
        w_all = jnp.zeros((LANES, n), F32)
        for k in range(TOP_K):
            w_all = jnp.where(w_row == k, exps[k] * inv, w_all)
        wgt_ref[...] = w_all.T
        slot0 = _dot(onehot.astype(BF16), earlier_scr[...]) + (run_scr[...] + start_scr[...])
        for k in range(TOP_K):
            pos_k = jnp.sum(jnp.where(sels[k], slot0, 0.0), axis=0, keepdims=True)
            pos_ref[k:k + 1, :] = pos_k.astype(jnp.int32)
        run_scr[...] = run_scr[...] + tile_cnt


def _router(logits):
    t_all = logits.shape[1]
    col = pltpu.VMEM((N_EXPERTS, 1), F32)
    return pl.pallas_call(
        _router_kernel,
        grid=(2, t_all // ROUTER_TILE),
        in_specs=[pl.BlockSpec((N_EXPERTS, ROUTER_TILE), lambda ph, i: (0, i))],
        out_specs=(
            pl.BlockSpec((ROUTER_TILE, LANES), lambda ph, i: (i * ph, 0)),
            pl.BlockSpec((TOP_K, ROUTER_TILE), lambda ph, i: (0, i * ph)),
            _const_out_spec((PLAN_ROWS, PLAN_LANES)),
        ),
        out_shape=(
            jax.ShapeDtypeStruct((t_all, LANES), F32),
            jax.ShapeDtypeStruct((TOP_K, t_all), jnp.int32),
            jax.ShapeDtypeStruct((PLAN_ROWS, PLAN_LANES), jnp.int32),
        ),
        scratch_shapes=[col, col, col, pltpu.VMEM((ROUTER_TILE, ROUTER_TILE), BF16)],
        compiler_params=pltpu.CompilerParams(
            dimension_semantics=("arbitrary", "arbitrary"), vmem_limit_bytes=VMEM_LIMIT),
        name="router",
    )(logits)


def _expert_kernel(plan_ref, x_ref, w1_hbm, b1_ref, w2_hbm, b2_ref, o_ref,
                   w1f_scr, w2f_scr, w1b_scr, w2b_scr, sem, *, layer):
    def weight_copies(expert, slot):
        return (pltpu.make_async_copy(w1_hbm.at[layer, expert], w1f_scr.at[slot], sem.at[0, slot]),
                pltpu.make_async_copy(w2_hbm.at[layer, expert], w2f_scr.at[slot], sem.at[1, slot]))

    for sub in range(TILES_PER_STEP):
        _expert_tile(pl.program_id(0) * TILES_PER_STEP + sub, sub * GEMM_TILE, plan_ref, x_ref,
                     b1_ref, b2_ref, o_ref, w1f_scr, w2f_scr, w1b_scr, w2b_scr, weight_copies)


def _expert_tile(g, row0, plan_ref, x_ref, b1_ref, b2_ref, o_ref, w1f_scr, w2f_scr, w1b_scr, w2b_scr,
                 weight_copies):
    nv = plan_ref[PLAN_VALID, g]
    expert = plan_ref[PLAN_EXPERT, g]

    @pl.when(plan_ref[PLAN_FIRST, g] == 1)
    def _():
        slot = plan_ref[PLAN_SLOT, g]
        nxt = plan_ref[PLAN_NEXT, g]

        @pl.when(g == 0)
        def _():
            for cp in weight_copies(expert, slot):
                cp.start()
        for cp in weight_copies(expert, slot):
            cp.wait()

        @pl.when(nxt >= 0)
        def _():
            for cp in weight_copies(nxt, 1 - slot):
                cp.start()
        w1b_scr[...] = w1f_scr[slot].astype(BF16)
        w2b_scr[...] = w2f_scr[slot].astype(BF16)

    def ffn(m):
        x_left, x_right = _unpack_bf16_pair(x_ref[row0:row0 + m, :])
        x = jnp.concatenate([x_left.astype(BF16), x_right.astype(BF16)], axis=1)

        def hidden(col0):
            cols = slice(col0, col0 + FF_BLOCK)
            return _dot(x, w1b_scr[:, cols]) + b1_ref[expert, :, cols]

        y = jnp.zeros((m, D_MODEL), F32) + b2_ref[expert]
        for jb in range(D_FF // FF_BLOCK):
            gg = jnp.minimum(hidden(jb * FF_BLOCK), SWIGLU_LIMIT)
            up = jnp.clip(hidden(D_FF + jb * FF_BLOCK), -SWIGLU_LIMIT, SWIGLU_LIMIT)
            act = gg * _sigmoid(SWIGLU_ALPHA * gg) * (up + 1.0)
            y = y + _dot(act.astype(BF16), w2b_scr[jb * FF_BLOCK:(jb + 1) * FF_BLOCK, :])
        row = lax.broadcasted_iota(jnp.int32, y.shape, 0)
        o_ref[row0:row0 + m, :] = _pack_bf16_pair(jnp.where(row < nv, y, 0.0))
        if m < GEMM_TILE:
            o_ref[row0 + m:row0 + GEMM_TILE, :] = jnp.zeros((GEMM_TILE - m, D_WORDS), jnp.int32)

    @pl.when(nv > GEMM_TILE // 2)
    def _():
        ffn(GEMM_TILE)

    @pl.when(jnp.logical_and(nv > 0, nv <= GEMM_TILE // 2))
    def _():
        ffn(GEMM_TILE // 2)

    @pl.when(nv == 0)
    def _():
        o_ref[row0:row0 + GEMM_TILE, :] = jnp.zeros((GEMM_TILE, D_WORDS), jnp.int32)


def _experts(xs, plan, w1, b1, w2, b2, layer):
    n_tiles = xs.shape[0] // GEMM_TILE
    depth = w1.shape[0]
    assert n_tiles % TILES_PER_STEP == 0
    step_rows = TILES_PER_STEP * GEMM_TILE
    bias_map = lambda s, plan: (layer, 0, 0, 0)
    grid_spec = pltpu.PrefetchScalarGridSpec(
        num_scalar_prefetch=1,
        grid=(n_tiles // TILES_PER_STEP,),
        in_specs=[
            pl.BlockSpec((step_rows, D_WORDS), lambda s, plan: (s, 0)),
            pl.BlockSpec(memory_space=pl.ANY),
            pl.BlockSpec((None, N_EXPERTS, 1, 2 * D_FF), bias_map, pipeline_mode=pl.Buffered(1)),
            pl.BlockSpec(memory_space=pl.ANY),
            pl.BlockSpec((None, N_EXPERTS, 1, D_MODEL), bias_map, pipeline_mode=pl.Buffered(1)),
        ],
        out_specs=pl.BlockSpec((step_rows, D_WORDS), lambda s, plan: (s, 0)),
        scratch_shapes=[
            pltpu.VMEM((2, D_MODEL, 2 * D_FF), F32),
            pltpu.VMEM((2, D_FF, D_MODEL), F32),
            pltpu.VMEM((D_MODEL, 2 * D_FF), BF16),
            pltpu.VMEM((D_FF, D_MODEL), BF16),
            pltpu.SemaphoreType.DMA((2, 2)),
        ],
    )
    return pl.pallas_call(
        functools.partial(_expert_kernel, layer=layer),
        grid_spec=grid_spec,
        out_shape=jax.ShapeDtypeStruct((n_tiles * GEMM_TILE, D_WORDS), jnp.int32),
        compiler_params=pltpu.CompilerParams(
            dimension_semantics=("arbitrary",), vmem_limit_bytes=VMEM_LIMIT),
        name="experts",
    )(plan, xs, w1, b1.reshape(depth, N_EXPERTS, 1, 2 * D_FF), w2,
      b2.reshape(depth, N_EXPERTS, 1, D_MODEL))


def _final_kernel(x_ref, yg_ref, wgt_ref, gp_ref, gs_ref, fg_ref, op_ref, os_ref, *, n_prompt_blocks):
    i = pl.program_id(0)
    n_seq = gs_ref.shape[0]

    def run(gate, o_ref):
        def body(r0):
            r = pl.ds(r0, COMBINE_CHUNK)
            w = wgt_ref[r, :]
            g = gate(r0)
            xl = x_ref[r, 0:D_WORDS] + g[:, 0:D_WORDS] * _moe_half(yg_ref, w, r, False)
            xr = x_ref[r, D_WORDS:D_MODEL] + g[:, D_WORDS:D_MODEL] * _moe_half(yg_ref, w, r, True)
            ms = (jnp.sum(xl * xl, axis=-1, keepdims=True)
                  + jnp.sum(xr * xr, axis=-1, keepdims=True)) * (1.0 / D_MODEL)
            scale = lax.rsqrt(ms + EPS)
            o_ref[r, 0:D_WORDS] = xl * scale * fg_ref[:, 0:D_WORDS]
            o_ref[r, D_WORDS:D_MODEL] = xr * scale * fg_ref[:, D_WORDS:D_MODEL]
        _row_loop(x_ref.shape[0], body, COMBINE_CHUNK)

    @pl.when(i < n_prompt_blocks)
    def _():
        run(lambda r0: gp_ref[0], op_ref)

    @pl.when(i >= n_prompt_blocks)
    def _():
        run(lambda r0: gs_ref[pl.ds(pl.multiple_of(lax.rem(r0, n_seq), COMBINE_CHUNK), COMBINE_CHUNK), :],
            os_ref)


def _final(xmid, yg, wgt, gate_p, gate_s, final_g, n_prompt, seq_len):
    t_all = xmid.shape[0]
    n_prompt_blocks = n_prompt // ROW_TILE
    per_seq = seq_len // ROW_TILE
    n_seq_p = gate_p.shape[0]
    assert t_all - n_prompt == ROW_TILE
    out_specs = (
        pl.BlockSpec((ROW_TILE, D_MODEL), lambda i: (jnp.minimum(i, n_prompt_blocks - 1), 0)),
        pl.BlockSpec((ROW_TILE, D_MODEL), lambda i: (0, 0)),
    )
    out_shape = (jax.ShapeDtypeStruct((n_prompt, D_MODEL), F32),
                 jax.ShapeDtypeStruct((ROW_TILE, D_MODEL), F32))
    return pl.pallas_call(
        functools.partial(_final_kernel, n_prompt_blocks=n_prompt_blocks),
        grid=(t_all // ROW_TILE,),
        in_specs=[
            pl.BlockSpec((ROW_TILE, D_MODEL), lambda i: (i, 0)),
            pl.BlockSpec((TOP_K, ROW_TILE, D_WORDS), lambda i: (0, i, 0)),
            pl.BlockSpec((ROW_TILE, LANES), lambda i: (i, 0)),
            pl.BlockSpec((1, 1, D_MODEL), lambda i: (jnp.minimum(i // per_seq, n_seq_p - 1), 0, 0)),
            _const_spec(gate_s.shape),
            _const_spec((1, D_MODEL)),
        ],
        out_specs=out_specs,
        out_shape=out_shape,
        compiler_params=pltpu.CompilerParams(
            dimension_semantics=("arbitrary",), vmem_limit_bytes=VMEM_LIMIT),
        name="final_combine",
    )(xmid, yg, wgt, gate_p.reshape(n_seq_p, 1, D_MODEL), gate_s, final_g.reshape(1, D_MODEL))


def _sc_mesh():
    return plsc.VectorSubcoreMesh(core_axis_name="core", subcore_axis_name="subcore",
                                  num_cores=SC_CORES, num_subcores=SC_SUBCORES)


def _sc_worker_id():
    return lax.axis_index("subcore") * SC_CORES + lax.axis_index("core")


def _sc_dispatch(h2w, pos_c, n_slots):
    n_chunks = pos_c.shape[1]

    @functools.partial(
        pl.kernel, mesh=_sc_mesh(),
        out_type=jax.ShapeDtypeStruct((n_slots, D_WORDS), jnp.int32),
        scratch_types=[pltpu.VMEM((TOP_K, SC_CHUNK), jnp.int32),
                       pltpu.VMEM((SC_CHUNK, D_WORDS), jnp.int32),
                       pltpu.SemaphoreType.DMA],
        name="moe_dispatch")
    def run(h_hbm, pos_hbm, out_hbm, idx_v, rows_v, sem):
        wid = _sc_worker_id()

        @pl.loop(0, n_chunks)
        def _(c):
            base = pl.multiple_of((wid * n_chunks + c) * SC_CHUNK, SUBLANES)
            pltpu.sync_copy(pos_hbm.at[wid, c], idx_v)
            pltpu.sync_copy(h_hbm.at[pl.ds(base, SC_CHUNK)], rows_v)
            copies = [pltpu.async_copy(rows_v, out_hbm.at[idx_v.at[k]], sem) for k in range(TOP_K)]
            for cp in copies:
                cp.wait()

    return run(h2w, pos_c)


def _sc_gather_back(ys, pos_c, t_all):
    n_chunks = pos_c.shape[1]

    @functools.partial(
        pl.kernel, mesh=_sc_mesh(),
        out_type=jax.ShapeDtypeStruct((TOP_K, t_all, D_WORDS), jnp.int32),
        scratch_types=[pltpu.VMEM((TOP_K, SC_CHUNK), jnp.int32),
                       pltpu.VMEM((TOP_K, SC_CHUNK, D_WORDS), jnp.int32),
                       pltpu.SemaphoreType.DMA],
        name="moe_gather_back")
    def run(ys_hbm, pos_hbm, out_hbm, idx_v, rows_v, sem):
        wid = _sc_worker_id()

        @pl.loop(0, n_chunks)
        def _(c):
            base = pl.multiple_of((wid * n_chunks + c) * SC_CHUNK, SUBLANES)
            pltpu.sync_copy(pos_hbm.at[wid, c], idx_v)
            copies = [pltpu.async_copy(ys_hbm.at[idx_v.at[k]], rows_v.at[k], sem) for k in range(TOP_K)]
            for cp in copies:
                cp.wait()
            for k in range(TOP_K):
                pltpu.sync_copy(rows_v.at[k], out_hbm.at[k, pl.ds(base, SC_CHUNK)])

    return run(ys, pos_c)


def kernel(x_prompt, x_sample, state_conv, c_prompt, c_sample, w_ada, b_ada, w_in, w_conv, b_conv,
           gn_g, gn_b, sgu_ln_g, sgu_ln_b, w_s, b_s, beta_a, beta_b, w_out, w_router, b_router,
           w1, b1, w2, b2, final_g):
    n_bp, seq_len, _ = x_prompt.shape
    n_bs, n_pos, _ = x_sample.shape
    depth = w_ada.shape[0]
    n_prompt = n_bp * seq_len
    n_sample = n_bs * n_pos
    t_all = n_prompt + n_sample
    n_tiles = (t_all * TOP_K) // GEMM_TILE + N_EXPERTS
    n_workers = SC_CORES * SC_SUBCORES
    assert t_all % (n_workers * SC_CHUNK) == 0 and n_tiles <= PLAN_LANES and t_all % ROUTER_TILE == 0

    x_p = x_prompt.reshape(n_prompt, D_MODEL)
    x_s = x_sample.transpose(1, 0, 2).reshape(n_sample, D_MODEL)
    x_s_block = 0
    moe = None
    mods =_adaln(jnp.concatenate([c_prompt, c_sample], axis=0), w_ada, b_ada)
    state_t = state_conv.transpose(0, 2, 1, 3)
    grp = jnp.arange(C_CONV) // (C_CONV // CONV_GROUPS)
    g_mat = ((grp[:, None] == grp[None, :]).astype(F32) / (C_CONV // CONV_GROUPS)).astype(BF16)

    conv_p, conv_s, v_s = [], [], []
    for l in range(depth):
        mod_p = mods[l, :n_bp].reshape(n_bp, 1, N_MOD * D_MODEL)
        mod_s = mods[l, n_bp:]
        wconv_p = jnp.pad(w_conv[l], ((0, CARRY_ROWS - CONV_WIDTH), (0, 0)))
        zero = jnp.zeros_like(b_conv[l])
        vec = jnp.stack([b_conv[l], gn_g[l], gn_b[l], sgu_ln_g[l], sgu_ln_b[l],
                         beta_a[l], beta_b[l], zero], axis=0)
        bs_full = jnp.repeat(b_s[l].T, SGU_HEAD_DIM, axis=1)
        wsv = jnp.repeat(w_s[l][:, :n_pos, :n_pos].transpose(1, 2, 0).reshape(n_pos * n_pos, SGU_HEADS),
                         SGU_HEAD_DIM, axis=1)
        bsv = jnp.repeat(b_s[l][:, :n_pos].T, SGU_HEAD_DIM, axis=1)
        xmid, h2, logits, cst, a_new, v_new = _mixer(
            x_p, x_s, x_s_block, moe, mod_p, mod_s, w_in[l].astype(BF16), w_out[l].astype(BF16), wconv_p,
            vec, g_mat,
            w_s[l], bs_full, wsv, bsv, w_router[l].T, b_router[l].reshape(N_EXPERTS, 1),
            state_t[l], n_bp, seq_len)
        conv_p.append(cst)
        conv_s.append(a_new)
        v_s.append(v_new)

        wgt, pos, plan = _router(logits)
        pos_c = pos.reshape(TOP_K, n_workers, -1, SC_CHUNK).transpose(1, 2, 0, 3)
        xs = _sc_dispatch(h2, pos_c, n_tiles * GEMM_TILE)
        ys = _experts(xs, plan, w1, b1, w2, b2, l)
        yg = _sc_gather_back(ys, pos_c, t_all)
        moe = (yg, wgt, mod_p[:, 0, 5 * D_MODEL:], mod_s[:, 5 * D_MODEL:])
        x_p = x_s = xmid
        x_s_block = n_prompt // n_sample

    y_p, y_s = _final(xmid, *moe, final_g, n_prompt, seq_len)
    y_prompt = y_p.reshape(n_bp, seq_len, D_MODEL)
    y_sample = y_s.reshape(n_pos, n_bs, D_MODEL).transpose(1, 0, 2)
    def seq_major(parts, width):
        return jnp.stack(parts, axis=0).reshape(depth, n_pos, n_bs, width).transpose(0, 2, 1, 3)
    new_conv_s = jnp.concatenate([state_conv[:, :, n_pos:], seq_major(conv_s, C_CONV)], axis=2)
    return (y_prompt, y_sample, jnp.stack(conv_p, axis=0), new_conv_s, seq_major(v_s, C_SGU))
```

```python
import functools
import math

import jax
import jax.numpy as jnp
from jax import lax
from jax.experimental import pallas as pl
from jax.experimental.pallas import tpu as pltpu
from jax.experimental.pallas import tpu_sc as plsc

F32 = jnp.float32
BF16 = jnp.bfloat16

D_MODEL = 1024
C_CONV = 512
C_SGU = 512
CONV_GROUPS = 8
CONV_WIDTH = 31
CONV_STATE = CONV_WIDTH - 1
SGU_HEADS = 4
SGU_HEAD_DIM = C_SGU // SGU_HEADS
CHUNK = 128
N_EXPERTS = 32
TOP_K = 4
D_FF = 1024
SWIGLU_LIMIT = 7.0
SWIGLU_ALPHA = 1.702
N_MOD = 6
EPS = 1e-5

ROW_TILE = 512
ROUTER_TILE = 1536
ROW_CHUNK = 128
CONV_CHUNK = 32
COMBINE_CHUNK = 64
CARRY_ROWS = 32
SUBLANES = 8
LANES = 128
PLAN_ROWS = 8
PLAN_LANES = 256
PLAN_EXPERT, PLAN_VALID, PLAN_FIRST, PLAN_SLOT, PLAN_NEXT = range(5)
GEMM_TILE = 512
FF_BLOCK = 256
TILES_PER_STEP = 2
VMEM_LIMIT = 56 * 1024 * 1024
D_WORDS = D_MODEL // 2
SC_CHUNK = 48
HI_MASK = -65536
SC_CORES = 2
SC_SUBCORES = 16


def _rms(x):
    return x * lax.rsqrt(jnp.mean(x * x, axis=-1, keepdims=True) + EPS)


def _gelu(x):
    return 0.5 * x * (1.0 + lax.erf(x * (1.0 / math.sqrt(2.0))))


def _sigmoid(x):
    return 1.0 / (1.0 + jnp.exp(-x))


def _split_bf16(x):
    hi = x.astype(BF16)
    lo = (x - hi.astype(F32)).astype(BF16)
    return hi, lo


def _pack_bf16_pair(x):
    bits = lax.bitcast_convert_type(x.astype(BF16).astype(F32), jnp.int32)
    return lax.shift_right_logical(bits[:, :D_WORDS], 16) | (bits[:, D_WORDS:] & HI_MASK)


def _unpack_bf16_pair(w):
    left = lax.bitcast_convert_type(lax.shift_left(w, 16), F32)
    right = lax.bitcast_convert_type(w & HI_MASK, F32)
    return left, right


def _dot(a, b):
    return jnp.dot(a, b, preferred_element_type=F32)


def _dot_nt(a, b):
    return lax.dot_general(a, b, (((1,), (1,)), ((), ())), preferred_element_type=F32)


def _row_loop(n_rows, body, chunk=None):
    chunk = ROW_CHUNK if chunk is None else chunk

    def step(i, carry):
        body(pl.multiple_of(i * chunk, chunk))
        return carry
    lax.fori_loop(0, n_rows // chunk, step, 0)


def _static_loop(n_rows, body, chunk=None):
    chunk = ROW_CHUNK if chunk is None else chunk
    for r0 in range(0, n_rows, chunk):
        body(r0)


def _adaln_kernel(c_ref, w_ref, b_ref, o_ref):
    c = c_ref[...]
    s_hi, s_lo = _split_bf16(c * _sigmoid(c))
    w_hi, w_lo = _split_bf16(w_ref[0])
    acc = _dot(s_hi, w_hi) + _dot(s_hi, w_lo) + _dot(s_lo, w_hi)
    o_ref[0] = acc + b_ref[0]


def _adaln(c_all, w_ada, b_ada):
    depth, _, n_out = w_ada.shape
    n_rows = c_all.shape[0]
    tn = 1024
    return pl.pallas_call(
        _adaln_kernel,
        grid=(depth, n_out // tn),
        in_specs=[
            pl.BlockSpec((n_rows, D_MODEL), lambda l, j: (0, 0)),
            pl.BlockSpec((1, D_MODEL, tn), lambda l, j: (l, 0, j)),
            pl.BlockSpec((1, 1, tn), lambda l, j: (l, 0, j)),
        ],
        out_specs=pl.BlockSpec((1, n_rows, tn), lambda l, j: (l, 0, j)),
        out_shape=jax.ShapeDtypeStruct((depth, n_rows, n_out), F32),
        compiler_params=pltpu.CompilerParams(
            dimension_semantics=("arbitrary", "arbitrary"), vmem_limit_bytes=VMEM_LIMIT),
        name="adaln",
    )(c_all, w_ada, b_ada.reshape(depth, 1, n_out))


def _phase_in_norm(x_ref, h_scr, mod, n_rows, loop=_row_loop):
    def body(r0):
        r = pl.ds(r0, ROW_CHUNK)
        h = _rms(x_ref[r, :]) * (1.0 + mod(1, r0)) + mod(0, r0)
        h_scr[r, :] = h.astype(BF16)
    loop(n_rows, body)


def _phase_glu(z_scr, a_dst, a_off, n_rows, loop=_row_loop):
    def body(r0):
        r = pl.ds(r0, ROW_CHUNK)
        a_dst[pl.ds(r0 + a_off, ROW_CHUNK), :] = z_scr[r, 0:C_CONV] * _sigmoid(z_scr[r, C_CONV:2 * C_CONV])
    loop(n_rows, body)


def _phase_uv(z_scr, vec_ref, u_scr, v_scr, vout_ref, n_rows, loop=_row_loop):
    ln_g = vec_ref[3:4, :]
    ln_b = vec_ref[4:5, :]

    def body(r0):
        r = pl.ds(r0, ROW_CHUNK)
        u_scr[r, :] = _gelu(z_scr[r, 2 * C_CONV:2 * C_CONV + C_SGU])
        gv = _gelu(z_scr[r, 2 * C_CONV + C_SGU:2 * C_CONV + 2 * C_SGU])
        mu = jnp.mean(gv, axis=-1, keepdims=True)
        dv = gv - mu
        var = jnp.mean(dv * dv, axis=-1, keepdims=True)
        v = dv * lax.rsqrt(var + EPS) * ln_g + ln_b
        if vout_ref is not None:
            vout_ref[r, :] = v
        v_scr[r, :] = v.astype(v_scr.dtype)
    loop(n_rows, body)


def _phase_group_norm(conv_scr, convb_scr, stat_scr, g_ref, vec_ref, y_scr, n_rows, loop=_row_loop):
    stat_scr[...] = _dot(convb_scr[...], g_ref[...])

    def center(r0):
        r = pl.ds(r0, ROW_CHUNK)
        d = conv_scr[r, :] - stat_scr[r, :]
        conv_scr[r, :] = d
        convb_scr[r, :] = (d * d).astype(BF16)
    loop(n_rows, center)
    stat_scr[...] = _dot(convb_scr[...], g_ref[...])
    gn_g = vec_ref[1:2, :]
    gn_b = vec_ref[2:3, :]
    beta_a = vec_ref[5:6, :]

    def finish(r0):
        r = pl.ds(r0, ROW_CHUNK)
        gn = conv_scr[r, :] * lax.rsqrt(stat_scr[r, :] + EPS) * gn_g + gn_b
        ya = gn * _sigmoid(gn)
        y_scr[r, 0:C_CONV] = (_rms(ya) * beta_a).astype(BF16)
    loop(n_rows, finish)


def _phase_out(x_ref, y_scr, wout_ref, z_scr, wr_ref, br_ref, mod, h_scr,
               xmid_ref, h2_ref, logit_ref, n_rows, loop=_row_loop):
    z_scr[:, 0:D_MODEL] = (_dot(y_scr[:, 0:C_CONV], wout_ref[0:C_CONV, :])
                           + _dot(y_scr[:, C_CONV:C_CONV + C_SGU], wout_ref[C_CONV:C_CONV + C_SGU, :]))

    def body(r0):
        r = pl.ds(r0, ROW_CHUNK)
        xm = x_ref[r, :] + mod(2, r0) * z_scr[r, 0:D_MODEL]
        xmid_ref[r, :] = xm
        h2 = _rms(xm) * (1.0 + mod(4, r0)) + mod(3, r0)
        hi, lo = _split_bf16(h2)
        h2_ref[r, :] = _pack_bf16_pair(h2)
        y_scr[r, :] = hi
        h_scr[r, :] = lo
    loop(n_rows, body)
    w_hi, w_lo = _split_bf16(wr_ref[...])
    both = _dot_nt(jnp.concatenate([w_hi, w_lo], axis=0), y_scr[...])
    logit_ref[...] = (both[0:N_EXPERTS] + both[N_EXPERTS:2 * N_EXPERTS]
                      + _dot_nt(w_hi, h_scr[...])) + br_ref[...]


def _moe_half(yg_ref, w, r, hi_half):
    moe = jnp.zeros((COMBINE_CHUNK, D_WORDS), F32)
    for k in range(TOP_K):
        word = yg_ref[k, r, :]
        bits = (word & HI_MASK) if hi_half else lax.shift_left(word, 16)
        moe = moe + w[:, k:k + 1] * lax.bitcast_convert_type(bits, F32)
    return moe


def _phase_moe_residual(xmid_ref, yg_ref, wgt_ref, gate, x_dst, n_rows, loop=_row_loop):
    def body(r0):
        r = pl.ds(r0, COMBINE_CHUNK)
        w = wgt_ref[r, :]
        g = gate(r0)
        for hi_half in (False, True):
            cols = slice(D_WORDS, D_MODEL) if hi_half else slice(0, D_WORDS)
            x_dst[r, cols] = xmid_ref[r, cols] + g[:, cols] * _moe_half(yg_ref, w, r, hi_half)
    loop(n_rows, body, COMBINE_CHUNK)


def _mixer_prompt_kernel(*refs, tiles_per_seq, n_prompt_tiles, fused):
    x_ref, mod_ref = refs[:2]
    refs = refs[2:]
    if fused:
        yg_ref, wgt_ref, gate_ref = refs[:3]
        refs = refs[3:]
        x_scr = refs[-1]
        refs = refs[:-1]
    (win_ref, wout_ref, wconv_ref, vec_ref, g_ref, ws_ref, bs_ref, wr_ref, br_ref,
     smid_ref, sh2_ref, slogit_ref, xmid_ref, h2_ref, logit_ref, cst_ref,
     h_scr, z_scr, aext_scr, conv_scr, convb_scr, stat_scr, u_scr, v_scr, y_scr) = refs
    i = pl.program_id(0)

    @pl.when(i == n_prompt_tiles)
    def _():
        xmid_ref[...] = smid_ref[...]
        h2_ref[...] = sh2_ref[...]
        logit_ref[...] = slogit_ref[...]

    @pl.when(i < n_prompt_tiles)
    def _():
        x_src = x_ref
        if fused:
            _phase_moe_residual(x_ref, yg_ref, wgt_ref, lambda r0: gate_ref[0], x_scr, ROW_TILE,
                                _static_loop)
            x_src = x_scr
        _mixer_prompt_tile(lax.rem(i, tiles_per_seq), tiles_per_seq,
                           x_src, mod_ref, win_ref, wout_ref, wconv_ref, vec_ref, g_ref,
                           ws_ref, bs_ref, wr_ref, br_ref, xmid_ref, h2_ref, logit_ref, cst_ref,
                           h_scr, z_scr, aext_scr, conv_scr, convb_scr, stat_scr, u_scr, v_scr, y_scr)


def _mixer_prompt_tile(j, tiles_per_seq, x_ref, mod_ref, win_ref, wout_ref, wconv_ref, vec_ref, g_ref,
                       ws_ref, bs_ref, wr_ref, br_ref, xmid_ref, h2_ref, logit_ref, cst_ref,
                       h_scr, z_scr, aext_scr, conv_scr, convb_scr, stat_scr, u_scr, v_scr, y_scr):
    n_rows = ROW_TILE

    def mod(idx, r0):
        del r0
        return mod_ref[0, :, idx * D_MODEL:(idx + 1) * D_MODEL]

    a_buf = aext_scr.at[0]

    @pl.when(j == 0)
    def _():
        a_buf[0:CARRY_ROWS, :] = jnp.zeros((CARRY_ROWS, C_CONV), F32)

    loop = _static_loop
    _phase_in_norm(x_ref, h_scr, mod, n_rows, loop)
    n_glu = 2 * C_CONV
    z_scr[:, 0:n_glu] = _dot(h_scr[...], win_ref[:, 0:n_glu])
    _phase_glu(z_scr, a_buf, CARRY_ROWS, n_rows, loop)
    z_scr[:, n_glu:] = _dot(h_scr[...], win_ref[:, n_glu:])
    n_shift = n_rows + CARRY_ROWS - SUBLANES
    for s in range(1, SUBLANES):
        aext_scr[s, 0:n_shift, :] = a_buf[s:s + n_shift, :]

    b_conv = vec_ref[0:1, :]
    lead = CARRY_ROWS - CONV_STATE
    for r0 in range(0, n_rows, CONV_CHUNK):
        acc = jnp.zeros((CONV_CHUNK, C_CONV), F32)
        for k in range(CONV_WIDTH):
            s = (lead + k) % SUBLANES
            q0 = r0 + (lead + k - s)
            acc = acc + wconv_ref[k:k + 1, :] * aext_scr[s, q0:q0 + CONV_CHUNK, :]
        conv = acc + b_conv
        conv_scr[r0:r0 + CONV_CHUNK, :] = conv
        convb_scr[r0:r0 + CONV_CHUNK, :] = conv.astype(BF16)
    _phase_uv(z_scr, vec_ref, u_scr, v_scr, None, n_rows, loop)
    _phase_group_norm(conv_scr, convb_scr, stat_scr, g_ref, vec_ref, y_scr, n_rows, loop)

    row_i = lax.broadcasted_iota(jnp.int32, (CHUNK, CHUNK), 0)
    col_i = lax.broadcasted_iota(jnp.int32, (CHUNK, CHUNK), 1)
    tril = (col_i <= row_i).astype(F32)
    for hd in range(SGU_HEADS):
        ws_h = (ws_ref[hd] * tril).astype(BF16)
        cols = slice(hd * SGU_HEAD_DIM, (hd + 1) * SGU_HEAD_DIM)
        s_cols = slice(n_glu + hd * SGU_HEAD_DIM, n_glu + (hd + 1) * SGU_HEAD_DIM)
        for c in range(n_rows // CHUNK):
            rows = slice(c * CHUNK, (c + 1) * CHUNK)
            z_scr[rows, s_cols] = _dot(ws_h, v_scr[rows, cols])
    beta_b = vec_ref[6:7, :]

    def gate_body(r0):
        r = pl.ds(r0, ROW_CHUNK)
        rb = pl.ds(r0 % CHUNK, ROW_CHUNK)
        yb = u_scr[r, :] * (z_scr[r, n_glu:n_glu + C_SGU] + bs_ref[rb, :])
        y_scr[r, C_CONV:C_CONV + C_SGU] = (_rms(yb) * beta_b).astype(BF16)
    loop(n_rows, gate_body)

    _phase_out(x_ref, y_scr, wout_ref, z_scr, wr_ref, br_ref, mod, h_scr,
               xmid_ref, h2_ref, logit_ref, n_rows, loop)

    @pl.when(j == tiles_per_seq - 1)
    def _():
        cst_ref[0] = a_buf[n_rows + lead:n_rows + CARRY_ROWS, :]

    a_buf[0:CARRY_ROWS, :] = a_buf[n_rows:n_rows + CARRY_ROWS, :]


def _mixer_sample_kernel(*refs, fused):
    x_ref, mod_ref = refs[:2]
    refs = refs[2:]
    if fused:
        yg_ref, wgt_ref, gate_ref = refs[:3]
        refs = refs[3:]
        x_scr = refs[-1]
        refs = refs[:-1]
    (win_ref, wout_ref, wconv_ref, vec_ref, g_ref, wsv_ref, bsv_ref, wr_ref, br_ref, state_ref,
     xmid_ref, h2_ref, logit_ref, cnew_ref, vout_ref,
     h_scr, z_scr, conv_scr, convb_scr, stat_scr, u_scr, v_scr, y_scr, anew_ref) = refs
    n_rows = x_ref.shape[0]
    n_seq = state_ref.shape[1]
    n_pos = n_rows // n_seq
    loop = _static_loop

    def mod(idx, r0):
        return mod_ref[pl.ds(r0 % n_seq, ROW_CHUNK), idx * D_MODEL:(idx + 1) * D_MODEL]

    if fused:
        def gate(r0):
            return gate_ref[pl.ds(r0 % n_seq, COMBINE_CHUNK), :]
        _phase_moe_residual(x_ref, yg_ref, wgt_ref, gate, x_scr, n_rows, loop)
        x_ref = x_scr

    _phase_in_norm(x_ref, h_scr, mod, n_rows, loop)
    z_scr[...] = _dot(h_scr[...], win_ref[...])
    _phase_glu(z_scr, anew_ref, 0, n_rows, loop)
    _phase_uv(z_scr, vec_ref, u_scr, v_scr, vout_ref, n_rows, loop)

    keep = CONV_STATE - n_pos
    cnew_ref[0:keep] = state_ref[n_pos:CONV_STATE]
    for t in range(n_pos):
        cnew_ref[keep + t] = anew_ref[t * n_seq:(t + 1) * n_seq, :]

    b_conv = vec_ref[0:1, :]

    def conv_body(s0):
        rs = pl.ds(s0, CONV_CHUNK)
        for t in range(n_pos):
            acc = jnp.zeros((CONV_CHUNK, C_CONV), F32)
            for k in range(CONV_WIDTH):
                p = t + k
                if p < CONV_STATE:
                    src = state_ref[p, rs, :]
                else:
                    src = anew_ref[pl.ds(s0 + (p - CONV_STATE) * n_seq, CONV_CHUNK), :]
                acc = acc + wconv_ref[k:k + 1, :] * src
            conv = acc + b_conv
            r = pl.ds(s0 + t * n_seq, CONV_CHUNK)
            conv_scr[r, :] = conv
            convb_scr[r, :] = conv.astype(BF16)
    loop(n_seq, conv_body, CONV_CHUNK)

    _phase_group_norm(conv_scr, convb_scr, stat_scr, g_ref, vec_ref, y_scr, n_rows, loop)

    beta_b = vec_ref[6:7, :]

    def gate_body(s0):
        for t in range(n_pos):
            s = jnp.zeros((CONV_CHUNK, C_SGU), F32)
            for jj in range(t + 1):
                s = s + wsv_ref[t * n_pos + jj:t * n_pos + jj + 1, :] * v_scr[pl.ds(s0 + jj * n_seq, CONV_CHUNK), :]
            r = pl.ds(s0 + t * n_seq, CONV_CHUNK)
            yb = u_scr[r, :] * (s + bsv_ref[t:t + 1, :])
            y_scr[r, C_CONV:C_CONV + C_SGU] = (_rms(yb) * beta_b).astype(BF16)
    loop(n_seq, gate_body, CONV_CHUNK)

    _phase_out(x_ref, y_scr, wout_ref, z_scr, wr_ref, br_ref, mod, h_scr,
               xmid_ref, h2_ref, logit_ref, n_rows, loop)


def _const_spec(shape):
    nd = len(shape)
    return pl.BlockSpec(shape, lambda *_: (0,) * nd, pipeline_mode=pl.Buffered(1))


def _const_out_spec(shape):
    nd = len(shape)
    return pl.BlockSpec(shape, lambda *_: (0,) * nd)


def _mixer(x_p, x_s, x_s_block, moe, mod_p, mod_s, w_in_b, w_out_b, wconv_p, vec, g_mat, w_s, bs_full,
           wsv, bsv, w_router, b_router, state_t, n_prompt_seq, seq_len):
    fused = moe is not None
    n_prompt = n_prompt_seq * seq_len
    n_seq, n_pos = state_t.shape[1], bsv.shape[0]
    n_sample = n_seq * n_pos
    t_all = n_prompt + n_sample
    nj = seq_len // ROW_TILE
    weight_specs = [
        _const_spec((D_MODEL, 2 * C_CONV + 2 * C_SGU)),
        _const_spec((D_MODEL, D_MODEL)),
        _const_spec((CARRY_ROWS, C_CONV)),
        _const_spec((8, C_CONV)),
        _const_spec((C_CONV, C_CONV)),
    ]
    router_specs = [_const_spec((N_EXPERTS, D_MODEL)), _const_spec((N_EXPERTS, 1))]
    common_scratch = lambda n: [
        pltpu.VMEM((n, D_MODEL), BF16),
        pltpu.VMEM((n, 2 * C_CONV + 2 * C_SGU), F32),
    ]
    tail_scratch = lambda n, vdt: [
        pltpu.VMEM((n, C_CONV), F32),
        pltpu.VMEM((n, C_CONV), BF16),
        pltpu.VMEM((n, C_CONV), F32),
        pltpu.VMEM((n, C_SGU), F32),
        pltpu.VMEM((n, C_SGU), vdt),
        pltpu.VMEM((n, D_MODEL), BF16),
    ]
    assert n_sample == ROW_TILE
    n_tiles_p = n_prompt // ROW_TILE
    tile_p = lambda i: jnp.minimum(i, n_tiles_p - 1)
    seq_of = lambda i: jnp.minimum(i // nj, n_prompt_seq - 1)
    moe_s_specs, moe_p_specs, moe_s_args, moe_p_args, x_scratch = [], [], [], [], []
    if fused:
        yg, wgt, gate_p, gate_s = moe
        moe_s_specs = [pl.BlockSpec((TOP_K, n_sample, D_WORDS), lambda i: (0, x_s_block, 0)),
                       pl.BlockSpec((n_sample, LANES), lambda i: (x_s_block, 0)),
                       _const_spec((n_seq, D_MODEL))]
        moe_p_specs = [pl.BlockSpec((TOP_K, ROW_TILE, D_WORDS), lambda i: (0, tile_p(i), 0)),
                       pl.BlockSpec((ROW_TILE, LANES), lambda i: (tile_p(i), 0)),
                       pl.BlockSpec((1, 1, D_MODEL), lambda i: (seq_of(tile_p(i)), 0, 0))]
        moe_s_args = [yg, wgt, gate_s]
        moe_p_args = [yg, wgt, gate_p.reshape(n_prompt_seq, 1, D_MODEL)]
        x_scratch = [pltpu.VMEM((ROW_TILE, D_MODEL), F32)]
    smid, sh2, slogits, a_new, v_new = pl.pallas_call(
        functools.partial(_mixer_sample_kernel, fused=fused),
        grid=(1,),
        in_specs=[
            pl.BlockSpec((n_sample, D_MODEL), lambda i: (x_s_block, 0)),
            _const_spec((n_seq, N_MOD * D_MODEL)),
            *moe_s_specs,
            *weight_specs,
            _const_spec((n_pos * n_pos, C_SGU)),
            _const_spec((n_pos, C_SGU)),
            *router_specs,
            _const_spec((CONV_STATE, n_seq, C_CONV)),
        ],
        out_specs=(
            _const_out_spec((n_sample, D_MODEL)),
            _const_out_spec((n_sample, D_WORDS)),
            _const_out_spec((N_EXPERTS, n_sample)),
            _const_out_spec((CONV_STATE, n_seq, C_CONV)),
            _const_out_spec((n_sample, C_SGU)),
        ),
        out_shape=(
            jax.ShapeDtypeStruct((n_sample, D_MODEL), F32),
            jax.ShapeDtypeStruct((n_sample, D_WORDS), jnp.int32),
            jax.ShapeDtypeStruct((N_EXPERTS, n_sample), F32),
            jax.ShapeDtypeStruct((CONV_STATE, n_seq, C_CONV), F32),
            jax.ShapeDtypeStruct((n_sample, C_SGU), F32),
        ),
        scratch_shapes=common_scratch(n_sample) + tail_scratch(n_sample, F32)
        + [pltpu.VMEM((n_sample, C_CONV), F32)] + x_scratch,
        compiler_params=pltpu.CompilerParams(
            dimension_semantics=("arbitrary",), vmem_limit_bytes=VMEM_LIMIT),
        name="mixer_sample",
    )(x_s, mod_s, *moe_s_args, w_in_b, w_out_b, wconv_p, vec, g_mat, wsv, bsv, w_router, b_router,
      state_t)

    xmid, h2, logits, cst = pl.pallas_call(
        functools.partial(_mixer_prompt_kernel, tiles_per_seq=nj, n_prompt_tiles=n_tiles_p,
                          fused=fused),
        grid=(n_tiles_p + 1,),
        in_specs=[
            pl.BlockSpec((ROW_TILE, D_MODEL), lambda i: (tile_p(i), 0)),
            pl.BlockSpec((1, 1, N_MOD * D_MODEL), lambda i: (seq_of(i), 0, 0)),
            *moe_p_specs,
            *weight_specs,
            _const_spec((SGU_HEADS, CHUNK, CHUNK)),
            _const_spec((CHUNK, C_SGU)),
            *router_specs,
            _const_spec((n_sample, D_MODEL)),
            _const_spec((n_sample, D_WORDS)),
            _const_spec((N_EXPERTS, n_sample)),
        ],
        out_specs=(
            pl.BlockSpec((ROW_TILE, D_MODEL), lambda i: (i, 0)),
            pl.BlockSpec((ROW_TILE, D_WORDS), lambda i: (i, 0)),
            pl.BlockSpec((N_EXPERTS, ROW_TILE), lambda i: (0, i)),
            pl.BlockSpec((1, CONV_STATE, C_CONV), lambda i: (seq_of(i), 0, 0)),
        ),
        out_shape=(
            jax.ShapeDtypeStruct((t_all, D_MODEL), F32),
            jax.ShapeDtypeStruct((t_all, D_WORDS), jnp.int32),
            jax.ShapeDtypeStruct((N_EXPERTS, t_all), F32),
            jax.ShapeDtypeStruct((n_prompt_seq, CONV_STATE, C_CONV), F32),
        ),
        scratch_shapes=common_scratch(ROW_TILE)
        + [pltpu.VMEM((SUBLANES, CARRY_ROWS + ROW_TILE, C_CONV), F32)]
        + tail_scratch(ROW_TILE, BF16) + x_scratch,
        compiler_params=pltpu.CompilerParams(
            dimension_semantics=("arbitrary",), vmem_limit_bytes=VMEM_LIMIT),
        name="mixer_prompt",
    )(x_p, mod_p, *moe_p_args, w_in_b, w_out_b, wconv_p, vec, g_mat, w_s, bs_full, w_router, b_router,
      smid, sh2, slogits)
    return xmid, h2, logits, cst, a_new, v_new


def _tile_plan(cnt):
    e_sub = lax.broadcasted_iota(jnp.int32, (N_EXPERTS, N_EXPERTS), 0)
    e_lane = lax.broadcasted_iota(jnp.int32, (N_EXPERTS, N_EXPERTS), 1)
    tiles = jnp.floor((cnt + (GEMM_TILE - 1.0)) * (1.0 / GEMM_TILE))
    active = jnp.where(cnt > 0.0, 1.0, 0.0)

    def over_experts(tri, col):
        return _dot(tri.astype(BF16), jnp.broadcast_to(col, (N_EXPERTS, LANES)).astype(BF16))[:, 0:1]
    tile_end = over_experts(jnp.where(e_lane <= e_sub, 1.0, 0.0), tiles)
    tile_start = tile_end - tiles
    ordinal = over_experts(jnp.where(e_lane < e_sub, 1.0, 0.0), active)

    g = lax.broadcasted_iota(jnp.int32, (1, PLAN_LANES), 1).astype(F32)
    e_id = lax.broadcasted_iota(jnp.int32, (N_EXPERTS, PLAN_LANES), 0).astype(F32)
    done = jnp.sum(jnp.where(tile_end <= g, 1.0, 0.0), axis=0, keepdims=True)
    live = done < float(N_EXPERTS)
    last_e = jnp.max(jnp.where(cnt > 0.0, e_id[:, 0:1], 0.0), axis=0, keepdims=True)
    te = jnp.where(live, done, last_e)
    pick = e_id == te

    def at_tile(col):
        return jnp.sum(jnp.where(pick, col, 0.0), axis=0, keepdims=True)
    start_g = at_tile(tile_start)
    ord_g = at_tile(ordinal)
    valid = jnp.clip(at_tile(cnt) - (g - start_g) * GEMM_TILE, 0.0, float(GEMM_TILE))
    valid = jnp.where(live, valid, 0.0)
    first = jnp.where(live, jnp.where(g == start_g, 1.0, 0.0), 0.0)
    slot = ord_g - 2.0 * jnp.floor(ord_g * 0.5)
    later = jnp.where(e_id > te, jnp.where(cnt > 0.0, e_id, float(N_EXPERTS)), float(N_EXPERTS))
    nxt = jnp.min(later, axis=0, keepdims=True)
    nxt = jnp.where(nxt >= float(N_EXPERTS), -1.0, nxt)

    row = lax.broadcasted_iota(jnp.int32, (PLAN_ROWS, PLAN_LANES), 0)
    plan = jnp.zeros((PLAN_ROWS, PLAN_LANES), F32)
    for r, v in ((PLAN_EXPERT, te), (PLAN_VALID, valid), (PLAN_FIRST, first), (PLAN_SLOT, slot),
                 (PLAN_NEXT, nxt)):
        plan = jnp.where(row == r, v, plan)
    return plan.astype(jnp.int32), tile_start * GEMM_TILE


def _router_kernel(lg_ref, wgt_ref, pos_ref, plan_ref, cnt_scr, run_scr, start_scr, earlier_scr):
    ph = pl.program_id(0)
    i = pl.program_id(1)
    n = lg_ref.shape[1]

    @pl.when(jnp.logical_and(ph == 0, i == 0))
    def _():
        cnt_scr[...] = jnp.zeros_like(cnt_scr)

    lg = lg_ref[...]
    sub = lax.broadcasted_iota(jnp.int32, lg.shape, 0).astype(F32)
    vals, sels = [], []
    for _ in range(TOP_K):
        m = jnp.max(lg, axis=0, keepdims=True)
        idx = jnp.min(jnp.where(lg == m, sub, float(N_EXPERTS)), axis=0, keepdims=True)
        sel = sub == idx
        vals.append(m)
        sels.append(sel)
        lg = jnp.where(sel, -jnp.inf, lg)
    onehot = jnp.zeros(lg.shape, F32)
    for sel in sels:
        onehot = onehot + jnp.where(sel, 1.0, 0.0)
    tile_cnt = jnp.sum(onehot, axis=1, keepdims=True)

    @pl.when(ph == 0)
    def _():
        cnt_scr[...] = cnt_scr[...] + tile_cnt

    @pl.when(jnp.logical_and(ph == 1, i == 0))
    def _():
        plan, row_start = _tile_plan(cnt_scr[...])
        plan_ref[...] = plan
        start_scr[...] = row_start
        run_scr[...] = jnp.zeros_like(run_scr)
        t_row = lax.broadcasted_iota(jnp.int32, (n, n), 0)
        t_col = lax.broadcasted_iota(jnp.int32, (n, n), 1)
        earlier_scr[...] = jnp.where(t_row < t_col, 1.0, 0.0).astype(BF16)

    @pl.when(ph == 1)
    def _():
        exps = [jnp.exp(v - vals[0]) for v in vals]
        inv = 1.0 / (exps[0] + exps[1] + exps[2] + exps[3])
        w_row = lax.broadcasted_iota(jnp.int32, (LANES, n), 0)
        w_all = jnp.zeros((LANES, n), F32)
        for k in range(TOP_K):
            w_all = jnp.where(w_row == k, exps[k] * inv, w_all)
        wgt_ref[...] = w_all.T
        slot0 = _dot(onehot.astype(BF16), earlier_scr[...]) + (run_scr[...] + start_scr[...])
        for k in range(TOP_K):
            pos_k = jnp.sum(jnp.where(sels[k], slot0, 0.0), axis=0, keepdims=True)
            pos_ref[k:k + 1, :] = pos_k.astype(jnp.int32)
        run_scr[...] = run_scr[...] + tile_cnt


def _router(logits):
    t_all = logits.shape[1]
    col = pltpu.VMEM((N_EXPERTS, 1), F32)
    return pl.pallas_call(
        _router_kernel,
        grid=(2, t_all // ROUTER_TILE),
        in_specs=[pl.BlockSpec((N_EXPERTS, ROUTER_TILE), lambda ph, i: (0, i))],
        out_specs=(
            pl.BlockSpec((ROUTER_TILE, LANES), lambda ph, i: (i * ph, 0)),
            pl.BlockSpec((TOP_K, ROUTER_TILE), lambda ph, i: (0, i * ph)),
            _const_out_spec((PLAN_ROWS, PLAN_LANES)),
        ),
        out_shape=(
            jax.ShapeDtypeStruct((t_all, LANES), F32),
            jax.ShapeDtypeStruct((TOP_K, t_all), jnp.int32),
            jax.ShapeDtypeStruct((PLAN_ROWS, PLAN_LANES), jnp.int32),
        ),
        scratch_shapes=[col, col, col, pltpu.VMEM((ROUTER_TILE, ROUTER_TILE), BF16)],
        compiler_params=pltpu.CompilerParams(
            dimension_semantics=("arbitrary", "arbitrary"), vmem_limit_bytes=VMEM_LIMIT),
        name="router",
    )(logits)


def _expert_kernel(plan_ref, x_ref, w1_hbm, b1_ref, w2_hbm, b2_ref, o_ref,
                   w1f_scr, w2f_scr, w1b_scr, w2b_scr, sem, *, layer):
    def weight_copies(expert, slot):
        return (pltpu.make_async_copy(w1_hbm.at[layer, expert], w1f_scr.at[slot], sem.at[0, slot]),
                pltpu.make_async_copy(w2_hbm.at[layer, expert], w2f_scr.at[slot], sem.at[1, slot]))

    for sub in range(TILES_PER_STEP):
        _expert_tile(pl.program_id(0) * TILES_PER_STEP + sub, sub * GEMM_TILE, plan_ref, x_ref,
                     b1_ref, b2_ref, o_ref, w1f_scr, w2f_scr, w1b_scr, w2b_scr, weight_copies)


def _expert_tile(g, row0, plan_ref, x_ref, b1_ref, b2_ref, o_ref, w1f_scr, w2f_scr, w1b_scr, w2b_scr,
                 weight_copies):
    nv = plan_ref[PLAN_VALID, g]
    expert = plan_ref[PLAN_EXPERT, g]

    @pl.when(plan_ref[PLAN_FIRST, g] == 1)
    def _():
        slot = plan_ref[PLAN_SLOT, g]
        nxt = plan_ref[PLAN_NEXT, g]

        @pl.when(g == 0)
        def _():
            for cp in weight_copies(expert, slot):
                cp.start()
        for cp in weight_copies(expert, slot):
            cp.wait()

        @pl.when(nxt >= 0)
        def _():
            for cp in weight_copies(nxt, 1 - slot):
                cp.start()
        w1b_scr[...] = w1f_scr[slot].astype(BF16)
        w2b_scr[...] = w2f_scr[slot].astype(BF16)

    def ffn(m):
        x_left, x_right = _unpack_bf16_pair(x_ref[row0:row0 + m, :])
        x = jnp.concatenate([x_left.astype(BF16), x_right.astype(BF16)], axis=1)

        def hidden(col0):
            cols = slice(col0, col0 + FF_BLOCK)
            return _dot(x, w1b_scr[:, cols]) + b1_ref[expert, :, cols]

        n_blocks = D_FF // FF_BLOCK
        y = jnp.zeros((m, D_MODEL), F32) + b2_ref[expert]
        pre = (hidden(0), hidden(D_FF))
        for jb in range(n_blocks):
            g_pre, up_pre = pre
            if jb + 1 < n_blocks:
                pre = (hidden((jb + 1) * FF_BLOCK), hidden(D_FF + (jb + 1) * FF_BLOCK))
            gg = jnp.minimum(g_pre, SWIGLU_LIMIT)
            up = jnp.clip(up_pre, -SWIGLU_LIMIT, SWIGLU_LIMIT)
            act = gg * _sigmoid(SWIGLU_ALPHA * gg) * (up + 1.0)
            y = y + _dot(act.astype(BF16), w2b_scr[jb * FF_BLOCK:(jb + 1) * FF_BLOCK, :])
        row = lax.broadcasted_iota(jnp.int32, y.shape, 0)
        o_ref[row0:row0 + m, :] = _pack_bf16_pair(jnp.where(row < nv, y, 0.0))
        if m < GEMM_TILE:
            o_ref[row0 + m:row0 + GEMM_TILE, :] = jnp.zeros((GEMM_TILE - m, D_WORDS), jnp.int32)

    @pl.when(nv > GEMM_TILE // 2)
    def _():
        ffn(GEMM_TILE)

    @pl.when(jnp.logical_and(nv > 0, nv <= GEMM_TILE // 2))
    def _():
        ffn(GEMM_TILE // 2)

    @pl.when(nv == 0)
    def _():
        o_ref[row0:row0 + GEMM_TILE, :] = jnp.zeros((GEMM_TILE, D_WORDS), jnp.int32)


def _experts(xs, plan, w1, b1, w2, b2, layer):
    n_tiles = xs.shape[0] // GEMM_TILE
    depth = w1.shape[0]
    assert n_tiles % TILES_PER_STEP == 0
    step_rows = TILES_PER_STEP * GEMM_TILE
    bias_map = lambda s, plan: (layer, 0, 0, 0)
    grid_spec = pltpu.PrefetchScalarGridSpec(
        num_scalar_prefetch=1,
        grid=(n_tiles // TILES_PER_STEP,),
        in_specs=[
            pl.BlockSpec((step_rows, D_WORDS), lambda s, plan: (s, 0)),
            pl.BlockSpec(memory_space=pl.ANY),
            pl.BlockSpec((None, N_EXPERTS, 1, 2 * D_FF), bias_map, pipeline_mode=pl.Buffered(1)),
            pl.BlockSpec(memory_space=pl.ANY),
            pl.BlockSpec((None, N_EXPERTS, 1, D_MODEL), bias_map, pipeline_mode=pl.Buffered(1)),
        ],
        out_specs=pl.BlockSpec((step_rows, D_WORDS), lambda s, plan: (s, 0)),
        scratch_shapes=[
            pltpu.VMEM((2, D_MODEL, 2 * D_FF), F32),
            pltpu.VMEM((2, D_FF, D_MODEL), F32),
            pltpu.VMEM((D_MODEL, 2 * D_FF), BF16),
            pltpu.VMEM((D_FF, D_MODEL), BF16),
            pltpu.SemaphoreType.DMA((2, 2)),
        ],
    )
    return pl.pallas_call(
        functools.partial(_expert_kernel, layer=layer),
        grid_spec=grid_spec,
        out_shape=jax.ShapeDtypeStruct((n_tiles * GEMM_TILE, D_WORDS), jnp.int32),
        compiler_params=pltpu.CompilerParams(
            dimension_semantics=("arbitrary",), vmem_limit_bytes=VMEM_LIMIT),
        name="experts",
    )(plan, xs, w1, b1.reshape(depth, N_EXPERTS, 1, 2 * D_FF), w2,
      b2.reshape(depth, N_EXPERTS, 1, D_MODEL))


def _final_kernel(x_ref, yg_ref, wgt_ref, gp_ref, gs_ref, fg_ref, op_ref, os_ref, *, n_prompt_blocks):
    i = pl.program_id(0)
    n_seq = gs_ref.shape[0]

    def run(gate, o_ref):
        def body(r0):
            r = pl.ds(r0, COMBINE_CHUNK)
            w = wgt_ref[r, :]
            g = gate(r0)
            xl = x_ref[r, 0:D_WORDS] + g[:, 0:D_WORDS] * _moe_half(yg_ref, w, r, False)
            xr = x_ref[r, D_WORDS:D_MODEL] + g[:, D_WORDS:D_MODEL] * _moe_half(yg_ref, w, r, True)
            ms = (jnp.sum(xl * xl, axis=-1, keepdims=True)
                  + jnp.sum(xr * xr, axis=-1, keepdims=True)) * (1.0 / D_MODEL)
            scale = lax.rsqrt(ms + EPS)
            o_ref[r, 0:D_WORDS] = xl * scale * fg_ref[:, 0:D_WORDS]
            o_ref[r, D_WORDS:D_MODEL] = xr * scale * fg_ref[:, D_WORDS:D_MODEL]
        _row_loop(x_ref.shape[0], body, COMBINE_CHUNK)

    @pl.when(i < n_prompt_blocks)
    def _():
        run(lambda r0: gp_ref[0], op_ref)

    @pl.when(i >= n_prompt_blocks)
    def _():
        run(lambda r0: gs_ref[pl.ds(pl.multiple_of(lax.rem(r0, n_seq), COMBINE_CHUNK), COMBINE_CHUNK), :],
            os_ref)


def _final(xmid, yg, wgt, gate_p, gate_s, final_g, n_prompt, seq_len):
    t_all = xmid.shape[0]
    n_prompt_blocks = n_prompt // ROW_TILE
    per_seq = seq_len // ROW_TILE
    n_seq_p = gate_p.shape[0]
    assert t_all - n_prompt == ROW_TILE
    out_specs = (
        pl.BlockSpec((ROW_TILE, D_MODEL), lambda i: (jnp.minimum(i, n_prompt_blocks - 1), 0)),
        pl.BlockSpec((ROW_TILE, D_MODEL), lambda i: (0, 0)),
    )
    out_shape = (jax.ShapeDtypeStruct((n_prompt, D_MODEL), F32),
                 jax.ShapeDtypeStruct((ROW_TILE, D_MODEL), F32))
    return pl.pallas_call(
        functools.partial(_final_kernel, n_prompt_blocks=n_prompt_blocks),
        grid=(t_all // ROW_TILE,),
        in_specs=[
            pl.BlockSpec((ROW_TILE, D_MODEL), lambda i: (i, 0)),
            pl.BlockSpec((TOP_K, ROW_TILE, D_WORDS), lambda i: (0, i, 0)),
            pl.BlockSpec((ROW_TILE, LANES), lambda i: (i, 0)),
            pl.BlockSpec((1, 1, D_MODEL), lambda i: (jnp.minimum(i // per_seq, n_seq_p - 1), 0, 0)),
            _const_spec(gate_s.shape),
            _const_spec((1, D_MODEL)),
        ],
        out_specs=out_specs,
        out_shape=out_shape,
        compiler_params=pltpu.CompilerParams(
            dimension_semantics=("arbitrary",), vmem_limit_bytes=VMEM_LIMIT),
        name="final_combine",
    )(xmid, yg, wgt, gate_p.reshape(n_seq_p, 1, D_MODEL), gate_s, final_g.reshape(1, D_MODEL))


def _sc_mesh():
    return plsc.VectorSubcoreMesh(core_axis_name="core", subcore_axis_name="subcore",
                                  num_cores=SC_CORES, num_subcores=SC_SUBCORES)


def _sc_worker_id():
    return lax.axis_index("subcore") * SC_CORES + lax.axis_index("core")


def _sc_dispatch(h2w, pos_c, n_slots):
    n_chunks = pos_c.shape[1]

    @functools.partial(
        pl.kernel, mesh=_sc_mesh(),
        out_type=jax.ShapeDtypeStruct((n_slots, D_WORDS), jnp.int32),
        scratch_types=[pltpu.VMEM((TOP_K, SC_CHUNK), jnp.int32),
                       pltpu.VMEM((SC_CHUNK, D_WORDS), jnp.int32),
                       pltpu.SemaphoreType.DMA],
        name="moe_dispatch")
    def run(h_hbm, pos_hbm, out_hbm, idx_v, rows_v, sem):
        wid = _sc_worker_id()

        @pl.loop(0, n_chunks)
        def _(c):
            base = pl.multiple_of((wid * n_chunks + c) * SC_CHUNK, SUBLANES)
            pltpu.sync_copy(pos_hbm.at[wid, c], idx_v)
            pltpu.sync_copy(h_hbm.at[pl.ds(base, SC_CHUNK)], rows_v)
            copies = [pltpu.async_copy(rows_v, out_hbm.at[idx_v.at[k]], sem) for k in range(TOP_K)]
            for cp in copies:
                cp.wait()

    return run(h2w, pos_c)


def _sc_gather_back(ys, pos_c, t_all):
    n_chunks = pos_c.shape[1]

    @functools.partial(
        pl.kernel, mesh=_sc_mesh(),
        out_type=jax.ShapeDtypeStruct((TOP_K, t_all, D_WORDS), jnp.int32),
        scratch_types=[pltpu.VMEM((TOP_K, SC_CHUNK), jnp.int32),
                       pltpu.VMEM((TOP_K, SC_CHUNK, D_WORDS), jnp.int32),
                       pltpu.SemaphoreType.DMA],
        name="moe_gather_back")
    def run(ys_hbm, pos_hbm, out_hbm, idx_v, rows_v, sem):
        wid = _sc_worker_id()

        @pl.loop(0, n_chunks)
        def _(c):
            base = pl.multiple_of((wid * n_chunks + c) * SC_CHUNK, SUBLANES)
            pltpu.sync_copy(pos_hbm.at[wid, c], idx_v)
            copies = [pltpu.async_copy(ys_hbm.at[idx_v.at[k]], rows_v.at[k], sem) for k in range(TOP_K)]
            for cp in copies:
                cp.wait()
            for k in range(TOP_K):
                pltpu.sync_copy(rows_v.at[k], out_hbm.at[k, pl.ds(base, SC_CHUNK)])

    return run(ys, pos_c)


def kernel(x_prompt, x_sample, state_conv, c_prompt, c_sample, w_ada, b_ada, w_in, w_conv, b_conv,
           gn_g, gn_b, sgu_ln_g, sgu_ln_b, w_s, b_s, beta_a, beta_b, w_out, w_router, b_router,
           w1, b1, w2, b2, final_g):
    n_bp, seq_len, _ = x_prompt.shape
    n_bs, n_pos, _ = x_sample.shape
    depth = w_ada.shape[0]
    n_prompt = n_bp * seq_len
    n_sample = n_bs * n_pos
    t_all = n_prompt + n_sample
    n_tiles = (t_all * TOP_K) // GEMM_TILE + N_EXPERTS
    n_workers = SC_CORES * SC_SUBCORES
    assert t_all % (n_workers * SC_CHUNK) == 0 and n_tiles <= PLAN_LANES and t_all % ROUTER_TILE == 0

    x_p = x_prompt.reshape(n_prompt, D_MODEL)
    x_s = x_sample.transpose(1, 0, 2).reshape(n_sample, D_MODEL)
    x_s_block = 0
    moe = None
    mods =_adaln(jnp.concatenate([c_prompt, c_sample], axis=0), w_ada, b_ada)
    state_t = state_conv.transpose(0, 2, 1, 3)
    grp = jnp.arange(C_CONV) // (C_CONV // CONV_GROUPS)
    g_mat = ((grp[:, None] == grp[None, :]).astype(F32) / (C_CONV // CONV_GROUPS)).astype(BF16)

    conv_p, conv_s, v_s = [], [], []
    for l in range(depth):
        mod_p = mods[l, :n_bp].reshape(n_bp, 1, N_MOD * D_MODEL)
        mod_s = mods[l, n_bp:]
        wconv_p = jnp.pad(w_conv[l], ((0, CARRY_ROWS - CONV_WIDTH), (0, 0)))
        zero = jnp.zeros_like(b_conv[l])
        vec = jnp.stack([b_conv[l], gn_g[l], gn_b[l], sgu_ln_g[l], sgu_ln_b[l],
                         beta_a[l], beta_b[l], zero], axis=0)
        bs_full = jnp.repeat(b_s[l].T, SGU_HEAD_DIM, axis=1)
        wsv = jnp.repeat(w_s[l][:, :n_pos, :n_pos].transpose(1, 2, 0).reshape(n_pos * n_pos, SGU_HEADS),
                         SGU_HEAD_DIM, axis=1)
        bsv = jnp.repeat(b_s[l][:, :n_pos].T, SGU_HEAD_DIM, axis=1)
        xmid, h2, logits, cst, a_new, v_new = _mixer(
            x_p, x_s, x_s_block, moe, mod_p, mod_s, w_in[l].astype(BF16), w_out[l].astype(BF16), wconv_p,
            vec, g_mat,
            w_s[l], bs_full, wsv, bsv, w_router[l].T, b_router[l].reshape(N_EXPERTS, 1),
            state_t[l], n_bp, seq_len)
        conv_p.append(cst)
        conv_s.append(a_new)
        v_s.append(v_new)

        wgt, pos, plan = _router(logits)
        pos_c = pos.reshape(TOP_K, n_workers, -1, SC_CHUNK).transpose(1, 2, 0, 3)
        xs = _sc_dispatch(h2, pos_c, n_tiles * GEMM_TILE)
        ys = _experts(xs, plan, w1, b1, w2, b2, l)
        yg = _sc_gather_back(ys, pos_c, t_all)
        moe = (yg, wgt, mod_p[:, 0, 5 * D_MODEL:], mod_s[:, 5 * D_MODEL:])
        x_p = x_s = xmid
        x_s_block = n_prompt // n_sample

    y_p, y_s = _final(xmid, *moe, final_g, n_prompt, seq_len)
    y_prompt = y_p.reshape(n_bp, seq_len, D_MODEL)
    y_sample = y_s.reshape(n_pos, n_bs, D_MODEL).transpose(1, 0, 2)
    new_conv_s = jnp.stack(conv_s, axis=0).transpose(0, 2, 1, 3)
    new_v_s = jnp.stack(v_s, axis=0).reshape(depth, n_pos, n_bs, C_SGU).transpose(0, 2, 1, 3)
    return (y_prompt, y_sample, jnp.stack(conv_p, axis=0), new_conv_s, new_v_s)
```

```python
import functools
import math

import jax
import jax.numpy as jnp
from jax import lax
from jax.experimental import pallas as pl
from jax.experimental.pallas import tpu as pltpu
from jax.experimental.pallas import tpu_sc as plsc

F32 = jnp.float32
BF16 = jnp.bfloat16

D_MODEL = 1024
C_CONV = 512
C_SGU = 512
CONV_GROUPS = 8
CONV_WIDTH = 31
CONV_STATE = CONV_WIDTH - 1
SGU_HEADS = 4
SGU_HEAD_DIM = C_SGU // SGU_HEADS
CHUNK = 128
N_EXPERTS = 32
TOP_K = 4
D_FF = 1024
SWIGLU_LIMIT = 7.0
SWIGLU_ALPHA = 1.702
N_MOD = 6
EPS = 1e-5

ROW_TILE = 512
ROUTER_TILE = 1536
ROW_CHUNK = 128
CONV_CHUNK = 32
COMBINE_CHUNK = 64
CARRY_ROWS = 32
SUBLANES = 8
LANES = 128
PLAN_ROWS = 8
PLAN_LANES = 256
PLAN_EXPERT, PLAN_VALID, PLAN_FIRST, PLAN_SLOT, PLAN_NEXT = range(5)
GEMM_TILE = 512
FF_BLOCK = 256
TILES_PER_STEP = 4
VMEM_LIMIT = 56 * 1024 * 1024
D_WORDS = D_MODEL // 2
SC_CHUNK = 48
HI_MASK = -65536
SC_CORES = 2
SC_SUBCORES = 16


def _rms(x):
    return x * lax.rsqrt(jnp.mean(x * x, axis=-1, keepdims=True) + EPS)


def _gelu(x):
    return 0.5 * x * (1.0 + lax.erf(x * (1.0 / math.sqrt(2.0))))


def _sigmoid(x):
    return 1.0 / (1.0 + jnp.exp(-x))


def _split_bf16(x):
    hi = x.astype(BF16)
    lo = (x - hi.astype(F32)).astype(BF16)
    return hi, lo


def _pack_bf16_pair(x):
    bits = lax.bitcast_convert_type(x.astype(BF16).astype(F32), jnp.int32)
    return lax.shift_right_logical(bits[:, :D_WORDS], 16) | (bits[:, D_WORDS:] & HI_MASK)


def _unpack_bf16_pair(w):
    left = lax.bitcast_convert_type(lax.shift_left(w, 16), F32)
    right = lax.bitcast_convert_type(w & HI_MASK, F32)
    return left, right


def _dot(a, b):
    return jnp.dot(a, b, preferred_element_type=F32)


def _dot_nt(a, b):
    return lax.dot_general(a, b, (((1,), (1,)), ((), ())), preferred_element_type=F32)


def _row_loop(n_rows, body, chunk=None):
    chunk = ROW_CHUNK if chunk is None else chunk

    def step(i, carry):
        body(pl.multiple_of(i * chunk, chunk))
        return carry
    lax.fori_loop(0, n_rows // chunk, step, 0)


def _conv_tap(wconv_ref, k):
    return jnp.concatenate([wconv_ref[k]] * (CONV_CHUNK // SUBLANES), axis=0)


def _static_loop(n_rows, body, chunk=None):
    chunk = ROW_CHUNK if chunk is None else chunk
    for r0 in range(0, n_rows, chunk):
        body(r0)


def _adaln_kernel(c_ref, w_ref, b_ref, o_ref):
    c = c_ref[...]
    s_hi, s_lo = _split_bf16(c * _sigmoid(c))
    w_hi, w_lo = _split_bf16(w_ref[0])
    acc = _dot(s_hi, w_hi) + _dot(s_hi, w_lo) + _dot(s_lo, w_hi)
    o_ref[0] = acc + b_ref[0]


def _adaln(c_all, w_ada, b_ada):
    depth, _, n_out = w_ada.shape
    n_rows = c_all.shape[0]
    tn = 1024
    return pl.pallas_call(
        _adaln_kernel,
        grid=(depth, n_out // tn),
        in_specs=[
            pl.BlockSpec((n_rows, D_MODEL), lambda l, j: (0, 0)),
            pl.BlockSpec((1, D_MODEL, tn), lambda l, j: (l, 0, j)),
            pl.BlockSpec((1, 1, tn), lambda l, j: (l, 0, j)),
        ],
        out_specs=pl.BlockSpec((1, n_rows, tn), lambda l, j: (l, 0, j)),
        out_shape=jax.ShapeDtypeStruct((depth, n_rows, n_out), F32),
        compiler_params=pltpu.CompilerParams(
            dimension_semantics=("arbitrary", "arbitrary"), vmem_limit_bytes=VMEM_LIMIT),
        name="adaln",
    )(c_all, w_ada, b_ada.reshape(depth, 1, n_out))


def _phase_in_norm(x_ref, h_scr, mod, n_rows, loop=_row_loop):
    def body(r0):
        r = pl.ds(r0, ROW_CHUNK)
        h = _rms(x_ref[r, :]) * (1.0 + mod(1, r0)) + mod(0, r0)
        h_scr[r, :] = h.astype(BF16)
    loop(n_rows, body)


def _phase_glu(z_scr, a_dst, a_off, n_rows, loop=_row_loop):
    def body(r0):
        r = pl.ds(r0, ROW_CHUNK)
        a_dst[pl.ds(r0 + a_off, ROW_CHUNK), :] = z_scr[r, 0:C_CONV] * _sigmoid(z_scr[r, C_CONV:2 * C_CONV])
    loop(n_rows, body)


def _phase_uv(z_scr, vec_ref, u_scr, v_scr, vout_ref, n_rows, loop=_row_loop):
    ln_g = vec_ref[3:4, :]
    ln_b = vec_ref[4:5, :]

    def body(r0):
        r = pl.ds(r0, ROW_CHUNK)
        u_scr[r, :] = _gelu(z_scr[r, 2 * C_CONV:2 * C_CONV + C_SGU])
        gv = _gelu(z_scr[r, 2 * C_CONV + C_SGU:2 * C_CONV + 2 * C_SGU])
        mu = jnp.mean(gv, axis=-1, keepdims=True)
        dv = gv - mu
        var = jnp.mean(dv * dv, axis=-1, keepdims=True)
        v = dv * lax.rsqrt(var + EPS) * ln_g + ln_b
        if vout_ref is not None:
            vout_ref[r, :] = v
        v_scr[r, :] = v.astype(v_scr.dtype)
    loop(n_rows, body)


def _phase_group_norm(conv_scr, convb_scr, stat_scr, g_ref, vec_ref, y_scr, n_rows, loop=_row_loop):
    stat_scr[...] = _dot(convb_scr[...], g_ref[...])

    def center(r0):
        r = pl.ds(r0, ROW_CHUNK)
        d = conv_scr[r, :] - stat_scr[r, :]
        conv_scr[r, :] = d
        convb_scr[r, :] = (d * d).astype(BF16)
    loop(n_rows, center)
    stat_scr[...] = _dot(convb_scr[...], g_ref[...])
    gn_g = vec_ref[1:2, :]
    gn_b = vec_ref[2:3, :]
    beta_a = vec_ref[5:6, :]

    def finish(r0):
        r = pl.ds(r0, ROW_CHUNK)
        gn = conv_scr[r, :] * lax.rsqrt(stat_scr[r, :] + EPS) * gn_g + gn_b
        ya = gn * _sigmoid(gn)
        y_scr[r, 0:C_CONV] = (_rms(ya) * beta_a).astype(BF16)
    loop(n_rows, finish)


def _phase_out(x_ref, y_scr, wout_ref, z_scr, wr_ref, br_ref, mod, h_scr,
               xmid_ref, h2_ref, logit_ref, n_rows, loop=_row_loop):
    z_scr[:, 0:D_MODEL] = (_dot(y_scr[:, 0:C_CONV], wout_ref[0:C_CONV, :])
                           + _dot(y_scr[:, C_CONV:C_CONV + C_SGU], wout_ref[C_CONV:C_CONV + C_SGU, :]))

    def body(r0):
        r = pl.ds(r0, ROW_CHUNK)
        xm = x_ref[r, :] + mod(2, r0) * z_scr[r, 0:D_MODEL]
        xmid_ref[r, :] = xm
        h2 = _rms(xm) * (1.0 + mod(4, r0)) + mod(3, r0)
        hi, lo = _split_bf16(h2)
        h2_ref[r, :] = _pack_bf16_pair(h2)
        y_scr[r, :] = hi
        h_scr[r, :] = lo
    loop(n_rows, body)
    w_hi, w_lo = _split_bf16(wr_ref[...])
    both = _dot_nt(jnp.concatenate([w_hi, w_lo], axis=0), y_scr[...])
    logit_ref[...] = (both[0:N_EXPERTS] + both[N_EXPERTS:2 * N_EXPERTS]
                      + _dot_nt(w_hi, h_scr[...])) + br_ref[...]


def _moe_half(yg_ref, w, r, hi_half):
    moe = jnp.zeros((COMBINE_CHUNK, D_WORDS), F32)
    for k in range(TOP_K):
        word = yg_ref[k, r, :]
        bits = (word & HI_MASK) if hi_half else lax.shift_left(word, 16)
        moe = moe + w[:, k:k + 1] * lax.bitcast_convert_type(bits, F32)
    return moe


def _phase_moe_residual(xmid_ref, yg_ref, wgt_ref, gate, x_dst, n_rows, loop=_row_loop):
    def body(r0):
        r = pl.ds(r0, COMBINE_CHUNK)
        w = wgt_ref[r, :]
        g = gate(r0)
        for hi_half in (False, True):
            cols = slice(D_WORDS, D_MODEL) if hi_half else slice(0, D_WORDS)
            x_dst[r, cols] = xmid_ref[r, cols] + g[:, cols] * _moe_half(yg_ref, w, r, hi_half)
    loop(n_rows, body, COMBINE_CHUNK)


def _mixer_prompt_kernel(*refs, tiles_per_seq, n_prompt_tiles, fused):
    x_ref, mod_ref = refs[:2]
    refs = refs[2:]
    if fused:
        yg_ref, wgt_ref, gate_ref = refs[:3]
        refs = refs[3:]
        x_scr = refs[-1]
        refs = refs[:-1]
    (win_ref, wout_ref, wconv_ref, vec_ref, g_ref, ws_ref, bs_ref, wr_ref, br_ref,
     smid_ref, sh2_ref, slogit_ref, xmid_ref, h2_ref, logit_ref, cst_ref,
     h_scr, z_scr, aext_scr, conv_scr, convb_scr, stat_scr, u_scr, v_scr, y_scr) = refs
    i = pl.program_id(0)

    @pl.when(i == n_prompt_tiles)
    def _():
        xmid_ref[...] = smid_ref[...]
        h2_ref[...] = sh2_ref[...]
        logit_ref[...] = slogit_ref[...]

    @pl.when(i < n_prompt_tiles)
    def _():
        x_src = x_ref
        if fused:
            _phase_moe_residual(x_ref, yg_ref, wgt_ref, lambda r0: gate_ref[0], x_scr, ROW_TILE,
                                _static_loop)
            x_src = x_scr
        _mixer_prompt_tile(lax.rem(i, tiles_per_seq), tiles_per_seq,
                           x_src, mod_ref, win_ref, wout_ref, wconv_ref, vec_ref, g_ref,
                           ws_ref, bs_ref, wr_ref, br_ref, xmid_ref, h2_ref, logit_ref, cst_ref,
                           h_scr, z_scr, aext_scr, conv_scr, convb_scr, stat_scr, u_scr, v_scr, y_scr)


def _mixer_prompt_tile(j, tiles_per_seq, x_ref, mod_ref, win_ref, wout_ref, wconv_ref, vec_ref, g_ref,
                       ws_ref, bs_ref, wr_ref, br_ref, xmid_ref, h2_ref, logit_ref, cst_ref,
                       h_scr, z_scr, aext_scr, conv_scr, convb_scr, stat_scr, u_scr, v_scr, y_scr):
    n_rows = ROW_TILE

    def mod(idx, r0):
        del r0
        return mod_ref[0, :, idx * D_MODEL:(idx + 1) * D_MODEL]

    a_buf = aext_scr.at[0]

    @pl.when(j == 0)
    def _():
        a_buf[0:CARRY_ROWS, :] = jnp.zeros((CARRY_ROWS, C_CONV), F32)

    loop = _static_loop
    _phase_in_norm(x_ref, h_scr, mod, n_rows, loop)
    n_glu = 2 * C_CONV
    z_scr[:, 0:n_glu] = _dot(h_scr[...], win_ref[:, 0:n_glu])
    _phase_glu(z_scr, a_buf, CARRY_ROWS, n_rows, loop)
    z_scr[:, n_glu:] = _dot(h_scr[...], win_ref[:, n_glu:])
    n_shift = n_rows + CARRY_ROWS - SUBLANES
    for s in range(1, SUBLANES):
        aext_scr[s, 0:n_shift, :] = a_buf[s:s + n_shift, :]

    b_conv = vec_ref[0:1, :]
    lead = CARRY_ROWS - CONV_STATE
    for r0 in range(0, n_rows, CONV_CHUNK):
        acc = jnp.zeros((CONV_CHUNK, C_CONV), F32)
        for k in range(CONV_WIDTH):
            s = (lead + k) % SUBLANES
            q0 = r0 + (lead + k - s)
            acc = acc + _conv_tap(wconv_ref, k) * aext_scr[s, q0:q0 + CONV_CHUNK, :]
        conv = acc + b_conv
        conv_scr[r0:r0 + CONV_CHUNK, :] = conv
        convb_scr[r0:r0 + CONV_CHUNK, :] = conv.astype(BF16)
    _phase_uv(z_scr, vec_ref, u_scr, v_scr, None, n_rows, loop)
    _phase_group_norm(conv_scr, convb_scr, stat_scr, g_ref, vec_ref, y_scr, n_rows, loop)

    row_i = lax.broadcasted_iota(jnp.int32, (CHUNK, CHUNK), 0)
    col_i = lax.broadcasted_iota(jnp.int32, (CHUNK, CHUNK), 1)
    tril = (col_i <= row_i).astype(F32)
    for hd in range(SGU_HEADS):
        ws_h = (ws_ref[hd] * tril).astype(BF16)
        cols = slice(hd * SGU_HEAD_DIM, (hd + 1) * SGU_HEAD_DIM)
        s_cols = slice(n_glu + hd * SGU_HEAD_DIM, n_glu + (hd + 1) * SGU_HEAD_DIM)
        for c in range(n_rows // CHUNK):
            rows = slice(c * CHUNK, (c + 1) * CHUNK)
            z_scr[rows, s_cols] = _dot(ws_h, v_scr[rows, cols])
    beta_b = vec_ref[6:7, :]

    def gate_body(r0):
        r = pl.ds(r0, ROW_CHUNK)
        rb = pl.ds(r0 % CHUNK, ROW_CHUNK)
        yb = u_scr[r, :] * (z_scr[r, n_glu:n_glu + C_SGU] + bs_ref[rb, :])
        y_scr[r, C_CONV:C_CONV + C_SGU] = (_rms(yb) * beta_b).astype(BF16)
    loop(n_rows, gate_body)

    _phase_out(x_ref, y_scr, wout_ref, z_scr, wr_ref, br_ref, mod, h_scr,
               xmid_ref, h2_ref, logit_ref, n_rows, loop)

    @pl.when(j == tiles_per_seq - 1)
    def _():
        cst_ref[0] = a_buf[n_rows + lead:n_rows + CARRY_ROWS, :]

    a_buf[0:CARRY_ROWS, :] = a_buf[n_rows:n_rows + CARRY_ROWS, :]


def _mixer_sample_kernel(*refs, fused):
    x_ref, mod_ref = refs[:2]
    refs = refs[2:]
    if fused:
        yg_ref, wgt_ref, gate_ref = refs[:3]
        refs = refs[3:]
        x_scr = refs[-1]
        refs = refs[:-1]
    (win_ref, wout_ref, wconv_ref, vec_ref, g_ref, wsv_ref, bsv_ref, wr_ref, br_ref, state_ref,
     xmid_ref, h2_ref, logit_ref, cnew_ref, vout_ref,
     h_scr, z_scr, conv_scr, convb_scr, stat_scr, u_scr, v_scr, y_scr, anew_ref) = refs
    n_rows = x_ref.shape[0]
    n_seq = state_ref.shape[1]
    n_pos = n_rows // n_seq
    loop = _static_loop

    def mod(idx, r0):
        return mod_ref[pl.ds(r0 % n_seq, ROW_CHUNK), idx * D_MODEL:(idx + 1) * D_MODEL]

    if fused:
        def gate(r0):
            return gate_ref[pl.ds(r0 % n_seq, COMBINE_CHUNK), :]
        _phase_moe_residual(x_ref, yg_ref, wgt_ref, gate, x_scr, n_rows, loop)
        x_ref = x_scr

    _phase_in_norm(x_ref, h_scr, mod, n_rows, loop)
    z_scr[...] = _dot(h_scr[...], win_ref[...])
    _phase_glu(z_scr, anew_ref, 0, n_rows, loop)
    _phase_uv(z_scr, vec_ref, u_scr, v_scr, vout_ref, n_rows, loop)

    keep = CONV_STATE - n_pos
    cnew_ref[0:keep] = state_ref[n_pos:CONV_STATE]
    for t in range(n_pos):
        cnew_ref[keep + t] = anew_ref[t * n_seq:(t + 1) * n_seq, :]

    b_conv = vec_ref[0:1, :]

    def conv_body(s0):
        rs = pl.ds(s0, CONV_CHUNK)
        for t in range(n_pos):
            acc = jnp.zeros((CONV_CHUNK, C_CONV), F32)
            for k in range(CONV_WIDTH):
                p = t + k
                if p < CONV_STATE:
                    src = state_ref[p, rs, :]
                else:
                    src = anew_ref[pl.ds(s0 + (p - CONV_STATE) * n_seq, CONV_CHUNK), :]
                acc = acc + _conv_tap(wconv_ref, k) * src
            conv = acc + b_conv
            r = pl.ds(s0 + t * n_seq, CONV_CHUNK)
            conv_scr[r, :] = conv
            convb_scr[r, :] = conv.astype(BF16)
    loop(n_seq, conv_body, CONV_CHUNK)

    _phase_group_norm(conv_scr, convb_scr, stat_scr, g_ref, vec_ref, y_scr, n_rows, loop)

    beta_b = vec_ref[6:7, :]

    def gate_body(s0):
        for t in range(n_pos):
            s = jnp.zeros((CONV_CHUNK, C_SGU), F32)
            for jj in range(t + 1):
                s = s + wsv_ref[t * n_pos + jj:t * n_pos + jj + 1, :] * v_scr[pl.ds(s0 + jj * n_seq, CONV_CHUNK), :]
            r = pl.ds(s0 + t * n_seq, CONV_CHUNK)
            yb = u_scr[r, :] * (s + bsv_ref[t:t + 1, :])
            y_scr[r, C_CONV:C_CONV + C_SGU] = (_rms(yb) * beta_b).astype(BF16)
    loop(n_seq, gate_body, CONV_CHUNK)

    _phase_out(x_ref, y_scr, wout_ref, z_scr, wr_ref, br_ref, mod, h_scr,
               xmid_ref, h2_ref, logit_ref, n_rows, loop)


def _const_spec(shape):
    nd = len(shape)
    return pl.BlockSpec(shape, lambda *_: (0,) * nd, pipeline_mode=pl.Buffered(1))


def _const_out_spec(shape):
    nd = len(shape)
    return pl.BlockSpec(shape, lambda *_: (0,) * nd)


def _mixer(x_p, x_s, x_s_block, moe, mod_p, mod_s, w_in_b, w_out_b, wconv_p, vec, g_mat, w_s, bs_full,
           wsv, bsv, w_router, b_router, state_t, n_prompt_seq, seq_len):
    fused = moe is not None
    n_prompt = n_prompt_seq * seq_len
    n_seq, n_pos = state_t.shape[1], bsv.shape[0]
    n_sample = n_seq * n_pos
    t_all = n_prompt + n_sample
    nj = seq_len // ROW_TILE
    weight_specs = [
        _const_spec((D_MODEL, 2 * C_CONV + 2 * C_SGU)),
        _const_spec((D_MODEL, D_MODEL)),
        _const_spec((CARRY_ROWS, SUBLANES, C_CONV)),
        _const_spec((8, C_CONV)),
        _const_spec((C_CONV, C_CONV)),
    ]
    router_specs = [_const_spec((N_EXPERTS, D_MODEL)), _const_spec((N_EXPERTS, 1))]
    common_scratch = lambda n: [
        pltpu.VMEM((n, D_MODEL), BF16),
        pltpu.VMEM((n, 2 * C_CONV + 2 * C_SGU), F32),
    ]
    tail_scratch = lambda n, vdt: [
        pltpu.VMEM((n, C_CONV), F32),
        pltpu.VMEM((n, C_CONV), BF16),
        pltpu.VMEM((n, C_CONV), F32),
        pltpu.VMEM((n, C_SGU), F32),
        pltpu.VMEM((n, C_SGU), vdt),
        pltpu.VMEM((n, D_MODEL), BF16),
    ]
    assert n_sample == ROW_TILE
    n_tiles_p = n_prompt // ROW_TILE
    tile_p = lambda i: jnp.minimum(i, n_tiles_p - 1)
    seq_of = lambda i: jnp.minimum(i // nj, n_prompt_seq - 1)
    moe_s_specs, moe_p_specs, moe_s_args, moe_p_args, x_scratch = [], [], [], [], []
    if fused:
        yg, wgt, gate_p, gate_s = moe
        moe_s_specs = [pl.BlockSpec((TOP_K, n_sample, D_WORDS), lambda i: (0, x_s_block, 0)),
                       pl.BlockSpec((n_sample, LANES), lambda i: (x_s_block, 0)),
                       _const_spec((n_seq, D_MODEL))]
        moe_p_specs = [pl.BlockSpec((TOP_K, ROW_TILE, D_WORDS), lambda i: (0, tile_p(i), 0)),
                       pl.BlockSpec((ROW_TILE, LANES), lambda i: (tile_p(i), 0)),
                       pl.BlockSpec((1, 1, D_MODEL), lambda i: (seq_of(tile_p(i)), 0, 0))]
        moe_s_args = [yg, wgt, gate_s]
        moe_p_args = [yg, wgt, gate_p.reshape(n_prompt_seq, 1, D_MODEL)]
        x_scratch = [pltpu.VMEM((ROW_TILE, D_MODEL), F32)]
    smid, sh2, slogits, a_new, v_new = pl.pallas_call(
        functools.partial(_mixer_sample_kernel, fused=fused),
        grid=(1,),
        in_specs=[
            pl.BlockSpec((n_sample, D_MODEL), lambda i: (x_s_block, 0)),
            _const_spec((n_seq, N_MOD * D_MODEL)),
            *moe_s_specs,
            *weight_specs,
            _const_spec((n_pos * n_pos, C_SGU)),
            _const_spec((n_pos, C_SGU)),
            *router_specs,
            _const_spec((CONV_STATE, n_seq, C_CONV)),
        ],
        out_specs=(
            _const_out_spec((n_sample, D_MODEL)),
            _const_out_spec((n_sample, D_WORDS)),
            _const_out_spec((N_EXPERTS, n_sample)),
            _const_out_spec((CONV_STATE, n_seq, C_CONV)),
            _const_out_spec((n_sample, C_SGU)),
        ),
        out_shape=(
            jax.ShapeDtypeStruct((n_sample, D_MODEL), F32),
            jax.ShapeDtypeStruct((n_sample, D_WORDS), jnp.int32),
            jax.ShapeDtypeStruct((N_EXPERTS, n_sample), F32),
            jax.ShapeDtypeStruct((CONV_STATE, n_seq, C_CONV), F32),
            jax.ShapeDtypeStruct((n_sample, C_SGU), F32),
        ),
        scratch_shapes=common_scratch(n_sample) + tail_scratch(n_sample, F32)
        + [pltpu.VMEM((n_sample, C_CONV), F32)] + x_scratch,
        compiler_params=pltpu.CompilerParams(
            dimension_semantics=("arbitrary",), vmem_limit_bytes=VMEM_LIMIT),
        name="mixer_sample",
    )(x_s, mod_s, *moe_s_args, w_in_b, w_out_b, wconv_p, vec, g_mat, wsv, bsv, w_router, b_router,
      state_t)

    xmid, h2, logits, cst = pl.pallas_call(
        functools.partial(_mixer_prompt_kernel, tiles_per_seq=nj, n_prompt_tiles=n_tiles_p,
                          fused=fused),
        grid=(n_tiles_p + 1,),
        in_specs=[
            pl.BlockSpec((ROW_TILE, D_MODEL), lambda i: (tile_p(i), 0)),
            pl.BlockSpec((1, 1, N_MOD * D_MODEL), lambda i: (seq_of(i), 0, 0)),
            *moe_p_specs,
            *weight_specs,
            _const_spec((SGU_HEADS, CHUNK, CHUNK)),
            _const_spec((CHUNK, C_SGU)),
            *router_specs,
            _const_spec((n_sample, D_MODEL)),
            _const_spec((n_sample, D_WORDS)),
            _const_spec((N_EXPERTS, n_sample)),
        ],
        out_specs=(
            pl.BlockSpec((ROW_TILE, D_MODEL), lambda i: (i, 0)),
            pl.BlockSpec((ROW_TILE, D_WORDS), lambda i: (i, 0)),
            pl.BlockSpec((N_EXPERTS, ROW_TILE), lambda i: (0, i)),
            pl.BlockSpec((1, CONV_STATE, C_CONV), lambda i: (seq_of(i), 0, 0)),
        ),
        out_shape=(
            jax.ShapeDtypeStruct((t_all, D_MODEL), F32),
            jax.ShapeDtypeStruct((t_all, D_WORDS), jnp.int32),
            jax.ShapeDtypeStruct((N_EXPERTS, t_all), F32),
            jax.ShapeDtypeStruct((n_prompt_seq, CONV_STATE, C_CONV), F32),
        ),
        scratch_shapes=common_scratch(ROW_TILE)
        + [pltpu.VMEM((SUBLANES, CARRY_ROWS + ROW_TILE, C_CONV), F32)]
        + tail_scratch(ROW_TILE, BF16) + x_scratch,
        compiler_params=pltpu.CompilerParams(
            dimension_semantics=("arbitrary",), vmem_limit_bytes=VMEM_LIMIT),
        name="mixer_prompt",
    )(x_p, mod_p, *moe_p_args, w_in_b, w_out_b, wconv_p, vec, g_mat, w_s, bs_full, w_router, b_router,
      smid, sh2, slogits)
    return xmid, h2, logits, cst, a_new, v_new


def _tile_plan(cnt):
    e_sub = lax.broadcasted_iota(jnp.int32, (N_EXPERTS, N_EXPERTS), 0)
    e_lane = lax.broadcasted_iota(jnp.int32, (N_EXPERTS, N_EXPERTS), 1)
    tiles = jnp.floor((cnt + (GEMM_TILE - 1.0)) * (1.0 / GEMM_TILE))
    active = jnp.where(cnt > 0.0, 1.0, 0.0)

    def over_experts(tri, col):
        return _dot(tri.astype(BF16), jnp.broadcast_to(col, (N_EXPERTS, LANES)).astype(BF16))[:, 0:1]
    tile_end = over_experts(jnp.where(e_lane <= e_sub, 1.0, 0.0), tiles)
    tile_start = tile_end - tiles
    ordinal = over_experts(jnp.where(e_lane < e_sub, 1.0, 0.0), active)

    g = lax.broadcasted_iota(jnp.int32, (1, PLAN_LANES), 1).astype(F32)
    e_id = lax.broadcasted_iota(jnp.int32, (N_EXPERTS, PLAN_LANES), 0).astype(F32)
    done = jnp.sum(jnp.where(tile_end <= g, 1.0, 0.0), axis=0, keepdims=True)
    live = done < float(N_EXPERTS)
    last_e = jnp.max(jnp.where(cnt > 0.0, e_id[:, 0:1], 0.0), axis=0, keepdims=True)
    te = jnp.where(live, done, last_e)
    pick = e_id == te

    def at_tile(col):
        return jnp.sum(jnp.where(pick, col, 0.0), axis=0, keepdims=True)
    start_g = at_tile(tile_start)
    ord_g = at_tile(ordinal)
    valid = jnp.clip(at_tile(cnt) - (g - start_g) * GEMM_TILE, 0.0, float(GEMM_TILE))
    valid = jnp.where(live, valid, 0.0)
    first = jnp.where(live, jnp.where(g == start_g, 1.0, 0.0), 0.0)
    slot = ord_g - 2.0 * jnp.floor(ord_g * 0.5)
    later = jnp.where(e_id > te, jnp.where(cnt > 0.0, e_id, float(N_EXPERTS)), float(N_EXPERTS))
    nxt = jnp.min(later, axis=0, keepdims=True)
    nxt = jnp.where(nxt >= float(N_EXPERTS), -1.0, nxt)

    row = lax.broadcasted_iota(jnp.int32, (PLAN_ROWS, PLAN_LANES), 0)
    plan = jnp.zeros((PLAN_ROWS, PLAN_LANES), F32)
    for r, v in ((PLAN_EXPERT, te), (PLAN_VALID, valid), (PLAN_FIRST, first), (PLAN_SLOT, slot),
                 (PLAN_NEXT, nxt)):
        plan = jnp.where(row == r, v, plan)
    return plan.astype(jnp.int32), tile_start * GEMM_TILE


def _router_kernel(lg_ref, wgt_ref, pos_ref, plan_ref, cnt_scr, run_scr, start_scr, earlier_scr):
    ph = pl.program_id(0)
    i = pl.program_id(1)
    n = lg_ref.shape[1]

    @pl.when(jnp.logical_and(ph == 0, i == 0))
    def _():
        cnt_scr[...] = jnp.zeros_like(cnt_scr)

    lg = lg_ref[...]
    sub = lax.broadcasted_iota(jnp.int32, lg.shape, 0).astype(F32)
    vals, sels = [], []
    for _ in range(TOP_K):
        m = jnp.max(lg, axis=0, keepdims=True)
        idx = jnp.min(jnp.where(lg == m, sub, float(N_EXPERTS)), axis=0, keepdims=True)
        sel = sub == idx
        vals.append(m)
        sels.append(sel)
        lg = jnp.where(sel, -jnp.inf, lg)
    onehot = jnp.zeros(lg.shape, F32)
    for sel in sels:
        onehot = onehot + jnp.where(sel, 1.0, 0.0)
    tile_cnt = jnp.sum(onehot, axis=1, keepdims=True)

    @pl.when(ph == 0)
    def _():
        cnt_scr[...] = cnt_scr[...] + tile_cnt

    @pl.when(jnp.logical_and(ph == 1, i == 0))
    def _():
        plan, row_start = _tile_plan(cnt_scr[...])
        plan_ref[...] = plan
        start_scr[...] = row_start
        run_scr[...] = jnp.zeros_like(run_scr)
        t_row = lax.broadcasted_iota(jnp.int32, (n, n), 0)
        t_col = lax.broadcasted_iota(jnp.int32, (n, n), 1)
        earlier_scr[...] = jnp.where(t_row < t_col, 1.0, 0.0).astype(BF16)

    @pl.when(ph == 1)
    def _():
        exps = [jnp.exp(v - vals[0]) for v in vals]
        inv = 1.0 / (exps[0] + exps[1] + exps[2] + exps[3])
        w_row = lax.broadcasted_iota(jnp.int32, (LANES, n), 0)
        w_all = jnp.zeros((LANES, n), F32)
        for k in range(TOP_K):
            w_all = jnp.where(w_row == k, exps[k] * inv, w_all)
        wgt_ref[...] = w_all.T
        slot0 = _dot(onehot.astype(BF16), earlier_scr[...]) + (run_scr[...] + start_scr[...])
        for k in range(TOP_K):
            pos_k = jnp.sum(jnp.where(sels[k], slot0, 0.0), axis=0, keepdims=True)
            pos_ref[k:k + 1, :] = pos_k.astype(jnp.int32)
        run_scr[...] = run_scr[...] + tile_cnt


def _router(logits):
    t_all = logits.shape[1]
    col = pltpu.VMEM((N_EXPERTS, 1), F32)
    return pl.pallas_call(
        _router_kernel,
        grid=(2, t_all // ROUTER_TILE),
        in_specs=[pl.BlockSpec((N_EXPERTS, ROUTER_TILE), lambda ph, i: (0, i))],
        out_specs=(
            pl.BlockSpec((ROUTER_TILE, LANES), lambda ph, i: (i * ph, 0)),
            pl.BlockSpec((TOP_K, ROUTER_TILE), lambda ph, i: (0, i * ph)),
            _const_out_spec((PLAN_ROWS, PLAN_LANES)),
        ),
        out_shape=(
            jax.ShapeDtypeStruct((t_all, LANES), F32),
            jax.ShapeDtypeStruct((TOP_K, t_all), jnp.int32),
            jax.ShapeDtypeStruct((PLAN_ROWS, PLAN_LANES), jnp.int32),
        ),
        scratch_shapes=[col, col, col, pltpu.VMEM((ROUTER_TILE, ROUTER_TILE), BF16)],
        compiler_params=pltpu.CompilerParams(
            dimension_semantics=("arbitrary", "arbitrary"), vmem_limit_bytes=VMEM_LIMIT),
        name="router",
    )(logits)


def _expert_kernel(plan_ref, x_ref, w1_hbm, b1_ref, w2_hbm, b2_ref, o_ref,
                   w1f_scr, w2f_scr, w1b_scr, w2b_scr, sem, *, layer):
    def weight_copies(expert, slot):
        return (pltpu.make_async_copy(w1_hbm.at[layer, expert], w1f_scr.at[slot], sem.at[0, slot]),
                pltpu.make_async_copy(w2_hbm.at[layer, expert], w2f_scr.at[slot], sem.at[1, slot]))

    for sub in range(TILES_PER_STEP):
        _expert_tile(pl.program_id(0) * TILES_PER_STEP + sub, sub * GEMM_TILE, plan_ref, x_ref,
                     b1_ref, b2_ref, o_ref, w1f_scr, w2f_scr, w1b_scr, w2b_scr, weight_copies)


def _expert_tile(g, row0, plan_ref, x_ref, b1_ref, b2_ref, o_ref, w1f_scr, w2f_scr, w1b_scr, w2b_scr,
                 weight_copies):
    nv = plan_ref[PLAN_VALID, g]
    expert = plan_ref[PLAN_EXPERT, g]

    @pl.when(plan_ref[PLAN_FIRST, g] == 1)
    def _():
        slot = plan_ref[PLAN_SLOT, g]
        nxt = plan_ref[PLAN_NEXT, g]

        @pl.when(g == 0)
        def _():
            for cp in weight_copies(expert, slot):
                cp.start()
        for cp in weight_copies(expert, slot):
            cp.wait()

        @pl.when(nxt >= 0)
        def _():
            for cp in weight_copies(nxt, 1 - slot):
                cp.start()
        w1b_scr[...] = w1f_scr[slot].astype(BF16)
        w2b_scr[...] = w2f_scr[slot].astype(BF16)

    def ffn(m):
        x_left, x_right = _unpack_bf16_pair(x_ref[row0:row0 + m, :])
        x = jnp.concatenate([x_left.astype(BF16), x_right.astype(BF16)], axis=1)

        def hidden(col0):
            cols = slice(col0, col0 + FF_BLOCK)
            return _dot(x, w1b_scr[:, cols]) + b1_ref[expert, :, cols]

        n_blocks = D_FF // FF_BLOCK
        y = jnp.zeros((m, D_MODEL), F32) + b2_ref[expert]
        pre = (hidden(0), hidden(D_FF))
        for jb in range(n_blocks):
            g_pre, up_pre = pre
            if jb + 1 < n_blocks:
                pre = (hidden((jb + 1) * FF_BLOCK), hidden(D_FF + (jb + 1) * FF_BLOCK))
            gg = jnp.minimum(g_pre, SWIGLU_LIMIT)
            up = jnp.clip(up_pre, -SWIGLU_LIMIT, SWIGLU_LIMIT)
            act = gg * _sigmoid(SWIGLU_ALPHA * gg) * (up + 1.0)
            y = y + _dot(act.astype(BF16), w2b_scr[jb * FF_BLOCK:(jb + 1) * FF_BLOCK, :])
        row = lax.broadcasted_iota(jnp.int32, y.shape, 0)
        o_ref[row0:row0 + m, :] = _pack_bf16_pair(jnp.where(row < nv, y, 0.0))
        if m < GEMM_TILE:
            o_ref[row0 + m:row0 + GEMM_TILE, :] = jnp.zeros((GEMM_TILE - m, D_WORDS), jnp.int32)

    @pl.when(nv > GEMM_TILE // 2)
    def _():
        ffn(GEMM_TILE)

    @pl.when(jnp.logical_and(nv > 0, nv <= GEMM_TILE // 2))
    def _():
        ffn(GEMM_TILE // 2)

    @pl.when(nv == 0)
    def _():
        o_ref[row0:row0 + GEMM_TILE, :] = jnp.zeros((GEMM_TILE, D_WORDS), jnp.int32)


def _experts(xs, plan, w1, b1, w2, b2, layer):
    n_tiles = xs.shape[0] // GEMM_TILE
    depth = w1.shape[0]
    assert n_tiles % TILES_PER_STEP == 0
    step_rows = TILES_PER_STEP * GEMM_TILE
    bias_map = lambda s, plan: (layer, 0, 0, 0)
    grid_spec = pltpu.PrefetchScalarGridSpec(
        num_scalar_prefetch=1,
        grid=(n_tiles // TILES_PER_STEP,),
        in_specs=[
            pl.BlockSpec((step_rows, D_WORDS), lambda s, plan: (s, 0)),
            pl.BlockSpec(memory_space=pl.ANY),
            pl.BlockSpec((None, N_EXPERTS, 1, 2 * D_FF), bias_map, pipeline_mode=pl.Buffered(1)),
            pl.BlockSpec(memory_space=pl.ANY),
            pl.BlockSpec((None, N_EXPERTS, 1, D_MODEL), bias_map, pipeline_mode=pl.Buffered(1)),
        ],
        out_specs=pl.BlockSpec((step_rows, D_WORDS), lambda s, plan: (s, 0)),
        scratch_shapes=[
            pltpu.VMEM((2, D_MODEL, 2 * D_FF), F32),
            pltpu.VMEM((2, D_FF, D_MODEL), F32),
            pltpu.VMEM((D_MODEL, 2 * D_FF), BF16),
            pltpu.VMEM((D_FF, D_MODEL), BF16),
            pltpu.SemaphoreType.DMA((2, 2)),
        ],
    )
    return pl.pallas_call(
        functools.partial(_expert_kernel, layer=layer),
        grid_spec=grid_spec,
        out_shape=jax.ShapeDtypeStruct((n_tiles * GEMM_TILE, D_WORDS), jnp.int32),
        compiler_params=pltpu.CompilerParams(
            dimension_semantics=("arbitrary",), vmem_limit_bytes=VMEM_LIMIT),
        name="experts",
    )(plan, xs, w1, b1.reshape(depth, N_EXPERTS, 1, 2 * D_FF), w2,
      b2.reshape(depth, N_EXPERTS, 1, D_MODEL))


def _final_kernel(x_ref, yg_ref, wgt_ref, gp_ref, gs_ref, fg_ref, op_ref, os_ref, *, n_prompt_blocks):
    i = pl.program_id(0)
    n_seq = gs_ref.shape[0]

    def run(gate, o_ref):
        def body(r0):
            r = pl.ds(r0, COMBINE_CHUNK)
            w = wgt_ref[r, :]
            g = gate(r0)
            xl = x_ref[r, 0:D_WORDS] + g[:, 0:D_WORDS] * _moe_half(yg_ref, w, r, False)
            xr = x_ref[r, D_WORDS:D_MODEL] + g[:, D_WORDS:D_MODEL] * _moe_half(yg_ref, w, r, True)
            ms = (jnp.sum(xl * xl, axis=-1, keepdims=True)
                  + jnp.sum(xr * xr, axis=-1, keepdims=True)) * (1.0 / D_MODEL)
            scale = lax.rsqrt(ms + EPS)
            o_ref[r, 0:D_WORDS] = xl * scale * fg_ref[:, 0:D_WORDS]
            o_ref[r, D_WORDS:D_MODEL] = xr * scale * fg_ref[:, D_WORDS:D_MODEL]
        _row_loop(x_ref.shape[0], body, COMBINE_CHUNK)

    @pl.when(i < n_prompt_blocks)
    def _():
        run(lambda r0: gp_ref[0], op_ref)

    @pl.when(i >= n_prompt_blocks)
    def _():
        run(lambda r0: gs_ref[pl.ds(pl.multiple_of(lax.rem(r0, n_seq), COMBINE_CHUNK), COMBINE_CHUNK), :],
            os_ref)


def _final(xmid, yg, wgt, gate_p, gate_s, final_g, n_prompt, seq_len):
    t_all = xmid.shape[0]
    n_prompt_blocks = n_prompt // ROW_TILE
    per_seq = seq_len // ROW_TILE
    n_seq_p = gate_p.shape[0]
    assert t_all - n_prompt == ROW_TILE
    out_specs = (
        pl.BlockSpec((ROW_TILE, D_MODEL), lambda i: (jnp.minimum(i, n_prompt_blocks - 1), 0)),
        pl.BlockSpec((ROW_TILE, D_MODEL), lambda i: (0, 0)),
    )
    out_shape = (jax.ShapeDtypeStruct((n_prompt, D_MODEL), F32),
                 jax.ShapeDtypeStruct((ROW_TILE, D_MODEL), F32))
    return pl.pallas_call(
        functools.partial(_final_kernel, n_prompt_blocks=n_prompt_blocks),
        grid=(t_all // ROW_TILE,),
        in_specs=[
            pl.BlockSpec((ROW_TILE, D_MODEL), lambda i: (i, 0)),
            pl.BlockSpec((TOP_K, ROW_TILE, D_WORDS), lambda i: (0, i, 0)),
            pl.BlockSpec((ROW_TILE, LANES), lambda i: (i, 0)),
            pl.BlockSpec((1, 1, D_MODEL), lambda i: (jnp.minimum(i // per_seq, n_seq_p - 1), 0, 0)),
            _const_spec(gate_s.shape),
            _const_spec((1, D_MODEL)),
        ],
        out_specs=out_specs,
        out_shape=out_shape,
        compiler_params=pltpu.CompilerParams(
            dimension_semantics=("arbitrary",), vmem_limit_bytes=VMEM_LIMIT),
        name="final_combine",
    )(xmid, yg, wgt, gate_p.reshape(n_seq_p, 1, D_MODEL), gate_s, final_g.reshape(1, D_MODEL))


def _sc_mesh():
    return plsc.VectorSubcoreMesh(core_axis_name="core", subcore_axis_name="subcore",
                                  num_cores=SC_CORES, num_subcores=SC_SUBCORES)


def _sc_worker_id():
    return lax.axis_index("subcore") * SC_CORES + lax.axis_index("core")


def _sc_dispatch(h2w, pos_c, n_slots):
    n_chunks = pos_c.shape[1]

    @functools.partial(
        pl.kernel, mesh=_sc_mesh(),
        out_type=jax.ShapeDtypeStruct((n_slots, D_WORDS), jnp.int32),
        scratch_types=[pltpu.VMEM((TOP_K, SC_CHUNK), jnp.int32),
                       pltpu.VMEM((SC_CHUNK, D_WORDS), jnp.int32),
                       pltpu.SemaphoreType.DMA],
        name="moe_dispatch")
    def run(h_hbm, pos_hbm, out_hbm, idx_v, rows_v, sem):
        wid = _sc_worker_id()

        @pl.loop(0, n_chunks)
        def _(c):
            base = pl.multiple_of((wid * n_chunks + c) * SC_CHUNK, SUBLANES)
            pltpu.sync_copy(pos_hbm.at[wid, c], idx_v)
            pltpu.sync_copy(h_hbm.at[pl.ds(base, SC_CHUNK)], rows_v)
            copies = [pltpu.async_copy(rows_v, out_hbm.at[idx_v.at[k]], sem) for k in range(TOP_K)]
            for cp in copies:
                cp.wait()

    return run(h2w, pos_c)


def _sc_gather_back(ys, pos_c, t_all):
    n_chunks = pos_c.shape[1]

    @functools.partial(
        pl.kernel, mesh=_sc_mesh(),
        out_type=jax.ShapeDtypeStruct((TOP_K, t_all, D_WORDS), jnp.int32),
        scratch_types=[pltpu.VMEM((TOP_K, SC_CHUNK), jnp.int32),
                       pltpu.VMEM((TOP_K, SC_CHUNK, D_WORDS), jnp.int32),
                       pltpu.SemaphoreType.DMA],
        name="moe_gather_back")
    def run(ys_hbm, pos_hbm, out_hbm, idx_v, rows_v, sem):
        wid = _sc_worker_id()

        @pl.loop(0, n_chunks)
        def _(c):
            base = pl.multiple_of((wid * n_chunks + c) * SC_CHUNK, SUBLANES)
            pltpu.sync_copy(pos_hbm.at[wid, c], idx_v)
            copies = [pltpu.async_copy(ys_hbm.at[idx_v.at[k]], rows_v.at[k], sem) for k in range(TOP_K)]
            for cp in copies:
                cp.wait()
            for k in range(TOP_K):
                pltpu.sync_copy(rows_v.at[k], out_hbm.at[k, pl.ds(base, SC_CHUNK)])

    return run(ys, pos_c)


def kernel(x_prompt, x_sample, state_conv, c_prompt, c_sample, w_ada, b_ada, w_in, w_conv, b_conv,
           gn_g, gn_b, sgu_ln_g, sgu_ln_b, w_s, b_s, beta_a, beta_b, w_out, w_router, b_router,
           w1, b1, w2, b2, final_g):
    n_bp, seq_len, _ = x_prompt.shape
    n_bs, n_pos, _ = x_sample.shape
    depth = w_ada.shape[0]
    n_prompt = n_bp * seq_len
    n_sample = n_bs * n_pos
    t_all = n_prompt + n_sample
    n_tiles = (t_all * TOP_K) // GEMM_TILE + N_EXPERTS
    n_workers = SC_CORES * SC_SUBCORES
    assert t_all % (n_workers * SC_CHUNK) == 0 and n_tiles <= PLAN_LANES and t_all % ROUTER_TILE == 0

    x_p = x_prompt.reshape(n_prompt, D_MODEL)
    x_s = x_sample.transpose(1, 0, 2).reshape(n_sample, D_MODEL)
    x_s_block = 0
    moe = None
    mods =_adaln(jnp.concatenate([c_prompt, c_sample], axis=0), w_ada, b_ada)
    state_t = state_conv.transpose(0, 2, 1, 3)
    grp = jnp.arange(C_CONV) // (C_CONV // CONV_GROUPS)
    g_mat = ((grp[:, None] == grp[None, :]).astype(F32) / (C_CONV // CONV_GROUPS)).astype(BF16)

    conv_p, conv_s, v_s = [], [], []
    for l in range(depth):
        mod_p = mods[l, :n_bp].reshape(n_bp, 1, N_MOD * D_MODEL)
        mod_s = mods[l, n_bp:]
        wconv_p = jnp.broadcast_to(
            jnp.pad(w_conv[l], ((0, CARRY_ROWS - CONV_WIDTH), (0, 0)))[:, None, :],
            (CARRY_ROWS, SUBLANES, C_CONV))
        zero = jnp.zeros_like(b_conv[l])
        vec = jnp.stack([b_conv[l], gn_g[l], gn_b[l], sgu_ln_g[l], sgu_ln_b[l],
                         beta_a[l], beta_b[l], zero], axis=0)
        bs_full = jnp.repeat(b_s[l].T, SGU_HEAD_DIM, axis=1)
        wsv = jnp.repeat(w_s[l][:, :n_pos, :n_pos].transpose(1, 2, 0).reshape(n_pos * n_pos, SGU_HEADS),
                         SGU_HEAD_DIM, axis=1)
        bsv = jnp.repeat(b_s[l][:, :n_pos].T, SGU_HEAD_DIM, axis=1)
        xmid, h2, logits, cst, a_new, v_new = _mixer(
            x_p, x_s, x_s_block, moe, mod_p, mod_s, w_in[l].astype(BF16), w_out[l].astype(BF16), wconv_p,
            vec, g_mat,
            w_s[l], bs_full, wsv, bsv, w_router[l].T, b_router[l].reshape(N_EXPERTS, 1),
            state_t[l], n_bp, seq_len)
        conv_p.append(cst)
        conv_s.append(a_new)
        v_s.append(v_new)

        wgt, pos, plan = _router(logits)
        pos_c = pos.reshape(TOP_K, n_workers, -1, SC_CHUNK).transpose(1, 2, 0, 3)
        xs = _sc_dispatch(h2, pos_c, n_tiles * GEMM_TILE)
        ys = _experts(xs, plan, w1, b1, w2, b2, l)
        yg = _sc_gather_back(ys, pos_c, t_all)
        moe = (yg, wgt, mod_p[:, 0, 5 * D_MODEL:], mod_s[:, 5 * D_MODEL:])
        x_p = x_s = xmid
        x_s_block = n_prompt // n_sample

    y_p, y_s = _final(xmid, *moe, final_g, n_prompt, seq_len)
    y_prompt = y_p.reshape(n_bp, seq_len, D_MODEL)
    y_sample = y_s.reshape(n_pos, n_bs, D_MODEL).transpose(1, 0, 2)
    new_conv_s = jnp.stack(conv_s, axis=0).transpose(0, 2, 1, 3)
    new_v_s = jnp.stack(v_s, axis=0).reshape(depth, n_pos, n_bs, C_SGU).transpose(0, 2, 1, 3)
    return (y_prompt, y_sample, jnp.stack(conv_p, axis=0), new_conv_s, new_v_s)
```

```python
import functools
import math

import jax
import jax.numpy as jnp
from jax import lax
from jax.experimental import pallas as pl
from jax.experimental.pallas import tpu as pltpu
from jax.experimental.pallas import tpu_sc as plsc

F32 = jnp.float32
BF16 = jnp.bfloat16

D_MODEL = 1024
C_CONV = 512
C_SGU = 512
CONV_GROUPS = 8
CONV_WIDTH = 31
CONV_STATE = CONV_WIDTH - 1
SGU_HEADS = 4
SGU_HEAD_DIM = C_SGU // SGU_HEADS
CHUNK = 128
N_EXPERTS = 32
TOP_K = 4
D_FF = 1024
SWIGLU_LIMIT = 7.0
SWIGLU_ALPHA = 1.702
N_MOD = 6
EPS = 1e-5
LOG2_E = 1.0 / math.log(2.0)

ROW_TILE = 512
ROUTER_TILE = 1536
ROW_CHUNK = 128
CONV_CHUNK = 32
COMBINE_CHUNK = 64
CARRY_ROWS = 32
SUBLANES = 8
LANES = 128
PLAN_ROWS = 8
PLAN_LANES = 256
PLAN_EXPERT, PLAN_VALID, PLAN_FIRST, PLAN_SLOT, PLAN_NEXT = range(5)
GEMM_TILE = 512
FF_BLOCK = 256
TILES_PER_STEP = 4
VMEM_LIMIT = 56 * 1024 * 1024
D_WORDS = D_MODEL // 2
SC_CHUNK = 48
HI_MASK = -65536
SC_CORES = 2
SC_SUBCORES = 16


def _rms(x):
    return x * lax.rsqrt(jnp.mean(x * x, axis=-1, keepdims=True) + EPS)


def _gelu(x):
    return 0.5 * x * (1.0 + lax.erf(x * (1.0 / math.sqrt(2.0))))


def _sigmoid(x, scale=1.0):
    return 1.0 / (1.0 + jnp.exp2(x * (-scale * LOG2_E)))


def _split_bf16(x):
    hi = x.astype(BF16)
    lo = (x - hi.astype(F32)).astype(BF16)
    return hi, lo


def _pack_bf16_pair(x):
    bits = lax.bitcast_convert_type(x.astype(BF16).astype(F32), jnp.int32)
    return lax.shift_right_logical(bits[:, :D_WORDS], 16) | (bits[:, D_WORDS:] & HI_MASK)


def _unpack_bf16_pair(w):
    left = lax.bitcast_convert_type(lax.shift_left(w, 16), F32)
    right = lax.bitcast_convert_type(w & HI_MASK, F32)
    return left, right


def _dot(a, b):
    return jnp.dot(a, b, preferred_element_type=F32)


def _dot_nt(a, b):
    return lax.dot_general(a, b, (((1,), (1,)), ((), ())), preferred_element_type=F32)


def _row_loop(n_rows, body, chunk=None):
    chunk = ROW_CHUNK if chunk is None else chunk

    def step(i, carry):
        body(pl.multiple_of(i * chunk, chunk))
        return carry
    lax.fori_loop(0, n_rows // chunk, step, 0)


def _conv_tap(wconv_ref, k):
    return jnp.concatenate([wconv_ref[k]] * (CONV_CHUNK // SUBLANES), axis=0)


def _static_loop(n_rows, body, chunk=None):
    chunk = ROW_CHUNK if chunk is None else chunk
    for r0 in range(0, n_rows, chunk):
        body(r0)


def _adaln_kernel(c_ref, w_ref, b_ref, o_ref):
    c = c_ref[...]
    s_hi, s_lo = _split_bf16(c * _sigmoid(c))
    w_hi, w_lo = _split_bf16(w_ref[0])
    acc = _dot(s_hi, w_hi) + _dot(s_hi, w_lo) + _dot(s_lo, w_hi)
    o_ref[0] = acc + b_ref[0]


def _adaln(c_all, w_ada, b_ada):
    depth, _, n_out = w_ada.shape
    n_rows = c_all.shape[0]
    tn = 1024
    return pl.pallas_call(
        _adaln_kernel,
        grid=(depth, n_out // tn),
        in_specs=[
            pl.BlockSpec((n_rows, D_MODEL), lambda l, j: (0, 0)),
            pl.BlockSpec((1, D_MODEL, tn), lambda l, j: (l, 0, j)),
            pl.BlockSpec((1, 1, tn), lambda l, j: (l, 0, j)),
        ],
        out_specs=pl.BlockSpec((1, n_rows, tn), lambda l, j: (l, 0, j)),
        out_shape=jax.ShapeDtypeStruct((depth, n_rows, n_out), F32),
        compiler_params=pltpu.CompilerParams(
            dimension_semantics=("arbitrary", "arbitrary"), vmem_limit_bytes=VMEM_LIMIT),
        name="adaln",
    )(c_all, w_ada, b_ada.reshape(depth, 1, n_out))


def _phase_in_norm(x_ref, h_scr, mod, n_rows, loop=_row_loop):
    def body(r0):
        r = pl.ds(r0, ROW_CHUNK)
        h = _rms(x_ref[r, :]) * (1.0 + mod(1, r0)) + mod(0, r0)
        h_scr[r, :] = h.astype(BF16)
    loop(n_rows, body)


def _phase_glu(z_scr, a_dst, a_off, n_rows, loop=_row_loop):
    def body(r0):
        r = pl.ds(r0, ROW_CHUNK)
        a_dst[pl.ds(r0 + a_off, ROW_CHUNK), :] = z_scr[r, 0:C_CONV] * _sigmoid(z_scr[r, C_CONV:2 * C_CONV])
    loop(n_rows, body)


def _phase_uv(z_scr, vec_ref, u_scr, v_scr, vout_ref, n_rows, loop=_row_loop):
    ln_g = vec_ref[3:4, :]
    ln_b = vec_ref[4:5, :]

    def body(r0):
        r = pl.ds(r0, ROW_CHUNK)
        u_scr[r, :] = _gelu(z_scr[r, 2 * C_CONV:2 * C_CONV + C_SGU])
        gv = _gelu(z_scr[r, 2 * C_CONV + C_SGU:2 * C_CONV + 2 * C_SGU])
        mu = jnp.mean(gv, axis=-1, keepdims=True)
        dv = gv - mu
        var = jnp.mean(dv * dv, axis=-1, keepdims=True)
        v = dv * lax.rsqrt(var + EPS) * ln_g + ln_b
        if vout_ref is not None:
            vout_ref[r, :] = v
        v_scr[r, :] = v.astype(v_scr.dtype)
    loop(n_rows, body)


def _phase_group_norm(conv_scr, convb_scr, stat_scr, g_ref, vec_ref, y_scr, n_rows, loop=_row_loop):
    stat_scr[...] = _dot(convb_scr[...], g_ref[...])

    def center(r0):
        r = pl.ds(r0, ROW_CHUNK)
        d = conv_scr[r, :] - stat_scr[r, :]
        conv_scr[r, :] = d
        convb_scr[r, :] = (d * d).astype(BF16)
    loop(n_rows, center)
    stat_scr[...] = _dot(convb_scr[...], g_ref[...])
    gn_g = vec_ref[1:2, :]
    gn_b = vec_ref[2:3, :]
    beta_a = vec_ref[5:6, :]

    def finish(r0):
        r = pl.ds(r0, ROW_CHUNK)
        gn = conv_scr[r, :] * lax.rsqrt(stat_scr[r, :] + EPS) * gn_g + gn_b
        ya = gn * _sigmoid(gn)
        y_scr[r, 0:C_CONV] = (_rms(ya) * beta_a).astype(BF16)
    loop(n_rows, finish)


def _phase_out(x_ref, y_scr, wout_ref, z_scr, wr_ref, br_ref, mod, h_scr,
               xmid_ref, h2_ref, logit_ref, n_rows, loop=_row_loop):
    z_scr[:, 0:D_MODEL] = (_dot(y_scr[:, 0:C_CONV], wout_ref[0:C_CONV, :])
                           + _dot(y_scr[:, C_CONV:C_CONV + C_SGU], wout_ref[C_CONV:C_CONV + C_SGU, :]))

    def body(r0):
        r = pl.ds(r0, ROW_CHUNK)
        xm = x_ref[r, :] + mod(2, r0) * z_scr[r, 0:D_MODEL]
        xmid_ref[r, :] = xm
        h2 = _rms(xm) * (1.0 + mod(4, r0)) + mod(3, r0)
        hi, lo = _split_bf16(h2)
        h2_ref[r, :] = _pack_bf16_pair(h2)
        y_scr[r, :] = hi
        h_scr[r, :] = lo
    loop(n_rows, body)
    w_hi, w_lo = _split_bf16(wr_ref[...])
    both = _dot_nt(jnp.concatenate([w_hi, w_lo], axis=0), y_scr[...])
    logit_ref[...] = (both[0:N_EXPERTS] + both[N_EXPERTS:2 * N_EXPERTS]
                      + _dot_nt(w_hi, h_scr[...])) + br_ref[...]


def _moe_half(yg_ref, w, r, hi_half):
    moe = jnp.zeros((COMBINE_CHUNK, D_WORDS), F32)
    for k in range(TOP_K):
        word = yg_ref[k, r, :]
        bits = (word & HI_MASK) if hi_half else lax.shift_left(word, 16)
        moe = moe + w[:, k:k + 1] * lax.bitcast_convert_type(bits, F32)
    return moe


def _phase_moe_residual(xmid_ref, yg_ref, wgt_ref, gate, x_dst, n_rows, loop=_row_loop):
    def body(r0):
        r = pl.ds(r0, COMBINE_CHUNK)
        w = wgt_ref[r, :]
        g = gate(r0)
        for hi_half in (False, True):
            cols = slice(D_WORDS, D_MODEL) if hi_half else slice(0, D_WORDS)
            x_dst[r, cols] = xmid_ref[r, cols] + g[:, cols] * _moe_half(yg_ref, w, r, hi_half)
    loop(n_rows, body, COMBINE_CHUNK)


def _mixer_prompt_kernel(*refs, tiles_per_seq, n_prompt_tiles, fused):
    x_ref, mod_ref = refs[:2]
    refs = refs[2:]
    if fused:
        yg_ref, wgt_ref, gate_ref = refs[:3]
        refs = refs[3:]
        x_scr = refs[-1]
        refs = refs[:-1]
    (win_ref, wout_ref, wconv_ref, vec_ref, g_ref, ws_ref, bs_ref, wr_ref, br_ref,
     smid_ref, sh2_ref, slogit_ref, xmid_ref, h2_ref, logit_ref, cst_ref,
     h_scr, z_scr, aext_scr, conv_scr, convb_scr, stat_scr, u_scr, v_scr, y_scr) = refs
    i = pl.program_id(0)

    @pl.when(i == n_prompt_tiles)
    def _():
        xmid_ref[...] = smid_ref[...]
        h2_ref[...] = sh2_ref[...]
        logit_ref[...] = slogit_ref[...]

    @pl.when(i < n_prompt_tiles)
    def _():
        x_src = x_ref
        if fused:
            _phase_moe_residual(x_ref, yg_ref, wgt_ref, lambda r0: gate_ref[0], x_scr, ROW_TILE,
                                _static_loop)
            x_src = x_scr
        _mixer_prompt_tile(lax.rem(i, tiles_per_seq), tiles_per_seq,
                           x_src, mod_ref, win_ref, wout_ref, wconv_ref, vec_ref, g_ref,
                           ws_ref, bs_ref, wr_ref, br_ref, xmid_ref, h2_ref, logit_ref, cst_ref,
                           h_scr, z_scr, aext_scr, conv_scr, convb_scr, stat_scr, u_scr, v_scr, y_scr)


def _mixer_prompt_tile(j, tiles_per_seq, x_ref, mod_ref, win_ref, wout_ref, wconv_ref, vec_ref, g_ref,
                       ws_ref, bs_ref, wr_ref, br_ref, xmid_ref, h2_ref, logit_ref, cst_ref,
                       h_scr, z_scr, aext_scr, conv_scr, convb_scr, stat_scr, u_scr, v_scr, y_scr):
    n_rows = ROW_TILE

    def mod(idx, r0):
        del r0
        return mod_ref[0, :, idx * D_MODEL:(idx + 1) * D_MODEL]

    a_buf = aext_scr.at[0]

    @pl.when(j == 0)
    def _():
        a_buf[0:CARRY_ROWS, :] = jnp.zeros((CARRY_ROWS, C_CONV), F32)

    loop = _static_loop
    _phase_in_norm(x_ref, h_scr, mod, n_rows, loop)
    n_glu = 2 * C_CONV
    z_scr[:, 0:n_glu] = _dot(h_scr[...], win_ref[:, 0:n_glu])
    _phase_glu(z_scr, a_buf, CARRY_ROWS, n_rows, loop)
    z_scr[:, n_glu:] = _dot(h_scr[...], win_ref[:, n_glu:])
    n_shift = n_rows + CARRY_ROWS - SUBLANES
    for s in range(1, SUBLANES):
        aext_scr[s, 0:n_shift, :] = a_buf[s:s + n_shift, :]

    b_conv = vec_ref[0:1, :]
    lead = CARRY_ROWS - CONV_STATE
    for r0 in range(0, n_rows, CONV_CHUNK):
        acc = jnp.zeros((CONV_CHUNK, C_CONV), F32)
        for k in range(CONV_WIDTH):
            s = (lead + k) % SUBLANES
            q0 = r0 + (lead + k - s)
            acc = acc + _conv_tap(wconv_ref, k) * aext_scr[s, q0:q0 + CONV_CHUNK, :]
        conv = acc + b_conv
        conv_scr[r0:r0 + CONV_CHUNK, :] = conv
        convb_scr[r0:r0 + CONV_CHUNK, :] = conv.astype(BF16)
    _phase_uv(z_scr, vec_ref, u_scr, v_scr, None, n_rows, loop)
    _phase_group_norm(conv_scr, convb_scr, stat_scr, g_ref, vec_ref, y_scr, n_rows, loop)

    row_i = lax.broadcasted_iota(jnp.int32, (CHUNK, CHUNK), 0)
    col_i = lax.broadcasted_iota(jnp.int32, (CHUNK, CHUNK), 1)
    tril = (col_i <= row_i).astype(F32)
    for hd in range(SGU_HEADS):
        ws_h = (ws_ref[hd] * tril).astype(BF16)
        cols = slice(hd * SGU_HEAD_DIM, (hd + 1) * SGU_HEAD_DIM)
        s_cols = slice(n_glu + hd * SGU_HEAD_DIM, n_glu + (hd + 1) * SGU_HEAD_DIM)
        for c in range(n_rows // CHUNK):
            rows = slice(c * CHUNK, (c + 1) * CHUNK)
            z_scr[rows, s_cols] = _dot(ws_h, v_scr[rows, cols])
    beta_b = vec_ref[6:7, :]

    def gate_body(r0):
        r = pl.ds(r0, ROW_CHUNK)
        rb = pl.ds(r0 % CHUNK, ROW_CHUNK)
        yb = u_scr[r, :] * (z_scr[r, n_glu:n_glu + C_SGU] + bs_ref[rb, :])
        y_scr[r, C_CONV:C_CONV + C_SGU] = (_rms(yb) * beta_b).astype(BF16)
    loop(n_rows, gate_body)

    _phase_out(x_ref, y_scr, wout_ref, z_scr, wr_ref, br_ref, mod, h_scr,
               xmid_ref, h2_ref, logit_ref, n_rows, loop)

    @pl.when(j == tiles_per_seq - 1)
    def _():
        cst_ref[0] = a_buf[n_rows + lead:n_rows + CARRY_ROWS, :]

    a_buf[0:CARRY_ROWS, :] = a_buf[n_rows:n_rows + CARRY_ROWS, :]


def _mixer_sample_kernel(*refs, fused):
    x_ref, mod_ref = refs[:2]
    refs = refs[2:]
    if fused:
        yg_ref, wgt_ref, gate_ref = refs[:3]
        refs = refs[3:]
        x_scr = refs[-1]
        refs = refs[:-1]
    (win_ref, wout_ref, wconv_ref, vec_ref, g_ref, wsv_ref, bsv_ref, wr_ref, br_ref, state_ref,
     xmid_ref, h2_ref, logit_ref, cnew_ref, vout_ref,
     h_scr, z_scr, conv_scr, convb_scr, stat_scr, u_scr, v_scr, y_scr, anew_ref) = refs
    n_rows = x_ref.shape[0]
    n_seq = state_ref.shape[1]
    n_pos = n_rows // n_seq
    loop = _static_loop

    def mod(idx, r0):
        return mod_ref[pl.ds(r0 % n_seq, ROW_CHUNK), idx * D_MODEL:(idx + 1) * D_MODEL]

    if fused:
        def gate(r0):
            return gate_ref[pl.ds(r0 % n_seq, COMBINE_CHUNK), :]
        _phase_moe_residual(x_ref, yg_ref, wgt_ref, gate, x_scr, n_rows, loop)
        x_ref = x_scr

    _phase_in_norm(x_ref, h_scr, mod, n_rows, loop)
    z_scr[...] = _dot(h_scr[...], win_ref[...])
    _phase_glu(z_scr, anew_ref, 0, n_rows, loop)
    _phase_uv(z_scr, vec_ref, u_scr, v_scr, vout_ref, n_rows, loop)

    keep = CONV_STATE - n_pos
    cnew_ref[0:keep] = state_ref[n_pos:CONV_STATE]
    for t in range(n_pos):
        cnew_ref[keep + t] = anew_ref[t * n_seq:(t + 1) * n_seq, :]

    b_conv = vec_ref[0:1, :]

    def conv_body(s0):
        rs = pl.ds(s0, CONV_CHUNK)
        for t in range(n_pos):
            acc = jnp.zeros((CONV_CHUNK, C_CONV), F32)
            for k in range(CONV_WIDTH):
                p = t + k
                if p < CONV_STATE:
                    src = state_ref[p, rs, :]
                else:
                    src = anew_ref[pl.ds(s0 + (p - CONV_STATE) * n_seq, CONV_CHUNK), :]
                acc = acc + _conv_tap(wconv_ref, k) * src
            conv = acc + b_conv
            r = pl.ds(s0 + t * n_seq, CONV_CHUNK)
            conv_scr[r, :] = conv
            convb_scr[r, :] = conv.astype(BF16)
    loop(n_seq, conv_body, CONV_CHUNK)

    _phase_group_norm(conv_scr, convb_scr, stat_scr, g_ref, vec_ref, y_scr, n_rows, loop)

    beta_b = vec_ref[6:7, :]

    def gate_body(s0):
        for t in range(n_pos):
            s = jnp.zeros((CONV_CHUNK, C_SGU), F32)
            for jj in range(t + 1):
                s = s + wsv_ref[t * n_pos + jj:t * n_pos + jj + 1, :] * v_scr[pl.ds(s0 + jj * n_seq, CONV_CHUNK), :]
            r = pl.ds(s0 + t * n_seq, CONV_CHUNK)
            yb = u_scr[r, :] * (s + bsv_ref[t:t + 1, :])
            y_scr[r, C_CONV:C_CONV + C_SGU] = (_rms(yb) * beta_b).astype(BF16)
    loop(n_seq, gate_body, CONV_CHUNK)

    _phase_out(x_ref, y_scr, wout_ref, z_scr, wr_ref, br_ref, mod, h_scr,
               xmid_ref, h2_ref, logit_ref, n_rows, loop)


def _const_spec(shape):
    nd = len(shape)
    return pl.BlockSpec(shape, lambda *_: (0,) * nd, pipeline_mode=pl.Buffered(1))


def _const_out_spec(shape):
    nd = len(shape)
    return pl.BlockSpec(shape, lambda *_: (0,) * nd)


def _mixer(x_p, x_s, x_s_block, moe, mod_p, mod_s, w_in_b, w_out_b, wconv_p, vec, g_mat, w_s, bs_full,
           wsv, bsv, w_router, b_router, state_t, n_prompt_seq, seq_len):
    fused = moe is not None
    n_prompt = n_prompt_seq * seq_len
    n_seq, n_pos = state_t.shape[1], bsv.shape[0]
    n_sample = n_seq * n_pos
    t_all = n_prompt + n_sample
    nj = seq_len // ROW_TILE
    weight_specs = [
        _const_spec((D_MODEL, 2 * C_CONV + 2 * C_SGU)),
        _const_spec((D_MODEL, D_MODEL)),
        _const_spec((CARRY_ROWS, SUBLANES, C_CONV)),
        _const_spec((8, C_CONV)),
        _const_spec((C_CONV, C_CONV)),
    ]
    router_specs = [_const_spec((N_EXPERTS, D_MODEL)), _const_spec((N_EXPERTS, 1))]
    common_scratch = lambda n: [
        pltpu.VMEM((n, D_MODEL), BF16),
        pltpu.VMEM((n, 2 * C_CONV + 2 * C_SGU), F32),
    ]
    tail_scratch = lambda n, vdt: [
        pltpu.VMEM((n, C_CONV), F32),
        pltpu.VMEM((n, C_CONV), BF16),
        pltpu.VMEM((n, C_CONV), F32),
        pltpu.VMEM((n, C_SGU), F32),
        pltpu.VMEM((n, C_SGU), vdt),
        pltpu.VMEM((n, D_MODEL), BF16),
    ]
    assert n_sample == ROW_TILE
    n_tiles_p = n_prompt // ROW_TILE
    tile_p = lambda i: jnp.minimum(i, n_tiles_p - 1)
    seq_of = lambda i: jnp.minimum(i // nj, n_prompt_seq - 1)
    moe_s_specs, moe_p_specs, moe_s_args, moe_p_args, x_scratch = [], [], [], [], []
    if fused:
        yg, wgt, gate_p, gate_s = moe
        moe_s_specs = [pl.BlockSpec((TOP_K, n_sample, D_WORDS), lambda i: (0, x_s_block, 0)),
                       pl.BlockSpec((n_sample, LANES), lambda i: (x_s_block, 0)),
                       _const_spec((n_seq, D_MODEL))]
        moe_p_specs = [pl.BlockSpec((TOP_K, ROW_TILE, D_WORDS), lambda i: (0, tile_p(i), 0)),
                       pl.BlockSpec((ROW_TILE, LANES), lambda i: (tile_p(i), 0)),
                       pl.BlockSpec((1, 1, D_MODEL), lambda i: (seq_of(tile_p(i)), 0, 0))]
        moe_s_args = [yg, wgt, gate_s]
        moe_p_args = [yg, wgt, gate_p.reshape(n_prompt_seq, 1, D_MODEL)]
        x_scratch = [pltpu.VMEM((ROW_TILE, D_MODEL), F32)]
    smid, sh2, slogits, a_new, v_new = pl.pallas_call(
        functools.partial(_mixer_sample_kernel, fused=fused),
        grid=(1,),
        in_specs=[
            pl.BlockSpec((n_sample, D_MODEL), lambda i: (x_s_block, 0)),
            _const_spec((n_seq, N_MOD * D_MODEL)),
            *moe_s_specs,
            *weight_specs,
            _const_spec((n_pos * n_pos, C_SGU)),
            _const_spec((n_pos, C_SGU)),
            *router_specs,
            _const_spec((CONV_STATE, n_seq, C_CONV)),
        ],
        out_specs=(
            _const_out_spec((n_sample, D_MODEL)),
            _const_out_spec((n_sample, D_WORDS)),
            _const_out_spec((N_EXPERTS, n_sample)),
            _const_out_spec((CONV_STATE, n_seq, C_CONV)),
            _const_out_spec((n_sample, C_SGU)),
        ),
        out_shape=(
            jax.ShapeDtypeStruct((n_sample, D_MODEL), F32),
            jax.ShapeDtypeStruct((n_sample, D_WORDS), jnp.int32),
            jax.ShapeDtypeStruct((N_EXPERTS, n_sample), F32),
            jax.ShapeDtypeStruct((CONV_STATE, n_seq, C_CONV), F32),
            jax.ShapeDtypeStruct((n_sample, C_SGU), F32),
        ),
        scratch_shapes=common_scratch(n_sample) + tail_scratch(n_sample, F32)
        + [pltpu.VMEM((n_sample, C_CONV), F32)] + x_scratch,
        compiler_params=pltpu.CompilerParams(
            dimension_semantics=("arbitrary",), vmem_limit_bytes=VMEM_LIMIT),
        name="mixer_sample",
    )(x_s, mod_s, *moe_s_args, w_in_b, w_out_b, wconv_p, vec, g_mat, wsv, bsv, w_router, b_router,
      state_t)

    xmid, h2, logits, cst = pl.pallas_call(
        functools.partial(_mixer_prompt_kernel, tiles_per_seq=nj, n_prompt_tiles=n_tiles_p,
                          fused=fused),
        grid=(n_tiles_p + 1,),
        in_specs=[
            pl.BlockSpec((ROW_TILE, D_MODEL), lambda i: (tile_p(i), 0)),
            pl.BlockSpec((1, 1, N_MOD * D_MODEL), lambda i: (seq_of(i), 0, 0)),
            *moe_p_specs,
            *weight_specs,
            _const_spec((SGU_HEADS, CHUNK, CHUNK)),
            _const_spec((CHUNK, C_SGU)),
            *router_specs,
            _const_spec((n_sample, D_MODEL)),
            _const_spec((n_sample, D_WORDS)),
            _const_spec((N_EXPERTS, n_sample)),
        ],
        out_specs=(
            pl.BlockSpec((ROW_TILE, D_MODEL), lambda i: (i, 0)),
            pl.BlockSpec((ROW_TILE, D_WORDS), lambda i: (i, 0)),
            pl.BlockSpec((N_EXPERTS, ROW_TILE), lambda i: (0, i)),
            pl.BlockSpec((1, CONV_STATE, C_CONV), lambda i: (seq_of(i), 0, 0)),
        ),
        out_shape=(
            jax.ShapeDtypeStruct((t_all, D_MODEL), F32),
            jax.ShapeDtypeStruct((t_all, D_WORDS), jnp.int32),
            jax.ShapeDtypeStruct((N_EXPERTS, t_all), F32),
            jax.ShapeDtypeStruct((n_prompt_seq, CONV_STATE, C_CONV), F32),
        ),
        scratch_shapes=common_scratch(ROW_TILE)
        + [pltpu.VMEM((SUBLANES, CARRY_ROWS + ROW_TILE, C_CONV), F32)]
        + tail_scratch(ROW_TILE, BF16) + x_scratch,
        compiler_params=pltpu.CompilerParams(
            dimension_semantics=("arbitrary",), vmem_limit_bytes=VMEM_LIMIT),
        name="mixer_prompt",
    )(x_p, mod_p, *moe_p_args, w_in_b, w_out_b, wconv_p, vec, g_mat, w_s, bs_full, w_router, b_router,
      smid, sh2, slogits)
    return xmid, h2, logits, cst, a_new, v_new


def _tile_plan(cnt):
    e_sub = lax.broadcasted_iota(jnp.int32, (N_EXPERTS, N_EXPERTS), 0)
    e_lane = lax.broadcasted_iota(jnp.int32, (N_EXPERTS, N_EXPERTS), 1)
    tiles = jnp.floor((cnt + (GEMM_TILE - 1.0)) * (1.0 / GEMM_TILE))
    active = jnp.where(cnt > 0.0, 1.0, 0.0)

    def over_experts(tri, col):
        return _dot(tri.astype(BF16), jnp.broadcast_to(col, (N_EXPERTS, LANES)).astype(BF16))[:, 0:1]
    tile_end = over_experts(jnp.where(e_lane <= e_sub, 1.0, 0.0), tiles)
    tile_start = tile_end - tiles
    ordinal = over_experts(jnp.where(e_lane < e_sub, 1.0, 0.0), active)

    g = lax.broadcasted_iota(jnp.int32, (1, PLAN_LANES), 1).astype(F32)
    e_id = lax.broadcasted_iota(jnp.int32, (N_EXPERTS, PLAN_LANES), 0).astype(F32)
    done = jnp.sum(jnp.where(tile_end <= g, 1.0, 0.0), axis=0, keepdims=True)
    live = done < float(N_EXPERTS)
    last_e = jnp.max(jnp.where(cnt > 0.0, e_id[:, 0:1], 0.0), axis=0, keepdims=True)
    te = jnp.where(live, done, last_e)
    pick = e_id == te

    def at_tile(col):
        return jnp.sum(jnp.where(pick, col, 0.0), axis=0, keepdims=True)
    start_g = at_tile(tile_start)
    ord_g = at_tile(ordinal)
    valid = jnp.clip(at_tile(cnt) - (g - start_g) * GEMM_TILE, 0.0, float(GEMM_TILE))
    valid = jnp.where(live, valid, 0.0)
    first = jnp.where(live, jnp.where(g == start_g, 1.0, 0.0), 0.0)
    slot = ord_g - 2.0 * jnp.floor(ord_g * 0.5)
    later = jnp.where(e_id > te, jnp.where(cnt > 0.0, e_id, float(N_EXPERTS)), float(N_EXPERTS))
    nxt = jnp.min(later, axis=0, keepdims=True)
    nxt = jnp.where(nxt >= float(N_EXPERTS), -1.0, nxt)

    row = lax.broadcasted_iota(jnp.int32, (PLAN_ROWS, PLAN_LANES), 0)
    plan = jnp.zeros((PLAN_ROWS, PLAN_LANES), F32)
    for r, v in ((PLAN_EXPERT, te), (PLAN_VALID, valid), (PLAN_FIRST, first), (PLAN_SLOT, slot),
                 (PLAN_NEXT, nxt)):
        plan = jnp.where(row == r, v, plan)
    return plan.astype(jnp.int32), tile_start * GEMM_TILE


def _router_kernel(lg_ref, wgt_ref, pos_ref, plan_ref, cnt_scr, run_scr, start_scr, earlier_scr):
    ph = pl.program_id(0)
    i = pl.program_id(1)
    n = lg_ref.shape[1]

    @pl.when(jnp.logical_and(ph == 0, i == 0))
    def _():
        cnt_scr[...] = jnp.zeros_like(cnt_scr)

    lg = lg_ref[...]
    sub = lax.broadcasted_iota(jnp.int32, lg.shape, 0).astype(F32)
    vals, sels = [], []
    for _ in range(TOP_K):
        m = jnp.max(lg, axis=0, keepdims=True)
        idx = jnp.min(jnp.where(lg == m, sub, float(N_EXPERTS)), axis=0, keepdims=True)
        sel = sub == idx
        vals.append(m)
        sels.append(sel)
        lg = jnp.where(sel, -jnp.inf, lg)
    onehot = jnp.zeros(lg.shape, F32)
    for sel in sels:
        onehot = onehot + jnp.where(sel, 1.0, 0.0)
    tile_cnt = jnp.sum(onehot, axis=1, keepdims=True)

    @pl.when(ph == 0)
    def _():
        cnt_scr[...] = cnt_scr[...] + tile_cnt

    @pl.when(jnp.logical_and(ph == 1, i == 0))
    def _():
        plan, row_start = _tile_plan(cnt_scr[...])
        plan_ref[...] = plan
        start_scr[...] = row_start
        run_scr[...] = jnp.zeros_like(run_scr)
        t_row = lax.broadcasted_iota(jnp.int32, (n, n), 0)
        t_col = lax.broadcasted_iota(jnp.int32, (n, n), 1)
        earlier_scr[...] = jnp.where(t_row < t_col, 1.0, 0.0).astype(BF16)

    @pl.when(ph == 1)
    def _():
        exps = [jnp.exp(v - vals[0]) for v in vals]
        inv = 1.0 / (exps[0] + exps[1] + exps[2] + exps[3])
        w_row = lax.broadcasted_iota(jnp.int32, (LANES, n), 0)
        w_all = jnp.zeros((LANES, n), F32)
        for k in range(TOP_K):
            w_all = jnp.where(w_row == k, exps[k] * inv, w_all)
        wgt_ref[...] = w_all.T
        slot0 = _dot(onehot.astype(BF16), earlier_scr[...]) + (run_scr[...] + start_scr[...])
        for k in range(TOP_K):
            pos_k = jnp.sum(jnp.where(sels[k], slot0, 0.0), axis=0, keepdims=True)
            pos_ref[k:k + 1, :] = pos_k.astype(jnp.int32)
        run_scr[...] = run_scr[...] + tile_cnt


def _router(logits):
    t_all = logits.shape[1]
    col = pltpu.VMEM((N_EXPERTS, 1), F32)
    return pl.pallas_call(
        _router_kernel,
        grid=(2, t_all // ROUTER_TILE),
        in_specs=[pl.BlockSpec((N_EXPERTS, ROUTER_TILE), lambda ph, i: (0, i))],
        out_specs=(
            pl.BlockSpec((ROUTER_TILE, LANES), lambda ph, i: (i * ph, 0)),
            pl.BlockSpec((TOP_K, ROUTER_TILE), lambda ph, i: (0, i * ph)),
            _const_out_spec((PLAN_ROWS, PLAN_LANES)),
        ),
        out_shape=(
            jax.ShapeDtypeStruct((t_all, LANES), F32),
            jax.ShapeDtypeStruct((TOP_K, t_all), jnp.int32),
            jax.ShapeDtypeStruct((PLAN_ROWS, PLAN_LANES), jnp.int32),
        ),
        scratch_shapes=[col, col, col, pltpu.VMEM((ROUTER_TILE, ROUTER_TILE), BF16)],
        compiler_params=pltpu.CompilerParams(
            dimension_semantics=("arbitrary", "arbitrary"), vmem_limit_bytes=VMEM_LIMIT),
        name="router",
    )(logits)


def _expert_kernel(plan_ref, x_ref, w1_hbm, b1_ref, w2_hbm, b2_ref, o_ref,
                   w1f_scr, w2f_scr, w1b_scr, w2b_scr, sem, *, layer):
    def weight_copies(expert, slot):
        return (pltpu.make_async_copy(w1_hbm.at[layer, expert], w1f_scr.at[slot], sem.at[0, slot]),
                pltpu.make_async_copy(w2_hbm.at[layer, expert], w2f_scr.at[slot], sem.at[1, slot]))

    for sub in range(TILES_PER_STEP):
        _expert_tile(pl.program_id(0) * TILES_PER_STEP + sub, sub * GEMM_TILE, plan_ref, x_ref,
                     b1_ref, b2_ref, o_ref, w1f_scr, w2f_scr, w1b_scr, w2b_scr, weight_copies)


def _expert_tile(g, row0, plan_ref, x_ref, b1_ref, b2_ref, o_ref, w1f_scr, w2f_scr, w1b_scr, w2b_scr,
                 weight_copies):
    nv = plan_ref[PLAN_VALID, g]
    expert = plan_ref[PLAN_EXPERT, g]

    @pl.when(plan_ref[PLAN_FIRST, g] == 1)
    def _():
        slot = plan_ref[PLAN_SLOT, g]
        nxt = plan_ref[PLAN_NEXT, g]

        @pl.when(g == 0)
        def _():
            for cp in weight_copies(expert, slot):
                cp.start()
        for cp in weight_copies(expert, slot):
            cp.wait()

        @pl.when(nxt >= 0)
        def _():
            for cp in weight_copies(nxt, 1 - slot):
                cp.start()
        w1b_scr[...] = w1f_scr[slot].astype(BF16)
        w2b_scr[...] = w2f_scr[slot].astype(BF16)

    def ffn(m):
        x_left, x_right = _unpack_bf16_pair(x_ref[row0:row0 + m, :])
        x = jnp.concatenate([x_left.astype(BF16), x_right.astype(BF16)], axis=1)

        def hidden(col0):
            cols = slice(col0, col0 + FF_BLOCK)
            return _dot(x, w1b_scr[:, cols]) + b1_ref[expert, :, cols]

        n_blocks = D_FF // FF_BLOCK
        acts = []
        pre = (hidden(0), hidden(D_FF))
        for jb in range(n_blocks):
            g_pre, up_pre = pre
            if jb + 1 < n_blocks:
                pre = (hidden((jb + 1) * FF_BLOCK), hidden(D_FF + (jb + 1) * FF_BLOCK))
            gg = jnp.minimum(g_pre, SWIGLU_LIMIT)
            up = jnp.clip(up_pre, -SWIGLU_LIMIT, SWIGLU_LIMIT)
            acts.append((gg * _sigmoid(gg, SWIGLU_ALPHA) * (up + 1.0)).astype(BF16))
        y = _dot(jnp.concatenate(acts, axis=1), w2b_scr[...]) + b2_ref[expert]
        row = lax.broadcasted_iota(jnp.int32, y.shape, 0)
        o_ref[row0:row0 + m, :] = _pack_bf16_pair(jnp.where(row < nv, y, 0.0))
        if m < GEMM_TILE:
            o_ref[row0 + m:row0 + GEMM_TILE, :] = jnp.zeros((GEMM_TILE - m, D_WORDS), jnp.int32)

    @pl.when(nv > GEMM_TILE // 2)
    def _():
        ffn(GEMM_TILE)

    @pl.when(jnp.logical_and(nv > 0, nv <= GEMM_TILE // 2))
    def _():
        ffn(GEMM_TILE // 2)

    @pl.when(nv == 0)
    def _():
        o_ref[row0:row0 + GEMM_TILE, :] = jnp.zeros((GEMM_TILE, D_WORDS), jnp.int32)


def _experts(xs, plan, w1, b1, w2, b2, layer):
    n_tiles = xs.shape[0] // GEMM_TILE
    depth = w1.shape[0]
    assert n_tiles % TILES_PER_STEP == 0
    step_rows = TILES_PER_STEP * GEMM_TILE
    bias_map = lambda s, plan: (layer, 0, 0, 0)
    grid_spec = pltpu.PrefetchScalarGridSpec(
        num_scalar_prefetch=1,
        grid=(n_tiles // TILES_PER_STEP,),
        in_specs=[
            pl.BlockSpec((step_rows, D_WORDS), lambda s, plan: (s, 0)),
            pl.BlockSpec(memory_space=pl.ANY),
            pl.BlockSpec((None, N_EXPERTS, 1, 2 * D_FF), bias_map, pipeline_mode=pl.Buffered(1)),
            pl.BlockSpec(memory_space=pl.ANY),
            pl.BlockSpec((None, N_EXPERTS, 1, D_MODEL), bias_map, pipeline_mode=pl.Buffered(1)),
        ],
        out_specs=pl.BlockSpec((step_rows, D_WORDS), lambda s, plan: (s, 0)),
        scratch_shapes=[
            pltpu.VMEM((2, D_MODEL, 2 * D_FF), F32),
            pltpu.VMEM((2, D_FF, D_MODEL), F32),
            pltpu.VMEM((D_MODEL, 2 * D_FF), BF16),
            pltpu.VMEM((D_FF, D_MODEL), BF16),
            pltpu.SemaphoreType.DMA((2, 2)),
        ],
    )
    return pl.pallas_call(
        functools.partial(_expert_kernel, layer=layer),
        grid_spec=grid_spec,
        out_shape=jax.ShapeDtypeStruct((n_tiles * GEMM_TILE, D_WORDS), jnp.int32),
        compiler_params=pltpu.CompilerParams(
            dimension_semantics=("arbitrary",), vmem_limit_bytes=VMEM_LIMIT),
        name="experts",
    )(plan, xs, w1, b1.reshape(depth, N_EXPERTS, 1, 2 * D_FF), w2,
      b2.reshape(depth, N_EXPERTS, 1, D_MODEL))


def _final_kernel(x_ref, yg_ref, wgt_ref, gp_ref, gs_ref, fg_ref, op_ref, os_ref, *, n_prompt_blocks):
    i = pl.program_id(0)
    n_seq = gs_ref.shape[0]

    def run(gate, o_ref):
        def body(r0):
            r = pl.ds(r0, COMBINE_CHUNK)
            w = wgt_ref[r, :]
            g = gate(r0)
            xl = x_ref[r, 0:D_WORDS] + g[:, 0:D_WORDS] * _moe_half(yg_ref, w, r, False)
            xr = x_ref[r, D_WORDS:D_MODEL] + g[:, D_WORDS:D_MODEL] * _moe_half(yg_ref, w, r, True)
            ms = (jnp.sum(xl * xl, axis=-1, keepdims=True)
                  + jnp.sum(xr * xr, axis=-1, keepdims=True)) * (1.0 / D_MODEL)
            scale = lax.rsqrt(ms + EPS)
            o_ref[r, 0:D_WORDS] = xl * scale * fg_ref[:, 0:D_WORDS]
            o_ref[r, D_WORDS:D_MODEL] = xr * scale * fg_ref[:, D_WORDS:D_MODEL]
        _row_loop(x_ref.shape[0], body, COMBINE_CHUNK)

    @pl.when(i < n_prompt_blocks)
    def _():
        run(lambda r0: gp_ref[0], op_ref)

    @pl.when(i >= n_prompt_blocks)
    def _():
        run(lambda r0: gs_ref[pl.ds(pl.multiple_of(lax.rem(r0, n_seq), COMBINE_CHUNK), COMBINE_CHUNK), :],
            os_ref)


def _final(xmid, yg, wgt, gate_p, gate_s, final_g, n_prompt, seq_len):
    t_all = xmid.shape[0]
    n_prompt_blocks = n_prompt // ROW_TILE
    per_seq = seq_len // ROW_TILE
    n_seq_p = gate_p.shape[0]
    assert t_all - n_prompt == ROW_TILE
    out_specs = (
        pl.BlockSpec((ROW_TILE, D_MODEL), lambda i: (jnp.minimum(i, n_prompt_blocks - 1), 0)),
        pl.BlockSpec((ROW_TILE, D_MODEL), lambda i: (0, 0)),
    )
    out_shape = (jax.ShapeDtypeStruct((n_prompt, D_MODEL), F32),
                 jax.ShapeDtypeStruct((ROW_TILE, D_MODEL), F32))
    return pl.pallas_call(
        functools.partial(_final_kernel, n_prompt_blocks=n_prompt_blocks),
        grid=(t_all // ROW_TILE,),
        in_specs=[
            pl.BlockSpec((ROW_TILE, D_MODEL), lambda i: (i, 0)),
            pl.BlockSpec((TOP_K, ROW_TILE, D_WORDS), lambda i: (0, i, 0)),
            pl.BlockSpec((ROW_TILE, LANES), lambda i: (i, 0)),
            pl.BlockSpec((1, 1, D_MODEL), lambda i: (jnp.minimum(i // per_seq, n_seq_p - 1), 0, 0)),
            _const_spec(gate_s.shape),
            _const_spec((1, D_MODEL)),
        ],
        out_specs=out_specs,
        out_shape=out_shape,
        compiler_params=pltpu.CompilerParams(
            dimension_semantics=("arbitrary",), vmem_limit_bytes=VMEM_LIMIT),
        name="final_combine",
    )(xmid, yg, wgt, gate_p.reshape(n_seq_p, 1, D_MODEL), gate_s, final_g.reshape(1, D_MODEL))


def _sc_mesh():
    return plsc.VectorSubcoreMesh(core_axis_name="core", subcore_axis_name="subcore",
                                  num_cores=SC_CORES, num_subcores=SC_SUBCORES)


def _sc_worker_id():
    return lax.axis_index("subcore") * SC_CORES + lax.axis_index("core")


def _sc_dispatch(h2w, pos_c, n_slots):
    n_chunks = pos_c.shape[1]

    @functools.partial(
        pl.kernel, mesh=_sc_mesh(),
        out_type=jax.ShapeDtypeStruct((n_slots, D_WORDS), jnp.int32),
        scratch_types=[pltpu.VMEM((TOP_K, SC_CHUNK), jnp.int32),
                       pltpu.VMEM((SC_CHUNK, D_WORDS), jnp.int32),
                       pltpu.SemaphoreType.DMA],
        name="moe_dispatch")
    def run(h_hbm, pos_hbm, out_hbm, idx_v, rows_v, sem):
        wid = _sc_worker_id()

        @pl.loop(0, n_chunks)
        def _(c):
            base = pl.multiple_of((wid * n_chunks + c) * SC_CHUNK, SUBLANES)
            pltpu.sync_copy(pos_hbm.at[wid, c], idx_v)
            pltpu.sync_copy(h_hbm.at[pl.ds(base, SC_CHUNK)], rows_v)
            copies = [pltpu.async_copy(rows_v, out_hbm.at[idx_v.at[k]], sem) for k in range(TOP_K)]
            for cp in copies:
                cp.wait()

    return run(h2w, pos_c)


def _sc_gather_back(ys, pos_c, t_all):
    n_chunks = pos_c.shape[1]

    @functools.partial(
        pl.kernel, mesh=_sc_mesh(),
        out_type=jax.ShapeDtypeStruct((TOP_K, t_all, D_WORDS), jnp.int32),
        scratch_types=[pltpu.VMEM((TOP_K, SC_CHUNK), jnp.int32),
                       pltpu.VMEM((TOP_K, SC_CHUNK, D_WORDS), jnp.int32),
                       pltpu.SemaphoreType.DMA],
        name="moe_gather_back")
    def run(ys_hbm, pos_hbm, out_hbm, idx_v, rows_v, sem):
        wid = _sc_worker_id()

        @pl.loop(0, n_chunks)
        def _(c):
            base = pl.multiple_of((wid * n_chunks + c) * SC_CHUNK, SUBLANES)
            pltpu.sync_copy(pos_hbm.at[wid, c], idx_v)
            copies = [pltpu.async_copy(ys_hbm.at[idx_v.at[k]], rows_v.at[k], sem) for k in range(TOP_K)]
            for cp in copies:
                cp.wait()
            for k in range(TOP_K):
                pltpu.sync_copy(rows_v.at[k], out_hbm.at[k, pl.ds(base, SC_CHUNK)])

    return run(ys, pos_c)


def kernel(x_prompt, x_sample, state_conv, c_prompt, c_sample, w_ada, b_ada, w_in, w_conv, b_conv,
           gn_g, gn_b, sgu_ln_g, sgu_ln_b, w_s, b_s, beta_a, beta_b, w_out, w_router, b_router,
           w1, b1, w2, b2, final_g):
    n_bp, seq_len, _ = x_prompt.shape
    n_bs, n_pos, _ = x_sample.shape
    depth = w_ada.shape[0]
    n_prompt = n_bp * seq_len
    n_sample = n_bs * n_pos
    t_all = n_prompt + n_sample
    n_tiles = (t_all * TOP_K) // GEMM_TILE + N_EXPERTS
    n_workers = SC_CORES * SC_SUBCORES
    assert t_all % (n_workers * SC_CHUNK) == 0 and n_tiles <= PLAN_LANES and t_all % ROUTER_TILE == 0

    x_p = x_prompt.reshape(n_prompt, D_MODEL)
    x_s = x_sample.transpose(1, 0, 2).reshape(n_sample, D_MODEL)
    x_s_block = 0
    moe = None
    mods =_adaln(jnp.concatenate([c_prompt, c_sample], axis=0), w_ada, b_ada)
    state_t = state_conv.transpose(0, 2, 1, 3)
    grp = jnp.arange(C_CONV) // (C_CONV // CONV_GROUPS)
    g_mat = ((grp[:, None] == grp[None, :]).astype(F32) / (C_CONV // CONV_GROUPS)).astype(BF16)

    conv_p, conv_s, v_s = [], [], []
    for l in range(depth):
        mod_p = mods[l, :n_bp].reshape(n_bp, 1, N_MOD * D_MODEL)
        mod_s = mods[l, n_bp:]
        wconv_p = jnp.broadcast_to(
            jnp.pad(w_conv[l], ((0, CARRY_ROWS - CONV_WIDTH), (0, 0)))[:, None, :],
            (CARRY_ROWS, SUBLANES, C_CONV))
        zero = jnp.zeros_like(b_conv[l])
        vec = jnp.stack([b_conv[l], gn_g[l], gn_b[l], sgu_ln_g[l], sgu_ln_b[l],
                         beta_a[l], beta_b[l], zero], axis=0)
        bs_full = jnp.repeat(b_s[l].T, SGU_HEAD_DIM, axis=1)
        wsv = jnp.repeat(w_s[l][:, :n_pos, :n_pos].transpose(1, 2, 0).reshape(n_pos * n_pos, SGU_HEADS),
                         SGU_HEAD_DIM, axis=1)
        bsv = jnp.repeat(b_s[l][:, :n_pos].T, SGU_HEAD_DIM, axis=1)
        xmid, h2, logits, cst, a_new, v_new = _mixer(
            x_p, x_s, x_s_block, moe, mod_p, mod_s, w_in[l].astype(BF16), w_out[l].astype(BF16), wconv_p,
            vec, g_mat,
            w_s[l], bs_full, wsv, bsv, w_router[l].T, b_router[l].reshape(N_EXPERTS, 1),
            state_t[l], n_bp, seq_len)
        conv_p.append(cst)
        conv_s.append(a_new)
        v_s.append(v_new)

        wgt, pos, plan = _router(logits)
        pos_c = pos.reshape(TOP_K, n_workers, -1, SC_CHUNK).transpose(1, 2, 0, 3)
        xs = _sc_dispatch(h2, pos_c, n_tiles * GEMM_TILE)
        ys = _experts(xs, plan, w1, b1, w2, b2, l)
        yg = _sc_gather_back(ys, pos_c, t_all)
        moe = (yg, wgt, mod_p[:, 0, 5 * D_MODEL:], mod_s[:, 5 * D_MODEL:])
        x_p = x_s = xmid
        x_s_block = n_prompt // n_sample

    y_p, y_s = _final(xmid, *moe, final_g, n_prompt, seq_len)
    y_prompt = y_p.reshape(n_bp, seq_len, D_MODEL)
    y_sample = y_s.reshape(n_pos, n_bs, D_MODEL).transpose(1, 0, 2)
    new_conv_s = jnp.stack(conv_s, axis=0).transpose(0, 2, 1, 3)
    new_v_s = jnp.stack(v_s, axis=0).reshape(depth, n_pos, n_bs, C_SGU).transpose(0, 2, 1, 3)
    return (y_prompt, y_sample, jnp.stack(conv_p, axis=0), new_conv_s, new_v_s)
```

```python
import functools
import math

import jax
import jax.numpy as jnp
from jax import lax
from jax.experimental import pallas as pl
from jax.experimental.pallas import tpu as pltpu
from jax.experimental.pallas import tpu_sc as plsc

F32 = jnp.float32
BF16 = jnp.bfloat16

D_MODEL = 1024
C_CONV = 512
C_SGU = 512
CONV_GROUPS = 8
CONV_WIDTH = 31
CONV_STATE = CONV_WIDTH - 1
SGU_HEADS = 4
SGU_HEAD_DIM = C_SGU // SGU_HEADS
CHUNK = 128
N_EXPERTS = 32
TOP_K = 4
D_FF = 1024
SWIGLU_LIMIT = 7.0
SWIGLU_ALPHA = 1.702
N_MOD = 6
EPS = 1e-5
LOG2_E = 1.0 / math.log(2.0)

ROW_TILE = 512
ROUTER_TILE = 1536
ROW_CHUNK = 128
CONV_CHUNK = 32
COMBINE_CHUNK = 64
CARRY_ROWS = 32
SUBLANES = 8
LANES = 128
PLAN_ROWS = 8
PLAN_LANES = 256
PLAN_EXPERT, PLAN_VALID, PLAN_FIRST, PLAN_SLOT, PLAN_NEXT = range(5)
GEMM_TILE = 512
FF_BLOCK = 256
TILES_PER_STEP = 4
VMEM_LIMIT = 56 * 1024 * 1024
D_WORDS = D_MODEL // 2
DISPATCH_CHUNK = 88
GATHER_CHUNK = 48
HI_MASK = -65536
SC_CORES = 2
SC_SUBCORES = 16


def _rms(x):
    return x * lax.rsqrt(jnp.mean(x * x, axis=-1, keepdims=True) + EPS)


def _gelu(x):
    return 0.5 * x * (1.0 + lax.erf(x * (1.0 / math.sqrt(2.0))))


def _sigmoid(x, scale=1.0):
    return 1.0 / (1.0 + jnp.exp2(x * (-scale * LOG2_E)))


def _split_bf16(x):
    hi = x.astype(BF16)
    lo = (x - hi.astype(F32)).astype(BF16)
    return hi, lo


def _pack_bf16_pair(x):
    bits = lax.bitcast_convert_type(x.astype(BF16).astype(F32), jnp.int32)
    return lax.shift_right_logical(bits[:, :D_WORDS], 16) | (bits[:, D_WORDS:] & HI_MASK)


def _unpack_bf16_pair(w):
    left = lax.bitcast_convert_type(lax.shift_left(w, 16), F32)
    right = lax.bitcast_convert_type(w & HI_MASK, F32)
    return left, right


def _dot(a, b):
    return jnp.dot(a, b, preferred_element_type=F32)


def _dot_nt(a, b):
    return lax.dot_general(a, b, (((1,), (1,)), ((), ())), preferred_element_type=F32)


def _row_loop(n_rows, body, chunk=None):
    chunk = ROW_CHUNK if chunk is None else chunk

    def step(i, carry):
        body(pl.multiple_of(i * chunk, chunk))
        return carry
    lax.fori_loop(0, n_rows // chunk, step, 0)


def _conv_tap(wconv_ref, k):
    return jnp.concatenate([wconv_ref[k]] * (CONV_CHUNK // SUBLANES), axis=0)


def _static_loop(n_rows, body, chunk=None):
    chunk = ROW_CHUNK if chunk is None else chunk
    for r0 in range(0, n_rows, chunk):
        body(r0)


def _adaln_kernel(c_ref, w_ref, b_ref, o_ref):
    c = c_ref[...]
    s_hi, s_lo = _split_bf16(c * _sigmoid(c))
    w_hi, w_lo = _split_bf16(w_ref[0])
    acc = _dot(s_hi, w_hi) + _dot(s_hi, w_lo) + _dot(s_lo, w_hi)
    o_ref[0] = acc + b_ref[0]


def _adaln(c_all, w_ada, b_ada):
    depth, _, n_out = w_ada.shape
    n_rows = c_all.shape[0]
    tn = 1024
    return pl.pallas_call(
        _adaln_kernel,
        grid=(depth, n_out // tn),
        in_specs=[
            pl.BlockSpec((n_rows, D_MODEL), lambda l, j: (0, 0)),
            pl.BlockSpec((1, D_MODEL, tn), lambda l, j: (l, 0, j)),
            pl.BlockSpec((1, 1, tn), lambda l, j: (l, 0, j)),
        ],
        out_specs=pl.BlockSpec((1, n_rows, tn), lambda l, j: (l, 0, j)),
        out_shape=jax.ShapeDtypeStruct((depth, n_rows, n_out), F32),
        compiler_params=pltpu.CompilerParams(
            dimension_semantics=("arbitrary", "arbitrary"), vmem_limit_bytes=VMEM_LIMIT),
        name="adaln",
    )(c_all, w_ada, b_ada.reshape(depth, 1, n_out))


def _phase_in_norm(x_ref, h_scr, mod, n_rows, loop=_row_loop):
    def body(r0):
        r = pl.ds(r0, ROW_CHUNK)
        h = _rms(x_ref[r, :]) * (1.0 + mod(1, r0)) + mod(0, r0)
        h_scr[r, :] = h.astype(BF16)
    loop(n_rows, body)


def _phase_glu(z_scr, a_dst, a_off, n_rows, loop=_row_loop):
    def body(r0):
        r = pl.ds(r0, ROW_CHUNK)
        a_dst[pl.ds(r0 + a_off, ROW_CHUNK), :] = z_scr[r, 0:C_CONV] * _sigmoid(z_scr[r, C_CONV:2 * C_CONV])
    loop(n_rows, body)


def _phase_uv(z_scr, vec_ref, u_scr, v_scr, vout_ref, n_rows, loop=_row_loop):
    ln_g = vec_ref[3:4, :]
    ln_b = vec_ref[4:5, :]

    def body(r0):
        r = pl.ds(r0, ROW_CHUNK)
        u_scr[r, :] = _gelu(z_scr[r, 2 * C_CONV:2 * C_CONV + C_SGU])
        gv = _gelu(z_scr[r, 2 * C_CONV + C_SGU:2 * C_CONV + 2 * C_SGU])
        mu = jnp.mean(gv, axis=-1, keepdims=True)
        dv = gv - mu
        var = jnp.mean(dv * dv, axis=-1, keepdims=True)
        v = dv * lax.rsqrt(var + EPS) * ln_g + ln_b
        if vout_ref is not None:
            vout_ref[r, :] = v
        v_scr[r, :] = v.astype(v_scr.dtype)
    loop(n_rows, body)


def _phase_group_norm(conv_scr, convb_scr, stat_scr, g_ref, vec_ref, y_scr, n_rows, loop=_row_loop):
    stat_scr[...] = _dot(convb_scr[...], g_ref[...])

    def center(r0):
        r = pl.ds(r0, ROW_CHUNK)
        d = conv_scr[r, :] - stat_scr[r, :]
        conv_scr[r, :] = d
        convb_scr[r, :] = (d * d).astype(BF16)
    loop(n_rows, center)
    stat_scr[...] = _dot(convb_scr[...], g_ref[...])
    gn_g = vec_ref[1:2, :]
    gn_b = vec_ref[2:3, :]
    beta_a = vec_ref[5:6, :]

    def finish(r0):
        r = pl.ds(r0, ROW_CHUNK)
        gn = conv_scr[r, :] * lax.rsqrt(stat_scr[r, :] + EPS) * gn_g + gn_b
        ya = gn * _sigmoid(gn)
        y_scr[r, 0:C_CONV] = (_rms(ya) * beta_a).astype(BF16)
    loop(n_rows, finish)


def _phase_out(x_ref, y_scr, wout_ref, z_scr, wr_ref, br_ref, mod, h_scr,
               xmid_ref, h2_ref, logit_ref, n_rows, loop=_row_loop):
    z_scr[:, 0:D_MODEL] = (_dot(y_scr[:, 0:C_CONV], wout_ref[0:C_CONV, :])
                           + _dot(y_scr[:, C_CONV:C_CONV + C_SGU], wout_ref[C_CONV:C_CONV + C_SGU, :]))

    def body(r0):
        r = pl.ds(r0, ROW_CHUNK)
        xm = x_ref[r, :] + mod(2, r0) * z_scr[r, 0:D_MODEL]
        xmid_ref[r, :] = xm
        h2 = _rms(xm) * (1.0 + mod(4, r0)) + mod(3, r0)
        hi, lo = _split_bf16(h2)
        h2_ref[r, :] = _pack_bf16_pair(h2)
        y_scr[r, :] = hi
        h_scr[r, :] = lo
    loop(n_rows, body)
    w_hi, w_lo = _split_bf16(wr_ref[...])
    both = _dot_nt(jnp.concatenate([w_hi, w_lo], axis=0), y_scr[...])
    logit_ref[...] = (both[0:N_EXPERTS] + both[N_EXPERTS:2 * N_EXPERTS]
                      + _dot_nt(w_hi, h_scr[...])) + br_ref[...]


def _moe_half(yg_ref, w, r, hi_half):
    moe = jnp.zeros((COMBINE_CHUNK, D_WORDS), F32)
    for k in range(TOP_K):
        word = yg_ref[k, r, :]
        bits = (word & HI_MASK) if hi_half else lax.shift_left(word, 16)
        moe = moe + w[:, k:k + 1] * lax.bitcast_convert_type(bits, F32)
    return moe


def _phase_moe_residual(xmid_ref, yg_ref, wgt_ref, gate, x_dst, n_rows, loop=_row_loop):
    def body(r0):
        r = pl.ds(r0, COMBINE_CHUNK)
        w = wgt_ref[r, :]
        g = gate(r0)
        for hi_half in (False, True):
            cols = slice(D_WORDS, D_MODEL) if hi_half else slice(0, D_WORDS)
            x_dst[r, cols] = xmid_ref[r, cols] + g[:, cols] * _moe_half(yg_ref, w, r, hi_half)
    loop(n_rows, body, COMBINE_CHUNK)


def _mixer_prompt_kernel(*refs, tiles_per_seq, n_prompt_tiles, fused):
    x_ref, mod_ref = refs[:2]
    refs = refs[2:]
    if fused:
        yg_ref, wgt_ref, gate_ref = refs[:3]
        refs = refs[3:]
        x_scr = refs[-1]
        refs = refs[:-1]
    (win_ref, wout_ref, wconv_ref, vec_ref, g_ref, ws_ref, bs_ref, wr_ref, br_ref,
     smid_ref, sh2_ref, slogit_ref, xmid_ref, h2_ref, logit_ref, cst_ref,
     h_scr, z_scr, aext_scr, conv_scr, convb_scr, stat_scr, u_scr, v_scr, y_scr) = refs
    i = pl.program_id(0)

    @pl.when(i == n_prompt_tiles)
    def _():
        xmid_ref[...] = smid_ref[...]
        h2_ref[...] = sh2_ref[...]
        logit_ref[...] = slogit_ref[...]

    @pl.when(i < n_prompt_tiles)
    def _():
        x_src = x_ref
        if fused:
            _phase_moe_residual(x_ref, yg_ref, wgt_ref, lambda r0: gate_ref[0], x_scr, ROW_TILE,
                                _static_loop)
            x_src = x_scr
        _mixer_prompt_tile(lax.rem(i, tiles_per_seq), tiles_per_seq,
                           x_src, mod_ref, win_ref, wout_ref, wconv_ref, vec_ref, g_ref,
                           ws_ref, bs_ref, wr_ref, br_ref, xmid_ref, h2_ref, logit_ref, cst_ref,
                           h_scr, z_scr, aext_scr, conv_scr, convb_scr, stat_scr, u_scr, v_scr, y_scr)


def _mixer_prompt_tile(j, tiles_per_seq, x_ref, mod_ref, win_ref, wout_ref, wconv_ref, vec_ref, g_ref,
                       ws_ref, bs_ref, wr_ref, br_ref, xmid_ref, h2_ref, logit_ref, cst_ref,
                       h_scr, z_scr, aext_scr, conv_scr, convb_scr, stat_scr, u_scr, v_scr, y_scr):
    n_rows = ROW_TILE

    def mod(idx, r0):
        del r0
        return mod_ref[0, :, idx * D_MODEL:(idx + 1) * D_MODEL]

    a_buf = aext_scr.at[0]

    @pl.when(j == 0)
    def _():
        a_buf[0:CARRY_ROWS, :] = jnp.zeros((CARRY_ROWS, C_CONV), F32)

    loop = _static_loop
    _phase_in_norm(x_ref, h_scr, mod, n_rows, loop)
    n_glu = 2 * C_CONV
    z_scr[:, 0:n_glu] = _dot(h_scr[...], win_ref[:, 0:n_glu])
    _phase_glu(z_scr, a_buf, CARRY_ROWS, n_rows, loop)
    z_scr[:, n_glu:] = _dot(h_scr[...], win_ref[:, n_glu:])
    n_shift = n_rows + CARRY_ROWS - SUBLANES
    for s in range(1, SUBLANES):
        aext_scr[s, 0:n_shift, :] = a_buf[s:s + n_shift, :]

    b_conv = vec_ref[0:1, :]
    lead = CARRY_ROWS - CONV_STATE
    for r0 in range(0, n_rows, CONV_CHUNK):
        acc = jnp.zeros((CONV_CHUNK, C_CONV), F32)
        for k in range(CONV_WIDTH):
            s = (lead + k) % SUBLANES
            q0 = r0 + (lead + k - s)
            acc = acc + _conv_tap(wconv_ref, k) * aext_scr[s, q0:q0 + CONV_CHUNK, :]
        conv = acc + b_conv
        conv_scr[r0:r0 + CONV_CHUNK, :] = conv
        convb_scr[r0:r0 + CONV_CHUNK, :] = conv.astype(BF16)
    _phase_uv(z_scr, vec_ref, u_scr, v_scr, None, n_rows, loop)
    _phase_group_norm(conv_scr, convb_scr, stat_scr, g_ref, vec_ref, y_scr, n_rows, loop)

    row_i = lax.broadcasted_iota(jnp.int32, (CHUNK, CHUNK), 0)
    col_i = lax.broadcasted_iota(jnp.int32, (CHUNK, CHUNK), 1)
    tril = (col_i <= row_i).astype(F32)
    for hd in range(SGU_HEADS):
        ws_h = (ws_ref[hd] * tril).astype(BF16)
        cols = slice(hd * SGU_HEAD_DIM, (hd + 1) * SGU_HEAD_DIM)
        s_cols = slice(n_glu + hd * SGU_HEAD_DIM, n_glu + (hd + 1) * SGU_HEAD_DIM)
        for c in range(n_rows // CHUNK):
            rows = slice(c * CHUNK, (c + 1) * CHUNK)
            z_scr[rows, s_cols] = _dot(ws_h, v_scr[rows, cols])
    beta_b = vec_ref[6:7, :]

    def gate_body(r0):
        r = pl.ds(r0, ROW_CHUNK)
        rb = pl.ds(r0 % CHUNK, ROW_CHUNK)
        yb = u_scr[r, :] * (z_scr[r, n_glu:n_glu + C_SGU] + bs_ref[rb, :])
        y_scr[r, C_CONV:C_CONV + C_SGU] = (_rms(yb) * beta_b).astype(BF16)
    loop(n_rows, gate_body)

    _phase_out(x_ref, y_scr, wout_ref, z_scr, wr_ref, br_ref, mod, h_scr,
               xmid_ref, h2_ref, logit_ref, n_rows, loop)

    @pl.when(j == tiles_per_seq - 1)
    def _():
        cst_ref[0] = a_buf[n_rows + lead:n_rows + CARRY_ROWS, :]

    a_buf[0:CARRY_ROWS, :] = a_buf[n_rows:n_rows + CARRY_ROWS, :]


def _mixer_sample_kernel(*refs, fused):
    x_ref, mod_ref = refs[:2]
    refs = refs[2:]
    if fused:
        yg_ref, wgt_ref, gate_ref = refs[:3]
        refs = refs[3:]
        x_scr = refs[-1]
        refs = refs[:-1]
    (win_ref, wout_ref, wconv_ref, vec_ref, g_ref, wsv_ref, bsv_ref, wr_ref, br_ref, state_ref,
     xmid_ref, h2_ref, logit_ref, cnew_ref, vout_ref,
     h_scr, z_scr, conv_scr, convb_scr, stat_scr, u_scr, v_scr, y_scr, anew_ref) = refs
    n_rows = x_ref.shape[0]
    n_seq = state_ref.shape[1]
    n_pos = n_rows // n_seq
    loop = _static_loop

    def mod(idx, r0):
        return mod_ref[pl.ds(r0 % n_seq, ROW_CHUNK), idx * D_MODEL:(idx + 1) * D_MODEL]

    if fused:
        def gate(r0):
            return gate_ref[pl.ds(r0 % n_seq, COMBINE_CHUNK), :]
        _phase_moe_residual(x_ref, yg_ref, wgt_ref, gate, x_scr, n_rows, loop)
        x_ref = x_scr

    _phase_in_norm(x_ref, h_scr, mod, n_rows, loop)
    z_scr[...] = _dot(h_scr[...], win_ref[...])
    _phase_glu(z_scr, anew_ref, 0, n_rows, loop)
    _phase_uv(z_scr, vec_ref, u_scr, v_scr, vout_ref, n_rows, loop)

    keep = CONV_STATE - n_pos
    cnew_ref[0:keep] = state_ref[n_pos:CONV_STATE]
    for t in range(n_pos):
        cnew_ref[keep + t] = anew_ref[t * n_seq:(t + 1) * n_seq, :]

    b_conv = vec_ref[0:1, :]

    def conv_body(s0):
        rs = pl.ds(s0, CONV_CHUNK)
        for t in range(n_pos):
            acc = jnp.zeros((CONV_CHUNK, C_CONV), F32)
            for k in range(CONV_WIDTH):
                p = t + k
                if p < CONV_STATE:
                    src = state_ref[p, rs, :]
                else:
                    src = anew_ref[pl.ds(s0 + (p - CONV_STATE) * n_seq, CONV_CHUNK), :]
                acc = acc + _conv_tap(wconv_ref, k) * src
            conv = acc + b_conv
            r = pl.ds(s0 + t * n_seq, CONV_CHUNK)
            conv_scr[r, :] = conv
            convb_scr[r, :] = conv.astype(BF16)
    loop(n_seq, conv_body, CONV_CHUNK)

    _phase_group_norm(conv_scr, convb_scr, stat_scr, g_ref, vec_ref, y_scr, n_rows, loop)

    beta_b = vec_ref[6:7, :]

    def gate_body(s0):
        for t in range(n_pos):
            s = jnp.zeros((CONV_CHUNK, C_SGU), F32)
            for jj in range(t + 1):
                s = s + wsv_ref[t * n_pos + jj:t * n_pos + jj + 1, :] * v_scr[pl.ds(s0 + jj * n_seq, CONV_CHUNK), :]
            r = pl.ds(s0 + t * n_seq, CONV_CHUNK)
            yb = u_scr[r, :] * (s + bsv_ref[t:t + 1, :])
            y_scr[r, C_CONV:C_CONV + C_SGU] = (_rms(yb) * beta_b).astype(BF16)
    loop(n_seq, gate_body, CONV_CHUNK)

    _phase_out(x_ref, y_scr, wout_ref, z_scr, wr_ref, br_ref, mod, h_scr,
               xmid_ref, h2_ref, logit_ref, n_rows, loop)


def _const_spec(shape):
    nd = len(shape)
    return pl.BlockSpec(shape, lambda *_: (0,) * nd, pipeline_mode=pl.Buffered(1))


def _const_out_spec(shape):
    nd = len(shape)
    return pl.BlockSpec(shape, lambda *_: (0,) * nd)


def _mixer(x_p, x_s, x_s_block, moe, mod_p, mod_s, w_in_b, w_out_b, wconv_p, vec, g_mat, w_s, bs_full,
           wsv, bsv, w_router, b_router, state_t, n_prompt_seq, seq_len):
    fused = moe is not None
    n_prompt = n_prompt_seq * seq_len
    n_seq, n_pos = state_t.shape[1], bsv.shape[0]
    n_sample = n_seq * n_pos
    t_all = n_prompt + n_sample
    nj = seq_len // ROW_TILE
    weight_specs = [
        _const_spec((D_MODEL, 2 * C_CONV + 2 * C_SGU)),
        _const_spec((D_MODEL, D_MODEL)),
        _const_spec((CARRY_ROWS, SUBLANES, C_CONV)),
        _const_spec((8, C_CONV)),
        _const_spec((C_CONV, C_CONV)),
    ]
    router_specs = [_const_spec((N_EXPERTS, D_MODEL)), _const_spec((N_EXPERTS, 1))]
    common_scratch = lambda n: [
        pltpu.VMEM((n, D_MODEL), BF16),
        pltpu.VMEM((n, 2 * C_CONV + 2 * C_SGU), F32),
    ]
    tail_scratch = lambda n, vdt: [
        pltpu.VMEM((n, C_CONV), F32),
        pltpu.VMEM((n, C_CONV), BF16),
        pltpu.VMEM((n, C_CONV), F32),
        pltpu.VMEM((n, C_SGU), F32),
        pltpu.VMEM((n, C_SGU), vdt),
        pltpu.VMEM((n, D_MODEL), BF16),
    ]
    assert n_sample == ROW_TILE
    n_tiles_p = n_prompt // ROW_TILE
    tile_p = lambda i: jnp.minimum(i, n_tiles_p - 1)
    seq_of = lambda i: jnp.minimum(i // nj, n_prompt_seq - 1)
    moe_s_specs, moe_p_specs, moe_s_args, moe_p_args, x_scratch = [], [], [], [], []
    if fused:
        yg, wgt, gate_p, gate_s = moe
        moe_s_specs = [pl.BlockSpec((TOP_K, n_sample, D_WORDS), lambda i: (0, x_s_block, 0)),
                       pl.BlockSpec((n_sample, LANES), lambda i: (x_s_block, 0)),
                       _const_spec((n_seq, D_MODEL))]
        moe_p_specs = [pl.BlockSpec((TOP_K, ROW_TILE, D_WORDS), lambda i: (0, tile_p(i), 0)),
                       pl.BlockSpec((ROW_TILE, LANES), lambda i: (tile_p(i), 0)),
                       pl.BlockSpec((1, 1, D_MODEL), lambda i: (seq_of(tile_p(i)), 0, 0))]
        moe_s_args = [yg, wgt, gate_s]
        moe_p_args = [yg, wgt, gate_p.reshape(n_prompt_seq, 1, D_MODEL)]
        x_scratch = [pltpu.VMEM((ROW_TILE, D_MODEL), F32)]
    smid, sh2, slogits, a_new, v_new = pl.pallas_call(
        functools.partial(_mixer_sample_kernel, fused=fused),
        grid=(1,),
        in_specs=[
            pl.BlockSpec((n_sample, D_MODEL), lambda i: (x_s_block, 0)),
            _const_spec((n_seq, N_MOD * D_MODEL)),
            *moe_s_specs,
            *weight_specs,
            _const_spec((n_pos * n_pos, C_SGU)),
            _const_spec((n_pos, C_SGU)),
            *router_specs,
            _const_spec((CONV_STATE, n_seq, C_CONV)),
        ],
        out_specs=(
            _const_out_spec((n_sample, D_MODEL)),
            _const_out_spec((n_sample, D_WORDS)),
            _const_out_spec((N_EXPERTS, n_sample)),
            _const_out_spec((CONV_STATE, n_seq, C_CONV)),
            _const_out_spec((n_sample, C_SGU)),
        ),
        out_shape=(
            jax.ShapeDtypeStruct((n_sample, D_MODEL), F32),
            jax.ShapeDtypeStruct((n_sample, D_WORDS), jnp.int32),
            jax.ShapeDtypeStruct((N_EXPERTS, n_sample), F32),
            jax.ShapeDtypeStruct((CONV_STATE, n_seq, C_CONV), F32),
            jax.ShapeDtypeStruct((n_sample, C_SGU), F32),
        ),
        scratch_shapes=common_scratch(n_sample) + tail_scratch(n_sample, F32)
        + [pltpu.VMEM((n_sample, C_CONV), F32)] + x_scratch,
        compiler_params=pltpu.CompilerParams(
            dimension_semantics=("arbitrary",), vmem_limit_bytes=VMEM_LIMIT),
        name="mixer_sample",
    )(x_s, mod_s, *moe_s_args, w_in_b, w_out_b, wconv_p, vec, g_mat, wsv, bsv, w_router, b_router,
      state_t)

    xmid, h2, logits, cst = pl.pallas_call(
        functools.partial(_mixer_prompt_kernel, tiles_per_seq=nj, n_prompt_tiles=n_tiles_p,
                          fused=fused),
        grid=(n_tiles_p + 1,),
        in_specs=[
            pl.BlockSpec((ROW_TILE, D_MODEL), lambda i: (tile_p(i), 0)),
            pl.BlockSpec((1, 1, N_MOD * D_MODEL), lambda i: (seq_of(i), 0, 0)),
            *moe_p_specs,
            *weight_specs,
            _const_spec((SGU_HEADS, CHUNK, CHUNK)),
            _const_spec((CHUNK, C_SGU)),
            *router_specs,
            _const_spec((n_sample, D_MODEL)),
            _const_spec((n_sample, D_WORDS)),
            _const_spec((N_EXPERTS, n_sample)),
        ],
        out_specs=(
            pl.BlockSpec((ROW_TILE, D_MODEL), lambda i: (i, 0)),
            pl.BlockSpec((ROW_TILE, D_WORDS), lambda i: (i, 0)),
            pl.BlockSpec((N_EXPERTS, ROW_TILE), lambda i: (0, i)),
            pl.BlockSpec((1, CONV_STATE, C_CONV), lambda i: (seq_of(i), 0, 0)),
        ),
        out_shape=(
            jax.ShapeDtypeStruct((t_all, D_MODEL), F32),
            jax.ShapeDtypeStruct((t_all, D_WORDS), jnp.int32),
            jax.ShapeDtypeStruct((N_EXPERTS, t_all), F32),
            jax.ShapeDtypeStruct((n_prompt_seq, CONV_STATE, C_CONV), F32),
        ),
        scratch_shapes=common_scratch(ROW_TILE)
        + [pltpu.VMEM((SUBLANES, CARRY_ROWS + ROW_TILE, C_CONV), F32)]
        + tail_scratch(ROW_TILE, BF16) + x_scratch,
        compiler_params=pltpu.CompilerParams(
            dimension_semantics=("arbitrary",), vmem_limit_bytes=VMEM_LIMIT),
        name="mixer_prompt",
    )(x_p, mod_p, *moe_p_args, w_in_b, w_out_b, wconv_p, vec, g_mat, w_s, bs_full, w_router, b_router,
      smid, sh2, slogits)
    return xmid, h2, logits, cst, a_new, v_new


def _tile_plan(cnt):
    e_sub = lax.broadcasted_iota(jnp.int32, (N_EXPERTS, N_EXPERTS), 0)
    e_lane = lax.broadcasted_iota(jnp.int32, (N_EXPERTS, N_EXPERTS), 1)
    tiles = jnp.floor((cnt + (GEMM_TILE - 1.0)) * (1.0 / GEMM_TILE))
    active = jnp.where(cnt > 0.0, 1.0, 0.0)

    def over_experts(tri, col):
        return _dot(tri.astype(BF16), jnp.broadcast_to(col, (N_EXPERTS, LANES)).astype(BF16))[:, 0:1]
    tile_end = over_experts(jnp.where(e_lane <= e_sub, 1.0, 0.0), tiles)
    tile_start = tile_end - tiles
    ordinal = over_experts(jnp.where(e_lane < e_sub, 1.0, 0.0), active)

    g = lax.broadcasted_iota(jnp.int32, (1, PLAN_LANES), 1).astype(F32)
    e_id = lax.broadcasted_iota(jnp.int32, (N_EXPERTS, PLAN_LANES), 0).astype(F32)
    done = jnp.sum(jnp.where(tile_end <= g, 1.0, 0.0), axis=0, keepdims=True)
    live = done < float(N_EXPERTS)
    last_e = jnp.max(jnp.where(cnt > 0.0, e_id[:, 0:1], 0.0), axis=0, keepdims=True)
    te = jnp.where(live, done, last_e)
    pick = e_id == te

    def at_tile(col):
        return jnp.sum(jnp.where(pick, col, 0.0), axis=0, keepdims=True)
    start_g = at_tile(tile_start)
    ord_g = at_tile(ordinal)
    valid = jnp.clip(at_tile(cnt) - (g - start_g) * GEMM_TILE, 0.0, float(GEMM_TILE))
    valid = jnp.where(live, valid, 0.0)
    first = jnp.where(live, jnp.where(g == start_g, 1.0, 0.0), 0.0)
    slot = ord_g - 2.0 * jnp.floor(ord_g * 0.5)
    later = jnp.where(e_id > te, jnp.where(cnt > 0.0, e_id, float(N_EXPERTS)), float(N_EXPERTS))
    nxt = jnp.min(later, axis=0, keepdims=True)
    nxt = jnp.where(nxt >= float(N_EXPERTS), -1.0, nxt)

    row = lax.broadcasted_iota(jnp.int32, (PLAN_ROWS, PLAN_LANES), 0)
    plan = jnp.zeros((PLAN_ROWS, PLAN_LANES), F32)
    for r, v in ((PLAN_EXPERT, te), (PLAN_VALID, valid), (PLAN_FIRST, first), (PLAN_SLOT, slot),
                 (PLAN_NEXT, nxt)):
        plan = jnp.where(row == r, v, plan)
    return plan.astype(jnp.int32), tile_start * GEMM_TILE


def _router_kernel(lg_ref, wgt_ref, pos_ref, plan_ref, cnt_scr, run_scr, start_scr, earlier_scr):
    ph = pl.program_id(0)
    i = pl.program_id(1)
    n = lg_ref.shape[1]

    @pl.when(jnp.logical_and(ph == 0, i == 0))
    def _():
        cnt_scr[...] = jnp.zeros_like(cnt_scr)

    lg = lg_ref[...]
    sub = lax.broadcasted_iota(jnp.int32, lg.shape, 0).astype(F32)
    vals, sels = [], []
    for _ in range(TOP_K):
        m = jnp.max(lg, axis=0, keepdims=True)
        idx = jnp.min(jnp.where(lg == m, sub, float(N_EXPERTS)), axis=0, keepdims=True)
        sel = sub == idx
        vals.append(m)
        sels.append(sel)
        lg = jnp.where(sel, -jnp.inf, lg)
    onehot = jnp.zeros(lg.shape, F32)
    for sel in sels:
        onehot = onehot + jnp.where(sel, 1.0, 0.0)
    tile_cnt = jnp.sum(onehot, axis=1, keepdims=True)

    @pl.when(ph == 0)
    def _():
        cnt_scr[...] = cnt_scr[...] + tile_cnt

    @pl.when(jnp.logical_and(ph == 1, i == 0))
    def _():
        plan, row_start = _tile_plan(cnt_scr[...])
        plan_ref[...] = plan
        start_scr[...] = row_start
        run_scr[...] = jnp.zeros_like(run_scr)
        t_row = lax.broadcasted_iota(jnp.int32, (n, n), 0)
        t_col = lax.broadcasted_iota(jnp.int32, (n, n), 1)
        earlier_scr[...] = jnp.where(t_row < t_col, 1.0, 0.0).astype(BF16)

    @pl.when(ph == 1)
    def _():
        exps = [jnp.exp(v - vals[0]) for v in vals]
        inv = 1.0 / (exps[0] + exps[1] + exps[2] + exps[3])
        w_row = lax.broadcasted_iota(jnp.int32, (LANES, n), 0)
        w_all = jnp.zeros((LANES, n), F32)
        for k in range(TOP_K):
            w_all = jnp.where(w_row == k, exps[k] * inv, w_all)
        wgt_ref[...] = w_all.T
        slot0 = _dot(onehot.astype(BF16), earlier_scr[...]) + (run_scr[...] + start_scr[...])
        for k in range(TOP_K):
            pos_k = jnp.sum(jnp.where(sels[k], slot0, 0.0), axis=0, keepdims=True)
            pos_ref[k:k + 1, :] = pos_k.astype(jnp.int32)
        run_scr[...] = run_scr[...] + tile_cnt


def _router(logits):
    t_all = logits.shape[1]
    col = pltpu.VMEM((N_EXPERTS, 1), F32)
    return pl.pallas_call(
        _router_kernel,
        grid=(2, t_all // ROUTER_TILE),
        in_specs=[pl.BlockSpec((N_EXPERTS, ROUTER_TILE), lambda ph, i: (0, i))],
        out_specs=(
            pl.BlockSpec((ROUTER_TILE, LANES), lambda ph, i: (i * ph, 0)),
            pl.BlockSpec((TOP_K, ROUTER_TILE), lambda ph, i: (0, i * ph)),
            _const_out_spec((PLAN_ROWS, PLAN_LANES)),
        ),
        out_shape=(
            jax.ShapeDtypeStruct((t_all, LANES), F32),
            jax.ShapeDtypeStruct((TOP_K, t_all), jnp.int32),
            jax.ShapeDtypeStruct((PLAN_ROWS, PLAN_LANES), jnp.int32),
        ),
        scratch_shapes=[col, col, col, pltpu.VMEM((ROUTER_TILE, ROUTER_TILE), BF16)],
        compiler_params=pltpu.CompilerParams(
            dimension_semantics=("arbitrary", "arbitrary"), vmem_limit_bytes=VMEM_LIMIT),
        name="router",
    )(logits)


def _expert_kernel(plan_ref, x_ref, w1_hbm, b1_ref, w2_hbm, b2_ref, o_ref,
                   w1f_scr, w2f_scr, w1b_scr, w2b_scr, sem, *, layer):
    def weight_copies(expert, slot):
        return (pltpu.make_async_copy(w1_hbm.at[layer, expert], w1f_scr.at[slot], sem.at[0, slot]),
                pltpu.make_async_copy(w2_hbm.at[layer, expert], w2f_scr.at[slot], sem.at[1, slot]))

    for sub in range(TILES_PER_STEP):
        _expert_tile(pl.program_id(0) * TILES_PER_STEP + sub, sub * GEMM_TILE, plan_ref, x_ref,
                     b1_ref, b2_ref, o_ref, w1f_scr, w2f_scr, w1b_scr, w2b_scr, weight_copies)


def _expert_tile(g, row0, plan_ref, x_ref, b1_ref, b2_ref, o_ref, w1f_scr, w2f_scr, w1b_scr, w2b_scr,
                 weight_copies):
    nv = plan_ref[PLAN_VALID, g]
    expert = plan_ref[PLAN_EXPERT, g]

    @pl.when(plan_ref[PLAN_FIRST, g] == 1)
    def _():
        slot = plan_ref[PLAN_SLOT, g]
        nxt = plan_ref[PLAN_NEXT, g]

        @pl.when(g == 0)
        def _():
            for cp in weight_copies(expert, slot):
                cp.start()
        for cp in weight_copies(expert, slot):
            cp.wait()

        @pl.when(nxt >= 0)
        def _():
            for cp in weight_copies(nxt, 1 - slot):
                cp.start()
        w1b_scr[...] = w1f_scr[slot].astype(BF16)
        w2b_scr[...] = w2f_scr[slot].astype(BF16)

    def ffn(m):
        x_left, x_right = _unpack_bf16_pair(x_ref[row0:row0 + m, :])
        x = jnp.concatenate([x_left.astype(BF16), x_right.astype(BF16)], axis=1)

        def hidden(col0):
            cols = slice(col0, col0 + FF_BLOCK)
            return _dot(x, w1b_scr[:, cols]) + b1_ref[expert, :, cols]

        acts = []
        for jb in range(D_FF // FF_BLOCK):
            gg = jnp.minimum(hidden(jb * FF_BLOCK), SWIGLU_LIMIT)
            up = jnp.clip(hidden(D_FF + jb * FF_BLOCK), -SWIGLU_LIMIT, SWIGLU_LIMIT)
            acts.append((gg * _sigmoid(gg, SWIGLU_ALPHA) * (up + 1.0)).astype(BF16))
        y = _dot(jnp.concatenate(acts, axis=1), w2b_scr[...]) + b2_ref[expert]
        row = lax.broadcasted_iota(jnp.int32, y.shape, 0)
        o_ref[row0:row0 + m, :] = _pack_bf16_pair(jnp.where(row < nv, y, 0.0))
        if m < GEMM_TILE:
            o_ref[row0 + m:row0 + GEMM_TILE, :] = jnp.zeros((GEMM_TILE - m, D_WORDS), jnp.int32)

    @pl.when(nv > GEMM_TILE // 2)
    def _():
        ffn(GEMM_TILE)

    @pl.when(jnp.logical_and(nv > 0, nv <= GEMM_TILE // 2))
    def _():
        ffn(GEMM_TILE // 2)

    @pl.when(nv == 0)
    def _():
        o_ref[row0:row0 + GEMM_TILE, :] = jnp.zeros((GEMM_TILE, D_WORDS), jnp.int32)


def _experts(xs, plan, w1, b1, w2, b2, layer):
    n_tiles = xs.shape[0] // GEMM_TILE
    depth = w1.shape[0]
    assert n_tiles % TILES_PER_STEP == 0
    step_rows = TILES_PER_STEP * GEMM_TILE
    bias_map = lambda s, plan: (layer, 0, 0, 0)
    grid_spec = pltpu.PrefetchScalarGridSpec(
        num_scalar_prefetch=1,
        grid=(n_tiles // TILES_PER_STEP,),
        in_specs=[
            pl.BlockSpec((step_rows, D_WORDS), lambda s, plan: (s, 0)),
            pl.BlockSpec(memory_space=pl.ANY),
            pl.BlockSpec((None, N_EXPERTS, 1, 2 * D_FF), bias_map, pipeline_mode=pl.Buffered(1)),
            pl.BlockSpec(memory_space=pl.ANY),
            pl.BlockSpec((None, N_EXPERTS, 1, D_MODEL), bias_map, pipeline_mode=pl.Buffered(1)),
        ],
        out_specs=pl.BlockSpec((step_rows, D_WORDS), lambda s, plan: (s, 0)),
        scratch_shapes=[
            pltpu.VMEM((2, D_MODEL, 2 * D_FF), F32),
            pltpu.VMEM((2, D_FF, D_MODEL), F32),
            pltpu.VMEM((D_MODEL, 2 * D_FF), BF16),
            pltpu.VMEM((D_FF, D_MODEL), BF16),
            pltpu.SemaphoreType.DMA((2, 2)),
        ],
    )
    return pl.pallas_call(
        functools.partial(_expert_kernel, layer=layer),
        grid_spec=grid_spec,
        out_shape=jax.ShapeDtypeStruct((n_tiles * GEMM_TILE, D_WORDS), jnp.int32),
        compiler_params=pltpu.CompilerParams(
            dimension_semantics=("arbitrary",), vmem_limit_bytes=VMEM_LIMIT),
        name="experts",
    )(plan, xs, w1, b1.reshape(depth, N_EXPERTS, 1, 2 * D_FF), w2,
      b2.reshape(depth, N_EXPERTS, 1, D_MODEL))


def _final_kernel(x_ref, yg_ref, wgt_ref, gp_ref, gs_ref, fg_ref, op_ref, os_ref, *, n_prompt_blocks):
    i = pl.program_id(0)
    n_seq = gs_ref.shape[0]

    def run(gate, o_ref):
        def body(r0):
            r = pl.ds(r0, COMBINE_CHUNK)
            w = wgt_ref[r, :]
            g = gate(r0)
            xl = x_ref[r, 0:D_WORDS] + g[:, 0:D_WORDS] * _moe_half(yg_ref, w, r, False)
            xr = x_ref[r, D_WORDS:D_MODEL] + g[:, D_WORDS:D_MODEL] * _moe_half(yg_ref, w, r, True)
            ms = (jnp.sum(xl * xl, axis=-1, keepdims=True)
                  + jnp.sum(xr * xr, axis=-1, keepdims=True)) * (1.0 / D_MODEL)
            scale = lax.rsqrt(ms + EPS)
            o_ref[r, 0:D_WORDS] = xl * scale * fg_ref[:, 0:D_WORDS]
            o_ref[r, D_WORDS:D_MODEL] = xr * scale * fg_ref[:, D_WORDS:D_MODEL]
        _row_loop(x_ref.shape[0], body, COMBINE_CHUNK)

    @pl.when(i < n_prompt_blocks)
    def _():
        run(lambda r0: gp_ref[0], op_ref)

    @pl.when(i >= n_prompt_blocks)
    def _():
        run(lambda r0: gs_ref[pl.ds(pl.multiple_of(lax.rem(r0, n_seq), COMBINE_CHUNK), COMBINE_CHUNK), :],
            os_ref)


def _final(xmid, yg, wgt, gate_p, gate_s, final_g, n_prompt, seq_len):
    t_all = xmid.shape[0]
    n_prompt_blocks = n_prompt // ROW_TILE
    per_seq = seq_len // ROW_TILE
    n_seq_p = gate_p.shape[0]
    assert t_all - n_prompt == ROW_TILE
    out_specs = (
        pl.BlockSpec((ROW_TILE, D_MODEL), lambda i: (jnp.minimum(i, n_prompt_blocks - 1), 0)),
        pl.BlockSpec((ROW_TILE, D_MODEL), lambda i: (0, 0)),
    )
    out_shape = (jax.ShapeDtypeStruct((n_prompt, D_MODEL), F32),
                 jax.ShapeDtypeStruct((ROW_TILE, D_MODEL), F32))
    return pl.pallas_call(
        functools.partial(_final_kernel, n_prompt_blocks=n_prompt_blocks),
        grid=(t_all // ROW_TILE,),
        in_specs=[
            pl.BlockSpec((ROW_TILE, D_MODEL), lambda i: (i, 0)),
            pl.BlockSpec((TOP_K, ROW_TILE, D_WORDS), lambda i: (0, i, 0)),
            pl.BlockSpec((ROW_TILE, LANES), lambda i: (i, 0)),
            pl.BlockSpec((1, 1, D_MODEL), lambda i: (jnp.minimum(i // per_seq, n_seq_p - 1), 0, 0)),
            _const_spec(gate_s.shape),
            _const_spec((1, D_MODEL)),
        ],
        out_specs=out_specs,
        out_shape=out_shape,
        compiler_params=pltpu.CompilerParams(
            dimension_semantics=("arbitrary",), vmem_limit_bytes=VMEM_LIMIT),
        name="final_combine",
    )(xmid, yg, wgt, gate_p.reshape(n_seq_p, 1, D_MODEL), gate_s, final_g.reshape(1, D_MODEL))


def _sc_mesh():
    return plsc.VectorSubcoreMesh(core_axis_name="core", subcore_axis_name="subcore",
                                  num_cores=SC_CORES, num_subcores=SC_SUBCORES)


def _sc_worker_id():
    return lax.axis_index("subcore") * SC_CORES + lax.axis_index("core")


def _sc_dispatch(h2w, pos_c, n_slots):
    _, n_chunks, _, chunk = pos_c.shape

    @functools.partial(
        pl.kernel, mesh=_sc_mesh(),
        out_type=jax.ShapeDtypeStruct((n_slots, D_WORDS), jnp.int32),
        scratch_types=[pltpu.VMEM((n_chunks, TOP_K, chunk), jnp.int32),
                       pltpu.VMEM((chunk, D_WORDS), jnp.int32),
                       pltpu.SemaphoreType.DMA],
        name="moe_dispatch")
    def run(h_hbm, pos_hbm, out_hbm, idx_v, rows_v, sem):
        wid = _sc_worker_id()
        pltpu.sync_copy(pos_hbm.at[wid], idx_v)

        @pl.loop(0, n_chunks)
        def _(c):
            base = pl.multiple_of((wid * n_chunks + c) * chunk, SUBLANES)
            pltpu.sync_copy(h_hbm.at[pl.ds(base, chunk)], rows_v)
            copies = [pltpu.async_copy(rows_v, out_hbm.at[idx_v.at[c, k]], sem) for k in range(TOP_K)]
            for cp in copies:
                cp.wait()

    return run(h2w, pos_c)


def _sc_gather_back(ys, pos_c, t_all):
    _, n_chunks, _, chunk = pos_c.shape

    @functools.partial(
        pl.kernel, mesh=_sc_mesh(),
        out_type=jax.ShapeDtypeStruct((TOP_K, t_all, D_WORDS), jnp.int32),
        scratch_types=[pltpu.VMEM((n_chunks, TOP_K, chunk), jnp.int32),
                       pltpu.VMEM((TOP_K, chunk, D_WORDS), jnp.int32),
                       pltpu.SemaphoreType.DMA,
                       pltpu.SemaphoreType.DMA],
        name="moe_gather_back")
    def run(ys_hbm, pos_hbm, out_hbm, idx_v, rows_v, sem_in, sem_out):
        wid = _sc_worker_id()
        pltpu.sync_copy(pos_hbm.at[wid], idx_v)

        @pl.loop(0, n_chunks)
        def _(c):
            base = pl.multiple_of((wid * n_chunks + c) * chunk, SUBLANES)
            gathers = [pltpu.async_copy(ys_hbm.at[idx_v.at[c, k]], rows_v.at[k], sem_in)
                       for k in range(TOP_K)]
            for cp in gathers:
                cp.wait()
            writes = [pltpu.async_copy(rows_v.at[k], out_hbm.at[k, pl.ds(base, chunk)], sem_out)
                      for k in range(TOP_K)]
            for cp in writes:
                cp.wait()

    return run(ys, pos_c)


def kernel(x_prompt, x_sample, state_conv, c_prompt, c_sample, w_ada, b_ada, w_in, w_conv, b_conv,
           gn_g, gn_b, sgu_ln_g, sgu_ln_b, w_s, b_s, beta_a, beta_b, w_out, w_router, b_router,
           w1, b1, w2, b2, final_g):
    n_bp, seq_len, _ = x_prompt.shape
    n_bs, n_pos, _ = x_sample.shape
    depth = w_ada.shape[0]
    n_prompt = n_bp * seq_len
    n_sample = n_bs * n_pos
    t_all = n_prompt + n_sample
    n_tiles = (t_all * TOP_K) // GEMM_TILE + N_EXPERTS
    n_workers = SC_CORES * SC_SUBCORES
    assert t_all % (n_workers * DISPATCH_CHUNK) == 0 and t_all % (n_workers * GATHER_CHUNK) == 0
    assert n_tiles <= PLAN_LANES and t_all % ROUTER_TILE == 0

    x_p = x_prompt.reshape(n_prompt, D_MODEL)
    x_s = x_sample.transpose(1, 0, 2).reshape(n_sample, D_MODEL)
    x_s_block = 0
    moe = None
    mods =_adaln(jnp.concatenate([c_prompt, c_sample], axis=0), w_ada, b_ada)
    state_t = state_conv.transpose(0, 2, 1, 3)
    grp = jnp.arange(C_CONV) // (C_CONV // CONV_GROUPS)
    g_mat = ((grp[:, None] == grp[None, :]).astype(F32) / (C_CONV // CONV_GROUPS)).astype(BF16)

    conv_p, conv_s, v_s = [], [], []
    for l in range(depth):
        mod_p = mods[l, :n_bp].reshape(n_bp, 1, N_MOD * D_MODEL)
        mod_s = mods[l, n_bp:]
        wconv_p = jnp.broadcast_to(
            jnp.pad(w_conv[l], ((0, CARRY_ROWS - CONV_WIDTH), (0, 0)))[:, None, :],
            (CARRY_ROWS, SUBLANES, C_CONV))
        zero = jnp.zeros_like(b_conv[l])
        vec = jnp.stack([b_conv[l], gn_g[l], gn_b[l], sgu_ln_g[l], sgu_ln_b[l],
                         beta_a[l], beta_b[l], zero], axis=0)
        bs_full = jnp.repeat(b_s[l].T, SGU_HEAD_DIM, axis=1)
        wsv = jnp.repeat(w_s[l][:, :n_pos, :n_pos].transpose(1, 2, 0).reshape(n_pos * n_pos, SGU_HEADS),
                         SGU_HEAD_DIM, axis=1)
        bsv = jnp.repeat(b_s[l][:, :n_pos].T, SGU_HEAD_DIM, axis=1)
        xmid, h2, logits, cst, a_new, v_new = _mixer(
            x_p, x_s, x_s_block, moe, mod_p, mod_s, w_in[l].astype(BF16), w_out[l].astype(BF16), wconv_p,
            vec, g_mat,
            w_s[l], bs_full, wsv, bsv, w_router[l].T, b_router[l].reshape(N_EXPERTS, 1),
            state_t[l], n_bp, seq_len)
        conv_p.append(cst)
        conv_s.append(a_new)
        v_s.append(v_new)

        wgt, pos, plan = _router(logits)
        def per_worker(chunk):
            return pos.reshape(TOP_K, n_workers, -1, chunk).transpose(1, 2, 0, 3)
        xs = _sc_dispatch(h2, per_worker(DISPATCH_CHUNK), n_tiles * GEMM_TILE)
        ys = _experts(xs, plan, w1, b1, w2, b2, l)
        yg = _sc_gather_back(ys, per_worker(GATHER_CHUNK), t_all)
        moe = (yg, wgt, mod_p[:, 0, 5 * D_MODEL:], mod_s[:, 5 * D_MODEL:])
        x_p = x_s = xmid
        x_s_block = n_prompt // n_sample

    y_p, y_s = _final(xmid, *moe, final_g, n_prompt, seq_len)
    y_prompt = y_p.reshape(n_bp, seq_len, D_MODEL)
    y_sample = y_s.reshape(n_pos, n_bs, D_MODEL).transpose(1, 0, 2)
    new_conv_s = jnp.stack(conv_s, axis=0).transpose(0, 2, 1, 3)
    new_v_s = jnp.stack(v_s, axis=0).reshape(depth, n_pos, n_bs, C_SGU).transpose(0, 2, 1, 3)
    return (y_prompt, y_sample, jnp.stack(conv_p, axis=0), new_conv_s, new_v_s)
```

```python
import functools
import math

import jax
import jax.numpy as jnp
from jax import lax
from jax.experimental import pallas as pl
from jax.experimental.pallas import tpu as pltpu
from jax.experimental.pallas import tpu_sc as plsc

F32 = jnp.float32
BF16 = jnp.bfloat16

D_MODEL = 1024
C_CONV = 512
C_SGU = 512
CONV_GROUPS = 8
CONV_WIDTH = 31
CONV_STATE = CONV_WIDTH - 1
SGU_HEADS = 4
SGU_HEAD_DIM = C_SGU // SGU_HEADS
CHUNK = 128
N_EXPERTS = 32
TOP_K = 4
D_FF = 1024
SWIGLU_LIMIT = 7.0
SWIGLU_ALPHA = 1.702
N_MOD = 6
EPS = 1e-5
LOG2_E = 1.0 / math.log(2.0)

ROW_TILE = 512
ROUTER_TILE = 1536
PREFIX_BLOCK = 256
ADALN_COLS = 1024
ROW_CHUNK = 128
CONV_CHUNK = 32
COMBINE_CHUNK = 64
CARRY_ROWS = 32
SUBLANES = 8
LANES = 128
PLAN_ROWS = 8
PLAN_LANES = 256
PLAN_EXPERT, PLAN_VALID, PLAN_FIRST, PLAN_SLOT, PLAN_NEXT = range(5)
GEMM_TILE = 512
FF_BLOCK = 256
TILES_PER_STEP = 4
VMEM_LIMIT = 56 * 1024 * 1024
D_WORDS = D_MODEL // 2
DISPATCH_CHUNK = 88
GATHER_CHUNK = 48
HI_MASK = -65536
SC_CORES = 2
SC_SUBCORES = 16


def _rms(x):
    return x * lax.rsqrt(jnp.mean(x * x, axis=-1, keepdims=True) + EPS)


def _gelu(x):
    return 0.5 * x * (1.0 + lax.erf(x * (1.0 / math.sqrt(2.0))))


def _sigmoid(x, scale=1.0):
    return 1.0 / (1.0 + jnp.exp2(x * (-scale * LOG2_E)))


def _split_bf16(x):
    hi = x.astype(BF16)
    lo = (x - hi.astype(F32)).astype(BF16)
    return hi, lo


def _pack_bf16_pair(x):
    bits = lax.bitcast_convert_type(x.astype(BF16).astype(F32), jnp.int32)
    return lax.shift_right_logical(bits[:, :D_WORDS], 16) | (bits[:, D_WORDS:] & HI_MASK)


def _unpack_bf16_pair(w):
    left = lax.bitcast_convert_type(lax.shift_left(w, 16), F32)
    right = lax.bitcast_convert_type(w & HI_MASK, F32)
    return left, right


def _dot(a, b):
    return jnp.dot(a, b, preferred_element_type=F32)


def _dot_nt(a, b):
    return lax.dot_general(a, b, (((1,), (1,)), ((), ())), preferred_element_type=F32)


def _row_loop(n_rows, body, chunk=None):
    chunk = ROW_CHUNK if chunk is None else chunk

    def step(i, carry):
        body(pl.multiple_of(i * chunk, chunk))
        return carry
    lax.fori_loop(0, n_rows // chunk, step, 0)


def _conv_tap(wconv_ref, k):
    return jnp.concatenate([wconv_ref[k]] * (CONV_CHUNK // SUBLANES), axis=0)


def _static_loop(n_rows, body, chunk=None):
    chunk = ROW_CHUNK if chunk is None else chunk
    for r0 in range(0, n_rows, chunk):
        body(r0)


def _adaln_kernel(c_ref, w_ref, b_ref, o_ref):
    c = c_ref[...]
    s_hi, s_lo = _split_bf16(c * _sigmoid(c))
    w_hi, w_lo = _split_bf16(w_ref[0])
    acc = _dot(s_hi, w_hi) + _dot(s_hi, w_lo) + _dot(s_lo, w_hi)
    o_ref[0] = acc + b_ref[0]


def _adaln(c_all, w_ada, b_ada):
    depth, _, n_out = w_ada.shape
    n_rows = c_all.shape[0]
    tn = ADALN_COLS
    return pl.pallas_call(
        _adaln_kernel,
        grid=(depth, n_out // tn),
        in_specs=[
            pl.BlockSpec((n_rows, D_MODEL), lambda l, j: (0, 0)),
            pl.BlockSpec((1, D_MODEL, tn), lambda l, j: (l, 0, j)),
            pl.BlockSpec((1, 1, tn), lambda l, j: (l, 0, j)),
        ],
        out_specs=pl.BlockSpec((1, n_rows, tn), lambda l, j: (l, 0, j)),
        out_shape=jax.ShapeDtypeStruct((depth, n_rows, n_out), F32),
        compiler_params=pltpu.CompilerParams(
            dimension_semantics=("arbitrary", "arbitrary"), vmem_limit_bytes=VMEM_LIMIT),
        name="adaln",
    )(c_all, w_ada, b_ada.reshape(depth, 1, n_out))


def _phase_in_norm(x_ref, h_scr, mod, n_rows, loop=_row_loop):
    def body(r0):
        r = pl.ds(r0, ROW_CHUNK)
        h = _rms(x_ref[r, :]) * (1.0 + mod(1, r0)) + mod(0, r0)
        h_scr[r, :] = h.astype(BF16)
    loop(n_rows, body)


def _phase_glu(z_scr, a_dst, a_off, n_rows, loop=_row_loop):
    def body(r0):
        r = pl.ds(r0, ROW_CHUNK)
        a_dst[pl.ds(r0 + a_off, ROW_CHUNK), :] = z_scr[r, 0:C_CONV] * _sigmoid(z_scr[r, C_CONV:2 * C_CONV])
    loop(n_rows, body)


def _phase_uv(z_scr, vec_ref, u_scr, v_scr, vout_ref, n_rows, loop=_row_loop):
    ln_g = vec_ref[3:4, :]
    ln_b = vec_ref[4:5, :]

    def body(r0):
        r = pl.ds(r0, ROW_CHUNK)
        u_scr[r, :] = _gelu(z_scr[r, 2 * C_CONV:2 * C_CONV + C_SGU])
        gv = _gelu(z_scr[r, 2 * C_CONV + C_SGU:2 * C_CONV + 2 * C_SGU])
        mu = jnp.mean(gv, axis=-1, keepdims=True)
        dv = gv - mu
        var = jnp.mean(dv * dv, axis=-1, keepdims=True)
        v = dv * lax.rsqrt(var + EPS) * ln_g + ln_b
        if vout_ref is not None:
            vout_ref[r, :] = v
        v_scr[r, :] = v.astype(v_scr.dtype)
    loop(n_rows, body)


def _phase_group_norm(conv_scr, convb_scr, stat_scr, g_ref, vec_ref, y_scr, n_rows, loop=_row_loop):
    stat_scr[...] = _dot(convb_scr[...], g_ref[...])

    def center(r0):
        r = pl.ds(r0, ROW_CHUNK)
        d = conv_scr[r, :] - stat_scr[r, :]
        conv_scr[r, :] = d
        convb_scr[r, :] = (d * d).astype(BF16)
    loop(n_rows, center)
    stat_scr[...] = _dot(convb_scr[...], g_ref[...])
    gn_g = vec_ref[1:2, :]
    gn_b = vec_ref[2:3, :]
    beta_a = vec_ref[5:6, :]

    def finish(r0):
        r = pl.ds(r0, ROW_CHUNK)
        gn = conv_scr[r, :] * lax.rsqrt(stat_scr[r, :] + EPS) * gn_g + gn_b
        ya = gn * _sigmoid(gn)
        y_scr[r, 0:C_CONV] = (_rms(ya) * beta_a).astype(BF16)
    loop(n_rows, finish)


def _phase_out(x_ref, y_scr, wout_ref, z_scr, wr_ref, br_ref, mod, h_scr,
               xmid_ref, h2_ref, logit_ref, n_rows, loop=_row_loop):
    z_scr[:, 0:D_MODEL] = (_dot(y_scr[:, 0:C_CONV], wout_ref[0:C_CONV, :])
                           + _dot(y_scr[:, C_CONV:C_CONV + C_SGU], wout_ref[C_CONV:C_CONV + C_SGU, :]))

    def body(r0):
        r = pl.ds(r0, ROW_CHUNK)
        xm = x_ref[r, :] + mod(2, r0) * z_scr[r, 0:D_MODEL]
        xmid_ref[r, :] = xm
        h2 = _rms(xm) * (1.0 + mod(4, r0)) + mod(3, r0)
        hi, lo = _split_bf16(h2)
        h2_ref[r, :] = _pack_bf16_pair(h2)
        y_scr[r, :] = hi
        h_scr[r, :] = lo
    loop(n_rows, body)
    w_hi, w_lo = _split_bf16(wr_ref[...])
    both = _dot_nt(jnp.concatenate([w_hi, w_lo], axis=0), y_scr[...])
    logit_ref[...] = (both[0:N_EXPERTS] + both[N_EXPERTS:2 * N_EXPERTS]
                      + _dot_nt(w_hi, h_scr[...])) + br_ref[...]


def _moe_half(yg_ref, w, r, hi_half):
    moe = jnp.zeros((COMBINE_CHUNK, D_WORDS), F32)
    for k in range(TOP_K):
        word = yg_ref[k, r, :]
        bits = (word & HI_MASK) if hi_half else lax.shift_left(word, 16)
        moe = moe + w[:, k:k + 1] * lax.bitcast_convert_type(bits, F32)
    return moe


def _phase_moe_residual(xmid_ref, yg_ref, wgt_ref, gate, x_dst, n_rows, loop=_row_loop):
    def body(r0):
        r = pl.ds(r0, COMBINE_CHUNK)
        w = wgt_ref[r, :]
        g = gate(r0)
        for hi_half in (False, True):
            cols = slice(D_WORDS, D_MODEL) if hi_half else slice(0, D_WORDS)
            x_dst[r, cols] = xmid_ref[r, cols] + g[:, cols] * _moe_half(yg_ref, w, r, hi_half)
    loop(n_rows, body, COMBINE_CHUNK)


def _mixer_prompt_kernel(*refs, tiles_per_seq, n_prompt_tiles, fused):
    x_ref, mod_ref = refs[:2]
    refs = refs[2:]
    if fused:
        yg_ref, wgt_ref, gate_ref = refs[:3]
        refs = refs[3:]
        x_scr = refs[-1]
        refs = refs[:-1]
    (win_ref, wout_ref, wconv_ref, vec_ref, g_ref, ws_ref, bs_ref, wr_ref, br_ref,
     smid_ref, sh2_ref, slogit_ref, xmid_ref, h2_ref, logit_ref, cst_ref,
     h_scr, z_scr, aext_scr, conv_scr, convb_scr, stat_scr, u_scr, v_scr, y_scr) = refs
    i = pl.program_id(0)

    @pl.when(i == n_prompt_tiles)
    def _():
        xmid_ref[...] = smid_ref[...]
        h2_ref[...] = sh2_ref[...]
        logit_ref[...] = slogit_ref[...]

    @pl.when(i < n_prompt_tiles)
    def _():
        x_src = x_ref
        if fused:
            _phase_moe_residual(x_ref, yg_ref, wgt_ref, lambda r0: gate_ref[0], x_scr, ROW_TILE,
                                _static_loop)
            x_src = x_scr
        _mixer_prompt_tile(lax.rem(i, tiles_per_seq), tiles_per_seq,
                           x_src, mod_ref, win_ref, wout_ref, wconv_ref, vec_ref, g_ref,
                           ws_ref, bs_ref, wr_ref, br_ref, xmid_ref, h2_ref, logit_ref, cst_ref,
                           h_scr, z_scr, aext_scr, conv_scr, convb_scr, stat_scr, u_scr, v_scr, y_scr)


def _mixer_prompt_tile(j, tiles_per_seq, x_ref, mod_ref, win_ref, wout_ref, wconv_ref, vec_ref, g_ref,
                       ws_ref, bs_ref, wr_ref, br_ref, xmid_ref, h2_ref, logit_ref, cst_ref,
                       h_scr, z_scr, aext_scr, conv_scr, convb_scr, stat_scr, u_scr, v_scr, y_scr):
    n_rows = ROW_TILE

    def mod(idx, r0):
        del r0
        return mod_ref[0, :, idx * D_MODEL:(idx + 1) * D_MODEL]

    a_buf = aext_scr.at[0]

    @pl.when(j == 0)
    def _():
        a_buf[0:CARRY_ROWS, :] = jnp.zeros((CARRY_ROWS, C_CONV), F32)

    loop = _static_loop
    _phase_in_norm(x_ref, h_scr, mod, n_rows, loop)
    n_glu = 2 * C_CONV
    z_scr[:, 0:n_glu] = _dot(h_scr[...], win_ref[:, 0:n_glu])
    _phase_glu(z_scr, a_buf, CARRY_ROWS, n_rows, loop)
    z_scr[:, n_glu:] = _dot(h_scr[...], win_ref[:, n_glu:])
    n_shift = n_rows + CARRY_ROWS - SUBLANES
    for s in range(1, SUBLANES):
        aext_scr[s, 0:n_shift, :] = a_buf[s:s + n_shift, :]

    b_conv = vec_ref[0:1, :]
    lead = CARRY_ROWS - CONV_STATE
    for r0 in range(0, n_rows, CONV_CHUNK):
        acc = jnp.zeros((CONV_CHUNK, C_CONV), F32)
        for k in range(CONV_WIDTH):
            s = (lead + k) % SUBLANES
            q0 = r0 + (lead + k - s)
            acc = acc + _conv_tap(wconv_ref, k) * aext_scr[s, q0:q0 + CONV_CHUNK, :]
        conv = acc + b_conv
        conv_scr[r0:r0 + CONV_CHUNK, :] = conv
        convb_scr[r0:r0 + CONV_CHUNK, :] = conv.astype(BF16)
    _phase_uv(z_scr, vec_ref, u_scr, v_scr, None, n_rows, loop)
    _phase_group_norm(conv_scr, convb_scr, stat_scr, g_ref, vec_ref, y_scr, n_rows, loop)

    row_i = lax.broadcasted_iota(jnp.int32, (CHUNK, CHUNK), 0)
    col_i = lax.broadcasted_iota(jnp.int32, (CHUNK, CHUNK), 1)
    tril = (col_i <= row_i).astype(F32)
    for hd in range(SGU_HEADS):
        ws_h = (ws_ref[hd] * tril).astype(BF16)
        cols = slice(hd * SGU_HEAD_DIM, (hd + 1) * SGU_HEAD_DIM)
        s_cols = slice(n_glu + hd * SGU_HEAD_DIM, n_glu + (hd + 1) * SGU_HEAD_DIM)
        for c in range(n_rows // CHUNK):
            rows = slice(c * CHUNK, (c + 1) * CHUNK)
            z_scr[rows, s_cols] = _dot(ws_h, v_scr[rows, cols])
    beta_b = vec_ref[6:7, :]

    def gate_body(r0):
        r = pl.ds(r0, ROW_CHUNK)
        rb = pl.ds(r0 % CHUNK, ROW_CHUNK)
        yb = u_scr[r, :] * (z_scr[r, n_glu:n_glu + C_SGU] + bs_ref[rb, :])
        y_scr[r, C_CONV:C_CONV + C_SGU] = (_rms(yb) * beta_b).astype(BF16)
    loop(n_rows, gate_body)

    _phase_out(x_ref, y_scr, wout_ref, z_scr, wr_ref, br_ref, mod, h_scr,
               xmid_ref, h2_ref, logit_ref, n_rows, loop)

    @pl.when(j == tiles_per_seq - 1)
    def _():
        cst_ref[0] = a_buf[n_rows + lead:n_rows + CARRY_ROWS, :]

    a_buf[0:CARRY_ROWS, :] = a_buf[n_rows:n_rows + CARRY_ROWS, :]


def _mixer_sample_kernel(*refs, fused):
    x_ref, mod_ref = refs[:2]
    refs = refs[2:]
    if fused:
        yg_ref, wgt_ref, gate_ref = refs[:3]
        refs = refs[3:]
        x_scr = refs[-1]
        refs = refs[:-1]
    (win_ref, wout_ref, wconv_ref, vec_ref, g_ref, wsv_ref, bsv_ref, wr_ref, br_ref, state_ref,
     xmid_ref, h2_ref, logit_ref, cnew_ref, vout_ref,
     h_scr, z_scr, conv_scr, convb_scr, stat_scr, u_scr, v_scr, y_scr, anew_ref) = refs
    n_rows = x_ref.shape[0]
    n_seq = state_ref.shape[1]
    n_pos = n_rows // n_seq
    loop = _static_loop

    def mod(idx, r0):
        return mod_ref[pl.ds(r0 % n_seq, ROW_CHUNK), idx * D_MODEL:(idx + 1) * D_MODEL]

    if fused:
        def gate(r0):
            return gate_ref[pl.ds(r0 % n_seq, COMBINE_CHUNK), :]
        _phase_moe_residual(x_ref, yg_ref, wgt_ref, gate, x_scr, n_rows, loop)
        x_ref = x_scr

    _phase_in_norm(x_ref, h_scr, mod, n_rows, loop)
    z_scr[...] = _dot(h_scr[...], win_ref[...])
    _phase_glu(z_scr, anew_ref, 0, n_rows, loop)
    _phase_uv(z_scr, vec_ref, u_scr, v_scr, vout_ref, n_rows, loop)

    keep = CONV_STATE - n_pos
    cnew_ref[0:keep] = state_ref[n_pos:CONV_STATE]
    for t in range(n_pos):
        cnew_ref[keep + t] = anew_ref[t * n_seq:(t + 1) * n_seq, :]

    b_conv = vec_ref[0:1, :]

    def conv_body(s0):
        rs = pl.ds(s0, CONV_CHUNK)
        for t in range(n_pos):
            acc = jnp.zeros((CONV_CHUNK, C_CONV), F32)
            for k in range(CONV_WIDTH):
                p = t + k
                if p < CONV_STATE:
                    src = state_ref[p, rs, :]
                else:
                    src = anew_ref[pl.ds(s0 + (p - CONV_STATE) * n_seq, CONV_CHUNK), :]
                acc = acc + _conv_tap(wconv_ref, k) * src
            conv = acc + b_conv
            r = pl.ds(s0 + t * n_seq, CONV_CHUNK)
            conv_scr[r, :] = conv
            convb_scr[r, :] = conv.astype(BF16)
    loop(n_seq, conv_body, CONV_CHUNK)

    _phase_group_norm(conv_scr, convb_scr, stat_scr, g_ref, vec_ref, y_scr, n_rows, loop)

    beta_b = vec_ref[6:7, :]

    def gate_body(s0):
        for t in range(n_pos):
            s = jnp.zeros((CONV_CHUNK, C_SGU), F32)
            for jj in range(t + 1):
                s = s + wsv_ref[t * n_pos + jj:t * n_pos + jj + 1, :] * v_scr[pl.ds(s0 + jj * n_seq, CONV_CHUNK), :]
            r = pl.ds(s0 + t * n_seq, CONV_CHUNK)
            yb = u_scr[r, :] * (s + bsv_ref[t:t + 1, :])
            y_scr[r, C_CONV:C_CONV + C_SGU] = (_rms(yb) * beta_b).astype(BF16)
    loop(n_seq, gate_body, CONV_CHUNK)

    _phase_out(x_ref, y_scr, wout_ref, z_scr, wr_ref, br_ref, mod, h_scr,
               xmid_ref, h2_ref, logit_ref, n_rows, loop)


def _const_spec(shape):
    nd = len(shape)
    return pl.BlockSpec(shape, lambda *_: (0,) * nd, pipeline_mode=pl.Buffered(1))


def _const_out_spec(shape):
    nd = len(shape)
    return pl.BlockSpec(shape, lambda *_: (0,) * nd)


def _mixer(x_p, x_s, x_s_block, moe, mod_p, mod_s, w_in_b, w_out_b, wconv_p, vec, g_mat, w_s, bs_full,
           wsv, bsv, w_router, b_router, state_t, n_prompt_seq, seq_len):
    fused = moe is not None
    n_prompt = n_prompt_seq * seq_len
    n_seq, n_pos = state_t.shape[1], bsv.shape[0]
    n_sample = n_seq * n_pos
    t_all = n_prompt + n_sample
    nj = seq_len // ROW_TILE
    weight_specs = [
        _const_spec((D_MODEL, 2 * C_CONV + 2 * C_SGU)),
        _const_spec((D_MODEL, D_MODEL)),
        _const_spec((CARRY_ROWS, SUBLANES, C_CONV)),
        _const_spec((SUBLANES, C_CONV)),
        _const_spec((C_CONV, C_CONV)),
    ]
    router_specs = [_const_spec((N_EXPERTS, D_MODEL)), _const_spec((N_EXPERTS, 1))]
    common_scratch = lambda n: [
        pltpu.VMEM((n, D_MODEL), BF16),
        pltpu.VMEM((n, 2 * C_CONV + 2 * C_SGU), F32),
    ]
    tail_scratch = lambda n, vdt: [
        pltpu.VMEM((n, C_CONV), F32),
        pltpu.VMEM((n, C_CONV), BF16),
        pltpu.VMEM((n, C_CONV), F32),
        pltpu.VMEM((n, C_SGU), F32),
        pltpu.VMEM((n, C_SGU), vdt),
        pltpu.VMEM((n, D_MODEL), BF16),
    ]
    assert n_sample == ROW_TILE
    n_tiles_p = n_prompt // ROW_TILE
    tile_p = lambda i: jnp.minimum(i, n_tiles_p - 1)
    seq_of = lambda i: jnp.minimum(i // nj, n_prompt_seq - 1)
    moe_s_specs, moe_p_specs, moe_s_args, moe_p_args, x_scratch = [], [], [], [], []
    if fused:
        yg, wgt, gate_p, gate_s = moe
        moe_s_specs = [pl.BlockSpec((TOP_K, n_sample, D_WORDS), lambda i: (0, x_s_block, 0)),
                       pl.BlockSpec((n_sample, LANES), lambda i: (x_s_block, 0)),
                       _const_spec((n_seq, D_MODEL))]
        moe_p_specs = [pl.BlockSpec((TOP_K, ROW_TILE, D_WORDS), lambda i: (0, tile_p(i), 0)),
                       pl.BlockSpec((ROW_TILE, LANES), lambda i: (tile_p(i), 0)),
                       pl.BlockSpec((1, 1, D_MODEL), lambda i: (seq_of(tile_p(i)), 0, 0))]
        moe_s_args = [yg, wgt, gate_s]
        moe_p_args = [yg, wgt, gate_p.reshape(n_prompt_seq, 1, D_MODEL)]
        x_scratch = [pltpu.VMEM((ROW_TILE, D_MODEL), F32)]
    smid, sh2, slogits, a_new, v_new = pl.pallas_call(
        functools.partial(_mixer_sample_kernel, fused=fused),
        grid=(1,),
        in_specs=[
            pl.BlockSpec((n_sample, D_MODEL), lambda i: (x_s_block, 0)),
            _const_spec((n_seq, N_MOD * D_MODEL)),
            *moe_s_specs,
            *weight_specs,
            _const_spec((n_pos * n_pos, C_SGU)),
            _const_spec((n_pos, C_SGU)),
            *router_specs,
            _const_spec((CONV_STATE, n_seq, C_CONV)),
        ],
        out_specs=(
            _const_out_spec((n_sample, D_MODEL)),
            _const_out_spec((n_sample, D_WORDS)),
            _const_out_spec((N_EXPERTS, n_sample)),
            _const_out_spec((CONV_STATE, n_seq, C_CONV)),
            _const_out_spec((n_sample, C_SGU)),
        ),
        out_shape=(
            jax.ShapeDtypeStruct((n_sample, D_MODEL), F32),
            jax.ShapeDtypeStruct((n_sample, D_WORDS), jnp.int32),
            jax.ShapeDtypeStruct((N_EXPERTS, n_sample), F32),
            jax.ShapeDtypeStruct((CONV_STATE, n_seq, C_CONV), F32),
            jax.ShapeDtypeStruct((n_sample, C_SGU), F32),
        ),
        scratch_shapes=common_scratch(n_sample) + tail_scratch(n_sample, F32)
        + [pltpu.VMEM((n_sample, C_CONV), F32)] + x_scratch,
        compiler_params=pltpu.CompilerParams(
            dimension_semantics=("arbitrary",), vmem_limit_bytes=VMEM_LIMIT),
        name="mixer_sample",
    )(x_s, mod_s, *moe_s_args, w_in_b, w_out_b, wconv_p, vec, g_mat, wsv, bsv, w_router, b_router,
      state_t)

    xmid, h2, logits, cst = pl.pallas_call(
        functools.partial(_mixer_prompt_kernel, tiles_per_seq=nj, n_prompt_tiles=n_tiles_p,
                          fused=fused),
        grid=(n_tiles_p + 1,),
        in_specs=[
            pl.BlockSpec((ROW_TILE, D_MODEL), lambda i: (tile_p(i), 0)),
            pl.BlockSpec((1, 1, N_MOD * D_MODEL), lambda i: (seq_of(i), 0, 0)),
            *moe_p_specs,
            *weight_specs,
            _const_spec((SGU_HEADS, CHUNK, CHUNK)),
            _const_spec((CHUNK, C_SGU)),
            *router_specs,
            _const_spec((n_sample, D_MODEL)),
            _const_spec((n_sample, D_WORDS)),
            _const_spec((N_EXPERTS, n_sample)),
        ],
        out_specs=(
            pl.BlockSpec((ROW_TILE, D_MODEL), lambda i: (i, 0)),
            pl.BlockSpec((ROW_TILE, D_WORDS), lambda i: (i, 0)),
            pl.BlockSpec((N_EXPERTS, ROW_TILE), lambda i: (0, i)),
            pl.BlockSpec((1, CONV_STATE, C_CONV), lambda i: (seq_of(i), 0, 0)),
        ),
        out_shape=(
            jax.ShapeDtypeStruct((t_all, D_MODEL), F32),
            jax.ShapeDtypeStruct((t_all, D_WORDS), jnp.int32),
            jax.ShapeDtypeStruct((N_EXPERTS, t_all), F32),
            jax.ShapeDtypeStruct((n_prompt_seq, CONV_STATE, C_CONV), F32),
        ),
        scratch_shapes=common_scratch(ROW_TILE)
        + [pltpu.VMEM((SUBLANES, CARRY_ROWS + ROW_TILE, C_CONV), F32)]
        + tail_scratch(ROW_TILE, BF16) + x_scratch,
        compiler_params=pltpu.CompilerParams(
            dimension_semantics=("arbitrary",), vmem_limit_bytes=VMEM_LIMIT),
        name="mixer_prompt",
    )(x_p, mod_p, *moe_p_args, w_in_b, w_out_b, wconv_p, vec, g_mat, w_s, bs_full, w_router, b_router,
      smid, sh2, slogits)
    return xmid, h2, logits, cst, a_new, v_new


def _tile_plan(cnt):
    e_sub = lax.broadcasted_iota(jnp.int32, (N_EXPERTS, N_EXPERTS), 0)
    e_lane = lax.broadcasted_iota(jnp.int32, (N_EXPERTS, N_EXPERTS), 1)
    tiles = jnp.floor((cnt + (GEMM_TILE - 1.0)) * (1.0 / GEMM_TILE))
    active = jnp.where(cnt > 0.0, 1.0, 0.0)

    def over_experts(tri, col):
        return _dot(tri.astype(BF16), jnp.broadcast_to(col, (N_EXPERTS, LANES)).astype(BF16))[:, 0:1]
    tile_end = over_experts(jnp.where(e_lane <= e_sub, 1.0, 0.0), tiles)
    tile_start = tile_end - tiles
    ordinal = over_experts(jnp.where(e_lane < e_sub, 1.0, 0.0), active)

    g = lax.broadcasted_iota(jnp.int32, (1, PLAN_LANES), 1).astype(F32)
    e_id = lax.broadcasted_iota(jnp.int32, (N_EXPERTS, PLAN_LANES), 0).astype(F32)
    done = jnp.sum(jnp.where(tile_end <= g, 1.0, 0.0), axis=0, keepdims=True)
    live = done < float(N_EXPERTS)
    last_e = jnp.max(jnp.where(cnt > 0.0, e_id[:, 0:1], 0.0), axis=0, keepdims=True)
    te = jnp.where(live, done, last_e)
    pick = e_id == te

    def at_tile(col):
        return jnp.sum(jnp.where(pick, col, 0.0), axis=0, keepdims=True)
    start_g = at_tile(tile_start)
    ord_g = at_tile(ordinal)
    valid = jnp.clip(at_tile(cnt) - (g - start_g) * GEMM_TILE, 0.0, float(GEMM_TILE))
    valid = jnp.where(live, valid, 0.0)
    first = jnp.where(live, jnp.where(g == start_g, 1.0, 0.0), 0.0)
    slot = ord_g - 2.0 * jnp.floor(ord_g * 0.5)
    later = jnp.where(e_id > te, jnp.where(cnt > 0.0, e_id, float(N_EXPERTS)), float(N_EXPERTS))
    nxt = jnp.min(later, axis=0, keepdims=True)
    nxt = jnp.where(nxt >= float(N_EXPERTS), -1.0, nxt)

    row = lax.broadcasted_iota(jnp.int32, (PLAN_ROWS, PLAN_LANES), 0)
    plan = jnp.zeros((PLAN_ROWS, PLAN_LANES), F32)
    for r, v in ((PLAN_EXPERT, te), (PLAN_VALID, valid), (PLAN_FIRST, first), (PLAN_SLOT, slot),
                 (PLAN_NEXT, nxt)):
        plan = jnp.where(row == r, v, plan)
    return plan.astype(jnp.int32), tile_start * GEMM_TILE


def _router_kernel(lg_ref, wgt_ref, pos_ref, plan_ref, cnt_scr, run_scr, start_scr, earlier_scr):
    ph = pl.program_id(0)
    i = pl.program_id(1)
    n = lg_ref.shape[1]

    @pl.when(jnp.logical_and(ph == 0, i == 0))
    def _():
        cnt_scr[...] = jnp.zeros_like(cnt_scr)

    lg = lg_ref[...]
    sub = lax.broadcasted_iota(jnp.int32, lg.shape, 0).astype(F32)
    vals, sels = [], []
    for _ in range(TOP_K):
        m = jnp.max(lg, axis=0, keepdims=True)
        idx = jnp.min(jnp.where(lg == m, sub, float(N_EXPERTS)), axis=0, keepdims=True)
        sel = sub == idx
        vals.append(m)
        sels.append(sel)
        lg = jnp.where(sel, -jnp.inf, lg)
    onehot = jnp.zeros(lg.shape, F32)
    for sel in sels:
        onehot = onehot + jnp.where(sel, 1.0, 0.0)
    tile_cnt = jnp.sum(onehot, axis=1, keepdims=True)

    @pl.when(ph == 0)
    def _():
        cnt_scr[...] = cnt_scr[...] + tile_cnt

    @pl.when(jnp.logical_and(ph == 1, i == 0))
    def _():
        plan, row_start = _tile_plan(cnt_scr[...])
        plan_ref[...] = plan
        start_scr[...] = row_start
        run_scr[...] = jnp.zeros_like(run_scr)
        t_row = lax.broadcasted_iota(jnp.int32, (PREFIX_BLOCK, PREFIX_BLOCK), 0)
        t_col = lax.broadcasted_iota(jnp.int32, (PREFIX_BLOCK, PREFIX_BLOCK), 1)
        earlier_scr[...] = jnp.where(t_row < t_col, 1.0, 0.0).astype(BF16)

    @pl.when(ph == 1)
    def _():
        exps = [jnp.exp(v - vals[0]) for v in vals]
        inv = 1.0 / (exps[0] + exps[1] + exps[2] + exps[3])
        w_row = lax.broadcasted_iota(jnp.int32, (LANES, n), 0)
        w_all = jnp.zeros((LANES, n), F32)
        for k in range(TOP_K):
            w_all = jnp.where(w_row == k, exps[k] * inv, w_all)
        wgt_ref[...] = w_all.T
        base = run_scr[...] + start_scr[...]
        onehot_b = onehot.astype(BF16)
        for b0 in range(0, n, PREFIX_BLOCK):
            blk = slice(b0, b0 + PREFIX_BLOCK)
            slot0 = _dot(onehot_b[:, blk], earlier_scr[...]) + base
            for k in range(TOP_K):
                pos_k = jnp.sum(jnp.where(sels[k][:, blk], slot0, 0.0), axis=0, keepdims=True)
                pos_ref[k:k + 1, blk] = pos_k.astype(jnp.int32)
            base = base + jnp.sum(onehot[:, blk], axis=1, keepdims=True)
        run_scr[...] = run_scr[...] + tile_cnt


def _router(logits):
    t_all = logits.shape[1]
    col = pltpu.VMEM((N_EXPERTS, 1), F32)
    return pl.pallas_call(
        _router_kernel,
        grid=(2, t_all // ROUTER_TILE),
        in_specs=[pl.BlockSpec((N_EXPERTS, ROUTER_TILE), lambda ph, i: (0, i))],
        out_specs=(
            pl.BlockSpec((ROUTER_TILE, LANES), lambda ph, i: (i * ph, 0)),
            pl.BlockSpec((TOP_K, ROUTER_TILE), lambda ph, i: (0, i * ph)),
            _const_out_spec((PLAN_ROWS, PLAN_LANES)),
        ),
        out_shape=(
            jax.ShapeDtypeStruct((t_all, LANES), F32),
            jax.ShapeDtypeStruct((TOP_K, t_all), jnp.int32),
            jax.ShapeDtypeStruct((PLAN_ROWS, PLAN_LANES), jnp.int32),
        ),
        scratch_shapes=[col, col, col, pltpu.VMEM((PREFIX_BLOCK, PREFIX_BLOCK), BF16)],
        compiler_params=pltpu.CompilerParams(
            dimension_semantics=("arbitrary", "arbitrary"), vmem_limit_bytes=VMEM_LIMIT),
        name="router",
    )(logits)


def _expert_kernel(plan_ref, x_ref, w1_hbm, b1_ref, w2_hbm, b2_ref, o_ref,
                   w1f_scr, w2f_scr, w1b_scr, w2b_scr, sem, *, layer):
    def weight_copies(expert, slot):
        return (pltpu.make_async_copy(w1_hbm.at[layer, expert], w1f_scr.at[slot], sem.at[0, slot]),
                pltpu.make_async_copy(w2_hbm.at[layer, expert], w2f_scr.at[slot], sem.at[1, slot]))

    for sub in range(TILES_PER_STEP):
        _expert_tile(pl.program_id(0) * TILES_PER_STEP + sub, sub * GEMM_TILE, plan_ref, x_ref,
                     b1_ref, b2_ref, o_ref, w1f_scr, w2f_scr, w1b_scr, w2b_scr, weight_copies)


def _expert_tile(g, row0, plan_ref, x_ref, b1_ref, b2_ref, o_ref, w1f_scr, w2f_scr, w1b_scr, w2b_scr,
                 weight_copies):
    nv = plan_ref[PLAN_VALID, g]
    expert = plan_ref[PLAN_EXPERT, g]

    @pl.when(plan_ref[PLAN_FIRST, g] == 1)
    def _():
        slot = plan_ref[PLAN_SLOT, g]
        nxt = plan_ref[PLAN_NEXT, g]

        @pl.when(g == 0)
        def _():
            for cp in weight_copies(expert, slot):
                cp.start()
        for cp in weight_copies(expert, slot):
            cp.wait()

        @pl.when(nxt >= 0)
        def _():
            for cp in weight_copies(nxt, 1 - slot):
                cp.start()
        w1b_scr[...] = w1f_scr[slot].astype(BF16)
        w2b_scr[...] = w2f_scr[slot].astype(BF16)

    def ffn(m):
        x_left, x_right = _unpack_bf16_pair(x_ref[row0:row0 + m, :])
        x = jnp.concatenate([x_left.astype(BF16), x_right.astype(BF16)], axis=1)

        def hidden(col0):
            cols = slice(col0, col0 + FF_BLOCK)
            return _dot(x, w1b_scr[:, cols]) + b1_ref[expert, :, cols]

        acts = []
        for jb in range(D_FF // FF_BLOCK):
            gg = jnp.minimum(hidden(jb * FF_BLOCK), SWIGLU_LIMIT)
            up = jnp.clip(hidden(D_FF + jb * FF_BLOCK), -SWIGLU_LIMIT, SWIGLU_LIMIT)
            acts.append((gg * _sigmoid(gg, SWIGLU_ALPHA) * (up + 1.0)).astype(BF16))
        y = _dot(jnp.concatenate(acts, axis=1), w2b_scr[...]) + b2_ref[expert]
        row = lax.broadcasted_iota(jnp.int32, y.shape, 0)
        o_ref[row0:row0 + m, :] = _pack_bf16_pair(jnp.where(row < nv, y, 0.0))
        if m < GEMM_TILE:
            o_ref[row0 + m:row0 + GEMM_TILE, :] = jnp.zeros((GEMM_TILE - m, D_WORDS), jnp.int32)

    @pl.when(nv > GEMM_TILE // 2)
    def _():
        ffn(GEMM_TILE)

    @pl.when(jnp.logical_and(nv > 0, nv <= GEMM_TILE // 2))
    def _():
        ffn(GEMM_TILE // 2)

    @pl.when(nv == 0)
    def _():
        o_ref[row0:row0 + GEMM_TILE, :] = jnp.zeros((GEMM_TILE, D_WORDS), jnp.int32)


def _experts(xs, plan, w1, b1, w2, b2, layer):
    n_tiles = xs.shape[0] // GEMM_TILE
    depth = w1.shape[0]
    assert n_tiles % TILES_PER_STEP == 0
    step_rows = TILES_PER_STEP * GEMM_TILE
    bias_map = lambda s, plan: (layer, 0, 0, 0)
    grid_spec = pltpu.PrefetchScalarGridSpec(
        num_scalar_prefetch=1,
        grid=(n_tiles // TILES_PER_STEP,),
        in_specs=[
            pl.BlockSpec((step_rows, D_WORDS), lambda s, plan: (s, 0)),
            pl.BlockSpec(memory_space=pl.ANY),
            pl.BlockSpec((None, N_EXPERTS, 1, 2 * D_FF), bias_map, pipeline_mode=pl.Buffered(1)),
            pl.BlockSpec(memory_space=pl.ANY),
            pl.BlockSpec((None, N_EXPERTS, 1, D_MODEL), bias_map, pipeline_mode=pl.Buffered(1)),
        ],
        out_specs=pl.BlockSpec((step_rows, D_WORDS), lambda s, plan: (s, 0)),
        scratch_shapes=[
            pltpu.VMEM((2, D_MODEL, 2 * D_FF), F32),
            pltpu.VMEM((2, D_FF, D_MODEL), F32),
            pltpu.VMEM((D_MODEL, 2 * D_FF), BF16),
            pltpu.VMEM((D_FF, D_MODEL), BF16),
            pltpu.SemaphoreType.DMA((2, 2)),
        ],
    )
    return pl.pallas_call(
        functools.partial(_expert_kernel, layer=layer),
        grid_spec=grid_spec,
        out_shape=jax.ShapeDtypeStruct((n_tiles * GEMM_TILE, D_WORDS), jnp.int32),
        compiler_params=pltpu.CompilerParams(
            dimension_semantics=("arbitrary",), vmem_limit_bytes=VMEM_LIMIT),
        name="experts",
    )(plan, xs, w1, b1.reshape(depth, N_EXPERTS, 1, 2 * D_FF), w2,
      b2.reshape(depth, N_EXPERTS, 1, D_MODEL))


def _final_kernel(x_ref, yg_ref, wgt_ref, gp_ref, gs_ref, fg_ref, op_ref, os_ref, *, n_prompt_blocks):
    i = pl.program_id(0)
    n_seq = gs_ref.shape[0]

    def run(gate, o_ref):
        def body(r0):
            r = pl.ds(r0, COMBINE_CHUNK)
            w = wgt_ref[r, :]
            g = gate(r0)
            xl = x_ref[r, 0:D_WORDS] + g[:, 0:D_WORDS] * _moe_half(yg_ref, w, r, False)
            xr = x_ref[r, D_WORDS:D_MODEL] + g[:, D_WORDS:D_MODEL] * _moe_half(yg_ref, w, r, True)
            ms = (jnp.sum(xl * xl, axis=-1, keepdims=True)
                  + jnp.sum(xr * xr, axis=-1, keepdims=True)) * (1.0 / D_MODEL)
            scale = lax.rsqrt(ms + EPS)
            o_ref[r, 0:D_WORDS] = xl * scale * fg_ref[:, 0:D_WORDS]
            o_ref[r, D_WORDS:D_MODEL] = xr * scale * fg_ref[:, D_WORDS:D_MODEL]
        _row_loop(x_ref.shape[0], body, COMBINE_CHUNK)

    @pl.when(i < n_prompt_blocks)
    def _():
        run(lambda r0: gp_ref[0], op_ref)

    @pl.when(i >= n_prompt_blocks)
    def _():
        run(lambda r0: gs_ref[pl.ds(pl.multiple_of(lax.rem(r0, n_seq), COMBINE_CHUNK), COMBINE_CHUNK), :],
            os_ref)


def _final(xmid, yg, wgt, gate_p, gate_s, final_g, n_prompt, seq_len):
    t_all = xmid.shape[0]
    n_prompt_blocks = n_prompt // ROW_TILE
    per_seq = seq_len // ROW_TILE
    n_seq_p = gate_p.shape[0]
    assert t_all - n_prompt == ROW_TILE
    out_specs = (
        pl.BlockSpec((ROW_TILE, D_MODEL), lambda i: (jnp.minimum(i, n_prompt_blocks - 1), 0)),
        pl.BlockSpec((ROW_TILE, D_MODEL), lambda i: (0, 0)),
    )
    out_shape = (jax.ShapeDtypeStruct((n_prompt, D_MODEL), F32),
                 jax.ShapeDtypeStruct((ROW_TILE, D_MODEL), F32))
    return pl.pallas_call(
        functools.partial(_final_kernel, n_prompt_blocks=n_prompt_blocks),
        grid=(t_all // ROW_TILE,),
        in_specs=[
            pl.BlockSpec((ROW_TILE, D_MODEL), lambda i: (i, 0)),
            pl.BlockSpec((TOP_K, ROW_TILE, D_WORDS), lambda i: (0, i, 0)),
            pl.BlockSpec((ROW_TILE, LANES), lambda i: (i, 0)),
            pl.BlockSpec((1, 1, D_MODEL), lambda i: (jnp.minimum(i // per_seq, n_seq_p - 1), 0, 0)),
            _const_spec(gate_s.shape),
            _const_spec((1, D_MODEL)),
        ],
        out_specs=out_specs,
        out_shape=out_shape,
        compiler_params=pltpu.CompilerParams(
            dimension_semantics=("arbitrary",), vmem_limit_bytes=VMEM_LIMIT),
        name="final_combine",
    )(xmid, yg, wgt, gate_p.reshape(n_seq_p, 1, D_MODEL), gate_s, final_g.reshape(1, D_MODEL))


def _sc_mesh():
    return plsc.VectorSubcoreMesh(core_axis_name="core", subcore_axis_name="subcore",
                                  num_cores=SC_CORES, num_subcores=SC_SUBCORES)


def _sc_worker_id():
    return lax.axis_index("subcore") * SC_CORES + lax.axis_index("core")


def _sc_dispatch(h2w, pos_c, n_slots):
    _, n_chunks, _, chunk = pos_c.shape

    @functools.partial(
        pl.kernel, mesh=_sc_mesh(),
        out_type=jax.ShapeDtypeStruct((n_slots, D_WORDS), jnp.int32),
        scratch_types=[pltpu.VMEM((n_chunks, TOP_K, chunk), jnp.int32),
                       pltpu.VMEM((chunk, D_WORDS), jnp.int32),
                       pltpu.SemaphoreType.DMA],
        name="moe_dispatch")
    def run(h_hbm, pos_hbm, out_hbm, idx_v, rows_v, sem):
        wid = _sc_worker_id()
        pltpu.sync_copy(pos_hbm.at[wid], idx_v)

        @pl.loop(0, n_chunks)
        def _(c):
            base = pl.multiple_of((wid * n_chunks + c) * chunk, SUBLANES)
            pltpu.sync_copy(h_hbm.at[pl.ds(base, chunk)], rows_v)
            copies = [pltpu.async_copy(rows_v, out_hbm.at[idx_v.at[c, k]], sem) for k in range(TOP_K)]
            for cp in copies:
                cp.wait()

    return run(h2w, pos_c)


def _sc_gather_back(ys, pos_c, t_all):
    _, n_chunks, _, chunk = pos_c.shape

    @functools.partial(
        pl.kernel, mesh=_sc_mesh(),
        out_type=jax.ShapeDtypeStruct((TOP_K, t_all, D_WORDS), jnp.int32),
        scratch_types=[pltpu.VMEM((n_chunks, TOP_K, chunk), jnp.int32),
                       pltpu.VMEM((TOP_K, chunk, D_WORDS), jnp.int32),
                       pltpu.SemaphoreType.DMA,
                       pltpu.SemaphoreType.DMA],
        name="moe_gather_back")
    def run(ys_hbm, pos_hbm, out_hbm, idx_v, rows_v, sem_in, sem_out):
        wid = _sc_worker_id()
        pltpu.sync_copy(pos_hbm.at[wid], idx_v)

        @pl.loop(0, n_chunks)
        def _(c):
            base = pl.multiple_of((wid * n_chunks + c) * chunk, SUBLANES)
            gathers = [pltpu.async_copy(ys_hbm.at[idx_v.at[c, k]], rows_v.at[k], sem_in)
                       for k in range(TOP_K)]
            for cp in gathers:
                cp.wait()
            writes = [pltpu.async_copy(rows_v.at[k], out_hbm.at[k, pl.ds(base, chunk)], sem_out)
                      for k in range(TOP_K)]
            for cp in writes:
                cp.wait()

    return run(ys, pos_c)


def kernel(x_prompt, x_sample, state_conv, c_prompt, c_sample, w_ada, b_ada, w_in, w_conv, b_conv,
           gn_g, gn_b, sgu_ln_g, sgu_ln_b, w_s, b_s, beta_a, beta_b, w_out, w_router, b_router,
           w1, b1, w2, b2, final_g):
    n_bp, seq_len, _ = x_prompt.shape
    n_bs, n_pos, _ = x_sample.shape
    depth = w_ada.shape[0]
    n_prompt = n_bp * seq_len
    n_sample = n_bs * n_pos
    t_all = n_prompt + n_sample
    n_tiles = (t_all * TOP_K) // GEMM_TILE + N_EXPERTS
    n_workers = SC_CORES * SC_SUBCORES
    assert t_all % (n_workers * DISPATCH_CHUNK) == 0 and t_all % (n_workers * GATHER_CHUNK) == 0
    assert n_tiles <= PLAN_LANES and t_all % ROUTER_TILE == 0

    x_p = x_prompt.reshape(n_prompt, D_MODEL)
    x_s = x_sample.transpose(1, 0, 2).reshape(n_sample, D_MODEL)
    x_s_block = 0
    moe = None
    mods =_adaln(jnp.concatenate([c_prompt, c_sample], axis=0), w_ada, b_ada)
    state_t = state_conv.transpose(0, 2, 1, 3)
    grp = jnp.arange(C_CONV) // (C_CONV // CONV_GROUPS)
    g_mat = ((grp[:, None] == grp[None, :]).astype(F32) / (C_CONV // CONV_GROUPS)).astype(BF16)

    conv_p, conv_s, v_s = [], [], []
    for l in range(depth):
        mod_p = mods[l, :n_bp].reshape(n_bp, 1, N_MOD * D_MODEL)
        mod_s = mods[l, n_bp:]
        wconv_p = jnp.broadcast_to(
            jnp.pad(w_conv[l], ((0, CARRY_ROWS - CONV_WIDTH), (0, 0)))[:, None, :],
            (CARRY_ROWS, SUBLANES, C_CONV))
        zero = jnp.zeros_like(b_conv[l])
        vec = jnp.stack([b_conv[l], gn_g[l], gn_b[l], sgu_ln_g[l], sgu_ln_b[l],
                         beta_a[l], beta_b[l], zero], axis=0)
        bs_full = jnp.repeat(b_s[l].T, SGU_HEAD_DIM, axis=1)
        wsv = jnp.repeat(w_s[l][:, :n_pos, :n_pos].transpose(1, 2, 0).reshape(n_pos * n_pos, SGU_HEADS),
                         SGU_HEAD_DIM, axis=1)
        bsv = jnp.repeat(b_s[l][:, :n_pos].T, SGU_HEAD_DIM, axis=1)
        xmid, h2, logits, cst, a_new, v_new = _mixer(
            x_p, x_s, x_s_block, moe, mod_p, mod_s, w_in[l].astype(BF16), w_out[l].astype(BF16), wconv_p,
            vec, g_mat,
            w_s[l], bs_full, wsv, bsv, w_router[l].T, b_router[l].reshape(N_EXPERTS, 1),
            state_t[l], n_bp, seq_len)
        conv_p.append(cst)
        conv_s.append(a_new)
        v_s.append(v_new)

        wgt, pos, plan = _router(logits)
        def per_worker(chunk):
            return pos.reshape(TOP_K, n_workers, -1, chunk).transpose(1, 2, 0, 3)
        xs = _sc_dispatch(h2, per_worker(DISPATCH_CHUNK), n_tiles * GEMM_TILE)
        ys = _experts(xs, plan, w1, b1, w2, b2, l)
        yg = _sc_gather_back(ys, per_worker(GATHER_CHUNK), t_all)
        moe = (yg, wgt, mod_p[:, 0, 5 * D_MODEL:], mod_s[:, 5 * D_MODEL:])
        x_p = x_s = xmid
        x_s_block = n_prompt // n_sample

    y_p, y_s = _final(xmid, *moe, final_g, n_prompt, seq_len)
    y_prompt = y_p.reshape(n_bp, seq_len, D_MODEL)
    y_sample = y_s.reshape(n_pos, n_bs, D_MODEL).transpose(1, 0, 2)
    new_conv_s = jnp.stack(conv_s, axis=0).transpose(0, 2, 1, 3)
    new_v_s = jnp.stack(v_s, axis=0).reshape(depth, n_pos, n_bs, C_SGU).transpose(0, 2, 1, 3)
    return (y_prompt, y_sample, jnp.stack(conv_p, axis=0), new_conv_s, new_v_s)
```

```python
import functools
import math

import jax
import jax.numpy as jnp
from jax import lax
from jax.experimental import pallas as pl
from jax.experimental.pallas import tpu as pltpu
from jax.experimental.pallas import tpu_sc as plsc

F32 = jnp.float32
BF16 = jnp.bfloat16

D_MODEL = 1024
C_CONV = 512
C_SGU = 512
CONV_GROUPS = 8
CONV_WIDTH = 31
CONV_STATE = CONV_WIDTH - 1
SGU_HEADS = 4
SGU_HEAD_DIM = C_SGU // SGU_HEADS
CHUNK = 128
N_EXPERTS = 32
TOP_K = 4
D_FF = 1024
SWIGLU_LIMIT = 7.0
SWIGLU_ALPHA = 1.702
N_MOD = 6
EPS = 1e-5
LOG2_E = 1.0 / math.log(2.0)

ROW_TILE = 512
ROUTER_TILE = 1536
PREFIX_BLOCK = 256
ADALN_COLS = 1024
ROW_CHUNK = 128
CONV_CHUNK = 32
COMBINE_CHUNK = 64
CARRY_ROWS = 32
SUBLANES = 8
LANES = 128
PLAN_ROWS = 8
PLAN_LANES = 256
PLAN_EXPERT, PLAN_VALID, PLAN_FIRST, PLAN_SLOT, PLAN_NEXT = range(5)
GEMM_TILE = 512
FF_BLOCK = 256
TILES_PER_STEP = 4
VMEM_LIMIT = 56 * 1024 * 1024
D_WORDS = D_MODEL // 2
DISPATCH_CHUNK = 88
GATHER_CHUNK = 48
HI_MASK = -65536
SC_CORES = 2
SC_SUBCORES = 16


def _rms(x):
    return x * lax.rsqrt(jnp.mean(x * x, axis=-1, keepdims=True) + EPS)


def _gelu(x):
    return 0.5 * x * (1.0 + lax.erf(x * (1.0 / math.sqrt(2.0))))


def _sigmoid(x, scale=1.0):
    return 1.0 / (1.0 + jnp.exp2(x * (-scale * LOG2_E)))


def _split_bf16(x):
    hi = x.astype(BF16)
    lo = (x - hi.astype(F32)).astype(BF16)
    return hi, lo


def _pack_bf16_pair(x):
    bits = lax.bitcast_convert_type(x.astype(BF16).astype(F32), jnp.int32)
    return lax.shift_right_logical(bits[:, :D_WORDS], 16) | (bits[:, D_WORDS:] & HI_MASK)


def _unpack_bf16_pair(w):
    left = lax.bitcast_convert_type(lax.shift_left(w, 16), F32)
    right = lax.bitcast_convert_type(w & HI_MASK, F32)
    return left, right


def _dot(a, b):
    return jnp.dot(a, b, preferred_element_type=F32)


def _dot_nt(a, b):
    return lax.dot_general(a, b, (((1,), (1,)), ((), ())), preferred_element_type=F32)


def _row_loop(n_rows, body, chunk=None):
    chunk = ROW_CHUNK if chunk is None else chunk

    def step(i, carry):
        body(pl.multiple_of(i * chunk, chunk))
        return carry
    lax.fori_loop(0, n_rows // chunk, step, 0)


def _conv_tap(wconv_ref, k):
    return jnp.concatenate([wconv_ref[k]] * (CONV_CHUNK // SUBLANES), axis=0)


def _static_loop(n_rows, body, chunk=None):
    chunk = ROW_CHUNK if chunk is None else chunk
    for r0 in range(0, n_rows, chunk):
        body(r0)


def _adaln_kernel(c_ref, w_ref, b_ref, o_ref):
    c = c_ref[...]
    s_hi, s_lo = _split_bf16(c * _sigmoid(c))
    w_hi, w_lo = _split_bf16(w_ref[0])
    acc = _dot(s_hi, w_hi) + _dot(s_hi, w_lo) + _dot(s_lo, w_hi)
    o_ref[0] = acc + b_ref[0]


def _adaln(c_all, w_ada, b_ada):
    depth, _, n_out = w_ada.shape
    n_rows = c_all.shape[0]
    tn = ADALN_COLS
    return pl.pallas_call(
        _adaln_kernel,
        grid=(depth, n_out // tn),
        in_specs=[
            pl.BlockSpec((n_rows, D_MODEL), lambda l, j: (0, 0)),
            pl.BlockSpec((1, D_MODEL, tn), lambda l, j: (l, 0, j)),
            pl.BlockSpec((1, 1, tn), lambda l, j: (l, 0, j)),
        ],
        out_specs=pl.BlockSpec((1, n_rows, tn), lambda l, j: (l, 0, j)),
        out_shape=jax.ShapeDtypeStruct((depth, n_rows, n_out), F32),
        compiler_params=pltpu.CompilerParams(
            dimension_semantics=("arbitrary", "arbitrary"), vmem_limit_bytes=VMEM_LIMIT),
        name="adaln",
    )(c_all, w_ada, b_ada.reshape(depth, 1, n_out))


def _phase_in_norm(x_ref, h_scr, mod, n_rows, loop=_row_loop):
    def body(r0):
        r = pl.ds(r0, ROW_CHUNK)
        h = _rms(x_ref[r, :]) * (1.0 + mod(1, r0)) + mod(0, r0)
        h_scr[r, :] = h.astype(BF16)
    loop(n_rows, body)


def _phase_glu(z_scr, a_dst, a_off, n_rows, loop=_row_loop):
    def body(r0):
        r = pl.ds(r0, ROW_CHUNK)
        a_dst[pl.ds(r0 + a_off, ROW_CHUNK), :] = z_scr[r, 0:C_CONV] * _sigmoid(z_scr[r, C_CONV:2 * C_CONV])
    loop(n_rows, body)


def _phase_uv(z_scr, vec_ref, u_scr, v_scr, vout_ref, n_rows, loop=_row_loop):
    ln_g = vec_ref[3:4, :]
    ln_b = vec_ref[4:5, :]

    def body(r0):
        r = pl.ds(r0, ROW_CHUNK)
        u_scr[r, :] = _gelu(z_scr[r, 2 * C_CONV:2 * C_CONV + C_SGU])
        gv = _gelu(z_scr[r, 2 * C_CONV + C_SGU:2 * C_CONV + 2 * C_SGU])
        mu = jnp.mean(gv, axis=-1, keepdims=True)
        dv = gv - mu
        var = jnp.mean(dv * dv, axis=-1, keepdims=True)
        v = dv * lax.rsqrt(var + EPS) * ln_g + ln_b
        if vout_ref is not None:
            vout_ref[r, :] = v
        v_scr[r, :] = v.astype(v_scr.dtype)
    loop(n_rows, body)


def _phase_group_norm(conv_scr, convb_scr, stat_scr, g_ref, vec_ref, y_scr, n_rows, loop=_row_loop):
    stat_scr[...] = _dot(convb_scr[...], g_ref[...])

    def center(r0):
        r = pl.ds(r0, ROW_CHUNK)
        d = conv_scr[r, :] - stat_scr[r, :]
        conv_scr[r, :] = d
        convb_scr[r, :] = (d * d).astype(BF16)
    loop(n_rows, center)
    stat_scr[...] = _dot(convb_scr[...], g_ref[...])
    gn_g = vec_ref[1:2, :]
    gn_b = vec_ref[2:3, :]
    beta_a = vec_ref[5:6, :]

    def finish(r0):
        r = pl.ds(r0, ROW_CHUNK)
        gn = conv_scr[r, :] * lax.rsqrt(stat_scr[r, :] + EPS) * gn_g + gn_b
        ya = gn * _sigmoid(gn)
        y_scr[r, 0:C_CONV] = (_rms(ya) * beta_a).astype(BF16)
    loop(n_rows, finish)


def _phase_out(x_ref, y_scr, wout_ref, z_scr, wr_ref, br_ref, mod, h_scr,
               xmid_ref, h2_ref, logit_ref, n_rows, loop=_row_loop):
    z_scr[:, 0:D_MODEL] = (_dot(y_scr[:, 0:C_CONV], wout_ref[0:C_CONV, :])
                           + _dot(y_scr[:, C_CONV:C_CONV + C_SGU], wout_ref[C_CONV:C_CONV + C_SGU, :]))

    def body(r0):
        r = pl.ds(r0, ROW_CHUNK)
        xm = x_ref[r, :] + mod(2, r0) * z_scr[r, 0:D_MODEL]
        xmid_ref[r, :] = xm
        h2 = _rms(xm) * (1.0 + mod(4, r0)) + mod(3, r0)
        hi, lo = _split_bf16(h2)
        h2_ref[r, :] = _pack_bf16_pair(h2)
        y_scr[r, :] = hi
        h_scr[r, :] = lo
    loop(n_rows, body)
    w_hi, w_lo = _split_bf16(wr_ref[...])
    both = _dot_nt(jnp.concatenate([w_hi, w_lo], axis=0), y_scr[...])
    logit_ref[...] = (both[0:N_EXPERTS] + both[N_EXPERTS:2 * N_EXPERTS]
                      + _dot_nt(w_hi, h_scr[...])) + br_ref[...]


def _moe_half(yg_ref, w, r, hi_half):
    moe = jnp.zeros((COMBINE_CHUNK, D_WORDS), F32)
    for k in range(TOP_K):
        word = yg_ref[k, r, :]
        bits = (word & HI_MASK) if hi_half else lax.shift_left(word, 16)
        moe = moe + w[:, k:k + 1] * lax.bitcast_convert_type(bits, F32)
    return moe


def _phase_moe_residual(xmid_ref, yg_ref, wgt_ref, gate, x_dst, n_rows, loop=_row_loop):
    def body(r0):
        r = pl.ds(r0, COMBINE_CHUNK)
        w = wgt_ref[r, :]
        g = gate(r0)
        for hi_half in (False, True):
            cols = slice(D_WORDS, D_MODEL) if hi_half else slice(0, D_WORDS)
            x_dst[r, cols] = xmid_ref[r, cols] + g[:, cols] * _moe_half(yg_ref, w, r, hi_half)
    loop(n_rows, body, COMBINE_CHUNK)


def _mixer_prompt_kernel(*refs, tiles_per_seq, n_prompt_tiles, fused):
    x_ref, mod_ref = refs[:2]
    refs = refs[2:]
    if fused:
        yg_ref, wgt_ref, gate_ref = refs[:3]
        refs = refs[3:]
        x_scr = refs[-1]
        refs = refs[:-1]
    (win_ref, wout_ref, wconv_ref, vec_ref, g_ref, ws_ref, bs_ref, wr_ref, br_ref,
     smid_ref, sh2_ref, slogit_ref, xmid_ref, h2_ref, logit_ref, cst_ref,
     h_scr, z_scr, aext_scr, conv_scr, convb_scr, stat_scr, u_scr, v_scr, y_scr) = refs
    i = pl.program_id(0)

    @pl.when(i == n_prompt_tiles)
    def _():
        xmid_ref[...] = smid_ref[...]
        h2_ref[...] = sh2_ref[...]
        logit_ref[...] = slogit_ref[...]

    @pl.when(i < n_prompt_tiles)
    def _():
        x_src = x_ref
        if fused:
            _phase_moe_residual(x_ref, yg_ref, wgt_ref, lambda r0: gate_ref[0], x_scr, ROW_TILE,
                                _static_loop)
            x_src = x_scr
        _mixer_prompt_tile(lax.rem(i, tiles_per_seq), tiles_per_seq,
                           x_src, mod_ref, win_ref, wout_ref, wconv_ref, vec_ref, g_ref,
                           ws_ref, bs_ref, wr_ref, br_ref, xmid_ref, h2_ref, logit_ref, cst_ref,
                           h_scr, z_scr, aext_scr, conv_scr, convb_scr, stat_scr, u_scr, v_scr, y_scr)


def _mixer_prompt_tile(j, tiles_per_seq, x_ref, mod_ref, win_ref, wout_ref, wconv_ref, vec_ref, g_ref,
                       ws_ref, bs_ref, wr_ref, br_ref, xmid_ref, h2_ref, logit_ref, cst_ref,
                       h_scr, z_scr, aext_scr, conv_scr, convb_scr, stat_scr, u_scr, v_scr, y_scr):
    n_rows = ROW_TILE

    def mod(idx, r0):
        del r0
        return mod_ref[0, :, idx * D_MODEL:(idx + 1) * D_MODEL]

    a_buf = aext_scr.at[0]

    @pl.when(j == 0)
    def _():
        a_buf[0:CARRY_ROWS, :] = jnp.zeros((CARRY_ROWS, C_CONV), F32)

    loop = _static_loop
    _phase_in_norm(x_ref, h_scr, mod, n_rows, loop)
    n_glu = 2 * C_CONV
    z_scr[:, 0:n_glu] = _dot(h_scr[...], win_ref[:, 0:n_glu])
    _phase_glu(z_scr, a_buf, CARRY_ROWS, n_rows, loop)
    z_scr[:, n_glu:] = _dot(h_scr[...], win_ref[:, n_glu:])
    n_shift = n_rows + CARRY_ROWS - SUBLANES
    for s in range(1, SUBLANES):
        aext_scr[s, 0:n_shift, :] = a_buf[s:s + n_shift, :]

    b_conv = vec_ref[0:1, :]
    lead = CARRY_ROWS - CONV_STATE
    for r0 in range(0, n_rows, CONV_CHUNK):
        acc = jnp.zeros((CONV_CHUNK, C_CONV), F32)
        for k in range(CONV_WIDTH):
            s = (lead + k) % SUBLANES
            q0 = r0 + (lead + k - s)
            acc = acc + _conv_tap(wconv_ref, k) * aext_scr[s, q0:q0 + CONV_CHUNK, :]
        conv = acc + b_conv
        conv_scr[r0:r0 + CONV_CHUNK, :] = conv
        convb_scr[r0:r0 + CONV_CHUNK, :] = conv.astype(BF16)
    _phase_uv(z_scr, vec_ref, u_scr, v_scr, None, n_rows, loop)
    _phase_group_norm(conv_scr, convb_scr, stat_scr, g_ref, vec_ref, y_scr, n_rows, loop)

    row_i = lax.broadcasted_iota(jnp.int32, (CHUNK, CHUNK), 0)
    col_i = lax.broadcasted_iota(jnp.int32, (CHUNK, CHUNK), 1)
    tril = (col_i <= row_i).astype(F32)
    for hd in range(SGU_HEADS):
        ws_h = (ws_ref[hd] * tril).astype(BF16)
        cols = slice(hd * SGU_HEAD_DIM, (hd + 1) * SGU_HEAD_DIM)
        s_cols = slice(n_glu + hd * SGU_HEAD_DIM, n_glu + (hd + 1) * SGU_HEAD_DIM)
        for c in range(n_rows // CHUNK):
            rows = slice(c * CHUNK, (c + 1) * CHUNK)
            z_scr[rows, s_cols] = _dot(ws_h, v_scr[rows, cols])
    beta_b = vec_ref[6:7, :]

    def gate_body(r0):
        r = pl.ds(r0, ROW_CHUNK)
        rb = pl.ds(r0 % CHUNK, ROW_CHUNK)
        yb = u_scr[r, :] * (z_scr[r, n_glu:n_glu + C_SGU] + bs_ref[rb, :])
        y_scr[r, C_CONV:C_CONV + C_SGU] = (_rms(yb) * beta_b).astype(BF16)
    loop(n_rows, gate_body)

    _phase_out(x_ref, y_scr, wout_ref, z_scr, wr_ref, br_ref, mod, h_scr,
               xmid_ref, h2_ref, logit_ref, n_rows, loop)

    @pl.when(j == tiles_per_seq - 1)
    def _():
        cst_ref[0] = a_buf[n_rows + lead:n_rows + CARRY_ROWS, :]

    a_buf[0:CARRY_ROWS, :] = a_buf[n_rows:n_rows + CARRY_ROWS, :]


def _mixer_sample_kernel(*refs, fused):
    x_ref, mod_ref = refs[:2]
    refs = refs[2:]
    if fused:
        yg_ref, wgt_ref, gate_ref = refs[:3]
        refs = refs[3:]
        x_scr = refs[-1]
        refs = refs[:-1]
    (win_ref, wout_ref, wconv_ref, vec_ref, g_ref, wsv_ref, bsv_ref, wr_ref, br_ref, state_ref,
     xmid_ref, h2_ref, logit_ref, cnew_ref, vout_ref,
     h_scr, z_scr, conv_scr, convb_scr, stat_scr, u_scr, v_scr, y_scr, anew_ref) = refs
    n_rows = x_ref.shape[0]
    n_seq = state_ref.shape[1]
    n_pos = n_rows // n_seq
    loop = _static_loop

    def mod(idx, r0):
        return mod_ref[pl.ds(r0 % n_seq, ROW_CHUNK), idx * D_MODEL:(idx + 1) * D_MODEL]

    if fused:
        def gate(r0):
            return gate_ref[pl.ds(r0 % n_seq, COMBINE_CHUNK), :]
        _phase_moe_residual(x_ref, yg_ref, wgt_ref, gate, x_scr, n_rows, loop)
        x_ref = x_scr

    _phase_in_norm(x_ref, h_scr, mod, n_rows, loop)
    z_scr[...] = _dot(h_scr[...], win_ref[...])
    _phase_glu(z_scr, anew_ref, 0, n_rows, loop)
    _phase_uv(z_scr, vec_ref, u_scr, v_scr, vout_ref, n_rows, loop)

    keep = CONV_STATE - n_pos
    cnew_ref[0:keep] = state_ref[n_pos:CONV_STATE]
    for t in range(n_pos):
        cnew_ref[keep + t] = anew_ref[t * n_seq:(t + 1) * n_seq, :]

    b_conv = vec_ref[0:1, :]

    def conv_body(s0):
        rs = pl.ds(s0, CONV_CHUNK)
        for t in range(n_pos):
            acc = jnp.zeros((CONV_CHUNK, C_CONV), F32)
            for k in range(CONV_WIDTH):
                p = t + k
                if p < CONV_STATE:
                    src = state_ref[p, rs, :]
                else:
                    src = anew_ref[pl.ds(s0 + (p - CONV_STATE) * n_seq, CONV_CHUNK), :]
                acc = acc + _conv_tap(wconv_ref, k) * src
            conv = acc + b_conv
            r = pl.ds(s0 + t * n_seq, CONV_CHUNK)
            conv_scr[r, :] = conv
            convb_scr[r, :] = conv.astype(BF16)
    loop(n_seq, conv_body, CONV_CHUNK)

    _phase_group_norm(conv_scr, convb_scr, stat_scr, g_ref, vec_ref, y_scr, n_rows, loop)

    beta_b = vec_ref[6:7, :]

    def gate_body(s0):
        for t in range(n_pos):
            s = jnp.zeros((CONV_CHUNK, C_SGU), F32)
            for jj in range(t + 1):
                s = s + wsv_ref[t * n_pos + jj:t * n_pos + jj + 1, :] * v_scr[pl.ds(s0 + jj * n_seq, CONV_CHUNK), :]
            r = pl.ds(s0 + t * n_seq, CONV_CHUNK)
            yb = u_scr[r, :] * (s + bsv_ref[t:t + 1, :])
            y_scr[r, C_CONV:C_CONV + C_SGU] = (_rms(yb) * beta_b).astype(BF16)
    loop(n_seq, gate_body, CONV_CHUNK)

    _phase_out(x_ref, y_scr, wout_ref, z_scr, wr_ref, br_ref, mod, h_scr,
               xmid_ref, h2_ref, logit_ref, n_rows, loop)


def _const_spec(shape):
    nd = len(shape)
    return pl.BlockSpec(shape, lambda *_: (0,) * nd, pipeline_mode=pl.Buffered(1))


def _const_out_spec(shape):
    nd = len(shape)
    return pl.BlockSpec(shape, lambda *_: (0,) * nd)


def _layer_spec(shape, layer):
    nd = len(shape)
    return pl.BlockSpec((None,) + tuple(shape), lambda *_: (layer,) + (0,) * nd,
                        pipeline_mode=pl.Buffered(1))


def _mixer(layer, x_p, x_s, x_s_block, moe, mod_p, mod_s, gate_p, gate_s, w_in_b, w_out_b, wconv_p, vec,
           g_mat, w_s, bs_full, wsv, bsv, w_router, b_router, state_t, n_prompt_seq, seq_len):
    fused = moe is not None
    n_prompt = n_prompt_seq * seq_len
    n_seq, n_pos = state_t.shape[2], bsv.shape[1]
    n_sample = n_seq * n_pos
    t_all = n_prompt + n_sample
    nj = seq_len // ROW_TILE
    weight_specs = [
        _layer_spec((D_MODEL, 2 * C_CONV + 2 * C_SGU), layer),
        _layer_spec((D_MODEL, D_MODEL), layer),
        _layer_spec((CARRY_ROWS, SUBLANES, C_CONV), layer),
        _layer_spec((SUBLANES, C_CONV), layer),
        _const_spec((C_CONV, C_CONV)),
    ]
    router_specs = [_layer_spec((N_EXPERTS, D_MODEL), layer), _layer_spec((N_EXPERTS, 1), layer)]
    common_scratch = lambda n: [
        pltpu.VMEM((n, D_MODEL), BF16),
        pltpu.VMEM((n, 2 * C_CONV + 2 * C_SGU), F32),
    ]
    tail_scratch = lambda n, vdt: [
        pltpu.VMEM((n, C_CONV), F32),
        pltpu.VMEM((n, C_CONV), BF16),
        pltpu.VMEM((n, C_CONV), F32),
        pltpu.VMEM((n, C_SGU), F32),
        pltpu.VMEM((n, C_SGU), vdt),
        pltpu.VMEM((n, D_MODEL), BF16),
    ]
    assert n_sample == ROW_TILE
    n_tiles_p = n_prompt // ROW_TILE
    tile_p = lambda i: jnp.minimum(i, n_tiles_p - 1)
    seq_of = lambda i: jnp.minimum(i // nj, n_prompt_seq - 1)
    moe_s_specs, moe_p_specs, moe_s_args, moe_p_args, x_scratch = [], [], [], [], []
    if fused:
        yg, wgt = moe
        moe_s_specs = [pl.BlockSpec((TOP_K, n_sample, D_WORDS), lambda i: (0, x_s_block, 0)),
                       pl.BlockSpec((n_sample, LANES), lambda i: (x_s_block, 0)),
                       _layer_spec((n_seq, D_MODEL), layer - 1)]
        moe_p_specs = [pl.BlockSpec((TOP_K, ROW_TILE, D_WORDS), lambda i: (0, tile_p(i), 0)),
                       pl.BlockSpec((ROW_TILE, LANES), lambda i: (tile_p(i), 0)),
                       pl.BlockSpec((None, 1, 1, D_MODEL), lambda i: (layer - 1, seq_of(tile_p(i)), 0, 0))]
        moe_s_args = [yg, wgt, gate_s]
        moe_p_args = [yg, wgt, gate_p]
        x_scratch = [pltpu.VMEM((ROW_TILE, D_MODEL), F32)]
    smid, sh2, slogits, a_new, v_new = pl.pallas_call(
        functools.partial(_mixer_sample_kernel, fused=fused),
        grid=(1,),
        in_specs=[
            pl.BlockSpec((n_sample, D_MODEL), lambda i: (x_s_block, 0)),
            _layer_spec((n_seq, N_MOD * D_MODEL), layer),
            *moe_s_specs,
            *weight_specs,
            _layer_spec((n_pos * n_pos, C_SGU), layer),
            _layer_spec((n_pos, C_SGU), layer),
            *router_specs,
            _layer_spec((CONV_STATE, n_seq, C_CONV), layer),
        ],
        out_specs=(
            _const_out_spec((n_sample, D_MODEL)),
            _const_out_spec((n_sample, D_WORDS)),
            _const_out_spec((N_EXPERTS, n_sample)),
            _const_out_spec((CONV_STATE, n_seq, C_CONV)),
            _const_out_spec((n_sample, C_SGU)),
        ),
        out_shape=(
            jax.ShapeDtypeStruct((n_sample, D_MODEL), F32),
            jax.ShapeDtypeStruct((n_sample, D_WORDS), jnp.int32),
            jax.ShapeDtypeStruct((N_EXPERTS, n_sample), F32),
            jax.ShapeDtypeStruct((CONV_STATE, n_seq, C_CONV), F32),
            jax.ShapeDtypeStruct((n_sample, C_SGU), F32),
        ),
        scratch_shapes=common_scratch(n_sample) + tail_scratch(n_sample, F32)
        + [pltpu.VMEM((n_sample, C_CONV), F32)] + x_scratch,
        compiler_params=pltpu.CompilerParams(
            dimension_semantics=("arbitrary",), vmem_limit_bytes=VMEM_LIMIT),
        name="mixer_sample",
    )(x_s, mod_s, *moe_s_args, w_in_b, w_out_b, wconv_p, vec, g_mat, wsv, bsv, w_router, b_router,
      state_t)

    xmid, h2, logits, cst = pl.pallas_call(
        functools.partial(_mixer_prompt_kernel, tiles_per_seq=nj, n_prompt_tiles=n_tiles_p,
                          fused=fused),
        grid=(n_tiles_p + 1,),
        in_specs=[
            pl.BlockSpec((ROW_TILE, D_MODEL), lambda i: (tile_p(i), 0)),
            pl.BlockSpec((None, 1, 1, N_MOD * D_MODEL), lambda i: (layer, seq_of(i), 0, 0)),
            *moe_p_specs,
            *weight_specs,
            _layer_spec((SGU_HEADS, CHUNK, CHUNK), layer),
            _layer_spec((CHUNK, C_SGU), layer),
            *router_specs,
            _const_spec((n_sample, D_MODEL)),
            _const_spec((n_sample, D_WORDS)),
            _const_spec((N_EXPERTS, n_sample)),
        ],
        out_specs=(
            pl.BlockSpec((ROW_TILE, D_MODEL), lambda i: (i, 0)),
            pl.BlockSpec((ROW_TILE, D_WORDS), lambda i: (i, 0)),
            pl.BlockSpec((N_EXPERTS, ROW_TILE), lambda i: (0, i)),
            pl.BlockSpec((1, CONV_STATE, C_CONV), lambda i: (seq_of(i), 0, 0)),
        ),
        out_shape=(
            jax.ShapeDtypeStruct((t_all, D_MODEL), F32),
            jax.ShapeDtypeStruct((t_all, D_WORDS), jnp.int32),
            jax.ShapeDtypeStruct((N_EXPERTS, t_all), F32),
            jax.ShapeDtypeStruct((n_prompt_seq, CONV_STATE, C_CONV), F32),
        ),
        scratch_shapes=common_scratch(ROW_TILE)
        + [pltpu.VMEM((SUBLANES, CARRY_ROWS + ROW_TILE, C_CONV), F32)]
        + tail_scratch(ROW_TILE, BF16) + x_scratch,
        compiler_params=pltpu.CompilerParams(
            dimension_semantics=("arbitrary",), vmem_limit_bytes=VMEM_LIMIT),
        name="mixer_prompt",
    )(x_p, mod_p, *moe_p_args, w_in_b, w_out_b, wconv_p, vec, g_mat, w_s, bs_full, w_router, b_router,
      smid, sh2, slogits)
    return xmid, h2, logits, cst, a_new, v_new


def _tile_plan(cnt):
    e_sub = lax.broadcasted_iota(jnp.int32, (N_EXPERTS, N_EXPERTS), 0)
    e_lane = lax.broadcasted_iota(jnp.int32, (N_EXPERTS, N_EXPERTS), 1)
    tiles = jnp.floor((cnt + (GEMM_TILE - 1.0)) * (1.0 / GEMM_TILE))
    active = jnp.where(cnt > 0.0, 1.0, 0.0)

    def over_experts(tri, col):
        return _dot(tri.astype(BF16), jnp.broadcast_to(col, (N_EXPERTS, LANES)).astype(BF16))[:, 0:1]
    tile_end = over_experts(jnp.where(e_lane <= e_sub, 1.0, 0.0), tiles)
    tile_start = tile_end - tiles
    ordinal = over_experts(jnp.where(e_lane < e_sub, 1.0, 0.0), active)

    g = lax.broadcasted_iota(jnp.int32, (1, PLAN_LANES), 1).astype(F32)
    e_id = lax.broadcasted_iota(jnp.int32, (N_EXPERTS, PLAN_LANES), 0).astype(F32)
    done = jnp.sum(jnp.where(tile_end <= g, 1.0, 0.0), axis=0, keepdims=True)
    live = done < float(N_EXPERTS)
    last_e = jnp.max(jnp.where(cnt > 0.0, e_id[:, 0:1], 0.0), axis=0, keepdims=True)
    te = jnp.where(live, done, last_e)
    pick = e_id == te

    def at_tile(col):
        return jnp.sum(jnp.where(pick, col, 0.0), axis=0, keepdims=True)
    start_g = at_tile(tile_start)
    ord_g = at_tile(ordinal)
    valid = jnp.clip(at_tile(cnt) - (g - start_g) * GEMM_TILE, 0.0, float(GEMM_TILE))
    valid = jnp.where(live, valid, 0.0)
    first = jnp.where(live, jnp.where(g == start_g, 1.0, 0.0), 0.0)
    slot = ord_g - 2.0 * jnp.floor(ord_g * 0.5)
    later = jnp.where(e_id > te, jnp.where(cnt > 0.0, e_id, float(N_EXPERTS)), float(N_EXPERTS))
    nxt = jnp.min(later, axis=0, keepdims=True)
    nxt = jnp.where(nxt >= float(N_EXPERTS), -1.0, nxt)

    row = lax.broadcasted_iota(jnp.int32, (PLAN_ROWS, PLAN_LANES), 0)
    plan = jnp.zeros((PLAN_ROWS, PLAN_LANES), F32)
    for r, v in ((PLAN_EXPERT, te), (PLAN_VALID, valid), (PLAN_FIRST, first), (PLAN_SLOT, slot),
                 (PLAN_NEXT, nxt)):
        plan = jnp.where(row == r, v, plan)
    return plan.astype(jnp.int32), tile_start * GEMM_TILE


def _router_kernel(lg_ref, wgt_ref, pos_ref, plan_ref, cnt_scr, run_scr, start_scr, earlier_scr):
    ph = pl.program_id(0)
    i = pl.program_id(1)
    n = lg_ref.shape[1]

    @pl.when(jnp.logical_and(ph == 0, i == 0))
    def _():
        cnt_scr[...] = jnp.zeros_like(cnt_scr)

    lg = lg_ref[...]
    sub = lax.broadcasted_iota(jnp.int32, lg.shape, 0).astype(F32)
    vals, sels = [], []
    for _ in range(TOP_K):
        m = jnp.max(lg, axis=0, keepdims=True)
        idx = jnp.min(jnp.where(lg == m, sub, float(N_EXPERTS)), axis=0, keepdims=True)
        sel = sub == idx
        vals.append(m)
        sels.append(sel)
        lg = jnp.where(sel, -jnp.inf, lg)
    onehot = jnp.zeros(lg.shape, F32)
    for sel in sels:
        onehot = onehot + jnp.where(sel, 1.0, 0.0)
    tile_cnt = jnp.sum(onehot, axis=1, keepdims=True)

    @pl.when(ph == 0)
    def _():
        cnt_scr[...] = cnt_scr[...] + tile_cnt

    @pl.when(jnp.logical_and(ph == 1, i == 0))
    def _():
        plan, row_start = _tile_plan(cnt_scr[...])
        plan_ref[...] = plan
        start_scr[...] = row_start
        run_scr[...] = jnp.zeros_like(run_scr)
        t_row = lax.broadcasted_iota(jnp.int32, (PREFIX_BLOCK, PREFIX_BLOCK), 0)
        t_col = lax.broadcasted_iota(jnp.int32, (PREFIX_BLOCK, PREFIX_BLOCK), 1)
        earlier_scr[...] = jnp.where(t_row < t_col, 1.0, 0.0).astype(BF16)

    @pl.when(ph == 1)
    def _():
        exps = [jnp.exp(v - vals[0]) for v in vals]
        inv = 1.0 / (exps[0] + exps[1] + exps[2] + exps[3])
        w_row = lax.broadcasted_iota(jnp.int32, (LANES, n), 0)
        w_all = jnp.zeros((LANES, n), F32)
        for k in range(TOP_K):
            w_all = jnp.where(w_row == k, exps[k] * inv, w_all)
        wgt_ref[...] = w_all.T
        base = run_scr[...] + start_scr[...]
        onehot_b = onehot.astype(BF16)
        for b0 in range(0, n, PREFIX_BLOCK):
            blk = slice(b0, b0 + PREFIX_BLOCK)
            slot0 = _dot(onehot_b[:, blk], earlier_scr[...]) + base
            for k in range(TOP_K):
                pos_k = jnp.sum(jnp.where(sels[k][:, blk], slot0, 0.0), axis=0, keepdims=True)
                pos_ref[k:k + 1, blk] = pos_k.astype(jnp.int32)
            base = base + jnp.sum(onehot[:, blk], axis=1, keepdims=True)
        run_scr[...] = run_scr[...] + tile_cnt


def _router(logits):
    t_all = logits.shape[1]
    col = pltpu.VMEM((N_EXPERTS, 1), F32)
    return pl.pallas_call(
        _router_kernel,
        grid=(2, t_all // ROUTER_TILE),
        in_specs=[pl.BlockSpec((N_EXPERTS, ROUTER_TILE), lambda ph, i: (0, i))],
        out_specs=(
            pl.BlockSpec((ROUTER_TILE, LANES), lambda ph, i: (i * ph, 0)),
            pl.BlockSpec((TOP_K, ROUTER_TILE), lambda ph, i: (0, i * ph)),
            _const_out_spec((PLAN_ROWS, PLAN_LANES)),
        ),
        out_shape=(
            jax.ShapeDtypeStruct((t_all, LANES), F32),
            jax.ShapeDtypeStruct((TOP_K, t_all), jnp.int32),
            jax.ShapeDtypeStruct((PLAN_ROWS, PLAN_LANES), jnp.int32),
        ),
        scratch_shapes=[col, col, col, pltpu.VMEM((PREFIX_BLOCK, PREFIX_BLOCK), BF16)],
        compiler_params=pltpu.CompilerParams(
            dimension_semantics=("arbitrary", "arbitrary"), vmem_limit_bytes=VMEM_LIMIT),
        name="router",
    )(logits)


def _expert_kernel(plan_ref, x_ref, w1_hbm, b1_ref, w2_hbm, b2_ref, o_ref,
                   w1f_scr, w2f_scr, w1b_scr, w2b_scr, sem, *, layer):
    def weight_copies(expert, slot):
        return (pltpu.make_async_copy(w1_hbm.at[layer, expert], w1f_scr.at[slot], sem.at[0, slot]),
                pltpu.make_async_copy(w2_hbm.at[layer, expert], w2f_scr.at[slot], sem.at[1, slot]))

    for sub in range(TILES_PER_STEP):
        _expert_tile(pl.program_id(0) * TILES_PER_STEP + sub, sub * GEMM_TILE, plan_ref, x_ref,
                     b1_ref, b2_ref, o_ref, w1f_scr, w2f_scr, w1b_scr, w2b_scr, weight_copies)


def _expert_tile(g, row0, plan_ref, x_ref, b1_ref, b2_ref, o_ref, w1f_scr, w2f_scr, w1b_scr, w2b_scr,
                 weight_copies):
    nv = plan_ref[PLAN_VALID, g]
    expert = plan_ref[PLAN_EXPERT, g]

    @pl.when(plan_ref[PLAN_FIRST, g] == 1)
    def _():
        slot = plan_ref[PLAN_SLOT, g]
        nxt = plan_ref[PLAN_NEXT, g]

        @pl.when(g == 0)
        def _():
            for cp in weight_copies(expert, slot):
                cp.start()
        for cp in weight_copies(expert, slot):
            cp.wait()

        @pl.when(nxt >= 0)
        def _():
            for cp in weight_copies(nxt, 1 - slot):
                cp.start()
        w1b_scr[...] = w1f_scr[slot].astype(BF16)
        w2b_scr[...] = w2f_scr[slot].astype(BF16)

    def ffn(m):
        x_left, x_right = _unpack_bf16_pair(x_ref[row0:row0 + m, :])
        x = jnp.concatenate([x_left.astype(BF16), x_right.astype(BF16)], axis=1)

        def hidden(col0):
            cols = slice(col0, col0 + FF_BLOCK)
            return _dot(x, w1b_scr[:, cols]) + b1_ref[expert, :, cols]

        acts = []
        for jb in range(D_FF // FF_BLOCK):
            gg = jnp.minimum(hidden(jb * FF_BLOCK), SWIGLU_LIMIT)
            up = jnp.clip(hidden(D_FF + jb * FF_BLOCK), -SWIGLU_LIMIT, SWIGLU_LIMIT)
            acts.append((gg * _sigmoid(gg, SWIGLU_ALPHA) * (up + 1.0)).astype(BF16))
        y = _dot(jnp.concatenate(acts, axis=1), w2b_scr[...]) + b2_ref[expert]
        row = lax.broadcasted_iota(jnp.int32, y.shape, 0)
        o_ref[row0:row0 + m, :] = _pack_bf16_pair(jnp.where(row < nv, y, 0.0))
        if m < GEMM_TILE:
            o_ref[row0 + m:row0 + GEMM_TILE, :] = jnp.zeros((GEMM_TILE - m, D_WORDS), jnp.int32)

    @pl.when(nv > GEMM_TILE // 2)
    def _():
        ffn(GEMM_TILE)

    @pl.when(jnp.logical_and(nv > 0, nv <= GEMM_TILE // 2))
    def _():
        ffn(GEMM_TILE // 2)

    @pl.when(nv == 0)
    def _():
        o_ref[row0:row0 + GEMM_TILE, :] = jnp.zeros((GEMM_TILE, D_WORDS), jnp.int32)


def _experts(xs, plan, w1, b1, w2, b2, layer):
    n_tiles = xs.shape[0] // GEMM_TILE
    depth = w1.shape[0]
    assert n_tiles % TILES_PER_STEP == 0
    step_rows = TILES_PER_STEP * GEMM_TILE
    bias_map = lambda s, plan: (layer, 0, 0, 0)
    grid_spec = pltpu.PrefetchScalarGridSpec(
        num_scalar_prefetch=1,
        grid=(n_tiles // TILES_PER_STEP,),
        in_specs=[
            pl.BlockSpec((step_rows, D_WORDS), lambda s, plan: (s, 0)),
            pl.BlockSpec(memory_space=pl.ANY),
            pl.BlockSpec((None, N_EXPERTS, 1, 2 * D_FF), bias_map, pipeline_mode=pl.Buffered(1)),
            pl.BlockSpec(memory_space=pl.ANY),
            pl.BlockSpec((None, N_EXPERTS, 1, D_MODEL), bias_map, pipeline_mode=pl.Buffered(1)),
        ],
        out_specs=pl.BlockSpec((step_rows, D_WORDS), lambda s, plan: (s, 0)),
        scratch_shapes=[
            pltpu.VMEM((2, D_MODEL, 2 * D_FF), F32),
            pltpu.VMEM((2, D_FF, D_MODEL), F32),
            pltpu.VMEM((D_MODEL, 2 * D_FF), BF16),
            pltpu.VMEM((D_FF, D_MODEL), BF16),
            pltpu.SemaphoreType.DMA((2, 2)),
        ],
    )
    return pl.pallas_call(
        functools.partial(_expert_kernel, layer=layer),
        grid_spec=grid_spec,
        out_shape=jax.ShapeDtypeStruct((n_tiles * GEMM_TILE, D_WORDS), jnp.int32),
        compiler_params=pltpu.CompilerParams(
            dimension_semantics=("arbitrary",), vmem_limit_bytes=VMEM_LIMIT),
        name="experts",
    )(plan, xs, w1, b1.reshape(depth, N_EXPERTS, 1, 2 * D_FF), w2,
      b2.reshape(depth, N_EXPERTS, 1, D_MODEL))


def _final_kernel(x_ref, yg_ref, wgt_ref, gp_ref, gs_ref, fg_ref, op_ref, os_ref, *, n_prompt_blocks):
    i = pl.program_id(0)
    n_seq = gs_ref.shape[0]

    def run(gate, o_ref):
        def body(r0):
            r = pl.ds(r0, COMBINE_CHUNK)
            w = wgt_ref[r, :]
            g = gate(r0)
            xl = x_ref[r, 0:D_WORDS] + g[:, 0:D_WORDS] * _moe_half(yg_ref, w, r, False)
            xr = x_ref[r, D_WORDS:D_MODEL] + g[:, D_WORDS:D_MODEL] * _moe_half(yg_ref, w, r, True)
            ms = (jnp.sum(xl * xl, axis=-1, keepdims=True)
                  + jnp.sum(xr * xr, axis=-1, keepdims=True)) * (1.0 / D_MODEL)
            scale = lax.rsqrt(ms + EPS)
            o_ref[r, 0:D_WORDS] = xl * scale * fg_ref[:, 0:D_WORDS]
            o_ref[r, D_WORDS:D_MODEL] = xr * scale * fg_ref[:, D_WORDS:D_MODEL]
        _row_loop(x_ref.shape[0], body, COMBINE_CHUNK)

    @pl.when(i < n_prompt_blocks)
    def _():
        run(lambda r0: gp_ref[0], op_ref)

    @pl.when(i >= n_prompt_blocks)
    def _():
        run(lambda r0: gs_ref[pl.ds(pl.multiple_of(lax.rem(r0, n_seq), COMBINE_CHUNK), COMBINE_CHUNK), :],
            os_ref)


def _final(xmid, yg, wgt, gate_p, gate_s, final_g, n_prompt, seq_len):
    t_all = xmid.shape[0]
    n_prompt_blocks = n_prompt // ROW_TILE
    per_seq = seq_len // ROW_TILE
    n_seq_p = gate_p.shape[0]
    assert t_all - n_prompt == ROW_TILE
    out_specs = (
        pl.BlockSpec((ROW_TILE, D_MODEL), lambda i: (jnp.minimum(i, n_prompt_blocks - 1), 0)),
        pl.BlockSpec((ROW_TILE, D_MODEL), lambda i: (0, 0)),
    )
    out_shape = (jax.ShapeDtypeStruct((n_prompt, D_MODEL), F32),
                 jax.ShapeDtypeStruct((ROW_TILE, D_MODEL), F32))
    return pl.pallas_call(
        functools.partial(_final_kernel, n_prompt_blocks=n_prompt_blocks),
        grid=(t_all // ROW_TILE,),
        in_specs=[
            pl.BlockSpec((ROW_TILE, D_MODEL), lambda i: (i, 0)),
            pl.BlockSpec((TOP_K, ROW_TILE, D_WORDS), lambda i: (0, i, 0)),
            pl.BlockSpec((ROW_TILE, LANES), lambda i: (i, 0)),
            pl.BlockSpec((1, 1, D_MODEL), lambda i: (jnp.minimum(i // per_seq, n_seq_p - 1), 0, 0)),
            _const_spec(gate_s.shape),
            _const_spec((1, D_MODEL)),
        ],
        out_specs=out_specs,
        out_shape=out_shape,
        compiler_params=pltpu.CompilerParams(
            dimension_semantics=("arbitrary",), vmem_limit_bytes=VMEM_LIMIT),
        name="final_combine",
    )(xmid, yg, wgt, gate_p.reshape(n_seq_p, 1, D_MODEL), gate_s, final_g.reshape(1, D_MODEL))


def _sc_mesh():
    return plsc.VectorSubcoreMesh(core_axis_name="core", subcore_axis_name="subcore",
                                  num_cores=SC_CORES, num_subcores=SC_SUBCORES)


def _sc_worker_id():
    return lax.axis_index("subcore") * SC_CORES + lax.axis_index("core")


def _sc_dispatch(h2w, pos_c, n_slots):
    _, n_chunks, _, chunk = pos_c.shape

    @functools.partial(
        pl.kernel, mesh=_sc_mesh(),
        out_type=jax.ShapeDtypeStruct((n_slots, D_WORDS), jnp.int32),
        scratch_types=[pltpu.VMEM((n_chunks, TOP_K, chunk), jnp.int32),
                       pltpu.VMEM((chunk, D_WORDS), jnp.int32),
                       pltpu.SemaphoreType.DMA],
        name="moe_dispatch")
    def run(h_hbm, pos_hbm, out_hbm, idx_v, rows_v, sem):
        wid = _sc_worker_id()
        pltpu.sync_copy(pos_hbm.at[wid], idx_v)

        @pl.loop(0, n_chunks)
        def _(c):
            base = pl.multiple_of((wid * n_chunks + c) * chunk, SUBLANES)
            pltpu.sync_copy(h_hbm.at[pl.ds(base, chunk)], rows_v)
            copies = [pltpu.async_copy(rows_v, out_hbm.at[idx_v.at[c, k]], sem) for k in range(TOP_K)]
            for cp in copies:
                cp.wait()

    return run(h2w, pos_c)


def _sc_gather_back(ys, pos_c, t_all):
    _, n_chunks, _, chunk = pos_c.shape

    @functools.partial(
        pl.kernel, mesh=_sc_mesh(),
        out_type=jax.ShapeDtypeStruct((TOP_K, t_all, D_WORDS), jnp.int32),
        scratch_types=[pltpu.VMEM((n_chunks, TOP_K, chunk), jnp.int32),
                       pltpu.VMEM((TOP_K, chunk, D_WORDS), jnp.int32),
                       pltpu.SemaphoreType.DMA,
                       pltpu.SemaphoreType.DMA],
        name="moe_gather_back")
    def run(ys_hbm, pos_hbm, out_hbm, idx_v, rows_v, sem_in, sem_out):
        wid = _sc_worker_id()
        pltpu.sync_copy(pos_hbm.at[wid], idx_v)

        @pl.loop(0, n_chunks)
        def _(c):
            base = pl.multiple_of((wid * n_chunks + c) * chunk, SUBLANES)
            gathers = [pltpu.async_copy(ys_hbm.at[idx_v.at[c, k]], rows_v.at[k], sem_in)
                       for k in range(TOP_K)]
            for cp in gathers:
                cp.wait()
            writes = [pltpu.async_copy(rows_v.at[k], out_hbm.at[k, pl.ds(base, chunk)], sem_out)
                      for k in range(TOP_K)]
            for cp in writes:
                cp.wait()

    return run(ys, pos_c)


def kernel(x_prompt, x_sample, state_conv, c_prompt, c_sample, w_ada, b_ada, w_in, w_conv, b_conv,
           gn_g, gn_b, sgu_ln_g, sgu_ln_b, w_s, b_s, beta_a, beta_b, w_out, w_router, b_router,
           w1, b1, w2, b2, final_g):
    n_bp, seq_len, _ = x_prompt.shape
    n_bs, n_pos, _ = x_sample.shape
    depth = w_ada.shape[0]
    n_prompt = n_bp * seq_len
    n_sample = n_bs * n_pos
    t_all = n_prompt + n_sample
    n_tiles = (t_all * TOP_K) // GEMM_TILE + N_EXPERTS
    n_workers = SC_CORES * SC_SUBCORES
    assert t_all % (n_workers * DISPATCH_CHUNK) == 0 and t_all % (n_workers * GATHER_CHUNK) == 0
    assert n_tiles <= PLAN_LANES and t_all % ROUTER_TILE == 0

    x_p = x_prompt.reshape(n_prompt, D_MODEL)
    x_s = x_sample.transpose(1, 0, 2).reshape(n_sample, D_MODEL)
    x_s_block = 0
    moe = None
    mods =_adaln(jnp.concatenate([c_prompt, c_sample], axis=0), w_ada, b_ada)
    state_t = state_conv.transpose(0, 2, 1, 3)
    grp = jnp.arange(C_CONV) // (C_CONV // CONV_GROUPS)
    g_mat = ((grp[:, None] == grp[None, :]).astype(F32) / (C_CONV // CONV_GROUPS)).astype(BF16)

    mod_p = mods[:, :n_bp].reshape(depth, n_bp, 1, N_MOD * D_MODEL)
    mod_s = mods[:, n_bp:]
    gate_p = mod_p[..., 5 * D_MODEL:]
    gate_s = mod_s[..., 5 * D_MODEL:]
    wconv_p = jnp.broadcast_to(
        jnp.pad(w_conv, ((0, 0), (0, CARRY_ROWS - CONV_WIDTH), (0, 0)))[:, :, None, :],
        (depth, CARRY_ROWS, SUBLANES, C_CONV))
    vec = jnp.stack([b_conv, gn_g, gn_b, sgu_ln_g, sgu_ln_b, beta_a, beta_b, jnp.zeros_like(b_conv)],
                    axis=1)
    bs_full = jnp.repeat(b_s.transpose(0, 2, 1), SGU_HEAD_DIM, axis=2)
    wsv = jnp.repeat(w_s[:, :, :n_pos, :n_pos].transpose(0, 2, 3, 1).reshape(depth, n_pos * n_pos, SGU_HEADS),
                     SGU_HEAD_DIM, axis=2)
    bsv = jnp.repeat(b_s[:, :, :n_pos].transpose(0, 2, 1), SGU_HEAD_DIM, axis=2)
    w_in_b = w_in.astype(BF16)
    w_out_b = w_out.astype(BF16)
    w_router_t = w_router.transpose(0, 2, 1)
    b_router_c = b_router.reshape(depth, N_EXPERTS, 1)

    conv_p, conv_s, v_s = [], [], []
    for l in range(depth):
        xmid, h2, logits, cst, a_new, v_new = _mixer(
            l, x_p, x_s, x_s_block, moe, mod_p, mod_s, gate_p, gate_s, w_in_b, w_out_b, wconv_p, vec,
            g_mat, w_s, bs_full, wsv, bsv, w_router_t, b_router_c, state_t, n_bp, seq_len)
        conv_p.append(cst)
        conv_s.append(a_new)
        v_s.append(v_new)

        wgt, pos, plan = _router(logits)
        def per_worker(chunk):
            return pos.reshape(TOP_K, n_workers, -1, chunk).transpose(1, 2, 0, 3)
        xs = _sc_dispatch(h2, per_worker(DISPATCH_CHUNK), n_tiles * GEMM_TILE)
        ys = _experts(xs, plan, w1, b1, w2, b2, l)
        yg = _sc_gather_back(ys, per_worker(GATHER_CHUNK), t_all)
        moe = (yg, wgt)
        x_p = x_s = xmid
        x_s_block = n_prompt // n_sample

    y_p, y_s = _final(xmid, *moe, gate_p[depth - 1, :, 0], gate_s[depth - 1], final_g, n_prompt, seq_len)
    y_prompt = y_p.reshape(n_bp, seq_len, D_MODEL)
    y_sample = y_s.reshape(n_pos, n_bs, D_MODEL).transpose(1, 0, 2)
    new_conv_s = jnp.stack(conv_s, axis=0).transpose(0, 2, 1, 3)
    new_v_s = jnp.stack(v_s, axis=0).reshape(depth, n_pos, n_bs, C_SGU).transpose(0, 2, 1, 3)
    return (y_prompt, y_sample, jnp.stack(conv_p, axis=0), new_conv_s, new_v_s)
```

```python
import functools
import math

import jax
import jax.numpy as jnp
from jax import lax
from jax.experimental import pallas as pl
from jax.experimental.pallas import tpu as pltpu
from jax.experimental.pallas import tpu_sc as plsc

F32 = jnp.float32
BF16 = jnp.bfloat16

D_MODEL = 1024
C_CONV = 512
C_SGU = 512
CONV_GROUPS = 8
CONV_WIDTH = 31
CONV_STATE = CONV_WIDTH - 1
SGU_HEADS = 4
SGU_HEAD_DIM = C_SGU // SGU_HEADS
CHUNK = 128
N_EXPERTS = 32
TOP_K = 4
D_FF = 1024
SWIGLU_LIMIT = 7.0
SWIGLU_ALPHA = 1.702
N_MOD = 6
EPS = 1e-5
LOG2_E = 1.0 / math.log(2.0)

ROW_TILE = 512
ROUTER_TILE = 1536
PREFIX_BLOCK = 256
ADALN_COLS = 1024
ROW_CHUNK = 128
CONV_CHUNK = 32
COMBINE_CHUNK = 64
CARRY_ROWS = 32
SUBLANES = 8
LANES = 128
PLAN_ROWS = 8
PLAN_LANES = 256
PLAN_EXPERT, PLAN_VALID, PLAN_FIRST, PLAN_SLOT, PLAN_NEXT = range(5)
GEMM_TILE = 512
FF_BLOCK = 256
TILES_PER_STEP = 4
VMEM_LIMIT = 56 * 1024 * 1024
D_WORDS = D_MODEL // 2
DISPATCH_CHUNK = 88
GATHER_CHUNK = 48
HI_MASK = -65536
SC_CORES = 2
SC_SUBCORES = 16


def _rms(x):
    return x * lax.rsqrt(jnp.mean(x * x, axis=-1, keepdims=True) + EPS)


def _gelu(x):
    return 0.5 * x * (1.0 + lax.erf(x * (1.0 / math.sqrt(2.0))))


def _sigmoid(x, scale=1.0):
    return 1.0 / (1.0 + jnp.exp2(x * (-scale * LOG2_E)))


def _split_bf16(x):
    hi = x.astype(BF16)
    lo = (x - hi.astype(F32)).astype(BF16)
    return hi, lo


def _pack_bf16_pair(x):
    bits = lax.bitcast_convert_type(x.astype(BF16).astype(F32), jnp.int32)
    return lax.shift_right_logical(bits[:, :D_WORDS], 16) | (bits[:, D_WORDS:] & HI_MASK)


def _unpack_bf16_pair(w):
    left = lax.bitcast_convert_type(lax.shift_left(w, 16), F32)
    right = lax.bitcast_convert_type(w & HI_MASK, F32)
    return left, right


def _dot(a, b):
    return jnp.dot(a, b, preferred_element_type=F32)


def _dot_nt(a, b):
    return lax.dot_general(a, b, (((1,), (1,)), ((), ())), preferred_element_type=F32)


def _conv_tap(wconv_ref, k):
    return jnp.concatenate([wconv_ref[k]] * (CONV_CHUNK // SUBLANES), axis=0)


def _static_loop(n_rows, body, chunk=None):
    chunk = ROW_CHUNK if chunk is None else chunk
    for r0 in range(0, n_rows, chunk):
        body(r0)


def _adaln_kernel(c_ref, w_ref, b_ref, o_ref):
    c = c_ref[...]
    s_hi, s_lo = _split_bf16(c * _sigmoid(c))
    w_hi, w_lo = _split_bf16(w_ref[0])
    acc = _dot(s_hi, w_hi) + _dot(s_hi, w_lo) + _dot(s_lo, w_hi)
    o_ref[0] = acc + b_ref[0]


def _adaln(c_all, w_ada, b_ada):
    depth, _, n_out = w_ada.shape
    n_rows = c_all.shape[0]
    tn = ADALN_COLS
    return pl.pallas_call(
        _adaln_kernel,
        grid=(depth, n_out // tn),
        in_specs=[
            pl.BlockSpec((n_rows, D_MODEL), lambda l, j: (0, 0)),
            pl.BlockSpec((1, D_MODEL, tn), lambda l, j: (l, 0, j)),
            pl.BlockSpec((1, 1, tn), lambda l, j: (l, 0, j)),
        ],
        out_specs=pl.BlockSpec((1, n_rows, tn), lambda l, j: (l, 0, j)),
        out_shape=jax.ShapeDtypeStruct((depth, n_rows, n_out), F32),
        compiler_params=pltpu.CompilerParams(
            dimension_semantics=("arbitrary", "arbitrary"), vmem_limit_bytes=VMEM_LIMIT),
        name="adaln",
    )(c_all, w_ada, b_ada.reshape(depth, 1, n_out))


def _phase_in_norm(x_ref, h_scr, mod, n_rows):
    def body(r0):
        r = pl.ds(r0, ROW_CHUNK)
        h = _rms(x_ref[r, :]) * (1.0 + mod(1, r0)) + mod(0, r0)
        h_scr[r, :] = h.astype(BF16)
    _static_loop(n_rows, body)


def _phase_glu(z_scr, a_dst, a_off, n_rows):
    def body(r0):
        r = pl.ds(r0, ROW_CHUNK)
        a_dst[pl.ds(r0 + a_off, ROW_CHUNK), :] = z_scr[r, 0:C_CONV] * _sigmoid(z_scr[r, C_CONV:2 * C_CONV])
    _static_loop(n_rows, body)


def _phase_uv(z_scr, vec_ref, u_scr, v_scr, vout_ref, n_rows):
    ln_g = vec_ref[3:4, :]
    ln_b = vec_ref[4:5, :]

    def body(r0):
        r = pl.ds(r0, ROW_CHUNK)
        u_scr[r, :] = _gelu(z_scr[r, 2 * C_CONV:2 * C_CONV + C_SGU])
        gv = _gelu(z_scr[r, 2 * C_CONV + C_SGU:2 * C_CONV + 2 * C_SGU])
        mu = jnp.mean(gv, axis=-1, keepdims=True)
        dv = gv - mu
        var = jnp.mean(dv * dv, axis=-1, keepdims=True)
        v = dv * lax.rsqrt(var + EPS) * ln_g + ln_b
        if vout_ref is not None:
            vout_ref[r, :] = v
        v_scr[r, :] = v.astype(v_scr.dtype)
    _static_loop(n_rows, body)


def _phase_group_norm(conv_scr, convb_scr, stat_scr, g_ref, vec_ref, y_scr, n_rows):
    stat_scr[...] = _dot(convb_scr[...], g_ref[...])

    def center(r0):
        r = pl.ds(r0, ROW_CHUNK)
        d = conv_scr[r, :] - stat_scr[r, :]
        conv_scr[r, :] = d
        convb_scr[r, :] = (d * d).astype(BF16)
    _static_loop(n_rows, center)
    stat_scr[...] = _dot(convb_scr[...], g_ref[...])
    gn_g = vec_ref[1:2, :]
    gn_b = vec_ref[2:3, :]
    beta_a = vec_ref[5:6, :]

    def finish(r0):
        r = pl.ds(r0, ROW_CHUNK)
        gn = conv_scr[r, :] * lax.rsqrt(stat_scr[r, :] + EPS) * gn_g + gn_b
        ya = gn * _sigmoid(gn)
        y_scr[r, 0:C_CONV] = (_rms(ya) * beta_a).astype(BF16)
    _static_loop(n_rows, finish)


def _phase_out(x_ref, y_scr, wout_ref, z_scr, wr_ref, br_ref, mod, h_scr,
               xmid_ref, h2_ref, logit_ref, n_rows):
    z_scr[:, 0:D_MODEL] = (_dot(y_scr[:, 0:C_CONV], wout_ref[0:C_CONV, :])
                           + _dot(y_scr[:, C_CONV:C_CONV + C_SGU], wout_ref[C_CONV:C_CONV + C_SGU, :]))

    def body(r0):
        r = pl.ds(r0, ROW_CHUNK)
        xm = x_ref[r, :] + mod(2, r0) * z_scr[r, 0:D_MODEL]
        xmid_ref[r, :] = xm
        h2 = _rms(xm) * (1.0 + mod(4, r0)) + mod(3, r0)
        hi, lo = _split_bf16(h2)
        h2_ref[r, :] = _pack_bf16_pair(h2)
        y_scr[r, :] = hi
        h_scr[r, :] = lo
    _static_loop(n_rows, body)
    w_hi, w_lo = _split_bf16(wr_ref[...])
    both = _dot_nt(jnp.concatenate([w_hi, w_lo], axis=0), y_scr[...])
    logit_ref[...] = (both[0:N_EXPERTS] + both[N_EXPERTS:2 * N_EXPERTS]
                      + _dot_nt(w_hi, h_scr[...])) + br_ref[...]


def _moe_half(yg_ref, w, r, hi_half):
    moe = jnp.zeros((COMBINE_CHUNK, D_WORDS), F32)
    for k in range(TOP_K):
        word = yg_ref[k, r, :]
        bits = (word & HI_MASK) if hi_half else lax.shift_left(word, 16)
        moe = moe + w[:, k:k + 1] * lax.bitcast_convert_type(bits, F32)
    return moe


def _phase_moe_residual(xmid_ref, yg_ref, wgt_ref, gate, x_dst, n_rows):
    def body(r0):
        r = pl.ds(r0, COMBINE_CHUNK)
        w = wgt_ref[r, :]
        g = gate(r0)
        for hi_half in (False, True):
            cols = slice(D_WORDS, D_MODEL) if hi_half else slice(0, D_WORDS)
            x_dst[r, cols] = xmid_ref[r, cols] + g[:, cols] * _moe_half(yg_ref, w, r, hi_half)
    _static_loop(n_rows, body, COMBINE_CHUNK)


def _mixer_prompt_kernel(*refs, tiles_per_seq, n_prompt_tiles, fused):
    x_ref, mod_ref = refs[:2]
    refs = refs[2:]
    if fused:
        yg_ref, wgt_ref, gate_ref = refs[:3]
        refs = refs[3:]
        x_scr = refs[-1]
        refs = refs[:-1]
    (win_ref, wout_ref, wconv_ref, vec_ref, g_ref, ws_ref, bs_ref, wr_ref, br_ref,
     smid_ref, sh2_ref, slogit_ref, xmid_ref, h2_ref, logit_ref, cst_ref,
     h_scr, z_scr, aext_scr, conv_scr, convb_scr, stat_scr, u_scr, v_scr, y_scr) = refs
    i = pl.program_id(0)

    @pl.when(i == n_prompt_tiles)
    def _():
        xmid_ref[...] = smid_ref[...]
        h2_ref[...] = sh2_ref[...]
        logit_ref[...] = slogit_ref[...]

    @pl.when(i < n_prompt_tiles)
    def _():
        x_src = x_ref
        if fused:
            _phase_moe_residual(x_ref, yg_ref, wgt_ref, lambda r0: gate_ref[0], x_scr, ROW_TILE)
            x_src = x_scr
        _mixer_prompt_tile(lax.rem(i, tiles_per_seq), tiles_per_seq,
                           x_src, mod_ref, win_ref, wout_ref, wconv_ref, vec_ref, g_ref,
                           ws_ref, bs_ref, wr_ref, br_ref, xmid_ref, h2_ref, logit_ref, cst_ref,
                           h_scr, z_scr, aext_scr, conv_scr, convb_scr, stat_scr, u_scr, v_scr, y_scr)


def _mixer_prompt_tile(j, tiles_per_seq, x_ref, mod_ref, win_ref, wout_ref, wconv_ref, vec_ref, g_ref,
                       ws_ref, bs_ref, wr_ref, br_ref, xmid_ref, h2_ref, logit_ref, cst_ref,
                       h_scr, z_scr, aext_scr, conv_scr, convb_scr, stat_scr, u_scr, v_scr, y_scr):
    n_rows = ROW_TILE

    def mod(idx, r0):
        del r0
        return mod_ref[0, :, idx * D_MODEL:(idx + 1) * D_MODEL]

    a_buf = aext_scr.at[0]

    @pl.when(j == 0)
    def _():
        a_buf[0:CARRY_ROWS, :] = jnp.zeros((CARRY_ROWS, C_CONV), F32)

    _phase_in_norm(x_ref, h_scr, mod, n_rows)
    n_glu = 2 * C_CONV
    z_scr[:, 0:n_glu] = _dot(h_scr[...], win_ref[:, 0:n_glu])
    _phase_glu(z_scr, a_buf, CARRY_ROWS, n_rows)
    z_scr[:, n_glu:] = _dot(h_scr[...], win_ref[:, n_glu:])
    n_shift = n_rows + CARRY_ROWS - SUBLANES
    for s in range(1, SUBLANES):
        aext_scr[s, 0:n_shift, :] = a_buf[s:s + n_shift, :]

    b_conv = vec_ref[0:1, :]
    lead = CARRY_ROWS - CONV_STATE
    for r0 in range(0, n_rows, CONV_CHUNK):
        acc = jnp.zeros((CONV_CHUNK, C_CONV), F32)
        for k in range(CONV_WIDTH):
            s = (lead + k) % SUBLANES
            q0 = r0 + (lead + k - s)
            acc = acc + _conv_tap(wconv_ref, k) * aext_scr[s, q0:q0 + CONV_CHUNK, :]
        conv = acc + b_conv
        conv_scr[r0:r0 + CONV_CHUNK, :] = conv
        convb_scr[r0:r0 + CONV_CHUNK, :] = conv.astype(BF16)
    _phase_uv(z_scr, vec_ref, u_scr, v_scr, None, n_rows)
    _phase_group_norm(conv_scr, convb_scr, stat_scr, g_ref, vec_ref, y_scr, n_rows)

    row_i = lax.broadcasted_iota(jnp.int32, (CHUNK, CHUNK), 0)
    col_i = lax.broadcasted_iota(jnp.int32, (CHUNK, CHUNK), 1)
    tril = (col_i <= row_i).astype(F32)
    for hd in range(SGU_HEADS):
        ws_h = (ws_ref[hd] * tril).astype(BF16)
        cols = slice(hd * SGU_HEAD_DIM, (hd + 1) * SGU_HEAD_DIM)
        s_cols = slice(n_glu + hd * SGU_HEAD_DIM, n_glu + (hd + 1) * SGU_HEAD_DIM)
        for c in range(n_rows // CHUNK):
            rows = slice(c * CHUNK, (c + 1) * CHUNK)
            z_scr[rows, s_cols] = _dot(ws_h, v_scr[rows, cols])
    beta_b = vec_ref[6:7, :]

    def gate_body(r0):
        r = pl.ds(r0, ROW_CHUNK)
        rb = pl.ds(r0 % CHUNK, ROW_CHUNK)
        yb = u_scr[r, :] * (z_scr[r, n_glu:n_glu + C_SGU] + bs_ref[rb, :])
        y_scr[r, C_CONV:C_CONV + C_SGU] = (_rms(yb) * beta_b).astype(BF16)
    _static_loop(n_rows, gate_body)

    _phase_out(x_ref, y_scr, wout_ref, z_scr, wr_ref, br_ref, mod, h_scr,
               xmid_ref, h2_ref, logit_ref, n_rows)

    @pl.when(j == tiles_per_seq - 1)
    def _():
        cst_ref[0] = a_buf[n_rows + lead:n_rows + CARRY_ROWS, :]

    a_buf[0:CARRY_ROWS, :] = a_buf[n_rows:n_rows + CARRY_ROWS, :]


def _mixer_sample_kernel(*refs, fused):
    x_ref, mod_ref = refs[:2]
    refs = refs[2:]
    if fused:
        yg_ref, wgt_ref, gate_ref = refs[:3]
        refs = refs[3:]
        x_scr = refs[-1]
        refs = refs[:-1]
    (win_ref, wout_ref, wconv_ref, vec_ref, g_ref, wsv_ref, bsv_ref, wr_ref, br_ref, state_ref,
     xmid_ref, h2_ref, logit_ref, cnew_ref, vout_ref,
     h_scr, z_scr, conv_scr, convb_scr, stat_scr, u_scr, v_scr, y_scr, anew_ref) = refs
    n_rows = x_ref.shape[0]
    n_seq = state_ref.shape[1]
    n_pos = n_rows // n_seq

    def mod(idx, r0):
        return mod_ref[pl.ds(r0 % n_seq, ROW_CHUNK), idx * D_MODEL:(idx + 1) * D_MODEL]

    if fused:
        def gate(r0):
            return gate_ref[pl.ds(r0 % n_seq, COMBINE_CHUNK), :]
        _phase_moe_residual(x_ref, yg_ref, wgt_ref, gate, x_scr, n_rows)
        x_ref = x_scr

    _phase_in_norm(x_ref, h_scr, mod, n_rows)
    z_scr[...] = _dot(h_scr[...], win_ref[...])
    _phase_glu(z_scr, anew_ref, 0, n_rows)
    _phase_uv(z_scr, vec_ref, u_scr, v_scr, vout_ref, n_rows)

    keep = CONV_STATE - n_pos
    cnew_ref[0:keep] = state_ref[n_pos:CONV_STATE]
    for t in range(n_pos):
        cnew_ref[keep + t] = anew_ref[t * n_seq:(t + 1) * n_seq, :]

    b_conv = vec_ref[0:1, :]

    def conv_body(s0):
        rs = pl.ds(s0, CONV_CHUNK)
        for t in range(n_pos):
            acc = jnp.zeros((CONV_CHUNK, C_CONV), F32)
            for k in range(CONV_WIDTH):
                p = t + k
                if p < CONV_STATE:
                    src = state_ref[p, rs, :]
                else:
                    src = anew_ref[pl.ds(s0 + (p - CONV_STATE) * n_seq, CONV_CHUNK), :]
                acc = acc + _conv_tap(wconv_ref, k) * src
            conv = acc + b_conv
            r = pl.ds(s0 + t * n_seq, CONV_CHUNK)
            conv_scr[r, :] = conv
            convb_scr[r, :] = conv.astype(BF16)
    _static_loop(n_seq, conv_body, CONV_CHUNK)

    _phase_group_norm(conv_scr, convb_scr, stat_scr, g_ref, vec_ref, y_scr, n_rows)

    beta_b = vec_ref[6:7, :]

    def gate_body(s0):
        for t in range(n_pos):
            s = jnp.zeros((CONV_CHUNK, C_SGU), F32)
            for jj in range(t + 1):
                s = s + wsv_ref[t * n_pos + jj:t * n_pos + jj + 1, :] * v_scr[pl.ds(s0 + jj * n_seq, CONV_CHUNK), :]
            r = pl.ds(s0 + t * n_seq, CONV_CHUNK)
            yb = u_scr[r, :] * (s + bsv_ref[t:t + 1, :])
            y_scr[r, C_CONV:C_CONV + C_SGU] = (_rms(yb) * beta_b).astype(BF16)
    _static_loop(n_seq, gate_body, CONV_CHUNK)

    _phase_out(x_ref, y_scr, wout_ref, z_scr, wr_ref, br_ref, mod, h_scr,
               xmid_ref, h2_ref, logit_ref, n_rows)


def _const_spec(shape):
    nd = len(shape)
    return pl.BlockSpec(shape, lambda *_: (0,) * nd, pipeline_mode=pl.Buffered(1))


def _const_out_spec(shape):
    nd = len(shape)
    return pl.BlockSpec(shape, lambda *_: (0,) * nd)


def _layer_spec(shape, layer):
    nd = len(shape)
    return pl.BlockSpec((None,) + tuple(shape), lambda *_: (layer,) + (0,) * nd,
                        pipeline_mode=pl.Buffered(1))


def _mixer(layer, x_p, x_s, x_s_block, moe, mod_p, mod_s, gate_p, gate_s, w_in_b, w_out_b, wconv_p, vec,
           g_mat, w_s, bs_full, wsv, bsv, w_router, b_router, state_t, n_prompt_seq, seq_len):
    fused = moe is not None
    n_prompt = n_prompt_seq * seq_len
    n_seq, n_pos = state_t.shape[2], bsv.shape[1]
    n_sample = n_seq * n_pos
    t_all = n_prompt + n_sample
    nj = seq_len // ROW_TILE
    weight_specs = [
        _layer_spec((D_MODEL, 2 * C_CONV + 2 * C_SGU), layer),
        _layer_spec((D_MODEL, D_MODEL), layer),
        _layer_spec((CARRY_ROWS, SUBLANES, C_CONV), layer),
        _layer_spec((SUBLANES, C_CONV), layer),
        _const_spec((C_CONV, C_CONV)),
    ]
    router_specs = [_layer_spec((N_EXPERTS, D_MODEL), layer), _layer_spec((N_EXPERTS, 1), layer)]
    common_scratch = lambda n: [
        pltpu.VMEM((n, D_MODEL), BF16),
        pltpu.VMEM((n, 2 * C_CONV + 2 * C_SGU), F32),
    ]
    tail_scratch = lambda n, vdt: [
        pltpu.VMEM((n, C_CONV), F32),
        pltpu.VMEM((n, C_CONV), BF16),
        pltpu.VMEM((n, C_CONV), F32),
        pltpu.VMEM((n, C_SGU), F32),
        pltpu.VMEM((n, C_SGU), vdt),
        pltpu.VMEM((n, D_MODEL), BF16),
    ]
    assert n_sample == ROW_TILE
    n_tiles_p = n_prompt // ROW_TILE
    tile_p = lambda i: jnp.minimum(i, n_tiles_p - 1)
    seq_of = lambda i: jnp.minimum(i // nj, n_prompt_seq - 1)
    moe_s_specs, moe_p_specs, moe_s_args, moe_p_args, x_scratch = [], [], [], [], []
    if fused:
        yg, wgt = moe
        moe_s_specs = [pl.BlockSpec((TOP_K, n_sample, D_WORDS), lambda i: (0, x_s_block, 0)),
                       pl.BlockSpec((n_sample, LANES), lambda i: (x_s_block, 0)),
                       _layer_spec((n_seq, D_MODEL), layer - 1)]
        moe_p_specs = [pl.BlockSpec((TOP_K, ROW_TILE, D_WORDS), lambda i: (0, tile_p(i), 0)),
                       pl.BlockSpec((ROW_TILE, LANES), lambda i: (tile_p(i), 0)),
                       pl.BlockSpec((None, 1, 1, D_MODEL), lambda i: (layer - 1, seq_of(tile_p(i)), 0, 0))]
        moe_s_args = [yg, wgt, gate_s]
        moe_p_args = [yg, wgt, gate_p]
        x_scratch = [pltpu.VMEM((ROW_TILE, D_MODEL), F32)]
    smid, sh2, slogits, a_new, v_new = pl.pallas_call(
        functools.partial(_mixer_sample_kernel, fused=fused),
        grid=(1,),
        in_specs=[
            pl.BlockSpec((n_sample, D_MODEL), lambda i: (x_s_block, 0)),
            _layer_spec((n_seq, N_MOD * D_MODEL), layer),
            *moe_s_specs,
            *weight_specs,
            _layer_spec((n_pos * n_pos, C_SGU), layer),
            _layer_spec((n_pos, C_SGU), layer),
            *router_specs,
            _layer_spec((CONV_STATE, n_seq, C_CONV), layer),
        ],
        out_specs=(
            _const_out_spec((n_sample, D_MODEL)),
            _const_out_spec((n_sample, D_WORDS)),
            _const_out_spec((N_EXPERTS, n_sample)),
            _const_out_spec((CONV_STATE, n_seq, C_CONV)),
            _const_out_spec((n_sample, C_SGU)),
        ),
        out_shape=(
            jax.ShapeDtypeStruct((n_sample, D_MODEL), F32),
            jax.ShapeDtypeStruct((n_sample, D_WORDS), jnp.int32),
            jax.ShapeDtypeStruct((N_EXPERTS, n_sample), F32),
            jax.ShapeDtypeStruct((CONV_STATE, n_seq, C_CONV), F32),
            jax.ShapeDtypeStruct((n_sample, C_SGU), F32),
        ),
        scratch_shapes=common_scratch(n_sample) + tail_scratch(n_sample, F32)
        + [pltpu.VMEM((n_sample, C_CONV), F32)] + x_scratch,
        compiler_params=pltpu.CompilerParams(
            dimension_semantics=("arbitrary",), vmem_limit_bytes=VMEM_LIMIT),
        name="mixer_sample",
    )(x_s, mod_s, *moe_s_args, w_in_b, w_out_b, wconv_p, vec, g_mat, wsv, bsv, w_router, b_router,
      state_t)

    xmid, h2, logits, cst = pl.pallas_call(
        functools.partial(_mixer_prompt_kernel, tiles_per_seq=nj, n_prompt_tiles=n_tiles_p,
                          fused=fused),
        grid=(n_tiles_p + 1,),
        in_specs=[
            pl.BlockSpec((ROW_TILE, D_MODEL), lambda i: (tile_p(i), 0)),
            pl.BlockSpec((None, 1, 1, N_MOD * D_MODEL), lambda i: (layer, seq_of(i), 0, 0)),
            *moe_p_specs,
            *weight_specs,
            _layer_spec((SGU_HEADS, CHUNK, CHUNK), layer),
            _layer_spec((CHUNK, C_SGU), layer),
            *router_specs,
            _const_spec((n_sample, D_MODEL)),
            _const_spec((n_sample, D_WORDS)),
            _const_spec((N_EXPERTS, n_sample)),
        ],
        out_specs=(
            pl.BlockSpec((ROW_TILE, D_MODEL), lambda i: (i, 0)),
            pl.BlockSpec((ROW_TILE, D_WORDS), lambda i: (i, 0)),
            pl.BlockSpec((N_EXPERTS, ROW_TILE), lambda i: (0, i)),
            pl.BlockSpec((1, CONV_STATE, C_CONV), lambda i: (seq_of(i), 0, 0)),
        ),
        out_shape=(
            jax.ShapeDtypeStruct((t_all, D_MODEL), F32),
            jax.ShapeDtypeStruct((t_all, D_WORDS), jnp.int32),
            jax.ShapeDtypeStruct((N_EXPERTS, t_all), F32),
            jax.ShapeDtypeStruct((n_prompt_seq, CONV_STATE, C_CONV), F32),
        ),
        scratch_shapes=common_scratch(ROW_TILE)
        + [pltpu.VMEM((SUBLANES, CARRY_ROWS + ROW_TILE, C_CONV), F32)]
        + tail_scratch(ROW_TILE, BF16) + x_scratch,
        compiler_params=pltpu.CompilerParams(
            dimension_semantics=("arbitrary",), vmem_limit_bytes=VMEM_LIMIT),
        name="mixer_prompt",
    )(x_p, mod_p, *moe_p_args, w_in_b, w_out_b, wconv_p, vec, g_mat, w_s, bs_full, w_router, b_router,
      smid, sh2, slogits)
    return xmid, h2, logits, cst, a_new, v_new


def _tile_plan(cnt):
    e_sub = lax.broadcasted_iota(jnp.int32, (N_EXPERTS, N_EXPERTS), 0)
    e_lane = lax.broadcasted_iota(jnp.int32, (N_EXPERTS, N_EXPERTS), 1)
    tiles = jnp.floor((cnt + (GEMM_TILE - 1.0)) * (1.0 / GEMM_TILE))
    active = jnp.where(cnt > 0.0, 1.0, 0.0)

    def over_experts(tri, col):
        return _dot(tri.astype(BF16), jnp.broadcast_to(col, (N_EXPERTS, LANES)).astype(BF16))[:, 0:1]
    tile_end = over_experts(jnp.where(e_lane <= e_sub, 1.0, 0.0), tiles)
    tile_start = tile_end - tiles
    ordinal = over_experts(jnp.where(e_lane < e_sub, 1.0, 0.0), active)

    g = lax.broadcasted_iota(jnp.int32, (1, PLAN_LANES), 1).astype(F32)
    e_id = lax.broadcasted_iota(jnp.int32, (N_EXPERTS, PLAN_LANES), 0).astype(F32)
    done = jnp.sum(jnp.where(tile_end <= g, 1.0, 0.0), axis=0, keepdims=True)
    live = done < float(N_EXPERTS)
    last_e = jnp.max(jnp.where(cnt > 0.0, e_id[:, 0:1], 0.0), axis=0, keepdims=True)
    te = jnp.where(live, done, last_e)
    pick = e_id == te

    def at_tile(col):
        return jnp.sum(jnp.where(pick, col, 0.0), axis=0, keepdims=True)
    start_g = at_tile(tile_start)
    ord_g = at_tile(ordinal)
    valid = jnp.clip(at_tile(cnt) - (g - start_g) * GEMM_TILE, 0.0, float(GEMM_TILE))
    valid = jnp.where(live, valid, 0.0)
    first = jnp.where(live, jnp.where(g == start_g, 1.0, 0.0), 0.0)
    slot = ord_g - 2.0 * jnp.floor(ord_g * 0.5)
    later = jnp.where(e_id > te, jnp.where(cnt > 0.0, e_id, float(N_EXPERTS)), float(N_EXPERTS))
    nxt = jnp.min(later, axis=0, keepdims=True)
    nxt = jnp.where(nxt >= float(N_EXPERTS), -1.0, nxt)

    row = lax.broadcasted_iota(jnp.int32, (PLAN_ROWS, PLAN_LANES), 0)
    plan = jnp.zeros((PLAN_ROWS, PLAN_LANES), F32)
    for r, v in ((PLAN_EXPERT, te), (PLAN_VALID, valid), (PLAN_FIRST, first), (PLAN_SLOT, slot),
                 (PLAN_NEXT, nxt)):
        plan = jnp.where(row == r, v, plan)
    return plan.astype(jnp.int32), tile_start * GEMM_TILE


def _router_kernel(lg_ref, wgt_ref, pos_ref, plan_ref, cnt_scr, run_scr, start_scr, earlier_scr):
    ph = pl.program_id(0)
    i = pl.program_id(1)
    n = lg_ref.shape[1]

    @pl.when(jnp.logical_and(ph == 0, i == 0))
    def _():
        cnt_scr[...] = jnp.zeros_like(cnt_scr)

    lg = lg_ref[...]
    sub = lax.broadcasted_iota(jnp.int32, lg.shape, 0).astype(F32)
    vals, sels = [], []
    for _ in range(TOP_K):
        m = jnp.max(lg, axis=0, keepdims=True)
        idx = jnp.min(jnp.where(lg == m, sub, float(N_EXPERTS)), axis=0, keepdims=True)
        sel = sub == idx
        vals.append(m)
        sels.append(sel)
        lg = jnp.where(sel, -jnp.inf, lg)
    onehot = jnp.zeros(lg.shape, F32)
    for sel in sels:
        onehot = onehot + jnp.where(sel, 1.0, 0.0)
    tile_cnt = jnp.sum(onehot, axis=1, keepdims=True)

    @pl.when(ph == 0)
    def _():
        cnt_scr[...] = cnt_scr[...] + tile_cnt

    @pl.when(jnp.logical_and(ph == 1, i == 0))
    def _():
        plan, row_start = _tile_plan(cnt_scr[...])
        plan_ref[...] = plan
        start_scr[...] = row_start
        run_scr[...] = jnp.zeros_like(run_scr)
        t_row = lax.broadcasted_iota(jnp.int32, (PREFIX_BLOCK, PREFIX_BLOCK), 0)
        t_col = lax.broadcasted_iota(jnp.int32, (PREFIX_BLOCK, PREFIX_BLOCK), 1)
        earlier_scr[...] = jnp.where(t_row < t_col, 1.0, 0.0).astype(BF16)

    @pl.when(ph == 1)
    def _():
        exps = [jnp.exp(v - vals[0]) for v in vals]
        inv = 1.0 / (exps[0] + exps[1] + exps[2] + exps[3])
        w_row = lax.broadcasted_iota(jnp.int32, (LANES, n), 0)
        w_all = jnp.zeros((LANES, n), F32)
        for k in range(TOP_K):
            w_all = jnp.where(w_row == k, exps[k] * inv, w_all)
        wgt_ref[...] = w_all.T
        base = run_scr[...] + start_scr[...]
        onehot_b = onehot.astype(BF16)
        for b0 in range(0, n, PREFIX_BLOCK):
            blk = slice(b0, b0 + PREFIX_BLOCK)
            slot0 = _dot(onehot_b[:, blk], earlier_scr[...]) + base
            for k in range(TOP_K):
                pos_k = jnp.sum(jnp.where(sels[k][:, blk], slot0, 0.0), axis=0, keepdims=True)
                pos_ref[k:k + 1, blk] = pos_k.astype(jnp.int32)
            base = base + jnp.sum(onehot[:, blk], axis=1, keepdims=True)
        run_scr[...] = run_scr[...] + tile_cnt


def _router(logits):
    t_all = logits.shape[1]
    col = pltpu.VMEM((N_EXPERTS, 1), F32)
    return pl.pallas_call(
        _router_kernel,
        grid=(2, t_all // ROUTER_TILE),
        in_specs=[pl.BlockSpec((N_EXPERTS, ROUTER_TILE), lambda ph, i: (0, i))],
        out_specs=(
            pl.BlockSpec((ROUTER_TILE, LANES), lambda ph, i: (i * ph, 0)),
            pl.BlockSpec((TOP_K, ROUTER_TILE), lambda ph, i: (0, i * ph)),
            _const_out_spec((PLAN_ROWS, PLAN_LANES)),
        ),
        out_shape=(
            jax.ShapeDtypeStruct((t_all, LANES), F32),
            jax.ShapeDtypeStruct((TOP_K, t_all), jnp.int32),
            jax.ShapeDtypeStruct((PLAN_ROWS, PLAN_LANES), jnp.int32),
        ),
        scratch_shapes=[col, col, col, pltpu.VMEM((PREFIX_BLOCK, PREFIX_BLOCK), BF16)],
        compiler_params=pltpu.CompilerParams(
            dimension_semantics=("arbitrary", "arbitrary"), vmem_limit_bytes=VMEM_LIMIT),
        name="router",
    )(logits)


def _expert_kernel(plan_ref, x_ref, w1_hbm, b1_ref, w2_hbm, b2_ref, o_ref,
                   w1f_scr, w2f_scr, w1b_scr, w2b_scr, sem, *, layer):
    def weight_copies(expert, slot):
        return (pltpu.make_async_copy(w1_hbm.at[layer, expert], w1f_scr.at[slot], sem.at[0, slot]),
                pltpu.make_async_copy(w2_hbm.at[layer, expert], w2f_scr.at[slot], sem.at[1, slot]))

    for sub in range(TILES_PER_STEP):
        _expert_tile(pl.program_id(0) * TILES_PER_STEP + sub, sub * GEMM_TILE, plan_ref, x_ref,
                     b1_ref, b2_ref, o_ref, w1f_scr, w2f_scr, w1b_scr, w2b_scr, weight_copies)


def _expert_tile(g, row0, plan_ref, x_ref, b1_ref, b2_ref, o_ref, w1f_scr, w2f_scr, w1b_scr, w2b_scr,
                 weight_copies):
    nv = plan_ref[PLAN_VALID, g]
    expert = plan_ref[PLAN_EXPERT, g]

    @pl.when(plan_ref[PLAN_FIRST, g] == 1)
    def _():
        slot = plan_ref[PLAN_SLOT, g]
        nxt = plan_ref[PLAN_NEXT, g]

        @pl.when(g == 0)
        def _():
            for cp in weight_copies(expert, slot):
                cp.start()
        for cp in weight_copies(expert, slot):
            cp.wait()

        @pl.when(nxt >= 0)
        def _():
            for cp in weight_copies(nxt, 1 - slot):
                cp.start()
        w1b_scr[...] = w1f_scr[slot].astype(BF16)
        w2b_scr[...] = w2f_scr[slot].astype(BF16)

    def ffn(m):
        x_left, x_right = _unpack_bf16_pair(x_ref[row0:row0 + m, :])
        x = jnp.concatenate([x_left.astype(BF16), x_right.astype(BF16)], axis=1)

        def hidden(col0):
            cols = slice(col0, col0 + FF_BLOCK)
            return _dot(x, w1b_scr[:, cols]) + b1_ref[expert, :, cols]

        acts = []
        for jb in range(D_FF // FF_BLOCK):
            gg = jnp.minimum(hidden(jb * FF_BLOCK), SWIGLU_LIMIT)
            up = jnp.clip(hidden(D_FF + jb * FF_BLOCK), -SWIGLU_LIMIT, SWIGLU_LIMIT)
            acts.append((gg * _sigmoid(gg, SWIGLU_ALPHA) * (up + 1.0)).astype(BF16))
        y = _dot(jnp.concatenate(acts, axis=1), w2b_scr[...]) + b2_ref[expert]
        row = lax.broadcasted_iota(jnp.int32, y.shape, 0)
        o_ref[row0:row0 + m, :] = _pack_bf16_pair(jnp.where(row < nv, y, 0.0))
        if m < GEMM_TILE:
            o_ref[row0 + m:row0 + GEMM_TILE, :] = jnp.zeros((GEMM_TILE - m, D_WORDS), jnp.int32)

    @pl.when(nv > GEMM_TILE // 2)
    def _():
        ffn(GEMM_TILE)

    @pl.when(jnp.logical_and(nv > 0, nv <= GEMM_TILE // 2))
    def _():
        ffn(GEMM_TILE // 2)

    @pl.when(nv == 0)
    def _():
        o_ref[row0:row0 + GEMM_TILE, :] = jnp.zeros((GEMM_TILE, D_WORDS), jnp.int32)


def _experts(xs, plan, w1, b1, w2, b2, layer):
    n_tiles = xs.shape[0] // GEMM_TILE
    depth = w1.shape[0]
    assert n_tiles % TILES_PER_STEP == 0
    step_rows = TILES_PER_STEP * GEMM_TILE
    bias_map = lambda s, plan: (layer, 0, 0, 0)
    grid_spec = pltpu.PrefetchScalarGridSpec(
        num_scalar_prefetch=1,
        grid=(n_tiles // TILES_PER_STEP,),
        in_specs=[
            pl.BlockSpec((step_rows, D_WORDS), lambda s, plan: (s, 0)),
            pl.BlockSpec(memory_space=pl.ANY),
            pl.BlockSpec((None, N_EXPERTS, 1, 2 * D_FF), bias_map, pipeline_mode=pl.Buffered(1)),
            pl.BlockSpec(memory_space=pl.ANY),
            pl.BlockSpec((None, N_EXPERTS, 1, D_MODEL), bias_map, pipeline_mode=pl.Buffered(1)),
        ],
        out_specs=pl.BlockSpec((step_rows, D_WORDS), lambda s, plan: (s, 0)),
        scratch_shapes=[
            pltpu.VMEM((2, D_MODEL, 2 * D_FF), F32),
            pltpu.VMEM((2, D_FF, D_MODEL), F32),
            pltpu.VMEM((D_MODEL, 2 * D_FF), BF16),
            pltpu.VMEM((D_FF, D_MODEL), BF16),
            pltpu.SemaphoreType.DMA((2, 2)),
        ],
    )
    return pl.pallas_call(
        functools.partial(_expert_kernel, layer=layer),
        grid_spec=grid_spec,
        out_shape=jax.ShapeDtypeStruct((n_tiles * GEMM_TILE, D_WORDS), jnp.int32),
        compiler_params=pltpu.CompilerParams(
            dimension_semantics=("arbitrary",), vmem_limit_bytes=VMEM_LIMIT),
        name="experts",
    )(plan, xs, w1, b1.reshape(depth, N_EXPERTS, 1, 2 * D_FF), w2,
      b2.reshape(depth, N_EXPERTS, 1, D_MODEL))


def _final_kernel(x_ref, yg_ref, wgt_ref, gp_ref, gs_ref, fg_ref, op_ref, os_ref, *, n_prompt_blocks):
    i = pl.program_id(0)
    n_seq = gs_ref.shape[0]

    def run(gate, o_ref):
        def body(r0):
            r = pl.ds(r0, COMBINE_CHUNK)
            w = wgt_ref[r, :]
            g = gate(r0)
            xl = x_ref[r, 0:D_WORDS] + g[:, 0:D_WORDS] * _moe_half(yg_ref, w, r, False)
            xr = x_ref[r, D_WORDS:D_MODEL] + g[:, D_WORDS:D_MODEL] * _moe_half(yg_ref, w, r, True)
            ms = (jnp.sum(xl * xl, axis=-1, keepdims=True)
                  + jnp.sum(xr * xr, axis=-1, keepdims=True)) * (1.0 / D_MODEL)
            scale = lax.rsqrt(ms + EPS)
            o_ref[r, 0:D_WORDS] = xl * scale * fg_ref[:, 0:D_WORDS]
            o_ref[r, D_WORDS:D_MODEL] = xr * scale * fg_ref[:, D_WORDS:D_MODEL]
        _static_loop(x_ref.shape[0], body, COMBINE_CHUNK)

    @pl.when(i < n_prompt_blocks)
    def _():
        run(lambda r0: gp_ref[0], op_ref)

    @pl.when(i >= n_prompt_blocks)
    def _():
        run(lambda r0: gs_ref[pl.ds(r0 % n_seq, COMBINE_CHUNK), :], os_ref)


def _final(xmid, yg, wgt, gate_p, gate_s, final_g, n_prompt, seq_len):
    t_all = xmid.shape[0]
    n_prompt_blocks = n_prompt // ROW_TILE
    per_seq = seq_len // ROW_TILE
    n_seq_p = gate_p.shape[0]
    assert t_all - n_prompt == ROW_TILE
    out_specs = (
        pl.BlockSpec((ROW_TILE, D_MODEL), lambda i: (jnp.minimum(i, n_prompt_blocks - 1), 0)),
        pl.BlockSpec((ROW_TILE, D_MODEL), lambda i: (0, 0)),
    )
    out_shape = (jax.ShapeDtypeStruct((n_prompt, D_MODEL), F32),
                 jax.ShapeDtypeStruct((ROW_TILE, D_MODEL), F32))
    return pl.pallas_call(
        functools.partial(_final_kernel, n_prompt_blocks=n_prompt_blocks),
        grid=(t_all // ROW_TILE,),
        in_specs=[
            pl.BlockSpec((ROW_TILE, D_MODEL), lambda i: (i, 0)),
            pl.BlockSpec((TOP_K, ROW_TILE, D_WORDS), lambda i: (0, i, 0)),
            pl.BlockSpec((ROW_TILE, LANES), lambda i: (i, 0)),
            pl.BlockSpec((1, 1, D_MODEL), lambda i: (jnp.minimum(i // per_seq, n_seq_p - 1), 0, 0)),
            _const_spec(gate_s.shape),
            _const_spec((1, D_MODEL)),
        ],
        out_specs=out_specs,
        out_shape=out_shape,
        compiler_params=pltpu.CompilerParams(
            dimension_semantics=("arbitrary",), vmem_limit_bytes=VMEM_LIMIT),
        name="final_combine",
    )(xmid, yg, wgt, gate_p.reshape(n_seq_p, 1, D_MODEL), gate_s, final_g.reshape(1, D_MODEL))


def _sc_mesh():
    return plsc.VectorSubcoreMesh(core_axis_name="core", subcore_axis_name="subcore",
                                  num_cores=SC_CORES, num_subcores=SC_SUBCORES)


def _sc_worker_id():
    return lax.axis_index("subcore") * SC_CORES + lax.axis_index("core")


def _sc_dispatch(h2w, pos_c, n_slots):
    _, n_chunks, _, chunk = pos_c.shape

    @functools.partial(
        pl.kernel, mesh=_sc_mesh(),
        out_type=jax.ShapeDtypeStruct((n_slots, D_WORDS), jnp.int32),
        scratch_types=[pltpu.VMEM((n_chunks, TOP_K, chunk), jnp.int32),
                       pltpu.VMEM((chunk, D_WORDS), jnp.int32),
                       pltpu.SemaphoreType.DMA],
        name="moe_dispatch")
    def run(h_hbm, pos_hbm, out_hbm, idx_v, rows_v, sem):
        wid = _sc_worker_id()
        pltpu.sync_copy(pos_hbm.at[wid], idx_v)

        @pl.loop(0, n_chunks)
        def _(c):
            base = pl.multiple_of((wid * n_chunks + c) * chunk, SUBLANES)
            pltpu.sync_copy(h_hbm.at[pl.ds(base, chunk)], rows_v)
            copies = [pltpu.async_copy(rows_v, out_hbm.at[idx_v.at[c, k]], sem) for k in range(TOP_K)]
            for cp in copies:
                cp.wait()

    return run(h2w, pos_c)


def _sc_gather_back(ys, pos_c, t_all):
    _, n_chunks, _, chunk = pos_c.shape

    @functools.partial(
        pl.kernel, mesh=_sc_mesh(),
        out_type=jax.ShapeDtypeStruct((TOP_K, t_all, D_WORDS), jnp.int32),
        scratch_types=[pltpu.VMEM((n_chunks, TOP_K, chunk), jnp.int32),
                       pltpu.VMEM((TOP_K, chunk, D_WORDS), jnp.int32),
                       pltpu.SemaphoreType.DMA,
                       pltpu.SemaphoreType.DMA],
        name="moe_gather_back")
    def run(ys_hbm, pos_hbm, out_hbm, idx_v, rows_v, sem_in, sem_out):
        wid = _sc_worker_id()
        pltpu.sync_copy(pos_hbm.at[wid], idx_v)

        @pl.loop(0, n_chunks)
        def _(c):
            base = pl.multiple_of((wid * n_chunks + c) * chunk, SUBLANES)
            gathers = [pltpu.async_copy(ys_hbm.at[idx_v.at[c, k]], rows_v.at[k], sem_in)
                       for k in range(TOP_K)]
            for cp in gathers:
                cp.wait()
            writes = [pltpu.async_copy(rows_v.at[k], out_hbm.at[k, pl.ds(base, chunk)], sem_out)
                      for k in range(TOP_K)]
            for cp in writes:
                cp.wait()

    return run(ys, pos_c)


def kernel(x_prompt, x_sample, state_conv, c_prompt, c_sample, w_ada, b_ada, w_in, w_conv, b_conv,
           gn_g, gn_b, sgu_ln_g, sgu_ln_b, w_s, b_s, beta_a, beta_b, w_out, w_router, b_router,
           w1, b1, w2, b2, final_g):
    n_bp, seq_len, _ = x_prompt.shape
    n_bs, n_pos, _ = x_sample.shape
    depth = w_ada.shape[0]
    n_prompt = n_bp * seq_len
    n_sample = n_bs * n_pos
    t_all = n_prompt + n_sample
    n_tiles = (t_all * TOP_K) // GEMM_TILE + N_EXPERTS
    n_workers = SC_CORES * SC_SUBCORES
    assert t_all % (n_workers * DISPATCH_CHUNK) == 0 and t_all % (n_workers * GATHER_CHUNK) == 0
    assert n_tiles <= PLAN_LANES and t_all % ROUTER_TILE == 0

    x_p = x_prompt.reshape(n_prompt, D_MODEL)
    x_s = x_sample.transpose(1, 0, 2).reshape(n_sample, D_MODEL)
    x_s_block = 0
    moe = None
    mods =_adaln(jnp.concatenate([c_prompt, c_sample], axis=0), w_ada, b_ada)
    state_t = state_conv.transpose(0, 2, 1, 3)
    grp = jnp.arange(C_CONV) // (C_CONV // CONV_GROUPS)
    g_mat = ((grp[:, None] == grp[None, :]).astype(F32) / (C_CONV // CONV_GROUPS)).astype(BF16)

    mod_p = mods[:, :n_bp].reshape(depth, n_bp, 1, N_MOD * D_MODEL)
    mod_s = mods[:, n_bp:]
    gate_p = mod_p[..., 5 * D_MODEL:]
    gate_s = mod_s[..., 5 * D_MODEL:]
    wconv_p = jnp.broadcast_to(
        jnp.pad(w_conv, ((0, 0), (0, CARRY_ROWS - CONV_WIDTH), (0, 0)))[:, :, None, :],
        (depth, CARRY_ROWS, SUBLANES, C_CONV))
    vec = jnp.stack([b_conv, gn_g, gn_b, sgu_ln_g, sgu_ln_b, beta_a, beta_b, jnp.zeros_like(b_conv)],
                    axis=1)
    bs_full = jnp.repeat(b_s.transpose(0, 2, 1), SGU_HEAD_DIM, axis=2)
    wsv = jnp.repeat(w_s[:, :, :n_pos, :n_pos].transpose(0, 2, 3, 1).reshape(depth, n_pos * n_pos, SGU_HEADS),
                     SGU_HEAD_DIM, axis=2)
    bsv = jnp.repeat(b_s[:, :, :n_pos].transpose(0, 2, 1), SGU_HEAD_DIM, axis=2)
    w_in_b = w_in.astype(BF16)
    w_out_b = w_out.astype(BF16)
    w_router_t = w_router.transpose(0, 2, 1)
    b_router_c = b_router.reshape(depth, N_EXPERTS, 1)

    conv_p, conv_s, v_s = [], [], []
    for l in range(depth):
        xmid, h2, logits, cst, a_new, v_new = _mixer(
            l, x_p, x_s, x_s_block, moe, mod_p, mod_s, gate_p, gate_s, w_in_b, w_out_b, wconv_p, vec,
            g_mat, w_s, bs_full, wsv, bsv, w_router_t, b_router_c, state_t, n_bp, seq_len)
        conv_p.append(cst)
        conv_s.append(a_new)
        v_s.append(v_new)

        wgt, pos, plan = _router(logits)
        def per_worker(chunk):
            return pos.reshape(TOP_K, n_workers, -1, chunk).transpose(1, 2, 0, 3)
        xs = _sc_dispatch(h2, per_worker(DISPATCH_CHUNK), n_tiles * GEMM_TILE)
        ys = _experts(xs, plan, w1, b1, w2, b2, l)
        yg = _sc_gather_back(ys, per_worker(GATHER_CHUNK), t_all)
        moe = (yg, wgt)
        x_p = x_s = xmid
        x_s_block = n_prompt // n_sample

    y_p, y_s = _final(xmid, *moe, gate_p[depth - 1, :, 0], gate_s[depth - 1], final_g, n_prompt, seq_len)
    y_prompt = y_p.reshape(n_bp, seq_len, D_MODEL)
    y_sample = y_s.reshape(n_pos, n_bs, D_MODEL).transpose(1, 0, 2)
    new_conv_s = jnp.stack(conv_s, axis=0).transpose(0, 2, 1, 3)
    new_v_s = jnp.stack(v_s, axis=0).reshape(depth, n_pos, n_bs, C_SGU).transpose(0, 2, 1, 3)
    return (y_prompt, y_sample, jnp.stack(conv_p, axis=0), new_conv_s, new_v_s)
```

```python
import functools
import math

import jax
import jax.numpy as jnp
from jax import lax
from jax.experimental import pallas as pl
from jax.experimental.pallas import tpu as pltpu
from jax.experimental.pallas import tpu_sc as plsc

F32 = jnp.float32
BF16 = jnp.bfloat16

D_MODEL = 1024
C_CONV = 512
C_SGU = 512
CONV_GROUPS = 8
CONV_WIDTH = 31
CONV_STATE = CONV_WIDTH - 1
SGU_HEADS = 4
SGU_HEAD_DIM = C_SGU // SGU_HEADS
CHUNK = 128
N_EXPERTS = 32
TOP_K = 4
D_FF = 1024
SWIGLU_LIMIT = 7.0
SWIGLU_ALPHA = 1.702
N_MOD = 6
EPS = 1e-5
LOG2_E = 1.0 / math.log(2.0)

ROW_TILE = 512
ROUTER_TILE = 1536
PREFIX_BLOCK = 256
ADALN_COLS = 1024
ROW_CHUNK = 128
CONV_CHUNK = 32
COMBINE_CHUNK = 64
CARRY_ROWS = 32
SUBLANES = 8
LANES = 128
PLAN_ROWS = 8
PLAN_LANES = 256
PLAN_EXPERT, PLAN_VALID, PLAN_FIRST, PLAN_SLOT, PLAN_NEXT = range(5)
GEMM_TILE = 512
FF_BLOCK = 256
TILES_PER_STEP = 2
VMEM_LIMIT = 56 * 1024 * 1024
D_WORDS = D_MODEL // 2
DISPATCH_CHUNK = 88
GATHER_CHUNK = 48
HI_MASK = -65536
SC_CORES = 2
SC_SUBCORES = 16


def _rms(x):
    return x * lax.rsqrt(jnp.mean(x * x, axis=-1, keepdims=True) + EPS)


def _gelu(x):
    return 0.5 * x * (1.0 + lax.erf(x * (1.0 / math.sqrt(2.0))))


def _sigmoid(x, scale=1.0):
    return 1.0 / (1.0 + jnp.exp2(x * (-scale * LOG2_E)))


def _split_bf16(x):
    hi = x.astype(BF16)
    lo = (x - hi.astype(F32)).astype(BF16)
    return hi, lo


def _pack_bf16_pair(x):
    bits = lax.bitcast_convert_type(x.astype(BF16).astype(F32), jnp.int32)
    return lax.shift_right_logical(bits[:, :D_WORDS], 16) | (bits[:, D_WORDS:] & HI_MASK)


def _unpack_bf16_pair(w):
    left = lax.bitcast_convert_type(lax.shift_left(w, 16), F32)
    right = lax.bitcast_convert_type(w & HI_MASK, F32)
    return left, right


def _dot(a, b):
    return jnp.dot(a, b, preferred_element_type=F32)


def _dot_nt(a, b):
    return lax.dot_general(a, b, (((1,), (1,)), ((), ())), preferred_element_type=F32)


def _conv_tap(wconv_ref, k):
    return jnp.concatenate([wconv_ref[k]] * (CONV_CHUNK // SUBLANES), axis=0)


def _static_loop(n_rows, body, chunk=None):
    chunk = ROW_CHUNK if chunk is None else chunk
    for r0 in range(0, n_rows, chunk):
        body(r0)


def _adaln_kernel(c_ref, w_ref, b_ref, o_ref):
    c = c_ref[...]
    s_hi, s_lo = _split_bf16(c * _sigmoid(c))
    w_hi, w_lo = _split_bf16(w_ref[0])
    acc = _dot(s_hi, w_hi) + _dot(s_hi, w_lo) + _dot(s_lo, w_hi)
    o_ref[0] = acc + b_ref[0]


def _adaln(c_all, w_ada, b_ada):
    depth, _, n_out = w_ada.shape
    n_rows = c_all.shape[0]
    tn = ADALN_COLS
    return pl.pallas_call(
        _adaln_kernel,
        grid=(depth, n_out // tn),
        in_specs=[
            pl.BlockSpec((n_rows, D_MODEL), lambda l, j: (0, 0)),
            pl.BlockSpec((1, D_MODEL, tn), lambda l, j: (l, 0, j)),
            pl.BlockSpec((1, 1, tn), lambda l, j: (l, 0, j)),
        ],
        out_specs=pl.BlockSpec((1, n_rows, tn), lambda l, j: (l, 0, j)),
        out_shape=jax.ShapeDtypeStruct((depth, n_rows, n_out), F32),
        compiler_params=pltpu.CompilerParams(
            dimension_semantics=("arbitrary", "arbitrary"), vmem_limit_bytes=VMEM_LIMIT),
        name="adaln",
    )(c_all, w_ada, b_ada.reshape(depth, 1, n_out))


def _phase_in_norm(x_ref, h_scr, mod, n_rows):
    def body(r0):
        r = pl.ds(r0, ROW_CHUNK)
        h = _rms(x_ref[r, :]) * (1.0 + mod(1, r0)) + mod(0, r0)
        h_scr[r, :] = h.astype(BF16)
    _static_loop(n_rows, body)


def _phase_glu(z_scr, a_dst, a_off, n_rows):
    def body(r0):
        r = pl.ds(r0, ROW_CHUNK)
        a_dst[pl.ds(r0 + a_off, ROW_CHUNK), :] = z_scr[r, 0:C_CONV] * _sigmoid(z_scr[r, C_CONV:2 * C_CONV])
    _static_loop(n_rows, body)


def _phase_uv(z_scr, vec_ref, u_scr, v_scr, vout_ref, n_rows):
    ln_g = vec_ref[3:4, :]
    ln_b = vec_ref[4:5, :]

    def body(r0):
        r = pl.ds(r0, ROW_CHUNK)
        u_scr[r, :] = _gelu(z_scr[r, 2 * C_CONV:2 * C_CONV + C_SGU])
        gv = _gelu(z_scr[r, 2 * C_CONV + C_SGU:2 * C_CONV + 2 * C_SGU])
        mu = jnp.mean(gv, axis=-1, keepdims=True)
        dv = gv - mu
        var = jnp.mean(dv * dv, axis=-1, keepdims=True)
        v = dv * lax.rsqrt(var + EPS) * ln_g + ln_b
        if vout_ref is not None:
            vout_ref[r, :] = v
        v_scr[r, :] = v.astype(v_scr.dtype)
    _static_loop(n_rows, body)


def _phase_group_norm(conv_scr, convb_scr, stat_scr, g_ref, vec_ref, y_scr, n_rows):
    stat_scr[...] = _dot(convb_scr[...], g_ref[...])

    def center(r0):
        r = pl.ds(r0, ROW_CHUNK)
        d = conv_scr[r, :] - stat_scr[r, :]
        conv_scr[r, :] = d
        convb_scr[r, :] = (d * d).astype(BF16)
    _static_loop(n_rows, center)
    stat_scr[...] = _dot(convb_scr[...], g_ref[...])
    gn_g = vec_ref[1:2, :]
    gn_b = vec_ref[2:3, :]
    beta_a = vec_ref[5:6, :]

    def finish(r0):
        r = pl.ds(r0, ROW_CHUNK)
        gn = conv_scr[r, :] * lax.rsqrt(stat_scr[r, :] + EPS) * gn_g + gn_b
        ya = gn * _sigmoid(gn)
        y_scr[r, 0:C_CONV] = (_rms(ya) * beta_a).astype(BF16)
    _static_loop(n_rows, finish)


def _phase_out(x_ref, y_scr, wout_ref, z_scr, wr_ref, br_ref, mod, h_scr,
               xmid_ref, h2_ref, logit_ref, n_rows):
    z_scr[:, 0:D_MODEL] = (_dot(y_scr[:, 0:C_CONV], wout_ref[0:C_CONV, :])
                           + _dot(y_scr[:, C_CONV:C_CONV + C_SGU], wout_ref[C_CONV:C_CONV + C_SGU, :]))

    def body(r0):
        r = pl.ds(r0, ROW_CHUNK)
        xm = x_ref[r, :] + mod(2, r0) * z_scr[r, 0:D_MODEL]
        xmid_ref[r, :] = xm
        h2 = _rms(xm) * (1.0 + mod(4, r0)) + mod(3, r0)
        hi, lo = _split_bf16(h2)
        h2_ref[r, :] = _pack_bf16_pair(h2)
        y_scr[r, :] = hi
        h_scr[r, :] = lo
    _static_loop(n_rows, body)
    w_hi, w_lo = _split_bf16(wr_ref[...])
    both = _dot_nt(jnp.concatenate([w_hi, w_lo], axis=0), y_scr[...])
    logit_ref[...] = (both[0:N_EXPERTS] + both[N_EXPERTS:2 * N_EXPERTS]
                      + _dot_nt(w_hi, h_scr[...])) + br_ref[...]


def _moe_half(yg_ref, w, r, hi_half):
    moe = jnp.zeros((COMBINE_CHUNK, D_WORDS), F32)
    for k in range(TOP_K):
        word = yg_ref[k, r, :]
        bits = (word & HI_MASK) if hi_half else lax.shift_left(word, 16)
        moe = moe + w[:, k:k + 1] * lax.bitcast_convert_type(bits, F32)
    return moe


def _phase_moe_residual(xmid_ref, yg_ref, wgt_ref, gate, x_dst, n_rows):
    def body(r0):
        r = pl.ds(r0, COMBINE_CHUNK)
        w = wgt_ref[r, :]
        g = gate(r0)
        for hi_half in (False, True):
            cols = slice(D_WORDS, D_MODEL) if hi_half else slice(0, D_WORDS)
            x_dst[r, cols] = xmid_ref[r, cols] + g[:, cols] * _moe_half(yg_ref, w, r, hi_half)
    _static_loop(n_rows, body, COMBINE_CHUNK)


def _mixer_prompt_kernel(*refs, tiles_per_seq, n_prompt_tiles, fused):
    x_ref, mod_ref = refs[:2]
    refs = refs[2:]
    if fused:
        yg_ref, wgt_ref, gate_ref = refs[:3]
        refs = refs[3:]
        x_scr = refs[-1]
        refs = refs[:-1]
    (win_ref, wout_ref, wconv_ref, vec_ref, g_ref, ws_ref, bs_ref, wr_ref, br_ref,
     smid_ref, sh2_ref, slogit_ref, xmid_ref, h2_ref, logit_ref, cst_ref,
     h_scr, z_scr, aext_scr, conv_scr, convb_scr, stat_scr, u_scr, v_scr, y_scr) = refs
    i = pl.program_id(0)

    @pl.when(i == n_prompt_tiles)
    def _():
        xmid_ref[...] = smid_ref[...]
        h2_ref[...] = sh2_ref[...]
        logit_ref[...] = slogit_ref[...]

    @pl.when(i < n_prompt_tiles)
    def _():
        x_src = x_ref
        if fused:
            _phase_moe_residual(x_ref, yg_ref, wgt_ref, lambda r0: gate_ref[0], x_scr, ROW_TILE)
            x_src = x_scr
        _mixer_prompt_tile(lax.rem(i, tiles_per_seq), tiles_per_seq,
                           x_src, mod_ref, win_ref, wout_ref, wconv_ref, vec_ref, g_ref,
                           ws_ref, bs_ref, wr_ref, br_ref, xmid_ref, h2_ref, logit_ref, cst_ref,
                           h_scr, z_scr, aext_scr, conv_scr, convb_scr, stat_scr, u_scr, v_scr, y_scr)


def _mixer_prompt_tile(j, tiles_per_seq, x_ref, mod_ref, win_ref, wout_ref, wconv_ref, vec_ref, g_ref,
                       ws_ref, bs_ref, wr_ref, br_ref, xmid_ref, h2_ref, logit_ref, cst_ref,
                       h_scr, z_scr, aext_scr, conv_scr, convb_scr, stat_scr, u_scr, v_scr, y_scr):
    n_rows = ROW_TILE

    def mod(idx, r0):
        del r0
        return mod_ref[0, :, idx * D_MODEL:(idx + 1) * D_MODEL]

    a_buf = aext_scr.at[0]

    @pl.when(j == 0)
    def _():
        a_buf[0:CARRY_ROWS, :] = jnp.zeros((CARRY_ROWS, C_CONV), F32)

    _phase_in_norm(x_ref, h_scr, mod, n_rows)
    n_glu = 2 * C_CONV
    z_scr[:, 0:n_glu] = _dot(h_scr[...], win_ref[:, 0:n_glu])
    _phase_glu(z_scr, a_buf, CARRY_ROWS, n_rows)
    z_scr[:, n_glu:] = _dot(h_scr[...], win_ref[:, n_glu:])
    n_shift = n_rows + CARRY_ROWS - SUBLANES
    for s in range(1, SUBLANES):
        aext_scr[s, 0:n_shift, :] = a_buf[s:s + n_shift, :]

    b_conv = vec_ref[0:1, :]
    lead = CARRY_ROWS - CONV_STATE
    for r0 in range(0, n_rows, CONV_CHUNK):
        acc = jnp.zeros((CONV_CHUNK, C_CONV), F32)
        for k in range(CONV_WIDTH):
            s = (lead + k) % SUBLANES
            q0 = r0 + (lead + k - s)
            acc = acc + _conv_tap(wconv_ref, k) * aext_scr[s, q0:q0 + CONV_CHUNK, :]
        conv = acc + b_conv
        conv_scr[r0:r0 + CONV_CHUNK, :] = conv
        convb_scr[r0:r0 + CONV_CHUNK, :] = conv.astype(BF16)
    _phase_uv(z_scr, vec_ref, u_scr, v_scr, None, n_rows)
    _phase_group_norm(conv_scr, convb_scr, stat_scr, g_ref, vec_ref, y_scr, n_rows)

    row_i = lax.broadcasted_iota(jnp.int32, (CHUNK, CHUNK), 0)
    col_i = lax.broadcasted_iota(jnp.int32, (CHUNK, CHUNK), 1)
    tril = (col_i <= row_i).astype(F32)
    for hd in range(SGU_HEADS):
        ws_h = (ws_ref[hd] * tril).astype(BF16)
        cols = slice(hd * SGU_HEAD_DIM, (hd + 1) * SGU_HEAD_DIM)
        s_cols = slice(n_glu + hd * SGU_HEAD_DIM, n_glu + (hd + 1) * SGU_HEAD_DIM)
        for c in range(n_rows // CHUNK):
            rows = slice(c * CHUNK, (c + 1) * CHUNK)
            z_scr[rows, s_cols] = _dot(ws_h, v_scr[rows, cols])
    beta_b = vec_ref[6:7, :]

    def gate_body(r0):
        r = pl.ds(r0, ROW_CHUNK)
        rb = pl.ds(r0 % CHUNK, ROW_CHUNK)
        yb = u_scr[r, :] * (z_scr[r, n_glu:n_glu + C_SGU] + bs_ref[rb, :])
        y_scr[r, C_CONV:C_CONV + C_SGU] = (_rms(yb) * beta_b).astype(BF16)
    _static_loop(n_rows, gate_body)

    _phase_out(x_ref, y_scr, wout_ref, z_scr, wr_ref, br_ref, mod, h_scr,
               xmid_ref, h2_ref, logit_ref, n_rows)

    @pl.when(j == tiles_per_seq - 1)
    def _():
        cst_ref[0] = a_buf[n_rows + lead:n_rows + CARRY_ROWS, :]

    a_buf[0:CARRY_ROWS, :] = a_buf[n_rows:n_rows + CARRY_ROWS, :]


def _mixer_sample_kernel(*refs, fused):
    x_ref, mod_ref = refs[:2]
    refs = refs[2:]
    if fused:
        yg_ref, wgt_ref, gate_ref = refs[:3]
        refs = refs[3:]
        x_scr = refs[-1]
        refs = refs[:-1]
    (win_ref, wout_ref, wconv_ref, vec_ref, g_ref, wsv_ref, bsv_ref, wr_ref, br_ref, state_ref,
     xmid_ref, h2_ref, logit_ref, cnew_ref, vout_ref,
     h_scr, z_scr, conv_scr, convb_scr, stat_scr, u_scr, v_scr, y_scr, anew_ref) = refs
    n_rows = x_ref.shape[0]
    n_seq = state_ref.shape[1]
    n_pos = n_rows // n_seq

    def mod(idx, r0):
        return mod_ref[pl.ds(r0 % n_seq, ROW_CHUNK), idx * D_MODEL:(idx + 1) * D_MODEL]

    if fused:
        def gate(r0):
            return gate_ref[pl.ds(r0 % n_seq, COMBINE_CHUNK), :]
        _phase_moe_residual(x_ref, yg_ref, wgt_ref, gate, x_scr, n_rows)
        x_ref = x_scr

    _phase_in_norm(x_ref, h_scr, mod, n_rows)
    z_scr[...] = _dot(h_scr[...], win_ref[...])
    _phase_glu(z_scr, anew_ref, 0, n_rows)
    _phase_uv(z_scr, vec_ref, u_scr, v_scr, vout_ref, n_rows)

    keep = CONV_STATE - n_pos
    cnew_ref[0:keep] = state_ref[n_pos:CONV_STATE]
    for t in range(n_pos):
        cnew_ref[keep + t] = anew_ref[t * n_seq:(t + 1) * n_seq, :]

    b_conv = vec_ref[0:1, :]

    def conv_body(s0):
        rs = pl.ds(s0, CONV_CHUNK)
        for t in range(n_pos):
            acc = jnp.zeros((CONV_CHUNK, C_CONV), F32)
            for k in range(CONV_WIDTH):
                p = t + k
                if p < CONV_STATE:
                    src = state_ref[p, rs, :]
                else:
                    src = anew_ref[pl.ds(s0 + (p - CONV_STATE) * n_seq, CONV_CHUNK), :]
                acc = acc + _conv_tap(wconv_ref, k) * src
            conv = acc + b_conv
            r = pl.ds(s0 + t * n_seq, CONV_CHUNK)
            conv_scr[r, :] = conv
            convb_scr[r, :] = conv.astype(BF16)
    _static_loop(n_seq, conv_body, CONV_CHUNK)

    _phase_group_norm(conv_scr, convb_scr, stat_scr, g_ref, vec_ref, y_scr, n_rows)

    beta_b = vec_ref[6:7, :]

    def gate_body(s0):
        for t in range(n_pos):
            s = jnp.zeros((CONV_CHUNK, C_SGU), F32)
            for jj in range(t + 1):
                s = s + wsv_ref[t * n_pos + jj:t * n_pos + jj + 1, :] * v_scr[pl.ds(s0 + jj * n_seq, CONV_CHUNK), :]
            r = pl.ds(s0 + t * n_seq, CONV_CHUNK)
            yb = u_scr[r, :] * (s + bsv_ref[t:t + 1, :])
            y_scr[r, C_CONV:C_CONV + C_SGU] = (_rms(yb) * beta_b).astype(BF16)
    _static_loop(n_seq, gate_body, CONV_CHUNK)

    _phase_out(x_ref, y_scr, wout_ref, z_scr, wr_ref, br_ref, mod, h_scr,
               xmid_ref, h2_ref, logit_ref, n_rows)


def _const_spec(shape):
    nd = len(shape)
    return pl.BlockSpec(shape, lambda *_: (0,) * nd, pipeline_mode=pl.Buffered(1))


def _const_out_spec(shape):
    nd = len(shape)
    return pl.BlockSpec(shape, lambda *_: (0,) * nd)


def _layer_spec(shape, layer):
    nd = len(shape)
    return pl.BlockSpec((None,) + tuple(shape), lambda *_: (layer,) + (0,) * nd,
                        pipeline_mode=pl.Buffered(1))


def _mixer(layer, x_p, x_s, x_s_block, moe, mod_p, mod_s, gate_p, gate_s, w_in_b, w_out_b, wconv_p, vec,
           g_mat, w_s, bs_full, wsv, bsv, w_router, b_router, state_t, n_prompt_seq, seq_len):
    fused = moe is not None
    n_prompt = n_prompt_seq * seq_len
    n_seq, n_pos = state_t.shape[2], bsv.shape[1]
    n_sample = n_seq * n_pos
    t_all = n_prompt + n_sample
    nj = seq_len // ROW_TILE
    weight_specs = [
        _layer_spec((D_MODEL, 2 * C_CONV + 2 * C_SGU), layer),
        _layer_spec((D_MODEL, D_MODEL), layer),
        _layer_spec((CARRY_ROWS, SUBLANES, C_CONV), layer),
        _layer_spec((SUBLANES, C_CONV), layer),
        _const_spec((C_CONV, C_CONV)),
    ]
    router_specs = [_layer_spec((N_EXPERTS, D_MODEL), layer), _layer_spec((N_EXPERTS, 1), layer)]
    common_scratch = lambda n: [
        pltpu.VMEM((n, D_MODEL), BF16),
        pltpu.VMEM((n, 2 * C_CONV + 2 * C_SGU), F32),
    ]
    tail_scratch = lambda n, vdt: [
        pltpu.VMEM((n, C_CONV), F32),
        pltpu.VMEM((n, C_CONV), BF16),
        pltpu.VMEM((n, C_CONV), F32),
        pltpu.VMEM((n, C_SGU), F32),
        pltpu.VMEM((n, C_SGU), vdt),
        pltpu.VMEM((n, D_MODEL), BF16),
    ]
    assert n_sample == ROW_TILE
    n_tiles_p = n_prompt // ROW_TILE
    tile_p = lambda i: jnp.minimum(i, n_tiles_p - 1)
    seq_of = lambda i: jnp.minimum(i // nj, n_prompt_seq - 1)
    moe_s_specs, moe_p_specs, moe_s_args, moe_p_args, x_scratch = [], [], [], [], []
    if fused:
        yg, wgt = moe
        moe_s_specs = [pl.BlockSpec((TOP_K, n_sample, D_WORDS), lambda i: (0, x_s_block, 0)),
                       pl.BlockSpec((n_sample, LANES), lambda i: (x_s_block, 0)),
                       _layer_spec((n_seq, D_MODEL), layer - 1)]
        moe_p_specs = [pl.BlockSpec((TOP_K, ROW_TILE, D_WORDS), lambda i: (0, tile_p(i), 0)),
                       pl.BlockSpec((ROW_TILE, LANES), lambda i: (tile_p(i), 0)),
                       pl.BlockSpec((None, 1, 1, D_MODEL), lambda i: (layer - 1, seq_of(tile_p(i)), 0, 0))]
        moe_s_args = [yg, wgt, gate_s]
        moe_p_args = [yg, wgt, gate_p]
        x_scratch = [pltpu.VMEM((ROW_TILE, D_MODEL), F32)]
    smid, sh2, slogits, a_new, v_new = pl.pallas_call(
        functools.partial(_mixer_sample_kernel, fused=fused),
        grid=(1,),
        in_specs=[
            pl.BlockSpec((n_sample, D_MODEL), lambda i: (x_s_block, 0)),
            _layer_spec((n_seq, N_MOD * D_MODEL), layer),
            *moe_s_specs,
            *weight_specs,
            _layer_spec((n_pos * n_pos, C_SGU), layer),
            _layer_spec((n_pos, C_SGU), layer),
            *router_specs,
            _layer_spec((CONV_STATE, n_seq, C_CONV), layer),
        ],
        out_specs=(
            _const_out_spec((n_sample, D_MODEL)),
            _const_out_spec((n_sample, D_WORDS)),
            _const_out_spec((N_EXPERTS, n_sample)),
            _const_out_spec((CONV_STATE, n_seq, C_CONV)),
            _const_out_spec((n_sample, C_SGU)),
        ),
        out_shape=(
            jax.ShapeDtypeStruct((n_sample, D_MODEL), F32),
            jax.ShapeDtypeStruct((n_sample, D_WORDS), jnp.int32),
            jax.ShapeDtypeStruct((N_EXPERTS, n_sample), F32),
            jax.ShapeDtypeStruct((CONV_STATE, n_seq, C_CONV), F32),
            jax.ShapeDtypeStruct((n_sample, C_SGU), F32),
        ),
        scratch_shapes=common_scratch(n_sample) + tail_scratch(n_sample, F32)
        + [pltpu.VMEM((n_sample, C_CONV), F32)] + x_scratch,
        compiler_params=pltpu.CompilerParams(
            dimension_semantics=("arbitrary",), vmem_limit_bytes=VMEM_LIMIT),
        name="mixer_sample",
    )(x_s, mod_s, *moe_s_args, w_in_b, w_out_b, wconv_p, vec, g_mat, wsv, bsv, w_router, b_router,
      state_t)

    xmid, h2, logits, cst = pl.pallas_call(
        functools.partial(_mixer_prompt_kernel, tiles_per_seq=nj, n_prompt_tiles=n_tiles_p,
                          fused=fused),
        grid=(n_tiles_p + 1,),
        in_specs=[
            pl.BlockSpec((ROW_TILE, D_MODEL), lambda i: (tile_p(i), 0)),
            pl.BlockSpec((None, 1, 1, N_MOD * D_MODEL), lambda i: (layer, seq_of(i), 0, 0)),
            *moe_p_specs,
            *weight_specs,
            _layer_spec((SGU_HEADS, CHUNK, CHUNK), layer),
            _layer_spec((CHUNK, C_SGU), layer),
            *router_specs,
            _const_spec((n_sample, D_MODEL)),
            _const_spec((n_sample, D_WORDS)),
            _const_spec((N_EXPERTS, n_sample)),
        ],
        out_specs=(
            pl.BlockSpec((ROW_TILE, D_MODEL), lambda i: (i, 0)),
            pl.BlockSpec((ROW_TILE, D_WORDS), lambda i: (i, 0)),
            pl.BlockSpec((N_EXPERTS, ROW_TILE), lambda i: (0, i)),
            pl.BlockSpec((1, CONV_STATE, C_CONV), lambda i: (seq_of(i), 0, 0)),
        ),
        out_shape=(
            jax.ShapeDtypeStruct((t_all, D_MODEL), F32),
            jax.ShapeDtypeStruct((t_all, D_WORDS), jnp.int32),
            jax.ShapeDtypeStruct((N_EXPERTS, t_all), F32),
            jax.ShapeDtypeStruct((n_prompt_seq, CONV_STATE, C_CONV), F32),
        ),
        scratch_shapes=common_scratch(ROW_TILE)
        + [pltpu.VMEM((SUBLANES, CARRY_ROWS + ROW_TILE, C_CONV), F32)]
        + tail_scratch(ROW_TILE, BF16) + x_scratch,
        compiler_params=pltpu.CompilerParams(
            dimension_semantics=("arbitrary",), vmem_limit_bytes=VMEM_LIMIT),
        name="mixer_prompt",
    )(x_p, mod_p, *moe_p_args, w_in_b, w_out_b, wconv_p, vec, g_mat, w_s, bs_full, w_router, b_router,
      smid, sh2, slogits)
    return xmid, h2, logits, cst, a_new, v_new


def _tile_plan(cnt):
    e_sub = lax.broadcasted_iota(jnp.int32, (N_EXPERTS, N_EXPERTS), 0)
    e_lane = lax.broadcasted_iota(jnp.int32, (N_EXPERTS, N_EXPERTS), 1)
    tiles = jnp.floor((cnt + (GEMM_TILE - 1.0)) * (1.0 / GEMM_TILE))
    active = jnp.where(cnt > 0.0, 1.0, 0.0)

    def over_experts(tri, col):
        return _dot(tri.astype(BF16), jnp.broadcast_to(col, (N_EXPERTS, LANES)).astype(BF16))[:, 0:1]
    tile_end = over_experts(jnp.where(e_lane <= e_sub, 1.0, 0.0), tiles)
    tile_start = tile_end - tiles
    ordinal = over_experts(jnp.where(e_lane < e_sub, 1.0, 0.0), active)

    g = lax.broadcasted_iota(jnp.int32, (1, PLAN_LANES), 1).astype(F32)
    e_id = lax.broadcasted_iota(jnp.int32, (N_EXPERTS, PLAN_LANES), 0).astype(F32)
    done = jnp.sum(jnp.where(tile_end <= g, 1.0, 0.0), axis=0, keepdims=True)
    live = done < float(N_EXPERTS)
    last_e = jnp.max(jnp.where(cnt > 0.0, e_id[:, 0:1], 0.0), axis=0, keepdims=True)
    te = jnp.where(live, done, last_e)
    pick = e_id == te

    def at_tile(col):
        return jnp.sum(jnp.where(pick, col, 0.0), axis=0, keepdims=True)
    start_g = at_tile(tile_start)
    ord_g = at_tile(ordinal)
    valid = jnp.clip(at_tile(cnt) - (g - start_g) * GEMM_TILE, 0.0, float(GEMM_TILE))
    valid = jnp.where(live, valid, 0.0)
    first = jnp.where(live, jnp.where(g == start_g, 1.0, 0.0), 0.0)
    slot = ord_g - 2.0 * jnp.floor(ord_g * 0.5)
    later = jnp.where(e_id > te, jnp.where(cnt > 0.0, e_id, float(N_EXPERTS)), float(N_EXPERTS))
    nxt = jnp.min(later, axis=0, keepdims=True)
    nxt = jnp.where(nxt >= float(N_EXPERTS), -1.0, nxt)

    row = lax.broadcasted_iota(jnp.int32, (PLAN_ROWS, PLAN_LANES), 0)
    plan = jnp.zeros((PLAN_ROWS, PLAN_LANES), F32)
    for r, v in ((PLAN_EXPERT, te), (PLAN_VALID, valid), (PLAN_FIRST, first), (PLAN_SLOT, slot),
                 (PLAN_NEXT, nxt)):
        plan = jnp.where(row == r, v, plan)
    return plan.astype(jnp.int32), tile_start * GEMM_TILE


def _router_kernel(lg_ref, wgt_ref, pos_ref, plan_ref, cnt_scr, run_scr, start_scr, earlier_scr):
    ph = pl.program_id(0)
    i = pl.program_id(1)
    n = lg_ref.shape[1]

    @pl.when(jnp.logical_and(ph == 0, i == 0))
    def _():
        cnt_scr[...] = jnp.zeros_like(cnt_scr)

    lg = lg_ref[...]
    sub = lax.broadcasted_iota(jnp.int32, lg.shape, 0).astype(F32)
    vals, sels = [], []
    for _ in range(TOP_K):
        m = jnp.max(lg, axis=0, keepdims=True)
        idx = jnp.min(jnp.where(lg == m, sub, float(N_EXPERTS)), axis=0, keepdims=True)
        sel = sub == idx
        vals.append(m)
        sels.append(sel)
        lg = jnp.where(sel, -jnp.inf, lg)
    onehot = jnp.zeros(lg.shape, F32)
    for sel in sels:
        onehot = onehot + jnp.where(sel, 1.0, 0.0)
    tile_cnt = jnp.sum(onehot, axis=1, keepdims=True)

    @pl.when(ph == 0)
    def _():
        cnt_scr[...] = cnt_scr[...] + tile_cnt

    @pl.when(jnp.logical_and(ph == 1, i == 0))
    def _():
        plan, row_start = _tile_plan(cnt_scr[...])
        plan_ref[...] = plan
        start_scr[...] = row_start
        run_scr[...] = jnp.zeros_like(run_scr)
        t_row = lax.broadcasted_iota(jnp.int32, (PREFIX_BLOCK, PREFIX_BLOCK), 0)
        t_col = lax.broadcasted_iota(jnp.int32, (PREFIX_BLOCK, PREFIX_BLOCK), 1)
        earlier_scr[...] = jnp.where(t_row < t_col, 1.0, 0.0).astype(BF16)

    @pl.when(ph == 1)
    def _():
        exps = [jnp.exp(v - vals[0]) for v in vals]
        inv = 1.0 / (exps[0] + exps[1] + exps[2] + exps[3])
        w_row = lax.broadcasted_iota(jnp.int32, (LANES, n), 0)
        w_all = jnp.zeros((LANES, n), F32)
        for k in range(TOP_K):
            w_all = jnp.where(w_row == k, exps[k] * inv, w_all)
        wgt_ref[...] = w_all.T
        base = run_scr[...] + start_scr[...]
        onehot_b = onehot.astype(BF16)
        for b0 in range(0, n, PREFIX_BLOCK):
            blk = slice(b0, b0 + PREFIX_BLOCK)
            slot0 = _dot(onehot_b[:, blk], earlier_scr[...]) + base
            for k in range(TOP_K):
                pos_k = jnp.sum(jnp.where(sels[k][:, blk], slot0, 0.0), axis=0, keepdims=True)
                pos_ref[k:k + 1, blk] = pos_k.astype(jnp.int32)
            base = base + jnp.sum(onehot[:, blk], axis=1, keepdims=True)
        run_scr[...] = run_scr[...] + tile_cnt


def _router(logits):
    t_all = logits.shape[1]
    col = pltpu.VMEM((N_EXPERTS, 1), F32)
    return pl.pallas_call(
        _router_kernel,
        grid=(2, t_all // ROUTER_TILE),
        in_specs=[pl.BlockSpec((N_EXPERTS, ROUTER_TILE), lambda ph, i: (0, i))],
        out_specs=(
            pl.BlockSpec((ROUTER_TILE, LANES), lambda ph, i: (i * ph, 0)),
            pl.BlockSpec((TOP_K, ROUTER_TILE), lambda ph, i: (0, i * ph)),
            _const_out_spec((PLAN_ROWS, PLAN_LANES)),
        ),
        out_shape=(
            jax.ShapeDtypeStruct((t_all, LANES), F32),
            jax.ShapeDtypeStruct((TOP_K, t_all), jnp.int32),
            jax.ShapeDtypeStruct((PLAN_ROWS, PLAN_LANES), jnp.int32),
        ),
        scratch_shapes=[col, col, col, pltpu.VMEM((PREFIX_BLOCK, PREFIX_BLOCK), BF16)],
        compiler_params=pltpu.CompilerParams(
            dimension_semantics=("arbitrary", "arbitrary"), vmem_limit_bytes=VMEM_LIMIT),
        name="router",
    )(logits)


def _expert_kernel(plan_ref, x_ref, w1_hbm, b1_ref, w2_hbm, b2_ref, o_ref,
                   w1f_scr, w2f_scr, w1b_scr, w2b_scr, sem, *, layer):
    def weight_copies(expert, slot):
        return (pltpu.make_async_copy(w1_hbm.at[layer, expert], w1f_scr.at[slot], sem.at[0, slot]),
                pltpu.make_async_copy(w2_hbm.at[layer, expert], w2f_scr.at[slot], sem.at[1, slot]))

    for sub in range(TILES_PER_STEP):
        _expert_tile(pl.program_id(0) * TILES_PER_STEP + sub, sub * GEMM_TILE, plan_ref, x_ref,
                     b1_ref, b2_ref, o_ref, w1f_scr, w2f_scr, w1b_scr, w2b_scr, weight_copies)


def _expert_tile(g, row0, plan_ref, x_ref, b1_ref, b2_ref, o_ref, w1f_scr, w2f_scr, w1b_scr, w2b_scr,
                 weight_copies):
    nv = plan_ref[PLAN_VALID, g]
    expert = plan_ref[PLAN_EXPERT, g]
    slot = plan_ref[PLAN_SLOT, g]
    first = plan_ref[PLAN_FIRST, g] == 1

    @pl.when(first)
    def _():
        nxt = plan_ref[PLAN_NEXT, g]

        @pl.when(g == 0)
        def _():
            for cp in weight_copies(expert, slot):
                cp.start()
        for cp in weight_copies(expert, slot):
            cp.wait()

        @pl.when(nxt >= 0)
        def _():
            for cp in weight_copies(nxt, 1 - slot):
                cp.start()

    def ffn(m, fresh):
        x_left, x_right = _unpack_bf16_pair(x_ref[row0:row0 + m, :])
        x = jnp.concatenate([x_left.astype(BF16), x_right.astype(BF16)], axis=1)

        def hidden(col0):
            cols = slice(col0, col0 + FF_BLOCK)
            if fresh:
                w1b_scr[:, cols] = w1f_scr[slot, :, cols].astype(BF16)
            return _dot(x, w1b_scr[:, cols]) + b1_ref[expert, :, cols]

        acts = []
        for jb in range(D_FF // FF_BLOCK):
            gg = jnp.minimum(hidden(jb * FF_BLOCK), SWIGLU_LIMIT)
            up = jnp.clip(hidden(D_FF + jb * FF_BLOCK), -SWIGLU_LIMIT, SWIGLU_LIMIT)
            acts.append((gg * _sigmoid(gg, SWIGLU_ALPHA) * (up + 1.0)).astype(BF16))
        if fresh:
            w2b_scr[...] = w2f_scr[slot].astype(BF16)
        y = _dot(jnp.concatenate(acts, axis=1), w2b_scr[...]) + b2_ref[expert]
        row = lax.broadcasted_iota(jnp.int32, y.shape, 0)
        o_ref[row0:row0 + m, :] = _pack_bf16_pair(jnp.where(row < nv, y, 0.0))
        if m < GEMM_TILE:
            o_ref[row0 + m:row0 + GEMM_TILE, :] = jnp.zeros((GEMM_TILE - m, D_WORDS), jnp.int32)

    full = nv > GEMM_TILE // 2
    half = jnp.logical_and(nv > 0, nv <= GEMM_TILE // 2)
    for rows, cond in ((GEMM_TILE, full), (GEMM_TILE // 2, half)):
        for fresh in (True, False):
            pl.when(jnp.logical_and(cond, first == fresh))(functools.partial(ffn, rows, fresh))

    @pl.when(nv == 0)
    def _():
        o_ref[row0:row0 + GEMM_TILE, :] = jnp.zeros((GEMM_TILE, D_WORDS), jnp.int32)


def _experts(xs, plan, w1, b1, w2, b2, layer):
    n_tiles = xs.shape[0] // GEMM_TILE
    depth = w1.shape[0]
    assert n_tiles % TILES_PER_STEP == 0
    step_rows = TILES_PER_STEP * GEMM_TILE
    bias_map = lambda s, plan: (layer, 0, 0, 0)
    grid_spec = pltpu.PrefetchScalarGridSpec(
        num_scalar_prefetch=1,
        grid=(n_tiles // TILES_PER_STEP,),
        in_specs=[
            pl.BlockSpec((step_rows, D_WORDS), lambda s, plan: (s, 0)),
            pl.BlockSpec(memory_space=pl.ANY),
            pl.BlockSpec((None, N_EXPERTS, 1, 2 * D_FF), bias_map, pipeline_mode=pl.Buffered(1)),
            pl.BlockSpec(memory_space=pl.ANY),
            pl.BlockSpec((None, N_EXPERTS, 1, D_MODEL), bias_map, pipeline_mode=pl.Buffered(1)),
        ],
        out_specs=pl.BlockSpec((step_rows, D_WORDS), lambda s, plan: (s, 0)),
        scratch_shapes=[
            pltpu.VMEM((2, D_MODEL, 2 * D_FF), F32),
            pltpu.VMEM((2, D_FF, D_MODEL), F32),
            pltpu.VMEM((D_MODEL, 2 * D_FF), BF16),
            pltpu.VMEM((D_FF, D_MODEL), BF16),
            pltpu.SemaphoreType.DMA((2, 2)),
        ],
    )
    return pl.pallas_call(
        functools.partial(_expert_kernel, layer=layer),
        grid_spec=grid_spec,
        out_shape=jax.ShapeDtypeStruct((n_tiles * GEMM_TILE, D_WORDS), jnp.int32),
        compiler_params=pltpu.CompilerParams(
            dimension_semantics=("arbitrary",), vmem_limit_bytes=VMEM_LIMIT),
        name="experts",
    )(plan, xs, w1, b1.reshape(depth, N_EXPERTS, 1, 2 * D_FF), w2,
      b2.reshape(depth, N_EXPERTS, 1, D_MODEL))


def _final_kernel(x_ref, yg_ref, wgt_ref, gp_ref, gs_ref, fg_ref, op_ref, os_ref, *, n_prompt_blocks):
    i = pl.program_id(0)
    n_seq = gs_ref.shape[0]

    def run(gate, o_ref):
        def body(r0):
            r = pl.ds(r0, COMBINE_CHUNK)
            w = wgt_ref[r, :]
            g = gate(r0)
            xl = x_ref[r, 0:D_WORDS] + g[:, 0:D_WORDS] * _moe_half(yg_ref, w, r, False)
            xr = x_ref[r, D_WORDS:D_MODEL] + g[:, D_WORDS:D_MODEL] * _moe_half(yg_ref, w, r, True)
            ms = (jnp.sum(xl * xl, axis=-1, keepdims=True)
                  + jnp.sum(xr * xr, axis=-1, keepdims=True)) * (1.0 / D_MODEL)
            scale = lax.rsqrt(ms + EPS)
            o_ref[r, 0:D_WORDS] = xl * scale * fg_ref[:, 0:D_WORDS]
            o_ref[r, D_WORDS:D_MODEL] = xr * scale * fg_ref[:, D_WORDS:D_MODEL]
        _static_loop(x_ref.shape[0], body, COMBINE_CHUNK)

    @pl.when(i < n_prompt_blocks)
    def _():
        run(lambda r0: gp_ref[0], op_ref)

    @pl.when(i >= n_prompt_blocks)
    def _():
        run(lambda r0: gs_ref[pl.ds(r0 % n_seq, COMBINE_CHUNK), :], os_ref)


def _final(xmid, yg, wgt, gate_p, gate_s, final_g, n_prompt, seq_len):
    t_all = xmid.shape[0]
    n_prompt_blocks = n_prompt // ROW_TILE
    per_seq = seq_len // ROW_TILE
    n_seq_p = gate_p.shape[0]
    assert t_all - n_prompt == ROW_TILE
    out_specs = (
        pl.BlockSpec((ROW_TILE, D_MODEL), lambda i: (jnp.minimum(i, n_prompt_blocks - 1), 0)),
        pl.BlockSpec((ROW_TILE, D_MODEL), lambda i: (0, 0)),
    )
    out_shape = (jax.ShapeDtypeStruct((n_prompt, D_MODEL), F32),
                 jax.ShapeDtypeStruct((ROW_TILE, D_MODEL), F32))
    return pl.pallas_call(
        functools.partial(_final_kernel, n_prompt_blocks=n_prompt_blocks),
        grid=(t_all // ROW_TILE,),
        in_specs=[
            pl.BlockSpec((ROW_TILE, D_MODEL), lambda i: (i, 0)),
            pl.BlockSpec((TOP_K, ROW_TILE, D_WORDS), lambda i: (0, i, 0)),
            pl.BlockSpec((ROW_TILE, LANES), lambda i: (i, 0)),
            pl.BlockSpec((1, 1, D_MODEL), lambda i: (jnp.minimum(i // per_seq, n_seq_p - 1), 0, 0)),
            _const_spec(gate_s.shape),
            _const_spec((1, D_MODEL)),
        ],
        out_specs=out_specs,
        out_shape=out_shape,
        compiler_params=pltpu.CompilerParams(
            dimension_semantics=("arbitrary",), vmem_limit_bytes=VMEM_LIMIT),
        name="final_combine",
    )(xmid, yg, wgt, gate_p.reshape(n_seq_p, 1, D_MODEL), gate_s, final_g.reshape(1, D_MODEL))


def _sc_mesh():
    return plsc.VectorSubcoreMesh(core_axis_name="core", subcore_axis_name="subcore",
                                  num_cores=SC_CORES, num_subcores=SC_SUBCORES)


def _sc_worker_id():
    return lax.axis_index("subcore") * SC_CORES + lax.axis_index("core")


def _sc_dispatch(h2w, pos_c, n_slots):
    _, n_chunks, _, chunk = pos_c.shape

    @functools.partial(
        pl.kernel, mesh=_sc_mesh(),
        out_type=jax.ShapeDtypeStruct((n_slots, D_WORDS), jnp.int32),
        scratch_types=[pltpu.VMEM((n_chunks, TOP_K, chunk), jnp.int32),
                       pltpu.VMEM((chunk, D_WORDS), jnp.int32),
                       pltpu.SemaphoreType.DMA],
        name="moe_dispatch")
    def run(h_hbm, pos_hbm, out_hbm, idx_v, rows_v, sem):
        wid = _sc_worker_id()
        pltpu.sync_copy(pos_hbm.at[wid], idx_v)

        @pl.loop(0, n_chunks)
        def _(c):
            base = pl.multiple_of((wid * n_chunks + c) * chunk, SUBLANES)
            pltpu.sync_copy(h_hbm.at[pl.ds(base, chunk)], rows_v)
            copies = [pltpu.async_copy(rows_v, out_hbm.at[idx_v.at[c, k]], sem) for k in range(TOP_K)]
            for cp in copies:
                cp.wait()

    return run(h2w, pos_c)


def _sc_gather_back(ys, pos_c, t_all):
    _, n_chunks, _, chunk = pos_c.shape

    @functools.partial(
        pl.kernel, mesh=_sc_mesh(),
        out_type=jax.ShapeDtypeStruct((TOP_K, t_all, D_WORDS), jnp.int32),
        scratch_types=[pltpu.VMEM((n_chunks, TOP_K, chunk), jnp.int32),
                       pltpu.VMEM((TOP_K, chunk, D_WORDS), jnp.int32),
                       pltpu.SemaphoreType.DMA,
                       pltpu.SemaphoreType.DMA],
        name="moe_gather_back")
    def run(ys_hbm, pos_hbm, out_hbm, idx_v, rows_v, sem_in, sem_out):
        wid = _sc_worker_id()
        pltpu.sync_copy(pos_hbm.at[wid], idx_v)

        @pl.loop(0, n_chunks)
        def _(c):
            base = pl.multiple_of((wid * n_chunks + c) * chunk, SUBLANES)
            gathers = [pltpu.async_copy(ys_hbm.at[idx_v.at[c, k]], rows_v.at[k], sem_in)
                       for k in range(TOP_K)]
            for cp in gathers:
                cp.wait()
            writes = [pltpu.async_copy(rows_v.at[k], out_hbm.at[k, pl.ds(base, chunk)], sem_out)
                      for k in range(TOP_K)]
            for cp in writes:
                cp.wait()

    return run(ys, pos_c)


def kernel(x_prompt, x_sample, state_conv, c_prompt, c_sample, w_ada, b_ada, w_in, w_conv, b_conv,
           gn_g, gn_b, sgu_ln_g, sgu_ln_b, w_s, b_s, beta_a, beta_b, w_out, w_router, b_router,
           w1, b1, w2, b2, final_g):
    n_bp, seq_len, _ = x_prompt.shape
    n_bs, n_pos, _ = x_sample.shape
    depth = w_ada.shape[0]
    n_prompt = n_bp * seq_len
    n_sample = n_bs * n_pos
    t_all = n_prompt + n_sample
    n_tiles = (t_all * TOP_K) // GEMM_TILE + N_EXPERTS
    n_workers = SC_CORES * SC_SUBCORES
    assert t_all % (n_workers * DISPATCH_CHUNK) == 0 and t_all % (n_workers * GATHER_CHUNK) == 0
    assert n_tiles <= PLAN_LANES and t_all % ROUTER_TILE == 0

    x_p = x_prompt.reshape(n_prompt, D_MODEL)
    x_s = x_sample.transpose(1, 0, 2).reshape(n_sample, D_MODEL)
    x_s_block = 0
    moe = None
    mods =_adaln(jnp.concatenate([c_prompt, c_sample], axis=0), w_ada, b_ada)
    state_t = state_conv.transpose(0, 2, 1, 3)
    grp = jnp.arange(C_CONV) // (C_CONV // CONV_GROUPS)
    g_mat = ((grp[:, None] == grp[None, :]).astype(F32) / (C_CONV // CONV_GROUPS)).astype(BF16)

    mod_p = mods[:, :n_bp].reshape(depth, n_bp, 1, N_MOD * D_MODEL)
    mod_s = mods[:, n_bp:]
    gate_p = mod_p[..., 5 * D_MODEL:]
    gate_s = mod_s[..., 5 * D_MODEL:]
    wconv_p = jnp.broadcast_to(
        jnp.pad(w_conv, ((0, 0), (0, CARRY_ROWS - CONV_WIDTH), (0, 0)))[:, :, None, :],
        (depth, CARRY_ROWS, SUBLANES, C_CONV))
    vec = jnp.stack([b_conv, gn_g, gn_b, sgu_ln_g, sgu_ln_b, beta_a, beta_b, jnp.zeros_like(b_conv)],
                    axis=1)
    bs_full = jnp.repeat(b_s.transpose(0, 2, 1), SGU_HEAD_DIM, axis=2)
    wsv = jnp.repeat(w_s[:, :, :n_pos, :n_pos].transpose(0, 2, 3, 1).reshape(depth, n_pos * n_pos, SGU_HEADS),
                     SGU_HEAD_DIM, axis=2)
    bsv = jnp.repeat(b_s[:, :, :n_pos].transpose(0, 2, 1), SGU_HEAD_DIM, axis=2)
    w_in_b = w_in.astype(BF16)
    w_out_b = w_out.astype(BF16)
    w_router_t = w_router.transpose(0, 2, 1)
    b_router_c = b_router.reshape(depth, N_EXPERTS, 1)

    conv_p, conv_s, v_s = [], [], []
    for l in range(depth):
        xmid, h2, logits, cst, a_new, v_new = _mixer(
            l, x_p, x_s, x_s_block, moe, mod_p, mod_s, gate_p, gate_s, w_in_b, w_out_b, wconv_p, vec,
            g_mat, w_s, bs_full, wsv, bsv, w_router_t, b_router_c, state_t, n_bp, seq_len)
        conv_p.append(cst)
        conv_s.append(a_new)
        v_s.append(v_new)

        wgt, pos, plan = _router(logits)
        def per_worker(chunk):
            return pos.reshape(TOP_K, n_workers, -1, chunk).transpose(1, 2, 0, 3)
        xs = _sc_dispatch(h2, per_worker(DISPATCH_CHUNK), n_tiles * GEMM_TILE)
        ys = _experts(xs, plan, w1, b1, w2, b2, l)
        yg = _sc_gather_back(ys, per_worker(GATHER_CHUNK), t_all)
        moe = (yg, wgt)
        x_p = x_s = xmid
        x_s_block = n_prompt // n_sample

    y_p, y_s = _final(xmid, *moe, gate_p[depth - 1, :, 0], gate_s[depth - 1], final_g, n_prompt, seq_len)
    y_prompt = y_p.reshape(n_bp, seq_len, D_MODEL)
    y_sample = y_s.reshape(n_pos, n_bs, D_MODEL).transpose(1, 0, 2)
    new_conv_s = jnp.stack(conv_s, axis=0).transpose(0, 2, 1, 3)
    new_v_s = jnp.stack(v_s, axis=0).reshape(depth, n_pos, n_bs, C_SGU).transpose(0, 2, 1, 3)
    return (y_prompt, y_sample, jnp.stack(conv_p, axis=0), new_conv_s, new_v_s)
```

```python
import functools
import math

import jax
import jax.numpy as jnp
from jax import lax
from jax.experimental import pallas as pl
from jax.experimental.pallas import tpu as pltpu
from jax.experimental.pallas import tpu_sc as plsc

F32 = jnp.float32
BF16 = jnp.bfloat16

D_MODEL = 1024
C_CONV = 512
C_SGU = 512
CONV_GROUPS = 8
CONV_WIDTH = 31
CONV_STATE = CONV_WIDTH - 1
SGU_HEADS = 4
SGU_HEAD_DIM = C_SGU // SGU_HEADS
CHUNK = 128
N_EXPERTS = 32
TOP_K = 4
D_FF = 1024
SWIGLU_LIMIT = 7.0
SWIGLU_ALPHA = 1.702
N_MOD = 6
EPS = 1e-5
LOG2_E = 1.0 / math.log(2.0)

ROW_TILE = 512
ROUTER_TILE = 1536
PREFIX_BLOCK = 256
ADALN_COLS = 1024
ROW_CHUNK = 128
CONV_CHUNK = 32
COMBINE_CHUNK = 64
CARRY_ROWS = 32
SUBLANES = 8
LANES = 128
PLAN_ROWS = 8
PLAN_LANES = 256
PLAN_EXPERT, PLAN_VALID, PLAN_FIRST, PLAN_SLOT, PLAN_NEXT = range(5)
GEMM_TILE = 512
FF_BLOCK = 256
TILES_PER_STEP = 2
VMEM_LIMIT = 56 * 1024 * 1024
D_WORDS = D_MODEL // 2
DISPATCH_CHUNK = 88
GATHER_CHUNK = 48
HI_MASK = -65536
SC_CORES = 2
SC_SUBCORES = 16


def _rms(x):
    return x * lax.rsqrt(jnp.mean(x * x, axis=-1, keepdims=True) + EPS)


def _gelu(x):
    return 0.5 * x * (1.0 + lax.erf(x * (1.0 / math.sqrt(2.0))))


def _sigmoid(x, scale=1.0):
    return 1.0 / (1.0 + jnp.exp2(x * (-scale * LOG2_E)))


def _split_bf16(x):
    hi = x.astype(BF16)
    lo = (x - hi.astype(F32)).astype(BF16)
    return hi, lo


def _pack_bf16_pair(x):
    bits = lax.bitcast_convert_type(x.astype(BF16).astype(F32), jnp.int32)
    return lax.shift_right_logical(bits[:, :D_WORDS], 16) | (bits[:, D_WORDS:] & HI_MASK)


def _unpack_bf16_pair(w):
    left = lax.bitcast_convert_type(lax.shift_left(w, 16), F32)
    right = lax.bitcast_convert_type(w & HI_MASK, F32)
    return left, right


def _dot(a, b):
    return jnp.dot(a, b, preferred_element_type=F32)


def _dot_nt(a, b):
    return lax.dot_general(a, b, (((1,), (1,)), ((), ())), preferred_element_type=F32)


def _conv_tap(wconv_ref, k):
    return jnp.concatenate([wconv_ref[k]] * (CONV_CHUNK // SUBLANES), axis=0)


def _static_loop(n_rows, body, chunk=None):
    chunk = ROW_CHUNK if chunk is None else chunk
    for r0 in range(0, n_rows, chunk):
        body(r0)


def _adaln_kernel(c_ref, w_ref, b_ref, o_ref):
    c = c_ref[...]
    s_hi, s_lo = _split_bf16(c * _sigmoid(c))
    w_hi, w_lo = _split_bf16(w_ref[0])
    acc = _dot(s_hi, w_hi) + _dot(s_hi, w_lo) + _dot(s_lo, w_hi)
    o_ref[0] = acc + b_ref[0]


def _adaln(c_all, w_ada, b_ada, layer):
    depth, _, n_out = w_ada.shape
    n_rows = c_all.shape[0]
    tn = ADALN_COLS
    return pl.pallas_call(
        _adaln_kernel,
        grid=(n_out // tn,),
        in_specs=[
            pl.BlockSpec((n_rows, D_MODEL), lambda j: (0, 0)),
            pl.BlockSpec((1, D_MODEL, tn), lambda j: (layer, 0, j)),
            pl.BlockSpec((1, 1, tn), lambda j: (layer, 0, j)),
        ],
        out_specs=pl.BlockSpec((1, n_rows, tn), lambda j: (0, 0, j)),
        out_shape=jax.ShapeDtypeStruct((1, n_rows, n_out), F32),
        compiler_params=pltpu.CompilerParams(
            dimension_semantics=("arbitrary",), vmem_limit_bytes=VMEM_LIMIT),
        name="adaln",
    )(c_all, w_ada, b_ada.reshape(depth, 1, n_out))


def _phase_in_norm(x_ref, h_scr, mod, n_rows):
    def body(r0):
        r = pl.ds(r0, ROW_CHUNK)
        h = _rms(x_ref[r, :]) * (1.0 + mod(1, r0)) + mod(0, r0)
        h_scr[r, :] = h.astype(BF16)
    _static_loop(n_rows, body)


def _phase_glu(z_scr, a_dst, a_off, n_rows):
    def body(r0):
        r = pl.ds(r0, ROW_CHUNK)
        a_dst[pl.ds(r0 + a_off, ROW_CHUNK), :] = z_scr[r, 0:C_CONV] * _sigmoid(z_scr[r, C_CONV:2 * C_CONV])
    _static_loop(n_rows, body)


def _phase_uv(z_scr, vec_ref, u_scr, v_scr, vout_ref, n_rows):
    ln_g = vec_ref[3:4, :]
    ln_b = vec_ref[4:5, :]

    def body(r0):
        r = pl.ds(r0, ROW_CHUNK)
        u_scr[r, :] = _gelu(z_scr[r, 2 * C_CONV:2 * C_CONV + C_SGU])
        gv = _gelu(z_scr[r, 2 * C_CONV + C_SGU:2 * C_CONV + 2 * C_SGU])
        mu = jnp.mean(gv, axis=-1, keepdims=True)
        dv = gv - mu
        var = jnp.mean(dv * dv, axis=-1, keepdims=True)
        v = dv * lax.rsqrt(var + EPS) * ln_g + ln_b
        if vout_ref is not None:
            vout_ref[r, :] = v
        v_scr[r, :] = v.astype(v_scr.dtype)
    _static_loop(n_rows, body)


def _phase_group_norm(conv_scr, convb_scr, stat_scr, g_ref, vec_ref, y_scr, n_rows):
    stat_scr[...] = _dot(convb_scr[...], g_ref[...])

    def center(r0):
        r = pl.ds(r0, ROW_CHUNK)
        d = conv_scr[r, :] - stat_scr[r, :]
        conv_scr[r, :] = d
        convb_scr[r, :] = (d * d).astype(BF16)
    _static_loop(n_rows, center)
    stat_scr[...] = _dot(convb_scr[...], g_ref[...])
    gn_g = vec_ref[1:2, :]
    gn_b = vec_ref[2:3, :]
    beta_a = vec_ref[5:6, :]

    def finish(r0):
        r = pl.ds(r0, ROW_CHUNK)
        gn = conv_scr[r, :] * lax.rsqrt(stat_scr[r, :] + EPS) * gn_g + gn_b
        ya = gn * _sigmoid(gn)
        y_scr[r, 0:C_CONV] = (_rms(ya) * beta_a).astype(BF16)
    _static_loop(n_rows, finish)


def _phase_out(x_ref, y_scr, wout_ref, z_scr, wr_ref, br_ref, mod, h_scr,
               xmid_ref, h2_ref, logit_ref, n_rows):
    z_scr[:, 0:D_MODEL] = (_dot(y_scr[:, 0:C_CONV], wout_ref[0:C_CONV, :])
                           + _dot(y_scr[:, C_CONV:C_CONV + C_SGU], wout_ref[C_CONV:C_CONV + C_SGU, :]))

    def body(r0):
        r = pl.ds(r0, ROW_CHUNK)
        xm = x_ref[r, :] + mod(2, r0) * z_scr[r, 0:D_MODEL]
        xmid_ref[r, :] = xm
        h2 = _rms(xm) * (1.0 + mod(4, r0)) + mod(3, r0)
        hi, lo = _split_bf16(h2)
        h2_ref[r, :] = _pack_bf16_pair(h2)
        y_scr[r, :] = hi
        h_scr[r, :] = lo
    _static_loop(n_rows, body)
    w_hi, w_lo = _split_bf16(wr_ref[...])
    both = _dot_nt(jnp.concatenate([w_hi, w_lo], axis=0), y_scr[...])
    logit_ref[...] = (both[0:N_EXPERTS] + both[N_EXPERTS:2 * N_EXPERTS]
                      + _dot_nt(w_hi, h_scr[...])) + br_ref[...]


def _moe_half(yg_ref, w, r, hi_half):
    moe = jnp.zeros((COMBINE_CHUNK, D_WORDS), F32)
    for k in range(TOP_K):
        word = yg_ref[k, r, :]
        bits = (word & HI_MASK) if hi_half else lax.shift_left(word, 16)
        moe = moe + w[:, k:k + 1] * lax.bitcast_convert_type(bits, F32)
    return moe


def _phase_moe_residual(xmid_ref, yg_ref, wgt_ref, gate, x_dst, n_rows):
    def body(r0):
        r = pl.ds(r0, COMBINE_CHUNK)
        w = wgt_ref[r, :]
        g = gate(r0)
        for hi_half in (False, True):
            cols = slice(D_WORDS, D_MODEL) if hi_half else slice(0, D_WORDS)
            x_dst[r, cols] = xmid_ref[r, cols] + g[:, cols] * _moe_half(yg_ref, w, r, hi_half)
    _static_loop(n_rows, body, COMBINE_CHUNK)


def _mixer_prompt_kernel(*refs, tiles_per_seq, n_prompt_tiles, fused):
    x_ref, mod_ref = refs[:2]
    refs = refs[2:]
    if fused:
        yg_ref, wgt_ref, gate_ref = refs[:3]
        refs = refs[3:]
        x_scr = refs[-1]
        refs = refs[:-1]
    (win_ref, wout_ref, wconv_ref, vec_ref, g_ref, ws_ref, bs_ref, wr_ref, br_ref,
     smid_ref, sh2_ref, slogit_ref, xmid_ref, h2_ref, logit_ref, cst_ref,
     h_scr, z_scr, aext_scr, conv_scr, convb_scr, stat_scr, u_scr, v_scr, y_scr) = refs
    i = pl.program_id(0)

    @pl.when(i == n_prompt_tiles)
    def _():
        xmid_ref[...] = smid_ref[...]
        h2_ref[...] = sh2_ref[...]
        logit_ref[...] = slogit_ref[...]

    @pl.when(i < n_prompt_tiles)
    def _():
        x_src = x_ref
        if fused:
            _phase_moe_residual(x_ref, yg_ref, wgt_ref, lambda r0: gate_ref[0], x_scr, ROW_TILE)
            x_src = x_scr
        _mixer_prompt_tile(lax.rem(i, tiles_per_seq), tiles_per_seq,
                           x_src, mod_ref, win_ref, wout_ref, wconv_ref, vec_ref, g_ref,
                           ws_ref, bs_ref, wr_ref, br_ref, xmid_ref, h2_ref, logit_ref, cst_ref,
                           h_scr, z_scr, aext_scr, conv_scr, convb_scr, stat_scr, u_scr, v_scr, y_scr)


def _mixer_prompt_tile(j, tiles_per_seq, x_ref, mod_ref, win_ref, wout_ref, wconv_ref, vec_ref, g_ref,
                       ws_ref, bs_ref, wr_ref, br_ref, xmid_ref, h2_ref, logit_ref, cst_ref,
                       h_scr, z_scr, aext_scr, conv_scr, convb_scr, stat_scr, u_scr, v_scr, y_scr):
    n_rows = ROW_TILE

    def mod(idx, r0):
        del r0
        return mod_ref[0, :, idx * D_MODEL:(idx + 1) * D_MODEL]

    a_buf = aext_scr.at[0]

    @pl.when(j == 0)
    def _():
        a_buf[0:CARRY_ROWS, :] = jnp.zeros((CARRY_ROWS, C_CONV), F32)

    _phase_in_norm(x_ref, h_scr, mod, n_rows)
    n_glu = 2 * C_CONV
    z_scr[:, 0:n_glu] = _dot(h_scr[...], win_ref[:, 0:n_glu])
    _phase_glu(z_scr, a_buf, CARRY_ROWS, n_rows)
    z_scr[:, n_glu:] = _dot(h_scr[...], win_ref[:, n_glu:])
    n_shift = n_rows + CARRY_ROWS - SUBLANES
    for s in range(1, SUBLANES):
        aext_scr[s, 0:n_shift, :] = a_buf[s:s + n_shift, :]

    b_conv = vec_ref[0:1, :]
    lead = CARRY_ROWS - CONV_STATE
    for r0 in range(0, n_rows, CONV_CHUNK):
        acc = jnp.zeros((CONV_CHUNK, C_CONV), F32)
        for k in range(CONV_WIDTH):
            s = (lead + k) % SUBLANES
            q0 = r0 + (lead + k - s)
            acc = acc + _conv_tap(wconv_ref, k) * aext_scr[s, q0:q0 + CONV_CHUNK, :]
        conv = acc + b_conv
        conv_scr[r0:r0 + CONV_CHUNK, :] = conv
        convb_scr[r0:r0 + CONV_CHUNK, :] = conv.astype(BF16)
    _phase_uv(z_scr, vec_ref, u_scr, v_scr, None, n_rows)
    _phase_group_norm(conv_scr, convb_scr, stat_scr, g_ref, vec_ref, y_scr, n_rows)

    row_i = lax.broadcasted_iota(jnp.int32, (CHUNK, CHUNK), 0)
    col_i = lax.broadcasted_iota(jnp.int32, (CHUNK, CHUNK), 1)
    tril = (col_i <= row_i).astype(F32)
    for hd in range(SGU_HEADS):
        ws_h = (ws_ref[hd] * tril).astype(BF16)
        cols = slice(hd * SGU_HEAD_DIM, (hd + 1) * SGU_HEAD_DIM)
        s_cols = slice(n_glu + hd * SGU_HEAD_DIM, n_glu + (hd + 1) * SGU_HEAD_DIM)
        for c in range(n_rows // CHUNK):
            rows = slice(c * CHUNK, (c + 1) * CHUNK)
            z_scr[rows, s_cols] = _dot(ws_h, v_scr[rows, cols])
    beta_b = vec_ref[6:7, :]

    def gate_body(r0):
        r = pl.ds(r0, ROW_CHUNK)
        rb = pl.ds(r0 % CHUNK, ROW_CHUNK)
        yb = u_scr[r, :] * (z_scr[r, n_glu:n_glu + C_SGU] + bs_ref[rb, :])
        y_scr[r, C_CONV:C_CONV + C_SGU] = (_rms(yb) * beta_b).astype(BF16)
    _static_loop(n_rows, gate_body)

    _phase_out(x_ref, y_scr, wout_ref, z_scr, wr_ref, br_ref, mod, h_scr,
               xmid_ref, h2_ref, logit_ref, n_rows)

    @pl.when(j == tiles_per_seq - 1)
    def _():
        cst_ref[0] = a_buf[n_rows + lead:n_rows + CARRY_ROWS, :]

    a_buf[0:CARRY_ROWS, :] = a_buf[n_rows:n_rows + CARRY_ROWS, :]


def _mixer_sample_kernel(*refs, fused):
    x_ref, mod_ref = refs[:2]
    refs = refs[2:]
    if fused:
        yg_ref, wgt_ref, gate_ref = refs[:3]
        refs = refs[3:]
        x_scr = refs[-1]
        refs = refs[:-1]
    (win_ref, wout_ref, wconv_ref, vec_ref, g_ref, wsv_ref, bsv_ref, wr_ref, br_ref, state_ref,
     xmid_ref, h2_ref, logit_ref, cnew_ref, vout_ref,
     h_scr, z_scr, conv_scr, convb_scr, stat_scr, u_scr, v_scr, y_scr, anew_ref) = refs
    n_rows = x_ref.shape[0]
    n_seq = state_ref.shape[1]
    n_pos = n_rows // n_seq

    def mod(idx, r0):
        return mod_ref[pl.ds(r0 % n_seq, ROW_CHUNK), idx * D_MODEL:(idx + 1) * D_MODEL]

    if fused:
        def gate(r0):
            return gate_ref[pl.ds(r0 % n_seq, COMBINE_CHUNK), :]
        _phase_moe_residual(x_ref, yg_ref, wgt_ref, gate, x_scr, n_rows)
        x_ref = x_scr

    _phase_in_norm(x_ref, h_scr, mod, n_rows)
    z_scr[...] = _dot(h_scr[...], win_ref[...])
    _phase_glu(z_scr, anew_ref, 0, n_rows)
    _phase_uv(z_scr, vec_ref, u_scr, v_scr, vout_ref, n_rows)

    keep = CONV_STATE - n_pos
    cnew_ref[0:keep] = state_ref[n_pos:CONV_STATE]
    for t in range(n_pos):
        cnew_ref[keep + t] = anew_ref[t * n_seq:(t + 1) * n_seq, :]

    b_conv = vec_ref[0:1, :]

    def conv_body(s0):
        rs = pl.ds(s0, CONV_CHUNK)
        for t in range(n_pos):
            acc = jnp.zeros((CONV_CHUNK, C_CONV), F32)
            for k in range(CONV_WIDTH):
                p = t + k
                if p < CONV_STATE:
                    src = state_ref[p, rs, :]
                else:
                    src = anew_ref[pl.ds(s0 + (p - CONV_STATE) * n_seq, CONV_CHUNK), :]
                acc = acc + _conv_tap(wconv_ref, k) * src
            conv = acc + b_conv
            r = pl.ds(s0 + t * n_seq, CONV_CHUNK)
            conv_scr[r, :] = conv
            convb_scr[r, :] = conv.astype(BF16)
    _static_loop(n_seq, conv_body, CONV_CHUNK)

    _phase_group_norm(conv_scr, convb_scr, stat_scr, g_ref, vec_ref, y_scr, n_rows)

    beta_b = vec_ref[6:7, :]

    def gate_body(s0):
        for t in range(n_pos):
            s = jnp.zeros((CONV_CHUNK, C_SGU), F32)
            for jj in range(t + 1):
                s = s + wsv_ref[t * n_pos + jj:t * n_pos + jj + 1, :] * v_scr[pl.ds(s0 + jj * n_seq, CONV_CHUNK), :]
            r = pl.ds(s0 + t * n_seq, CONV_CHUNK)
            yb = u_scr[r, :] * (s + bsv_ref[t:t + 1, :])
            y_scr[r, C_CONV:C_CONV + C_SGU] = (_rms(yb) * beta_b).astype(BF16)
    _static_loop(n_seq, gate_body, CONV_CHUNK)

    _phase_out(x_ref, y_scr, wout_ref, z_scr, wr_ref, br_ref, mod, h_scr,
               xmid_ref, h2_ref, logit_ref, n_rows)


def _const_spec(shape):
    nd = len(shape)
    return pl.BlockSpec(shape, lambda *_: (0,) * nd, pipeline_mode=pl.Buffered(1))


def _const_out_spec(shape):
    nd = len(shape)
    return pl.BlockSpec(shape, lambda *_: (0,) * nd)


def _layer_spec(shape, layer):
    nd = len(shape)
    return pl.BlockSpec((None,) + tuple(shape), lambda *_: (layer,) + (0,) * nd,
                        pipeline_mode=pl.Buffered(1))


def _mixer(layer, x_p, x_s, x_s_block, moe, mod_p, mod_s, gate_p, gate_s, w_in_b, w_out_b, wconv_p, vec,
           g_mat, w_s, bs_full, wsv, bsv, w_router, b_router, state_t, n_prompt_seq, seq_len):
    fused = moe is not None
    n_prompt = n_prompt_seq * seq_len
    n_seq, n_pos = state_t.shape[2], bsv.shape[1]
    n_sample = n_seq * n_pos
    t_all = n_prompt + n_sample
    nj = seq_len // ROW_TILE
    weight_specs = [
        _layer_spec((D_MODEL, 2 * C_CONV + 2 * C_SGU), 0),
        _layer_spec((D_MODEL, D_MODEL), 0),
        _layer_spec((CARRY_ROWS, SUBLANES, C_CONV), layer),
        _layer_spec((SUBLANES, C_CONV), layer),
        _const_spec((C_CONV, C_CONV)),
    ]
    router_specs = [_layer_spec((N_EXPERTS, D_MODEL), layer), _layer_spec((N_EXPERTS, 1), layer)]
    common_scratch = lambda n: [
        pltpu.VMEM((n, D_MODEL), BF16),
        pltpu.VMEM((n, 2 * C_CONV + 2 * C_SGU), F32),
    ]
    tail_scratch = lambda n, vdt: [
        pltpu.VMEM((n, C_CONV), F32),
        pltpu.VMEM((n, C_CONV), BF16),
        pltpu.VMEM((n, C_CONV), F32),
        pltpu.VMEM((n, C_SGU), F32),
        pltpu.VMEM((n, C_SGU), vdt),
        pltpu.VMEM((n, D_MODEL), BF16),
    ]
    assert n_sample == ROW_TILE
    n_tiles_p = n_prompt // ROW_TILE
    tile_p = lambda i: jnp.minimum(i, n_tiles_p - 1)
    seq_of = lambda i: jnp.minimum(i // nj, n_prompt_seq - 1)
    moe_s_specs, moe_p_specs, moe_s_args, moe_p_args, x_scratch = [], [], [], [], []
    if fused:
        yg, wgt = moe
        moe_s_specs = [pl.BlockSpec((TOP_K, n_sample, D_WORDS), lambda i: (0, x_s_block, 0)),
                       pl.BlockSpec((n_sample, LANES), lambda i: (x_s_block, 0)),
                       _layer_spec((n_seq, D_MODEL), 0)]
        moe_p_specs = [pl.BlockSpec((TOP_K, ROW_TILE, D_WORDS), lambda i: (0, tile_p(i), 0)),
                       pl.BlockSpec((ROW_TILE, LANES), lambda i: (tile_p(i), 0)),
                       pl.BlockSpec((None, 1, 1, D_MODEL), lambda i: (0, seq_of(tile_p(i)), 0, 0))]
        moe_s_args = [yg, wgt, gate_s]
        moe_p_args = [yg, wgt, gate_p]
        x_scratch = [pltpu.VMEM((ROW_TILE, D_MODEL), F32)]
    smid, sh2, slogits, a_new, v_new = pl.pallas_call(
        functools.partial(_mixer_sample_kernel, fused=fused),
        grid=(1,),
        in_specs=[
            pl.BlockSpec((n_sample, D_MODEL), lambda i: (x_s_block, 0)),
            _layer_spec((n_seq, N_MOD * D_MODEL), 0),
            *moe_s_specs,
            *weight_specs,
            _layer_spec((n_pos * n_pos, C_SGU), layer),
            _layer_spec((n_pos, C_SGU), layer),
            *router_specs,
            _layer_spec((CONV_STATE, n_seq, C_CONV), layer),
        ],
        out_specs=(
            _const_out_spec((n_sample, D_MODEL)),
            _const_out_spec((n_sample, D_WORDS)),
            _const_out_spec((N_EXPERTS, n_sample)),
            _const_out_spec((CONV_STATE, n_seq, C_CONV)),
            _const_out_spec((n_sample, C_SGU)),
        ),
        out_shape=(
            jax.ShapeDtypeStruct((n_sample, D_MODEL), F32),
            jax.ShapeDtypeStruct((n_sample, D_WORDS), jnp.int32),
            jax.ShapeDtypeStruct((N_EXPERTS, n_sample), F32),
            jax.ShapeDtypeStruct((CONV_STATE, n_seq, C_CONV), F32),
            jax.ShapeDtypeStruct((n_sample, C_SGU), F32),
        ),
        scratch_shapes=common_scratch(n_sample) + tail_scratch(n_sample, F32)
        + [pltpu.VMEM((n_sample, C_CONV), F32)] + x_scratch,
        compiler_params=pltpu.CompilerParams(
            dimension_semantics=("arbitrary",), vmem_limit_bytes=VMEM_LIMIT),
        name="mixer_sample",
    )(x_s, mod_s, *moe_s_args, w_in_b, w_out_b, wconv_p, vec, g_mat, wsv, bsv, w_router, b_router,
      state_t)

    xmid, h2, logits, cst = pl.pallas_call(
        functools.partial(_mixer_prompt_kernel, tiles_per_seq=nj, n_prompt_tiles=n_tiles_p,
                          fused=fused),
        grid=(n_tiles_p + 1,),
        in_specs=[
            pl.BlockSpec((ROW_TILE, D_MODEL), lambda i: (tile_p(i), 0)),
            pl.BlockSpec((None, 1, 1, N_MOD * D_MODEL), lambda i: (0, seq_of(i), 0, 0)),
            *moe_p_specs,
            *weight_specs,
            _layer_spec((SGU_HEADS, CHUNK, CHUNK), layer),
            _layer_spec((CHUNK, C_SGU), layer),
            *router_specs,
            _const_spec((n_sample, D_MODEL)),
            _const_spec((n_sample, D_WORDS)),
            _const_spec((N_EXPERTS, n_sample)),
        ],
        out_specs=(
            pl.BlockSpec((ROW_TILE, D_MODEL), lambda i: (i, 0)),
            pl.BlockSpec((ROW_TILE, D_WORDS), lambda i: (i, 0)),
            pl.BlockSpec((N_EXPERTS, ROW_TILE), lambda i: (0, i)),
            pl.BlockSpec((1, CONV_STATE, C_CONV), lambda i: (seq_of(i), 0, 0)),
        ),
        out_shape=(
            jax.ShapeDtypeStruct((t_all, D_MODEL), F32),
            jax.ShapeDtypeStruct((t_all, D_WORDS), jnp.int32),
            jax.ShapeDtypeStruct((N_EXPERTS, t_all), F32),
            jax.ShapeDtypeStruct((n_prompt_seq, CONV_STATE, C_CONV), F32),
        ),
        scratch_shapes=common_scratch(ROW_TILE)
        + [pltpu.VMEM((SUBLANES, CARRY_ROWS + ROW_TILE, C_CONV), F32)]
        + tail_scratch(ROW_TILE, BF16) + x_scratch,
        compiler_params=pltpu.CompilerParams(
            dimension_semantics=("arbitrary",), vmem_limit_bytes=VMEM_LIMIT),
        name="mixer_prompt",
    )(x_p, mod_p, *moe_p_args, w_in_b, w_out_b, wconv_p, vec, g_mat, w_s, bs_full, w_router, b_router,
      smid, sh2, slogits)
    return xmid, h2, logits, cst, a_new, v_new


def _tile_plan(cnt):
    e_sub = lax.broadcasted_iota(jnp.int32, (N_EXPERTS, N_EXPERTS), 0)
    e_lane = lax.broadcasted_iota(jnp.int32, (N_EXPERTS, N_EXPERTS), 1)
    tiles = jnp.floor((cnt + (GEMM_TILE - 1.0)) * (1.0 / GEMM_TILE))
    active = jnp.where(cnt > 0.0, 1.0, 0.0)

    def over_experts(tri, col):
        return _dot(tri.astype(BF16), jnp.broadcast_to(col, (N_EXPERTS, LANES)).astype(BF16))[:, 0:1]
    tile_end = over_experts(jnp.where(e_lane <= e_sub, 1.0, 0.0), tiles)
    tile_start = tile_end - tiles
    ordinal = over_experts(jnp.where(e_lane < e_sub, 1.0, 0.0), active)

    g = lax.broadcasted_iota(jnp.int32, (1, PLAN_LANES), 1).astype(F32)
    e_id = lax.broadcasted_iota(jnp.int32, (N_EXPERTS, PLAN_LANES), 0).astype(F32)
    done = jnp.sum(jnp.where(tile_end <= g, 1.0, 0.0), axis=0, keepdims=True)
    live = done < float(N_EXPERTS)
    last_e = jnp.max(jnp.where(cnt > 0.0, e_id[:, 0:1], 0.0), axis=0, keepdims=True)
    te = jnp.where(live, done, last_e)
    pick = e_id == te

    def at_tile(col):
        return jnp.sum(jnp.where(pick, col, 0.0), axis=0, keepdims=True)
    start_g = at_tile(tile_start)
    ord_g = at_tile(ordinal)
    valid = jnp.clip(at_tile(cnt) - (g - start_g) * GEMM_TILE, 0.0, float(GEMM_TILE))
    valid = jnp.where(live, valid, 0.0)
    first = jnp.where(live, jnp.where(g == start_g, 1.0, 0.0), 0.0)
    slot = ord_g - 2.0 * jnp.floor(ord_g * 0.5)
    later = jnp.where(e_id > te, jnp.where(cnt > 0.0, e_id, float(N_EXPERTS)), float(N_EXPERTS))
    nxt = jnp.min(later, axis=0, keepdims=True)
    nxt = jnp.where(nxt >= float(N_EXPERTS), -1.0, nxt)

    row = lax.broadcasted_iota(jnp.int32, (PLAN_ROWS, PLAN_LANES), 0)
    plan = jnp.zeros((PLAN_ROWS, PLAN_LANES), F32)
    for r, v in ((PLAN_EXPERT, te), (PLAN_VALID, valid), (PLAN_FIRST, first), (PLAN_SLOT, slot),
                 (PLAN_NEXT, nxt)):
        plan = jnp.where(row == r, v, plan)
    return plan.astype(jnp.int32), tile_start * GEMM_TILE


def _router_kernel(lg_ref, wgt_ref, pos_ref, plan_ref, cnt_scr, run_scr, start_scr, earlier_scr):
    ph = pl.program_id(0)
    i = pl.program_id(1)
    n = lg_ref.shape[1]

    @pl.when(jnp.logical_and(ph == 0, i == 0))
    def _():
        cnt_scr[...] = jnp.zeros_like(cnt_scr)

    lg = lg_ref[...]
    sub = lax.broadcasted_iota(jnp.int32, lg.shape, 0).astype(F32)
    vals, sels = [], []
    for _ in range(TOP_K):
        m = jnp.max(lg, axis=0, keepdims=True)
        idx = jnp.min(jnp.where(lg == m, sub, float(N_EXPERTS)), axis=0, keepdims=True)
        sel = sub == idx
        vals.append(m)
        sels.append(sel)
        lg = jnp.where(sel, -jnp.inf, lg)
    onehot = jnp.zeros(lg.shape, F32)
    for sel in sels:
        onehot = onehot + jnp.where(sel, 1.0, 0.0)
    tile_cnt = jnp.sum(onehot, axis=1, keepdims=True)

    @pl.when(ph == 0)
    def _():
        cnt_scr[...] = cnt_scr[...] + tile_cnt

    @pl.when(jnp.logical_and(ph == 1, i == 0))
    def _():
        plan, row_start = _tile_plan(cnt_scr[...])
        plan_ref[...] = plan
        start_scr[...] = row_start
        run_scr[...] = jnp.zeros_like(run_scr)
        t_row = lax.broadcasted_iota(jnp.int32, (PREFIX_BLOCK, PREFIX_BLOCK), 0)
        t_col = lax.broadcasted_iota(jnp.int32, (PREFIX_BLOCK, PREFIX_BLOCK), 1)
        earlier_scr[...] = jnp.where(t_row < t_col, 1.0, 0.0).astype(BF16)

    @pl.when(ph == 1)
    def _():
        exps = [jnp.exp(v - vals[0]) for v in vals]
        inv = 1.0 / (exps[0] + exps[1] + exps[2] + exps[3])
        w_row = lax.broadcasted_iota(jnp.int32, (LANES, n), 0)
        w_all = jnp.zeros((LANES, n), F32)
        for k in range(TOP_K):
            w_all = jnp.where(w_row == k, exps[k] * inv, w_all)
        wgt_ref[...] = w_all.T
        base = run_scr[...] + start_scr[...]
        onehot_b = onehot.astype(BF16)
        for b0 in range(0, n, PREFIX_BLOCK):
            blk = slice(b0, b0 + PREFIX_BLOCK)
            slot0 = _dot(onehot_b[:, blk], earlier_scr[...]) + base
            for k in range(TOP_K):
                pos_k = jnp.sum(jnp.where(sels[k][:, blk], slot0, 0.0), axis=0, keepdims=True)
                pos_ref[k:k + 1, blk] = pos_k.astype(jnp.int32)
            base = base + jnp.sum(onehot[:, blk], axis=1, keepdims=True)
        run_scr[...] = run_scr[...] + tile_cnt


def _router(logits):
    t_all = logits.shape[1]
    col = pltpu.VMEM((N_EXPERTS, 1), F32)
    return pl.pallas_call(
        _router_kernel,
        grid=(2, t_all // ROUTER_TILE),
        in_specs=[pl.BlockSpec((N_EXPERTS, ROUTER_TILE), lambda ph, i: (0, i))],
        out_specs=(
            pl.BlockSpec((ROUTER_TILE, LANES), lambda ph, i: (i * ph, 0)),
            pl.BlockSpec((TOP_K, ROUTER_TILE), lambda ph, i: (0, i * ph)),
            _const_out_spec((PLAN_ROWS, PLAN_LANES)),
        ),
        out_shape=(
            jax.ShapeDtypeStruct((t_all, LANES), F32),
            jax.ShapeDtypeStruct((TOP_K, t_all), jnp.int32),
            jax.ShapeDtypeStruct((PLAN_ROWS, PLAN_LANES), jnp.int32),
        ),
        scratch_shapes=[col, col, col, pltpu.VMEM((PREFIX_BLOCK, PREFIX_BLOCK), BF16)],
        compiler_params=pltpu.CompilerParams(
            dimension_semantics=("arbitrary", "arbitrary"), vmem_limit_bytes=VMEM_LIMIT),
        name="router",
    )(logits)


def _expert_kernel(plan_ref, x_ref, w1_hbm, b1_ref, w2_hbm, b2_ref, o_ref,
                   w1f_scr, w2f_scr, w1b_scr, w2b_scr, sem, *, layer):
    def weight_copies(expert, slot):
        return (pltpu.make_async_copy(w1_hbm.at[layer, expert], w1f_scr.at[slot], sem.at[0, slot]),
                pltpu.make_async_copy(w2_hbm.at[layer, expert], w2f_scr.at[slot], sem.at[1, slot]))

    for sub in range(TILES_PER_STEP):
        _expert_tile(pl.program_id(0) * TILES_PER_STEP + sub, sub * GEMM_TILE, plan_ref, x_ref,
                     b1_ref, b2_ref, o_ref, w1f_scr, w2f_scr, w1b_scr, w2b_scr, weight_copies)


def _expert_tile(g, row0, plan_ref, x_ref, b1_ref, b2_ref, o_ref, w1f_scr, w2f_scr, w1b_scr, w2b_scr,
                 weight_copies):
    nv = plan_ref[PLAN_VALID, g]
    expert = plan_ref[PLAN_EXPERT, g]
    slot = plan_ref[PLAN_SLOT, g]
    first = plan_ref[PLAN_FIRST, g] == 1

    @pl.when(first)
    def _():
        nxt = plan_ref[PLAN_NEXT, g]

        @pl.when(g == 0)
        def _():
            for cp in weight_copies(expert, slot):
                cp.start()
        for cp in weight_copies(expert, slot):
            cp.wait()

        @pl.when(nxt >= 0)
        def _():
            for cp in weight_copies(nxt, 1 - slot):
                cp.start()

    def ffn(m, fresh):
        x_left, x_right = _unpack_bf16_pair(x_ref[row0:row0 + m, :])
        x = jnp.concatenate([x_left.astype(BF16), x_right.astype(BF16)], axis=1)

        def hidden(col0):
            cols = slice(col0, col0 + FF_BLOCK)
            if fresh:
                w1b_scr[:, cols] = w1f_scr[slot, :, cols].astype(BF16)
            return _dot(x, w1b_scr[:, cols]) + b1_ref[expert, :, cols]

        acts = []
        for jb in range(D_FF // FF_BLOCK):
            gg = jnp.minimum(hidden(jb * FF_BLOCK), SWIGLU_LIMIT)
            up = jnp.clip(hidden(D_FF + jb * FF_BLOCK), -SWIGLU_LIMIT, SWIGLU_LIMIT)
            acts.append((gg * _sigmoid(gg, SWIGLU_ALPHA) * (up + 1.0)).astype(BF16))
        if fresh:
            w2b_scr[...] = w2f_scr[slot].astype(BF16)
        y = _dot(jnp.concatenate(acts, axis=1), w2b_scr[...]) + b2_ref[expert]
        row = lax.broadcasted_iota(jnp.int32, y.shape, 0)
        o_ref[row0:row0 + m, :] = _pack_bf16_pair(jnp.where(row < nv, y, 0.0))
        if m < GEMM_TILE:
            o_ref[row0 + m:row0 + GEMM_TILE, :] = jnp.zeros((GEMM_TILE - m, D_WORDS), jnp.int32)

    full = nv > GEMM_TILE // 2
    half = jnp.logical_and(nv > 0, nv <= GEMM_TILE // 2)
    for rows, cond in ((GEMM_TILE, full), (GEMM_TILE // 2, half)):
        for fresh in (True, False):
            pl.when(jnp.logical_and(cond, first == fresh))(functools.partial(ffn, rows, fresh))

    @pl.when(nv == 0)
    def _():
        o_ref[row0:row0 + GEMM_TILE, :] = jnp.zeros((GEMM_TILE, D_WORDS), jnp.int32)


def _experts(xs, plan, w1, b1, w2, b2, layer):
    n_tiles = xs.shape[0] // GEMM_TILE
    depth = w1.shape[0]
    assert n_tiles % TILES_PER_STEP == 0
    step_rows = TILES_PER_STEP * GEMM_TILE
    bias_map = lambda s, plan: (layer, 0, 0, 0)
    grid_spec = pltpu.PrefetchScalarGridSpec(
        num_scalar_prefetch=1,
        grid=(n_tiles // TILES_PER_STEP,),
        in_specs=[
            pl.BlockSpec((step_rows, D_WORDS), lambda s, plan: (s, 0)),
            pl.BlockSpec(memory_space=pl.ANY),
            pl.BlockSpec((None, N_EXPERTS, 1, 2 * D_FF), bias_map, pipeline_mode=pl.Buffered(1)),
            pl.BlockSpec(memory_space=pl.ANY),
            pl.BlockSpec((None, N_EXPERTS, 1, D_MODEL), bias_map, pipeline_mode=pl.Buffered(1)),
        ],
        out_specs=pl.BlockSpec((step_rows, D_WORDS), lambda s, plan: (s, 0)),
        scratch_shapes=[
            pltpu.VMEM((2, D_MODEL, 2 * D_FF), F32),
            pltpu.VMEM((2, D_FF, D_MODEL), F32),
            pltpu.VMEM((D_MODEL, 2 * D_FF), BF16),
            pltpu.VMEM((D_FF, D_MODEL), BF16),
            pltpu.SemaphoreType.DMA((2, 2)),
        ],
    )
    return pl.pallas_call(
        functools.partial(_expert_kernel, layer=layer),
        grid_spec=grid_spec,
        out_shape=jax.ShapeDtypeStruct((n_tiles * GEMM_TILE, D_WORDS), jnp.int32),
        compiler_params=pltpu.CompilerParams(
            dimension_semantics=("arbitrary",), vmem_limit_bytes=VMEM_LIMIT),
        name="experts",
    )(plan, xs, w1, b1.reshape(depth, N_EXPERTS, 1, 2 * D_FF), w2,
      b2.reshape(depth, N_EXPERTS, 1, D_MODEL))


def _final_kernel(x_ref, yg_ref, wgt_ref, gp_ref, gs_ref, fg_ref, op_ref, os_ref, *, n_prompt_blocks):
    i = pl.program_id(0)
    n_seq = gs_ref.shape[0]

    def run(gate, o_ref):
        def body(r0):
            r = pl.ds(r0, COMBINE_CHUNK)
            w = wgt_ref[r, :]
            g = gate(r0)
            xl = x_ref[r, 0:D_WORDS] + g[:, 0:D_WORDS] * _moe_half(yg_ref, w, r, False)
            xr = x_ref[r, D_WORDS:D_MODEL] + g[:, D_WORDS:D_MODEL] * _moe_half(yg_ref, w, r, True)
            ms = (jnp.sum(xl * xl, axis=-1, keepdims=True)
                  + jnp.sum(xr * xr, axis=-1, keepdims=True)) * (1.0 / D_MODEL)
            scale = lax.rsqrt(ms + EPS)
            o_ref[r, 0:D_WORDS] = xl * scale * fg_ref[:, 0:D_WORDS]
            o_ref[r, D_WORDS:D_MODEL] = xr * scale * fg_ref[:, D_WORDS:D_MODEL]
        _static_loop(x_ref.shape[0], body, COMBINE_CHUNK)

    @pl.when(i < n_prompt_blocks)
    def _():
        run(lambda r0: gp_ref[0], op_ref)

    @pl.when(i >= n_prompt_blocks)
    def _():
        run(lambda r0: gs_ref[pl.ds(r0 % n_seq, COMBINE_CHUNK), :], os_ref)


def _final(xmid, yg, wgt, gate_p, gate_s, final_g, n_prompt, seq_len):
    t_all = xmid.shape[0]
    n_prompt_blocks = n_prompt // ROW_TILE
    per_seq = seq_len // ROW_TILE
    n_seq_p = gate_p.shape[0]
    assert t_all - n_prompt == ROW_TILE
    out_specs = (
        pl.BlockSpec((ROW_TILE, D_MODEL), lambda i: (jnp.minimum(i, n_prompt_blocks - 1), 0)),
        pl.BlockSpec((ROW_TILE, D_MODEL), lambda i: (0, 0)),
    )
    out_shape = (jax.ShapeDtypeStruct((n_prompt, D_MODEL), F32),
                 jax.ShapeDtypeStruct((ROW_TILE, D_MODEL), F32))
    return pl.pallas_call(
        functools.partial(_final_kernel, n_prompt_blocks=n_prompt_blocks),
        grid=(t_all // ROW_TILE,),
        in_specs=[
            pl.BlockSpec((ROW_TILE, D_MODEL), lambda i: (i, 0)),
            pl.BlockSpec((TOP_K, ROW_TILE, D_WORDS), lambda i: (0, i, 0)),
            pl.BlockSpec((ROW_TILE, LANES), lambda i: (i, 0)),
            pl.BlockSpec((1, 1, D_MODEL), lambda i: (jnp.minimum(i // per_seq, n_seq_p - 1), 0, 0)),
            _const_spec(gate_s.shape),
            _const_spec((1, D_MODEL)),
        ],
        out_specs=out_specs,
        out_shape=out_shape,
        compiler_params=pltpu.CompilerParams(
            dimension_semantics=("arbitrary",), vmem_limit_bytes=VMEM_LIMIT),
        name="final_combine",
    )(xmid, yg, wgt, gate_p.reshape(n_seq_p, 1, D_MODEL), gate_s, final_g.reshape(1, D_MODEL))


def _sc_mesh():
    return plsc.VectorSubcoreMesh(core_axis_name="core", subcore_axis_name="subcore",
                                  num_cores=SC_CORES, num_subcores=SC_SUBCORES)


def _sc_worker_id():
    return lax.axis_index("subcore") * SC_CORES + lax.axis_index("core")


def _sc_dispatch(h2w, pos_c, n_slots):
    _, n_chunks, _, chunk = pos_c.shape

    @functools.partial(
        pl.kernel, mesh=_sc_mesh(),
        out_type=jax.ShapeDtypeStruct((n_slots, D_WORDS), jnp.int32),
        scratch_types=[pltpu.VMEM((n_chunks, TOP_K, chunk), jnp.int32),
                       pltpu.VMEM((chunk, D_WORDS), jnp.int32),
                       pltpu.SemaphoreType.DMA],
        name="moe_dispatch")
    def run(h_hbm, pos_hbm, out_hbm, idx_v, rows_v, sem):
        wid = _sc_worker_id()
        pltpu.sync_copy(pos_hbm.at[wid], idx_v)

        @pl.loop(0, n_chunks)
        def _(c):
            base = pl.multiple_of((wid * n_chunks + c) * chunk, SUBLANES)
            pltpu.sync_copy(h_hbm.at[pl.ds(base, chunk)], rows_v)
            copies = [pltpu.async_copy(rows_v, out_hbm.at[idx_v.at[c, k]], sem) for k in range(TOP_K)]
            for cp in copies:
                cp.wait()

    return run(h2w, pos_c)


def _sc_gather_back(ys, pos_c, t_all):
    _, n_chunks, _, chunk = pos_c.shape

    @functools.partial(
        pl.kernel, mesh=_sc_mesh(),
        out_type=jax.ShapeDtypeStruct((TOP_K, t_all, D_WORDS), jnp.int32),
        scratch_types=[pltpu.VMEM((n_chunks, TOP_K, chunk), jnp.int32),
                       pltpu.VMEM((TOP_K, chunk, D_WORDS), jnp.int32),
                       pltpu.SemaphoreType.DMA,
                       pltpu.SemaphoreType.DMA],
        name="moe_gather_back")
    def run(ys_hbm, pos_hbm, out_hbm, idx_v, rows_v, sem_in, sem_out):
        wid = _sc_worker_id()
        pltpu.sync_copy(pos_hbm.at[wid], idx_v)

        @pl.loop(0, n_chunks)
        def _(c):
            base = pl.multiple_of((wid * n_chunks + c) * chunk, SUBLANES)
            gathers = [pltpu.async_copy(ys_hbm.at[idx_v.at[c, k]], rows_v.at[k], sem_in)
                       for k in range(TOP_K)]
            for cp in gathers:
                cp.wait()
            writes = [pltpu.async_copy(rows_v.at[k], out_hbm.at[k, pl.ds(base, chunk)], sem_out)
                      for k in range(TOP_K)]
            for cp in writes:
                cp.wait()

    return run(ys, pos_c)


def kernel(x_prompt, x_sample, state_conv, c_prompt, c_sample, w_ada, b_ada, w_in, w_conv, b_conv,
           gn_g, gn_b, sgu_ln_g, sgu_ln_b, w_s, b_s, beta_a, beta_b, w_out, w_router, b_router,
           w1, b1, w2, b2, final_g):
    n_bp, seq_len, _ = x_prompt.shape
    n_bs, n_pos, _ = x_sample.shape
    depth = w_ada.shape[0]
    n_prompt = n_bp * seq_len
    n_sample = n_bs * n_pos
    t_all = n_prompt + n_sample
    n_tiles = (t_all * TOP_K) // GEMM_TILE + N_EXPERTS
    n_workers = SC_CORES * SC_SUBCORES
    assert t_all % (n_workers * DISPATCH_CHUNK) == 0 and t_all % (n_workers * GATHER_CHUNK) == 0
    assert n_tiles <= PLAN_LANES and t_all % ROUTER_TILE == 0

    x_p = x_prompt.reshape(n_prompt, D_MODEL)
    x_s = x_sample.transpose(1, 0, 2).reshape(n_sample, D_MODEL)
    x_s_block = 0
    moe = None
    c_all = jnp.concatenate([c_prompt, c_sample], axis=0)
    state_t = state_conv.transpose(0, 2, 1, 3)
    grp = jnp.arange(C_CONV) // (C_CONV // CONV_GROUPS)
    g_mat = ((grp[:, None] == grp[None, :]).astype(F32) / (C_CONV // CONV_GROUPS)).astype(BF16)

    wconv_p = jnp.broadcast_to(
        jnp.pad(w_conv, ((0, 0), (0, CARRY_ROWS - CONV_WIDTH), (0, 0)))[:, :, None, :],
        (depth, CARRY_ROWS, SUBLANES, C_CONV))
    vec = jnp.stack([b_conv, gn_g, gn_b, sgu_ln_g, sgu_ln_b, beta_a, beta_b, jnp.zeros_like(b_conv)],
                    axis=1)
    bs_full = jnp.repeat(b_s.transpose(0, 2, 1), SGU_HEAD_DIM, axis=2)
    wsv = jnp.repeat(w_s[:, :, :n_pos, :n_pos].transpose(0, 2, 3, 1).reshape(depth, n_pos * n_pos, SGU_HEADS),
                     SGU_HEAD_DIM, axis=2)
    bsv = jnp.repeat(b_s[:, :, :n_pos].transpose(0, 2, 1), SGU_HEAD_DIM, axis=2)
    w_router_t = w_router.transpose(0, 2, 1)
    b_router_c = b_router.reshape(depth, N_EXPERTS, 1)

    conv_p, conv_s, v_s = [], [], []
    gate_p = gate_s = None
    for l in range(depth):
        mods = _adaln(c_all, w_ada, b_ada, l)
        mod_p = mods[:, :n_bp].reshape(1, n_bp, 1, N_MOD * D_MODEL)
        mod_s = mods[:, n_bp:]
        xmid, h2, logits, cst, a_new, v_new = _mixer(
            l, x_p, x_s, x_s_block, moe, mod_p, mod_s, gate_p, gate_s, w_in[l:l + 1].astype(BF16),
            w_out[l:l + 1].astype(BF16), wconv_p, vec,
            g_mat, w_s, bs_full, wsv, bsv, w_router_t, b_router_c, state_t, n_bp, seq_len)
        gate_p = mod_p[..., 5 * D_MODEL:]
        gate_s = mod_s[..., 5 * D_MODEL:]
        conv_p.append(cst)
        conv_s.append(a_new)
        v_s.append(v_new)

        wgt, pos, plan = _router(logits)
        def per_worker(chunk):
            return pos.reshape(TOP_K, n_workers, -1, chunk).transpose(1, 2, 0, 3)
        xs = _sc_dispatch(h2, per_worker(DISPATCH_CHUNK), n_tiles * GEMM_TILE)
        ys = _experts(xs, plan, w1, b1, w2, b2, l)
        yg = _sc_gather_back(ys, per_worker(GATHER_CHUNK), t_all)
        moe = (yg, wgt)
        x_p = x_s = xmid
        x_s_block = n_prompt // n_sample

    y_p, y_s = _final(xmid, *moe, gate_p[0, :, 0], gate_s[0], final_g, n_prompt, seq_len)
    y_prompt = y_p.reshape(n_bp, seq_len, D_MODEL)
    y_sample = y_s.reshape(n_pos, n_bs, D_MODEL).transpose(1, 0, 2)
    new_conv_s = jnp.stack(conv_s, axis=0).transpose(0, 2, 1, 3)
    new_v_s = jnp.stack(v_s, axis=0).reshape(depth, n_pos, n_bs, C_SGU).transpose(0, 2, 1, 3)
    return (y_prompt, y_sample, jnp.stack(conv_p, axis=0), new_conv_s, new_v_s)
```

```python
import functools
import math

import jax
import jax.numpy as jnp
from jax import lax
from jax.experimental import pallas as pl
from jax.experimental.pallas import tpu as pltpu
from jax.experimental.pallas import tpu_sc as plsc

F32 = jnp.float32
BF16 = jnp.bfloat16

D_MODEL = 1024
C_CONV = 512
C_SGU = 512
CONV_GROUPS = 8
CONV_WIDTH = 31
CONV_STATE = CONV_WIDTH - 1
SGU_HEADS = 4
SGU_HEAD_DIM = C_SGU // SGU_HEADS
CHUNK = 128
N_EXPERTS = 32
TOP_K = 4
D_FF = 1024
SWIGLU_LIMIT = 7.0
SWIGLU_ALPHA = 1.702
N_MOD = 6
EPS = 1e-5
LOG2_E = 1.0 / math.log(2.0)

ROW_TILE = 512
ROUTER_TILE = 1536
PREFIX_BLOCK = 256
ADALN_COLS = 1024
ROW_CHUNK = 128
CONV_CHUNK = 32
COMBINE_CHUNK = 64
CARRY_ROWS = 32
SUBLANES = 8
LANES = 128
PLAN_ROWS = 8
PLAN_LANES = 256
PLAN_EXPERT, PLAN_VALID, PLAN_FIRST, PLAN_SLOT, PLAN_NEXT = range(5)
GEMM_TILE = 512
FF_BLOCK = 256
TILES_PER_STEP = 2
VMEM_LIMIT = 56 * 1024 * 1024
D_WORDS = D_MODEL // 2
DISPATCH_CHUNK = 88
PROMPT_GATHER_CHUNK = 32
SAMPLE_GATHER_CHUNK = 16
HI_MASK = -65536
SC_CORES = 2
SC_SUBCORES = 16


def _rms(x):
    return x * lax.rsqrt(jnp.mean(x * x, axis=-1, keepdims=True) + EPS)


def _gelu(x):
    return 0.5 * x * (1.0 + lax.erf(x * (1.0 / math.sqrt(2.0))))


def _sigmoid(x, scale=1.0):
    return 1.0 / (1.0 + jnp.exp2(x * (-scale * LOG2_E)))


def _split_bf16(x):
    hi = x.astype(BF16)
    lo = (x - hi.astype(F32)).astype(BF16)
    return hi, lo


def _pack_bf16_pair(x):
    bits = lax.bitcast_convert_type(x.astype(BF16).astype(F32), jnp.int32)
    return lax.shift_right_logical(bits[:, :D_WORDS], 16) | (bits[:, D_WORDS:] & HI_MASK)


def _unpack_bf16_pair(w):
    left = lax.bitcast_convert_type(lax.shift_left(w, 16), F32)
    right = lax.bitcast_convert_type(w & HI_MASK, F32)
    return left, right


def _dot(a, b):
    return jnp.dot(a, b, preferred_element_type=F32)


def _dot_nt(a, b):
    return lax.dot_general(a, b, (((1,), (1,)), ((), ())), preferred_element_type=F32)


def _conv_tap(wconv_ref, k):
    return jnp.concatenate([wconv_ref[k]] * (CONV_CHUNK // SUBLANES), axis=0)


def _static_loop(n_rows, body, chunk=None):
    chunk = ROW_CHUNK if chunk is None else chunk
    for r0 in range(0, n_rows, chunk):
        body(r0)


def _adaln_kernel(c_ref, w_ref, b_ref, o_ref):
    c = c_ref[...]
    s_hi, s_lo = _split_bf16(c * _sigmoid(c))
    w_hi, w_lo = _split_bf16(w_ref[0])
    acc = _dot(s_hi, w_hi) + _dot(s_hi, w_lo) + _dot(s_lo, w_hi)
    o_ref[0] = acc + b_ref[0]


def _adaln(c_all, w_ada, b_ada, layer):
    depth, _, n_out = w_ada.shape
    n_rows = c_all.shape[0]
    tn = ADALN_COLS
    return pl.pallas_call(
        _adaln_kernel,
        grid=(n_out // tn,),
        in_specs=[
            pl.BlockSpec((n_rows, D_MODEL), lambda j: (0, 0)),
            pl.BlockSpec((1, D_MODEL, tn), lambda j: (layer, 0, j)),
            pl.BlockSpec((1, 1, tn), lambda j: (layer, 0, j)),
        ],
        out_specs=pl.BlockSpec((1, n_rows, tn), lambda j: (0, 0, j)),
        out_shape=jax.ShapeDtypeStruct((1, n_rows, n_out), F32),
        compiler_params=pltpu.CompilerParams(
            dimension_semantics=("arbitrary",), vmem_limit_bytes=VMEM_LIMIT),
        name="adaln",
    )(c_all, w_ada, b_ada.reshape(depth, 1, n_out))


def _phase_in_norm(x_ref, h_scr, mod, n_rows):
    def body(r0):
        r = pl.ds(r0, ROW_CHUNK)
        h = _rms(x_ref[r, :]) * (1.0 + mod(1, r0)) + mod(0, r0)
        h_scr[r, :] = h.astype(BF16)
    _static_loop(n_rows, body)


def _phase_glu(z_scr, a_dst, a_off, n_rows):
    def body(r0):
        r = pl.ds(r0, ROW_CHUNK)
        a_dst[pl.ds(r0 + a_off, ROW_CHUNK), :] = z_scr[r, 0:C_CONV] * _sigmoid(z_scr[r, C_CONV:2 * C_CONV])
    _static_loop(n_rows, body)


def _phase_uv(z_scr, vec_ref, u_scr, v_scr, vout_ref, n_rows):
    ln_g = vec_ref[3:4, :]
    ln_b = vec_ref[4:5, :]

    def body(r0):
        r = pl.ds(r0, ROW_CHUNK)
        u_scr[r, :] = _gelu(z_scr[r, 2 * C_CONV:2 * C_CONV + C_SGU])
        gv = _gelu(z_scr[r, 2 * C_CONV + C_SGU:2 * C_CONV + 2 * C_SGU])
        mu = jnp.mean(gv, axis=-1, keepdims=True)
        dv = gv - mu
        var = jnp.mean(dv * dv, axis=-1, keepdims=True)
        v = dv * lax.rsqrt(var + EPS) * ln_g + ln_b
        if vout_ref is not None:
            vout_ref[r, :] = v
        v_scr[r, :] = v.astype(v_scr.dtype)
    _static_loop(n_rows, body)


def _phase_group_norm(conv_scr, convb_scr, stat_scr, g_ref, vec_ref, y_scr, n_rows):
    stat_scr[...] = _dot(convb_scr[...], g_ref[...])

    def center(r0):
        r = pl.ds(r0, ROW_CHUNK)
        d = conv_scr[r, :] - stat_scr[r, :]
        conv_scr[r, :] = d
        convb_scr[r, :] = (d * d).astype(BF16)
    _static_loop(n_rows, center)
    stat_scr[...] = _dot(convb_scr[...], g_ref[...])
    gn_g = vec_ref[1:2, :]
    gn_b = vec_ref[2:3, :]
    beta_a = vec_ref[5:6, :]

    def finish(r0):
        r = pl.ds(r0, ROW_CHUNK)
        gn = conv_scr[r, :] * lax.rsqrt(stat_scr[r, :] + EPS) * gn_g + gn_b
        ya = gn * _sigmoid(gn)
        y_scr[r, 0:C_CONV] = (_rms(ya) * beta_a).astype(BF16)
    _static_loop(n_rows, finish)


def _phase_out(x_ref, y_scr, wout_ref, z_scr, wr_ref, br_ref, mod, h_scr,
               xmid_ref, h2_ref, logit_ref, n_rows):
    z_scr[:, 0:D_MODEL] = (_dot(y_scr[:, 0:C_CONV], wout_ref[0:C_CONV, :])
                           + _dot(y_scr[:, C_CONV:C_CONV + C_SGU], wout_ref[C_CONV:C_CONV + C_SGU, :]))

    def body(r0):
        r = pl.ds(r0, ROW_CHUNK)
        xm = x_ref[r, :] + mod(2, r0) * z_scr[r, 0:D_MODEL]
        xmid_ref[r, :] = xm
        h2 = _rms(xm) * (1.0 + mod(4, r0)) + mod(3, r0)
        hi, lo = _split_bf16(h2)
        h2_ref[r, :] = _pack_bf16_pair(h2)
        y_scr[r, :] = hi
        h_scr[r, :] = lo
    _static_loop(n_rows, body)
    w_hi, w_lo = _split_bf16(wr_ref[...])
    both = _dot_nt(jnp.concatenate([w_hi, w_lo], axis=0), y_scr[...])
    logit_ref[...] = (both[0:N_EXPERTS] + both[N_EXPERTS:2 * N_EXPERTS]
                      + _dot_nt(w_hi, h_scr[...])) + br_ref[...]


def _moe_half(yg_ref, w, r, hi_half):
    moe = jnp.zeros((COMBINE_CHUNK, D_WORDS), F32)
    for k in range(TOP_K):
        word = yg_ref[k, r, :]
        bits = (word & HI_MASK) if hi_half else lax.shift_left(word, 16)
        moe = moe + w[:, k:k + 1] * lax.bitcast_convert_type(bits, F32)
    return moe


def _phase_moe_residual(xmid_ref, yg_ref, wgt_ref, gate, x_dst, n_rows):
    def body(r0):
        r = pl.ds(r0, COMBINE_CHUNK)
        w = wgt_ref[r, :]
        g = gate(r0)
        for hi_half in (False, True):
            cols = slice(D_WORDS, D_MODEL) if hi_half else slice(0, D_WORDS)
            x_dst[r, cols] = xmid_ref[r, cols] + g[:, cols] * _moe_half(yg_ref, w, r, hi_half)
    _static_loop(n_rows, body, COMBINE_CHUNK)


def _mixer_prompt_kernel(*refs, tiles_per_seq, n_prompt_tiles, fused):
    x_ref, mod_ref = refs[:2]
    refs = refs[2:]
    if fused:
        yg_ref, wgt_ref, gate_ref = refs[:3]
        refs = refs[3:]
        x_scr = refs[-1]
        refs = refs[:-1]
    (win_ref, wout_ref, wconv_ref, vec_ref, g_ref, ws_ref, bs_ref, wr_ref, br_ref,
     smid_ref, sh2_ref, slogit_ref, xmid_ref, h2_ref, logit_ref, cst_ref,
     h_scr, z_scr, aext_scr, conv_scr, convb_scr, stat_scr, u_scr, v_scr, y_scr) = refs
    i = pl.program_id(0)

    @pl.when(i == n_prompt_tiles)
    def _():
        xmid_ref[...] = smid_ref[...]
        h2_ref[...] = sh2_ref[...]
        logit_ref[...] = slogit_ref[...]

    @pl.when(i < n_prompt_tiles)
    def _():
        x_src = x_ref
        if fused:
            _phase_moe_residual(x_ref, yg_ref, wgt_ref, lambda r0: gate_ref[0], x_scr, ROW_TILE)
            x_src = x_scr
        _mixer_prompt_tile(lax.rem(i, tiles_per_seq), tiles_per_seq,
                           x_src, mod_ref, win_ref, wout_ref, wconv_ref, vec_ref, g_ref,
                           ws_ref, bs_ref, wr_ref, br_ref, xmid_ref, h2_ref, logit_ref, cst_ref,
                           h_scr, z_scr, aext_scr, conv_scr, convb_scr, stat_scr, u_scr, v_scr, y_scr)


def _mixer_prompt_tile(j, tiles_per_seq, x_ref, mod_ref, win_ref, wout_ref, wconv_ref, vec_ref, g_ref,
                       ws_ref, bs_ref, wr_ref, br_ref, xmid_ref, h2_ref, logit_ref, cst_ref,
                       h_scr, z_scr, aext_scr, conv_scr, convb_scr, stat_scr, u_scr, v_scr, y_scr):
    n_rows = ROW_TILE

    def mod(idx, r0):
        del r0
        return mod_ref[0, :, idx * D_MODEL:(idx + 1) * D_MODEL]

    a_buf = aext_scr.at[0]

    @pl.when(j == 0)
    def _():
        a_buf[0:CARRY_ROWS, :] = jnp.zeros((CARRY_ROWS, C_CONV), F32)

    _phase_in_norm(x_ref, h_scr, mod, n_rows)
    n_glu = 2 * C_CONV
    z_scr[:, 0:n_glu] = _dot(h_scr[...], win_ref[:, 0:n_glu])
    _phase_glu(z_scr, a_buf, CARRY_ROWS, n_rows)
    z_scr[:, n_glu:] = _dot(h_scr[...], win_ref[:, n_glu:])
    n_shift = n_rows + CARRY_ROWS - SUBLANES
    for s in range(1, SUBLANES):
        aext_scr[s, 0:n_shift, :] = a_buf[s:s + n_shift, :]

    b_conv = vec_ref[0:1, :]
    lead = CARRY_ROWS - CONV_STATE
    for r0 in range(0, n_rows, CONV_CHUNK):
        acc = jnp.zeros((CONV_CHUNK, C_CONV), F32)
        for k in range(CONV_WIDTH):
            s = (lead + k) % SUBLANES
            q0 = r0 + (lead + k - s)
            acc = acc + _conv_tap(wconv_ref, k) * aext_scr[s, q0:q0 + CONV_CHUNK, :]
        conv = acc + b_conv
        conv_scr[r0:r0 + CONV_CHUNK, :] = conv
        convb_scr[r0:r0 + CONV_CHUNK, :] = conv.astype(BF16)
    _phase_uv(z_scr, vec_ref, u_scr, v_scr, None, n_rows)
    _phase_group_norm(conv_scr, convb_scr, stat_scr, g_ref, vec_ref, y_scr, n_rows)

    row_i = lax.broadcasted_iota(jnp.int32, (CHUNK, CHUNK), 0)
    col_i = lax.broadcasted_iota(jnp.int32, (CHUNK, CHUNK), 1)
    tril = (col_i <= row_i).astype(F32)
    for hd in range(SGU_HEADS):
        ws_h = (ws_ref[hd] * tril).astype(BF16)
        cols = slice(hd * SGU_HEAD_DIM, (hd + 1) * SGU_HEAD_DIM)
        s_cols = slice(n_glu + hd * SGU_HEAD_DIM, n_glu + (hd + 1) * SGU_HEAD_DIM)
        for c in range(n_rows // CHUNK):
            rows = slice(c * CHUNK, (c + 1) * CHUNK)
            z_scr[rows, s_cols] = _dot(ws_h, v_scr[rows, cols])
    beta_b = vec_ref[6:7, :]

    def gate_body(r0):
        r = pl.ds(r0, ROW_CHUNK)
        rb = pl.ds(r0 % CHUNK, ROW_CHUNK)
        yb = u_scr[r, :] * (z_scr[r, n_glu:n_glu + C_SGU] + bs_ref[rb, :])
        y_scr[r, C_CONV:C_CONV + C_SGU] = (_rms(yb) * beta_b).astype(BF16)
    _static_loop(n_rows, gate_body)

    _phase_out(x_ref, y_scr, wout_ref, z_scr, wr_ref, br_ref, mod, h_scr,
               xmid_ref, h2_ref, logit_ref, n_rows)

    @pl.when(j == tiles_per_seq - 1)
    def _():
        cst_ref[0] = a_buf[n_rows + lead:n_rows + CARRY_ROWS, :]

    a_buf[0:CARRY_ROWS, :] = a_buf[n_rows:n_rows + CARRY_ROWS, :]


def _mixer_sample_kernel(*refs, fused):
    x_ref, mod_ref = refs[:2]
    refs = refs[2:]
    if fused:
        yg_ref, wgt_ref, gate_ref = refs[:3]
        refs = refs[3:]
        x_scr = refs[-1]
        refs = refs[:-1]
    (win_ref, wout_ref, wconv_ref, vec_ref, g_ref, wsv_ref, bsv_ref, wr_ref, br_ref, state_ref,
     xmid_ref, h2_ref, logit_ref, cnew_ref, vout_ref,
     h_scr, z_scr, conv_scr, convb_scr, stat_scr, u_scr, v_scr, y_scr, anew_ref) = refs
    n_rows = x_ref.shape[0]
    n_seq = state_ref.shape[1]
    n_pos = n_rows // n_seq

    def mod(idx, r0):
        return mod_ref[pl.ds(r0 % n_seq, ROW_CHUNK), idx * D_MODEL:(idx + 1) * D_MODEL]

    if fused:
        def gate(r0):
            return gate_ref[pl.ds(r0 % n_seq, COMBINE_CHUNK), :]
        _phase_moe_residual(x_ref, yg_ref, wgt_ref, gate, x_scr, n_rows)
        x_ref = x_scr

    _phase_in_norm(x_ref, h_scr, mod, n_rows)
    z_scr[...] = _dot(h_scr[...], win_ref[...])
    _phase_glu(z_scr, anew_ref, 0, n_rows)
    _phase_uv(z_scr, vec_ref, u_scr, v_scr, vout_ref, n_rows)

    keep = CONV_STATE - n_pos
    cnew_ref[0:keep] = state_ref[n_pos:CONV_STATE]
    for t in range(n_pos):
        cnew_ref[keep + t] = anew_ref[t * n_seq:(t + 1) * n_seq, :]

    b_conv = vec_ref[0:1, :]

    def conv_body(s0):
        rs = pl.ds(s0, CONV_CHUNK)
        for t in range(n_pos):
            acc = jnp.zeros((CONV_CHUNK, C_CONV), F32)
            for k in range(CONV_WIDTH):
                p = t + k
                if p < CONV_STATE:
                    src = state_ref[p, rs, :]
                else:
                    src = anew_ref[pl.ds(s0 + (p - CONV_STATE) * n_seq, CONV_CHUNK), :]
                acc = acc + _conv_tap(wconv_ref, k) * src
            conv = acc + b_conv
            r = pl.ds(s0 + t * n_seq, CONV_CHUNK)
            conv_scr[r, :] = conv
            convb_scr[r, :] = conv.astype(BF16)
    _static_loop(n_seq, conv_body, CONV_CHUNK)

    _phase_group_norm(conv_scr, convb_scr, stat_scr, g_ref, vec_ref, y_scr, n_rows)

    beta_b = vec_ref[6:7, :]

    def gate_body(s0):
        for t in range(n_pos):
            s = jnp.zeros((CONV_CHUNK, C_SGU), F32)
            for jj in range(t + 1):
                s = s + wsv_ref[t * n_pos + jj:t * n_pos + jj + 1, :] * v_scr[pl.ds(s0 + jj * n_seq, CONV_CHUNK), :]
            r = pl.ds(s0 + t * n_seq, CONV_CHUNK)
            yb = u_scr[r, :] * (s + bsv_ref[t:t + 1, :])
            y_scr[r, C_CONV:C_CONV + C_SGU] = (_rms(yb) * beta_b).astype(BF16)
    _static_loop(n_seq, gate_body, CONV_CHUNK)

    _phase_out(x_ref, y_scr, wout_ref, z_scr, wr_ref, br_ref, mod, h_scr,
               xmid_ref, h2_ref, logit_ref, n_rows)


def _const_spec(shape):
    nd = len(shape)
    return pl.BlockSpec(shape, lambda *_: (0,) * nd, pipeline_mode=pl.Buffered(1))


def _const_out_spec(shape):
    nd = len(shape)
    return pl.BlockSpec(shape, lambda *_: (0,) * nd)


def _layer_spec(shape, layer):
    nd = len(shape)
    return pl.BlockSpec((None,) + tuple(shape), lambda *_: (layer,) + (0,) * nd,
                        pipeline_mode=pl.Buffered(1))


def _mixer(layer, x_p, x_s, x_s_block, moe, mod_p, mod_s, gate_p, gate_s, w_in_b, w_out_b, wconv_p, vec,
           g_mat, w_s, bs_full, wsv, bsv, w_router, b_router, state_t, n_prompt_seq, seq_len):
    fused = moe is not None
    n_prompt = n_prompt_seq * seq_len
    n_seq, n_pos = state_t.shape[2], bsv.shape[1]
    n_sample = n_seq * n_pos
    t_all = n_prompt + n_sample
    nj = seq_len // ROW_TILE
    weight_specs = [
        _layer_spec((D_MODEL, 2 * C_CONV + 2 * C_SGU), 0),
        _layer_spec((D_MODEL, D_MODEL), 0),
        _layer_spec((CARRY_ROWS, SUBLANES, C_CONV), layer),
        _layer_spec((SUBLANES, C_CONV), layer),
        _const_spec((C_CONV, C_CONV)),
    ]
    router_specs = [_layer_spec((N_EXPERTS, D_MODEL), layer), _layer_spec((N_EXPERTS, 1), layer)]
    common_scratch = lambda n: [
        pltpu.VMEM((n, D_MODEL), BF16),
        pltpu.VMEM((n, 2 * C_CONV + 2 * C_SGU), F32),
    ]
    tail_scratch = lambda n, vdt: [
        pltpu.VMEM((n, C_CONV), F32),
        pltpu.VMEM((n, C_CONV), BF16),
        pltpu.VMEM((n, C_CONV), F32),
        pltpu.VMEM((n, C_SGU), F32),
        pltpu.VMEM((n, C_SGU), vdt),
        pltpu.VMEM((n, D_MODEL), BF16),
    ]
    assert n_sample == ROW_TILE
    n_tiles_p = n_prompt // ROW_TILE
    tile_p = lambda i: jnp.minimum(i, n_tiles_p - 1)
    seq_of = lambda i: jnp.minimum(i // nj, n_prompt_seq - 1)
    moe_s_specs, moe_p_specs, moe_s_args, moe_p_args, x_scratch = [], [], [], [], []
    if fused:
        yg_p, yg_s, wgt = moe
        moe_s_specs = [_const_spec((TOP_K, n_sample, D_WORDS)),
                       pl.BlockSpec((n_sample, LANES), lambda i: (x_s_block, 0)),
                       _layer_spec((n_seq, D_MODEL), 0)]
        moe_p_specs = [pl.BlockSpec((TOP_K, ROW_TILE, D_WORDS), lambda i: (0, tile_p(i), 0)),
                       pl.BlockSpec((ROW_TILE, LANES), lambda i: (tile_p(i), 0)),
                       pl.BlockSpec((None, 1, 1, D_MODEL), lambda i: (0, seq_of(tile_p(i)), 0, 0))]
        moe_s_args = [yg_s, wgt, gate_s]
        moe_p_args = [yg_p, wgt, gate_p]
        x_scratch = [pltpu.VMEM((ROW_TILE, D_MODEL), F32)]
    smid, sh2, slogits, a_new, v_new = pl.pallas_call(
        functools.partial(_mixer_sample_kernel, fused=fused),
        grid=(1,),
        in_specs=[
            pl.BlockSpec((n_sample, D_MODEL), lambda i: (x_s_block, 0)),
            _layer_spec((n_seq, N_MOD * D_MODEL), 0),
            *moe_s_specs,
            *weight_specs,
            _layer_spec((n_pos * n_pos, C_SGU), layer),
            _layer_spec((n_pos, C_SGU), layer),
            *router_specs,
            _layer_spec((CONV_STATE, n_seq, C_CONV), layer),
        ],
        out_specs=(
            _const_out_spec((n_sample, D_MODEL)),
            _const_out_spec((n_sample, D_WORDS)),
            _const_out_spec((N_EXPERTS, n_sample)),
            _const_out_spec((CONV_STATE, n_seq, C_CONV)),
            _const_out_spec((n_sample, C_SGU)),
        ),
        out_shape=(
            jax.ShapeDtypeStruct((n_sample, D_MODEL), F32),
            jax.ShapeDtypeStruct((n_sample, D_WORDS), jnp.int32),
            jax.ShapeDtypeStruct((N_EXPERTS, n_sample), F32),
            jax.ShapeDtypeStruct((CONV_STATE, n_seq, C_CONV), F32),
            jax.ShapeDtypeStruct((n_sample, C_SGU), F32),
        ),
        scratch_shapes=common_scratch(n_sample) + tail_scratch(n_sample, F32)
        + [pltpu.VMEM((n_sample, C_CONV), F32)] + x_scratch,
        compiler_params=pltpu.CompilerParams(
            dimension_semantics=("arbitrary",), vmem_limit_bytes=VMEM_LIMIT),
        name="mixer_sample",
    )(x_s, mod_s, *moe_s_args, w_in_b, w_out_b, wconv_p, vec, g_mat, wsv, bsv, w_router, b_router,
      state_t)

    xmid, h2, logits, cst = pl.pallas_call(
        functools.partial(_mixer_prompt_kernel, tiles_per_seq=nj, n_prompt_tiles=n_tiles_p,
                          fused=fused),
        grid=(n_tiles_p + 1,),
        in_specs=[
            pl.BlockSpec((ROW_TILE, D_MODEL), lambda i: (tile_p(i), 0)),
            pl.BlockSpec((None, 1, 1, N_MOD * D_MODEL), lambda i: (0, seq_of(i), 0, 0)),
            *moe_p_specs,
            *weight_specs,
            _layer_spec((SGU_HEADS, CHUNK, CHUNK), layer),
            _layer_spec((CHUNK, C_SGU), layer),
            *router_specs,
            _const_spec((n_sample, D_MODEL)),
            _const_spec((n_sample, D_WORDS)),
            _const_spec((N_EXPERTS, n_sample)),
        ],
        out_specs=(
            pl.BlockSpec((ROW_TILE, D_MODEL), lambda i: (i, 0)),
            pl.BlockSpec((ROW_TILE, D_WORDS), lambda i: (i, 0)),
            pl.BlockSpec((N_EXPERTS, ROW_TILE), lambda i: (0, i)),
            pl.BlockSpec((1, CONV_STATE, C_CONV), lambda i: (seq_of(i), 0, 0)),
        ),
        out_shape=(
            jax.ShapeDtypeStruct((t_all, D_MODEL), F32),
            jax.ShapeDtypeStruct((t_all, D_WORDS), jnp.int32),
            jax.ShapeDtypeStruct((N_EXPERTS, t_all), F32),
            jax.ShapeDtypeStruct((n_prompt_seq, CONV_STATE, C_CONV), F32),
        ),
        scratch_shapes=common_scratch(ROW_TILE)
        + [pltpu.VMEM((SUBLANES, CARRY_ROWS + ROW_TILE, C_CONV), F32)]
        + tail_scratch(ROW_TILE, BF16) + x_scratch,
        compiler_params=pltpu.CompilerParams(
            dimension_semantics=("arbitrary",), vmem_limit_bytes=VMEM_LIMIT),
        name="mixer_prompt",
    )(x_p, mod_p, *moe_p_args, w_in_b, w_out_b, wconv_p, vec, g_mat, w_s, bs_full, w_router, b_router,
      smid, sh2, slogits)
    return xmid, h2, logits, cst, a_new, v_new


def _tile_plan(cnt):
    e_sub = lax.broadcasted_iota(jnp.int32, (N_EXPERTS, N_EXPERTS), 0)
    e_lane = lax.broadcasted_iota(jnp.int32, (N_EXPERTS, N_EXPERTS), 1)
    tiles = jnp.floor((cnt + (GEMM_TILE - 1.0)) * (1.0 / GEMM_TILE))
    active = jnp.where(cnt > 0.0, 1.0, 0.0)

    def over_experts(tri, col):
        return _dot(tri.astype(BF16), jnp.broadcast_to(col, (N_EXPERTS, LANES)).astype(BF16))[:, 0:1]
    tile_end = over_experts(jnp.where(e_lane <= e_sub, 1.0, 0.0), tiles)
    tile_start = tile_end - tiles
    ordinal = over_experts(jnp.where(e_lane < e_sub, 1.0, 0.0), active)

    g = lax.broadcasted_iota(jnp.int32, (1, PLAN_LANES), 1).astype(F32)
    e_id = lax.broadcasted_iota(jnp.int32, (N_EXPERTS, PLAN_LANES), 0).astype(F32)
    done = jnp.sum(jnp.where(tile_end <= g, 1.0, 0.0), axis=0, keepdims=True)
    live = done < float(N_EXPERTS)
    last_e = jnp.max(jnp.where(cnt > 0.0, e_id[:, 0:1], 0.0), axis=0, keepdims=True)
    te = jnp.where(live, done, last_e)
    pick = e_id == te

    def at_tile(col):
        return jnp.sum(jnp.where(pick, col, 0.0), axis=0, keepdims=True)
    start_g = at_tile(tile_start)
    ord_g = at_tile(ordinal)
    valid = jnp.clip(at_tile(cnt) - (g - start_g) * GEMM_TILE, 0.0, float(GEMM_TILE))
    valid = jnp.where(live, valid, 0.0)
    first = jnp.where(live, jnp.where(g == start_g, 1.0, 0.0), 0.0)
    slot = ord_g - 2.0 * jnp.floor(ord_g * 0.5)
    later = jnp.where(e_id > te, jnp.where(cnt > 0.0, e_id, float(N_EXPERTS)), float(N_EXPERTS))
    nxt = jnp.min(later, axis=0, keepdims=True)
    nxt = jnp.where(nxt >= float(N_EXPERTS), -1.0, nxt)

    row = lax.broadcasted_iota(jnp.int32, (PLAN_ROWS, PLAN_LANES), 0)
    plan = jnp.zeros((PLAN_ROWS, PLAN_LANES), F32)
    for r, v in ((PLAN_EXPERT, te), (PLAN_VALID, valid), (PLAN_FIRST, first), (PLAN_SLOT, slot),
                 (PLAN_NEXT, nxt)):
        plan = jnp.where(row == r, v, plan)
    return plan.astype(jnp.int32), tile_start * GEMM_TILE


def _router_kernel(lg_ref, wgt_ref, pos_ref, plan_ref, cnt_scr, run_scr, start_scr, earlier_scr):
    ph = pl.program_id(0)
    i = pl.program_id(1)
    n = lg_ref.shape[1]

    @pl.when(jnp.logical_and(ph == 0, i == 0))
    def _():
        cnt_scr[...] = jnp.zeros_like(cnt_scr)

    lg = lg_ref[...]
    sub = lax.broadcasted_iota(jnp.int32, lg.shape, 0).astype(F32)
    vals, sels = [], []
    for _ in range(TOP_K):
        m = jnp.max(lg, axis=0, keepdims=True)
        idx = jnp.min(jnp.where(lg == m, sub, float(N_EXPERTS)), axis=0, keepdims=True)
        sel = sub == idx
        vals.append(m)
        sels.append(sel)
        lg = jnp.where(sel, -jnp.inf, lg)
    onehot = jnp.zeros(lg.shape, F32)
    for sel in sels:
        onehot = onehot + jnp.where(sel, 1.0, 0.0)
    tile_cnt = jnp.sum(onehot, axis=1, keepdims=True)

    @pl.when(ph == 0)
    def _():
        cnt_scr[...] = cnt_scr[...] + tile_cnt

    @pl.when(jnp.logical_and(ph == 1, i == 0))
    def _():
        plan, row_start = _tile_plan(cnt_scr[...])
        plan_ref[...] = plan
        start_scr[...] = row_start
        run_scr[...] = jnp.zeros_like(run_scr)
        t_row = lax.broadcasted_iota(jnp.int32, (PREFIX_BLOCK, PREFIX_BLOCK), 0)
        t_col = lax.broadcasted_iota(jnp.int32, (PREFIX_BLOCK, PREFIX_BLOCK), 1)
        earlier_scr[...] = jnp.where(t_row < t_col, 1.0, 0.0).astype(BF16)

    @pl.when(ph == 1)
    def _():
        exps = [jnp.exp(v - vals[0]) for v in vals]
        inv = 1.0 / (exps[0] + exps[1] + exps[2] + exps[3])
        w_row = lax.broadcasted_iota(jnp.int32, (LANES, n), 0)
        w_all = jnp.zeros((LANES, n), F32)
        for k in range(TOP_K):
            w_all = jnp.where(w_row == k, exps[k] * inv, w_all)
        wgt_ref[...] = w_all.T
        base = run_scr[...] + start_scr[...]
        onehot_b = onehot.astype(BF16)
        for b0 in range(0, n, PREFIX_BLOCK):
            blk = slice(b0, b0 + PREFIX_BLOCK)
            slot0 = _dot(onehot_b[:, blk], earlier_scr[...]) + base
            for k in range(TOP_K):
                pos_k = jnp.sum(jnp.where(sels[k][:, blk], slot0, 0.0), axis=0, keepdims=True)
                pos_ref[k:k + 1, blk] = pos_k.astype(jnp.int32)
            base = base + jnp.sum(onehot[:, blk], axis=1, keepdims=True)
        run_scr[...] = run_scr[...] + tile_cnt


def _router(logits):
    t_all = logits.shape[1]
    col = pltpu.VMEM((N_EXPERTS, 1), F32)
    return pl.pallas_call(
        _router_kernel,
        grid=(2, t_all // ROUTER_TILE),
        in_specs=[pl.BlockSpec((N_EXPERTS, ROUTER_TILE), lambda ph, i: (0, i))],
        out_specs=(
            pl.BlockSpec((ROUTER_TILE, LANES), lambda ph, i: (i * ph, 0)),
            pl.BlockSpec((TOP_K, ROUTER_TILE), lambda ph, i: (0, i * ph)),
            _const_out_spec((PLAN_ROWS, PLAN_LANES)),
        ),
        out_shape=(
            jax.ShapeDtypeStruct((t_all, LANES), F32),
            jax.ShapeDtypeStruct((TOP_K, t_all), jnp.int32),
            jax.ShapeDtypeStruct((PLAN_ROWS, PLAN_LANES), jnp.int32),
        ),
        scratch_shapes=[col, col, col, pltpu.VMEM((PREFIX_BLOCK, PREFIX_BLOCK), BF16)],
        compiler_params=pltpu.CompilerParams(
            dimension_semantics=("arbitrary", "arbitrary"), vmem_limit_bytes=VMEM_LIMIT),
        name="router",
    )(logits)


def _expert_kernel(plan_ref, x_ref, w1_hbm, b1_ref, w2_hbm, b2_ref, o_ref,
                   w1f_scr, w2f_scr, w1b_scr, w2b_scr, sem, *, layer):
    def weight_copies(expert, slot):
        return (pltpu.make_async_copy(w1_hbm.at[layer, expert], w1f_scr.at[slot], sem.at[0, slot]),
                pltpu.make_async_copy(w2_hbm.at[layer, expert], w2f_scr.at[slot], sem.at[1, slot]))

    for sub in range(TILES_PER_STEP):
        _expert_tile(pl.program_id(0) * TILES_PER_STEP + sub, sub * GEMM_TILE, plan_ref, x_ref,
                     b1_ref, b2_ref, o_ref, w1f_scr, w2f_scr, w1b_scr, w2b_scr, weight_copies)


def _expert_tile(g, row0, plan_ref, x_ref, b1_ref, b2_ref, o_ref, w1f_scr, w2f_scr, w1b_scr, w2b_scr,
                 weight_copies):
    nv = plan_ref[PLAN_VALID, g]
    expert = plan_ref[PLAN_EXPERT, g]
    slot = plan_ref[PLAN_SLOT, g]
    first = plan_ref[PLAN_FIRST, g] == 1

    @pl.when(first)
    def _():
        nxt = plan_ref[PLAN_NEXT, g]

        @pl.when(g == 0)
        def _():
            for cp in weight_copies(expert, slot):
                cp.start()
        for cp in weight_copies(expert, slot):
            cp.wait()

        @pl.when(nxt >= 0)
        def _():
            for cp in weight_copies(nxt, 1 - slot):
                cp.start()

    def ffn(m, fresh):
        x_left, x_right = _unpack_bf16_pair(x_ref[row0:row0 + m, :])
        x = jnp.concatenate([x_left.astype(BF16), x_right.astype(BF16)], axis=1)

        def hidden(col0):
            cols = slice(col0, col0 + FF_BLOCK)
            if fresh:
                w1b_scr[:, cols] = w1f_scr[slot, :, cols].astype(BF16)
            return _dot(x, w1b_scr[:, cols]) + b1_ref[expert, :, cols]

        acts = []
        for jb in range(D_FF // FF_BLOCK):
            gg = jnp.minimum(hidden(jb * FF_BLOCK), SWIGLU_LIMIT)
            up = jnp.clip(hidden(D_FF + jb * FF_BLOCK), -SWIGLU_LIMIT, SWIGLU_LIMIT)
            acts.append((gg * _sigmoid(gg, SWIGLU_ALPHA) * (up + 1.0)).astype(BF16))
        if fresh:
            w2b_scr[...] = w2f_scr[slot].astype(BF16)
        y = _dot(jnp.concatenate(acts, axis=1), w2b_scr[...]) + b2_ref[expert]
        row = lax.broadcasted_iota(jnp.int32, y.shape, 0)
        o_ref[row0:row0 + m, :] = _pack_bf16_pair(jnp.where(row < nv, y, 0.0))
        if m < GEMM_TILE:
            o_ref[row0 + m:row0 + GEMM_TILE, :] = jnp.zeros((GEMM_TILE - m, D_WORDS), jnp.int32)

    full = nv > GEMM_TILE // 2
    half = jnp.logical_and(nv > 0, nv <= GEMM_TILE // 2)
    for rows, cond in ((GEMM_TILE, full), (GEMM_TILE // 2, half)):
        for fresh in (True, False):
            pl.when(jnp.logical_and(cond, first == fresh))(functools.partial(ffn, rows, fresh))

    @pl.when(nv == 0)
    def _():
        o_ref[row0:row0 + GEMM_TILE, :] = jnp.zeros((GEMM_TILE, D_WORDS), jnp.int32)


def _experts(xs, plan, w1, b1, w2, b2, layer):
    n_tiles = xs.shape[0] // GEMM_TILE
    depth = w1.shape[0]
    assert n_tiles % TILES_PER_STEP == 0
    step_rows = TILES_PER_STEP * GEMM_TILE
    bias_map = lambda s, plan: (layer, 0, 0, 0)
    grid_spec = pltpu.PrefetchScalarGridSpec(
        num_scalar_prefetch=1,
        grid=(n_tiles // TILES_PER_STEP,),
        in_specs=[
            pl.BlockSpec((step_rows, D_WORDS), lambda s, plan: (s, 0)),
            pl.BlockSpec(memory_space=pl.ANY),
            pl.BlockSpec((None, N_EXPERTS, 1, 2 * D_FF), bias_map, pipeline_mode=pl.Buffered(1)),
            pl.BlockSpec(memory_space=pl.ANY),
            pl.BlockSpec((None, N_EXPERTS, 1, D_MODEL), bias_map, pipeline_mode=pl.Buffered(1)),
        ],
        out_specs=pl.BlockSpec((step_rows, D_WORDS), lambda s, plan: (s, 0)),
        scratch_shapes=[
            pltpu.VMEM((2, D_MODEL, 2 * D_FF), F32),
            pltpu.VMEM((2, D_FF, D_MODEL), F32),
            pltpu.VMEM((D_MODEL, 2 * D_FF), BF16),
            pltpu.VMEM((D_FF, D_MODEL), BF16),
            pltpu.SemaphoreType.DMA((2, 2)),
        ],
    )
    return pl.pallas_call(
        functools.partial(_expert_kernel, layer=layer),
        grid_spec=grid_spec,
        out_shape=jax.ShapeDtypeStruct((n_tiles * GEMM_TILE, D_WORDS), jnp.int32),
        compiler_params=pltpu.CompilerParams(
            dimension_semantics=("arbitrary",), vmem_limit_bytes=VMEM_LIMIT),
        name="experts",
    )(plan, xs, w1, b1.reshape(depth, N_EXPERTS, 1, 2 * D_FF), w2,
      b2.reshape(depth, N_EXPERTS, 1, D_MODEL))


def _final_kernel(x_ref, ygp_ref, ygs_ref, wgt_ref, gp_ref, gs_ref, fg_ref, op_ref, os_ref, *,
                  n_prompt_blocks):
    i = pl.program_id(0)
    n_seq = gs_ref.shape[0]

    def run(gate, yg_ref, o_ref):
        def body(r0):
            r = pl.ds(r0, COMBINE_CHUNK)
            w = wgt_ref[r, :]
            g = gate(r0)
            xl = x_ref[r, 0:D_WORDS] + g[:, 0:D_WORDS] * _moe_half(yg_ref, w, r, False)
            xr = x_ref[r, D_WORDS:D_MODEL] + g[:, D_WORDS:D_MODEL] * _moe_half(yg_ref, w, r, True)
            ms = (jnp.sum(xl * xl, axis=-1, keepdims=True)
                  + jnp.sum(xr * xr, axis=-1, keepdims=True)) * (1.0 / D_MODEL)
            scale = lax.rsqrt(ms + EPS)
            o_ref[r, 0:D_WORDS] = xl * scale * fg_ref[:, 0:D_WORDS]
            o_ref[r, D_WORDS:D_MODEL] = xr * scale * fg_ref[:, D_WORDS:D_MODEL]
        _static_loop(x_ref.shape[0], body, COMBINE_CHUNK)

    @pl.when(i < n_prompt_blocks)
    def _():
        run(lambda r0: gp_ref[0], ygp_ref, op_ref)

    @pl.when(i >= n_prompt_blocks)
    def _():
        run(lambda r0: gs_ref[pl.ds(r0 % n_seq, COMBINE_CHUNK), :], ygs_ref, os_ref)


def _final(xmid, yg_p, yg_s, wgt, gate_p, gate_s, final_g, n_prompt, seq_len):
    t_all = xmid.shape[0]
    n_prompt_blocks = n_prompt // ROW_TILE
    per_seq = seq_len // ROW_TILE
    n_seq_p = gate_p.shape[0]
    assert t_all - n_prompt == ROW_TILE
    out_specs = (
        pl.BlockSpec((ROW_TILE, D_MODEL), lambda i: (jnp.minimum(i, n_prompt_blocks - 1), 0)),
        pl.BlockSpec((ROW_TILE, D_MODEL), lambda i: (0, 0)),
    )
    out_shape = (jax.ShapeDtypeStruct((n_prompt, D_MODEL), F32),
                 jax.ShapeDtypeStruct((ROW_TILE, D_MODEL), F32))
    return pl.pallas_call(
        functools.partial(_final_kernel, n_prompt_blocks=n_prompt_blocks),
        grid=(t_all // ROW_TILE,),
        in_specs=[
            pl.BlockSpec((ROW_TILE, D_MODEL), lambda i: (i, 0)),
            pl.BlockSpec((TOP_K, ROW_TILE, D_WORDS), lambda i: (0, jnp.minimum(i, n_prompt_blocks - 1), 0)),
            _const_spec((TOP_K, ROW_TILE, D_WORDS)),
            pl.BlockSpec((ROW_TILE, LANES), lambda i: (i, 0)),
            pl.BlockSpec((1, 1, D_MODEL), lambda i: (jnp.minimum(i // per_seq, n_seq_p - 1), 0, 0)),
            _const_spec(gate_s.shape),
            _const_spec((1, D_MODEL)),
        ],
        out_specs=out_specs,
        out_shape=out_shape,
        compiler_params=pltpu.CompilerParams(
            dimension_semantics=("arbitrary",), vmem_limit_bytes=VMEM_LIMIT),
        name="final_combine",
    )(xmid, yg_p, yg_s, wgt, gate_p.reshape(n_seq_p, 1, D_MODEL), gate_s, final_g.reshape(1, D_MODEL))


def _sc_mesh():
    return plsc.VectorSubcoreMesh(core_axis_name="core", subcore_axis_name="subcore",
                                  num_cores=SC_CORES, num_subcores=SC_SUBCORES)


def _sc_worker_id():
    return lax.axis_index("subcore") * SC_CORES + lax.axis_index("core")


def _sc_dispatch(h2w, pos_c, n_slots):
    _, n_chunks, _, chunk = pos_c.shape

    @functools.partial(
        pl.kernel, mesh=_sc_mesh(),
        out_type=jax.ShapeDtypeStruct((n_slots, D_WORDS), jnp.int32),
        scratch_types=[pltpu.VMEM((n_chunks, TOP_K, chunk), jnp.int32),
                       pltpu.VMEM((chunk, D_WORDS), jnp.int32),
                       pltpu.SemaphoreType.DMA],
        name="moe_dispatch")
    def run(h_hbm, pos_hbm, out_hbm, idx_v, rows_v, sem):
        wid = _sc_worker_id()
        pltpu.sync_copy(pos_hbm.at[wid], idx_v)

        @pl.loop(0, n_chunks)
        def _(c):
            base = pl.multiple_of((wid * n_chunks + c) * chunk, SUBLANES)
            pltpu.sync_copy(h_hbm.at[pl.ds(base, chunk)], rows_v)
            copies = [pltpu.async_copy(rows_v, out_hbm.at[idx_v.at[c, k]], sem) for k in range(TOP_K)]
            for cp in copies:
                cp.wait()

    return run(h2w, pos_c)


def _sc_gather_back(ys, pos_c, t_all):
    _, n_chunks, _, chunk = pos_c.shape

    @functools.partial(
        pl.kernel, mesh=_sc_mesh(),
        out_type=jax.ShapeDtypeStruct((TOP_K, t_all, D_WORDS), jnp.int32),
        scratch_types=[pltpu.VMEM((n_chunks, TOP_K, chunk), jnp.int32),
                       pltpu.VMEM((TOP_K, chunk, D_WORDS), jnp.int32),
                       pltpu.SemaphoreType.DMA,
                       pltpu.SemaphoreType.DMA],
        name="moe_gather_back")
    def run(ys_hbm, pos_hbm, out_hbm, idx_v, rows_v, sem_in, sem_out):
        wid = _sc_worker_id()
        pltpu.sync_copy(pos_hbm.at[wid], idx_v)

        @pl.loop(0, n_chunks)
        def _(c):
            base = pl.multiple_of((wid * n_chunks + c) * chunk, SUBLANES)
            gathers = [pltpu.async_copy(ys_hbm.at[idx_v.at[c, k]], rows_v.at[k], sem_in)
                       for k in range(TOP_K)]
            for cp in gathers:
                cp.wait()
            writes = [pltpu.async_copy(rows_v.at[k], out_hbm.at[k, pl.ds(base, chunk)], sem_out)
                      for k in range(TOP_K)]
            for cp in writes:
                cp.wait()

    return run(ys, pos_c)


def kernel(x_prompt, x_sample, state_conv, c_prompt, c_sample, w_ada, b_ada, w_in, w_conv, b_conv,
           gn_g, gn_b, sgu_ln_g, sgu_ln_b, w_s, b_s, beta_a, beta_b, w_out, w_router, b_router,
           w1, b1, w2, b2, final_g):
    n_bp, seq_len, _ = x_prompt.shape
    n_bs, n_pos, _ = x_sample.shape
    depth = w_ada.shape[0]
    n_prompt = n_bp * seq_len
    n_sample = n_bs * n_pos
    t_all = n_prompt + n_sample
    n_tiles = (t_all * TOP_K) // GEMM_TILE + N_EXPERTS
    n_workers = SC_CORES * SC_SUBCORES
    assert t_all % (n_workers * DISPATCH_CHUNK) == 0
    assert n_prompt % (n_workers * PROMPT_GATHER_CHUNK) == 0 and n_sample == n_workers * SAMPLE_GATHER_CHUNK
    assert n_tiles <= PLAN_LANES and t_all % ROUTER_TILE == 0

    x_p = x_prompt.reshape(n_prompt, D_MODEL)
    x_s = x_sample.transpose(1, 0, 2).reshape(n_sample, D_MODEL)
    x_s_block = 0
    moe = None
    c_all = jnp.concatenate([c_prompt, c_sample], axis=0)
    state_t = state_conv.transpose(0, 2, 1, 3)
    grp = jnp.arange(C_CONV) // (C_CONV // CONV_GROUPS)
    g_mat = ((grp[:, None] == grp[None, :]).astype(F32) / (C_CONV // CONV_GROUPS)).astype(BF16)

    wconv_p = jnp.broadcast_to(
        jnp.pad(w_conv, ((0, 0), (0, CARRY_ROWS - CONV_WIDTH), (0, 0)))[:, :, None, :],
        (depth, CARRY_ROWS, SUBLANES, C_CONV))
    vec = jnp.stack([b_conv, gn_g, gn_b, sgu_ln_g, sgu_ln_b, beta_a, beta_b, jnp.zeros_like(b_conv)],
                    axis=1)
    bs_full = jnp.repeat(b_s.transpose(0, 2, 1), SGU_HEAD_DIM, axis=2)
    wsv = jnp.repeat(w_s[:, :, :n_pos, :n_pos].transpose(0, 2, 3, 1).reshape(depth, n_pos * n_pos, SGU_HEADS),
                     SGU_HEAD_DIM, axis=2)
    bsv = jnp.repeat(b_s[:, :, :n_pos].transpose(0, 2, 1), SGU_HEAD_DIM, axis=2)
    w_router_t = w_router.transpose(0, 2, 1)
    b_router_c = b_router.reshape(depth, N_EXPERTS, 1)

    conv_p, conv_s, v_s = [], [], []
    gate_p = gate_s = None
    for l in range(depth):
        mods = _adaln(c_all, w_ada, b_ada, l)
        mod_p = mods[:, :n_bp].reshape(1, n_bp, 1, N_MOD * D_MODEL)
        mod_s = mods[:, n_bp:]
        xmid, h2, logits, cst, a_new, v_new = _mixer(
            l, x_p, x_s, x_s_block, moe, mod_p, mod_s, gate_p, gate_s, w_in[l:l + 1].astype(BF16),
            w_out[l:l + 1].astype(BF16), wconv_p, vec,
            g_mat, w_s, bs_full, wsv, bsv, w_router_t, b_router_c, state_t, n_bp, seq_len)
        gate_p = mod_p[..., 5 * D_MODEL:]
        gate_s = mod_s[..., 5 * D_MODEL:]
        conv_p.append(cst)
        conv_s.append(a_new)
        v_s.append(v_new)

        wgt, pos, plan = _router(logits)
        def per_worker(slots, chunk):
            return slots.reshape(TOP_K, n_workers, -1, chunk).transpose(1, 2, 0, 3)
        xs = _sc_dispatch(h2, per_worker(pos, DISPATCH_CHUNK), n_tiles * GEMM_TILE)
        ys = _experts(xs, plan, w1, b1, w2, b2, l)
        yg_s = _sc_gather_back(ys, per_worker(pos[:, n_prompt:], SAMPLE_GATHER_CHUNK), n_sample)
        yg_p = _sc_gather_back(ys, per_worker(pos[:, :n_prompt], PROMPT_GATHER_CHUNK), n_prompt)
        moe = (yg_p, yg_s, wgt)
        x_p = x_s = xmid
        x_s_block = n_prompt // n_sample

    y_p, y_s = _final(xmid, *moe, gate_p[0, :, 0], gate_s[0], final_g, n_prompt, seq_len)
    y_prompt = y_p.reshape(n_bp, seq_len, D_MODEL)
    y_sample = y_s.reshape(n_pos, n_bs, D_MODEL).transpose(1, 0, 2)
    new_conv_s = jnp.stack(conv_s, axis=0).transpose(0, 2, 1, 3)
    new_v_s = jnp.stack(v_s, axis=0).reshape(depth, n_pos, n_bs, C_SGU).transpose(0, 2, 1, 3)
    return (y_prompt, y_sample, jnp.stack(conv_p, axis=0), new_conv_s, new_v_s)
```

```python
import functools
import math

import jax
import jax.numpy as jnp
from jax import lax
from jax.experimental import pallas as pl
from jax.experimental.pallas import tpu as pltpu
from jax.experimental.pallas import tpu_sc as plsc

F32 = jnp.float32
BF16 = jnp.bfloat16

D_MODEL = 1024
C_CONV = 512
C_SGU = 512
CONV_GROUPS = 8
CONV_WIDTH = 31
CONV_STATE = CONV_WIDTH - 1
SGU_HEADS = 4
SGU_HEAD_DIM = C_SGU // SGU_HEADS
CHUNK = 128
N_EXPERTS = 32
TOP_K = 4
D_FF = 1024
SWIGLU_LIMIT = 7.0
SWIGLU_ALPHA = 1.702
N_MOD = 6
EPS = 1e-5
LOG2_E = 1.0 / math.log(2.0)

ROW_TILE = 512
ROUTER_TILE = 1536
PREFIX_BLOCK = 256
ADALN_COLS = 1024
ROW_CHUNK = 128
CONV_CHUNK = 32
COMBINE_CHUNK = 64
CARRY_ROWS = 32
SUBLANES = 8
LANES = 128
PLAN_ROWS = 8
PLAN_LANES = 256
PLAN_EXPERT, PLAN_VALID, PLAN_FIRST, PLAN_SLOT, PLAN_NEXT = range(5)
GEMM_TILE = 512
FF_BLOCK = 256
TILES_PER_STEP = 2
ROW_PARTS = 2
VMEM_LIMIT = 56 * 1024 * 1024
D_WORDS = D_MODEL // 2
DISPATCH_CHUNK = 88
GATHER_CHUNK = 48
HI_MASK = -65536
SC_CORES = 2
SC_SUBCORES = 16


def _rms(x):
    return x * lax.rsqrt(jnp.mean(x * x, axis=-1, keepdims=True) + EPS)


def _gelu(x):
    return 0.5 * x * (1.0 + lax.erf(x * (1.0 / math.sqrt(2.0))))


def _sigmoid(x, scale=1.0):
    return 1.0 / (1.0 + jnp.exp2(x * (-scale * LOG2_E)))


def _split_bf16(x):
    hi = x.astype(BF16)
    lo = (x - hi.astype(F32)).astype(BF16)
    return hi, lo


def _pack_bf16_pair(x):
    bits = lax.bitcast_convert_type(x.astype(BF16).astype(F32), jnp.int32)
    return lax.shift_right_logical(bits[:, :D_WORDS], 16) | (bits[:, D_WORDS:] & HI_MASK)


def _unpack_bf16_pair(w):
    left = lax.bitcast_convert_type(lax.shift_left(w, 16), F32)
    right = lax.bitcast_convert_type(w & HI_MASK, F32)
    return left, right


def _dot(a, b):
    return jnp.dot(a, b, preferred_element_type=F32)


def _dot_nt(a, b):
    return lax.dot_general(a, b, (((1,), (1,)), ((), ())), preferred_element_type=F32)


def _conv_tap(wconv_ref, k):
    return jnp.concatenate([wconv_ref[k]] * (CONV_CHUNK // SUBLANES), axis=0)


def _static_loop(n_rows, body, chunk=None):
    chunk = ROW_CHUNK if chunk is None else chunk
    for r0 in range(0, n_rows, chunk):
        body(r0)


def _row_parts(n_rows):
    step = n_rows // ROW_PARTS
    return [slice(p * step, (p + 1) * step) for p in range(ROW_PARTS)]


def _adaln_kernel(c_ref, w_ref, b_ref, o_ref):
    c = c_ref[...]
    s_hi, s_lo = _split_bf16(c * _sigmoid(c))
    w_hi, w_lo = _split_bf16(w_ref[0])
    acc = _dot(s_hi, w_hi) + _dot(s_hi, w_lo) + _dot(s_lo, w_hi)
    o_ref[0] = acc + b_ref[0]


def _adaln(c_all, w_ada, b_ada, layer):
    depth, _, n_out = w_ada.shape
    n_rows = c_all.shape[0]
    tn = ADALN_COLS
    return pl.pallas_call(
        _adaln_kernel,
        grid=(n_out // tn,),
        in_specs=[
            pl.BlockSpec((n_rows, D_MODEL), lambda j: (0, 0)),
            pl.BlockSpec((1, D_MODEL, tn), lambda j: (layer, 0, j)),
            pl.BlockSpec((1, 1, tn), lambda j: (layer, 0, j)),
        ],
        out_specs=pl.BlockSpec((1, n_rows, tn), lambda j: (0, 0, j)),
        out_shape=jax.ShapeDtypeStruct((1, n_rows, n_out), F32),
        compiler_params=pltpu.CompilerParams(
            dimension_semantics=("arbitrary",), vmem_limit_bytes=VMEM_LIMIT),
        name="adaln",
    )(c_all, w_ada, b_ada.reshape(depth, 1, n_out))


def _phase_in_norm(x_ref, h_scr, mod, n_rows):
    def body(r0):
        r = pl.ds(r0, ROW_CHUNK)
        h = _rms(x_ref[r, :]) * (1.0 + mod(1, r0)) + mod(0, r0)
        h_scr[r, :] = h.astype(BF16)
    _static_loop(n_rows, body)


def _phase_glu(z_scr, a_dst, a_off, n_rows):
    def body(r0):
        r = pl.ds(r0, ROW_CHUNK)
        a_dst[pl.ds(r0 + a_off, ROW_CHUNK), :] = z_scr[r, 0:C_CONV] * _sigmoid(z_scr[r, C_CONV:2 * C_CONV])
    _static_loop(n_rows, body)


def _phase_uv(z_scr, vec_ref, u_scr, v_scr, vout_ref, n_rows):
    ln_g = vec_ref[3:4, :]
    ln_b = vec_ref[4:5, :]

    def body(r0):
        r = pl.ds(r0, ROW_CHUNK)
        u_scr[r, :] = _gelu(z_scr[r, 2 * C_CONV:2 * C_CONV + C_SGU])
        gv = _gelu(z_scr[r, 2 * C_CONV + C_SGU:2 * C_CONV + 2 * C_SGU])
        mu = jnp.mean(gv, axis=-1, keepdims=True)
        dv = gv - mu
        var = jnp.mean(dv * dv, axis=-1, keepdims=True)
        v = dv * lax.rsqrt(var + EPS) * ln_g + ln_b
        if vout_ref is not None:
            vout_ref[r, :] = v
        v_scr[r, :] = v.astype(v_scr.dtype)
    _static_loop(n_rows, body)


def _phase_group_norm(conv_scr, convb_scr, stat_scr, g_ref, vec_ref, y_scr, n_rows):
    for rows in _row_parts(n_rows):
        stat_scr[rows, :] = _dot(convb_scr[rows, :], g_ref[...])

    def center(r0):
        r = pl.ds(r0, ROW_CHUNK)
        d = conv_scr[r, :] - stat_scr[r, :]
        conv_scr[r, :] = d
        convb_scr[r, :] = (d * d).astype(BF16)
    _static_loop(n_rows, center)
    for rows in _row_parts(n_rows):
        stat_scr[rows, :] = _dot(convb_scr[rows, :], g_ref[...])
    gn_g = vec_ref[1:2, :]
    gn_b = vec_ref[2:3, :]
    beta_a = vec_ref[5:6, :]

    def finish(r0):
        r = pl.ds(r0, ROW_CHUNK)
        gn = conv_scr[r, :] * lax.rsqrt(stat_scr[r, :] + EPS) * gn_g + gn_b
        ya = gn * _sigmoid(gn)
        y_scr[r, 0:C_CONV] = (_rms(ya) * beta_a).astype(BF16)
    _static_loop(n_rows, finish)


def _phase_out(x_ref, y_scr, wout_ref, z_scr, wr_ref, br_ref, mod, h_scr,
               xmid_ref, h2_ref, logit_ref, n_rows):
    for rows in _row_parts(n_rows):
        z_scr[rows, 0:D_MODEL] = (
            _dot(y_scr[rows, 0:C_CONV], wout_ref[0:C_CONV, :])
            + _dot(y_scr[rows, C_CONV:C_CONV + C_SGU], wout_ref[C_CONV:C_CONV + C_SGU, :]))

    def body(r0):
        r = pl.ds(r0, ROW_CHUNK)
        xm = x_ref[r, :] + mod(2, r0) * z_scr[r, 0:D_MODEL]
        xmid_ref[r, :] = xm
        h2 = _rms(xm) * (1.0 + mod(4, r0)) + mod(3, r0)
        hi, lo = _split_bf16(h2)
        h2_ref[r, :] = _pack_bf16_pair(h2)
        y_scr[r, :] = hi
        h_scr[r, :] = lo
    _static_loop(n_rows, body)
    w_hi, w_lo = _split_bf16(wr_ref[...])
    w_both = jnp.concatenate([w_hi, w_lo], axis=0)
    for rows in _row_parts(n_rows):
        both = _dot_nt(w_both, y_scr[rows, :])
        logit_ref[:, rows] = (both[0:N_EXPERTS] + both[N_EXPERTS:2 * N_EXPERTS]
                              + _dot_nt(w_hi, h_scr[rows, :])) + br_ref[...]


def _moe_half(yg_ref, w, r, hi_half):
    moe = jnp.zeros((COMBINE_CHUNK, D_WORDS), F32)
    for k in range(TOP_K):
        word = yg_ref[k, r, :]
        bits = (word & HI_MASK) if hi_half else lax.shift_left(word, 16)
        moe = moe + w[:, k:k + 1] * lax.bitcast_convert_type(bits, F32)
    return moe


def _phase_moe_residual(xmid_ref, yg_ref, wgt_ref, gate, x_dst, n_rows):
    def body(r0):
        r = pl.ds(r0, COMBINE_CHUNK)
        w = wgt_ref[r, :]
        g = gate(r0)
        for hi_half in (False, True):
            cols = slice(D_WORDS, D_MODEL) if hi_half else slice(0, D_WORDS)
            x_dst[r, cols] = xmid_ref[r, cols] + g[:, cols] * _moe_half(yg_ref, w, r, hi_half)
    _static_loop(n_rows, body, COMBINE_CHUNK)


def _mixer_prompt_kernel(*refs, tiles_per_seq, n_prompt_tiles, fused):
    x_ref, mod_ref = refs[:2]
    refs = refs[2:]
    if fused:
        yg_ref, wgt_ref, gate_ref = refs[:3]
        refs = refs[3:]
        x_scr = refs[-1]
        refs = refs[:-1]
    (win_ref, wout_ref, wconv_ref, vec_ref, g_ref, ws_ref, bs_ref, wr_ref, br_ref,
     smid_ref, sh2_ref, slogit_ref, xmid_ref, h2_ref, logit_ref, cst_ref,
     h_scr, z_scr, aext_scr, conv_scr, convb_scr, stat_scr, u_scr, v_scr, y_scr) = refs
    i = pl.program_id(0)

    @pl.when(i == n_prompt_tiles)
    def _():
        xmid_ref[...] = smid_ref[...]
        h2_ref[...] = sh2_ref[...]
        logit_ref[...] = slogit_ref[...]

    @pl.when(i < n_prompt_tiles)
    def _():
        x_src = x_ref
        if fused:
            _phase_moe_residual(x_ref, yg_ref, wgt_ref, lambda r0: gate_ref[0], x_scr, ROW_TILE)
            x_src = x_scr
        _mixer_prompt_tile(lax.rem(i, tiles_per_seq), tiles_per_seq,
                           x_src, mod_ref, win_ref, wout_ref, wconv_ref, vec_ref, g_ref,
                           ws_ref, bs_ref, wr_ref, br_ref, xmid_ref, h2_ref, logit_ref, cst_ref,
                           h_scr, z_scr, aext_scr, conv_scr, convb_scr, stat_scr, u_scr, v_scr, y_scr)


def _mixer_prompt_tile(j, tiles_per_seq, x_ref, mod_ref, win_ref, wout_ref, wconv_ref, vec_ref, g_ref,
                       ws_ref, bs_ref, wr_ref, br_ref, xmid_ref, h2_ref, logit_ref, cst_ref,
                       h_scr, z_scr, aext_scr, conv_scr, convb_scr, stat_scr, u_scr, v_scr, y_scr):
    n_rows = ROW_TILE

    def mod(idx, r0):
        del r0
        return mod_ref[0, :, idx * D_MODEL:(idx + 1) * D_MODEL]

    a_buf = aext_scr.at[0]

    @pl.when(j == 0)
    def _():
        a_buf[0:CARRY_ROWS, :] = jnp.zeros((CARRY_ROWS, C_CONV), F32)

    _phase_in_norm(x_ref, h_scr, mod, n_rows)
    n_glu = 2 * C_CONV
    for rows in _row_parts(n_rows):
        z_scr[rows, 0:n_glu] = _dot(h_scr[rows, :], win_ref[:, 0:n_glu])
    _phase_glu(z_scr, a_buf, CARRY_ROWS, n_rows)
    for rows in _row_parts(n_rows):
        z_scr[rows, n_glu:] = _dot(h_scr[rows, :], win_ref[:, n_glu:])
    n_shift = n_rows + CARRY_ROWS - SUBLANES
    n_head = n_rows // ROW_PARTS + CARRY_ROWS - SUBLANES
    for part in (slice(0, n_head), slice(n_head, n_shift)):
        for s in range(1, SUBLANES):
            aext_scr[s, part, :] = a_buf[part.start + s:part.stop + s, :]

    b_conv = vec_ref[0:1, :]
    lead = CARRY_ROWS - CONV_STATE
    for r0 in range(0, n_rows, CONV_CHUNK):
        acc = jnp.zeros((CONV_CHUNK, C_CONV), F32)
        for k in range(CONV_WIDTH):
            s = (lead + k) % SUBLANES
            q0 = r0 + (lead + k - s)
            acc = acc + _conv_tap(wconv_ref, k) * aext_scr[s, q0:q0 + CONV_CHUNK, :]
        conv = acc + b_conv
        conv_scr[r0:r0 + CONV_CHUNK, :] = conv
        convb_scr[r0:r0 + CONV_CHUNK, :] = conv.astype(BF16)
    _phase_uv(z_scr, vec_ref, u_scr, v_scr, None, n_rows)
    _phase_group_norm(conv_scr, convb_scr, stat_scr, g_ref, vec_ref, y_scr, n_rows)

    row_i = lax.broadcasted_iota(jnp.int32, (CHUNK, CHUNK), 0)
    col_i = lax.broadcasted_iota(jnp.int32, (CHUNK, CHUNK), 1)
    tril = (col_i <= row_i).astype(F32)
    for hd in range(SGU_HEADS):
        ws_h = (ws_ref[hd] * tril).astype(BF16)
        cols = slice(hd * SGU_HEAD_DIM, (hd + 1) * SGU_HEAD_DIM)
        s_cols = slice(n_glu + hd * SGU_HEAD_DIM, n_glu + (hd + 1) * SGU_HEAD_DIM)
        for c in range(n_rows // CHUNK):
            rows = slice(c * CHUNK, (c + 1) * CHUNK)
            z_scr[rows, s_cols] = _dot(ws_h, v_scr[rows, cols])
    beta_b = vec_ref[6:7, :]

    def gate_body(r0):
        r = pl.ds(r0, ROW_CHUNK)
        rb = pl.ds(r0 % CHUNK, ROW_CHUNK)
        yb = u_scr[r, :] * (z_scr[r, n_glu:n_glu + C_SGU] + bs_ref[rb, :])
        y_scr[r, C_CONV:C_CONV + C_SGU] = (_rms(yb) * beta_b).astype(BF16)
    _static_loop(n_rows, gate_body)

    _phase_out(x_ref, y_scr, wout_ref, z_scr, wr_ref, br_ref, mod, h_scr,
               xmid_ref, h2_ref, logit_ref, n_rows)

    @pl.when(j == tiles_per_seq - 1)
    def _():
        cst_ref[0] = a_buf[n_rows + lead:n_rows + CARRY_ROWS, :]

    a_buf[0:CARRY_ROWS, :] = a_buf[n_rows:n_rows + CARRY_ROWS, :]


def _mixer_sample_kernel(*refs, fused):
    x_ref, mod_ref = refs[:2]
    refs = refs[2:]
    if fused:
        yg_ref, wgt_ref, gate_ref = refs[:3]
        refs = refs[3:]
        x_scr = refs[-1]
        refs = refs[:-1]
    (win_ref, wout_ref, wconv_ref, vec_ref, g_ref, wsv_ref, bsv_ref, wr_ref, br_ref, state_ref,
     xmid_ref, h2_ref, logit_ref, cnew_ref, vout_ref,
     h_scr, z_scr, conv_scr, convb_scr, stat_scr, u_scr, v_scr, y_scr, anew_ref) = refs
    n_rows = x_ref.shape[0]
    n_seq = state_ref.shape[1]
    n_pos = n_rows // n_seq

    def mod(idx, r0):
        return mod_ref[pl.ds(r0 % n_seq, ROW_CHUNK), idx * D_MODEL:(idx + 1) * D_MODEL]

    if fused:
        def gate(r0):
            return gate_ref[pl.ds(r0 % n_seq, COMBINE_CHUNK), :]
        _phase_moe_residual(x_ref, yg_ref, wgt_ref, gate, x_scr, n_rows)
        x_ref = x_scr

    _phase_in_norm(x_ref, h_scr, mod, n_rows)
    z_scr[...] = _dot(h_scr[...], win_ref[...])
    _phase_glu(z_scr, anew_ref, 0, n_rows)
    _phase_uv(z_scr, vec_ref, u_scr, v_scr, vout_ref, n_rows)

    keep = CONV_STATE - n_pos
    cnew_ref[0:keep] = state_ref[n_pos:CONV_STATE]
    for t in range(n_pos):
        cnew_ref[keep + t] = anew_ref[t * n_seq:(t + 1) * n_seq, :]

    b_conv = vec_ref[0:1, :]

    def conv_body(s0):
        rs = pl.ds(s0, CONV_CHUNK)
        for t in range(n_pos):
            acc = jnp.zeros((CONV_CHUNK, C_CONV), F32)
            for k in range(CONV_WIDTH):
                p = t + k
                if p < CONV_STATE:
                    src = state_ref[p, rs, :]
                else:
                    src = anew_ref[pl.ds(s0 + (p - CONV_STATE) * n_seq, CONV_CHUNK), :]
                acc = acc + _conv_tap(wconv_ref, k) * src
            conv = acc + b_conv
            r = pl.ds(s0 + t * n_seq, CONV_CHUNK)
            conv_scr[r, :] = conv
            convb_scr[r, :] = conv.astype(BF16)
    _static_loop(n_seq, conv_body, CONV_CHUNK)

    _phase_group_norm(conv_scr, convb_scr, stat_scr, g_ref, vec_ref, y_scr, n_rows)

    beta_b = vec_ref[6:7, :]

    def gate_body(s0):
        for t in range(n_pos):
            s = jnp.zeros((CONV_CHUNK, C_SGU), F32)
            for jj in range(t + 1):
                s = s + wsv_ref[t * n_pos + jj:t * n_pos + jj + 1, :] * v_scr[pl.ds(s0 + jj * n_seq, CONV_CHUNK), :]
            r = pl.ds(s0 + t * n_seq, CONV_CHUNK)
            yb = u_scr[r, :] * (s + bsv_ref[t:t + 1, :])
            y_scr[r, C_CONV:C_CONV + C_SGU] = (_rms(yb) * beta_b).astype(BF16)
    _static_loop(n_seq, gate_body, CONV_CHUNK)

    _phase_out(x_ref, y_scr, wout_ref, z_scr, wr_ref, br_ref, mod, h_scr,
               xmid_ref, h2_ref, logit_ref, n_rows)


def _const_spec(shape):
    nd = len(shape)
    return pl.BlockSpec(shape, lambda *_: (0,) * nd, pipeline_mode=pl.Buffered(1))


def _const_out_spec(shape):
    nd = len(shape)
    return pl.BlockSpec(shape, lambda *_: (0,) * nd)


def _layer_spec(shape, layer):
    nd = len(shape)
    return pl.BlockSpec((None,) + tuple(shape), lambda *_: (layer,) + (0,) * nd,
                        pipeline_mode=pl.Buffered(1))


def _mixer(layer, x_p, x_s, x_s_block, moe, mod_p, mod_s, gate_p, gate_s, w_in_b, w_out_b, wconv_p, vec,
           g_mat, w_s, bs_full, wsv, bsv, w_router, b_router, state_t, n_prompt_seq, seq_len):
    fused = moe is not None
    n_prompt = n_prompt_seq * seq_len
    n_seq, n_pos = state_t.shape[2], bsv.shape[1]
    n_sample = n_seq * n_pos
    t_all = n_prompt + n_sample
    nj = seq_len // ROW_TILE
    weight_specs = [
        _layer_spec((D_MODEL, 2 * C_CONV + 2 * C_SGU), 0),
        _layer_spec((D_MODEL, D_MODEL), 0),
        _layer_spec((CARRY_ROWS, SUBLANES, C_CONV), layer),
        _layer_spec((SUBLANES, C_CONV), layer),
        _const_spec((C_CONV, C_CONV)),
    ]
    router_specs = [_layer_spec((N_EXPERTS, D_MODEL), layer), _layer_spec((N_EXPERTS, 1), layer)]
    common_scratch = lambda n: [
        pltpu.VMEM((n, D_MODEL), BF16),
        pltpu.VMEM((n, 2 * C_CONV + 2 * C_SGU), F32),
    ]
    tail_scratch = lambda n, vdt: [
        pltpu.VMEM((n, C_CONV), F32),
        pltpu.VMEM((n, C_CONV), BF16),
        pltpu.VMEM((n, C_CONV), F32),
        pltpu.VMEM((n, C_SGU), F32),
        pltpu.VMEM((n, C_SGU), vdt),
        pltpu.VMEM((n, D_MODEL), BF16),
    ]
    assert n_sample == ROW_TILE
    n_tiles_p = n_prompt // ROW_TILE
    tile_p = lambda i: jnp.minimum(i, n_tiles_p - 1)
    seq_of = lambda i: jnp.minimum(i // nj, n_prompt_seq - 1)
    moe_s_specs, moe_p_specs, moe_s_args, moe_p_args, x_scratch = [], [], [], [], []
    if fused:
        yg, wgt = moe
        moe_s_specs = [pl.BlockSpec((TOP_K, n_sample, D_WORDS), lambda i: (0, x_s_block, 0)),
                       pl.BlockSpec((n_sample, LANES), lambda i: (x_s_block, 0)),
                       _layer_spec((n_seq, D_MODEL), 0)]
        moe_p_specs = [pl.BlockSpec((TOP_K, ROW_TILE, D_WORDS), lambda i: (0, tile_p(i), 0)),
                       pl.BlockSpec((ROW_TILE, LANES), lambda i: (tile_p(i), 0)),
                       pl.BlockSpec((None, 1, 1, D_MODEL), lambda i: (0, seq_of(tile_p(i)), 0, 0))]
        moe_s_args = [yg, wgt, gate_s]
        moe_p_args = [yg, wgt, gate_p]
        x_scratch = [pltpu.VMEM((ROW_TILE, D_MODEL), F32)]
    smid, sh2, slogits, a_new, v_new = pl.pallas_call(
        functools.partial(_mixer_sample_kernel, fused=fused),
        grid=(1,),
        in_specs=[
            pl.BlockSpec((n_sample, D_MODEL), lambda i: (x_s_block, 0)),
            _layer_spec((n_seq, N_MOD * D_MODEL), 0),
            *moe_s_specs,
            *weight_specs,
            _layer_spec((n_pos * n_pos, C_SGU), layer),
            _layer_spec((n_pos, C_SGU), layer),
            *router_specs,
            _layer_spec((CONV_STATE, n_seq, C_CONV), layer),
        ],
        out_specs=(
            _const_out_spec((n_sample, D_MODEL)),
            _const_out_spec((n_sample, D_WORDS)),
            _const_out_spec((N_EXPERTS, n_sample)),
            _const_out_spec((CONV_STATE, n_seq, C_CONV)),
            _const_out_spec((n_sample, C_SGU)),
        ),
        out_shape=(
            jax.ShapeDtypeStruct((n_sample, D_MODEL), F32),
            jax.ShapeDtypeStruct((n_sample, D_WORDS), jnp.int32),
            jax.ShapeDtypeStruct((N_EXPERTS, n_sample), F32),
            jax.ShapeDtypeStruct((CONV_STATE, n_seq, C_CONV), F32),
            jax.ShapeDtypeStruct((n_sample, C_SGU), F32),
        ),
        scratch_shapes=common_scratch(n_sample) + tail_scratch(n_sample, F32)
        + [pltpu.VMEM((n_sample, C_CONV), F32)] + x_scratch,
        compiler_params=pltpu.CompilerParams(
            dimension_semantics=("arbitrary",), vmem_limit_bytes=VMEM_LIMIT),
        name="mixer_sample",
    )(x_s, mod_s, *moe_s_args, w_in_b, w_out_b, wconv_p, vec, g_mat, wsv, bsv, w_router, b_router,
      state_t)

    xmid, h2, logits, cst = pl.pallas_call(
        functools.partial(_mixer_prompt_kernel, tiles_per_seq=nj, n_prompt_tiles=n_tiles_p,
                          fused=fused),
        grid=(n_tiles_p + 1,),
        in_specs=[
            pl.BlockSpec((ROW_TILE, D_MODEL), lambda i: (tile_p(i), 0)),
            pl.BlockSpec((None, 1, 1, N_MOD * D_MODEL), lambda i: (0, seq_of(i), 0, 0)),
            *moe_p_specs,
            *weight_specs,
            _layer_spec((SGU_HEADS, CHUNK, CHUNK), layer),
            _layer_spec((CHUNK, C_SGU), layer),
            *router_specs,
            _const_spec((n_sample, D_MODEL)),
            _const_spec((n_sample, D_WORDS)),
            _const_spec((N_EXPERTS, n_sample)),
        ],
        out_specs=(
            pl.BlockSpec((ROW_TILE, D_MODEL), lambda i: (i, 0)),
            pl.BlockSpec((ROW_TILE, D_WORDS), lambda i: (i, 0)),
            pl.BlockSpec((N_EXPERTS, ROW_TILE), lambda i: (0, i)),
            pl.BlockSpec((1, CONV_STATE, C_CONV), lambda i: (seq_of(i), 0, 0)),
        ),
        out_shape=(
            jax.ShapeDtypeStruct((t_all, D_MODEL), F32),
            jax.ShapeDtypeStruct((t_all, D_WORDS), jnp.int32),
            jax.ShapeDtypeStruct((N_EXPERTS, t_all), F32),
            jax.ShapeDtypeStruct((n_prompt_seq, CONV_STATE, C_CONV), F32),
        ),
        scratch_shapes=common_scratch(ROW_TILE)
        + [pltpu.VMEM((SUBLANES, CARRY_ROWS + ROW_TILE, C_CONV), F32)]
        + tail_scratch(ROW_TILE, BF16) + x_scratch,
        compiler_params=pltpu.CompilerParams(
            dimension_semantics=("arbitrary",), vmem_limit_bytes=VMEM_LIMIT),
        name="mixer_prompt",
    )(x_p, mod_p, *moe_p_args, w_in_b, w_out_b, wconv_p, vec, g_mat, w_s, bs_full, w_router, b_router,
      smid, sh2, slogits)
    return xmid, h2, logits, cst, a_new, v_new


def _tile_plan(cnt):
    e_sub = lax.broadcasted_iota(jnp.int32, (N_EXPERTS, N_EXPERTS), 0)
    e_lane = lax.broadcasted_iota(jnp.int32, (N_EXPERTS, N_EXPERTS), 1)
    tiles = jnp.floor((cnt + (GEMM_TILE - 1.0)) * (1.0 / GEMM_TILE))
    active = jnp.where(cnt > 0.0, 1.0, 0.0)

    def over_experts(tri, col):
        return _dot(tri.astype(BF16), jnp.broadcast_to(col, (N_EXPERTS, LANES)).astype(BF16))[:, 0:1]
    tile_end = over_experts(jnp.where(e_lane <= e_sub, 1.0, 0.0), tiles)
    tile_start = tile_end - tiles
    ordinal = over_experts(jnp.where(e_lane < e_sub, 1.0, 0.0), active)

    g = lax.broadcasted_iota(jnp.int32, (1, PLAN_LANES), 1).astype(F32)
    e_id = lax.broadcasted_iota(jnp.int32, (N_EXPERTS, PLAN_LANES), 0).astype(F32)
    done = jnp.sum(jnp.where(tile_end <= g, 1.0, 0.0), axis=0, keepdims=True)
    live = done < float(N_EXPERTS)
    last_e = jnp.max(jnp.where(cnt > 0.0, e_id[:, 0:1], 0.0), axis=0, keepdims=True)
    te = jnp.where(live, done, last_e)
    pick = e_id == te

    def at_tile(col):
        return jnp.sum(jnp.where(pick, col, 0.0), axis=0, keepdims=True)
    start_g = at_tile(tile_start)
    ord_g = at_tile(ordinal)
    valid = jnp.clip(at_tile(cnt) - (g - start_g) * GEMM_TILE, 0.0, float(GEMM_TILE))
    valid = jnp.where(live, valid, 0.0)
    first = jnp.where(live, jnp.where(g == start_g, 1.0, 0.0), 0.0)
    slot = ord_g - 2.0 * jnp.floor(ord_g * 0.5)
    later = jnp.where(e_id > te, jnp.where(cnt > 0.0, e_id, float(N_EXPERTS)), float(N_EXPERTS))
    nxt = jnp.min(later, axis=0, keepdims=True)
    nxt = jnp.where(nxt >= float(N_EXPERTS), -1.0, nxt)

    row = lax.broadcasted_iota(jnp.int32, (PLAN_ROWS, PLAN_LANES), 0)
    plan = jnp.zeros((PLAN_ROWS, PLAN_LANES), F32)
    for r, v in ((PLAN_EXPERT, te), (PLAN_VALID, valid), (PLAN_FIRST, first), (PLAN_SLOT, slot),
                 (PLAN_NEXT, nxt)):
        plan = jnp.where(row == r, v, plan)
    return plan.astype(jnp.int32), tile_start * GEMM_TILE


def _router_kernel(lg_ref, wgt_ref, pos_ref, plan_ref, cnt_scr, run_scr, start_scr, earlier_scr):
    ph = pl.program_id(0)
    i = pl.program_id(1)
    n = lg_ref.shape[1]

    @pl.when(jnp.logical_and(ph == 0, i == 0))
    def _():
        cnt_scr[...] = jnp.zeros_like(cnt_scr)

    lg = lg_ref[...]
    sub = lax.broadcasted_iota(jnp.int32, lg.shape, 0).astype(F32)
    vals, sels = [], []
    for _ in range(TOP_K):
        m = jnp.max(lg, axis=0, keepdims=True)
        idx = jnp.min(jnp.where(lg == m, sub, float(N_EXPERTS)), axis=0, keepdims=True)
        sel = sub == idx
        vals.append(m)
        sels.append(sel)
        lg = jnp.where(sel, -jnp.inf, lg)
    onehot = jnp.zeros(lg.shape, F32)
    for sel in sels:
        onehot = onehot + jnp.where(sel, 1.0, 0.0)
    tile_cnt = jnp.sum(onehot, axis=1, keepdims=True)

    @pl.when(ph == 0)
    def _():
        cnt_scr[...] = cnt_scr[...] + tile_cnt

    @pl.when(jnp.logical_and(ph == 1, i == 0))
    def _():
        plan, row_start = _tile_plan(cnt_scr[...])
        plan_ref[...] = plan
        start_scr[...] = row_start
        run_scr[...] = jnp.zeros_like(run_scr)
        t_row = lax.broadcasted_iota(jnp.int32, (PREFIX_BLOCK, PREFIX_BLOCK), 0)
        t_col = lax.broadcasted_iota(jnp.int32, (PREFIX_BLOCK, PREFIX_BLOCK), 1)
        earlier_scr[...] = jnp.where(t_row < t_col, 1.0, 0.0).astype(BF16)

    @pl.when(ph == 1)
    def _():
        exps = [jnp.exp(v - vals[0]) for v in vals]
        inv = 1.0 / (exps[0] + exps[1] + exps[2] + exps[3])
        w_row = lax.broadcasted_iota(jnp.int32, (LANES, n), 0)
        w_all = jnp.zeros((LANES, n), F32)
        for k in range(TOP_K):
            w_all = jnp.where(w_row == k, exps[k] * inv, w_all)
        wgt_ref[...] = w_all.T
        base = run_scr[...] + start_scr[...]
        onehot_b = onehot.astype(BF16)
        for b0 in range(0, n, PREFIX_BLOCK):
            blk = slice(b0, b0 + PREFIX_BLOCK)
            slot0 = _dot(onehot_b[:, blk], earlier_scr[...]) + base
            for k in range(TOP_K):
                pos_k = jnp.sum(jnp.where(sels[k][:, blk], slot0, 0.0), axis=0, keepdims=True)
                pos_ref[k:k + 1, blk] = pos_k.astype(jnp.int32)
            base = base + jnp.sum(onehot[:, blk], axis=1, keepdims=True)
        run_scr[...] = run_scr[...] + tile_cnt


def _router(logits):
    t_all = logits.shape[1]
    col = pltpu.VMEM((N_EXPERTS, 1), F32)
    return pl.pallas_call(
        _router_kernel,
        grid=(2, t_all // ROUTER_TILE),
        in_specs=[pl.BlockSpec((N_EXPERTS, ROUTER_TILE), lambda ph, i: (0, i))],
        out_specs=(
            pl.BlockSpec((ROUTER_TILE, LANES), lambda ph, i: (i * ph, 0)),
            pl.BlockSpec((TOP_K, ROUTER_TILE), lambda ph, i: (0, i * ph)),
            _const_out_spec((PLAN_ROWS, PLAN_LANES)),
        ),
        out_shape=(
            jax.ShapeDtypeStruct((t_all, LANES), F32),
            jax.ShapeDtypeStruct((TOP_K, t_all), jnp.int32),
            jax.ShapeDtypeStruct((PLAN_ROWS, PLAN_LANES), jnp.int32),
        ),
        scratch_shapes=[col, col, col, pltpu.VMEM((PREFIX_BLOCK, PREFIX_BLOCK), BF16)],
        compiler_params=pltpu.CompilerParams(
            dimension_semantics=("arbitrary", "arbitrary"), vmem_limit_bytes=VMEM_LIMIT),
        name="router",
    )(logits)


def _expert_kernel(plan_ref, x_ref, w1_hbm, b1_ref, w2_hbm, b2_ref, o_ref,
                   w1f_scr, w2f_scr, w1b_scr, w2b_scr, sem, *, layer):
    def weight_copies(expert, slot):
        return (pltpu.make_async_copy(w1_hbm.at[layer, expert], w1f_scr.at[slot], sem.at[0, slot]),
                pltpu.make_async_copy(w2_hbm.at[layer, expert], w2f_scr.at[slot], sem.at[1, slot]))

    for sub in range(TILES_PER_STEP):
        _expert_tile(pl.program_id(0) * TILES_PER_STEP + sub, sub * GEMM_TILE, plan_ref, x_ref,
                     b1_ref, b2_ref, o_ref, w1f_scr, w2f_scr, w1b_scr, w2b_scr, weight_copies)


def _expert_tile(g, row0, plan_ref, x_ref, b1_ref, b2_ref, o_ref, w1f_scr, w2f_scr, w1b_scr, w2b_scr,
                 weight_copies):
    nv = plan_ref[PLAN_VALID, g]
    expert = plan_ref[PLAN_EXPERT, g]
    slot = plan_ref[PLAN_SLOT, g]
    first = plan_ref[PLAN_FIRST, g] == 1

    @pl.when(first)
    def _():
        nxt = plan_ref[PLAN_NEXT, g]

        @pl.when(g == 0)
        def _():
            for cp in weight_copies(expert, slot):
                cp.start()
        for cp in weight_copies(expert, slot):
            cp.wait()

        @pl.when(nxt >= 0)
        def _():
            for cp in weight_copies(nxt, 1 - slot):
                cp.start()

    def ffn(m, fresh):
        x_left, x_right = _unpack_bf16_pair(x_ref[row0:row0 + m, :])
        x = jnp.concatenate([x_left.astype(BF16), x_right.astype(BF16)], axis=1)

        def hidden(col0):
            cols = slice(col0, col0 + FF_BLOCK)
            if fresh:
                w1b_scr[:, cols] = w1f_scr[slot, :, cols].astype(BF16)
            return _dot(x, w1b_scr[:, cols]) + b1_ref[expert, :, cols]

        acts = []
        for jb in range(D_FF // FF_BLOCK):
            gg = jnp.minimum(hidden(jb * FF_BLOCK), SWIGLU_LIMIT)
            up = jnp.clip(hidden(D_FF + jb * FF_BLOCK), -SWIGLU_LIMIT, SWIGLU_LIMIT)
            acts.append((gg * _sigmoid(gg, SWIGLU_ALPHA) * (up + 1.0)).astype(BF16))
        if fresh:
            w2b_scr[...] = w2f_scr[slot].astype(BF16)
        y = _dot(jnp.concatenate(acts, axis=1), w2b_scr[...]) + b2_ref[expert]
        row = lax.broadcasted_iota(jnp.int32, y.shape, 0)
        o_ref[row0:row0 + m, :] = _pack_bf16_pair(jnp.where(row < nv, y, 0.0))
        if m < GEMM_TILE:
            o_ref[row0 + m:row0 + GEMM_TILE, :] = jnp.zeros((GEMM_TILE - m, D_WORDS), jnp.int32)

    full = nv > GEMM_TILE // 2
    half = jnp.logical_and(nv > 0, nv <= GEMM_TILE // 2)
    for rows, cond in ((GEMM_TILE, full), (GEMM_TILE // 2, half)):
        for fresh in (True, False):
            pl.when(jnp.logical_and(cond, first == fresh))(functools.partial(ffn, rows, fresh))

    @pl.when(nv == 0)
    def _():
        o_ref[row0:row0 + GEMM_TILE, :] = jnp.zeros((GEMM_TILE, D_WORDS), jnp.int32)


def _experts(xs, plan, w1, b1, w2, b2, layer):
    n_tiles = xs.shape[0] // GEMM_TILE
    depth = w1.shape[0]
    assert n_tiles % TILES_PER_STEP == 0
    step_rows = TILES_PER_STEP * GEMM_TILE
    bias_map = lambda s, plan: (layer, 0, 0, 0)
    grid_spec = pltpu.PrefetchScalarGridSpec(
        num_scalar_prefetch=1,
        grid=(n_tiles // TILES_PER_STEP,),
        in_specs=[
            pl.BlockSpec((step_rows, D_WORDS), lambda s, plan: (s, 0)),
            pl.BlockSpec(memory_space=pl.ANY),
            pl.BlockSpec((None, N_EXPERTS, 1, 2 * D_FF), bias_map, pipeline_mode=pl.Buffered(1)),
            pl.BlockSpec(memory_space=pl.ANY),
            pl.BlockSpec((None, N_EXPERTS, 1, D_MODEL), bias_map, pipeline_mode=pl.Buffered(1)),
        ],
        out_specs=pl.BlockSpec((step_rows, D_WORDS), lambda s, plan: (s, 0)),
        scratch_shapes=[
            pltpu.VMEM((2, D_MODEL, 2 * D_FF), F32),
            pltpu.VMEM((2, D_FF, D_MODEL), F32),
            pltpu.VMEM((D_MODEL, 2 * D_FF), BF16),
            pltpu.VMEM((D_FF, D_MODEL), BF16),
            pltpu.SemaphoreType.DMA((2, 2)),
        ],
    )
    return pl.pallas_call(
        functools.partial(_expert_kernel, layer=layer),
        grid_spec=grid_spec,
        out_shape=jax.ShapeDtypeStruct((n_tiles * GEMM_TILE, D_WORDS), jnp.int32),
        compiler_params=pltpu.CompilerParams(
            dimension_semantics=("arbitrary",), vmem_limit_bytes=VMEM_LIMIT),
        name="experts",
    )(plan, xs, w1, b1.reshape(depth, N_EXPERTS, 1, 2 * D_FF), w2,
      b2.reshape(depth, N_EXPERTS, 1, D_MODEL))


def _final_kernel(x_ref, yg_ref, wgt_ref, gp_ref, gs_ref, fg_ref, op_ref, os_ref, *, n_prompt_blocks):
    i = pl.program_id(0)
    n_seq = gs_ref.shape[0]

    def run(gate, o_ref):
        def body(r0):
            r = pl.ds(r0, COMBINE_CHUNK)
            w = wgt_ref[r, :]
            g = gate(r0)
            xl = x_ref[r, 0:D_WORDS] + g[:, 0:D_WORDS] * _moe_half(yg_ref, w, r, False)
            xr = x_ref[r, D_WORDS:D_MODEL] + g[:, D_WORDS:D_MODEL] * _moe_half(yg_ref, w, r, True)
            ms = (jnp.sum(xl * xl, axis=-1, keepdims=True)
                  + jnp.sum(xr * xr, axis=-1, keepdims=True)) * (1.0 / D_MODEL)
            scale = lax.rsqrt(ms + EPS)
            o_ref[r, 0:D_WORDS] = xl * scale * fg_ref[:, 0:D_WORDS]
            o_ref[r, D_WORDS:D_MODEL] = xr * scale * fg_ref[:, D_WORDS:D_MODEL]
        _static_loop(x_ref.shape[0], body, COMBINE_CHUNK)

    @pl.when(i < n_prompt_blocks)
    def _():
        run(lambda r0: gp_ref[0], op_ref)

    @pl.when(i >= n_prompt_blocks)
    def _():
        run(lambda r0: gs_ref[pl.ds(r0 % n_seq, COMBINE_CHUNK), :], os_ref)


def _final(xmid, yg, wgt, gate_p, gate_s, final_g, n_prompt, seq_len):
    t_all = xmid.shape[0]
    n_prompt_blocks = n_prompt // ROW_TILE
    per_seq = seq_len // ROW_TILE
    n_seq_p = gate_p.shape[0]
    assert t_all - n_prompt == ROW_TILE
    out_specs = (
        pl.BlockSpec((ROW_TILE, D_MODEL), lambda i: (jnp.minimum(i, n_prompt_blocks - 1), 0)),
        pl.BlockSpec((ROW_TILE, D_MODEL), lambda i: (0, 0)),
    )
    out_shape = (jax.ShapeDtypeStruct((n_prompt, D_MODEL), F32),
                 jax.ShapeDtypeStruct((ROW_TILE, D_MODEL), F32))
    return pl.pallas_call(
        functools.partial(_final_kernel, n_prompt_blocks=n_prompt_blocks),
        grid=(t_all // ROW_TILE,),
        in_specs=[
            pl.BlockSpec((ROW_TILE, D_MODEL), lambda i: (i, 0)),
            pl.BlockSpec((TOP_K, ROW_TILE, D_WORDS), lambda i: (0, i, 0)),
            pl.BlockSpec((ROW_TILE, LANES), lambda i: (i, 0)),
            pl.BlockSpec((1, 1, D_MODEL), lambda i: (jnp.minimum(i // per_seq, n_seq_p - 1), 0, 0)),
            _const_spec(gate_s.shape),
            _const_spec((1, D_MODEL)),
        ],
        out_specs=out_specs,
        out_shape=out_shape,
        compiler_params=pltpu.CompilerParams(
            dimension_semantics=("arbitrary",), vmem_limit_bytes=VMEM_LIMIT),
        name="final_combine",
    )(xmid, yg, wgt, gate_p.reshape(n_seq_p, 1, D_MODEL), gate_s, final_g.reshape(1, D_MODEL))


def _sc_mesh():
    return plsc.VectorSubcoreMesh(core_axis_name="core", subcore_axis_name="subcore",
                                  num_cores=SC_CORES, num_subcores=SC_SUBCORES)


def _sc_worker_id():
    return lax.axis_index("subcore") * SC_CORES + lax.axis_index("core")


def _sc_dispatch(h2w, pos_c, n_slots):
    _, n_chunks, _, chunk = pos_c.shape

    @functools.partial(
        pl.kernel, mesh=_sc_mesh(),
        out_type=jax.ShapeDtypeStruct((n_slots, D_WORDS), jnp.int32),
        scratch_types=[pltpu.VMEM((n_chunks, TOP_K, chunk), jnp.int32),
                       pltpu.VMEM((chunk, D_WORDS), jnp.int32),
                       pltpu.SemaphoreType.DMA],
        name="moe_dispatch")
    def run(h_hbm, pos_hbm, out_hbm, idx_v, rows_v, sem):
        wid = _sc_worker_id()
        pltpu.sync_copy(pos_hbm.at[wid], idx_v)

        @pl.loop(0, n_chunks)
        def _(c):
            base = pl.multiple_of((wid * n_chunks + c) * chunk, SUBLANES)
            pltpu.sync_copy(h_hbm.at[pl.ds(base, chunk)], rows_v)
            copies = [pltpu.async_copy(rows_v, out_hbm.at[idx_v.at[c, k]], sem) for k in range(TOP_K)]
            for cp in copies:
                cp.wait()

    return run(h2w, pos_c)


def _sc_gather_back(ys, pos_c, t_all):
    _, n_chunks, _, chunk = pos_c.shape

    @functools.partial(
        pl.kernel, mesh=_sc_mesh(),
        out_type=jax.ShapeDtypeStruct((TOP_K, t_all, D_WORDS), jnp.int32),
        scratch_types=[pltpu.VMEM((n_chunks, TOP_K, chunk), jnp.int32),
                       pltpu.VMEM((TOP_K, chunk, D_WORDS), jnp.int32),
                       pltpu.SemaphoreType.DMA,
                       pltpu.SemaphoreType.DMA],
        name="moe_gather_back")
    def run(ys_hbm, pos_hbm, out_hbm, idx_v, rows_v, sem_in, sem_out):
        wid = _sc_worker_id()
        pltpu.sync_copy(pos_hbm.at[wid], idx_v)

        @pl.loop(0, n_chunks)
        def _(c):
            base = pl.multiple_of((wid * n_chunks + c) * chunk, SUBLANES)
            gathers = [pltpu.async_copy(ys_hbm.at[idx_v.at[c, k]], rows_v.at[k], sem_in)
                       for k in range(TOP_K)]
            for cp in gathers:
                cp.wait()
            writes = [pltpu.async_copy(rows_v.at[k], out_hbm.at[k, pl.ds(base, chunk)], sem_out)
                      for k in range(TOP_K)]
            for cp in writes:
                cp.wait()

    return run(ys, pos_c)


def kernel(x_prompt, x_sample, state_conv, c_prompt, c_sample, w_ada, b_ada, w_in, w_conv, b_conv,
           gn_g, gn_b, sgu_ln_g, sgu_ln_b, w_s, b_s, beta_a, beta_b, w_out, w_router, b_router,
           w1, b1, w2, b2, final_g):
    n_bp, seq_len, _ = x_prompt.shape
    n_bs, n_pos, _ = x_sample.shape
    depth = w_ada.shape[0]
    n_prompt = n_bp * seq_len
    n_sample = n_bs * n_pos
    t_all = n_prompt + n_sample
    n_tiles = (t_all * TOP_K) // GEMM_TILE + N_EXPERTS
    n_workers = SC_CORES * SC_SUBCORES
    assert t_all % (n_workers * DISPATCH_CHUNK) == 0 and t_all % (n_workers * GATHER_CHUNK) == 0
    assert n_tiles <= PLAN_LANES and t_all % ROUTER_TILE == 0

    x_p = x_prompt.reshape(n_prompt, D_MODEL)
    x_s = x_sample.transpose(1, 0, 2).reshape(n_sample, D_MODEL)
    x_s_block = 0
    moe = None
    c_all = jnp.concatenate([c_prompt, c_sample], axis=0)
    state_t = state_conv.transpose(0, 2, 1, 3)
    grp = jnp.arange(C_CONV) // (C_CONV // CONV_GROUPS)
    g_mat = ((grp[:, None] == grp[None, :]).astype(F32) / (C_CONV // CONV_GROUPS)).astype(BF16)

    wconv_p = jnp.broadcast_to(
        jnp.pad(w_conv, ((0, 0), (0, CARRY_ROWS - CONV_WIDTH), (0, 0)))[:, :, None, :],
        (depth, CARRY_ROWS, SUBLANES, C_CONV))
    vec = jnp.stack([b_conv, gn_g, gn_b, sgu_ln_g, sgu_ln_b, beta_a, beta_b, jnp.zeros_like(b_conv)],
                    axis=1)
    bs_full = jnp.repeat(b_s.transpose(0, 2, 1), SGU_HEAD_DIM, axis=2)
    wsv = jnp.repeat(w_s[:, :, :n_pos, :n_pos].transpose(0, 2, 3, 1).reshape(depth, n_pos * n_pos, SGU_HEADS),
                     SGU_HEAD_DIM, axis=2)
    bsv = jnp.repeat(b_s[:, :, :n_pos].transpose(0, 2, 1), SGU_HEAD_DIM, axis=2)
    w_router_t = w_router.transpose(0, 2, 1)
    b_router_c = b_router.reshape(depth, N_EXPERTS, 1)

    conv_p, conv_s, v_s = [], [], []
    gate_p = gate_s = None
    for l in range(depth):
        mods = _adaln(c_all, w_ada, b_ada, l)
        mod_p = mods[:, :n_bp].reshape(1, n_bp, 1, N_MOD * D_MODEL)
        mod_s = mods[:, n_bp:]
        xmid, h2, logits, cst, a_new, v_new = _mixer(
            l, x_p, x_s, x_s_block, moe, mod_p, mod_s, gate_p, gate_s, w_in[l:l + 1].astype(BF16),
            w_out[l:l + 1].astype(BF16), wconv_p, vec,
            g_mat, w_s, bs_full, wsv, bsv, w_router_t, b_router_c, state_t, n_bp, seq_len)
        gate_p = mod_p[..., 5 * D_MODEL:]
        gate_s = mod_s[..., 5 * D_MODEL:]
        conv_p.append(cst)
        conv_s.append(a_new)
        v_s.append(v_new)

        wgt, pos, plan = _router(logits)
        def per_worker(chunk):
            return pos.reshape(TOP_K, n_workers, -1, chunk).transpose(1, 2, 0, 3)
        xs = _sc_dispatch(h2, per_worker(DISPATCH_CHUNK), n_tiles * GEMM_TILE)
        ys = _experts(xs, plan, w1, b1, w2, b2, l)
        yg = _sc_gather_back(ys, per_worker(GATHER_CHUNK), t_all)
        moe = (yg, wgt)
        x_p = x_s = xmid
        x_s_block = n_prompt // n_sample

    y_p, y_s = _final(xmid, *moe, gate_p[0, :, 0], gate_s[0], final_g, n_prompt, seq_len)
    y_prompt = y_p.reshape(n_bp, seq_len, D_MODEL)
    y_sample = y_s.reshape(n_pos, n_bs, D_MODEL).transpose(1, 0, 2)
    new_conv_s = jnp.stack(conv_s, axis=0).transpose(0, 2, 1, 3)
    new_v_s = jnp.stack(v_s, axis=0).reshape(depth, n_pos, n_bs, C_SGU).transpose(0, 2, 1, 3)
    return (y_prompt, y_sample, jnp.stack(conv_p, axis=0), new_conv_s, new_v_s)
```

```python
import functools
import math

import jax
import jax.numpy as jnp
from jax import lax
from jax.experimental import pallas as pl
from jax.experimental.pallas import tpu as pltpu
from jax.experimental.pallas import tpu_sc as plsc

F32 = jnp.float32
BF16 = jnp.bfloat16

D_MODEL = 1024
C_CONV = 512
C_SGU = 512
CONV_GROUPS = 8
CONV_WIDTH = 31
CONV_STATE = CONV_WIDTH - 1
SGU_HEADS = 4
SGU_HEAD_DIM = C_SGU // SGU_HEADS
CHUNK = 128
N_EXPERTS = 32
TOP_K = 4
D_FF = 1024
SWIGLU_LIMIT = 7.0
SWIGLU_ALPHA = 1.702
N_MOD = 6
EPS = 1e-5
LOG2_E = 1.0 / math.log(2.0)

ROW_TILE = 512
ROUTER_TILE = 1536
PREFIX_BLOCK = 256
ADALN_COLS = 1024
ROW_CHUNK = 128
CONV_CHUNK = 32
COMBINE_CHUNK = 64
CARRY_ROWS = 32
SUBLANES = 8
LANES = 128
PLAN_ROWS = 8
PLAN_LANES = 256
PLAN_EXPERT, PLAN_VALID, PLAN_FIRST, PLAN_SLOT, PLAN_NEXT = range(5)
GEMM_TILE = 512
FF_BLOCK = 256
TILES_PER_STEP = 2
VMEM_LIMIT = 56 * 1024 * 1024
D_WORDS = D_MODEL // 2
DISPATCH_CHUNK = 88
GATHER_CHUNK = 48
HI_MASK = -65536
SC_CORES = 2
SC_SUBCORES = 16


def _rms(x):
    return x * lax.rsqrt(jnp.mean(x * x, axis=-1, keepdims=True) + EPS)


def _gelu(x):
    return 0.5 * x * (1.0 + lax.erf(x * (1.0 / math.sqrt(2.0))))


def _sigmoid(x, scale=1.0):
    return 1.0 / (1.0 + jnp.exp2(x * (-scale * LOG2_E)))


def _split_bf16(x):
    hi = x.astype(BF16)
    lo = (x - hi.astype(F32)).astype(BF16)
    return hi, lo


def _pack_bf16_pair(x):
    bits = lax.bitcast_convert_type(x.astype(BF16).astype(F32), jnp.int32)
    return lax.shift_right_logical(bits[:, :D_WORDS], 16) | (bits[:, D_WORDS:] & HI_MASK)


def _unpack_bf16_pair(w):
    left = lax.bitcast_convert_type(lax.shift_left(w, 16), F32)
    right = lax.bitcast_convert_type(w & HI_MASK, F32)
    return left, right


def _dot(a, b):
    return jnp.dot(a, b, preferred_element_type=F32)


def _dot_nt(a, b):
    return lax.dot_general(a, b, (((1,), (1,)), ((), ())), preferred_element_type=F32)


def _conv_tap(wconv_ref, k):
    return jnp.concatenate([wconv_ref[k]] * (CONV_CHUNK // SUBLANES), axis=0)


def _static_loop(n_rows, body, chunk=None):
    chunk = ROW_CHUNK if chunk is None else chunk
    for r0 in range(0, n_rows, chunk):
        body(r0)


def _adaln_kernel(c_ref, w_ref, b_ref, o_ref):
    c = c_ref[...]
    s_hi, s_lo = _split_bf16(c * _sigmoid(c))
    w_hi, w_lo = _split_bf16(w_ref[0])
    acc = _dot(s_hi, w_hi) + _dot(s_hi, w_lo) + _dot(s_lo, w_hi)
    o_ref[0] = acc + b_ref[0]


def _adaln(c_all, w_ada, b_ada, layer):
    depth, _, n_out = w_ada.shape
    n_rows = c_all.shape[0]
    tn = ADALN_COLS
    return pl.pallas_call(
        _adaln_kernel,
        grid=(n_out // tn,),
        in_specs=[
            pl.BlockSpec((n_rows, D_MODEL), lambda j: (0, 0)),
            pl.BlockSpec((1, D_MODEL, tn), lambda j: (layer, 0, j)),
            pl.BlockSpec((1, 1, tn), lambda j: (layer, 0, j)),
        ],
        out_specs=pl.BlockSpec((1, n_rows, tn), lambda j: (0, 0, j)),
        out_shape=jax.ShapeDtypeStruct((1, n_rows, n_out), F32),
        compiler_params=pltpu.CompilerParams(
            dimension_semantics=("arbitrary",), vmem_limit_bytes=VMEM_LIMIT),
        name="adaln",
    )(c_all, w_ada, b_ada.reshape(depth, 1, n_out))


def _phase_in_norm(x_ref, h_scr, mod, n_rows):
    def body(r0):
        r = pl.ds(r0, ROW_CHUNK)
        h = _rms(x_ref[r, :]) * (1.0 + mod(1, r0)) + mod(0, r0)
        h_scr[r, :] = h.astype(BF16)
    _static_loop(n_rows, body)


def _phase_glu(z_scr, a_dst, a_off, n_rows):
    def body(r0):
        r = pl.ds(r0, ROW_CHUNK)
        a_dst[pl.ds(r0 + a_off, ROW_CHUNK), :] = z_scr[r, 0:C_CONV] * _sigmoid(z_scr[r, C_CONV:2 * C_CONV])
    _static_loop(n_rows, body)


def _phase_uv(z_scr, vec_ref, u_scr, v_scr, vout_ref, n_rows):
    ln_g = vec_ref[3:4, :]
    ln_b = vec_ref[4:5, :]

    def body(r0):
        r = pl.ds(r0, ROW_CHUNK)
        u_scr[r, :] = _gelu(z_scr[r, 2 * C_CONV:2 * C_CONV + C_SGU])
        gv = _gelu(z_scr[r, 2 * C_CONV + C_SGU:2 * C_CONV + 2 * C_SGU])
        mu = jnp.mean(gv, axis=-1, keepdims=True)
        dv = gv - mu
        var = jnp.mean(dv * dv, axis=-1, keepdims=True)
        v = dv * lax.rsqrt(var + EPS) * ln_g + ln_b
        if vout_ref is not None:
            vout_ref[r, :] = v
        v_scr[r, :] = v.astype(v_scr.dtype)
    _static_loop(n_rows, body)


def _phase_group_norm(conv_scr, convb_scr, stat_scr, g_ref, vec_ref, y_scr, n_rows):
    stat_scr[...] = _dot(convb_scr[...], g_ref[...])

    def center(r0):
        r = pl.ds(r0, ROW_CHUNK)
        d = conv_scr[r, :] - stat_scr[r, :]
        conv_scr[r, :] = d
        convb_scr[r, :] = (d * d).astype(BF16)
    _static_loop(n_rows, center)
    stat_scr[...] = _dot(convb_scr[...], g_ref[...])
    gn_g = vec_ref[1:2, :]
    gn_b = vec_ref[2:3, :]
    beta_a = vec_ref[5:6, :]

    def finish(r0):
        r = pl.ds(r0, ROW_CHUNK)
        gn = conv_scr[r, :] * lax.rsqrt(stat_scr[r, :] + EPS) * gn_g + gn_b
        ya = gn * _sigmoid(gn)
        y_scr[r, 0:C_CONV] = (_rms(ya) * beta_a).astype(BF16)
    _static_loop(n_rows, finish)


def _phase_out(x_ref, y_scr, wout_ref, z_scr, wr_ref, br_ref, mod, h_scr,
               xmid_ref, h2_ref, logit_ref, n_rows):
    z_scr[:, 0:D_MODEL] = (_dot(y_scr[:, 0:C_CONV], wout_ref[0:C_CONV, :])
                           + _dot(y_scr[:, C_CONV:C_CONV + C_SGU], wout_ref[C_CONV:C_CONV + C_SGU, :]))

    def body(r0):
        r = pl.ds(r0, ROW_CHUNK)
        xm = x_ref[r, :] + mod(2, r0) * z_scr[r, 0:D_MODEL]
        xmid_ref[r, :] = xm
        h2 = _rms(xm) * (1.0 + mod(4, r0)) + mod(3, r0)
        hi, lo = _split_bf16(h2)
        h2_ref[r, :] = _pack_bf16_pair(h2)
        y_scr[r, :] = hi
        h_scr[r, :] = lo
    _static_loop(n_rows, body)
    w_hi, w_lo = _split_bf16(wr_ref[...])
    both = _dot_nt(jnp.concatenate([w_hi, w_lo], axis=0), y_scr[...])
    logit_ref[...] = (both[0:N_EXPERTS] + both[N_EXPERTS:2 * N_EXPERTS]
                      + _dot_nt(w_hi, h_scr[...])) + br_ref[...]


def _moe_half(yg_ref, w, r, hi_half):
    moe = jnp.zeros((COMBINE_CHUNK, D_WORDS), F32)
    for k in range(TOP_K):
        word = yg_ref[k, r, :]
        bits = (word & HI_MASK) if hi_half else lax.shift_left(word, 16)
        moe = moe + w[:, k:k + 1] * lax.bitcast_convert_type(bits, F32)
    return moe


def _phase_moe_residual(xmid_ref, yg_ref, wgt_ref, gate, x_dst, n_rows):
    def body(r0):
        r = pl.ds(r0, COMBINE_CHUNK)
        w = wgt_ref[r, :]
        g = gate(r0)
        for hi_half in (False, True):
            cols = slice(D_WORDS, D_MODEL) if hi_half else slice(0, D_WORDS)
            x_dst[r, cols] = xmid_ref[r, cols] + g[:, cols] * _moe_half(yg_ref, w, r, hi_half)
    _static_loop(n_rows, body, COMBINE_CHUNK)


def _mixer_prompt_kernel(*refs, tiles_per_seq, n_prompt_tiles, fused):
    x_ref, mod_ref = refs[:2]
    refs = refs[2:]
    if fused:
        yg_ref, wgt_ref, gate_ref = refs[:3]
        refs = refs[3:]
        x_scr = refs[-1]
        refs = refs[:-1]
    (win_ref, wout_ref, wconv_ref, vec_ref, g_ref, ws_ref, bs_ref, wr_ref, br_ref,
     smid_ref, sh2_ref, slogit_ref, xmid_ref, h2_ref, logit_ref, cst_ref,
     h_scr, z_scr, aext_scr, conv_scr, convb_scr, stat_scr, u_scr, v_scr, y_scr) = refs
    i = pl.program_id(0)

    @pl.when(i == n_prompt_tiles)
    def _():
        xmid_ref[...] = smid_ref[...]
        h2_ref[...] = sh2_ref[...]
        logit_ref[...] = slogit_ref[...]

    @pl.when(i < n_prompt_tiles)
    def _():
        x_src = x_ref
        if fused:
            _phase_moe_residual(x_ref, yg_ref, wgt_ref, lambda r0: gate_ref[0], x_scr, ROW_TILE)
            x_src = x_scr
        _mixer_prompt_tile(lax.rem(i, tiles_per_seq), tiles_per_seq,
                           x_src, mod_ref, win_ref, wout_ref, wconv_ref, vec_ref, g_ref,
                           ws_ref, bs_ref, wr_ref, br_ref, xmid_ref, h2_ref, logit_ref, cst_ref,
                           h_scr, z_scr, aext_scr, conv_scr, convb_scr, stat_scr, u_scr, v_scr, y_scr)


def _mixer_prompt_tile(j, tiles_per_seq, x_ref, mod_ref, win_ref, wout_ref, wconv_ref, vec_ref, g_ref,
                       ws_ref, bs_ref, wr_ref, br_ref, xmid_ref, h2_ref, logit_ref, cst_ref,
                       h_scr, z_scr, aext_scr, conv_scr, convb_scr, stat_scr, u_scr, v_scr, y_scr):
    n_rows = ROW_TILE

    def mod(idx, r0):
        del r0
        return mod_ref[0, :, idx * D_MODEL:(idx + 1) * D_MODEL]

    a_buf = aext_scr.at[0]

    @pl.when(j == 0)
    def _():
        a_buf[0:CARRY_ROWS, :] = jnp.zeros((CARRY_ROWS, C_CONV), F32)

    _phase_in_norm(x_ref, h_scr, mod, n_rows)
    n_glu = 2 * C_CONV
    z_scr[:, 0:n_glu] = _dot(h_scr[...], win_ref[:, 0:n_glu])
    _phase_glu(z_scr, a_buf, CARRY_ROWS, n_rows)
    z_scr[:, n_glu:] = _dot(h_scr[...], win_ref[:, n_glu:])
    n_shift = n_rows + CARRY_ROWS - SUBLANES
    for s in range(1, SUBLANES):
        aext_scr[s, 0:n_shift, :] = a_buf[s:s + n_shift, :]

    b_conv = vec_ref[0:1, :]
    lead = CARRY_ROWS - CONV_STATE
    for r0 in range(0, n_rows, CONV_CHUNK):
        acc = jnp.zeros((CONV_CHUNK, C_CONV), F32)
        for k in range(CONV_WIDTH):
            s = (lead + k) % SUBLANES
            q0 = r0 + (lead + k - s)
            acc = acc + _conv_tap(wconv_ref, k) * aext_scr[s, q0:q0 + CONV_CHUNK, :]
        conv = acc + b_conv
        conv_scr[r0:r0 + CONV_CHUNK, :] = conv
        convb_scr[r0:r0 + CONV_CHUNK, :] = conv.astype(BF16)
    _phase_uv(z_scr, vec_ref, u_scr, v_scr, None, n_rows)
    _phase_group_norm(conv_scr, convb_scr, stat_scr, g_ref, vec_ref, y_scr, n_rows)

    row_i = lax.broadcasted_iota(jnp.int32, (CHUNK, CHUNK), 0)
    col_i = lax.broadcasted_iota(jnp.int32, (CHUNK, CHUNK), 1)
    tril = (col_i <= row_i).astype(F32)
    for hd in range(SGU_HEADS):
        ws_h = (ws_ref[hd] * tril).astype(BF16)
        cols = slice(hd * SGU_HEAD_DIM, (hd + 1) * SGU_HEAD_DIM)
        s_cols = slice(n_glu + hd * SGU_HEAD_DIM, n_glu + (hd + 1) * SGU_HEAD_DIM)
        for c in range(n_rows // CHUNK):
            rows = slice(c * CHUNK, (c + 1) * CHUNK)
            z_scr[rows, s_cols] = _dot(ws_h, v_scr[rows, cols])
    beta_b = vec_ref[6:7, :]

    def gate_body(r0):
        r = pl.ds(r0, ROW_CHUNK)
        rb = pl.ds(r0 % CHUNK, ROW_CHUNK)
        yb = u_scr[r, :] * (z_scr[r, n_glu:n_glu + C_SGU] + bs_ref[rb, :])
        y_scr[r, C_CONV:C_CONV + C_SGU] = (_rms(yb) * beta_b).astype(BF16)
    _static_loop(n_rows, gate_body)

    _phase_out(x_ref, y_scr, wout_ref, z_scr, wr_ref, br_ref, mod, h_scr,
               xmid_ref, h2_ref, logit_ref, n_rows)

    @pl.when(j == tiles_per_seq - 1)
    def _():
        cst_ref[0] = a_buf[n_rows + lead:n_rows + CARRY_ROWS, :]

    a_buf[0:CARRY_ROWS, :] = a_buf[n_rows:n_rows + CARRY_ROWS, :]


def _mixer_sample_kernel(*refs, fused):
    x_ref, mod_ref = refs[:2]
    refs = refs[2:]
    if fused:
        yg_ref, wgt_ref, gate_ref = refs[:3]
        refs = refs[3:]
        x_scr = refs[-1]
        refs = refs[:-1]
    (win_ref, wout_ref, wconv_ref, vec_ref, g_ref, wsv_ref, bsv_ref, wr_ref, br_ref, state_ref,
     xmid_ref, h2_ref, logit_ref, cnew_ref, vout_ref,
     h_scr, z_scr, conv_scr, convb_scr, stat_scr, u_scr, v_scr, y_scr, anew_ref) = refs
    n_rows = x_ref.shape[0]
    n_seq = state_ref.shape[1]
    n_pos = n_rows // n_seq

    def mod(idx, r0):
        return mod_ref[pl.ds(r0 % n_seq, ROW_CHUNK), idx * D_MODEL:(idx + 1) * D_MODEL]

    if fused:
        def gate(r0):
            return gate_ref[pl.ds(r0 % n_seq, COMBINE_CHUNK), :]
        _phase_moe_residual(x_ref, yg_ref, wgt_ref, gate, x_scr, n_rows)
        x_ref = x_scr

    _phase_in_norm(x_ref, h_scr, mod, n_rows)
    z_scr[...] = _dot(h_scr[...], win_ref[...])
    _phase_glu(z_scr, anew_ref, 0, n_rows)
    _phase_uv(z_scr, vec_ref, u_scr, v_scr, vout_ref, n_rows)

    keep = CONV_STATE - n_pos
    cnew_ref[0:keep] = state_ref[n_pos:CONV_STATE]
    for t in range(n_pos):
        cnew_ref[keep + t] = anew_ref[t * n_seq:(t + 1) * n_seq, :]

    b_conv = vec_ref[0:1, :]

    def conv_body(s0):
        rs = pl.ds(s0, CONV_CHUNK)
        for t in range(n_pos):
            acc = jnp.zeros((CONV_CHUNK, C_CONV), F32)
            for k in range(CONV_WIDTH):
                p = t + k
                if p < CONV_STATE:
                    src = state_ref[p, rs, :]
                else:
                    src = anew_ref[pl.ds(s0 + (p - CONV_STATE) * n_seq, CONV_CHUNK), :]
                acc = acc + _conv_tap(wconv_ref, k) * src
            conv = acc + b_conv
            r = pl.ds(s0 + t * n_seq, CONV_CHUNK)
            conv_scr[r, :] = conv
            convb_scr[r, :] = conv.astype(BF16)
    _static_loop(n_seq, conv_body, CONV_CHUNK)

    _phase_group_norm(conv_scr, convb_scr, stat_scr, g_ref, vec_ref, y_scr, n_rows)

    beta_b = vec_ref[6:7, :]

    def gate_body(s0):
        for t in range(n_pos):
            s = jnp.zeros((CONV_CHUNK, C_SGU), F32)
            for jj in range(t + 1):
                s = s + wsv_ref[t * n_pos + jj:t * n_pos + jj + 1, :] * v_scr[pl.ds(s0 + jj * n_seq, CONV_CHUNK), :]
            r = pl.ds(s0 + t * n_seq, CONV_CHUNK)
            yb = u_scr[r, :] * (s + bsv_ref[t:t + 1, :])
            y_scr[r, C_CONV:C_CONV + C_SGU] = (_rms(yb) * beta_b).astype(BF16)
    _static_loop(n_seq, gate_body, CONV_CHUNK)

    _phase_out(x_ref, y_scr, wout_ref, z_scr, wr_ref, br_ref, mod, h_scr,
               xmid_ref, h2_ref, logit_ref, n_rows)


def _const_spec(shape):
    nd = len(shape)
    return pl.BlockSpec(shape, lambda *_: (0,) * nd, pipeline_mode=pl.Buffered(1))


def _const_out_spec(shape):
    nd = len(shape)
    return pl.BlockSpec(shape, lambda *_: (0,) * nd)


def _layer_spec(shape, layer):
    nd = len(shape)
    return pl.BlockSpec((None,) + tuple(shape), lambda *_: (layer,) + (0,) * nd,
                        pipeline_mode=pl.Buffered(1))


def _mixer(layer, x_p, x_s, x_s_block, moe, mod_p, mod_s, gate_p, gate_s, w_in_b, w_out_b, wconv_p, vec,
           g_mat, w_s, bs_full, wsv, bsv, w_router, b_router, state_t, n_prompt_seq, seq_len):
    fused = moe is not None
    n_prompt = n_prompt_seq * seq_len
    n_seq, n_pos = state_t.shape[2], bsv.shape[1]
    n_sample = n_seq * n_pos
    t_all = n_prompt + n_sample
    nj = seq_len // ROW_TILE
    weight_specs = [
        _layer_spec((D_MODEL, 2 * C_CONV + 2 * C_SGU), 0),
        _layer_spec((D_MODEL, D_MODEL), 0),
        _layer_spec((CARRY_ROWS, SUBLANES, C_CONV), layer),
        _layer_spec((SUBLANES, C_CONV), layer),
        _const_spec((C_CONV, C_CONV)),
    ]
    router_specs = [_layer_spec((N_EXPERTS, D_MODEL), layer), _layer_spec((N_EXPERTS, 1), layer)]
    common_scratch = lambda n: [
        pltpu.VMEM((n, D_MODEL), BF16),
        pltpu.VMEM((n, 2 * C_CONV + 2 * C_SGU), F32),
    ]
    tail_scratch = lambda n, vdt: [
        pltpu.VMEM((n, C_CONV), F32),
        pltpu.VMEM((n, C_CONV), BF16),
        pltpu.VMEM((n, C_CONV), F32),
        pltpu.VMEM((n, C_SGU), F32),
        pltpu.VMEM((n, C_SGU), vdt),
        pltpu.VMEM((n, D_MODEL), BF16),
    ]
    assert n_sample == ROW_TILE
    n_tiles_p = n_prompt // ROW_TILE
    tile_p = lambda i: jnp.minimum(i, n_tiles_p - 1)
    seq_of = lambda i: jnp.minimum(i // nj, n_prompt_seq - 1)
    moe_s_specs, moe_p_specs, moe_s_args, moe_p_args, x_scratch = [], [], [], [], []
    if fused:
        yg, wgt = moe
        moe_s_specs = [pl.BlockSpec((TOP_K, n_sample, D_WORDS), lambda i: (0, x_s_block, 0)),
                       pl.BlockSpec((n_sample, LANES), lambda i: (x_s_block, 0)),
                       _layer_spec((n_seq, D_MODEL), 0)]
        moe_p_specs = [pl.BlockSpec((TOP_K, ROW_TILE, D_WORDS), lambda i: (0, tile_p(i), 0)),
                       pl.BlockSpec((ROW_TILE, LANES), lambda i: (tile_p(i), 0)),
                       pl.BlockSpec((None, 1, 1, D_MODEL), lambda i: (0, seq_of(tile_p(i)), 0, 0))]
        moe_s_args = [yg, wgt, gate_s]
        moe_p_args = [yg, wgt, gate_p]
        x_scratch = [pltpu.VMEM((ROW_TILE, D_MODEL), F32)]
    smid, sh2, slogits, a_new, v_new = pl.pallas_call(
        functools.partial(_mixer_sample_kernel, fused=fused),
        grid=(1,),
        in_specs=[
            pl.BlockSpec((n_sample, D_MODEL), lambda i: (x_s_block, 0)),
            _layer_spec((n_seq, N_MOD * D_MODEL), 0),
            *moe_s_specs,
            *weight_specs,
            _layer_spec((n_pos * n_pos, C_SGU), layer),
            _layer_spec((n_pos, C_SGU), layer),
            *router_specs,
            _layer_spec((CONV_STATE, n_seq, C_CONV), layer),
        ],
        out_specs=(
            _const_out_spec((n_sample, D_MODEL)),
            _const_out_spec((n_sample, D_WORDS)),
            _const_out_spec((N_EXPERTS, n_sample)),
            _const_out_spec((CONV_STATE, n_seq, C_CONV)),
            _const_out_spec((n_sample, C_SGU)),
        ),
        out_shape=(
            jax.ShapeDtypeStruct((n_sample, D_MODEL), F32),
            jax.ShapeDtypeStruct((n_sample, D_WORDS), jnp.int32),
            jax.ShapeDtypeStruct((N_EXPERTS, n_sample), F32),
            jax.ShapeDtypeStruct((CONV_STATE, n_seq, C_CONV), F32),
            jax.ShapeDtypeStruct((n_sample, C_SGU), F32),
        ),
        scratch_shapes=common_scratch(n_sample) + tail_scratch(n_sample, F32)
        + [pltpu.VMEM((n_sample, C_CONV), F32)] + x_scratch,
        compiler_params=pltpu.CompilerParams(
            dimension_semantics=("arbitrary",), vmem_limit_bytes=VMEM_LIMIT),
        name="mixer_sample",
    )(x_s, mod_s, *moe_s_args, w_in_b, w_out_b, wconv_p, vec, g_mat, wsv, bsv, w_router, b_router,
      state_t)

    xmid, h2, logits, cst = pl.pallas_call(
        functools.partial(_mixer_prompt_kernel, tiles_per_seq=nj, n_prompt_tiles=n_tiles_p,
                          fused=fused),
        grid=(n_tiles_p + 1,),
        in_specs=[
            pl.BlockSpec((ROW_TILE, D_MODEL), lambda i: (tile_p(i), 0)),
            pl.BlockSpec((None, 1, 1, N_MOD * D_MODEL), lambda i: (0, seq_of(i), 0, 0)),
            *moe_p_specs,
            *weight_specs,
            _layer_spec((SGU_HEADS, CHUNK, CHUNK), layer),
            _layer_spec((CHUNK, C_SGU), layer),
            *router_specs,
            _const_spec((n_sample, D_MODEL)),
            _const_spec((n_sample, D_WORDS)),
            _const_spec((N_EXPERTS, n_sample)),
        ],
        out_specs=(
            pl.BlockSpec((ROW_TILE, D_MODEL), lambda i: (i, 0)),
            pl.BlockSpec((ROW_TILE, D_WORDS), lambda i: (i, 0)),
            pl.BlockSpec((N_EXPERTS, ROW_TILE), lambda i: (0, i)),
            pl.BlockSpec((1, CONV_STATE, C_CONV), lambda i: (seq_of(i), 0, 0)),
        ),
        out_shape=(
            jax.ShapeDtypeStruct((t_all, D_MODEL), F32),
            jax.ShapeDtypeStruct((t_all, D_WORDS), jnp.int32),
            jax.ShapeDtypeStruct((N_EXPERTS, t_all), F32),
            jax.ShapeDtypeStruct((n_prompt_seq, CONV_STATE, C_CONV), F32),
        ),
        scratch_shapes=common_scratch(ROW_TILE)
        + [pltpu.VMEM((SUBLANES, CARRY_ROWS + ROW_TILE, C_CONV), F32)]
        + tail_scratch(ROW_TILE, BF16) + x_scratch,
        compiler_params=pltpu.CompilerParams(
            dimension_semantics=("arbitrary",), vmem_limit_bytes=VMEM_LIMIT),
        name="mixer_prompt",
    )(x_p, mod_p, *moe_p_args, w_in_b, w_out_b, wconv_p, vec, g_mat, w_s, bs_full, w_router, b_router,
      smid, sh2, slogits)
    return xmid, h2, logits, cst, a_new, v_new


def _tile_plan(cnt):
    e_sub = lax.broadcasted_iota(jnp.int32, (N_EXPERTS, N_EXPERTS), 0)
    e_lane = lax.broadcasted_iota(jnp.int32, (N_EXPERTS, N_EXPERTS), 1)
    tiles = jnp.floor((cnt + (GEMM_TILE - 1.0)) * (1.0 / GEMM_TILE))
    active = jnp.where(cnt > 0.0, 1.0, 0.0)

    def over_experts(tri, col):
        return _dot(tri.astype(BF16), jnp.broadcast_to(col, (N_EXPERTS, LANES)).astype(BF16))[:, 0:1]
    tile_end = over_experts(jnp.where(e_lane <= e_sub, 1.0, 0.0), tiles)
    tile_start = tile_end - tiles
    ordinal = over_experts(jnp.where(e_lane < e_sub, 1.0, 0.0), active)

    g = lax.broadcasted_iota(jnp.int32, (1, PLAN_LANES), 1).astype(F32)
    e_id = lax.broadcasted_iota(jnp.int32, (N_EXPERTS, PLAN_LANES), 0).astype(F32)
    done = jnp.sum(jnp.where(tile_end <= g, 1.0, 0.0), axis=0, keepdims=True)
    live = done < float(N_EXPERTS)
    last_e = jnp.max(jnp.where(cnt > 0.0, e_id[:, 0:1], 0.0), axis=0, keepdims=True)
    te = jnp.where(live, done, last_e)
    pick = e_id == te

    def at_tile(col):
        return jnp.sum(jnp.where(pick, col, 0.0), axis=0, keepdims=True)
    start_g = at_tile(tile_start)
    ord_g = at_tile(ordinal)
    valid = jnp.clip(at_tile(cnt) - (g - start_g) * GEMM_TILE, 0.0, float(GEMM_TILE))
    valid = jnp.where(live, valid, 0.0)
    first = jnp.where(live, jnp.where(g == start_g, 1.0, 0.0), 0.0)
    slot = ord_g - 2.0 * jnp.floor(ord_g * 0.5)
    later = jnp.where(e_id > te, jnp.where(cnt > 0.0, e_id, float(N_EXPERTS)), float(N_EXPERTS))
    nxt = jnp.min(later, axis=0, keepdims=True)
    nxt = jnp.where(nxt >= float(N_EXPERTS), -1.0, nxt)

    row = lax.broadcasted_iota(jnp.int32, (PLAN_ROWS, PLAN_LANES), 0)
    plan = jnp.zeros((PLAN_ROWS, PLAN_LANES), F32)
    for r, v in ((PLAN_EXPERT, te), (PLAN_VALID, valid), (PLAN_FIRST, first), (PLAN_SLOT, slot),
                 (PLAN_NEXT, nxt)):
        plan = jnp.where(row == r, v, plan)
    return plan.astype(jnp.int32), tile_start * GEMM_TILE


def _router_kernel(lg_ref, wgt_ref, pos_ref, plan_ref, cnt_scr, run_scr, start_scr, earlier_scr):
    ph = pl.program_id(0)
    i = pl.program_id(1)
    n = lg_ref.shape[1]

    @pl.when(jnp.logical_and(ph == 0, i == 0))
    def _():
        cnt_scr[...] = jnp.zeros_like(cnt_scr)

    lg = lg_ref[...]
    sub = lax.broadcasted_iota(jnp.int32, lg.shape, 0).astype(F32)
    vals, sels = [], []
    for _ in range(TOP_K):
        m = jnp.max(lg, axis=0, keepdims=True)
        idx = jnp.min(jnp.where(lg == m, sub, float(N_EXPERTS)), axis=0, keepdims=True)
        sel = sub == idx
        vals.append(m)
        sels.append(sel)
        lg = jnp.where(sel, -jnp.inf, lg)
    onehot = jnp.zeros(lg.shape, F32)
    for sel in sels:
        onehot = onehot + jnp.where(sel, 1.0, 0.0)
    tile_cnt = jnp.sum(onehot, axis=1, keepdims=True)

    @pl.when(ph == 0)
    def _():
        cnt_scr[...] = cnt_scr[...] + tile_cnt

    @pl.when(jnp.logical_and(ph == 1, i == 0))
    def _():
        plan, row_start = _tile_plan(cnt_scr[...])
        plan_ref[...] = plan
        start_scr[...] = row_start
        run_scr[...] = jnp.zeros_like(run_scr)
        t_row = lax.broadcasted_iota(jnp.int32, (PREFIX_BLOCK, PREFIX_BLOCK), 0)
        t_col = lax.broadcasted_iota(jnp.int32, (PREFIX_BLOCK, PREFIX_BLOCK), 1)
        earlier_scr[...] = jnp.where(t_row < t_col, 1.0, 0.0).astype(BF16)

    @pl.when(ph == 1)
    def _():
        exps = [jnp.exp(v - vals[0]) for v in vals]
        inv = 1.0 / (exps[0] + exps[1] + exps[2] + exps[3])
        w_row = lax.broadcasted_iota(jnp.int32, (LANES, n), 0)
        w_all = jnp.zeros((LANES, n), F32)
        for k in range(TOP_K):
            w_all = jnp.where(w_row == k, exps[k] * inv, w_all)
        wgt_ref[...] = w_all.T
        base = run_scr[...] + start_scr[...]
        onehot_b = onehot.astype(BF16)
        for b0 in range(0, n, PREFIX_BLOCK):
            blk = slice(b0, b0 + PREFIX_BLOCK)
            slot0 = _dot(onehot_b[:, blk], earlier_scr[...]) + base
            for k in range(TOP_K):
                pos_k = jnp.sum(jnp.where(sels[k][:, blk], slot0, 0.0), axis=0, keepdims=True)
                pos_ref[k:k + 1, blk] = pos_k.astype(jnp.int32)
            base = base + jnp.sum(onehot[:, blk], axis=1, keepdims=True)
        run_scr[...] = run_scr[...] + tile_cnt


def _router(logits):
    t_all = logits.shape[1]
    col = pltpu.VMEM((N_EXPERTS, 1), F32)
    return pl.pallas_call(
        _router_kernel,
        grid=(2, t_all // ROUTER_TILE),
        in_specs=[pl.BlockSpec((N_EXPERTS, ROUTER_TILE), lambda ph, i: (0, i))],
        out_specs=(
            pl.BlockSpec((ROUTER_TILE, LANES), lambda ph, i: (i * ph, 0)),
            pl.BlockSpec((TOP_K, ROUTER_TILE), lambda ph, i: (0, i * ph)),
            _const_out_spec((PLAN_ROWS, PLAN_LANES)),
        ),
        out_shape=(
            jax.ShapeDtypeStruct((t_all, LANES), F32),
            jax.ShapeDtypeStruct((TOP_K, t_all), jnp.int32),
            jax.ShapeDtypeStruct((PLAN_ROWS, PLAN_LANES), jnp.int32),
        ),
        scratch_shapes=[col, col, col, pltpu.VMEM((PREFIX_BLOCK, PREFIX_BLOCK), BF16)],
        compiler_params=pltpu.CompilerParams(
            dimension_semantics=("arbitrary", "arbitrary"), vmem_limit_bytes=VMEM_LIMIT),
        name="router",
    )(logits)


def _expert_kernel(plan_ref, x_ref, w1_hbm, b1_ref, w2_hbm, b2_ref, o_ref,
                   w1f_scr, w2f_scr, w1b_scr, w2b_scr, sem, *, layer):
    def weight_copies(expert, slot):
        return (pltpu.make_async_copy(w1_hbm.at[layer, expert], w1f_scr.at[slot], sem.at[0, slot]),
                pltpu.make_async_copy(w2_hbm.at[layer, expert], w2f_scr.at[slot], sem.at[1, slot]))

    for sub in range(TILES_PER_STEP):
        _expert_tile(pl.program_id(0) * TILES_PER_STEP + sub, sub * GEMM_TILE, plan_ref, x_ref,
                     b1_ref, b2_ref, o_ref, w1f_scr, w2f_scr, w1b_scr, w2b_scr, weight_copies)


def _expert_tile(g, row0, plan_ref, x_ref, b1_ref, b2_ref, o_ref, w1f_scr, w2f_scr, w1b_scr, w2b_scr,
                 weight_copies):
    nv = plan_ref[PLAN_VALID, g]
    expert = plan_ref[PLAN_EXPERT, g]
    slot = plan_ref[PLAN_SLOT, g]
    first = plan_ref[PLAN_FIRST, g] == 1

    @pl.when(first)
    def _():
        nxt = plan_ref[PLAN_NEXT, g]

        @pl.when(g == 0)
        def _():
            for cp in weight_copies(expert, slot):
                cp.start()
        for cp in weight_copies(expert, slot):
            cp.wait()

        @pl.when(nxt >= 0)
        def _():
            for cp in weight_copies(nxt, 1 - slot):
                cp.start()

    def ffn(m, fresh):
        x_left, x_right = _unpack_bf16_pair(x_ref[row0:row0 + m, :])
        x = jnp.concatenate([x_left.astype(BF16), x_right.astype(BF16)], axis=1)

        def hidden(col0):
            cols = slice(col0, col0 + FF_BLOCK)
            if fresh:
                w1b_scr[:, cols] = w1f_scr[slot, :, cols].astype(BF16)
            return _dot(x, w1b_scr[:, cols]) + b1_ref[expert, :, cols]

        acts = []
        for jb in range(D_FF // FF_BLOCK):
            gg = jnp.minimum(hidden(jb * FF_BLOCK), SWIGLU_LIMIT)
            up = jnp.clip(hidden(D_FF + jb * FF_BLOCK), -SWIGLU_LIMIT, SWIGLU_LIMIT)
            acts.append((gg * _sigmoid(gg, SWIGLU_ALPHA) * (up + 1.0)).astype(BF16))
        if fresh:
            w2b_scr[...] = w2f_scr[slot].astype(BF16)
        y = _dot(jnp.concatenate(acts, axis=1), w2b_scr[...]) + b2_ref[expert]
        row = lax.broadcasted_iota(jnp.int32, y.shape, 0)
        o_ref[row0:row0 + m, :] = _pack_bf16_pair(jnp.where(row < nv, y, 0.0))
        if m < GEMM_TILE:
            o_ref[row0 + m:row0 + GEMM_TILE, :] = jnp.zeros((GEMM_TILE - m, D_WORDS), jnp.int32)

    full = nv > GEMM_TILE // 2
    half = jnp.logical_and(nv > 0, nv <= GEMM_TILE // 2)
    for rows, cond in ((GEMM_TILE, full), (GEMM_TILE // 2, half)):
        for fresh in (True, False):
            pl.when(jnp.logical_and(cond, first == fresh))(functools.partial(ffn, rows, fresh))

    @pl.when(nv == 0)
    def _():
        o_ref[row0:row0 + GEMM_TILE, :] = jnp.zeros((GEMM_TILE, D_WORDS), jnp.int32)


def _experts(xs, plan, w1, b1, w2, b2, layer):
    n_tiles = xs.shape[0] // GEMM_TILE
    depth = w1.shape[0]
    assert n_tiles % TILES_PER_STEP == 0
    step_rows = TILES_PER_STEP * GEMM_TILE
    bias_map = lambda s, plan: (layer, 0, 0, 0)
    grid_spec = pltpu.PrefetchScalarGridSpec(
        num_scalar_prefetch=1,
        grid=(n_tiles // TILES_PER_STEP,),
        in_specs=[
            pl.BlockSpec((step_rows, D_WORDS), lambda s, plan: (s, 0)),
            pl.BlockSpec(memory_space=pl.ANY),
            pl.BlockSpec((None, N_EXPERTS, 1, 2 * D_FF), bias_map, pipeline_mode=pl.Buffered(1)),
            pl.BlockSpec(memory_space=pl.ANY),
            pl.BlockSpec((None, N_EXPERTS, 1, D_MODEL), bias_map, pipeline_mode=pl.Buffered(1)),
        ],
        out_specs=pl.BlockSpec((step_rows, D_WORDS), lambda s, plan: (s, 0)),
        scratch_shapes=[
            pltpu.VMEM((2, D_MODEL, 2 * D_FF), F32),
            pltpu.VMEM((2, D_FF, D_MODEL), F32),
            pltpu.VMEM((D_MODEL, 2 * D_FF), BF16),
            pltpu.VMEM((D_FF, D_MODEL), BF16),
            pltpu.SemaphoreType.DMA((2, 2)),
        ],
    )
    return pl.pallas_call(
        functools.partial(_expert_kernel, layer=layer),
        grid_spec=grid_spec,
        out_shape=jax.ShapeDtypeStruct((n_tiles * GEMM_TILE, D_WORDS), jnp.int32),
        compiler_params=pltpu.CompilerParams(
            dimension_semantics=("arbitrary",), vmem_limit_bytes=VMEM_LIMIT),
        name="experts",
    )(plan, xs, w1, b1.reshape(depth, N_EXPERTS, 1, 2 * D_FF), w2,
      b2.reshape(depth, N_EXPERTS, 1, D_MODEL))


class _RefStack:
    def __init__(self, refs):
        self.refs = refs

    def __getitem__(self, idx):
        k, r, c = idx
        return self.refs[k][r, c]


def _final_kernel(x_ref, *refs, n_prompt_blocks):
    yg_ref = _RefStack(refs[:TOP_K])
    wgt_ref, gp_ref, gs_ref, fg_ref, op_ref, os_ref = refs[TOP_K:]
    i = pl.program_id(0)
    n_seq = gs_ref.shape[0]

    def run(gate, o_ref):
        def body(r0):
            r = pl.ds(r0, COMBINE_CHUNK)
            w = wgt_ref[r, :]
            g = gate(r0)
            xl = x_ref[r, 0:D_WORDS] + g[:, 0:D_WORDS] * _moe_half(yg_ref, w, r, False)
            xr = x_ref[r, D_WORDS:D_MODEL] + g[:, D_WORDS:D_MODEL] * _moe_half(yg_ref, w, r, True)
            ms = (jnp.sum(xl * xl, axis=-1, keepdims=True)
                  + jnp.sum(xr * xr, axis=-1, keepdims=True)) * (1.0 / D_MODEL)
            scale = lax.rsqrt(ms + EPS)
            o_ref[r, 0:D_WORDS] = xl * scale * fg_ref[:, 0:D_WORDS]
            o_ref[r, D_WORDS:D_MODEL] = xr * scale * fg_ref[:, D_WORDS:D_MODEL]
        _static_loop(x_ref.shape[0], body, COMBINE_CHUNK)

    @pl.when(i < n_prompt_blocks)
    def _():
        run(lambda r0: gp_ref[0], op_ref)

    @pl.when(i >= n_prompt_blocks)
    def _():
        run(lambda r0: gs_ref[pl.ds(r0 % n_seq, COMBINE_CHUNK), :], os_ref)


def _final(xmid, yg, wgt, gate_p, gate_s, final_g, n_prompt, seq_len):
    t_all = xmid.shape[0]
    n_prompt_blocks = n_prompt // ROW_TILE
    per_seq = seq_len // ROW_TILE
    n_seq_p = gate_p.shape[0]
    assert t_all - n_prompt == ROW_TILE
    out_specs = (
        pl.BlockSpec((ROW_TILE, D_MODEL), lambda i: (jnp.minimum(i, n_prompt_blocks - 1), 0)),
        pl.BlockSpec((ROW_TILE, D_MODEL), lambda i: (0, 0)),
    )
    out_shape = (jax.ShapeDtypeStruct((n_prompt, D_MODEL), F32),
                 jax.ShapeDtypeStruct((ROW_TILE, D_MODEL), F32))
    return pl.pallas_call(
        functools.partial(_final_kernel, n_prompt_blocks=n_prompt_blocks),
        grid=(t_all // ROW_TILE,),
        in_specs=[
            pl.BlockSpec((ROW_TILE, D_MODEL), lambda i: (i, 0)),
        ] + [
            pl.BlockSpec((None, ROW_TILE, D_WORDS), lambda i, k=k: (k, i, 0)) for k in range(TOP_K)
        ] + [
            pl.BlockSpec((ROW_TILE, LANES), lambda i: (i, 0)),
            pl.BlockSpec((1, 1, D_MODEL), lambda i: (jnp.minimum(i // per_seq, n_seq_p - 1), 0, 0)),
            _const_spec(gate_s.shape),
            _const_spec((1, D_MODEL)),
        ],
        out_specs=out_specs,
        out_shape=out_shape,
        compiler_params=pltpu.CompilerParams(
            dimension_semantics=("arbitrary",), vmem_limit_bytes=VMEM_LIMIT),
        name="final_combine",
    )(xmid, *([yg] * TOP_K), wgt, gate_p.reshape(n_seq_p, 1, D_MODEL), gate_s,
      final_g.reshape(1, D_MODEL))


def _sc_mesh():
    return plsc.VectorSubcoreMesh(core_axis_name="core", subcore_axis_name="subcore",
                                  num_cores=SC_CORES, num_subcores=SC_SUBCORES)


def _sc_worker_id():
    return lax.axis_index("subcore") * SC_CORES + lax.axis_index("core")


def _sc_dispatch(h2w, pos_c, n_slots):
    _, n_chunks, _, chunk = pos_c.shape

    @functools.partial(
        pl.kernel, mesh=_sc_mesh(),
        out_type=jax.ShapeDtypeStruct((n_slots, D_WORDS), jnp.int32),
        scratch_types=[pltpu.VMEM((n_chunks, TOP_K, chunk), jnp.int32),
                       pltpu.VMEM((chunk, D_WORDS), jnp.int32),
                       pltpu.SemaphoreType.DMA],
        name="moe_dispatch")
    def run(h_hbm, pos_hbm, out_hbm, idx_v, rows_v, sem):
        wid = _sc_worker_id()
        pltpu.sync_copy(pos_hbm.at[wid], idx_v)

        @pl.loop(0, n_chunks)
        def _(c):
            base = pl.multiple_of((wid * n_chunks + c) * chunk, SUBLANES)
            pltpu.sync_copy(h_hbm.at[pl.ds(base, chunk)], rows_v)
            copies = [pltpu.async_copy(rows_v, out_hbm.at[idx_v.at[c, k]], sem) for k in range(TOP_K)]
            for cp in copies:
                cp.wait()

    return run(h2w, pos_c)


def _sc_gather_back(ys, pos_c, t_all):
    _, n_chunks, _, chunk = pos_c.shape

    @functools.partial(
        pl.kernel, mesh=_sc_mesh(),
        out_type=jax.ShapeDtypeStruct((TOP_K, t_all, D_WORDS), jnp.int32),
        scratch_types=[pltpu.VMEM((n_chunks, TOP_K, chunk), jnp.int32),
                       pltpu.VMEM((TOP_K, chunk, D_WORDS), jnp.int32),
                       pltpu.SemaphoreType.DMA,
                       pltpu.SemaphoreType.DMA],
        name="moe_gather_back")
    def run(ys_hbm, pos_hbm, out_hbm, idx_v, rows_v, sem_in, sem_out):
        wid = _sc_worker_id()
        pltpu.sync_copy(pos_hbm.at[wid], idx_v)

        @pl.loop(0, n_chunks)
        def _(c):
            base = pl.multiple_of((wid * n_chunks + c) * chunk, SUBLANES)
            gathers = [pltpu.async_copy(ys_hbm.at[idx_v.at[c, k]], rows_v.at[k], sem_in)
                       for k in range(TOP_K)]
            for cp in gathers:
                cp.wait()
            writes = [pltpu.async_copy(rows_v.at[k], out_hbm.at[k, pl.ds(base, chunk)], sem_out)
                      for k in range(TOP_K)]
            for cp in writes:
                cp.wait()

    return run(ys, pos_c)


def kernel(x_prompt, x_sample, state_conv, c_prompt, c_sample, w_ada, b_ada, w_in, w_conv, b_conv,
           gn_g, gn_b, sgu_ln_g, sgu_ln_b, w_s, b_s, beta_a, beta_b, w_out, w_router, b_router,
           w1, b1, w2, b2, final_g):
    n_bp, seq_len, _ = x_prompt.shape
    n_bs, n_pos, _ = x_sample.shape
    depth = w_ada.shape[0]
    n_prompt = n_bp * seq_len
    n_sample = n_bs * n_pos
    t_all = n_prompt + n_sample
    n_tiles = (t_all * TOP_K) // GEMM_TILE + N_EXPERTS
    n_workers = SC_CORES * SC_SUBCORES
    assert t_all % (n_workers * DISPATCH_CHUNK) == 0 and t_all % (n_workers * GATHER_CHUNK) == 0
    assert n_tiles <= PLAN_LANES and t_all % ROUTER_TILE == 0

    x_p = x_prompt.reshape(n_prompt, D_MODEL)
    x_s = x_sample.transpose(1, 0, 2).reshape(n_sample, D_MODEL)
    x_s_block = 0
    moe = None
    c_all = jnp.concatenate([c_prompt, c_sample], axis=0)
    state_t = state_conv.transpose(0, 2, 1, 3)
    grp = jnp.arange(C_CONV) // (C_CONV // CONV_GROUPS)
    g_mat = ((grp[:, None] == grp[None, :]).astype(F32) / (C_CONV // CONV_GROUPS)).astype(BF16)

    wconv_p = jnp.broadcast_to(
        jnp.pad(w_conv, ((0, 0), (0, CARRY_ROWS - CONV_WIDTH), (0, 0)))[:, :, None, :],
        (depth, CARRY_ROWS, SUBLANES, C_CONV))
    vec = jnp.stack([b_conv, gn_g, gn_b, sgu_ln_g, sgu_ln_b, beta_a, beta_b, jnp.zeros_like(b_conv)],
                    axis=1)
    bs_full = jnp.repeat(b_s.transpose(0, 2, 1), SGU_HEAD_DIM, axis=2)
    wsv = jnp.repeat(w_s[:, :, :n_pos, :n_pos].transpose(0, 2, 3, 1).reshape(depth, n_pos * n_pos, SGU_HEADS),
                     SGU_HEAD_DIM, axis=2)
    bsv = jnp.repeat(b_s[:, :, :n_pos].transpose(0, 2, 1), SGU_HEAD_DIM, axis=2)
    w_router_t = w_router.transpose(0, 2, 1)
    b_router_c = b_router.reshape(depth, N_EXPERTS, 1)

    conv_p, conv_s, v_s = [], [], []
    gate_p = gate_s = None
    for l in range(depth):
        mods = _adaln(c_all, w_ada, b_ada, l)
        mod_p = mods[:, :n_bp].reshape(1, n_bp, 1, N_MOD * D_MODEL)
        mod_s = mods[:, n_bp:]
        xmid, h2, logits, cst, a_new, v_new = _mixer(
            l, x_p, x_s, x_s_block, moe, mod_p, mod_s, gate_p, gate_s, w_in[l:l + 1].astype(BF16),
            w_out[l:l + 1].astype(BF16), wconv_p, vec,
            g_mat, w_s, bs_full, wsv, bsv, w_router_t, b_router_c, state_t, n_bp, seq_len)
        gate_p = mod_p[..., 5 * D_MODEL:]
        gate_s = mod_s[..., 5 * D_MODEL:]
        conv_p.append(cst)
        conv_s.append(a_new)
        v_s.append(v_new)

        wgt, pos, plan = _router(logits)
        def per_worker(chunk):
            return pos.reshape(TOP_K, n_workers, -1, chunk).transpose(1, 2, 0, 3)
        xs = _sc_dispatch(h2, per_worker(DISPATCH_CHUNK), n_tiles * GEMM_TILE)
        ys = _experts(xs, plan, w1, b1, w2, b2, l)
        yg = _sc_gather_back(ys, per_worker(GATHER_CHUNK), t_all)
        moe = (yg, wgt)
        x_p = x_s = xmid
        x_s_block = n_prompt // n_sample

    y_p, y_s = _final(xmid, *moe, gate_p[0, :, 0], gate_s[0], final_g, n_prompt, seq_len)
    y_prompt = y_p.reshape(n_bp, seq_len, D_MODEL)
    y_sample = y_s.reshape(n_pos, n_bs, D_MODEL).transpose(1, 0, 2)
    new_conv_s = jnp.stack(conv_s, axis=0).transpose(0, 2, 1, 3)
    new_v_s = jnp.stack(v_s, axis=0).reshape(depth, n_pos, n_bs, C_SGU).transpose(0, 2, 1, 3)
    return (y_prompt, y_sample, jnp.stack(conv_p, axis=0), new_conv_s, new_v_s)
```

```python
import functools
import math

import jax
import jax.numpy as jnp
from jax import lax
from jax.experimental import pallas as pl
from jax.experimental.pallas import tpu as pltpu
from jax.experimental.pallas import tpu_sc as plsc

F32 = jnp.float32
BF16 = jnp.bfloat16

D_MODEL = 1024
C_CONV = 512
C_SGU = 512
CONV_GROUPS = 8
CONV_WIDTH = 31
CONV_STATE = CONV_WIDTH - 1
SGU_HEADS = 4
SGU_HEAD_DIM = C_SGU // SGU_HEADS
CHUNK = 128
N_EXPERTS = 32
TOP_K = 4
D_FF = 1024
SWIGLU_LIMIT = 7.0
SWIGLU_ALPHA = 1.702
N_MOD = 6
EPS = 1e-5
LOG2_E = 1.0 / math.log(2.0)

ROW_TILE = 512
ROUTER_TILE = 1536
PREFIX_BLOCK = 256
ADALN_COLS = 1024
ROW_CHUNK = 128
CONV_CHUNK = 32
COMBINE_CHUNK = 64
FINAL_CHUNK = 32
CARRY_ROWS = 32
SUBLANES = 8
LANES = 128
PLAN_ROWS = 8
PLAN_LANES = 256
PLAN_EXPERT, PLAN_VALID, PLAN_FIRST, PLAN_SLOT, PLAN_NEXT = range(5)
GEMM_TILE = 512
FF_BLOCK = 256
TILES_PER_STEP = 2
VMEM_LIMIT = 56 * 1024 * 1024
D_WORDS = D_MODEL // 2
DISPATCH_CHUNK = 88
GATHER_CHUNK = 48
HI_MASK = -65536
SC_CORES = 2
SC_SUBCORES = 16


def _rms(x):
    return x * lax.rsqrt(jnp.mean(x * x, axis=-1, keepdims=True) + EPS)


def _gelu(x):
    return 0.5 * x * (1.0 + lax.erf(x * (1.0 / math.sqrt(2.0))))


def _sigmoid(x, scale=1.0):
    return 1.0 / (1.0 + jnp.exp2(x * (-scale * LOG2_E)))


def _split_bf16(x):
    hi = x.astype(BF16)
    lo = (x - hi.astype(F32)).astype(BF16)
    return hi, lo


def _pack_bf16_pair(x):
    bits = lax.bitcast_convert_type(x.astype(BF16).astype(F32), jnp.int32)
    return lax.shift_right_logical(bits[:, :D_WORDS], 16) | (bits[:, D_WORDS:] & HI_MASK)


def _unpack_bf16_pair(w):
    left = lax.bitcast_convert_type(lax.shift_left(w, 16), F32)
    right = lax.bitcast_convert_type(w & HI_MASK, F32)
    return left, right


def _dot(a, b):
    return jnp.dot(a, b, preferred_element_type=F32)


def _dot_nt(a, b):
    return lax.dot_general(a, b, (((1,), (1,)), ((), ())), preferred_element_type=F32)


def _conv_tap(wconv_ref, k):
    return jnp.concatenate([wconv_ref[k]] * (CONV_CHUNK // SUBLANES), axis=0)


def _static_loop(n_rows, body, chunk=None):
    chunk = ROW_CHUNK if chunk is None else chunk
    for r0 in range(0, n_rows, chunk):
        body(r0)


def _adaln_kernel(c_ref, w_ref, b_ref, o_ref):
    c = c_ref[...]
    s_hi, s_lo = _split_bf16(c * _sigmoid(c))
    w_hi, w_lo = _split_bf16(w_ref[0])
    acc = _dot(s_hi, w_hi) + _dot(s_hi, w_lo) + _dot(s_lo, w_hi)
    o_ref[0] = acc + b_ref[0]


def _adaln(c_all, w_ada, b_ada, layer):
    depth, _, n_out = w_ada.shape
    n_rows = c_all.shape[0]
    tn = ADALN_COLS
    return pl.pallas_call(
        _adaln_kernel,
        grid=(n_out // tn,),
        in_specs=[
            pl.BlockSpec((n_rows, D_MODEL), lambda j: (0, 0)),
            pl.BlockSpec((1, D_MODEL, tn), lambda j: (layer, 0, j)),
            pl.BlockSpec((1, 1, tn), lambda j: (layer, 0, j)),
        ],
        out_specs=pl.BlockSpec((1, n_rows, tn), lambda j: (0, 0, j)),
        out_shape=jax.ShapeDtypeStruct((1, n_rows, n_out), F32),
        compiler_params=pltpu.CompilerParams(
            dimension_semantics=("arbitrary",), vmem_limit_bytes=VMEM_LIMIT),
        name="adaln",
    )(c_all, w_ada, b_ada.reshape(depth, 1, n_out))


def _phase_in_norm(x_ref, h_scr, mod, n_rows):
    def body(r0):
        r = pl.ds(r0, ROW_CHUNK)
        h = _rms(x_ref[r, :]) * (1.0 + mod(1, r0)) + mod(0, r0)
        h_scr[r, :] = h.astype(BF16)
    _static_loop(n_rows, body)


def _phase_glu(z_scr, a_dst, a_off, n_rows):
    def body(r0):
        r = pl.ds(r0, ROW_CHUNK)
        a_dst[pl.ds(r0 + a_off, ROW_CHUNK), :] = z_scr[r, 0:C_CONV] * _sigmoid(z_scr[r, C_CONV:2 * C_CONV])
    _static_loop(n_rows, body)


def _phase_uv(z_scr, vec_ref, u_scr, v_scr, vout_ref, n_rows):
    ln_g = vec_ref[3:4, :]
    ln_b = vec_ref[4:5, :]

    def body(r0):
        r = pl.ds(r0, ROW_CHUNK)
        u_scr[r, :] = _gelu(z_scr[r, 2 * C_CONV:2 * C_CONV + C_SGU])
        gv = _gelu(z_scr[r, 2 * C_CONV + C_SGU:2 * C_CONV + 2 * C_SGU])
        mu = jnp.mean(gv, axis=-1, keepdims=True)
        dv = gv - mu
        var = jnp.mean(dv * dv, axis=-1, keepdims=True)
        v = dv * lax.rsqrt(var + EPS) * ln_g + ln_b
        if vout_ref is not None:
            vout_ref[r, :] = v
        v_scr[r, :] = v.astype(v_scr.dtype)
    _static_loop(n_rows, body)


def _phase_group_norm(conv_scr, convb_scr, stat_scr, g_ref, vec_ref, y_scr, n_rows):
    stat_scr[...] = _dot(convb_scr[...], g_ref[...])

    def center(r0):
        r = pl.ds(r0, ROW_CHUNK)
        d = conv_scr[r, :] - stat_scr[r, :]
        conv_scr[r, :] = d
        convb_scr[r, :] = (d * d).astype(BF16)
    _static_loop(n_rows, center)
    stat_scr[...] = _dot(convb_scr[...], g_ref[...])
    gn_g = vec_ref[1:2, :]
    gn_b = vec_ref[2:3, :]
    beta_a = vec_ref[5:6, :]

    def finish(r0):
        r = pl.ds(r0, ROW_CHUNK)
        gn = conv_scr[r, :] * lax.rsqrt(stat_scr[r, :] + EPS) * gn_g + gn_b
        ya = gn * _sigmoid(gn)
        y_scr[r, 0:C_CONV] = (_rms(ya) * beta_a).astype(BF16)
    _static_loop(n_rows, finish)


def _phase_out(x_ref, y_scr, wout_ref, z_scr, wr_ref, br_ref, mod, h_scr,
               xmid_ref, h2_ref, logit_ref, n_rows):
    z_scr[:, 0:D_MODEL] = (_dot(y_scr[:, 0:C_CONV], wout_ref[0:C_CONV, :])
                           + _dot(y_scr[:, C_CONV:C_CONV + C_SGU], wout_ref[C_CONV:C_CONV + C_SGU, :]))

    def body(r0):
        r = pl.ds(r0, ROW_CHUNK)
        xm = x_ref[r, :] + mod(2, r0) * z_scr[r, 0:D_MODEL]
        xmid_ref[r, :] = xm
        h2 = _rms(xm) * (1.0 + mod(4, r0)) + mod(3, r0)
        hi, lo = _split_bf16(h2)
        h2_ref[r, :] = _pack_bf16_pair(h2)
        y_scr[r, :] = hi
        h_scr[r, :] = lo
    _static_loop(n_rows, body)
    w_hi, w_lo = _split_bf16(wr_ref[...])
    both = _dot_nt(jnp.concatenate([w_hi, w_lo], axis=0), y_scr[...])
    logit_ref[...] = (both[0:N_EXPERTS] + both[N_EXPERTS:2 * N_EXPERTS]
                      + _dot_nt(w_hi, h_scr[...])) + br_ref[...]


def _moe_half(yg_ref, w, r, hi_half):
    moe = jnp.zeros((w.shape[0], D_WORDS), F32)
    for k in range(TOP_K):
        word = yg_ref[k, r, :]
        bits = (word & HI_MASK) if hi_half else lax.shift_left(word, 16)
        moe = moe + w[:, k:k + 1] * lax.bitcast_convert_type(bits, F32)
    return moe


def _phase_moe_residual(xmid_ref, yg_ref, wgt_ref, gate, x_dst, n_rows):
    def body(r0):
        r = pl.ds(r0, COMBINE_CHUNK)
        w = wgt_ref[r, :]
        g = gate(r0)
        for hi_half in (False, True):
            cols = slice(D_WORDS, D_MODEL) if hi_half else slice(0, D_WORDS)
            x_dst[r, cols] = xmid_ref[r, cols] + g[:, cols] * _moe_half(yg_ref, w, r, hi_half)
    _static_loop(n_rows, body, COMBINE_CHUNK)


def _mixer_prompt_kernel(*refs, tiles_per_seq, n_prompt_tiles, fused):
    x_ref, mod_ref = refs[:2]
    refs = refs[2:]
    if fused:
        yg_ref, wgt_ref, gate_ref = refs[:3]
        refs = refs[3:]
        x_scr = refs[-1]
        refs = refs[:-1]
    (win_ref, wout_ref, wconv_ref, vec_ref, g_ref, ws_ref, bs_ref, wr_ref, br_ref,
     smid_ref, sh2_ref, slogit_ref, xmid_ref, h2_ref, logit_ref, cst_ref,
     h_scr, z_scr, aext_scr, conv_scr, convb_scr, stat_scr, u_scr, v_scr, y_scr) = refs
    i = pl.program_id(0)

    @pl.when(i == n_prompt_tiles)
    def _():
        xmid_ref[...] = smid_ref[...]
        h2_ref[...] = sh2_ref[...]
        logit_ref[...] = slogit_ref[...]

    @pl.when(i < n_prompt_tiles)
    def _():
        x_src = x_ref
        if fused:
            _phase_moe_residual(x_ref, yg_ref, wgt_ref, lambda r0: gate_ref[0], x_scr, ROW_TILE)
            x_src = x_scr
        _mixer_prompt_tile(lax.rem(i, tiles_per_seq), tiles_per_seq,
                           x_src, mod_ref, win_ref, wout_ref, wconv_ref, vec_ref, g_ref,
                           ws_ref, bs_ref, wr_ref, br_ref, xmid_ref, h2_ref, logit_ref, cst_ref,
                           h_scr, z_scr, aext_scr, conv_scr, convb_scr, stat_scr, u_scr, v_scr, y_scr)


def _mixer_prompt_tile(j, tiles_per_seq, x_ref, mod_ref, win_ref, wout_ref, wconv_ref, vec_ref, g_ref,
                       ws_ref, bs_ref, wr_ref, br_ref, xmid_ref, h2_ref, logit_ref, cst_ref,
                       h_scr, z_scr, aext_scr, conv_scr, convb_scr, stat_scr, u_scr, v_scr, y_scr):
    n_rows = ROW_TILE

    def mod(idx, r0):
        del r0
        return mod_ref[0, :, idx * D_MODEL:(idx + 1) * D_MODEL]

    a_buf = aext_scr.at[0]

    @pl.when(j == 0)
    def _():
        a_buf[0:CARRY_ROWS, :] = jnp.zeros((CARRY_ROWS, C_CONV), F32)

    _phase_in_norm(x_ref, h_scr, mod, n_rows)
    n_glu = 2 * C_CONV
    z_scr[:, 0:n_glu] = _dot(h_scr[...], win_ref[:, 0:n_glu])
    _phase_glu(z_scr, a_buf, CARRY_ROWS, n_rows)
    z_scr[:, n_glu:] = _dot(h_scr[...], win_ref[:, n_glu:])
    n_shift = n_rows + CARRY_ROWS - SUBLANES
    for s in range(1, SUBLANES):
        aext_scr[s, 0:n_shift, :] = a_buf[s:s + n_shift, :]

    b_conv = vec_ref[0:1, :]
    lead = CARRY_ROWS - CONV_STATE
    for r0 in range(0, n_rows, CONV_CHUNK):
        acc = jnp.zeros((CONV_CHUNK, C_CONV), F32)
        for k in range(CONV_WIDTH):
            s = (lead + k) % SUBLANES
            q0 = r0 + (lead + k - s)
            acc = acc + _conv_tap(wconv_ref, k) * aext_scr[s, q0:q0 + CONV_CHUNK, :]
        conv = acc + b_conv
        conv_scr[r0:r0 + CONV_CHUNK, :] = conv
        convb_scr[r0:r0 + CONV_CHUNK, :] = conv.astype(BF16)
    _phase_uv(z_scr, vec_ref, u_scr, v_scr, None, n_rows)
    _phase_group_norm(conv_scr, convb_scr, stat_scr, g_ref, vec_ref, y_scr, n_rows)

    row_i = lax.broadcasted_iota(jnp.int32, (CHUNK, CHUNK), 0)
    col_i = lax.broadcasted_iota(jnp.int32, (CHUNK, CHUNK), 1)
    tril = (col_i <= row_i).astype(F32)
    for hd in range(SGU_HEADS):
        ws_h = (ws_ref[hd] * tril).astype(BF16)
        cols = slice(hd * SGU_HEAD_DIM, (hd + 1) * SGU_HEAD_DIM)
        s_cols = slice(n_glu + hd * SGU_HEAD_DIM, n_glu + (hd + 1) * SGU_HEAD_DIM)
        for c in range(n_rows // CHUNK):
            rows = slice(c * CHUNK, (c + 1) * CHUNK)
            z_scr[rows, s_cols] = _dot(ws_h, v_scr[rows, cols])
    beta_b = vec_ref[6:7, :]

    def gate_body(r0):
        r = pl.ds(r0, ROW_CHUNK)
        rb = pl.ds(r0 % CHUNK, ROW_CHUNK)
        yb = u_scr[r, :] * (z_scr[r, n_glu:n_glu + C_SGU] + bs_ref[rb, :])
        y_scr[r, C_CONV:C_CONV + C_SGU] = (_rms(yb) * beta_b).astype(BF16)
    _static_loop(n_rows, gate_body)

    _phase_out(x_ref, y_scr, wout_ref, z_scr, wr_ref, br_ref, mod, h_scr,
               xmid_ref, h2_ref, logit_ref, n_rows)

    @pl.when(j == tiles_per_seq - 1)
    def _():
        cst_ref[0] = a_buf[n_rows + lead:n_rows + CARRY_ROWS, :]

    a_buf[0:CARRY_ROWS, :] = a_buf[n_rows:n_rows + CARRY_ROWS, :]


def _mixer_sample_kernel(*refs, fused):
    x_ref, mod_ref = refs[:2]
    refs = refs[2:]
    if fused:
        yg_ref, wgt_ref, gate_ref = refs[:3]
        refs = refs[3:]
        x_scr = refs[-1]
        refs = refs[:-1]
    (win_ref, wout_ref, wconv_ref, vec_ref, g_ref, wsv_ref, bsv_ref, wr_ref, br_ref, state_ref,
     xmid_ref, h2_ref, logit_ref, cnew_ref, vout_ref,
     h_scr, z_scr, conv_scr, convb_scr, stat_scr, u_scr, v_scr, y_scr, anew_ref) = refs
    n_rows = x_ref.shape[0]
    n_seq = state_ref.shape[1]
    n_pos = n_rows // n_seq

    def mod(idx, r0):
        return mod_ref[pl.ds(r0 % n_seq, ROW_CHUNK), idx * D_MODEL:(idx + 1) * D_MODEL]

    if fused:
        def gate(r0):
            return gate_ref[pl.ds(r0 % n_seq, COMBINE_CHUNK), :]
        _phase_moe_residual(x_ref, yg_ref, wgt_ref, gate, x_scr, n_rows)
        x_ref = x_scr

    _phase_in_norm(x_ref, h_scr, mod, n_rows)
    z_scr[...] = _dot(h_scr[...], win_ref[...])
    _phase_glu(z_scr, anew_ref, 0, n_rows)
    _phase_uv(z_scr, vec_ref, u_scr, v_scr, vout_ref, n_rows)

    keep = CONV_STATE - n_pos
    cnew_ref[0:keep] = state_ref[n_pos:CONV_STATE]
    for t in range(n_pos):
        cnew_ref[keep + t] = anew_ref[t * n_seq:(t + 1) * n_seq, :]

    b_conv = vec_ref[0:1, :]

    def conv_body(s0):
        rs = pl.ds(s0, CONV_CHUNK)
        for t in range(n_pos):
            acc = jnp.zeros((CONV_CHUNK, C_CONV), F32)
            for k in range(CONV_WIDTH):
                p = t + k
                if p < CONV_STATE:
                    src = state_ref[p, rs, :]
                else:
                    src = anew_ref[pl.ds(s0 + (p - CONV_STATE) * n_seq, CONV_CHUNK), :]
                acc = acc + _conv_tap(wconv_ref, k) * src
            conv = acc + b_conv
            r = pl.ds(s0 + t * n_seq, CONV_CHUNK)
            conv_scr[r, :] = conv
            convb_scr[r, :] = conv.astype(BF16)
    _static_loop(n_seq, conv_body, CONV_CHUNK)

    _phase_group_norm(conv_scr, convb_scr, stat_scr, g_ref, vec_ref, y_scr, n_rows)

    beta_b = vec_ref[6:7, :]

    def gate_body(s0):
        for t in range(n_pos):
            s = jnp.zeros((CONV_CHUNK, C_SGU), F32)
            for jj in range(t + 1):
                s = s + wsv_ref[t * n_pos + jj:t * n_pos + jj + 1, :] * v_scr[pl.ds(s0 + jj * n_seq, CONV_CHUNK), :]
            r = pl.ds(s0 + t * n_seq, CONV_CHUNK)
            yb = u_scr[r, :] * (s + bsv_ref[t:t + 1, :])
            y_scr[r, C_CONV:C_CONV + C_SGU] = (_rms(yb) * beta_b).astype(BF16)
    _static_loop(n_seq, gate_body, CONV_CHUNK)

    _phase_out(x_ref, y_scr, wout_ref, z_scr, wr_ref, br_ref, mod, h_scr,
               xmid_ref, h2_ref, logit_ref, n_rows)


def _const_spec(shape):
    nd = len(shape)
    return pl.BlockSpec(shape, lambda *_: (0,) * nd, pipeline_mode=pl.Buffered(1))


def _const_out_spec(shape):
    nd = len(shape)
    return pl.BlockSpec(shape, lambda *_: (0,) * nd)


def _layer_spec(shape, layer):
    nd = len(shape)
    return pl.BlockSpec((None,) + tuple(shape), lambda *_: (layer,) + (0,) * nd,
                        pipeline_mode=pl.Buffered(1))


def _mixer(layer, x_p, x_s, x_s_block, moe, mod_p, mod_s, gate_p, gate_s, w_in_b, w_out_b, wconv_p, vec,
           g_mat, w_s, bs_full, wsv, bsv, w_router, b_router, state_t, n_prompt_seq, seq_len):
    fused = moe is not None
    n_prompt = n_prompt_seq * seq_len
    n_seq, n_pos = state_t.shape[2], bsv.shape[1]
    n_sample = n_seq * n_pos
    t_all = n_prompt + n_sample
    nj = seq_len // ROW_TILE
    weight_specs = [
        _layer_spec((D_MODEL, 2 * C_CONV + 2 * C_SGU), 0),
        _layer_spec((D_MODEL, D_MODEL), 0),
        _layer_spec((CARRY_ROWS, SUBLANES, C_CONV), layer),
        _layer_spec((SUBLANES, C_CONV), layer),
        _const_spec((C_CONV, C_CONV)),
    ]
    router_specs = [_layer_spec((N_EXPERTS, D_MODEL), layer), _layer_spec((N_EXPERTS, 1), layer)]
    common_scratch = lambda n: [
        pltpu.VMEM((n, D_MODEL), BF16),
        pltpu.VMEM((n, 2 * C_CONV + 2 * C_SGU), F32),
    ]
    tail_scratch = lambda n, vdt: [
        pltpu.VMEM((n, C_CONV), F32),
        pltpu.VMEM((n, C_CONV), BF16),
        pltpu.VMEM((n, C_CONV), F32),
        pltpu.VMEM((n, C_SGU), F32),
        pltpu.VMEM((n, C_SGU), vdt),
        pltpu.VMEM((n, D_MODEL), BF16),
    ]
    assert n_sample == ROW_TILE
    n_tiles_p = n_prompt // ROW_TILE
    tile_p = lambda i: jnp.minimum(i, n_tiles_p - 1)
    seq_of = lambda i: jnp.minimum(i // nj, n_prompt_seq - 1)
    moe_s_specs, moe_p_specs, moe_s_args, moe_p_args, x_scratch = [], [], [], [], []
    if fused:
        yg, wgt = moe
        moe_s_specs = [pl.BlockSpec((TOP_K, n_sample, D_WORDS), lambda i: (0, x_s_block, 0)),
                       pl.BlockSpec((n_sample, LANES), lambda i: (x_s_block, 0)),
                       _layer_spec((n_seq, D_MODEL), 0)]
        moe_p_specs = [pl.BlockSpec((TOP_K, ROW_TILE, D_WORDS), lambda i: (0, tile_p(i), 0)),
                       pl.BlockSpec((ROW_TILE, LANES), lambda i: (tile_p(i), 0)),
                       pl.BlockSpec((None, 1, 1, D_MODEL), lambda i: (0, seq_of(tile_p(i)), 0, 0))]
        moe_s_args = [yg, wgt, gate_s]
        moe_p_args = [yg, wgt, gate_p]
        x_scratch = [pltpu.VMEM((ROW_TILE, D_MODEL), F32)]
    smid, sh2, slogits, a_new, v_new = pl.pallas_call(
        functools.partial(_mixer_sample_kernel, fused=fused),
        grid=(1,),
        in_specs=[
            pl.BlockSpec((n_sample, D_MODEL), lambda i: (x_s_block, 0)),
            _layer_spec((n_seq, N_MOD * D_MODEL), 0),
            *moe_s_specs,
            *weight_specs,
            _layer_spec((n_pos * n_pos, C_SGU), layer),
            _layer_spec((n_pos, C_SGU), layer),
            *router_specs,
            _layer_spec((CONV_STATE, n_seq, C_CONV), layer),
        ],
        out_specs=(
            _const_out_spec((n_sample, D_MODEL)),
            _const_out_spec((n_sample, D_WORDS)),
            _const_out_spec((N_EXPERTS, n_sample)),
            _const_out_spec((CONV_STATE, n_seq, C_CONV)),
            _const_out_spec((n_sample, C_SGU)),
        ),
        out_shape=(
            jax.ShapeDtypeStruct((n_sample, D_MODEL), F32),
            jax.ShapeDtypeStruct((n_sample, D_WORDS), jnp.int32),
            jax.ShapeDtypeStruct((N_EXPERTS, n_sample), F32),
            jax.ShapeDtypeStruct((CONV_STATE, n_seq, C_CONV), F32),
            jax.ShapeDtypeStruct((n_sample, C_SGU), F32),
        ),
        scratch_shapes=common_scratch(n_sample) + tail_scratch(n_sample, F32)
        + [pltpu.VMEM((n_sample, C_CONV), F32)] + x_scratch,
        compiler_params=pltpu.CompilerParams(
            dimension_semantics=("arbitrary",), vmem_limit_bytes=VMEM_LIMIT),
        name="mixer_sample",
    )(x_s, mod_s, *moe_s_args, w_in_b, w_out_b, wconv_p, vec, g_mat, wsv, bsv, w_router, b_router,
      state_t)

    xmid, h2, logits, cst = pl.pallas_call(
        functools.partial(_mixer_prompt_kernel, tiles_per_seq=nj, n_prompt_tiles=n_tiles_p,
                          fused=fused),
        grid=(n_tiles_p + 1,),
        in_specs=[
            pl.BlockSpec((ROW_TILE, D_MODEL), lambda i: (tile_p(i), 0)),
            pl.BlockSpec((None, 1, 1, N_MOD * D_MODEL), lambda i: (0, seq_of(i), 0, 0)),
            *moe_p_specs,
            *weight_specs,
            _layer_spec((SGU_HEADS, CHUNK, CHUNK), layer),
            _layer_spec((CHUNK, C_SGU), layer),
            *router_specs,
            _const_spec((n_sample, D_MODEL)),
            _const_spec((n_sample, D_WORDS)),
            _const_spec((N_EXPERTS, n_sample)),
        ],
        out_specs=(
            pl.BlockSpec((ROW_TILE, D_MODEL), lambda i: (i, 0)),
            pl.BlockSpec((ROW_TILE, D_WORDS), lambda i: (i, 0)),
            pl.BlockSpec((N_EXPERTS, ROW_TILE), lambda i: (0, i)),
            pl.BlockSpec((1, CONV_STATE, C_CONV), lambda i: (seq_of(i), 0, 0)),
        ),
        out_shape=(
            jax.ShapeDtypeStruct((t_all, D_MODEL), F32),
            jax.ShapeDtypeStruct((t_all, D_WORDS), jnp.int32),
            jax.ShapeDtypeStruct((N_EXPERTS, t_all), F32),
            jax.ShapeDtypeStruct((n_prompt_seq, CONV_STATE, C_CONV), F32),
        ),
        scratch_shapes=common_scratch(ROW_TILE)
        + [pltpu.VMEM((SUBLANES, CARRY_ROWS + ROW_TILE, C_CONV), F32)]
        + tail_scratch(ROW_TILE, BF16) + x_scratch,
        compiler_params=pltpu.CompilerParams(
            dimension_semantics=("arbitrary",), vmem_limit_bytes=VMEM_LIMIT),
        name="mixer_prompt",
    )(x_p, mod_p, *moe_p_args, w_in_b, w_out_b, wconv_p, vec, g_mat, w_s, bs_full, w_router, b_router,
      smid, sh2, slogits)
    return xmid, h2, logits, cst, a_new, v_new


def _tile_plan(cnt):
    e_sub = lax.broadcasted_iota(jnp.int32, (N_EXPERTS, N_EXPERTS), 0)
    e_lane = lax.broadcasted_iota(jnp.int32, (N_EXPERTS, N_EXPERTS), 1)
    tiles = jnp.floor((cnt + (GEMM_TILE - 1.0)) * (1.0 / GEMM_TILE))
    active = jnp.where(cnt > 0.0, 1.0, 0.0)

    def over_experts(tri, col):
        return _dot(tri.astype(BF16), jnp.broadcast_to(col, (N_EXPERTS, LANES)).astype(BF16))[:, 0:1]
    tile_end = over_experts(jnp.where(e_lane <= e_sub, 1.0, 0.0), tiles)
    tile_start = tile_end - tiles
    ordinal = over_experts(jnp.where(e_lane < e_sub, 1.0, 0.0), active)

    g = lax.broadcasted_iota(jnp.int32, (1, PLAN_LANES), 1).astype(F32)
    e_id = lax.broadcasted_iota(jnp.int32, (N_EXPERTS, PLAN_LANES), 0).astype(F32)
    done = jnp.sum(jnp.where(tile_end <= g, 1.0, 0.0), axis=0, keepdims=True)
    live = done < float(N_EXPERTS)
    last_e = jnp.max(jnp.where(cnt > 0.0, e_id[:, 0:1], 0.0), axis=0, keepdims=True)
    te = jnp.where(live, done, last_e)
    pick = e_id == te

    def at_tile(col):
        return jnp.sum(jnp.where(pick, col, 0.0), axis=0, keepdims=True)
    start_g = at_tile(tile_start)
    ord_g = at_tile(ordinal)
    valid = jnp.clip(at_tile(cnt) - (g - start_g) * GEMM_TILE, 0.0, float(GEMM_TILE))
    valid = jnp.where(live, valid, 0.0)
    first = jnp.where(live, jnp.where(g == start_g, 1.0, 0.0), 0.0)
    slot = ord_g - 2.0 * jnp.floor(ord_g * 0.5)
    later = jnp.where(e_id > te, jnp.where(cnt > 0.0, e_id, float(N_EXPERTS)), float(N_EXPERTS))
    nxt = jnp.min(later, axis=0, keepdims=True)
    nxt = jnp.where(nxt >= float(N_EXPERTS), -1.0, nxt)

    row = lax.broadcasted_iota(jnp.int32, (PLAN_ROWS, PLAN_LANES), 0)
    plan = jnp.zeros((PLAN_ROWS, PLAN_LANES), F32)
    for r, v in ((PLAN_EXPERT, te), (PLAN_VALID, valid), (PLAN_FIRST, first), (PLAN_SLOT, slot),
                 (PLAN_NEXT, nxt)):
        plan = jnp.where(row == r, v, plan)
    return plan.astype(jnp.int32), tile_start * GEMM_TILE


def _router_kernel(lg_ref, wgt_ref, pos_ref, plan_ref, cnt_scr, run_scr, start_scr, earlier_scr):
    ph = pl.program_id(0)
    i = pl.program_id(1)
    n = lg_ref.shape[1]

    @pl.when(jnp.logical_and(ph == 0, i == 0))
    def _():
        cnt_scr[...] = jnp.zeros_like(cnt_scr)

    lg = lg_ref[...]
    sub = lax.broadcasted_iota(jnp.int32, lg.shape, 0).astype(F32)
    vals, sels = [], []
    for _ in range(TOP_K):
        m = jnp.max(lg, axis=0, keepdims=True)
        idx = jnp.min(jnp.where(lg == m, sub, float(N_EXPERTS)), axis=0, keepdims=True)
        sel = sub == idx
        vals.append(m)
        sels.append(sel)
        lg = jnp.where(sel, -jnp.inf, lg)
    onehot = jnp.zeros(lg.shape, F32)
    for sel in sels:
        onehot = onehot + jnp.where(sel, 1.0, 0.0)
    tile_cnt = jnp.sum(onehot, axis=1, keepdims=True)

    @pl.when(ph == 0)
    def _():
        cnt_scr[...] = cnt_scr[...] + tile_cnt

    @pl.when(jnp.logical_and(ph == 1, i == 0))
    def _():
        plan, row_start = _tile_plan(cnt_scr[...])
        plan_ref[...] = plan
        start_scr[...] = row_start
        run_scr[...] = jnp.zeros_like(run_scr)
        t_row = lax.broadcasted_iota(jnp.int32, (PREFIX_BLOCK, PREFIX_BLOCK), 0)
        t_col = lax.broadcasted_iota(jnp.int32, (PREFIX_BLOCK, PREFIX_BLOCK), 1)
        earlier_scr[...] = jnp.where(t_row < t_col, 1.0, 0.0).astype(BF16)

    @pl.when(ph == 1)
    def _():
        exps = [jnp.exp(v - vals[0]) for v in vals]
        inv = 1.0 / (exps[0] + exps[1] + exps[2] + exps[3])
        w_row = lax.broadcasted_iota(jnp.int32, (LANES, n), 0)
        w_all = jnp.zeros((LANES, n), F32)
        for k in range(TOP_K):
            w_all = jnp.where(w_row == k, exps[k] * inv, w_all)
        wgt_ref[...] = w_all.T
        base = run_scr[...] + start_scr[...]
        onehot_b = onehot.astype(BF16)
        for b0 in range(0, n, PREFIX_BLOCK):
            blk = slice(b0, b0 + PREFIX_BLOCK)
            slot0 = _dot(onehot_b[:, blk], earlier_scr[...]) + base
            for k in range(TOP_K):
                pos_k = jnp.sum(jnp.where(sels[k][:, blk], slot0, 0.0), axis=0, keepdims=True)
                pos_ref[k:k + 1, blk] = pos_k.astype(jnp.int32)
            base = base + jnp.sum(onehot[:, blk], axis=1, keepdims=True)
        run_scr[...] = run_scr[...] + tile_cnt


def _router(logits):
    t_all = logits.shape[1]
    col = pltpu.VMEM((N_EXPERTS, 1), F32)
    return pl.pallas_call(
        _router_kernel,
        grid=(2, t_all // ROUTER_TILE),
        in_specs=[pl.BlockSpec((N_EXPERTS, ROUTER_TILE), lambda ph, i: (0, i))],
        out_specs=(
            pl.BlockSpec((ROUTER_TILE, LANES), lambda ph, i: (i * ph, 0)),
            pl.BlockSpec((TOP_K, ROUTER_TILE), lambda ph, i: (0, i * ph)),
            _const_out_spec((PLAN_ROWS, PLAN_LANES)),
        ),
        out_shape=(
            jax.ShapeDtypeStruct((t_all, LANES), F32),
            jax.ShapeDtypeStruct((TOP_K, t_all), jnp.int32),
            jax.ShapeDtypeStruct((PLAN_ROWS, PLAN_LANES), jnp.int32),
        ),
        scratch_shapes=[col, col, col, pltpu.VMEM((PREFIX_BLOCK, PREFIX_BLOCK), BF16)],
        compiler_params=pltpu.CompilerParams(
            dimension_semantics=("arbitrary", "arbitrary"), vmem_limit_bytes=VMEM_LIMIT),
        name="router",
    )(logits)


def _expert_kernel(plan_ref, x_ref, w1_hbm, b1_ref, w2_hbm, b2_ref, o_ref,
                   w1f_scr, w2f_scr, w1b_scr, w2b_scr, sem, *, layer):
    def weight_copies(expert, slot):
        return (pltpu.make_async_copy(w1_hbm.at[layer, expert], w1f_scr.at[slot], sem.at[0, slot]),
                pltpu.make_async_copy(w2_hbm.at[layer, expert], w2f_scr.at[slot], sem.at[1, slot]))

    for sub in range(TILES_PER_STEP):
        _expert_tile(pl.program_id(0) * TILES_PER_STEP + sub, sub * GEMM_TILE, plan_ref, x_ref,
                     b1_ref, b2_ref, o_ref, w1f_scr, w2f_scr, w1b_scr, w2b_scr, weight_copies)


def _expert_tile(g, row0, plan_ref, x_ref, b1_ref, b2_ref, o_ref, w1f_scr, w2f_scr, w1b_scr, w2b_scr,
                 weight_copies):
    nv = plan_ref[PLAN_VALID, g]
    expert = plan_ref[PLAN_EXPERT, g]
    slot = plan_ref[PLAN_SLOT, g]
    first = plan_ref[PLAN_FIRST, g] == 1

    @pl.when(first)
    def _():
        nxt = plan_ref[PLAN_NEXT, g]

        @pl.when(g == 0)
        def _():
            for cp in weight_copies(expert, slot):
                cp.start()
        for cp in weight_copies(expert, slot):
            cp.wait()

        @pl.when(nxt >= 0)
        def _():
            for cp in weight_copies(nxt, 1 - slot):
                cp.start()

    def ffn(m, fresh):
        x_left, x_right = _unpack_bf16_pair(x_ref[row0:row0 + m, :])
        x = jnp.concatenate([x_left.astype(BF16), x_right.astype(BF16)], axis=1)

        def hidden(col0):
            cols = slice(col0, col0 + FF_BLOCK)
            if fresh:
                w1b_scr[:, cols] = w1f_scr[slot, :, cols].astype(BF16)
            return _dot(x, w1b_scr[:, cols]) + b1_ref[expert, :, cols]

        acts = []
        for jb in range(D_FF // FF_BLOCK):
            gg = jnp.minimum(hidden(jb * FF_BLOCK), SWIGLU_LIMIT)
            up = jnp.clip(hidden(D_FF + jb * FF_BLOCK), -SWIGLU_LIMIT, SWIGLU_LIMIT)
            acts.append((gg * _sigmoid(gg, SWIGLU_ALPHA) * (up + 1.0)).astype(BF16))
        if fresh:
            w2b_scr[...] = w2f_scr[slot].astype(BF16)
        y = _dot(jnp.concatenate(acts, axis=1), w2b_scr[...]) + b2_ref[expert]
        row = lax.broadcasted_iota(jnp.int32, y.shape, 0)
        o_ref[row0:row0 + m, :] = _pack_bf16_pair(jnp.where(row < nv, y, 0.0))
        if m < GEMM_TILE:
            o_ref[row0 + m:row0 + GEMM_TILE, :] = jnp.zeros((GEMM_TILE - m, D_WORDS), jnp.int32)

    full = nv > GEMM_TILE // 2
    half = jnp.logical_and(nv > 0, nv <= GEMM_TILE // 2)
    for rows, cond in ((GEMM_TILE, full), (GEMM_TILE // 2, half)):
        for fresh in (True, False):
            pl.when(jnp.logical_and(cond, first == fresh))(functools.partial(ffn, rows, fresh))

    @pl.when(nv == 0)
    def _():
        o_ref[row0:row0 + GEMM_TILE, :] = jnp.zeros((GEMM_TILE, D_WORDS), jnp.int32)


def _experts(xs, plan, w1, b1, w2, b2, layer):
    n_tiles = xs.shape[0] // GEMM_TILE
    depth = w1.shape[0]
    assert n_tiles % TILES_PER_STEP == 0
    step_rows = TILES_PER_STEP * GEMM_TILE
    bias_map = lambda s, plan: (layer, 0, 0, 0)
    grid_spec = pltpu.PrefetchScalarGridSpec(
        num_scalar_prefetch=1,
        grid=(n_tiles // TILES_PER_STEP,),
        in_specs=[
            pl.BlockSpec((step_rows, D_WORDS), lambda s, plan: (s, 0)),
            pl.BlockSpec(memory_space=pl.ANY),
            pl.BlockSpec((None, N_EXPERTS, 1, 2 * D_FF), bias_map, pipeline_mode=pl.Buffered(1)),
            pl.BlockSpec(memory_space=pl.ANY),
            pl.BlockSpec((None, N_EXPERTS, 1, D_MODEL), bias_map, pipeline_mode=pl.Buffered(1)),
        ],
        out_specs=pl.BlockSpec((step_rows, D_WORDS), lambda s, plan: (s, 0)),
        scratch_shapes=[
            pltpu.VMEM((2, D_MODEL, 2 * D_FF), F32),
            pltpu.VMEM((2, D_FF, D_MODEL), F32),
            pltpu.VMEM((D_MODEL, 2 * D_FF), BF16),
            pltpu.VMEM((D_FF, D_MODEL), BF16),
            pltpu.SemaphoreType.DMA((2, 2)),
        ],
    )
    return pl.pallas_call(
        functools.partial(_expert_kernel, layer=layer),
        grid_spec=grid_spec,
        out_shape=jax.ShapeDtypeStruct((n_tiles * GEMM_TILE, D_WORDS), jnp.int32),
        compiler_params=pltpu.CompilerParams(
            dimension_semantics=("arbitrary",), vmem_limit_bytes=VMEM_LIMIT),
        name="experts",
    )(plan, xs, w1, b1.reshape(depth, N_EXPERTS, 1, 2 * D_FF), w2,
      b2.reshape(depth, N_EXPERTS, 1, D_MODEL))


def _final_kernel(x_ref, yg_ref, wgt_ref, gp_ref, gs_ref, fg_ref, op_ref, os_ref, *, n_prompt_blocks):
    i = pl.program_id(0)
    n_seq = gs_ref.shape[0]

    def run(gate, o_ref):
        def body(r0):
            r = pl.ds(r0, FINAL_CHUNK)
            w = wgt_ref[r, :]
            g = gate(r0)
            xl = x_ref[r, 0:D_WORDS] + g[:, 0:D_WORDS] * _moe_half(yg_ref, w, r, False)
            xr = x_ref[r, D_WORDS:D_MODEL] + g[:, D_WORDS:D_MODEL] * _moe_half(yg_ref, w, r, True)
            ms = (jnp.sum(xl * xl, axis=-1, keepdims=True)
                  + jnp.sum(xr * xr, axis=-1, keepdims=True)) * (1.0 / D_MODEL)
            scale = lax.rsqrt(ms + EPS)
            o_ref[r, 0:D_WORDS] = xl * scale * fg_ref[:, 0:D_WORDS]
            o_ref[r, D_WORDS:D_MODEL] = xr * scale * fg_ref[:, D_WORDS:D_MODEL]
        _static_loop(x_ref.shape[0], body, FINAL_CHUNK)

    @pl.when(i < n_prompt_blocks)
    def _():
        run(lambda r0: gp_ref[0], op_ref)

    @pl.when(i >= n_prompt_blocks)
    def _():
        run(lambda r0: gs_ref[pl.ds(r0 % n_seq, FINAL_CHUNK), :], os_ref)


def _final(xmid, yg, wgt, gate_p, gate_s, final_g, n_prompt, seq_len):
    t_all = xmid.shape[0]
    n_prompt_blocks = n_prompt // ROW_TILE
    per_seq = seq_len // ROW_TILE
    n_seq_p = gate_p.shape[0]
    assert t_all - n_prompt == ROW_TILE
    out_specs = (
        pl.BlockSpec((ROW_TILE, D_MODEL), lambda i: (jnp.minimum(i, n_prompt_blocks - 1), 0)),
        pl.BlockSpec((ROW_TILE, D_MODEL), lambda i: (0, 0)),
    )
    out_shape = (jax.ShapeDtypeStruct((n_prompt, D_MODEL), F32),
                 jax.ShapeDtypeStruct((ROW_TILE, D_MODEL), F32))
    return pl.pallas_call(
        functools.partial(_final_kernel, n_prompt_blocks=n_prompt_blocks),
        grid=(t_all // ROW_TILE,),
        in_specs=[
            pl.BlockSpec((ROW_TILE, D_MODEL), lambda i: (i, 0)),
            pl.BlockSpec((TOP_K, ROW_TILE, D_WORDS), lambda i: (0, i, 0)),
            pl.BlockSpec((ROW_TILE, LANES), lambda i: (i, 0)),
            pl.BlockSpec((1, 1, D_MODEL), lambda i: (jnp.minimum(i // per_seq, n_seq_p - 1), 0, 0)),
            _const_spec(gate_s.shape),
            _const_spec((1, D_MODEL)),
        ],
        out_specs=out_specs,
        out_shape=out_shape,
        compiler_params=pltpu.CompilerParams(
            dimension_semantics=("arbitrary",), vmem_limit_bytes=VMEM_LIMIT),
        name="final_combine",
    )(xmid, yg, wgt, gate_p.reshape(n_seq_p, 1, D_MODEL), gate_s, final_g.reshape(1, D_MODEL))


def _sc_mesh():
    return plsc.VectorSubcoreMesh(core_axis_name="core", subcore_axis_name="subcore",
                                  num_cores=SC_CORES, num_subcores=SC_SUBCORES)


def _sc_worker_id():
    return lax.axis_index("subcore") * SC_CORES + lax.axis_index("core")


def _sc_dispatch(h2w, pos_c, n_slots):
    _, n_chunks, _, chunk = pos_c.shape

    @functools.partial(
        pl.kernel, mesh=_sc_mesh(),
        out_type=jax.ShapeDtypeStruct((n_slots, D_WORDS), jnp.int32),
        scratch_types=[pltpu.VMEM((n_chunks, TOP_K, chunk), jnp.int32),
                       pltpu.VMEM((chunk, D_WORDS), jnp.int32),
                       pltpu.SemaphoreType.DMA],
        name="moe_dispatch")
    def run(h_hbm, pos_hbm, out_hbm, idx_v, rows_v, sem):
        wid = _sc_worker_id()
        pltpu.sync_copy(pos_hbm.at[wid], idx_v)

        @pl.loop(0, n_chunks)
        def _(c):
            base = pl.multiple_of((wid * n_chunks + c) * chunk, SUBLANES)
            pltpu.sync_copy(h_hbm.at[pl.ds(base, chunk)], rows_v)
            copies = [pltpu.async_copy(rows_v, out_hbm.at[idx_v.at[c, k]], sem) for k in range(TOP_K)]
            for cp in copies:
                cp.wait()

    return run(h2w, pos_c)


def _sc_gather_back(ys, pos_c, t_all):
    _, n_chunks, _, chunk = pos_c.shape

    @functools.partial(
        pl.kernel, mesh=_sc_mesh(),
        out_type=jax.ShapeDtypeStruct((TOP_K, t_all, D_WORDS), jnp.int32),
        scratch_types=[pltpu.VMEM((n_chunks, TOP_K, chunk), jnp.int32),
                       pltpu.VMEM((TOP_K, chunk, D_WORDS), jnp.int32),
                       pltpu.SemaphoreType.DMA,
                       pltpu.SemaphoreType.DMA],
        name="moe_gather_back")
    def run(ys_hbm, pos_hbm, out_hbm, idx_v, rows_v, sem_in, sem_out):
        wid = _sc_worker_id()
        pltpu.sync_copy(pos_hbm.at[wid], idx_v)

        @pl.loop(0, n_chunks)
        def _(c):
            base = pl.multiple_of((wid * n_chunks + c) * chunk, SUBLANES)
            gathers = [pltpu.async_copy(ys_hbm.at[idx_v.at[c, k]], rows_v.at[k], sem_in)
                       for k in range(TOP_K)]
            for cp in gathers:
                cp.wait()
            writes = [pltpu.async_copy(rows_v.at[k], out_hbm.at[k, pl.ds(base, chunk)], sem_out)
                      for k in range(TOP_K)]
            for cp in writes:
                cp.wait()

    return run(ys, pos_c)


def kernel(x_prompt, x_sample, state_conv, c_prompt, c_sample, w_ada, b_ada, w_in, w_conv, b_conv,
           gn_g, gn_b, sgu_ln_g, sgu_ln_b, w_s, b_s, beta_a, beta_b, w_out, w_router, b_router,
           w1, b1, w2, b2, final_g):
    n_bp, seq_len, _ = x_prompt.shape
    n_bs, n_pos, _ = x_sample.shape
    depth = w_ada.shape[0]
    n_prompt = n_bp * seq_len
    n_sample = n_bs * n_pos
    t_all = n_prompt + n_sample
    n_tiles = (t_all * TOP_K) // GEMM_TILE + N_EXPERTS
    n_workers = SC_CORES * SC_SUBCORES
    assert t_all % (n_workers * DISPATCH_CHUNK) == 0 and t_all % (n_workers * GATHER_CHUNK) == 0
    assert n_tiles <= PLAN_LANES and t_all % ROUTER_TILE == 0

    x_p = x_prompt.reshape(n_prompt, D_MODEL)
    x_s = x_sample.transpose(1, 0, 2).reshape(n_sample, D_MODEL)
    x_s_block = 0
    moe = None
    c_all = jnp.concatenate([c_prompt, c_sample], axis=0)
    state_t = state_conv.transpose(0, 2, 1, 3)
    grp = jnp.arange(C_CONV) // (C_CONV // CONV_GROUPS)
    g_mat = ((grp[:, None] == grp[None, :]).astype(F32) / (C_CONV // CONV_GROUPS)).astype(BF16)

    wconv_p = jnp.broadcast_to(
        jnp.pad(w_conv, ((0, 0), (0, CARRY_ROWS - CONV_WIDTH), (0, 0)))[:, :, None, :],
        (depth, CARRY_ROWS, SUBLANES, C_CONV))
    vec = jnp.stack([b_conv, gn_g, gn_b, sgu_ln_g, sgu_ln_b, beta_a, beta_b, jnp.zeros_like(b_conv)],
                    axis=1)
    bs_full = jnp.repeat(b_s.transpose(0, 2, 1), SGU_HEAD_DIM, axis=2)
    wsv = jnp.repeat(w_s[:, :, :n_pos, :n_pos].transpose(0, 2, 3, 1).reshape(depth, n_pos * n_pos, SGU_HEADS),
                     SGU_HEAD_DIM, axis=2)
    bsv = jnp.repeat(b_s[:, :, :n_pos].transpose(0, 2, 1), SGU_HEAD_DIM, axis=2)
    w_router_t = w_router.transpose(0, 2, 1)
    b_router_c = b_router.reshape(depth, N_EXPERTS, 1)

    conv_p, conv_s, v_s = [], [], []
    gate_p = gate_s = None
    for l in range(depth):
        mods = _adaln(c_all, w_ada, b_ada, l)
        mod_p = mods[:, :n_bp].reshape(1, n_bp, 1, N_MOD * D_MODEL)
        mod_s = mods[:, n_bp:]
        xmid, h2, logits, cst, a_new, v_new = _mixer(
            l, x_p, x_s, x_s_block, moe, mod_p, mod_s, gate_p, gate_s, w_in[l:l + 1].astype(BF16),
            w_out[l:l + 1].astype(BF16), wconv_p, vec,
            g_mat, w_s, bs_full, wsv, bsv, w_router_t, b_router_c, state_t, n_bp, seq_len)
        gate_p = mod_p[..., 5 * D_MODEL:]
        gate_s = mod_s[..., 5 * D_MODEL:]
        conv_p.append(cst)
        conv_s.append(a_new)
        v_s.append(v_new)

        wgt, pos, plan = _router(logits)
        def per_worker(chunk):
            return pos.reshape(TOP_K, n_workers, -1, chunk).transpose(1, 2, 0, 3)
        xs = _sc_dispatch(h2, per_worker(DISPATCH_CHUNK), n_tiles * GEMM_TILE)
        ys = _experts(xs, plan, w1, b1, w2, b2, l)
        yg = _sc_gather_back(ys, per_worker(GATHER_CHUNK), t_all)
        moe = (yg, wgt)
        x_p = x_s = xmid
        x_s_block = n_prompt // n_sample

    y_p, y_s = _final(xmid, *moe, gate_p[0, :, 0], gate_s[0], final_g, n_prompt, seq_len)
    y_prompt = y_p.reshape(n_bp, seq_len, D_MODEL)
    y_sample = y_s.reshape(n_pos, n_bs, D_MODEL).transpose(1, 0, 2)
    new_conv_s = jnp.stack(conv_s, axis=0).transpose(0, 2, 1, 3)
    new_v_s = jnp.stack(v_s, axis=0).reshape(depth, n_pos, n_bs, C_SGU).transpose(0, 2, 1, 3)
    return (y_prompt, y_sample, jnp.stack(conv_p, axis=0), new_conv_s, new_v_s)
```
